```python
import jax, jax.numpy as jnp
from jax import lax
import numpy as np

D_MODEL = 1024
BATCH = 8
SEQ = 4096
DEPTH = 2

N_META = 16
D_CONV = 512
CONV_A_WIDTH = 3
DN_HEADS = 4
DN_HEAD_DIM = 128
DN_DIM = DN_HEADS * DN_HEAD_DIM
DN_CONV_WIDTH = 4
DN_CHUNK = 64
D_MIX = D_CONV + DN_DIM
IN_DIM = 3 * D_CONV + 4 * DN_DIM + 2 * DN_HEADS
SWA_HEADS = 16
SWA_KV_HEADS = 4
SWA_HEAD_DIM = 64
SWA_WINDOW = 128
SWA_BLOCK = 128
D_FF = 2816
FFN_CONV_WIDTH = 3
EPS = 1e-6
N_EVEN = (DEPTH + 1) // 2
N_ODD = DEPTH // 2

kernel_name = "hybrid_shortconv_gdn_swa_convffn_meta"


def rms_norm(x, w):
    xf = x.astype(jnp.float32)
    y = xf * lax.rsqrt(jnp.mean(xf * xf, -1, keepdims=True) + EPS)
    return (y * w.astype(jnp.float32)).astype(x.dtype)


def l2_norm(x):
    return x * lax.rsqrt(jnp.sum(x * x, -1, keepdims=True) + EPS)


def causal_dwconv(x, w):
    k = w.shape[0]
    return lax.conv_general_dilated(
        x, w[:, None, :].astype(x.dtype), window_strides=(1,), padding=((k - 1, 0),),
        dimension_numbers=('NWC', 'WIO', 'NWC'), feature_group_count=x.shape[-1])


def gated_delta_rule(q, k, v, beta, g):
    q, k, v, beta, g = (t.astype(jnp.float32) for t in (q, k, v, beta, g))
    b, l, h, dk = q.shape
    dv = v.shape[-1]
    c = DN_CHUNK
    n = l // c

    def chunks(t):
        t = t.reshape((b, n, c, h) + t.shape[3:])
        return jnp.moveaxis(t, 3, 1)

    q, k, v, beta, g = chunks(q), chunks(k), chunks(v), chunks(beta), chunks(g)
    decay = jnp.cumsum(g, -1)
    diff = decay[..., :, None] - decay[..., None, :]
    idx = jnp.arange(c)
    strict = idx[:, None] > idx[None, :]
    incl = idx[:, None] >= idx[None, :]
    dmask = jnp.exp(jnp.where(incl, diff, -jnp.inf))
    kk = jnp.einsum('bhnid,bhnjd->bhnij', k, k)
    a_strict = jnp.where(strict, beta[..., None] * kk * dmask, 0.0)
    t_mat = a_strict + jnp.eye(c, dtype=jnp.float32)
    rhs = jnp.concatenate([v * beta[..., None], k * (beta * jnp.exp(decay))[..., None]], -1)
    sol = lax.linalg.triangular_solve(t_mat, rhs, left_side=True, lower=True, unit_diagonal=True)
    u = sol[..., :dv]
    w = sol[..., dv:]
    qk = jnp.einsum('bhnid,bhnjd->bhnij', q, k) * dmask
    q_dec = q * jnp.exp(decay)[..., None]
    k_dec = k * jnp.exp(decay[..., -1:] - decay)[..., None]
    g_last = jnp.exp(decay[..., -1])

    def step(s, inp):
        u_n, w_n, qk_n, qd_n, kd_n, gl_n = inp
        v_new = u_n - jnp.einsum('bhcd,bhde->bhce', w_n, s)
        o = jnp.einsum('bhcd,bhde->bhce', qd_n, s) + jnp.einsum('bhij,bhje->bhie', qk_n, v_new)
        s = s * gl_n[..., None, None] + jnp.einsum('bhcd,bhce->bhde', kd_n, v_new)
        return s, o

    xs = tuple(jnp.moveaxis(t, 2, 0) for t in (u, w, qk, q_dec, k_dec, g_last))
    s0 = jnp.zeros((b, h, dk, dv), jnp.float32)
    _, o = lax.scan(step, s0, xs)
    return jnp.transpose(o, (1, 0, 3, 2, 4)).reshape(b, l, h, dv)


def even_mixer(h, w_in, conv_a_w, dn_conv_w, a_log, dt_bias, dn_norm_w, w_out):
    b, l, _ = h.shape
    p = h @ w_in
    sizes = [D_CONV, D_CONV, D_CONV, 3 * DN_DIM, DN_DIM, DN_HEADS, DN_HEADS]
    splits = [sum(sizes[:i + 1]) for i in range(len(sizes) - 1)]
    a_gate_in, a_gate_out, a_h, qkv, z, beta_raw, alpha_raw = jnp.split(p, splits, -1)
    y_a = a_gate_out * causal_dwconv(a_gate_in * a_h, conv_a_w)
    qkv = jax.nn.silu(causal_dwconv(qkv, dn_conv_w))
    q, k, v = jnp.split(qkv, 3, -1)
    hd = (b, l, DN_HEADS, DN_HEAD_DIM)
    q = l2_norm(q.reshape(hd).astype(jnp.float32)) * (DN_HEAD_DIM ** -0.5)
    k = l2_norm(k.reshape(hd).astype(jnp.float32))
    v = v.reshape(hd).astype(jnp.float32)
    beta = jax.nn.sigmoid(beta_raw.astype(jnp.float32))
    g = -jnp.exp(a_log.astype(jnp.float32)) * jax.nn.softplus(
        alpha_raw.astype(jnp.float32) + dt_bias.astype(jnp.float32))
    pad = (-N_META) % DN_CHUNK
    padw = lambda t: jnp.pad(t, ((0, 0), (pad, 0)) + ((0, 0),) * (t.ndim - 2))
    o = gated_delta_rule(padw(q), padw(k), padw(v), padw(beta), padw(g))[:, pad:]
    o = rms_norm(o, dn_norm_w) * jax.nn.silu(z.reshape(hd).astype(jnp.float32))
    y_b = o.reshape(b, l, DN_DIM).astype(h.dtype)
    return jnp.concatenate([y_a, y_b], -1) @ w_out


def sink_softmax(logits, sink):
    m = jnp.maximum(jnp.max(logits, -1, keepdims=True), sink)
    e = jnp.exp(logits - m)
    return e / (jnp.sum(e, -1, keepdims=True) + jnp.exp(sink - m))


def swa_mixer(h, wq, wk, wv, q_norm_w, k_norm_w, sinks, wo):
    b, l, _ = h.shape
    kv, grp, d = SWA_KV_HEADS, SWA_HEADS // SWA_KV_HEADS, SWA_HEAD_DIM
    q = rms_norm((h @ wq).reshape(b, l, kv, grp, d), q_norm_w) * (d ** -0.5)
    k = rms_norm((h @ wk).reshape(b, l, kv, d), k_norm_w)
    v = (h @ wv).reshape(b, l, kv, d)
    sink = sinks.astype(jnp.float32).reshape(kv, grp)
    qm, qr = q[:, :N_META], q[:, N_META:]
    km, kr = k[:, :N_META], k[:, N_META:]
    vm, vr = v[:, :N_META], v[:, N_META:]
    sm = jnp.einsum('bikgd,bjkd->bkgij', qm, km).astype(jnp.float32)
    mmask = jnp.tril(jnp.ones((N_META, N_META), bool))
    pm = sink_softmax(jnp.where(mmask, sm, -jnp.inf), sink[None, :, :, None, None])
    om = jnp.einsum('bkgij,bjkd->bikgd', pm.astype(v.dtype), vm).reshape(b, N_META, SWA_HEADS * d)
    s_real = l - N_META
    nb = s_real // SWA_BLOCK
    qb = qr.reshape(b, nb, SWA_BLOCK, kv, grp, d)
    kb = kr.reshape(b, nb, SWA_BLOCK, kv, d)
    vb = vr.reshape(b, nb, SWA_BLOCK, kv, d)
    band = lambda t: jnp.concatenate(
        [jnp.concatenate([jnp.zeros_like(t[:, :1]), t[:, :-1]], 1), t], 2)
    kband, vband = band(kb), band(vb)
    s_meta = jnp.einsum('bnikgd,bjkd->bnkgij', qb, km).astype(jnp.float32)
    s_band = jnp.einsum('bnikgd,bnjkd->bnkgij', qb, kband).astype(jnp.float32)
    i = jnp.arange(SWA_BLOCK)[:, None]
    j = jnp.arange(2 * SWA_BLOCK)[None, :]
    n = jnp.arange(nb)[:, None, None]
    rel = i + SWA_BLOCK - j
    valid = (rel >= 0) & (rel < SWA_WINDOW) & (n * SWA_BLOCK - SWA_BLOCK + j >= 0)
    s_band = jnp.where(valid[None, :, None, None], s_band, -jnp.inf)
    p = sink_softmax(jnp.concatenate([s_meta, s_band], -1), sink[None, None, :, :, None, None])
    p = p.astype(v.dtype)
    orr = (jnp.einsum('bnkgim,bmkd->bnikgd', p[..., :N_META], vm)
           + jnp.einsum('bnkgij,bnjkd->bnikgd', p[..., N_META:], vband))
    orr = orr.reshape(b, s_real, SWA_HEADS * d)
    return jnp.concatenate([om, orr], 1) @ wo


def conv_ffn(h, w_up, conv_w, w_down):
    gate, val = jnp.split(h @ w_up, 2, -1)
    gate = causal_dwconv(gate, conv_w)
    return (jax.nn.silu(gate) * val) @ w_down


def _fwd_setup_inputs(seed: int = 0) -> dict:
    key = jax.random.key(seed)
    ks = jax.random.split(key, 24)
    nrm = lambda k, s, scale: jax.random.normal(k, s, jnp.float32) * scale
    gain = lambda k, s: 1.0 + 0.05 * jax.random.normal(k, s, jnp.float32)
    dt = jnp.exp(jax.random.uniform(ks[7], (N_EVEN, DN_HEADS), jnp.float32, np.log(1e-3), np.log(1e-1)))
    return {
        "x": nrm(ks[0], (BATCH, SEQ, D_MODEL), 1.0),
        "meta_tokens": nrm(ks[1], (N_META, D_MODEL), 1.0),
        "attn_norm_w": gain(ks[2], (DEPTH, D_MODEL)),
        "ffn_norm_w": gain(ks[3], (DEPTH, D_MODEL)),
        "mix_w_in": nrm(ks[4], (N_EVEN, D_MODEL, IN_DIM), D_MODEL ** -0.5),
        "conv_a_w": nrm(ks[5], (N_EVEN, CONV_A_WIDTH, D_CONV), CONV_A_WIDTH ** -0.5),
        "dn_conv_w": nrm(ks[6], (N_EVEN, DN_CONV_WIDTH, 3 * DN_DIM), DN_CONV_WIDTH ** -0.5),
        "dn_a_log": jnp.log(jax.random.uniform(ks[8], (N_EVEN, DN_HEADS), jnp.float32, 1.0, 16.0)),
        "dn_dt_bias": dt + jnp.log(-jnp.expm1(-dt)),
        "dn_norm_w": gain(ks[9], (N_EVEN, DN_HEAD_DIM)),
        "mix_w_out": nrm(ks[10], (N_EVEN, D_MIX, D_MODEL), D_MIX ** -0.5),
        "swa_wq": nrm(ks[11], (N_ODD, D_MODEL, SWA_HEADS * SWA_HEAD_DIM), D_MODEL ** -0.5),
        "swa_wk": nrm(ks[12], (N_ODD, D_MODEL, SWA_KV_HEADS * SWA_HEAD_DIM), D_MODEL ** -0.5),
        "swa_wv": nrm(ks[13], (N_ODD, D_MODEL, SWA_KV_HEADS * SWA_HEAD_DIM), D_MODEL ** -0.5),
        "swa_q_norm_w": gain(ks[14], (N_ODD, SWA_HEAD_DIM)),
        "swa_k_norm_w": gain(ks[15], (N_ODD, SWA_HEAD_DIM)),
        "swa_sinks": nrm(ks[16], (N_ODD, SWA_HEADS), 0.5),
        "swa_wo": nrm(ks[17], (N_ODD, SWA_HEADS * SWA_HEAD_DIM, D_MODEL), (SWA_HEADS * SWA_HEAD_DIM) ** -0.5),
        "ffn_w_up": nrm(ks[18], (DEPTH, D_MODEL, 2 * D_FF), D_MODEL ** -0.5),
        "ffn_conv_w": nrm(ks[19], (DEPTH, FFN_CONV_WIDTH, D_FF), FFN_CONV_WIDTH ** -0.5),
        "ffn_w_down": nrm(ks[20], (DEPTH, D_FF, D_MODEL), D_FF ** -0.5),
    }


def _fwd_reference(x, meta_tokens, attn_norm_w, ffn_norm_w, mix_w_in, conv_a_w, dn_conv_w, dn_a_log,
              dn_dt_bias, dn_norm_w, mix_w_out, swa_wq, swa_wk, swa_wv, swa_q_norm_w, swa_k_norm_w,
              swa_sinks, swa_wo, ffn_w_up, ffn_conv_w, ffn_w_down):
    b = x.shape[0]
    meta = jnp.broadcast_to(meta_tokens[None].astype(x.dtype), (b, N_META, D_MODEL))
    h = jnp.concatenate([meta, x], 1)
    for layer in range(DEPTH):
        i = layer // 2
        hn = rms_norm(h, attn_norm_w[layer])
        if layer % 2 == 0:
            mix = even_mixer(hn, mix_w_in[i], conv_a_w[i], dn_conv_w[i], dn_a_log[i],
                             dn_dt_bias[i], dn_norm_w[i], mix_w_out[i])
        else:
            mix = swa_mixer(hn, swa_wq[i], swa_wk[i], swa_wv[i], swa_q_norm_w[i],
                            swa_k_norm_w[i], swa_sinks[i], swa_wo[i])
        h = h + mix.astype(h.dtype)
        ff = conv_ffn(rms_norm(h, ffn_norm_w[layer]), ffn_w_up[layer], ffn_conv_w[layer], ffn_w_down[layer])
        h = h + ff.astype(h.dtype)
    return h[:, N_META:]


import jax as _jax
import jax.numpy as _jnp

TWIN_FORMAT = 'train_step'
FWD_PARAMS = ['x', 'meta_tokens', 'attn_norm_w', 'ffn_norm_w', 'mix_w_in', 'conv_a_w', 'dn_conv_w', 'dn_a_log', 'dn_dt_bias', 'dn_norm_w', 'mix_w_out', 'swa_wq', 'swa_wk', 'swa_wv', 'swa_q_norm_w', 'swa_k_norm_w', 'swa_sinks', 'swa_wo', 'ffn_w_up', 'ffn_conv_w', 'ffn_w_down']
TWIN_WEIGHTS = ['meta_tokens', 'attn_norm_w', 'ffn_norm_w', 'mix_w_in', 'conv_a_w', 'dn_conv_w', 'dn_a_log', 'dn_dt_bias', 'dn_norm_w', 'mix_w_out', 'swa_wq', 'swa_wk', 'swa_wv', 'swa_q_norm_w', 'swa_k_norm_w', 'swa_sinks', 'swa_wo', 'ffn_w_up', 'ffn_conv_w', 'ffn_w_down']
TWIN_DIFF_INPUT = 'x'
TWIN_INPUTS = ['x', 'meta_tokens', 'attn_norm_w', 'ffn_norm_w', 'mix_w_in', 'conv_a_w', 'dn_conv_w', 'dn_a_log', 'dn_dt_bias', 'dn_norm_w', 'mix_w_out', 'swa_wq', 'swa_wk', 'swa_wv', 'swa_q_norm_w', 'swa_k_norm_w', 'swa_sinks', 'swa_wo', 'ffn_w_up', 'ffn_conv_w', 'ffn_w_down', 'loss_target', 'm_meta_tokens', 'm_attn_norm_w', 'm_ffn_norm_w', 'm_mix_w_in', 'm_conv_a_w', 'm_dn_conv_w', 'm_dn_a_log', 'm_dn_dt_bias', 'm_dn_norm_w', 'm_mix_w_out', 'm_swa_wq', 'm_swa_wk', 'm_swa_wv', 'm_swa_q_norm_w', 'm_swa_k_norm_w', 'm_swa_sinks', 'm_swa_wo', 'm_ffn_w_up', 'm_ffn_conv_w', 'm_ffn_w_down', 'v_meta_tokens', 'v_attn_norm_w', 'v_ffn_norm_w', 'v_mix_w_in', 'v_conv_a_w', 'v_dn_conv_w', 'v_dn_a_log', 'v_dn_dt_bias', 'v_dn_norm_w', 'v_mix_w_out', 'v_swa_wq', 'v_swa_wk', 'v_swa_wv', 'v_swa_q_norm_w', 'v_swa_k_norm_w', 'v_swa_sinks', 'v_swa_wo', 'v_ffn_w_up', 'v_ffn_conv_w', 'v_ffn_w_down']
TWIN_OUTPUTS = ['loss', 'grad_x', 'grad_meta_tokens', 'grad_attn_norm_w', 'grad_ffn_norm_w', 'grad_mix_w_in', 'grad_conv_a_w', 'grad_dn_conv_w', 'grad_dn_a_log', 'grad_dn_dt_bias', 'grad_dn_norm_w', 'grad_mix_w_out', 'grad_swa_wq', 'grad_swa_wk', 'grad_swa_wv', 'grad_swa_q_norm_w', 'grad_swa_k_norm_w', 'grad_swa_sinks', 'grad_swa_wo', 'grad_ffn_w_up', 'grad_ffn_conv_w', 'grad_ffn_w_down', 'delta_meta_tokens', 'delta_attn_norm_w', 'delta_ffn_norm_w', 'delta_mix_w_in', 'delta_conv_a_w', 'delta_dn_conv_w', 'delta_dn_a_log', 'delta_dn_dt_bias', 'delta_dn_norm_w', 'delta_mix_w_out', 'delta_swa_wq', 'delta_swa_wk', 'delta_swa_wv', 'delta_swa_q_norm_w', 'delta_swa_k_norm_w', 'delta_swa_sinks', 'delta_swa_wo', 'delta_ffn_w_up', 'delta_ffn_conv_w', 'delta_ffn_w_down', 'new_m_meta_tokens', 'new_m_attn_norm_w', 'new_m_ffn_norm_w', 'new_m_mix_w_in', 'new_m_conv_a_w', 'new_m_dn_conv_w', 'new_m_dn_a_log', 'new_m_dn_dt_bias', 'new_m_dn_norm_w', 'new_m_mix_w_out', 'new_m_swa_wq', 'new_m_swa_wk', 'new_m_swa_wv', 'new_m_swa_q_norm_w', 'new_m_swa_k_norm_w', 'new_m_swa_sinks', 'new_m_swa_wo', 'new_m_ffn_w_up', 'new_m_ffn_conv_w', 'new_m_ffn_w_down', 'new_v_meta_tokens', 'new_v_attn_norm_w', 'new_v_ffn_norm_w', 'new_v_mix_w_in', 'new_v_conv_a_w', 'new_v_dn_conv_w', 'new_v_dn_a_log', 'new_v_dn_dt_bias', 'new_v_dn_norm_w', 'new_v_mix_w_out', 'new_v_swa_wq', 'new_v_swa_wk', 'new_v_swa_wv', 'new_v_swa_q_norm_w', 'new_v_swa_k_norm_w', 'new_v_swa_sinks', 'new_v_swa_wo', 'new_v_ffn_w_up', 'new_v_ffn_conv_w', 'new_v_ffn_w_down']
TWIN_LEAF_KINDS = {'loss': 'loss', 'grad_x': 'grad_x', 'grad_meta_tokens': 'grad_w', 'grad_attn_norm_w': 'grad_w', 'grad_ffn_norm_w': 'grad_w', 'grad_mix_w_in': 'grad_w', 'grad_conv_a_w': 'grad_w', 'grad_dn_conv_w': 'grad_w', 'grad_dn_a_log': 'grad_w', 'grad_dn_dt_bias': 'grad_w', 'grad_dn_norm_w': 'grad_w', 'grad_mix_w_out': 'grad_w', 'grad_swa_wq': 'grad_w', 'grad_swa_wk': 'grad_w', 'grad_swa_wv': 'grad_w', 'grad_swa_q_norm_w': 'grad_w', 'grad_swa_k_norm_w': 'grad_w', 'grad_swa_sinks': 'grad_w', 'grad_swa_wo': 'grad_w', 'grad_ffn_w_up': 'grad_w', 'grad_ffn_conv_w': 'grad_w', 'grad_ffn_w_down': 'grad_w', 'delta_meta_tokens': 'delta_w', 'delta_attn_norm_w': 'delta_w', 'delta_ffn_norm_w': 'delta_w', 'delta_mix_w_in': 'delta_w', 'delta_conv_a_w': 'delta_w', 'delta_dn_conv_w': 'delta_w', 'delta_dn_a_log': 'delta_w', 'delta_dn_dt_bias': 'delta_w', 'delta_dn_norm_w': 'delta_w', 'delta_mix_w_out': 'delta_w', 'delta_swa_wq': 'delta_w', 'delta_swa_wk': 'delta_w', 'delta_swa_wv': 'delta_w', 'delta_swa_q_norm_w': 'delta_w', 'delta_swa_k_norm_w': 'delta_w', 'delta_swa_sinks': 'delta_w', 'delta_swa_wo': 'delta_w', 'delta_ffn_w_up': 'delta_w', 'delta_ffn_conv_w': 'delta_w', 'delta_ffn_w_down': 'delta_w', 'new_m_meta_tokens': 'new_m', 'new_m_attn_norm_w': 'new_m', 'new_m_ffn_norm_w': 'new_m', 'new_m_mix_w_in': 'new_m', 'new_m_conv_a_w': 'new_m', 'new_m_dn_conv_w': 'new_m', 'new_m_dn_a_log': 'new_m', 'new_m_dn_dt_bias': 'new_m', 'new_m_dn_norm_w': 'new_m', 'new_m_mix_w_out': 'new_m', 'new_m_swa_wq': 'new_m', 'new_m_swa_wk': 'new_m', 'new_m_swa_wv': 'new_m', 'new_m_swa_q_norm_w': 'new_m', 'new_m_swa_k_norm_w': 'new_m', 'new_m_swa_sinks': 'new_m', 'new_m_swa_wo': 'new_m', 'new_m_ffn_w_up': 'new_m', 'new_m_ffn_conv_w': 'new_m', 'new_m_ffn_w_down': 'new_m', 'new_v_meta_tokens': 'new_v', 'new_v_attn_norm_w': 'new_v', 'new_v_ffn_norm_w': 'new_v', 'new_v_mix_w_in': 'new_v', 'new_v_conv_a_w': 'new_v', 'new_v_dn_conv_w': 'new_v', 'new_v_dn_a_log': 'new_v', 'new_v_dn_dt_bias': 'new_v', 'new_v_dn_norm_w': 'new_v', 'new_v_mix_w_out': 'new_v', 'new_v_swa_wq': 'new_v', 'new_v_swa_wk': 'new_v', 'new_v_swa_wv': 'new_v', 'new_v_swa_q_norm_w': 'new_v', 'new_v_swa_k_norm_w': 'new_v', 'new_v_swa_sinks': 'new_v', 'new_v_swa_wo': 'new_v', 'new_v_ffn_w_up': 'new_v', 'new_v_ffn_conv_w': 'new_v', 'new_v_ffn_w_down': 'new_v'}


def _forward(args):
    return _fwd_reference(*[args[k] for k in FWD_PARAMS])


def _output_shape():
    out = _jax.eval_shape(lambda: _forward(_fwd_setup_inputs(0)))
    return out.shape, out.dtype

N_MICROBATCH = 1
ADAM_LR = 0.001
ADAM_B1 = 0.9
ADAM_B2 = 0.999
ADAM_EPS = 1e-08
ADAM_WD = 0.01
ADAM_STEP = 10
PER_EXAMPLE_BATCH_AXIS = {'x': 0, 'loss_target': 0}
SHARED_INPUTS = []
_WEIGHT_DTYPES = {'meta_tokens': _jnp.float32, 'attn_norm_w': _jnp.float32, 'ffn_norm_w': _jnp.float32, 'mix_w_in': _jnp.float32, 'conv_a_w': _jnp.float32, 'dn_conv_w': _jnp.float32, 'dn_a_log': _jnp.float32, 'dn_dt_bias': _jnp.float32, 'dn_norm_w': _jnp.float32, 'mix_w_out': _jnp.float32, 'swa_wq': _jnp.float32, 'swa_wk': _jnp.float32, 'swa_wv': _jnp.float32, 'swa_q_norm_w': _jnp.float32, 'swa_k_norm_w': _jnp.float32, 'swa_sinks': _jnp.float32, 'swa_wo': _jnp.float32, 'ffn_w_up': _jnp.float32, 'ffn_conv_w': _jnp.float32, 'ffn_w_down': _jnp.float32}
MOMENT_SCALE = {'meta_tokens': 4.982307e-02, 'attn_norm_w': 3.895469e+01, 'ffn_norm_w': 2.542662e+01, 'mix_w_in': 7.803366e-01, 'conv_a_w': 1.806454e+01, 'dn_conv_w': 6.164966e-01, 'dn_a_log': 6.089575e+00, 'dn_dt_bias': 5.981374e+00, 'dn_norm_w': 4.951207e+01, 'mix_w_out': 1.168904e+00, 'swa_wq': 7.052724e-02, 'swa_wk': 1.428368e-01, 'swa_wv': 8.677013e-01, 'swa_q_norm_w': 6.045185e+00, 'swa_k_norm_w': 6.085933e+00, 'swa_sinks': 1.756465e-01, 'swa_wo': 3.435217e-01, 'ffn_w_up': 2.951994e-01, 'ffn_conv_w': 2.857182e+00, 'ffn_w_down': 3.864860e-01}


def _to_microbatches(a, axis):
    t = _jnp.moveaxis(a, axis, 0)
    t = t.reshape((N_MICROBATCH, t.shape[0] // N_MICROBATCH) + t.shape[1:])
    return _jnp.moveaxis(t, 1, axis + 1)


def setup_inputs(seed: int = 0) -> dict:
    inp = _fwd_setup_inputs(seed)
    key = _jax.random.fold_in(_jax.random.key(seed), 7919)
    shape, _ = _output_shape()
    out = dict(inp)
    out["loss_target"] = _jax.random.normal(_jax.random.fold_in(key, 0), shape, _jnp.float32)
    for i, name in enumerate(TWIN_WEIGHTS):
        w = inp[name].astype(_jnp.float32)
        if MOMENT_SCALE is None:
            s = _jnp.sqrt(_jnp.mean(_jnp.square(w)) + 1e-30)
        else:
            s = MOMENT_SCALE[name]
        km, kv = _jax.random.split(_jax.random.fold_in(key, i + 1))
        out[name] = w
        out["m_" + name] = s * _jax.random.normal(km, w.shape, _jnp.float32)
        out["v_" + name] = (s * s) * _jax.random.uniform(kv, w.shape, _jnp.float32, 0.5, 1.5)
    if N_MICROBATCH > 1:
        for name, axis in PER_EXAMPLE_BATCH_AXIS.items():
            out[name] = _to_microbatches(out[name], axis)
    return {'x': out['x'], 'meta_tokens': out['meta_tokens'], 'attn_norm_w': out['attn_norm_w'], 'ffn_norm_w': out['ffn_norm_w'], 'mix_w_in': out['mix_w_in'], 'conv_a_w': out['conv_a_w'], 'dn_conv_w': out['dn_conv_w'], 'dn_a_log': out['dn_a_log'], 'dn_dt_bias': out['dn_dt_bias'], 'dn_norm_w': out['dn_norm_w'], 'mix_w_out': out['mix_w_out'], 'swa_wq': out['swa_wq'], 'swa_wk': out['swa_wk'], 'swa_wv': out['swa_wv'], 'swa_q_norm_w': out['swa_q_norm_w'], 'swa_k_norm_w': out['swa_k_norm_w'], 'swa_sinks': out['swa_sinks'], 'swa_wo': out['swa_wo'], 'ffn_w_up': out['ffn_w_up'], 'ffn_conv_w': out['ffn_conv_w'], 'ffn_w_down': out['ffn_w_down'], 'loss_target': out['loss_target'], 'm_meta_tokens': out['m_meta_tokens'], 'm_attn_norm_w': out['m_attn_norm_w'], 'm_ffn_norm_w': out['m_ffn_norm_w'], 'm_mix_w_in': out['m_mix_w_in'], 'm_conv_a_w': out['m_conv_a_w'], 'm_dn_conv_w': out['m_dn_conv_w'], 'm_dn_a_log': out['m_dn_a_log'], 'm_dn_dt_bias': out['m_dn_dt_bias'], 'm_dn_norm_w': out['m_dn_norm_w'], 'm_mix_w_out': out['m_mix_w_out'], 'm_swa_wq': out['m_swa_wq'], 'm_swa_wk': out['m_swa_wk'], 'm_swa_wv': out['m_swa_wv'], 'm_swa_q_norm_w': out['m_swa_q_norm_w'], 'm_swa_k_norm_w': out['m_swa_k_norm_w'], 'm_swa_sinks': out['m_swa_sinks'], 'm_swa_wo': out['m_swa_wo'], 'm_ffn_w_up': out['m_ffn_w_up'], 'm_ffn_conv_w': out['m_ffn_conv_w'], 'm_ffn_w_down': out['m_ffn_w_down'], 'v_meta_tokens': out['v_meta_tokens'], 'v_attn_norm_w': out['v_attn_norm_w'], 'v_ffn_norm_w': out['v_ffn_norm_w'], 'v_mix_w_in': out['v_mix_w_in'], 'v_conv_a_w': out['v_conv_a_w'], 'v_dn_conv_w': out['v_dn_conv_w'], 'v_dn_a_log': out['v_dn_a_log'], 'v_dn_dt_bias': out['v_dn_dt_bias'], 'v_dn_norm_w': out['v_dn_norm_w'], 'v_mix_w_out': out['v_mix_w_out'], 'v_swa_wq': out['v_swa_wq'], 'v_swa_wk': out['v_swa_wk'], 'v_swa_wv': out['v_swa_wv'], 'v_swa_q_norm_w': out['v_swa_q_norm_w'], 'v_swa_k_norm_w': out['v_swa_k_norm_w'], 'v_swa_sinks': out['v_swa_sinks'], 'v_swa_wo': out['v_swa_wo'], 'v_ffn_w_up': out['v_ffn_w_up'], 'v_ffn_conv_w': out['v_ffn_conv_w'], 'v_ffn_w_down': out['v_ffn_w_down']}


def _loss(weights, diff, rest, loss_target):
    with _jax.named_scope("forward"):
        args = {**rest, TWIN_DIFF_INPUT: diff, **{k: w.astype(_WEIGHT_DTYPES[k]) for k, w in weights.items()}}
        y = _forward(args)
    with _jax.named_scope("loss_head"):
        err = _jnp.square(y.astype(_jnp.float32) - loss_target)
        return 0.5 * _jnp.sum(_jnp.mean(err, axis=-1)) if err.ndim else 0.5 * err


def _adamw(w, g, m, v):
    m = ADAM_B1 * m + (1.0 - ADAM_B1) * g
    v = ADAM_B2 * v + (1.0 - ADAM_B2) * _jnp.square(g)
    m_hat = m / (1.0 - ADAM_B1 ** ADAM_STEP)
    v_hat = v / (1.0 - ADAM_B2 ** ADAM_STEP)
    delta = -ADAM_LR * (m_hat / (_jnp.sqrt(v_hat) + ADAM_EPS) + ADAM_WD * w)
    return delta, m, v


def reference(x, meta_tokens, attn_norm_w, ffn_norm_w, mix_w_in, conv_a_w, dn_conv_w, dn_a_log, dn_dt_bias, dn_norm_w, mix_w_out, swa_wq, swa_wk, swa_wv, swa_q_norm_w, swa_k_norm_w, swa_sinks, swa_wo, ffn_w_up, ffn_conv_w, ffn_w_down, loss_target, m_meta_tokens, m_attn_norm_w, m_ffn_norm_w, m_mix_w_in, m_conv_a_w, m_dn_conv_w, m_dn_a_log, m_dn_dt_bias, m_dn_norm_w, m_mix_w_out, m_swa_wq, m_swa_wk, m_swa_wv, m_swa_q_norm_w, m_swa_k_norm_w, m_swa_sinks, m_swa_wo, m_ffn_w_up, m_ffn_conv_w, m_ffn_w_down, v_meta_tokens, v_attn_norm_w, v_ffn_norm_w, v_mix_w_in, v_conv_a_w, v_dn_conv_w, v_dn_a_log, v_dn_dt_bias, v_dn_norm_w, v_mix_w_out, v_swa_wq, v_swa_wk, v_swa_wv, v_swa_q_norm_w, v_swa_k_norm_w, v_swa_sinks, v_swa_wo, v_ffn_w_up, v_ffn_conv_w, v_ffn_w_down):
    given = dict(x=x, meta_tokens=meta_tokens, attn_norm_w=attn_norm_w, ffn_norm_w=ffn_norm_w, mix_w_in=mix_w_in, conv_a_w=conv_a_w, dn_conv_w=dn_conv_w, dn_a_log=dn_a_log, dn_dt_bias=dn_dt_bias, dn_norm_w=dn_norm_w, mix_w_out=mix_w_out, swa_wq=swa_wq, swa_wk=swa_wk, swa_wv=swa_wv, swa_q_norm_w=swa_q_norm_w, swa_k_norm_w=swa_k_norm_w, swa_sinks=swa_sinks, swa_wo=swa_wo, ffn_w_up=ffn_w_up, ffn_conv_w=ffn_conv_w, ffn_w_down=ffn_w_down, loss_target=loss_target, m_meta_tokens=m_meta_tokens, m_attn_norm_w=m_attn_norm_w, m_ffn_norm_w=m_ffn_norm_w, m_mix_w_in=m_mix_w_in, m_conv_a_w=m_conv_a_w, m_dn_conv_w=m_dn_conv_w, m_dn_a_log=m_dn_a_log, m_dn_dt_bias=m_dn_dt_bias, m_dn_norm_w=m_dn_norm_w, m_mix_w_out=m_mix_w_out, m_swa_wq=m_swa_wq, m_swa_wk=m_swa_wk, m_swa_wv=m_swa_wv, m_swa_q_norm_w=m_swa_q_norm_w, m_swa_k_norm_w=m_swa_k_norm_w, m_swa_sinks=m_swa_sinks, m_swa_wo=m_swa_wo, m_ffn_w_up=m_ffn_w_up, m_ffn_conv_w=m_ffn_conv_w, m_ffn_w_down=m_ffn_w_down, v_meta_tokens=v_meta_tokens, v_attn_norm_w=v_attn_norm_w, v_ffn_norm_w=v_ffn_norm_w, v_mix_w_in=v_mix_w_in, v_conv_a_w=v_conv_a_w, v_dn_conv_w=v_dn_conv_w, v_dn_a_log=v_dn_a_log, v_dn_dt_bias=v_dn_dt_bias, v_dn_norm_w=v_dn_norm_w, v_mix_w_out=v_mix_w_out, v_swa_wq=v_swa_wq, v_swa_wk=v_swa_wk, v_swa_wv=v_swa_wv, v_swa_q_norm_w=v_swa_q_norm_w, v_swa_k_norm_w=v_swa_k_norm_w, v_swa_sinks=v_swa_sinks, v_swa_wo=v_swa_wo, v_ffn_w_up=v_ffn_w_up, v_ffn_conv_w=v_ffn_conv_w, v_ffn_w_down=v_ffn_w_down)
    weights = {n: given[n] for n in TWIN_WEIGHTS}
    shared = {n: given[n] for n in SHARED_INPUTS}
    per_example = {n: given[n] for n in ['x']}
    grad_fn = _jax.value_and_grad(_loss, argnums=(0, 1))

    def one_microbatch(ex, loss_target):
        ex = dict(ex)
        diff = ex.pop(TWIN_DIFF_INPUT)
        return grad_fn(weights, diff, {**shared, **ex}, loss_target)

    if N_MICROBATCH == 1:
        loss, (grad_w, grad_x) = one_microbatch(per_example, given["loss_target"])
    else:
        def body(carry, xs):
            loss_sum, grad_sum = carry
            l_k, (gw_k, gx_k) = one_microbatch(xs[0], xs[1])
            with _jax.named_scope("update"):
                return (loss_sum + l_k, _jax.tree.map(_jnp.add, grad_sum, gw_k)), gx_k

        init = (_jnp.zeros((), _jnp.float32), _jax.tree.map(_jnp.zeros_like, weights))
        (loss, grad_w), grad_x = _jax.lax.scan(body, init, (per_example, given["loss_target"]))
    with _jax.named_scope("update"):
        delta_w, new_m, new_v = {}, {}, {}
        for n in TWIN_WEIGHTS:
            delta_w[n], new_m[n], new_v[n] = _adamw(weights[n], grad_w[n], given["m_" + n], given["v_" + n])
    return (loss, grad_x, *[grad_w[n] for n in TWIN_WEIGHTS], *[delta_w[n] for n in TWIN_WEIGHTS],
            *[new_m[n] for n in TWIN_WEIGHTS], *[new_v[n] for n in TWIN_WEIGHTS])
```

```python
import functools

import jax
import jax.numpy as jnp
from jax import lax
from jax.experimental import pallas as pl
from jax.experimental.pallas import tpu as pltpu

F32 = jnp.float32
BF16 = jnp.bfloat16
MXU_DTYPE = BF16
HIGHEST = lax.Precision.HIGHEST

D_MODEL = 1024
N_META = 16
PAD_ROWS = 112
D_CONV = 512
DN_HEADS = 4
DN_HEAD_DIM = 128
DN_DIM = DN_HEADS * DN_HEAD_DIM
DN_CHUNK = 64
SEG = 512
N_SEG = 7
SWA_HEADS = 16
SWA_KV_HEADS = 4
SWA_GROUP = SWA_HEADS // SWA_KV_HEADS
SWA_HEAD_DIM = 64
SWA_BLOCK = 128
D_FF = 2816
EPS = 1e-6
NEG = -1e30
N_DEV = 8

ADAM_LR = 0.001
ADAM_B1 = 0.9
ADAM_B2 = 0.999
ADAM_EPS = 1e-08
ADAM_WD = 0.01
ADAM_STEP = 10

VMEM_LIMIT_BYTES = 52 * 1024 * 1024
SUBLANES = 8
LANES = 128


def _pick(n, prefs):
    for p in prefs:
        if n % p == 0:
            return p
    return n


def _params(sem, vmem=VMEM_LIMIT_BYTES):
    return pltpu.CompilerParams(dimension_semantics=sem, vmem_limit_bytes=vmem)


def _rms(x, w):
    return x * lax.rsqrt(jnp.mean(x * x, -1, keepdims=True) + EPS) * w


def _norm_matmul(h, nw, w, *, o_seg=None, name):
    m, k = h.shape
    n = w.shape[1]
    tm = _pick(m, (1408, 384, 128))
    tn = _pick(o_seg or n, (512, 256, 128))

    def body(h_ref, nw_ref, w_ref, o_ref, hn_ref, hn_s):
        @pl.when(pl.program_id(1) == 0)
        def _():
            hn = _rms(h_ref[...], nw_ref[...]).astype(MXU_DTYPE)
            hn_s[...] = hn
            hn_ref[...] = hn

        o_ref[...] = jnp.dot(hn_s[...], w_ref[...], preferred_element_type=F32)

    if o_seg:
        per = o_seg // tn
        o_shape = jax.ShapeDtypeStruct((n // o_seg, m, o_seg), F32)
        o_spec = pl.BlockSpec((None, tm, tn), lambda i, j: (j // per, i, j % per))
    else:
        o_shape = jax.ShapeDtypeStruct((m, n), F32)
        o_spec = pl.BlockSpec((tm, tn), lambda i, j: (i, j))
    return pl.pallas_call(
        body, name=name, grid=(m // tm, n // tn),
        in_specs=[pl.BlockSpec((tm, k), lambda i, j: (i, 0)), pl.BlockSpec((1, k), lambda i, j: (0, 0)),
                  pl.BlockSpec((k, tn), lambda i, j: (0, j))],
        out_specs=[o_spec, pl.BlockSpec((tm, k), lambda i, j: (i, 0))],
        out_shape=[o_shape, jax.ShapeDtypeStruct((m, k), MXU_DTYPE)],
        scratch_shapes=[pltpu.VMEM((tm, k), MXU_DTYPE)],
        compiler_params=_params(("parallel", "arbitrary")),
    )(h, nw, w)


def _mm_nn(a, w, *, res=None, a_seg=False, name):
    if a_seg:
        s, m, seg = a.shape
        k = s * seg
    else:
        m, k = a.shape
        seg = k
    n = w.shape[1]
    tm = _pick(m, (1408, 384, 128))
    tn = _pick(n, (512, 256, 128))
    tk = _pick(seg, (1024, 1408, 512, 256, 128))
    nk = k // tk

    def body(*refs):
        if res is None:
            a_ref, w_ref, o_ref, acc = refs
        else:
            a_ref, w_ref, r_ref, o_ref, acc = refs
        kk = pl.program_id(2)

        @pl.when(kk == 0)
        def _():
            acc[...] = jnp.zeros_like(acc)

        acc[...] += jnp.dot(a_ref[...].astype(MXU_DTYPE), w_ref[...], preferred_element_type=F32)

        @pl.when(kk == nk - 1)
        def _():
            o_ref[...] = acc[...] if res is None else acc[...] + r_ref[...]

    if a_seg:
        per = seg // tk
        a_spec = pl.BlockSpec((None, tm, tk), lambda i, j, kk: (kk // per, i, kk % per))
    else:
        a_spec = pl.BlockSpec((tm, tk), lambda i, j, kk: (i, kk))
    in_specs = [a_spec, pl.BlockSpec((tk, tn), lambda i, j, kk: (kk, j))]
    args = [a, w]
    if res is not None:
        in_specs.append(pl.BlockSpec((tm, tn), lambda i, j, kk: (i, j)))
        args.append(res)
    return pl.pallas_call(
        body, name=name, grid=(m // tm, n // tn, nk), in_specs=in_specs,
        out_specs=pl.BlockSpec((tm, tn), lambda i, j, kk: (i, j)),
        out_shape=jax.ShapeDtypeStruct((m, n), F32),
        scratch_shapes=[pltpu.VMEM((tm, tn), F32)],
        compiler_params=_params(("parallel", "parallel", "arbitrary")),
    )(*args)


def _mm_tn(a, b, *, b_seg=False, name):
    m, ka = a.shape
    if b_seg:
        s, _, seg = b.shape
        n = s * seg
    else:
        n = b.shape[1]
        seg = n
    tka = _pick(ka, (1024, 1408, 512, 256, 128))
    tn = _pick(seg, (512, 256, 128))
    tmc = _pick(m, (1408, 384, 128))
    nm = m // tmc

    def body(a_ref, b_ref, o_ref, acc):
        mm = pl.program_id(2)

        @pl.when(mm == 0)
        def _():
            acc[...] = jnp.zeros_like(acc)

        acc[...] += lax.dot_general(a_ref[...].astype(MXU_DTYPE), b_ref[...].astype(MXU_DTYPE),
                                    (((0,), (0,)), ((), ())), preferred_element_type=F32)

        @pl.when(mm == nm - 1)
        def _():
            o_ref[...] = acc[...]

    if b_seg:
        per = seg // tn
        b_spec = pl.BlockSpec((None, tmc, tn), lambda i, j, mm: (j // per, mm, j % per))
    else:
        b_spec = pl.BlockSpec((tmc, tn), lambda i, j, mm: (mm, j))
    return pl.pallas_call(
        body, name=name, grid=(ka // tka, n // tn, nm),
        in_specs=[pl.BlockSpec((tmc, tka), lambda i, j, mm: (mm, i)), b_spec],
        out_specs=pl.BlockSpec((tka, tn), lambda i, j, mm: (i, j)),
        out_shape=jax.ShapeDtypeStruct((ka, n), F32),
        scratch_shapes=[pltpu.VMEM((tka, tn), F32)],
        compiler_params=_params(("parallel", "parallel", "arbitrary")),
    )(a, b)


def _rmsnorm_bwd(dhn, h, nw, dres, *, name):
    m, d = h.shape
    tm = _pick(m, (384, 128))

    def body(dhn_ref, h_ref, nw_ref, dres_ref, dh_ref, dnw_ref):
        i = pl.program_id(0)
        x = h_ref[...]
        r = lax.rsqrt(jnp.mean(x * x, -1, keepdims=True) + EPS)
        xh = x * r
        dy = dhn_ref[...]
        dxh = dy * nw_ref[...]
        dx = r * (dxh - xh * jnp.mean(dxh * xh, -1, keepdims=True))
        row = i * tm + lax.broadcasted_iota(jnp.int32, (tm, 1), 0)
        dh_ref[...] = jnp.where(row >= PAD_ROWS, dres_ref[...] + dx, 0.0)

        @pl.when(i == 0)
        def _():
            dnw_ref[...] = jnp.zeros_like(dnw_ref)

        dnw_ref[...] += jnp.sum(dy * xh, 0, keepdims=True)

    return pl.pallas_call(
        body, name=name, grid=(m // tm,),
        in_specs=[pl.BlockSpec((tm, d), lambda i: (i, 0)), pl.BlockSpec((tm, d), lambda i: (i, 0)),
                  pl.BlockSpec((1, d), lambda i: (0, 0)), pl.BlockSpec((tm, d), lambda i: (i, 0))],
        out_specs=[pl.BlockSpec((tm, d), lambda i: (i, 0)), pl.BlockSpec((1, d), lambda i: (0, 0))],
        out_shape=[jax.ShapeDtypeStruct((m, d), F32), jax.ShapeDtypeStruct((1, d), F32)],
        compiler_params=_params(("arbitrary",)),
    )(dhn, h, nw, dres)


ROW_CHUNK = 248


def _row_chunks(m):
    out, s = [], SUBLANES
    while s < m:
        n = min(ROW_CHUNK, m - s)
        out.append((s, n))
        s += n
    return out


def _conv_at(load, w, width, s, n):
    acc = w[width - 1:width, :] * load(s, n)
    for j in range(width - 1):
        acc = acc + w[j:j + 1, :] * load(s - (width - 1 - j), n)
    return acc


def _conv_t_at(load, w, width, s, n):
    acc = w[width - 1:width, :] * load(s, n)
    for j in range(width - 1):
        acc = acc + w[j:j + 1, :] * load(s + (width - 1 - j), n)
    return acc


def _dconv_w(load_x, d, width, s, n):
    rows = [jnp.sum(d * load_x(s - (width - 1 - j), n), 0, keepdims=True) for j in range(width)]
    rows.append(jnp.zeros((SUBLANES - width, d.shape[1]), F32))
    return jnp.concatenate(rows, 0)


def _pad_w(w):
    return jnp.concatenate([w, jnp.zeros((SUBLANES - w.shape[0], w.shape[1]), w.dtype)], 0)


def _sigmoid(x):
    return 1.0 / (1.0 + jnp.exp(-x))


def _ffn_act_fwd(u, cw, *, name):
    _, m, f = u.shape
    cb = _pick(f, (256, 128))
    chunks = _row_chunks(m)

    def body(g_ref, v_ref, w_ref, o_ref):
        w = w_ref[...]
        o_ref[pl.ds(0, SUBLANES), :] = jnp.zeros((SUBLANES, cb), o_ref.dtype)
        for s, n in chunks:
            c = _conv_at(lambda a, b: g_ref[pl.ds(a, b), :], w, 3, s, n)
            o_ref[pl.ds(s, n), :] = (c * _sigmoid(c) * v_ref[pl.ds(s, n), :]).astype(o_ref.dtype)

    return pl.pallas_call(
        body, name=name, grid=(f // cb,),
        in_specs=[pl.BlockSpec((None, m, cb), lambda j: (0, 0, j)), pl.BlockSpec((None, m, cb), lambda j: (1, 0, j)),
                  pl.BlockSpec((SUBLANES, cb), lambda j: (0, j))],
        out_specs=pl.BlockSpec((m, cb), lambda j: (0, j)),
        out_shape=jax.ShapeDtypeStruct((m, f), MXU_DTYPE),
        compiler_params=_params(("parallel",)),
    )(u, u, cw)


def _ffn_act_bwd(da, u, cw, *, name):
    _, m, f = u.shape
    cb = LANES
    chunks = _row_chunks(m)

    def body(da_ref, g_ref, v_ref, w_ref, du_ref, dw_ref, dg_s):
        w = w_ref[...]
        zeros8 = jnp.zeros((SUBLANES, cb), F32)
        dg_s[pl.ds(0, SUBLANES), :] = zeros8
        dg_s[pl.ds(m, SUBLANES), :] = zeros8
        du_ref[0, pl.ds(0, SUBLANES), :] = zeros8
        du_ref[1, pl.ds(0, SUBLANES), :] = zeros8
        load_g = lambda a, b: g_ref[pl.ds(a, b), :]
        dw = jnp.zeros((SUBLANES, cb), F32)
        for s, n in chunks:
            c = _conv_at(load_g, w, 3, s, n)
            sg = _sigmoid(c)
            d = da_ref[pl.ds(s, n), :]
            du_ref[1, pl.ds(s, n), :] = d * (c * sg)
            dc = d * v_ref[pl.ds(s, n), :] * (sg * (1.0 + c * (1.0 - sg)))
            dg_s[pl.ds(s, n), :] = dc
            dw = dw + _dconv_w(load_g, dc, 3, s, n)
        dw_ref[...] = dw
        for s, n in chunks:
            du_ref[0, pl.ds(s, n), :] = _conv_t_at(lambda a, b: dg_s[pl.ds(a, b), :], w, 3, s, n)

    return pl.pallas_call(
        body, name=name, grid=(f // cb,),
        in_specs=[pl.BlockSpec((m, cb), lambda j: (0, j)), pl.BlockSpec((None, m, cb), lambda j: (0, 0, j)),
                  pl.BlockSpec((None, m, cb), lambda j: (1, 0, j)), pl.BlockSpec((SUBLANES, cb), lambda j: (0, j))],
        out_specs=[pl.BlockSpec((2, m, cb), lambda j: (0, 0, j)), pl.BlockSpec((SUBLANES, cb), lambda j: (0, j))],
        out_shape=[jax.ShapeDtypeStruct((2, m, f), F32), jax.ShapeDtypeStruct((SUBLANES, f), F32)],
        scratch_shapes=[pltpu.VMEM((m + SUBLANES, cb), F32)],
        compiler_params=_params(("parallel",)),
    )(da, u, u, cw)


def _shortconv_fwd(pm, cw, *, name):
    _, m, seg = pm.shape
    cb = _pick(seg, (256, 128))
    chunks = _row_chunks(m)

    def body(gi_ref, go_ref, ah_ref, w_ref, o_ref):
        w = w_ref[...]
        o_ref[pl.ds(0, SUBLANES), :] = jnp.zeros((SUBLANES, cb), o_ref.dtype)
        load_m = lambda a, b: gi_ref[pl.ds(a, b), :] * ah_ref[pl.ds(a, b), :]
        for s, n in chunks:
            o_ref[pl.ds(s, n), :] = (go_ref[pl.ds(s, n), :] * _conv_at(load_m, w, 3, s, n)).astype(o_ref.dtype)

    return pl.pallas_call(
        body, name=name, grid=(seg // cb,),
        in_specs=[pl.BlockSpec((None, m, cb), lambda j: (0, 0, j)), pl.BlockSpec((None, m, cb), lambda j: (1, 0, j)),
                  pl.BlockSpec((None, m, cb), lambda j: (2, 0, j)), pl.BlockSpec((SUBLANES, cb), lambda j: (0, j))],
        out_specs=pl.BlockSpec((m, cb), lambda j: (0, j)),
        out_shape=jax.ShapeDtypeStruct((m, seg), MXU_DTYPE),
        compiler_params=_params(("parallel",)),
    )(pm, pm, pm, cw)


def _shortconv_bwd(dy, pm, cw, dpm, *, name):
    _, m, seg = pm.shape
    cb = LANES
    chunks = _row_chunks(m)

    def body(dy_ref, gi_ref, go_ref, ah_ref, w_ref, dpm_in, dp_ref, dw_ref, dc_s):
        del dpm_in
        w = w_ref[...]
        zeros8 = jnp.zeros((SUBLANES, cb), F32)
        dc_s[pl.ds(0, SUBLANES), :] = zeros8
        dc_s[pl.ds(m, SUBLANES), :] = zeros8
        for t in range(3):
            dp_ref[t, pl.ds(0, SUBLANES), :] = zeros8
        load_m = lambda a, b: gi_ref[pl.ds(a, b), :] * ah_ref[pl.ds(a, b), :]
        dw = jnp.zeros((SUBLANES, cb), F32)
        for s, n in chunks:
            d = dy_ref[pl.ds(s, n), :]
            dp_ref[1, pl.ds(s, n), :] = d * _conv_at(load_m, w, 3, s, n)
            dc = d * go_ref[pl.ds(s, n), :]
            dc_s[pl.ds(s, n), :] = dc
            dw = dw + _dconv_w(load_m, dc, 3, s, n)
        dw_ref[...] = dw
        for s, n in chunks:
            dm = _conv_t_at(lambda a, b: dc_s[pl.ds(a, b), :], w, 3, s, n)
            dp_ref[0, pl.ds(s, n), :] = dm * ah_ref[pl.ds(s, n), :]
            dp_ref[2, pl.ds(s, n), :] = dm * gi_ref[pl.ds(s, n), :]

    return pl.pallas_call(
        body, name=name, grid=(seg // cb,),
        in_specs=[pl.BlockSpec((m, cb), lambda j: (0, j)), pl.BlockSpec((None, m, cb), lambda j: (0, 0, j)),
                  pl.BlockSpec((None, m, cb), lambda j: (1, 0, j)), pl.BlockSpec((None, m, cb), lambda j: (2, 0, j)),
                  pl.BlockSpec((SUBLANES, cb), lambda j: (0, j)), pl.BlockSpec(memory_space=pl.ANY)],
        out_specs=[pl.BlockSpec((3, m, cb), lambda j: (0, 0, j)), pl.BlockSpec((SUBLANES, cb), lambda j: (0, j))],
        out_shape=[jax.ShapeDtypeStruct(dpm.shape, F32), jax.ShapeDtypeStruct((SUBLANES, seg), F32)],
        scratch_shapes=[pltpu.VMEM((m + SUBLANES, cb), F32)],
        input_output_aliases={5: 0},
        compiler_params=_params(("parallel",)),
    )(dy, pm, pm, pm, cw, dpm)


def _dnpre_fwd(pm, cw, *, name):
    _, m, seg = pm.shape
    cb = _pick(seg, (256, 128))
    per = seg // cb
    chunks = _row_chunks(m)

    def body(x_ref, w_ref, o_ref):
        w = w_ref[...]
        o_ref[pl.ds(0, SUBLANES), :] = jnp.zeros((SUBLANES, cb), F32)
        for s, n in chunks:
            c = _conv_at(lambda a, b: x_ref[pl.ds(a, b), :], w, 4, s, n)
            o_ref[pl.ds(s, n), :] = c * _sigmoid(c)

    return pl.pallas_call(
        body, name=name, grid=(3 * per,),
        in_specs=[pl.BlockSpec((None, m, cb), lambda j: (3 + j // per, 0, j % per)), pl.BlockSpec((SUBLANES, cb), lambda j: (0, j))],
        out_specs=pl.BlockSpec((None, m, cb), lambda j: (j // per, 0, j % per)),
        out_shape=jax.ShapeDtypeStruct((3, m, seg), F32),
        compiler_params=_params(("parallel",)),
    )(pm, cw)


def _dnpre_bwd(dqkv, pm, cw, dpm, *, name):
    _, m, seg = pm.shape
    cb = _pick(seg, (256, 128))
    per = seg // cb
    chunks = _row_chunks(m)

    def body(d_ref, x_ref, w_ref, dpm_in, dp_ref, dw_ref, dc_s):
        del dpm_in
        w = w_ref[...]
        zeros8 = jnp.zeros((SUBLANES, cb), F32)
        dc_s[pl.ds(0, SUBLANES), :] = zeros8
        dc_s[pl.ds(m, SUBLANES), :] = zeros8
        dp_ref[pl.ds(0, SUBLANES), :] = zeros8
        load_x = lambda a, b: x_ref[pl.ds(a, b), :]
        dw = jnp.zeros((SUBLANES, cb), F32)
        for s, n in chunks:
            c = _conv_at(load_x, w, 4, s, n)
            sg = _sigmoid(c)
            dc = d_ref[pl.ds(s, n), :] * (sg * (1.0 + c * (1.0 - sg)))
            dc_s[pl.ds(s, n), :] = dc
            dw = dw + _dconv_w(load_x, dc, 4, s, n)
        dw_ref[...] = dw
        for s, n in chunks:
            dp_ref[pl.ds(s, n), :] = _conv_t_at(lambda a, b: dc_s[pl.ds(a, b), :], w, 4, s, n)

    return pl.pallas_call(
        body, name=name, grid=(3 * per,),
        in_specs=[pl.BlockSpec((None, m, cb), lambda j: (j // per, 0, j % per)),
                  pl.BlockSpec((None, m, cb), lambda j: (3 + j // per, 0, j % per)),
                  pl.BlockSpec((SUBLANES, cb), lambda j: (0, j)), pl.BlockSpec(memory_space=pl.ANY)],
        out_specs=[pl.BlockSpec((None, m, cb), lambda j: (3 + j // per, 0, j % per)), pl.BlockSpec((SUBLANES, cb), lambda j: (0, j))],
        out_shape=[jax.ShapeDtypeStruct(dpm.shape, F32), jax.ShapeDtypeStruct((SUBLANES, 3 * seg), F32)],
        scratch_shapes=[pltpu.VMEM((m + SUBLANES, cb), F32)],
        input_output_aliases={3: 0},
        compiler_params=_params(("parallel",)),
    )(dqkv, pm, cw, dpm)


def _mxu_dot_impl(a, b, form):
    a = a.astype(MXU_DTYPE)
    b = b.astype(MXU_DTYPE)
    dims = {"nn": (((1,), (0,)), ((), ())), "nt": (((1,), (1,)), ((), ())), "tn": (((0,), (0,)), ((), ()))}[form]
    return lax.dot_general(a, b, dims, preferred_element_type=F32)


@functools.partial(jax.custom_vjp, nondiff_argnums=(2,))
def _mxu_dot(a, b, form):
    return _mxu_dot_impl(a, b, form)


def _mxu_dot_fwd(a, b, form):
    return _mxu_dot_impl(a, b, form), (a, b)


def _mxu_dot_bwd(form, saved, g):
    a, b = saved
    if form == "nn":
        return _mxu_dot_impl(g, b, "nt"), _mxu_dot_impl(a, g, "tn")
    if form == "nt":
        return _mxu_dot_impl(g, b, "nn"), _mxu_dot_impl(g, a, "tn")
    return _mxu_dot_impl(b, g, "nt"), _mxu_dot_impl(a, g, "nn")


_mxu_dot.defvjp(_mxu_dot_fwd, _mxu_dot_bwd)


def _hdot(a, b):
    return jnp.dot(a, b, precision=HIGHEST, preferred_element_type=F32)


def _softplus(x):
    return jnp.maximum(x, 0.0) + jnp.log(1.0 + jnp.exp(-jnp.abs(x)))


def _dn_chunk(qr, kr, v, z, braw, araw, alog, dtb, nw, state, valid):
    c = DN_CHUNK
    q = qr * lax.rsqrt(jnp.sum(qr * qr, -1, keepdims=True) + EPS) * (DN_HEAD_DIM ** -0.5)
    k = kr * lax.rsqrt(jnp.sum(kr * kr, -1, keepdims=True) + EPS)
    beta = _sigmoid(braw) * valid
    g = -jnp.exp(alog) * _softplus(araw + dtb) * valid
    row = lax.broadcasted_iota(jnp.int32, (c, c), 0)
    col = lax.broadcasted_iota(jnp.int32, (c, c), 1)
    incl = row >= col
    strict = row > col
    gb = jnp.broadcast_to(g, (c, c))
    dcol = _hdot(incl.astype(F32), gb)
    drow = _hdot(jnp.ones((c, c), F32), jnp.where(row <= col, gb, 0.0))
    dmask = jnp.where(incl, jnp.exp(jnp.where(incl, dcol - drow, 0.0)), 0.0)
    dec = dcol[:, :1]
    dlast = dcol[c - 1:c, :1]
    kk = _mxu_dot(k, k, "nt")
    a = jnp.where(strict, beta * kk * dmask, 0.0)
    x = jnp.where(row == col, 1.0, 0.0) - a
    p = _hdot(a, a)
    for it in range(5):
        x = x + _hdot(x, p)
        if it < 4:
            p = _hdot(p, p)
    u = _hdot(x, v * beta)
    w = _hdot(x, k * (beta * jnp.exp(dec)))
    qk = _mxu_dot(q, k, "nt") * dmask
    q_dec = q * jnp.exp(dec)
    k_dec = k * jnp.exp(dlast - dec)
    v_new = u - _mxu_dot(w, state, "nn")
    o = _mxu_dot(q_dec, state, "nn") + _mxu_dot(qk, v_new, "nn")
    new_state = state * jnp.exp(dlast) + _mxu_dot(k_dec, v_new, "tn")
    y = _rms(o, nw) * (z * _sigmoid(z))
    return y, new_state


def _dn_valid(n):
    row = n * DN_CHUNK + lax.broadcasted_iota(jnp.int32, (DN_CHUNK, 1), 0)
    return (row >= PAD_ROWS).astype(F32)


def _dn_in_specs(rev, nc):
    cn = (lambda n: nc - 1 - n) if rev else (lambda n: n)
    c, hd = DN_CHUNK, DN_HEAD_DIM
    return [
        pl.BlockSpec((None, c, hd), lambda h, n: (0, cn(n), h)),
        pl.BlockSpec((None, c, hd), lambda h, n: (1, cn(n), h)),
        pl.BlockSpec((None, c, hd), lambda h, n: (2, cn(n), h)),
        pl.BlockSpec((None, c, hd), lambda h, n: (6, cn(n), h)),
        pl.BlockSpec((None, 2, c, 1), lambda h, n: (h, 0, cn(n), 0)),
        pl.BlockSpec((None, SUBLANES, LANES), lambda h, n: (h, 0, 0)),
        pl.BlockSpec((1, hd), lambda h, n: (0, 0)),
    ]


def _delta_fwd(qkvc, pm, ba, hp, nw, *, name):
    _, m, _ = qkvc.shape
    nc = m // DN_CHUNK
    hd = DN_HEAD_DIM

    def body(q_ref, k_ref, v_ref, z_ref, ba_ref, hp_ref, nw_ref, y_ref, s_ref, state):
        n = pl.program_id(1)

        @pl.when(n == 0)
        def _():
            state[...] = jnp.zeros_like(state)

        s_ref[...] = state[...]
        hpv = hp_ref[...]
        y, new_state = _dn_chunk(q_ref[...], k_ref[...], v_ref[...], z_ref[...], ba_ref[0], ba_ref[1],
                                 hpv[0:1, 0:1], hpv[1:2, 0:1], nw_ref[...], state[...], _dn_valid(n))
        y_ref[...] = y.astype(y_ref.dtype)
        state[...] = new_state

    return pl.pallas_call(
        body, name=name, grid=(DN_HEADS, nc), in_specs=_dn_in_specs(False, nc),
        out_specs=[pl.BlockSpec((DN_CHUNK, hd), lambda h, n: (n, h)), pl.BlockSpec((None, None, hd, hd), lambda h, n: (h, n, 0, 0))],
        out_shape=[jax.ShapeDtypeStruct((m, DN_DIM), MXU_DTYPE), jax.ShapeDtypeStruct((DN_HEADS, nc, hd, hd), F32)],
        scratch_shapes=[pltpu.VMEM((hd, hd), F32)],
        compiler_params=_params(("parallel", "arbitrary")),
    )(qkvc, qkvc, qkvc, pm, ba, hp, nw)


def _delta_bwd(dy, qkvc, pm, ba, hp, nw, states, dpm, *, name):
    _, m, _ = qkvc.shape
    nc = m // DN_CHUNK
    hd, c = DN_HEAD_DIM, DN_CHUNK

    def body(q_ref, k_ref, v_ref, z_ref, ba_ref, hp_ref, nw_ref, s_ref, dy_ref, dpm_in,
             dz_ref, dqkv_ref, dba_ref, dhp_ref, dnw_ref, dstate):
        del dpm_in
        step = pl.program_id(1)
        n = nc - 1 - step

        @pl.when(step == 0)
        def _():
            dstate[...] = jnp.zeros_like(dstate)
            dhp_ref[...] = jnp.zeros_like(dhp_ref)
            dnw_ref[...] = jnp.zeros_like(dnw_ref)

        hpv = hp_ref[...]
        valid = _dn_valid(n)
        fn = lambda qr, kr, v, z, braw, araw, alog, dtb, nwv, st: _dn_chunk(qr, kr, v, z, braw, araw, alog, dtb, nwv, st, valid)
        _, vjp = jax.vjp(fn, q_ref[...], k_ref[...], v_ref[...], z_ref[...], ba_ref[0], ba_ref[1],
                         hpv[0:1, 0:1], hpv[1:2, 0:1], nw_ref[...], s_ref[...])
        dq, dk, dv, dz, dbr, dar, dalog, ddtb, dnw, dst = vjp((dy_ref[...], dstate[...]))
        dqkv_ref[0] = dq
        dqkv_ref[1] = dk
        dqkv_ref[2] = dv
        dz_ref[...] = dz
        dba_ref[0] = dbr
        dba_ref[1] = dar
        dstate[...] = dst
        dnw_ref[...] += dnw
        dhp_ref[...] += jnp.concatenate([jnp.broadcast_to(dalog, (1, LANES)), jnp.broadcast_to(ddtb, (1, LANES)),
                                         jnp.zeros((SUBLANES - 2, LANES), F32)], 0)

    rn = lambda n: nc - 1 - n
    in_specs = _dn_in_specs(True, nc) + [
        pl.BlockSpec((None, None, hd, hd), lambda h, n: (h, rn(n), 0, 0)),
        pl.BlockSpec((c, hd), lambda h, n: (rn(n), DN_HEADS + h)),
        pl.BlockSpec(memory_space=pl.ANY),
    ]
    out_specs = [
        pl.BlockSpec((None, c, hd), lambda h, n: (6, rn(n), h)),
        pl.BlockSpec((3, c, hd), lambda h, n: (0, rn(n), h)),
        pl.BlockSpec((None, 2, c, 1), lambda h, n: (h, 0, rn(n), 0)),
        pl.BlockSpec((None, SUBLANES, LANES), lambda h, n: (h, 0, 0)),
        pl.BlockSpec((None, 1, hd), lambda h, n: (h, 0, 0)),
    ]
    return pl.pallas_call(
        body, name=name, grid=(DN_HEADS, nc), in_specs=in_specs, out_specs=out_specs,
        out_shape=[jax.ShapeDtypeStruct(dpm.shape, F32), jax.ShapeDtypeStruct(qkvc.shape, F32),
                   jax.ShapeDtypeStruct(ba.shape, F32), jax.ShapeDtypeStruct(hp.shape, F32),
                   jax.ShapeDtypeStruct((DN_HEADS, 1, hd), F32)],
        scratch_shapes=[pltpu.VMEM((hd, hd), F32)],
        input_output_aliases={9: 0},
        compiler_params=_params(("parallel", "arbitrary")),
    )(qkvc, qkvc, qkvc, pm, ba, hp, nw, states, dy, dpm)


def _attn_block(q4, k0, kp, kc, v0, vp, vc, qw, kw, sink, n):
    g, b, hd = SWA_GROUP, SWA_BLOCK, SWA_HEAD_DIM
    qn = _rms(q4, qw) * (hd ** -0.5)
    kn = _rms(jnp.concatenate([k0, kp, kc], 0), kw)
    vcat = jnp.concatenate([v0, vp, vc], 0)
    s = _mxu_dot(qn.reshape(g * b, hd), kn, "nt").reshape(g, b, 3 * b)
    i = lax.broadcasted_iota(jnp.int32, (b, 3 * b), 0)
    c = lax.broadcasted_iota(jnp.int32, (b, 3 * b), 1)
    in_meta, in_prev, in_cur = c < b, (c >= b) & (c < 2 * b), c >= 2 * b
    j = c - jnp.where(in_meta, 0, jnp.where(in_prev, b, 2 * b))
    meta_lo = jnp.where(n == 0, b, PAD_ROWS)
    cur_lo = jnp.where(n == 0, PAD_ROWS, 0)
    prev_off = jnp.where(n >= 2, 0, 2 * b)
    valid = (in_meta & (j >= meta_lo)) | (in_prev & (j > i + prev_off)) | (in_cur & (j <= i) & (j >= cur_lo))
    s = jnp.where(valid[None], s, NEG)
    m = lax.stop_gradient(jnp.maximum(jnp.max(s, -1, keepdims=True), sink))
    e = jnp.exp(s - m)
    p = e / (jnp.sum(e, -1, keepdims=True) + jnp.exp(sink - m))
    return _mxu_dot(p.reshape(g * b, 3 * b), vcat, "nn").reshape(g, b, hd)


def _attn_in_specs():
    g, b, hd = SWA_GROUP, SWA_BLOCK, SWA_HEAD_DIM
    kv = lambda f: pl.BlockSpec((None, b, hd), lambda kh, n: (kh, f(n), 0))
    blocks = [lambda n: 0, lambda n: jnp.maximum(n - 1, 0), lambda n: n]
    return ([pl.BlockSpec((g, b, hd), lambda kh, n: (kh, n, 0))] + [kv(f) for f in blocks] + [kv(f) for f in blocks]
            + [pl.BlockSpec((1, hd), lambda kh, n: (0, 0)), pl.BlockSpec((1, hd), lambda kh, n: (0, 0)),
               pl.BlockSpec((None, g, 1, 1), lambda kh, n: (kh, 0, 0, 0))])


def _attn_fwd(q3, k3, v3, qw, kw, sink, *, name):
    _, m, hd = q3.shape
    g, b = SWA_GROUP, SWA_BLOCK

    def body(q_ref, k0, kp, kc, v0, vp, vc, qw_ref, kw_ref, s_ref, o_ref):
        o_ref[...] = _attn_block(q_ref[...], k0[...], kp[...], kc[...], v0[...], vp[...], vc[...], qw_ref[...], kw_ref[...],
                                 s_ref[...], pl.program_id(1))

    return pl.pallas_call(
        body, name=name, grid=(SWA_KV_HEADS, m // b), in_specs=_attn_in_specs(),
        out_specs=pl.BlockSpec((g, b, hd), lambda kh, n: (kh, n, 0)),
        out_shape=jax.ShapeDtypeStruct(q3.shape, F32),
        compiler_params=_params(("parallel", "parallel")),
    )(q3, k3, k3, k3, v3, v3, v3, qw, kw, sink)


def _attn_bwd(do3, q3, k3, v3, qw, kw, sink, *, name):
    _, m, hd = q3.shape
    g, b = SWA_GROUP, SWA_BLOCK

    def body(q_ref, k0, kp, kc, v0, vp, vc, qw_ref, kw_ref, s_ref, do_ref, dq_ref, dk_ref, dv_ref, dqw_ref, dkw_ref, ds_ref):
        n = pl.program_id(1)

        @pl.when(n == 0)
        def _():
            for r in (dk_ref, dv_ref, dqw_ref, dkw_ref, ds_ref):
                r[...] = jnp.zeros_like(r)

        fn = lambda *a: _attn_block(*a, n)
        _, vjp = jax.vjp(fn, q_ref[...], k0[...], kp[...], kc[...], v0[...], vp[...], vc[...], qw_ref[...], kw_ref[...], s_ref[...])
        dq, dk0, dkp, dkc, dv0, dvp, dvc, dqw, dkw, dsk = vjp(do_ref[...])
        dq_ref[...] = dq
        prev = pl.multiple_of(jnp.maximum(n - 1, 0) * b, b)
        cur = pl.multiple_of(n * b, b)
        for ref, parts in ((dk_ref, (dk0, dkp, dkc)), (dv_ref, (dv0, dvp, dvc))):
            ref[pl.ds(0, b), :] += parts[0]
            ref[pl.ds(prev, b), :] += parts[1]
            ref[pl.ds(cur, b), :] += parts[2]
        dqw_ref[...] += dqw
        dkw_ref[...] += dkw
        ds_ref[...] += dsk

    acc = lambda shape: pl.BlockSpec((None,) + shape, lambda kh, n: (kh,) + (0,) * len(shape))
    return pl.pallas_call(
        body, name=name, grid=(SWA_KV_HEADS, m // b),
        in_specs=_attn_in_specs() + [pl.BlockSpec((g, b, hd), lambda kh, n: (kh, n, 0))],
        out_specs=[pl.BlockSpec((g, b, hd), lambda kh, n: (kh, n, 0)), acc((m, hd)), acc((m, hd)), acc((1, hd)), acc((1, hd)),
                   acc((g, 1, 1))],
        out_shape=[jax.ShapeDtypeStruct(q3.shape, F32), jax.ShapeDtypeStruct(k3.shape, F32), jax.ShapeDtypeStruct(v3.shape, F32),
                   jax.ShapeDtypeStruct((SWA_KV_HEADS, 1, hd), F32), jax.ShapeDtypeStruct((SWA_KV_HEADS, 1, hd), F32),
                   jax.ShapeDtypeStruct(sink.shape, F32)],
        compiler_params=_params(("parallel", "arbitrary")),
    )(q3, k3, k3, k3, v3, v3, v3, qw, kw, sink, do3)


def _loss_bwd(h, target, *, name):
    m, d = h.shape
    b = SWA_BLOCK

    def body(h_ref, t_ref, l_ref, dh_ref):
        i = pl.program_id(0)

        @pl.when(i == 0)
        def _():
            l_ref[...] = jnp.zeros_like(l_ref)
            dh_ref[...] = jnp.zeros_like(dh_ref)

        @pl.when(i > 0)
        def _():
            e = h_ref[...] - t_ref[...]
            dh_ref[...] = e * (1.0 / d)
            l_ref[...] += jnp.sum(jnp.sum(e * e, 0, keepdims=True), 1, keepdims=True) * (0.5 / d)

    return pl.pallas_call(
        body, name=name, grid=(m // b,),
        in_specs=[pl.BlockSpec((b, d), lambda i: (i, 0)), pl.BlockSpec((b, d), lambda i: (jnp.maximum(i - 1, 0), 0))],
        out_specs=[pl.BlockSpec((1, LANES), lambda i: (0, 0)), pl.BlockSpec((b, d), lambda i: (i, 0))],
        out_shape=[jax.ShapeDtypeStruct((1, LANES), F32), jax.ShapeDtypeStruct((m, d), F32)],
        compiler_params=_params(("arbitrary",)),
    )(h, target)


def _heads_major(a, heads):
    m = a.shape[0]
    return a.reshape(m, heads, SWA_HEAD_DIM).transpose(1, 0, 2)


def _heads_minor(a3):
    heads, m, hd = a3.shape
    return a3.transpose(1, 0, 2).reshape(m, heads * hd)


def _ffn_fwd(h, nw, w_up, cw, w_down, tag):
    u, hn = _norm_matmul(h, nw, w_up, o_seg=D_FF, name=f"ffn_up_{tag}")
    a = _ffn_act_fwd(u, cw, name=f"ffn_act_{tag}")
    return _mm_nn(a, w_down, res=h, name=f"ffn_down_{tag}"), (h, hn, u, a)


def _ffn_bwd(dh, saved, nw, w_up_t, cw, w_down_t, tag):
    h, hn, u, a = saved
    da = _mm_nn(dh, w_down_t, name=f"ffn_da_{tag}")
    dw_down = _mm_tn(a, dh, name=f"ffn_dwdown_{tag}")
    du, dcw = _ffn_act_bwd(da, u, cw, name=f"ffn_act_bwd_{tag}")
    dhn = _mm_nn(du, w_up_t, a_seg=True, name=f"ffn_dhn_{tag}")
    dw_up = _mm_tn(hn, du, b_seg=True, name=f"ffn_dwup_{tag}")
    dh_in, dnw = _rmsnorm_bwd(dhn, h, nw, dh, name=f"ffn_norm_bwd_{tag}")
    return dh_in, dnw, dw_up, dcw, dw_down


def _local_step(x, target, w):
    seq, d = x.shape
    m = PAD_ROWS + N_META + seq
    h0 = jnp.concatenate([jnp.zeros((PAD_ROWS, d), F32), w["meta"], x], 0)

    pm, hn0 = _norm_matmul(h0, w["anw"][0], w["w_in_main"], o_seg=SEG, name="mix_in")
    pba = _mm_nn(hn0, w["w_in_tail"], name="mix_in_tail")
    ya = _shortconv_fwd(pm, w["caw"], name="shortconv")
    qkvc = _dnpre_fwd(pm, w["dcw"], name="dn_conv")
    ba = pba[:, :2 * DN_HEADS].T.reshape(2, DN_HEADS, m, 1).transpose(1, 0, 2, 3)
    yb, states = _delta_fwd(qkvc, pm, ba, w["hp"], w["dnw"], name="delta")
    y = jnp.concatenate([ya, yb], 1)
    h1 = _mm_nn(y, w["w_out"], res=h0, name="mix_out")
    h2, ffn0 = _ffn_fwd(h1, w["fnw"][0], w["w_up"][0], w["fcw"][0], w["w_down"][0], "l0")

    qkv, hn2 = _norm_matmul(h2, w["anw"][1], w["wqkv"], name="attn_qkv")
    nq, nkv = SWA_HEADS * SWA_HEAD_DIM, SWA_KV_HEADS * SWA_HEAD_DIM
    q3 = _heads_major(qkv[:, :nq], SWA_HEADS)
    k3 = _heads_major(qkv[:, nq:nq + nkv], SWA_KV_HEADS)
    v3 = _heads_major(qkv[:, nq + nkv:], SWA_KV_HEADS)
    o3 = _attn_fwd(q3, k3, v3, w["qnw"], w["knw"], w["sink"], name="attn")
    o = _heads_minor(o3).astype(MXU_DTYPE)
    h3 = _mm_nn(o, w["wo"], res=h2, name="attn_out")
    h4, ffn1 = _ffn_fwd(h3, w["fnw"][1], w["w_up"][1], w["fcw"][1], w["w_down"][1], "l1")

    loss, dh4 = _loss_bwd(h4, target, name="loss")

    g = {}
    dh3, dfnw1, dwup1, dfcw1, dwdown1 = _ffn_bwd(dh4, ffn1, w["fnw"][1], w["w_up_t"][1], w["fcw"][1], w["w_down_t"][1], "l1")

    do = _mm_nn(dh3, w["wo_t"], name="attn_do")
    g["wo"] = _mm_tn(o, dh3, name="attn_dwo")
    dq3, dk3, dv3, dqw, dkw, dsink = _attn_bwd(_heads_major(do, SWA_HEADS), q3, k3, v3, w["qnw"], w["knw"], w["sink"], name="attn_bwd")
    dqkv = jnp.concatenate([_heads_minor(dq3), _heads_minor(dk3), _heads_minor(dv3)], 1)
    dhn2 = _mm_nn(dqkv, w["wqkv_t"], name="attn_dhn")
    g["wqkv"] = _mm_tn(hn2, dqkv, name="attn_dwqkv")
    dh2, danw1 = _rmsnorm_bwd(dhn2, h2, w["anw"][1], dh3, name="attn_norm_bwd")

    dh1, dfnw0, dwup0, dfcw0, dwdown0 = _ffn_bwd(dh2, ffn0, w["fnw"][0], w["w_up_t"][0], w["fcw"][0], w["w_down_t"][0], "l0")

    dy = _mm_nn(dh1, w["w_out_t"], name="mix_dy")
    g["w_out"] = _mm_tn(y, dh1, name="mix_dwout")
    dpm = jnp.zeros(pm.shape, F32)
    dpm, dqkvc, dba, dhp, ddnw = _delta_bwd(dy, qkvc, pm, ba, w["hp"], w["dnw"], states, dpm, name="delta_bwd")
    dpm, ddcw = _dnpre_bwd(dqkvc, pm, w["dcw"], dpm, name="dn_conv_bwd")
    dpm, dcaw = _shortconv_bwd(dy, pm, w["caw"], dpm, name="shortconv_bwd")
    dpba = jnp.pad(dba.transpose(1, 0, 2, 3).reshape(2 * DN_HEADS, m).T, ((0, 0), (0, LANES - 2 * DN_HEADS)))
    dhn0 = _mm_nn(dpm, w["w_in_main_t"], a_seg=True, name="mix_dhn")
    dhn0 = _mm_nn(dpba, w["w_in_tail_t"], res=dhn0, name="mix_dhn_tail")
    g["w_in_main"] = _mm_tn(hn0, dpm, b_seg=True, name="mix_dwin")
    g["w_in_tail"] = _mm_tn(hn0, dpba, name="mix_dwin_tail")
    dh0, danw0 = _rmsnorm_bwd(dhn0, h0, w["anw"][0], dh1, name="mix_norm_bwd")

    g.update(
        x=dh0[PAD_ROWS + N_META:], meta=dh0[PAD_ROWS:PAD_ROWS + N_META], anw=[danw0, danw1], fnw=[dfnw0, dfnw1],
        caw=dcaw, dcw=ddcw, hp=dhp, dnw=jnp.sum(ddnw, 0), qnw=jnp.sum(dqw, 0), knw=jnp.sum(dkw, 0), sink=dsink,
        w_up=[dwup0, dwup1], fcw=[dfcw0, dfcw1], w_down=[dwdown0, dwdown1])
    return loss, g


N_TAIL = 2 * DN_HEADS


def _prepare_weights(p):
    n_main = N_SEG * SEG
    w_in = p["mix_w_in"][0]
    tail = jnp.pad(w_in[:, n_main:], ((0, 0), (0, LANES - N_TAIL)))
    wqkv = jnp.concatenate([p["swa_wq"][0], p["swa_wk"][0], p["swa_wv"][0]], 1)
    hp = jnp.zeros((DN_HEADS, SUBLANES, LANES), F32)
    hp = hp.at[:, 0, :].set(p["dn_a_log"][0][:, None]).at[:, 1, :].set(p["dn_dt_bias"][0][:, None])
    depth = p["ffn_w_up"].shape[0]
    return dict(
        meta=p["meta_tokens"], anw=[p["attn_norm_w"][i:i + 1] for i in range(depth)],
        fnw=[p["ffn_norm_w"][i:i + 1] for i in range(depth)],
        w_in_main=w_in[:, :n_main], w_in_tail=tail, w_in_main_t=w_in[:, :n_main].T, w_in_tail_t=tail.T,
        caw=_pad_w(p["conv_a_w"][0]), dcw=_pad_w(p["dn_conv_w"][0]), hp=hp, dnw=p["dn_norm_w"],
        w_out=p["mix_w_out"][0], w_out_t=p["mix_w_out"][0].T, wqkv=wqkv, wqkv_t=wqkv.T,
        qnw=p["swa_q_norm_w"], knw=p["swa_k_norm_w"], sink=p["swa_sinks"].reshape(SWA_KV_HEADS, SWA_GROUP, 1, 1),
        wo=p["swa_wo"][0], wo_t=p["swa_wo"][0].T,
        w_up=[p["ffn_w_up"][i] for i in range(depth)], w_up_t=[p["ffn_w_up"][i].T for i in range(depth)],
        fcw=[_pad_w(p["ffn_conv_w"][i]) for i in range(depth)],
        w_down=[p["ffn_w_down"][i] for i in range(depth)], w_down_t=[p["ffn_w_down"][i].T for i in range(depth)])


def _reference_named(g):
    nq, nkv = SWA_HEADS * SWA_HEAD_DIM, SWA_KV_HEADS * SWA_HEAD_DIM
    return dict(
        meta_tokens=g["meta"], attn_norm_w=jnp.concatenate(g["anw"], 0), ffn_norm_w=jnp.concatenate(g["fnw"], 0),
        mix_w_in=jnp.concatenate([g["w_in_main"], g["w_in_tail"][:, :N_TAIL]], 1)[None],
        conv_a_w=g["caw"][None, :3], dn_conv_w=g["dcw"][None, :4],
        dn_a_log=g["hp"][None, :, 0, 0], dn_dt_bias=g["hp"][None, :, 1, 0], dn_norm_w=g["dnw"],
        mix_w_out=g["w_out"][None], swa_wq=g["wqkv"][None, :, :nq], swa_wk=g["wqkv"][None, :, nq:nq + nkv],
        swa_wv=g["wqkv"][None, :, nq + nkv:], swa_q_norm_w=g["qnw"], swa_k_norm_w=g["knw"],
        swa_sinks=g["sink"].reshape(1, SWA_HEADS), swa_wo=g["wo"][None],
        ffn_w_up=jnp.stack(g["w_up"]), ffn_conv_w=jnp.stack([c[:3] for c in g["fcw"]]), ffn_w_down=jnp.stack(g["w_down"]))


def _my_index():
    return 4 * lax.axis_index("x") + 2 * lax.axis_index("y") + lax.axis_index("c")


def _exchange(arrays, modes, *, name):
    n = len(arrays)

    def body(*refs):
        ins, outs = refs[:n], refs[n:2 * n]
        send_sems, recv_sems, local_sems = refs[2 * n:]
        x, y, c = lax.axis_index("x"), lax.axis_index("y"), lax.axis_index("c")
        me = 4 * x + 2 * y + c
        copies = []
        for i in range(n):
            own = ins[i] if modes[i] == "gather" else ins[i].at[me]
            local = pltpu.make_async_copy(own, outs[i].at[me], local_sems.at[i])
            local.start()
            copies.append(local)
        for d in range(1, N_DEV):
            px, py, pc = x ^ (d >> 2), y ^ ((d >> 1) & 1), c ^ (d & 1)
            peer = 4 * px + 2 * py + pc
            for i in range(n):
                src = ins[i] if modes[i] == "gather" else ins[i].at[peer]
                remote = pltpu.make_async_remote_copy(
                    src_ref=src, dst_ref=outs[i].at[me], send_sem=send_sems.at[i, d], recv_sem=recv_sems.at[i, d],
                    device_id=(px, py, pc), device_id_type=pl.DeviceIdType.MESH)
                remote.start()
                copies.append(remote)
        for cp in copies:
            cp.wait()

    def out_struct(a, mode):
        block = a.shape if mode == "gather" else a.shape[1:]
        return jax.ShapeDtypeStruct((N_DEV,) + tuple(block), a.dtype)

    hbm = pl.BlockSpec(memory_space=pl.ANY)
    return pl.pallas_call(
        body, name=name, in_specs=[hbm] * n, out_specs=[hbm] * n,
        out_shape=[out_struct(a, md) for a, md in zip(arrays, modes)],
        scratch_shapes=[pltpu.SemaphoreType.DMA((n, N_DEV)), pltpu.SemaphoreType.DMA((n, N_DEV)), pltpu.SemaphoreType.DMA((n,))],
    )(*arrays)


def _row_tile(r, c):
    best = None
    for t in range(SUBLANES, r + 1, SUBLANES):
        if r % t == 0 and N_DEV * t * c * 4 <= 6 * 1024 * 1024:
            best = t
    return best or r


def _adamw(parts, w, m, v, *, name):
    r, c = w.shape
    tr = _row_tile(r, c)

    def body(p_ref, w_ref, m_ref, v_ref, g_ref, d_ref, nm_ref, nv_ref):
        g = p_ref[0]
        for j in range(1, N_DEV):
            g = g + p_ref[j]
        m2 = ADAM_B1 * m_ref[...] + (1.0 - ADAM_B1) * g
        v2 = ADAM_B2 * v_ref[...] + (1.0 - ADAM_B2) * jnp.square(g)
        m_hat = m2 / (1.0 - ADAM_B1 ** ADAM_STEP)
        v_hat = v2 / (1.0 - ADAM_B2 ** ADAM_STEP)
        g_ref[...] = g
        d_ref[...] = -ADAM_LR * (m_hat / (jnp.sqrt(v_hat) + ADAM_EPS) + ADAM_WD * w_ref[...])
        nm_ref[...] = m2
        nv_ref[...] = v2

    blk = pl.BlockSpec((tr, c), lambda i: (i, 0))
    out = jax.ShapeDtypeStruct((r, c), F32)
    return pl.pallas_call(
        body, name=name, grid=(r // tr,),
        in_specs=[pl.BlockSpec((N_DEV, tr, c), lambda i: (0, i, 0)), blk, blk, blk],
        out_specs=[blk, blk, blk, blk], out_shape=[out, out, out, out],
        compiler_params=_params(("parallel",)),
    )(parts, w, m, v)


SHARD_AXIS = dict(
    meta_tokens=1, attn_norm_w=None, ffn_norm_w=None, mix_w_in=2, conv_a_w=2, dn_conv_w=2, dn_a_log=None, dn_dt_bias=None,
    dn_norm_w=None, mix_w_out=1, swa_wq=1, swa_wk=1, swa_wv=1, swa_q_norm_w=None, swa_k_norm_w=None, swa_sinks=None,
    swa_wo=1, ffn_w_up=2, ffn_conv_w=2, ffn_w_down=1)
WEIGHTS = list(SHARD_AXIS)
BIG = ["mix_w_in", "mix_w_out", "swa_wq", "swa_wk", "swa_wv", "swa_wo", "ffn_w_up", "ffn_w_down"]
SMALL = [k for k in WEIGHTS if k not in BIG]
SMALL_SHARDED = [k for k in SMALL if SHARD_AXIS[k] is not None]


def _whole(g8, axis):
    t = jnp.moveaxis(g8, 0, axis)
    return t.reshape(t.shape[:axis] + (t.shape[axis] * t.shape[axis + 1],) + t.shape[axis + 2:])


def _by_owner(a, axis):
    s = a.shape[axis] // N_DEV
    return jnp.moveaxis(a.reshape(a.shape[:axis] + (N_DEV, s) + a.shape[axis + 1:]), axis, 0)


def _pack(arrays, lead=0):
    flat = jnp.concatenate([a.reshape(a.shape[:lead] + (-1,)) for a in arrays], -1)
    n = flat.shape[-1]
    rows = -(-n // (SUBLANES * LANES)) * SUBLANES
    flat = jnp.pad(flat, [(0, 0)] * lead + [(0, rows * LANES - n)])
    return flat.reshape(flat.shape[:lead] + (rows, LANES))


def _unpack(buf, shapes, lead=0):
    flat = buf.reshape(buf.shape[:lead] + (-1,))
    out, o = [], 0
    for s in shapes:
        n = 1
        for e in s:
            n *= e
        out.append(flat[..., o:o + n].reshape(buf.shape[:lead] + tuple(s)))
        o += n
    return out


def kernel(x, meta_tokens, attn_norm_w, ffn_norm_w, mix_w_in, conv_a_w, dn_conv_w, dn_a_log, dn_dt_bias, dn_norm_w, mix_w_out, swa_wq, swa_wk, swa_wv, swa_q_norm_w, swa_k_norm_w, swa_sinks, swa_wo, ffn_w_up, ffn_conv_w, ffn_w_down, loss_target, m_meta_tokens, m_attn_norm_w, m_ffn_norm_w, m_mix_w_in, m_conv_a_w, m_dn_conv_w, m_dn_a_log, m_dn_dt_bias, m_dn_norm_w, m_mix_w_out, m_swa_wq, m_swa_wk, m_swa_wv, m_swa_q_norm_w, m_swa_k_norm_w, m_swa_sinks, m_swa_wo, m_ffn_w_up, m_ffn_conv_w, m_ffn_w_down, v_meta_tokens, v_attn_norm_w, v_ffn_norm_w, v_mix_w_in, v_conv_a_w, v_dn_conv_w, v_dn_a_log, v_dn_dt_bias, v_dn_norm_w, v_mix_w_out, v_swa_wq, v_swa_wk, v_swa_wv, v_swa_q_norm_w, v_swa_k_norm_w, v_swa_sinks, v_swa_wo, v_ffn_w_up, v_ffn_conv_w, v_ffn_w_down):
    w = dict(meta_tokens=meta_tokens, attn_norm_w=attn_norm_w, ffn_norm_w=ffn_norm_w, mix_w_in=mix_w_in, conv_a_w=conv_a_w, dn_conv_w=dn_conv_w, dn_a_log=dn_a_log, dn_dt_bias=dn_dt_bias, dn_norm_w=dn_norm_w, mix_w_out=mix_w_out, swa_wq=swa_wq, swa_wk=swa_wk, swa_wv=swa_wv, swa_q_norm_w=swa_q_norm_w, swa_k_norm_w=swa_k_norm_w, swa_sinks=swa_sinks, swa_wo=swa_wo, ffn_w_up=ffn_w_up, ffn_conv_w=ffn_conv_w, ffn_w_down=ffn_w_down)
    mom = dict(meta_tokens=m_meta_tokens, attn_norm_w=m_attn_norm_w, ffn_norm_w=m_ffn_norm_w, mix_w_in=m_mix_w_in, conv_a_w=m_conv_a_w, dn_conv_w=m_dn_conv_w, dn_a_log=m_dn_a_log, dn_dt_bias=m_dn_dt_bias, dn_norm_w=m_dn_norm_w, mix_w_out=m_mix_w_out, swa_wq=m_swa_wq, swa_wk=m_swa_wk, swa_wv=m_swa_wv, swa_q_norm_w=m_swa_q_norm_w, swa_k_norm_w=m_swa_k_norm_w, swa_sinks=m_swa_sinks, swa_wo=m_swa_wo, ffn_w_up=m_ffn_w_up, ffn_conv_w=m_ffn_conv_w, ffn_w_down=m_ffn_w_down)
    var = dict(meta_tokens=v_meta_tokens, attn_norm_w=v_attn_norm_w, ffn_norm_w=v_ffn_norm_w, mix_w_in=v_mix_w_in, conv_a_w=v_conv_a_w, dn_conv_w=v_dn_conv_w, dn_a_log=v_dn_a_log, dn_dt_bias=v_dn_dt_bias, dn_norm_w=v_dn_norm_w, mix_w_out=v_mix_w_out, swa_wq=v_swa_wq, swa_wk=v_swa_wk, swa_wv=v_swa_wv, swa_q_norm_w=v_swa_q_norm_w, swa_k_norm_w=v_swa_k_norm_w, swa_sinks=v_swa_sinks, swa_wo=v_swa_wo, ffn_w_up=v_ffn_w_up, ffn_conv_w=v_ffn_conv_w, ffn_w_down=v_ffn_w_down)
    me = _my_index()

    small_shard_shapes = [w[k].shape for k in SMALL_SHARDED]
    sent = [w[k].astype(MXU_DTYPE) for k in BIG] + [_pack([w[k] for k in SMALL_SHARDED])]
    got = _exchange(sent, ["gather"] * len(sent), name="gather_weights")
    whole = {k: _whole(a, SHARD_AXIS[k]) for k, a in zip(BIG, got[:-1])}
    for k, a in zip(SMALL_SHARDED, _unpack(got[-1], small_shard_shapes, lead=1)):
        whole[k] = _whole(a, SHARD_AXIS[k])
    for k in SMALL:
        whole.setdefault(k, w[k])

    loss, g = _local_step(x[0], loss_target[0], _prepare_weights(whole))
    grads = _reference_named(g)

    sent = [_by_owner(grads[k], SHARD_AXIS[k]) for k in BIG]
    got = _exchange(sent, ["scatter"] * len(sent), name="scatter_grads")
    out_g, out_d, out_m, out_v = {}, {}, {}, {}
    for k, parts in zip(BIG, got):
        shp = w[k].shape
        r, c = shp[0] * shp[1], shp[2]
        res = _adamw(parts.reshape(N_DEV, r, c), w[k].reshape(r, c), mom[k].reshape(r, c), var[k].reshape(r, c), name=f"adamw_{k}")
        out_g[k], out_d[k], out_m[k], out_v[k] = [t.reshape(shp) for t in res]

    small_shapes = [grads[k].shape for k in SMALL]
    (all_small,) = _exchange([_pack([loss] + [grads[k] for k in SMALL])], ["gather"], name="gather_small_grads")
    loss_parts, *small_parts = _unpack(all_small, [loss.shape] + small_shapes, lead=1)
    mine = []
    for k, p in zip(SMALL, small_parts):
        ax = SHARD_AXIS[k]
        mine.append(p if ax is None else lax.dynamic_slice_in_dim(p, me * w[k].shape[ax], w[k].shape[ax], 1 + ax))
    zero = jnp.zeros(loss.shape, F32)
    packed = [_pack([z] + [d[k] for k in SMALL]) for z, d in ((zero, w), (zero, mom), (zero, var))]
    res = _adamw(_pack([loss_parts] + mine, lead=1), *packed, name="adamw_small")
    shapes = [loss.shape] + [w[k].shape for k in SMALL]
    for t, dst in zip(res, (out_g, out_d, out_m, out_v)):
        parts = _unpack(t, shapes)
        if dst is out_g:
            total_loss = parts[0][0, 0]
        for k, a in zip(SMALL, parts[1:]):
            dst[k] = a

    return (total_loss, g["x"][None], *[out_g[k] for k in WEIGHTS], *[out_d[k] for k in WEIGHTS],
            *[out_m[k] for k in WEIGHTS], *[out_v[k] for k in WEIGHTS])
```

```python
import functools

import jax
import jax.numpy as jnp
from jax import lax
from jax.experimental import pallas as pl
from jax.experimental.pallas import tpu as pltpu

F32 = jnp.float32
BF16 = jnp.bfloat16
MXU_DTYPE = BF16
GRAD_WIRE_DTYPE = BF16

D_MODEL = 1024
N_META = 16
PAD_ROWS = 112
D_CONV = 512
DN_HEADS = 4
DN_HEAD_DIM = 128
DN_DIM = DN_HEADS * DN_HEAD_DIM
DN_CHUNK = 64
SEG = 512
N_SEG = 7
SWA_HEADS = 16
SWA_KV_HEADS = 4
SWA_GROUP = SWA_HEADS // SWA_KV_HEADS
SWA_HEAD_DIM = 64
SWA_BLOCK = 128
D_FF = 2816
EPS = 1e-6
NEG = -1e30
N_DEV = 8

ADAM_LR = 0.001
ADAM_B1 = 0.9
ADAM_B2 = 0.999
ADAM_EPS = 1e-08
ADAM_WD = 0.01
ADAM_STEP = 10

VMEM_LIMIT_BYTES = 52 * 1024 * 1024
SUBLANES = 8
LANES = 128


def _pick(n, prefs):
    for p in prefs:
        if n % p == 0:
            return p
    return n


def _params(sem, vmem=VMEM_LIMIT_BYTES):
    return pltpu.CompilerParams(dimension_semantics=sem, vmem_limit_bytes=vmem)


def _rms(x, w):
    return x * lax.rsqrt(jnp.mean(x * x, -1, keepdims=True) + EPS) * w


def _norm_matmul(h, nw, w, *, o_seg=None, name):
    m, k = h.shape
    n = w.shape[1]
    tm = _pick(m, (1408, 384, 128))
    tn = _pick(o_seg or n, (512, 256, 128))

    def body(h_ref, nw_ref, w_ref, o_ref, hn_ref, hn_s):
        @pl.when(pl.program_id(1) == 0)
        def _():
            hn = _rms(h_ref[...], nw_ref[...]).astype(MXU_DTYPE)
            hn_s[...] = hn
            hn_ref[...] = hn

        o_ref[...] = jnp.dot(hn_s[...], w_ref[...], preferred_element_type=F32)

    if o_seg:
        per = o_seg // tn
        o_shape = jax.ShapeDtypeStruct((n // o_seg, m, o_seg), F32)
        o_spec = pl.BlockSpec((None, tm, tn), lambda i, j: (j // per, i, j % per))
    else:
        o_shape = jax.ShapeDtypeStruct((m, n), F32)
        o_spec = pl.BlockSpec((tm, tn), lambda i, j: (i, j))
    return pl.pallas_call(
        body, name=name, grid=(m // tm, n // tn),
        in_specs=[pl.BlockSpec((tm, k), lambda i, j: (i, 0)), pl.BlockSpec((1, k), lambda i, j: (0, 0)),
                  pl.BlockSpec((k, tn), lambda i, j: (0, j))],
        out_specs=[o_spec, pl.BlockSpec((tm, k), lambda i, j: (i, 0))],
        out_shape=[o_shape, jax.ShapeDtypeStruct((m, k), MXU_DTYPE)],
        scratch_shapes=[pltpu.VMEM((tm, k), MXU_DTYPE)],
        compiler_params=_params(("parallel", "arbitrary")),
    )(h, nw, w)


def _mm_nn(a, w, *, res=None, a_seg=False, name):
    if a_seg:
        s, m, seg = a.shape
        k = s * seg
    else:
        m, k = a.shape
        seg = k
    n = w.shape[1]
    tm = _pick(m, (1408, 384, 128))
    tn = _pick(n, (512, 256, 128))
    tk = _pick(seg, (1024, 1408, 512, 256, 128))
    nk = k // tk

    def body(*refs):
        if res is None:
            a_ref, w_ref, o_ref, acc = refs
        else:
            a_ref, w_ref, r_ref, o_ref, acc = refs
        kk = pl.program_id(2)

        @pl.when(kk == 0)
        def _():
            acc[...] = jnp.zeros_like(acc)

        acc[...] += jnp.dot(a_ref[...].astype(MXU_DTYPE), w_ref[...], preferred_element_type=F32)

        @pl.when(kk == nk - 1)
        def _():
            o_ref[...] = acc[...] if res is None else acc[...] + r_ref[...]

    if a_seg:
        per = seg // tk
        a_spec = pl.BlockSpec((None, tm, tk), lambda i, j, kk: (kk // per, i, kk % per))
    else:
        a_spec = pl.BlockSpec((tm, tk), lambda i, j, kk: (i, kk))
    in_specs = [a_spec, pl.BlockSpec((tk, tn), lambda i, j, kk: (kk, j))]
    args = [a, w]
    if res is not None:
        in_specs.append(pl.BlockSpec((tm, tn), lambda i, j, kk: (i, j)))
        args.append(res)
    return pl.pallas_call(
        body, name=name, grid=(m // tm, n // tn, nk), in_specs=in_specs,
        out_specs=pl.BlockSpec((tm, tn), lambda i, j, kk: (i, j)),
        out_shape=jax.ShapeDtypeStruct((m, n), F32),
        scratch_shapes=[pltpu.VMEM((tm, tn), F32)],
        compiler_params=_params(("parallel", "parallel", "arbitrary")),
    )(*args)


def _mm_tn(a, b, *, b_seg=False, out_dtype=None, name):
    out_dtype = out_dtype or GRAD_WIRE_DTYPE
    m, ka = a.shape
    if b_seg:
        s, _, seg = b.shape
        n = s * seg
    else:
        n = b.shape[1]
        seg = n
    tka = _pick(ka, (1024, 1408, 512, 256, 128))
    tn = _pick(seg, (512, 256, 128))
    tmc = _pick(m, (1408, 384, 128))
    nm = m // tmc

    def body(a_ref, b_ref, o_ref, acc):
        mm = pl.program_id(2)

        @pl.when(mm == 0)
        def _():
            acc[...] = jnp.zeros_like(acc)

        acc[...] += lax.dot_general(a_ref[...].astype(MXU_DTYPE), b_ref[...].astype(MXU_DTYPE),
                                    (((0,), (0,)), ((), ())), preferred_element_type=F32)

        @pl.when(mm == nm - 1)
        def _():
            o_ref[...] = acc[...].astype(o_ref.dtype)

    if b_seg:
        per = seg // tn
        b_spec = pl.BlockSpec((None, tmc, tn), lambda i, j, mm: (j // per, mm, j % per))
    else:
        b_spec = pl.BlockSpec((tmc, tn), lambda i, j, mm: (mm, j))
    return pl.pallas_call(
        body, name=name, grid=(ka // tka, n // tn, nm),
        in_specs=[pl.BlockSpec((tmc, tka), lambda i, j, mm: (mm, i)), b_spec],
        out_specs=pl.BlockSpec((tka, tn), lambda i, j, mm: (i, j)),
        out_shape=jax.ShapeDtypeStruct((ka, n), out_dtype),
        scratch_shapes=[pltpu.VMEM((tka, tn), F32)],
        compiler_params=_params(("parallel", "parallel", "arbitrary")),
    )(a, b)


def _rmsnorm_bwd(dhn, h, nw, dres, *, name):
    m, d = h.shape
    tm = _pick(m, (384, 128))

    def body(dhn_ref, h_ref, nw_ref, dres_ref, dh_ref, dnw_ref):
        i = pl.program_id(0)
        x = h_ref[...]
        r = lax.rsqrt(jnp.mean(x * x, -1, keepdims=True) + EPS)
        xh = x * r
        dy = dhn_ref[...]
        dxh = dy * nw_ref[...]
        dx = r * (dxh - xh * jnp.mean(dxh * xh, -1, keepdims=True))
        row = i * tm + lax.broadcasted_iota(jnp.int32, (tm, 1), 0)
        dh_ref[...] = jnp.where(row >= PAD_ROWS, dres_ref[...] + dx, 0.0)

        @pl.when(i == 0)
        def _():
            dnw_ref[...] = jnp.zeros_like(dnw_ref)

        dnw_ref[...] += jnp.sum(dy * xh, 0, keepdims=True)

    return pl.pallas_call(
        body, name=name, grid=(m // tm,),
        in_specs=[pl.BlockSpec((tm, d), lambda i: (i, 0)), pl.BlockSpec((tm, d), lambda i: (i, 0)),
                  pl.BlockSpec((1, d), lambda i: (0, 0)), pl.BlockSpec((tm, d), lambda i: (i, 0))],
        out_specs=[pl.BlockSpec((tm, d), lambda i: (i, 0)), pl.BlockSpec((1, d), lambda i: (0, 0))],
        out_shape=[jax.ShapeDtypeStruct((m, d), F32), jax.ShapeDtypeStruct((1, d), F32)],
        compiler_params=_params(("arbitrary",)),
    )(dhn, h, nw, dres)


ROW_CHUNK = 248


def _row_chunks(m):
    out, s = [], SUBLANES
    while s < m:
        n = min(ROW_CHUNK, m - s)
        out.append((s, n))
        s += n
    return out


def _conv_at(load, w, width, s, n):
    acc = w[width - 1:width, :] * load(s, n)
    for j in range(width - 1):
        acc = acc + w[j:j + 1, :] * load(s - (width - 1 - j), n)
    return acc


def _conv_t_at(load, w, width, s, n):
    acc = w[width - 1:width, :] * load(s, n)
    for j in range(width - 1):
        acc = acc + w[j:j + 1, :] * load(s + (width - 1 - j), n)
    return acc


def _dconv_w(load_x, d, width, s, n):
    rows = [jnp.sum(d * load_x(s - (width - 1 - j), n), 0, keepdims=True) for j in range(width)]
    rows.append(jnp.zeros((SUBLANES - width, d.shape[1]), F32))
    return jnp.concatenate(rows, 0)


def _pad_w(w):
    return jnp.concatenate([w, jnp.zeros((SUBLANES - w.shape[0], w.shape[1]), w.dtype)], 0)


def _sigmoid(x):
    return 1.0 / (1.0 + jnp.exp(-x))


def _ffn_act_fwd(u, cw, *, name):
    _, m, f = u.shape
    cb = _pick(f, (256, 128))
    chunks = _row_chunks(m)

    def body(g_ref, v_ref, w_ref, o_ref):
        w = w_ref[...]
        o_ref[pl.ds(0, SUBLANES), :] = jnp.zeros((SUBLANES, cb), o_ref.dtype)
        for s, n in chunks:
            c = _conv_at(lambda a, b: g_ref[pl.ds(a, b), :], w, 3, s, n)
            o_ref[pl.ds(s, n), :] = (c * _sigmoid(c) * v_ref[pl.ds(s, n), :]).astype(o_ref.dtype)

    return pl.pallas_call(
        body, name=name, grid=(f // cb,),
        in_specs=[pl.BlockSpec((None, m, cb), lambda j: (0, 0, j)), pl.BlockSpec((None, m, cb), lambda j: (1, 0, j)),
                  pl.BlockSpec((SUBLANES, cb), lambda j: (0, j))],
        out_specs=pl.BlockSpec((m, cb), lambda j: (0, j)),
        out_shape=jax.ShapeDtypeStruct((m, f), MXU_DTYPE),
        compiler_params=_params(("parallel",)),
    )(u, u, cw)


def _ffn_act_bwd(da, u, cw, *, name):
    _, m, f = u.shape
    cb = LANES
    chunks = _row_chunks(m)

    def body(da_ref, g_ref, v_ref, w_ref, du_ref, dw_ref, dg_s):
        w = w_ref[...]
        zeros8 = jnp.zeros((SUBLANES, cb), F32)
        dg_s[pl.ds(0, SUBLANES), :] = zeros8
        dg_s[pl.ds(m, SUBLANES), :] = zeros8
        du_ref[0, pl.ds(0, SUBLANES), :] = zeros8
        du_ref[1, pl.ds(0, SUBLANES), :] = zeros8
        load_g = lambda a, b: g_ref[pl.ds(a, b), :]
        dw = jnp.zeros((SUBLANES, cb), F32)
        for s, n in chunks:
            c = _conv_at(load_g, w, 3, s, n)
            sg = _sigmoid(c)
            d = da_ref[pl.ds(s, n), :]
            du_ref[1, pl.ds(s, n), :] = d * (c * sg)
            dc = d * v_ref[pl.ds(s, n), :] * (sg * (1.0 + c * (1.0 - sg)))
            dg_s[pl.ds(s, n), :] = dc
            dw = dw + _dconv_w(load_g, dc, 3, s, n)
        dw_ref[...] = dw
        for s, n in chunks:
            du_ref[0, pl.ds(s, n), :] = _conv_t_at(lambda a, b: dg_s[pl.ds(a, b), :], w, 3, s, n)

    return pl.pallas_call(
        body, name=name, grid=(f // cb,),
        in_specs=[pl.BlockSpec((m, cb), lambda j: (0, j)), pl.BlockSpec((None, m, cb), lambda j: (0, 0, j)),
                  pl.BlockSpec((None, m, cb), lambda j: (1, 0, j)), pl.BlockSpec((SUBLANES, cb), lambda j: (0, j))],
        out_specs=[pl.BlockSpec((2, m, cb), lambda j: (0, 0, j)), pl.BlockSpec((SUBLANES, cb), lambda j: (0, j))],
        out_shape=[jax.ShapeDtypeStruct((2, m, f), F32), jax.ShapeDtypeStruct((SUBLANES, f), F32)],
        scratch_shapes=[pltpu.VMEM((m + SUBLANES, cb), F32)],
        compiler_params=_params(("parallel",)),
    )(da, u, u, cw)


def _shortconv_fwd(pm, cw, *, name):
    _, m, seg = pm.shape
    cb = _pick(seg, (256, 128))
    chunks = _row_chunks(m)

    def body(gi_ref, go_ref, ah_ref, w_ref, o_ref):
        w = w_ref[...]
        o_ref[pl.ds(0, SUBLANES), :] = jnp.zeros((SUBLANES, cb), o_ref.dtype)
        load_m = lambda a, b: gi_ref[pl.ds(a, b), :] * ah_ref[pl.ds(a, b), :]
        for s, n in chunks:
            o_ref[pl.ds(s, n), :] = (go_ref[pl.ds(s, n), :] * _conv_at(load_m, w, 3, s, n)).astype(o_ref.dtype)

    return pl.pallas_call(
        body, name=name, grid=(seg // cb,),
        in_specs=[pl.BlockSpec((None, m, cb), lambda j: (0, 0, j)), pl.BlockSpec((None, m, cb), lambda j: (1, 0, j)),
                  pl.BlockSpec((None, m, cb), lambda j: (2, 0, j)), pl.BlockSpec((SUBLANES, cb), lambda j: (0, j))],
        out_specs=pl.BlockSpec((m, cb), lambda j: (0, j)),
        out_shape=jax.ShapeDtypeStruct((m, seg), MXU_DTYPE),
        compiler_params=_params(("parallel",)),
    )(pm, pm, pm, cw)


def _shortconv_bwd(dy, pm, cw, dpm, *, name):
    _, m, seg = pm.shape
    cb = LANES
    chunks = _row_chunks(m)

    def body(dy_ref, gi_ref, go_ref, ah_ref, w_ref, dpm_in, dp_ref, dw_ref, dc_s):
        del dpm_in
        w = w_ref[...]
        zeros8 = jnp.zeros((SUBLANES, cb), F32)
        dc_s[pl.ds(0, SUBLANES), :] = zeros8
        dc_s[pl.ds(m, SUBLANES), :] = zeros8
        for t in range(3):
            dp_ref[t, pl.ds(0, SUBLANES), :] = zeros8
        load_m = lambda a, b: gi_ref[pl.ds(a, b), :] * ah_ref[pl.ds(a, b), :]
        dw = jnp.zeros((SUBLANES, cb), F32)
        for s, n in chunks:
            d = dy_ref[pl.ds(s, n), :]
            dp_ref[1, pl.ds(s, n), :] = d * _conv_at(load_m, w, 3, s, n)
            dc = d * go_ref[pl.ds(s, n), :]
            dc_s[pl.ds(s, n), :] = dc
            dw = dw + _dconv_w(load_m, dc, 3, s, n)
        dw_ref[...] = dw
        for s, n in chunks:
            dm = _conv_t_at(lambda a, b: dc_s[pl.ds(a, b), :], w, 3, s, n)
            dp_ref[0, pl.ds(s, n), :] = dm * ah_ref[pl.ds(s, n), :]
            dp_ref[2, pl.ds(s, n), :] = dm * gi_ref[pl.ds(s, n), :]

    return pl.pallas_call(
        body, name=name, grid=(seg // cb,),
        in_specs=[pl.BlockSpec((m, cb), lambda j: (0, j)), pl.BlockSpec((None, m, cb), lambda j: (0, 0, j)),
                  pl.BlockSpec((None, m, cb), lambda j: (1, 0, j)), pl.BlockSpec((None, m, cb), lambda j: (2, 0, j)),
                  pl.BlockSpec((SUBLANES, cb), lambda j: (0, j)), pl.BlockSpec(memory_space=pl.ANY)],
        out_specs=[pl.BlockSpec((3, m, cb), lambda j: (0, 0, j)), pl.BlockSpec((SUBLANES, cb), lambda j: (0, j))],
        out_shape=[jax.ShapeDtypeStruct(dpm.shape, F32), jax.ShapeDtypeStruct((SUBLANES, seg), F32)],
        scratch_shapes=[pltpu.VMEM((m + SUBLANES, cb), F32)],
        input_output_aliases={5: 0},
        compiler_params=_params(("parallel",)),
    )(dy, pm, pm, pm, cw, dpm)


def _dnpre_fwd(pm, cw, *, name):
    _, m, seg = pm.shape
    cb = _pick(seg, (256, 128))
    per = seg // cb
    chunks = _row_chunks(m)

    def body(x_ref, w_ref, o_ref):
        w = w_ref[...]
        o_ref[pl.ds(0, SUBLANES), :] = jnp.zeros((SUBLANES, cb), F32)
        for s, n in chunks:
            c = _conv_at(lambda a, b: x_ref[pl.ds(a, b), :], w, 4, s, n)
            o_ref[pl.ds(s, n), :] = c * _sigmoid(c)

    return pl.pallas_call(
        body, name=name, grid=(3 * per,),
        in_specs=[pl.BlockSpec((None, m, cb), lambda j: (3 + j // per, 0, j % per)), pl.BlockSpec((SUBLANES, cb), lambda j: (0, j))],
        out_specs=pl.BlockSpec((None, m, cb), lambda j: (j // per, 0, j % per)),
        out_shape=jax.ShapeDtypeStruct((3, m, seg), F32),
        compiler_params=_params(("parallel",)),
    )(pm, cw)


def _dnpre_bwd(dqkv, pm, cw, dpm, *, name):
    _, m, seg = pm.shape
    cb = _pick(seg, (256, 128))
    per = seg // cb
    chunks = _row_chunks(m)

    def body(d_ref, x_ref, w_ref, dpm_in, dp_ref, dw_ref, dc_s):
        del dpm_in
        w = w_ref[...]
        zeros8 = jnp.zeros((SUBLANES, cb), F32)
        dc_s[pl.ds(0, SUBLANES), :] = zeros8
        dc_s[pl.ds(m, SUBLANES), :] = zeros8
        dp_ref[pl.ds(0, SUBLANES), :] = zeros8
        load_x = lambda a, b: x_ref[pl.ds(a, b), :]
        dw = jnp.zeros((SUBLANES, cb), F32)
        for s, n in chunks:
            c = _conv_at(load_x, w, 4, s, n)
            sg = _sigmoid(c)
            dc = d_ref[pl.ds(s, n), :] * (sg * (1.0 + c * (1.0 - sg)))
            dc_s[pl.ds(s, n), :] = dc
            dw = dw + _dconv_w(load_x, dc, 4, s, n)
        dw_ref[...] = dw
        for s, n in chunks:
            dp_ref[pl.ds(s, n), :] = _conv_t_at(lambda a, b: dc_s[pl.ds(a, b), :], w, 4, s, n)

    return pl.pallas_call(
        body, name=name, grid=(3 * per,),
        in_specs=[pl.BlockSpec((None, m, cb), lambda j: (j // per, 0, j % per)),
                  pl.BlockSpec((None, m, cb), lambda j: (3 + j // per, 0, j % per)),
                  pl.BlockSpec((SUBLANES, cb), lambda j: (0, j)), pl.BlockSpec(memory_space=pl.ANY)],
        out_specs=[pl.BlockSpec((None, m, cb), lambda j: (3 + j // per, 0, j % per)), pl.BlockSpec((SUBLANES, cb), lambda j: (0, j))],
        out_shape=[jax.ShapeDtypeStruct(dpm.shape, F32), jax.ShapeDtypeStruct((SUBLANES, 3 * seg), F32)],
        scratch_shapes=[pltpu.VMEM((m + SUBLANES, cb), F32)],
        input_output_aliases={3: 0},
        compiler_params=_params(("parallel",)),
    )(dqkv, pm, cw, dpm)


def _mxu_dot_impl(a, b, form):
    a = a.astype(MXU_DTYPE)
    b = b.astype(MXU_DTYPE)
    dims = {"nn": (((1,), (0,)), ((), ())), "nt": (((1,), (1,)), ((), ())), "tn": (((0,), (0,)), ((), ()))}[form]
    return lax.dot_general(a, b, dims, preferred_element_type=F32)


@functools.partial(jax.custom_vjp, nondiff_argnums=(2,))
def _mxu_dot(a, b, form):
    return _mxu_dot_impl(a, b, form)


def _mxu_dot_fwd(a, b, form):
    return _mxu_dot_impl(a, b, form), (a, b)


def _mxu_dot_bwd(form, saved, g):
    a, b = saved
    if form == "nn":
        return _mxu_dot_impl(g, b, "nt"), _mxu_dot_impl(a, g, "tn")
    if form == "nt":
        return _mxu_dot_impl(g, b, "nn"), _mxu_dot_impl(g, a, "tn")
    return _mxu_dot_impl(b, g, "nt"), _mxu_dot_impl(a, g, "nn")


_mxu_dot.defvjp(_mxu_dot_fwd, _mxu_dot_bwd)


_DOT_DIMS = {"nn": (((1,), (0,)), ((), ())), "nt": (((1,), (1,)), ((), ())), "tn": (((0,), (0,)), ((), ()))}


def _split(x):
    hi = x.astype(BF16)
    return hi, (x - hi.astype(F32)).astype(BF16)


def _dot3_impl(a, b, form):
    dg = lambda p, q: lax.dot_general(p, q, _DOT_DIMS[form], preferred_element_type=F32)
    ah, al = _split(a)
    bh, bl = _split(b)
    return dg(ah, bh) + (dg(ah, bl) + dg(al, bh))


@functools.partial(jax.custom_vjp, nondiff_argnums=(2,))
def _dot3(a, b, form):
    return _dot3_impl(a, b, form)


def _dot3_fwd(a, b, form):
    return _dot3_impl(a, b, form), (a, b)


def _dot3_bwd(form, saved, g):
    a, b = saved
    if form == "nn":
        return _dot3_impl(g, b, "nt"), _dot3_impl(a, g, "tn")
    if form == "nt":
        return _dot3_impl(g, b, "nn"), _dot3_impl(g, a, "tn")
    return _dot3_impl(b, g, "nt"), _dot3_impl(a, g, "nn")


_dot3.defvjp(_dot3_fwd, _dot3_bwd)


def _hdot(a, b):
    return _dot3(a, b, "nn")


def _mask_dot(mask, x, form):
    dg = lambda q: lax.dot_general(mask.astype(BF16), q, _DOT_DIMS[form], preferred_element_type=F32)
    x1 = x.astype(BF16)
    r1 = x - x1.astype(F32)
    x2 = r1.astype(BF16)
    x3 = (r1 - x2.astype(F32)).astype(BF16)
    return dg(x1) + (dg(x2) + dg(x3))


def _decay_masks(c):
    row = lax.broadcasted_iota(jnp.int32, (c, c), 0)
    col = lax.broadcasted_iota(jnp.int32, (c, c), 1)
    return (row >= col).astype(F32), row <= col


def _decay_impl(gb):
    lower, upper = _decay_masks(gb.shape[0])
    return _mask_dot(lower, gb, "nn"), _mask_dot(jnp.ones_like(gb), jnp.where(upper, gb, 0.0), "nn")


@jax.custom_vjp
def _decay_matrices(gb):
    return _decay_impl(gb)


def _decay_fwd(gb):
    return _decay_impl(gb), None


def _decay_bwd(_, cts):
    gc, gr = cts
    lower, upper = _decay_masks(gc.shape[0])
    return (_mask_dot(lower, gc, "tn") + jnp.where(upper, _mask_dot(jnp.ones_like(gr), gr, "tn"), 0.0),)


_decay_matrices.defvjp(_decay_fwd, _decay_bwd)


def _softplus(x):
    return jnp.maximum(x, 0.0) + jnp.log(1.0 + jnp.exp(-jnp.abs(x)))


def _heads(f, *lists):
    return [f(*t) for t in zip(*lists)]


def _dn_chunk(qr, kr, v, z, braw, araw, alog, dtb, nw, state, valid):
    c = DN_CHUNK
    row = lax.broadcasted_iota(jnp.int32, (c, c), 0)
    col = lax.broadcasted_iota(jnp.int32, (c, c), 1)
    incl = row >= col
    strict = row > col
    eye = jnp.where(row == col, 1.0, 0.0)
    q = _heads(lambda t: t * lax.rsqrt(jnp.sum(t * t, -1, keepdims=True) + EPS) * (DN_HEAD_DIM ** -0.5), qr)
    k = _heads(lambda t: t * lax.rsqrt(jnp.sum(t * t, -1, keepdims=True) + EPS), kr)
    beta = _heads(lambda t: _sigmoid(t) * valid, braw)
    g = _heads(lambda al, ar, dt: -jnp.exp(al) * _softplus(ar + dt) * valid, alog, araw, dtb)
    decay = _heads(lambda t: _decay_matrices(jnp.broadcast_to(t, (c, c))), g)
    dmask = _heads(lambda d: jnp.where(incl, jnp.exp(jnp.where(incl, d[0] - d[1], 0.0)), 0.0), decay)
    dec = _heads(lambda d: d[0][:, :1], decay)
    dlast = _heads(lambda d: d[0][c - 1:c, :1], decay)
    kk = _heads(lambda t: _mxu_dot(t, t, "nt"), k)
    a = _heads(lambda b, t, d: jnp.where(strict, b * t * d, 0.0), beta, kk, dmask)
    x = _heads(lambda t: eye - t, a)
    p = _heads(_hdot, a, a)
    for it in range(5):
        x = _heads(lambda s, t: s + _hdot(s, t), x, p)
        if it < 4:
            p = _heads(_hdot, p, p)
    u = _heads(lambda s, t, b: _hdot(s, t * b), x, v, beta)
    w = _heads(lambda s, t, b, d: _hdot(s, t * (b * jnp.exp(d))), x, k, beta, dec)
    qk = _heads(lambda s, t, d: _mxu_dot(s, t, "nt") * d, q, k, dmask)
    q_dec = _heads(lambda t, d: t * jnp.exp(d), q, dec)
    k_dec = _heads(lambda t, dl, d: t * jnp.exp(dl - d), k, dlast, dec)
    v_new = _heads(lambda s, t, st: s - _mxu_dot(t, st, "nn"), u, w, state)
    o = _heads(lambda qd, st, s, vn: _mxu_dot(qd, st, "nn") + _mxu_dot(s, vn, "nn"), q_dec, state, qk, v_new)
    new_state = _heads(lambda st, dl, kd, vn: st * jnp.exp(dl) + _mxu_dot(kd, vn, "tn"), state, dlast, k_dec, v_new)
    y = _heads(lambda t, zz: _rms(t, nw) * (zz * _sigmoid(zz)), o, z)
    return y, new_state


def _dn_valid(n):
    row = n * DN_CHUNK + lax.broadcasted_iota(jnp.int32, (DN_CHUNK, 1), 0)
    return (row >= PAD_ROWS).astype(F32)


def _dn_in_specs(rev, nc):
    cn = (lambda n: nc - 1 - n) if rev else (lambda n: n)
    c, hd = DN_CHUNK, DN_HEAD_DIM
    return [
        pl.BlockSpec((None, c, DN_DIM), lambda n: (0, cn(n), 0)),
        pl.BlockSpec((None, c, DN_DIM), lambda n: (1, cn(n), 0)),
        pl.BlockSpec((None, c, DN_DIM), lambda n: (2, cn(n), 0)),
        pl.BlockSpec((None, c, DN_DIM), lambda n: (6, cn(n), 0)),
        pl.BlockSpec((DN_HEADS, 2, c, 1), lambda n: (0, 0, cn(n), 0)),
        pl.BlockSpec((DN_HEADS, SUBLANES, LANES), lambda n: (0, 0, 0)),
        pl.BlockSpec((1, hd), lambda n: (0, 0)),
    ]


def _head(ref_or_val, h):
    return ref_or_val[:, h * DN_HEAD_DIM:(h + 1) * DN_HEAD_DIM]


def _dn_load(q_ref, k_ref, v_ref, z_ref, ba_ref, hp_ref):
    heads = range(DN_HEADS)
    return ([_head(q_ref, h) for h in heads], [_head(k_ref, h) for h in heads], [_head(v_ref, h) for h in heads],
            [_head(z_ref, h) for h in heads], [ba_ref[h, 0] for h in heads], [ba_ref[h, 1] for h in heads],
            [hp_ref[h, 0:1, 0:1] for h in heads], [hp_ref[h, 1:2, 0:1] for h in heads])


def _delta_fwd(qkvc, pm, ba, hp, nw, *, name):
    _, m, _ = qkvc.shape
    nc = m // DN_CHUNK
    hd = DN_HEAD_DIM

    def body(q_ref, k_ref, v_ref, z_ref, ba_ref, hp_ref, nw_ref, y_ref, s_ref, state):
        n = pl.program_id(0)

        @pl.when(n == 0)
        def _():
            state[...] = jnp.zeros_like(state)

        heads = range(DN_HEADS)
        old = [state[h] for h in heads]
        y, new = _dn_chunk(*_dn_load(q_ref, k_ref, v_ref, z_ref, ba_ref, hp_ref), nw_ref[...], old, _dn_valid(n))
        for h in heads:
            s_ref[h] = old[h]
            y_ref[:, h * hd:(h + 1) * hd] = y[h].astype(y_ref.dtype)
            state[h] = new[h]

    return pl.pallas_call(
        body, name=name, grid=(nc,), in_specs=_dn_in_specs(False, nc),
        out_specs=[pl.BlockSpec((DN_CHUNK, DN_DIM), lambda n: (n, 0)), pl.BlockSpec((DN_HEADS, None, hd, hd), lambda n: (0, n, 0, 0))],
        out_shape=[jax.ShapeDtypeStruct((m, DN_DIM), MXU_DTYPE), jax.ShapeDtypeStruct((DN_HEADS, nc, hd, hd), F32)],
        scratch_shapes=[pltpu.VMEM((DN_HEADS, hd, hd), F32)],
        compiler_params=_params(("arbitrary",)),
    )(qkvc, qkvc, qkvc, pm, ba, hp, nw)


def _delta_bwd(dy, qkvc, pm, ba, hp, nw, states, dpm, *, name):
    _, m, _ = qkvc.shape
    nc = m // DN_CHUNK
    hd, c = DN_HEAD_DIM, DN_CHUNK

    def body(q_ref, k_ref, v_ref, z_ref, ba_ref, hp_ref, nw_ref, s_ref, dy_ref, dpm_in,
             dz_ref, dqkv_ref, dba_ref, dhp_ref, dnw_ref, dstate):
        del dpm_in
        step = pl.program_id(0)
        n = nc - 1 - step

        @pl.when(step == 0)
        def _():
            dstate[...] = jnp.zeros_like(dstate)
            dhp_ref[...] = jnp.zeros_like(dhp_ref)
            dnw_ref[...] = jnp.zeros_like(dnw_ref)

        valid = _dn_valid(n)
        heads = range(DN_HEADS)
        fn = lambda *a: _dn_chunk(*a, valid)
        _, vjp = jax.vjp(fn, *_dn_load(q_ref, k_ref, v_ref, z_ref, ba_ref, hp_ref), nw_ref[...], [s_ref[h] for h in heads])
        dq, dk, dv, dz, dbr, dar, dalog, ddtb, dnw, dst = vjp(([_head(dy_ref, h) for h in heads], [dstate[h] for h in heads]))
        for h in heads:
            cols = slice(h * hd, (h + 1) * hd)
            dqkv_ref[0, :, cols] = dq[h]
            dqkv_ref[1, :, cols] = dk[h]
            dqkv_ref[2, :, cols] = dv[h]
            dz_ref[:, cols] = dz[h]
            dba_ref[h, 0] = dbr[h]
            dba_ref[h, 1] = dar[h]
            dstate[h] = dst[h]
            dhp_ref[h] += jnp.concatenate([jnp.broadcast_to(dalog[h], (1, LANES)), jnp.broadcast_to(ddtb[h], (1, LANES)),
                                           jnp.zeros((SUBLANES - 2, LANES), F32)], 0)
        dnw_ref[...] += dnw

    rn = lambda n: nc - 1 - n
    in_specs = _dn_in_specs(True, nc) + [
        pl.BlockSpec((DN_HEADS, None, hd, hd), lambda n: (0, rn(n), 0, 0)),
        pl.BlockSpec((c, DN_DIM), lambda n: (rn(n), 1)),
        pl.BlockSpec(memory_space=pl.ANY),
    ]
    out_specs = [
        pl.BlockSpec((None, c, DN_DIM), lambda n: (6, rn(n), 0)),
        pl.BlockSpec((3, c, DN_DIM), lambda n: (0, rn(n), 0)),
        pl.BlockSpec((DN_HEADS, 2, c, 1), lambda n: (0, 0, rn(n), 0)),
        pl.BlockSpec((DN_HEADS, SUBLANES, LANES), lambda n: (0, 0, 0)),
        pl.BlockSpec((1, hd), lambda n: (0, 0)),
    ]
    return pl.pallas_call(
        body, name=name, grid=(nc,), in_specs=in_specs, out_specs=out_specs,
        out_shape=[jax.ShapeDtypeStruct(dpm.shape, F32), jax.ShapeDtypeStruct(qkvc.shape, F32),
                   jax.ShapeDtypeStruct(ba.shape, F32), jax.ShapeDtypeStruct(hp.shape, F32),
                   jax.ShapeDtypeStruct((1, hd), F32)],
        scratch_shapes=[pltpu.VMEM((DN_HEADS, hd, hd), F32)],
        input_output_aliases={9: 0},
        compiler_params=_params(("arbitrary",)),
    )(qkvc, qkvc, qkvc, pm, ba, hp, nw, states, dy, dpm)


def _attn_block(q4, k0, kp, kc, v0, vp, vc, qw, kw, sink, n):
    g, b, hd = SWA_GROUP, SWA_BLOCK, SWA_HEAD_DIM
    qn = _rms(q4, qw) * (hd ** -0.5)
    kn = _rms(jnp.concatenate([k0, kp, kc], 0), kw)
    vcat = jnp.concatenate([v0, vp, vc], 0)
    s = _mxu_dot(qn.reshape(g * b, hd), kn, "nt").reshape(g, b, 3 * b)
    i = lax.broadcasted_iota(jnp.int32, (b, 3 * b), 0)
    c = lax.broadcasted_iota(jnp.int32, (b, 3 * b), 1)
    in_meta, in_prev, in_cur = c < b, (c >= b) & (c < 2 * b), c >= 2 * b
    j = c - jnp.where(in_meta, 0, jnp.where(in_prev, b, 2 * b))
    meta_lo = jnp.where(n == 0, b, PAD_ROWS)
    cur_lo = jnp.where(n == 0, PAD_ROWS, 0)
    prev_off = jnp.where(n >= 2, 0, 2 * b)
    valid = (in_meta & (j >= meta_lo)) | (in_prev & (j > i + prev_off)) | (in_cur & (j <= i) & (j >= cur_lo))
    s = jnp.where(valid[None], s, NEG)
    m = lax.stop_gradient(jnp.maximum(jnp.max(s, -1, keepdims=True), sink))
    e = jnp.exp(s - m)
    p = e / (jnp.sum(e, -1, keepdims=True) + jnp.exp(sink - m))
    return _mxu_dot(p.reshape(g * b, 3 * b), vcat, "nn").reshape(g, b, hd)


def _attn_in_specs():
    g, b, hd = SWA_GROUP, SWA_BLOCK, SWA_HEAD_DIM
    kv = lambda f: pl.BlockSpec((None, b, hd), lambda kh, n: (kh, f(n), 0))
    blocks = [lambda n: 0, lambda n: jnp.maximum(n - 1, 0), lambda n: n]
    return ([pl.BlockSpec((g, b, hd), lambda kh, n: (kh, n, 0))] + [kv(f) for f in blocks] + [kv(f) for f in blocks]
            + [pl.BlockSpec((1, hd), lambda kh, n: (0, 0)), pl.BlockSpec((1, hd), lambda kh, n: (0, 0)),
               pl.BlockSpec((None, g, 1, 1), lambda kh, n: (kh, 0, 0, 0))])


def _attn_fwd(q3, k3, v3, qw, kw, sink, *, name):
    _, m, hd = q3.shape
    g, b = SWA_GROUP, SWA_BLOCK

    def body(q_ref, k0, kp, kc, v0, vp, vc, qw_ref, kw_ref, s_ref, o_ref):
        o_ref[...] = _attn_block(q_ref[...], k0[...], kp[...], kc[...], v0[...], vp[...], vc[...], qw_ref[...], kw_ref[...],
                                 s_ref[...], pl.program_id(1))

    return pl.pallas_call(
        body, name=name, grid=(SWA_KV_HEADS, m // b), in_specs=_attn_in_specs(),
        out_specs=pl.BlockSpec((g, b, hd), lambda kh, n: (kh, n, 0)),
        out_shape=jax.ShapeDtypeStruct(q3.shape, F32),
        compiler_params=_params(("parallel", "parallel")),
    )(q3, k3, k3, k3, v3, v3, v3, qw, kw, sink)


def _attn_bwd(do3, q3, k3, v3, qw, kw, sink, *, name):
    _, m, hd = q3.shape
    g, b = SWA_GROUP, SWA_BLOCK

    def body(q_ref, k0, kp, kc, v0, vp, vc, qw_ref, kw_ref, s_ref, do_ref, dq_ref, dk_ref, dv_ref, dqw_ref, dkw_ref, ds_ref):
        n = pl.program_id(1)

        @pl.when(n == 0)
        def _():
            for r in (dk_ref, dv_ref, dqw_ref, dkw_ref, ds_ref):
                r[...] = jnp.zeros_like(r)

        fn = lambda *a: _attn_block(*a, n)
        _, vjp = jax.vjp(fn, q_ref[...], k0[...], kp[...], kc[...], v0[...], vp[...], vc[...], qw_ref[...], kw_ref[...], s_ref[...])
        dq, dk0, dkp, dkc, dv0, dvp, dvc, dqw, dkw, dsk = vjp(do_ref[...])
        dq_ref[...] = dq
        prev = pl.multiple_of(jnp.maximum(n - 1, 0) * b, b)
        cur = pl.multiple_of(n * b, b)
        for ref, parts in ((dk_ref, (dk0, dkp, dkc)), (dv_ref, (dv0, dvp, dvc))):
            ref[pl.ds(0, b), :] += parts[0]
            ref[pl.ds(prev, b), :] += parts[1]
            ref[pl.ds(cur, b), :] += parts[2]
        dqw_ref[...] += dqw
        dkw_ref[...] += dkw
        ds_ref[...] += dsk

    acc = lambda shape: pl.BlockSpec((None,) + shape, lambda kh, n: (kh,) + (0,) * len(shape))
    return pl.pallas_call(
        body, name=name, grid=(SWA_KV_HEADS, m // b),
        in_specs=_attn_in_specs() + [pl.BlockSpec((g, b, hd), lambda kh, n: (kh, n, 0))],
        out_specs=[pl.BlockSpec((g, b, hd), lambda kh, n: (kh, n, 0)), acc((m, hd)), acc((m, hd)), acc((1, hd)), acc((1, hd)),
                   acc((g, 1, 1))],
        out_shape=[jax.ShapeDtypeStruct(q3.shape, F32), jax.ShapeDtypeStruct(k3.shape, F32), jax.ShapeDtypeStruct(v3.shape, F32),
                   jax.ShapeDtypeStruct((SWA_KV_HEADS, 1, hd), F32), jax.ShapeDtypeStruct((SWA_KV_HEADS, 1, hd), F32),
                   jax.ShapeDtypeStruct(sink.shape, F32)],
        compiler_params=_params(("parallel", "arbitrary")),
    )(q3, k3, k3, k3, v3, v3, v3, qw, kw, sink, do3)


def _loss_bwd(h, target, *, name):
    m, d = h.shape
    b = SWA_BLOCK

    def body(h_ref, t_ref, l_ref, dh_ref):
        i = pl.program_id(0)

        @pl.when(i == 0)
        def _():
            l_ref[...] = jnp.zeros_like(l_ref)
            dh_ref[...] = jnp.zeros_like(dh_ref)

        @pl.when(i > 0)
        def _():
            e = h_ref[...] - t_ref[...]
            dh_ref[...] = e * (1.0 / d)
            l_ref[...] += jnp.sum(jnp.sum(e * e, 0, keepdims=True), 1, keepdims=True) * (0.5 / d)

    return pl.pallas_call(
        body, name=name, grid=(m // b,),
        in_specs=[pl.BlockSpec((b, d), lambda i: (i, 0)), pl.BlockSpec((b, d), lambda i: (jnp.maximum(i - 1, 0), 0))],
        out_specs=[pl.BlockSpec((1, LANES), lambda i: (0, 0)), pl.BlockSpec((b, d), lambda i: (i, 0))],
        out_shape=[jax.ShapeDtypeStruct((1, LANES), F32), jax.ShapeDtypeStruct((m, d), F32)],
        compiler_params=_params(("arbitrary",)),
    )(h, target)


def _heads_major(a, heads):
    m = a.shape[0]
    return a.reshape(m, heads, SWA_HEAD_DIM).transpose(1, 0, 2)


def _heads_minor(a3):
    heads, m, hd = a3.shape
    return a3.transpose(1, 0, 2).reshape(m, heads * hd)


def _ffn_fwd(h, nw, w_up, cw, w_down, tag):
    u, hn = _norm_matmul(h, nw, w_up, o_seg=D_FF, name=f"ffn_up_{tag}")
    a = _ffn_act_fwd(u, cw, name=f"ffn_act_{tag}")
    return _mm_nn(a, w_down, res=h, name=f"ffn_down_{tag}"), (h, hn, u, a)


def _ffn_bwd(dh, saved, nw, w_up_t, cw, w_down_t, tag):
    h, hn, u, a = saved
    da = _mm_nn(dh, w_down_t, name=f"ffn_da_{tag}")
    dw_down = _mm_tn(a, dh, name=f"ffn_dwdown_{tag}")
    du, dcw = _ffn_act_bwd(da, u, cw, name=f"ffn_act_bwd_{tag}")
    dhn = _mm_nn(du, w_up_t, a_seg=True, name=f"ffn_dhn_{tag}")
    dw_up = _mm_tn(hn, du, b_seg=True, name=f"ffn_dwup_{tag}")
    dh_in, dnw = _rmsnorm_bwd(dhn, h, nw, dh, name=f"ffn_norm_bwd_{tag}")
    return dh_in, dnw, dw_up, dcw, dw_down


def _local_step(x, target, w):
    seq, d = x.shape
    m = PAD_ROWS + N_META + seq
    h0 = jnp.concatenate([jnp.zeros((PAD_ROWS, d), F32), w["meta"], x], 0)

    pm, hn0 = _norm_matmul(h0, w["anw"][0], w["w_in_main"], o_seg=SEG, name="mix_in")
    pba = _mm_nn(hn0, w["w_in_tail"], name="mix_in_tail")
    ya = _shortconv_fwd(pm, w["caw"], name="shortconv")
    qkvc = _dnpre_fwd(pm, w["dcw"], name="dn_conv")
    ba = pba[:, :2 * DN_HEADS].T.reshape(2, DN_HEADS, m, 1).transpose(1, 0, 2, 3)
    yb, states = _delta_fwd(qkvc, pm, ba, w["hp"], w["dnw"], name="delta")
    y = jnp.concatenate([ya, yb], 1)
    h1 = _mm_nn(y, w["w_out"], res=h0, name="mix_out")
    h2, ffn0 = _ffn_fwd(h1, w["fnw"][0], w["w_up"][0], w["fcw"][0], w["w_down"][0], "l0")

    qkv, hn2 = _norm_matmul(h2, w["anw"][1], w["wqkv"], name="attn_qkv")
    nq, nkv = SWA_HEADS * SWA_HEAD_DIM, SWA_KV_HEADS * SWA_HEAD_DIM
    q3 = _heads_major(qkv[:, :nq], SWA_HEADS)
    k3 = _heads_major(qkv[:, nq:nq + nkv], SWA_KV_HEADS)
    v3 = _heads_major(qkv[:, nq + nkv:], SWA_KV_HEADS)
    o3 = _attn_fwd(q3, k3, v3, w["qnw"], w["knw"], w["sink"], name="attn")
    o = _heads_minor(o3).astype(MXU_DTYPE)
    h3 = _mm_nn(o, w["wo"], res=h2, name="attn_out")
    h4, ffn1 = _ffn_fwd(h3, w["fnw"][1], w["w_up"][1], w["fcw"][1], w["w_down"][1], "l1")

    loss, dh4 = _loss_bwd(h4, target, name="loss")

    g = {}
    dh3, dfnw1, dwup1, dfcw1, dwdown1 = _ffn_bwd(dh4, ffn1, w["fnw"][1], w["w_up_t"][1], w["fcw"][1], w["w_down_t"][1], "l1")

    do = _mm_nn(dh3, w["wo_t"], name="attn_do")
    g["wo"] = _mm_tn(o, dh3, name="attn_dwo")
    dq3, dk3, dv3, dqw, dkw, dsink = _attn_bwd(_heads_major(do, SWA_HEADS), q3, k3, v3, w["qnw"], w["knw"], w["sink"], name="attn_bwd")
    dqkv = jnp.concatenate([_heads_minor(dq3), _heads_minor(dk3), _heads_minor(dv3)], 1)
    dhn2 = _mm_nn(dqkv, w["wqkv_t"], name="attn_dhn")
    g["wqkv"] = _mm_tn(hn2, dqkv, name="attn_dwqkv")
    dh2, danw1 = _rmsnorm_bwd(dhn2, h2, w["anw"][1], dh3, name="attn_norm_bwd")

    dh1, dfnw0, dwup0, dfcw0, dwdown0 = _ffn_bwd(dh2, ffn0, w["fnw"][0], w["w_up_t"][0], w["fcw"][0], w["w_down_t"][0], "l0")

    dy = _mm_nn(dh1, w["w_out_t"], name="mix_dy")
    g["w_out"] = _mm_tn(y, dh1, name="mix_dwout")
    dpm = jnp.zeros(pm.shape, F32)
    dpm, dqkvc, dba, dhp, ddnw = _delta_bwd(dy, qkvc, pm, ba, w["hp"], w["dnw"], states, dpm, name="delta_bwd")
    dpm, ddcw = _dnpre_bwd(dqkvc, pm, w["dcw"], dpm, name="dn_conv_bwd")
    dpm, dcaw = _shortconv_bwd(dy, pm, w["caw"], dpm, name="shortconv_bwd")
    dpba = jnp.pad(dba.transpose(1, 0, 2, 3).reshape(2 * DN_HEADS, m).T, ((0, 0), (0, LANES - 2 * DN_HEADS)))
    dhn0 = _mm_nn(dpm, w["w_in_main_t"], a_seg=True, name="mix_dhn")
    dhn0 = _mm_nn(dpba, w["w_in_tail_t"], res=dhn0, name="mix_dhn_tail")
    g["w_in_main"] = _mm_tn(hn0, dpm, b_seg=True, name="mix_dwin")
    g["w_in_tail"] = _mm_tn(hn0, dpba, name="mix_dwin_tail")
    dh0, danw0 = _rmsnorm_bwd(dhn0, h0, w["anw"][0], dh1, name="mix_norm_bwd")

    g.update(
        x=dh0[PAD_ROWS + N_META:], meta=dh0[PAD_ROWS:PAD_ROWS + N_META], anw=[danw0, danw1], fnw=[dfnw0, dfnw1],
        caw=dcaw, dcw=ddcw, hp=dhp, dnw=ddnw, qnw=jnp.sum(dqw, 0), knw=jnp.sum(dkw, 0), sink=dsink,
        w_up=[dwup0, dwup1], fcw=[dfcw0, dfcw1], w_down=[dwdown0, dwdown1])
    return loss, g


N_TAIL = 2 * DN_HEADS


def _prepare_weights(p):
    n_main = N_SEG * SEG
    w_in = p["mix_w_in"][0]
    tail = jnp.pad(w_in[:, n_main:], ((0, 0), (0, LANES - N_TAIL)))
    wqkv = jnp.concatenate([p["swa_wq"][0], p["swa_wk"][0], p["swa_wv"][0]], 1)
    hp = jnp.zeros((DN_HEADS, SUBLANES, LANES), F32)
    hp = hp.at[:, 0, :].set(p["dn_a_log"][0][:, None]).at[:, 1, :].set(p["dn_dt_bias"][0][:, None])
    depth = p["ffn_w_up"].shape[0]
    return dict(
        meta=p["meta_tokens"], anw=[p["attn_norm_w"][i:i + 1] for i in range(depth)],
        fnw=[p["ffn_norm_w"][i:i + 1] for i in range(depth)],
        w_in_main=w_in[:, :n_main], w_in_tail=tail, w_in_main_t=w_in[:, :n_main].T, w_in_tail_t=tail.T,
        caw=_pad_w(p["conv_a_w"][0]), dcw=_pad_w(p["dn_conv_w"][0]), hp=hp, dnw=p["dn_norm_w"],
        w_out=p["mix_w_out"][0], w_out_t=p["mix_w_out"][0].T, wqkv=wqkv, wqkv_t=wqkv.T,
        qnw=p["swa_q_norm_w"], knw=p["swa_k_norm_w"], sink=p["swa_sinks"].reshape(SWA_KV_HEADS, SWA_GROUP, 1, 1),
        wo=p["swa_wo"][0], wo_t=p["swa_wo"][0].T,
        w_up=[p["ffn_w_up"][i] for i in range(depth)], w_up_t=[p["ffn_w_up"][i].T for i in range(depth)],
        fcw=[_pad_w(p["ffn_conv_w"][i]) for i in range(depth)],
        w_down=[p["ffn_w_down"][i] for i in range(depth)], w_down_t=[p["ffn_w_down"][i].T for i in range(depth)])


def _reference_named(g):
    nq, nkv = SWA_HEADS * SWA_HEAD_DIM, SWA_KV_HEADS * SWA_HEAD_DIM
    return dict(
        meta_tokens=g["meta"], attn_norm_w=jnp.concatenate(g["anw"], 0), ffn_norm_w=jnp.concatenate(g["fnw"], 0),
        mix_w_in=jnp.concatenate([g["w_in_main"], g["w_in_tail"][:, :N_TAIL]], 1)[None],
        conv_a_w=g["caw"][None, :3], dn_conv_w=g["dcw"][None, :4],
        dn_a_log=g["hp"][None, :, 0, 0], dn_dt_bias=g["hp"][None, :, 1, 0], dn_norm_w=g["dnw"],
        mix_w_out=g["w_out"][None], swa_wq=g["wqkv"][None, :, :nq], swa_wk=g["wqkv"][None, :, nq:nq + nkv],
        swa_wv=g["wqkv"][None, :, nq + nkv:], swa_q_norm_w=g["qnw"], swa_k_norm_w=g["knw"],
        swa_sinks=g["sink"].reshape(1, SWA_HEADS), swa_wo=g["wo"][None],
        ffn_w_up=jnp.stack(g["w_up"]), ffn_conv_w=jnp.stack([c[:3] for c in g["fcw"]]), ffn_w_down=jnp.stack(g["w_down"]))


def _my_index():
    return 4 * lax.axis_index("x") + 2 * lax.axis_index("y") + lax.axis_index("c")


def _all_to_all(arrays, *, name):
    n = len(arrays)

    def body(*refs):
        ins, outs = refs[:n], refs[n:2 * n]
        send_sems, recv_sems, local_sems = refs[2 * n:]
        x, y, c = lax.axis_index("x"), lax.axis_index("y"), lax.axis_index("c")
        me = 4 * x + 2 * y + c
        copies = []
        for i in range(n):
            local = pltpu.make_async_copy(ins[i].at[me], outs[i].at[me], local_sems.at[i])
            local.start()
            copies.append(local)
        for d in range(1, N_DEV):
            px, py, pc = x ^ (d >> 2), y ^ ((d >> 1) & 1), c ^ (d & 1)
            for i in range(n):
                remote = pltpu.make_async_remote_copy(
                    src_ref=ins[i].at[4 * px + 2 * py + pc], dst_ref=outs[i].at[me], send_sem=send_sems.at[i, d],
                    recv_sem=recv_sems.at[i, d], device_id=(px, py, pc), device_id_type=pl.DeviceIdType.MESH)
                remote.start()
                copies.append(remote)
        for cp in copies:
            cp.wait()

    hbm = pl.BlockSpec(memory_space=pl.ANY)
    return pl.pallas_call(
        body, name=name, in_specs=[hbm] * n, out_specs=[hbm] * n,
        out_shape=[jax.ShapeDtypeStruct(a.shape, a.dtype) for a in arrays],
        scratch_shapes=[pltpu.SemaphoreType.DMA((n, N_DEV)), pltpu.SemaphoreType.DMA((n, N_DEV)), pltpu.SemaphoreType.DMA((n,))],
    )(*arrays)


def _all_gather(arrays, *, name):
    n = len(arrays)

    def body(*refs):
        ins, outs = refs[:n], refs[n:2 * n]
        send_sems, recv_sems, local_sems = refs[2 * n:]
        x, y, c = lax.axis_index("x"), lax.axis_index("y"), lax.axis_index("c")
        me, sibling = (x, y, c), (x, y, 1 - c)
        chips = [(1 - x, y), (x, 1 - y), (1 - x, 1 - y)]

        def copy(i, k, block, to, src=None):
            rows = outs[i].at[4 * block[0] + 2 * block[1] + block[2]]
            return pltpu.make_async_remote_copy(
                src_ref=rows if src is None else src, dst_ref=rows, send_sem=send_sems.at[i, k], recv_sem=recv_sems.at[i, k],
                device_id=to, device_id_type=pl.DeviceIdType.MESH)

        mine = [pltpu.make_async_copy(ins[i], outs[i].at[4 * x + 2 * y + c], local_sems.at[i]) for i in range(n)]
        first = []
        for j, chip in enumerate(chips):
            first += [copy(i, 1 + j, me, (*chip, c), src=ins[i]) for i in range(n)]
        first += [copy(i, 0, me, sibling, src=ins[i]) for i in range(n)]
        for cp in first + mine:
            cp.start()
        passed = []
        for j, chip in enumerate(chips):
            for i in range(n):
                copy(i, 1 + j, (*chip, c), me).wait_recv()
                fwd = copy(i, 4 + j, (*chip, c), sibling)
                fwd.start()
                passed.append(fwd)
        for i in range(n):
            copy(i, 0, sibling, me).wait_recv()
            for j, chip in enumerate(chips):
                copy(i, 4 + j, (*chip, 1 - c), me).wait_recv()
        for cp in first + passed:
            cp.wait_send()
        for cp in mine:
            cp.wait()

    hbm = pl.BlockSpec(memory_space=pl.ANY)
    return pl.pallas_call(
        body, name=name, in_specs=[hbm] * n, out_specs=[hbm] * n,
        out_shape=[jax.ShapeDtypeStruct((N_DEV,) + tuple(a.shape), a.dtype) for a in arrays],
        scratch_shapes=[pltpu.SemaphoreType.DMA((n, 7)), pltpu.SemaphoreType.DMA((n, 7)), pltpu.SemaphoreType.DMA((n,))],
    )(*arrays)


def _row_tile(r, c):
    best = None
    for t in range(2 * SUBLANES, r + 1, 2 * SUBLANES):
        if r % t == 0 and N_DEV * t * c * 4 <= 6 * 1024 * 1024:
            best = t
    return best or r


def _adamw(parts, w, m, v, *, name):
    r, c = w.shape
    tr = _row_tile(r, c)

    def body(p_ref, w_ref, m_ref, v_ref, g_ref, d_ref, nm_ref, nv_ref):
        g = p_ref[0].astype(F32)
        for j in range(1, N_DEV):
            g = g + p_ref[j].astype(F32)
        m2 = ADAM_B1 * m_ref[...] + (1.0 - ADAM_B1) * g
        v2 = ADAM_B2 * v_ref[...] + (1.0 - ADAM_B2) * jnp.square(g)
        m_hat = m2 / (1.0 - ADAM_B1 ** ADAM_STEP)
        v_hat = v2 / (1.0 - ADAM_B2 ** ADAM_STEP)
        g_ref[...] = g
        d_ref[...] = -ADAM_LR * (m_hat / (jnp.sqrt(v_hat) + ADAM_EPS) + ADAM_WD * w_ref[...])
        nm_ref[...] = m2
        nv_ref[...] = v2

    blk = pl.BlockSpec((tr, c), lambda i: (i, 0))
    out = jax.ShapeDtypeStruct((r, c), F32)
    return pl.pallas_call(
        body, name=name, grid=(r // tr,),
        in_specs=[pl.BlockSpec((N_DEV, tr, c), lambda i: (0, i, 0)), blk, blk, blk],
        out_specs=[blk, blk, blk, blk], out_shape=[out, out, out, out],
        compiler_params=_params(("parallel",)),
    )(parts, w, m, v)


SHARD_AXIS = dict(
    meta_tokens=1, attn_norm_w=None, ffn_norm_w=None, mix_w_in=2, conv_a_w=2, dn_conv_w=2, dn_a_log=None, dn_dt_bias=None,
    dn_norm_w=None, mix_w_out=1, swa_wq=1, swa_wk=1, swa_wv=1, swa_q_norm_w=None, swa_k_norm_w=None, swa_sinks=None,
    swa_wo=1, ffn_w_up=2, ffn_conv_w=2, ffn_w_down=1)
WEIGHTS = list(SHARD_AXIS)
BIG = ["mix_w_in", "mix_w_out", "swa_wq", "swa_wk", "swa_wv", "swa_wo", "ffn_w_up", "ffn_w_down"]
SMALL = [k for k in WEIGHTS if k not in BIG]
SMALL_SHARDED = [k for k in SMALL if SHARD_AXIS[k] is not None]


def _whole(g8, axis):
    t = jnp.moveaxis(g8, 0, axis)
    return t.reshape(t.shape[:axis] + (t.shape[axis] * t.shape[axis + 1],) + t.shape[axis + 2:])


def _by_owner(a, axis):
    s = a.shape[axis] // N_DEV
    return jnp.moveaxis(a.reshape(a.shape[:axis] + (N_DEV, s) + a.shape[axis + 1:]), axis, 0)


def _pack(arrays, lead=0):
    flat = jnp.concatenate([a.reshape(a.shape[:lead] + (-1,)) for a in arrays], -1)
    n = flat.shape[-1]
    rows = -(-n // (SUBLANES * LANES)) * SUBLANES
    flat = jnp.pad(flat, [(0, 0)] * lead + [(0, rows * LANES - n)])
    return flat.reshape(flat.shape[:lead] + (rows, LANES))


def _unpack(buf, shapes, lead=0):
    flat = buf.reshape(buf.shape[:lead] + (-1,))
    out, o = [], 0
    for s in shapes:
        n = 1
        for e in s:
            n *= e
        out.append(flat[..., o:o + n].reshape(buf.shape[:lead] + tuple(s)))
        o += n
    return out


def kernel(x, meta_tokens, attn_norm_w, ffn_norm_w, mix_w_in, conv_a_w, dn_conv_w, dn_a_log, dn_dt_bias, dn_norm_w, mix_w_out, swa_wq, swa_wk, swa_wv, swa_q_norm_w, swa_k_norm_w, swa_sinks, swa_wo, ffn_w_up, ffn_conv_w, ffn_w_down, loss_target, m_meta_tokens, m_attn_norm_w, m_ffn_norm_w, m_mix_w_in, m_conv_a_w, m_dn_conv_w, m_dn_a_log, m_dn_dt_bias, m_dn_norm_w, m_mix_w_out, m_swa_wq, m_swa_wk, m_swa_wv, m_swa_q_norm_w, m_swa_k_norm_w, m_swa_sinks, m_swa_wo, m_ffn_w_up, m_ffn_conv_w, m_ffn_w_down, v_meta_tokens, v_attn_norm_w, v_ffn_norm_w, v_mix_w_in, v_conv_a_w, v_dn_conv_w, v_dn_a_log, v_dn_dt_bias, v_dn_norm_w, v_mix_w_out, v_swa_wq, v_swa_wk, v_swa_wv, v_swa_q_norm_w, v_swa_k_norm_w, v_swa_sinks, v_swa_wo, v_ffn_w_up, v_ffn_conv_w, v_ffn_w_down):
    w = dict(meta_tokens=meta_tokens, attn_norm_w=attn_norm_w, ffn_norm_w=ffn_norm_w, mix_w_in=mix_w_in, conv_a_w=conv_a_w, dn_conv_w=dn_conv_w, dn_a_log=dn_a_log, dn_dt_bias=dn_dt_bias, dn_norm_w=dn_norm_w, mix_w_out=mix_w_out, swa_wq=swa_wq, swa_wk=swa_wk, swa_wv=swa_wv, swa_q_norm_w=swa_q_norm_w, swa_k_norm_w=swa_k_norm_w, swa_sinks=swa_sinks, swa_wo=swa_wo, ffn_w_up=ffn_w_up, ffn_conv_w=ffn_conv_w, ffn_w_down=ffn_w_down)
    mom = dict(meta_tokens=m_meta_tokens, attn_norm_w=m_attn_norm_w, ffn_norm_w=m_ffn_norm_w, mix_w_in=m_mix_w_in, conv_a_w=m_conv_a_w, dn_conv_w=m_dn_conv_w, dn_a_log=m_dn_a_log, dn_dt_bias=m_dn_dt_bias, dn_norm_w=m_dn_norm_w, mix_w_out=m_mix_w_out, swa_wq=m_swa_wq, swa_wk=m_swa_wk, swa_wv=m_swa_wv, swa_q_norm_w=m_swa_q_norm_w, swa_k_norm_w=m_swa_k_norm_w, swa_sinks=m_swa_sinks, swa_wo=m_swa_wo, ffn_w_up=m_ffn_w_up, ffn_conv_w=m_ffn_conv_w, ffn_w_down=m_ffn_w_down)
    var = dict(meta_tokens=v_meta_tokens, attn_norm_w=v_attn_norm_w, ffn_norm_w=v_ffn_norm_w, mix_w_in=v_mix_w_in, conv_a_w=v_conv_a_w, dn_conv_w=v_dn_conv_w, dn_a_log=v_dn_a_log, dn_dt_bias=v_dn_dt_bias, dn_norm_w=v_dn_norm_w, mix_w_out=v_mix_w_out, swa_wq=v_swa_wq, swa_wk=v_swa_wk, swa_wv=v_swa_wv, swa_q_norm_w=v_swa_q_norm_w, swa_k_norm_w=v_swa_k_norm_w, swa_sinks=v_swa_sinks, swa_wo=v_swa_wo, ffn_w_up=v_ffn_w_up, ffn_conv_w=v_ffn_conv_w, ffn_w_down=v_ffn_w_down)
    me = _my_index()

    small_shard_shapes = [w[k].shape for k in SMALL_SHARDED]
    sent = [w[k].astype(MXU_DTYPE) for k in BIG] + [_pack([w[k] for k in SMALL_SHARDED])]
    got = _all_gather(sent, name="gather_weights")
    whole = {k: _whole(a, SHARD_AXIS[k]) for k, a in zip(BIG, got[:-1])}
    for k, a in zip(SMALL_SHARDED, _unpack(got[-1], small_shard_shapes, lead=1)):
        whole[k] = _whole(a, SHARD_AXIS[k])
    for k in SMALL:
        whole.setdefault(k, w[k])

    loss, g = _local_step(x[0], loss_target[0], _prepare_weights(whole))
    grads = _reference_named(g)

    sent = [_by_owner(grads[k], SHARD_AXIS[k]) for k in BIG]
    got = _all_to_all(sent, name="scatter_grads")
    out_g, out_d, out_m, out_v = {}, {}, {}, {}
    for k, parts in zip(BIG, got):
        shp = w[k].shape
        r, c = shp[0] * shp[1], shp[2]
        res = _adamw(parts.reshape(N_DEV, r, c), w[k].reshape(r, c), mom[k].reshape(r, c), var[k].reshape(r, c), name=f"adamw_{k}")
        out_g[k], out_d[k], out_m[k], out_v[k] = [t.reshape(shp) for t in res]

    small_shapes = [grads[k].shape for k in SMALL]
    (all_small,) = _all_gather([_pack([loss] + [grads[k].astype(F32) for k in SMALL])], name="gather_small_grads")
    loss_parts, *small_parts = _unpack(all_small, [loss.shape] + small_shapes, lead=1)
    mine = []
    for k, p in zip(SMALL, small_parts):
        ax = SHARD_AXIS[k]
        mine.append(p if ax is None else lax.dynamic_slice_in_dim(p, me * w[k].shape[ax], w[k].shape[ax], 1 + ax))
    zero = jnp.zeros(loss.shape, F32)
    packed = [_pack([z] + [d[k] for k in SMALL]) for z, d in ((zero, w), (zero, mom), (zero, var))]
    res = _adamw(_pack([loss_parts] + mine, lead=1), *packed, name="adamw_small")
    shapes = [loss.shape] + [w[k].shape for k in SMALL]
    for t, dst in zip(res, (out_g, out_d, out_m, out_v)):
        parts = _unpack(t, shapes)
        if dst is out_g:
            total_loss = parts[0][0, 0]
        for k, a in zip(SMALL, parts[1:]):
            dst[k] = a

    return (total_loss, g["x"][None], *[out_g[k] for k in WEIGHTS], *[out_d[k] for k in WEIGHTS],
            *[out_m[k] for k in WEIGHTS], *[out_v[k] for k in WEIGHTS])
```

```python
import functools

import jax
import jax.numpy as jnp
from jax import lax
from jax.experimental import pallas as pl
from jax.experimental.pallas import tpu as pltpu

F32 = jnp.float32
BF16 = jnp.bfloat16
MXU_DTYPE = BF16
GRAD_WIRE_DTYPE = BF16

D_MODEL = 1024
N_META = 16
PAD_ROWS = 112
D_CONV = 512
DN_HEADS = 4
DN_HEAD_DIM = 128
DN_DIM = DN_HEADS * DN_HEAD_DIM
DN_CHUNK = 64
SEG = 512
N_SEG = 7
SWA_HEADS = 16
SWA_KV_HEADS = 4
SWA_GROUP = SWA_HEADS // SWA_KV_HEADS
SWA_HEAD_DIM = 64
SWA_BLOCK = 128
D_FF = 2816
EPS = 1e-6
NEG = -1e30
N_DEV = 8

ADAM_LR = 0.001
ADAM_B1 = 0.9
ADAM_B2 = 0.999
ADAM_EPS = 1e-08
ADAM_WD = 0.01
ADAM_STEP = 10

VMEM_LIMIT_BYTES = 52 * 1024 * 1024
SUBLANES = 8
LANES = 128


def _pick(n, prefs):
    for p in prefs:
        if n % p == 0:
            return p
    return n


def _params(sem, vmem=VMEM_LIMIT_BYTES):
    return pltpu.CompilerParams(dimension_semantics=sem, vmem_limit_bytes=vmem)


def _rms(x, w):
    return x * lax.rsqrt(jnp.mean(x * x, -1, keepdims=True) + EPS) * w


def _norm_matmul(h, nw, w, *, o_seg=None, name):
    m, k = h.shape
    n = w.shape[1]
    tm = _pick(m, (1408, 384, 128))
    tn = _pick(o_seg or n, (1408, 1024, 512, 256, 128))

    def body(h_ref, nw_ref, w_ref, o_ref, hn_ref, hn_s):
        @pl.when(pl.program_id(1) == 0)
        def _():
            hn = _rms(h_ref[...], nw_ref[...]).astype(MXU_DTYPE)
            hn_s[...] = hn
            hn_ref[...] = hn

        o_ref[...] = jnp.dot(hn_s[...], w_ref[...], preferred_element_type=F32)

    if o_seg:
        per = o_seg // tn
        o_shape = jax.ShapeDtypeStruct((n // o_seg, m, o_seg), F32)
        o_spec = pl.BlockSpec((None, tm, tn), lambda i, j: (j // per, i, j % per))
    else:
        o_shape = jax.ShapeDtypeStruct((m, n), F32)
        o_spec = pl.BlockSpec((tm, tn), lambda i, j: (i, j))
    return pl.pallas_call(
        body, name=name, grid=(m // tm, n // tn),
        in_specs=[pl.BlockSpec((tm, k), lambda i, j: (i, 0)), pl.BlockSpec((1, k), lambda i, j: (0, 0)),
                  pl.BlockSpec((k, tn), lambda i, j: (0, j))],
        out_specs=[o_spec, pl.BlockSpec((tm, k), lambda i, j: (i, 0))],
        out_shape=[o_shape, jax.ShapeDtypeStruct((m, k), MXU_DTYPE)],
        scratch_shapes=[pltpu.VMEM((tm, k), MXU_DTYPE)],
        compiler_params=_params(("parallel", "arbitrary")),
    )(h, nw, w)


TILE_BUDGET_BYTES = 38 * 1024 * 1024
TILE_SIZES = (2816, 1792, 1536, 1408, 1024, 512, 256, 128)


def _divisor_tiles(n):
    return [t for t in TILE_SIZES if n % t == 0] or [n]


def _mm_nn(a, w, *, res=None, a_seg=False, trans_w=False, name):
    if a_seg:
        s, m, seg = a.shape
    else:
        m, seg = a.shape
        s = 1
    k = s * seg
    n = w.shape[0] if trans_w else w.shape[1]
    tm = _pick(m, (1408, 384, 128))
    ab = a.dtype.itemsize
    k_steps = [(sb, seg) for sb in range(s, 0, -1) if s % sb == 0] if a_seg else [(1, t) for t in _divisor_tiles(seg)]
    best = None
    for tn in _divisor_tiles(n):
        for sb, tk1 in k_steps:
            tk = sb * tk1
            nk = k // tk
            need = 2 * tm * tk * ab + 2 * tk * tn * 2 + 2 * tm * tn * 4 + (tm * tn * 4 if nk > 1 else 0) + (2 * tm * tn * 4 if res is not None else 0)
            if need <= TILE_BUDGET_BYTES and (best is None or tk * tn > best[0]):
                best = (tk * tn, tn, sb, tk1)
    _, tn, sb, tk1 = best
    tk = sb * tk1
    nk = k // tk
    w_dims = _DOT_DIMS["nt" if trans_w else "nn"]

    def body(*refs):
        a_ref, w_ref = refs[:2]
        r_ref = refs[2] if res is not None else None
        o_ref = refs[3 if res is not None else 2]

        def partial_product():
            if not a_seg:
                return lax.dot_general(a_ref[...].astype(MXU_DTYPE), w_ref[...], w_dims, preferred_element_type=F32)
            out = None
            for t in range(sb):
                wt = w_ref[:, t * seg:(t + 1) * seg] if trans_w else w_ref[t * seg:(t + 1) * seg, :]
                d = lax.dot_general(a_ref[t].astype(MXU_DTYPE), wt, w_dims, preferred_element_type=F32)
                out = d if out is None else out + d
            return out

        if nk == 1:
            o_ref[...] = partial_product() if res is None else partial_product() + r_ref[...]
            return
        acc = refs[-1]
        kk = pl.program_id(2)

        @pl.when(kk == 0)
        def _():
            acc[...] = jnp.zeros_like(acc)

        acc[...] += partial_product()

        @pl.when(kk == nk - 1)
        def _():
            o_ref[...] = acc[...] if res is None else acc[...] + r_ref[...]

    a_spec = pl.BlockSpec((sb, tm, seg), lambda i, j, kk: (kk, i, 0)) if a_seg else pl.BlockSpec((tm, tk), lambda i, j, kk: (i, kk))
    w_spec = pl.BlockSpec((tn, tk), lambda i, j, kk: (j, kk)) if trans_w else pl.BlockSpec((tk, tn), lambda i, j, kk: (kk, j))
    in_specs = [a_spec, w_spec]
    args = [a, w]
    if res is not None:
        in_specs.append(pl.BlockSpec((tm, tn), lambda i, j, kk: (i, j)))
        args.append(res)
    return pl.pallas_call(
        body, name=name, grid=(m // tm, n // tn, nk), in_specs=in_specs,
        out_specs=pl.BlockSpec((tm, tn), lambda i, j, kk: (i, j)),
        out_shape=jax.ShapeDtypeStruct((m, n), F32),
        scratch_shapes=[pltpu.VMEM((tm, tn), F32)] if nk > 1 else [],
        compiler_params=_params(("parallel", "parallel", "arbitrary")),
    )(*args)


def _mm_tn(a, b, *, b_seg=False, out_dtype=None, name):
    out_dtype = out_dtype or GRAD_WIRE_DTYPE
    m, ka = a.shape
    if b_seg:
        s, _, seg = b.shape
        n = s * seg
    else:
        n = b.shape[1]
        seg = n
    tmc = _pick(m, (1408, 384, 128))
    best = None
    for tka in _divisor_tiles(ka):
        for tn in _divisor_tiles(seg):
            need = 2 * tmc * tka * a.dtype.itemsize + 2 * tmc * tn * b.dtype.itemsize + tka * tn * 4 + 2 * tka * tn * 4
            if need <= TILE_BUDGET_BYTES and (best is None or (tka * tn, tn) > best[:2]):
                best = (tka * tn, tn, tka)
    _, tn, tka = best
    nm = m // tmc

    def body(a_ref, b_ref, o_ref, acc):
        mm = pl.program_id(2)

        @pl.when(mm == 0)
        def _():
            acc[...] = jnp.zeros_like(acc)

        acc[...] += lax.dot_general(a_ref[...].astype(MXU_DTYPE), b_ref[...].astype(MXU_DTYPE),
                                    (((0,), (0,)), ((), ())), preferred_element_type=F32)

        @pl.when(mm == nm - 1)
        def _():
            o_ref[...] = acc[...].astype(o_ref.dtype)

    if b_seg:
        per = seg // tn
        b_spec = pl.BlockSpec((None, tmc, tn), lambda i, j, mm: (j // per, mm, j % per))
    else:
        b_spec = pl.BlockSpec((tmc, tn), lambda i, j, mm: (mm, j))
    return pl.pallas_call(
        body, name=name, grid=(ka // tka, n // tn, nm),
        in_specs=[pl.BlockSpec((tmc, tka), lambda i, j, mm: (mm, i)), b_spec],
        out_specs=pl.BlockSpec((tka, tn), lambda i, j, mm: (i, j)),
        out_shape=jax.ShapeDtypeStruct((ka, n), out_dtype),
        scratch_shapes=[pltpu.VMEM((tka, tn), F32)],
        compiler_params=_params(("parallel", "parallel", "arbitrary")),
    )(a, b)


def _rmsnorm_bwd(dhn, h, nw, dres, *, name):
    m, d = h.shape
    tm = _pick(m, (384, 128))

    def body(dhn_ref, h_ref, nw_ref, dres_ref, dh_ref, dnw_ref):
        i = pl.program_id(0)
        x = h_ref[...]
        r = lax.rsqrt(jnp.mean(x * x, -1, keepdims=True) + EPS)
        xh = x * r
        dy = dhn_ref[...]
        dxh = dy * nw_ref[...]
        dx = r * (dxh - xh * jnp.mean(dxh * xh, -1, keepdims=True))
        row = i * tm + lax.broadcasted_iota(jnp.int32, (tm, 1), 0)
        dh_ref[...] = jnp.where(row >= PAD_ROWS, dres_ref[...] + dx, 0.0)

        @pl.when(i == 0)
        def _():
            dnw_ref[...] = jnp.zeros_like(dnw_ref)

        dnw_ref[...] += jnp.sum(dy * xh, 0, keepdims=True)

    return pl.pallas_call(
        body, name=name, grid=(m // tm,),
        in_specs=[pl.BlockSpec((tm, d), lambda i: (i, 0)), pl.BlockSpec((tm, d), lambda i: (i, 0)),
                  pl.BlockSpec((1, d), lambda i: (0, 0)), pl.BlockSpec((tm, d), lambda i: (i, 0))],
        out_specs=[pl.BlockSpec((tm, d), lambda i: (i, 0)), pl.BlockSpec((1, d), lambda i: (0, 0))],
        out_shape=[jax.ShapeDtypeStruct((m, d), F32), jax.ShapeDtypeStruct((1, d), F32)],
        compiler_params=_params(("arbitrary",)),
    )(dhn, h, nw, dres)


ROW_CHUNK = 248


def _row_chunks(m):
    out, s = [], SUBLANES
    while s < m:
        n = min(ROW_CHUNK, m - s)
        out.append((s, n))
        s += n
    return out


def _conv_at(load, w, width, s, n):
    acc = w[width - 1:width, :] * load(s, n)
    for j in range(width - 1):
        acc = acc + w[j:j + 1, :] * load(s - (width - 1 - j), n)
    return acc


def _conv_t_at(load, w, width, s, n):
    acc = w[width - 1:width, :] * load(s, n)
    for j in range(width - 1):
        acc = acc + w[j:j + 1, :] * load(s + (width - 1 - j), n)
    return acc


def _dconv_w(load_x, d, width, s, n):
    rows = [jnp.sum(d * load_x(s - (width - 1 - j), n), 0, keepdims=True) for j in range(width)]
    rows.append(jnp.zeros((SUBLANES - width, d.shape[1]), F32))
    return jnp.concatenate(rows, 0)


def _pad_w(w):
    return jnp.concatenate([w, jnp.zeros((SUBLANES - w.shape[0], w.shape[1]), w.dtype)], 0)


def _sigmoid(x):
    return 1.0 / (1.0 + jnp.exp(-x))


def _ffn_act_fwd(u, cw, *, name):
    _, m, f = u.shape
    cb = _pick(f, (256, 128))
    chunks = _row_chunks(m)

    def body(g_ref, v_ref, w_ref, o_ref):
        w = w_ref[...]
        o_ref[pl.ds(0, SUBLANES), :] = jnp.zeros((SUBLANES, cb), o_ref.dtype)
        for s, n in chunks:
            c = _conv_at(lambda a, b: g_ref[pl.ds(a, b), :], w, 3, s, n)
            o_ref[pl.ds(s, n), :] = (c * _sigmoid(c) * v_ref[pl.ds(s, n), :]).astype(o_ref.dtype)

    return pl.pallas_call(
        body, name=name, grid=(f // cb,),
        in_specs=[pl.BlockSpec((None, m, cb), lambda j: (0, 0, j)), pl.BlockSpec((None, m, cb), lambda j: (1, 0, j)),
                  pl.BlockSpec((SUBLANES, cb), lambda j: (0, j))],
        out_specs=pl.BlockSpec((m, cb), lambda j: (0, j)),
        out_shape=jax.ShapeDtypeStruct((m, f), MXU_DTYPE),
        compiler_params=_params(("parallel",)),
    )(u, u, cw)


def _ffn_act_bwd(da, u, cw, *, name):
    _, m, f = u.shape
    cb = LANES
    chunks = _row_chunks(m)

    def body(da_ref, g_ref, v_ref, w_ref, du_ref, dw_ref, dg_s):
        w = w_ref[...]
        zeros8 = jnp.zeros((SUBLANES, cb), F32)
        dg_s[pl.ds(0, SUBLANES), :] = zeros8
        dg_s[pl.ds(m, SUBLANES), :] = zeros8
        du_ref[0, pl.ds(0, SUBLANES), :] = zeros8.astype(du_ref.dtype)
        du_ref[1, pl.ds(0, SUBLANES), :] = zeros8.astype(du_ref.dtype)
        load_g = lambda a, b: g_ref[pl.ds(a, b), :]
        dw = jnp.zeros((SUBLANES, cb), F32)
        for s, n in chunks:
            c = _conv_at(load_g, w, 3, s, n)
            sg = _sigmoid(c)
            d = da_ref[pl.ds(s, n), :]
            du_ref[1, pl.ds(s, n), :] = (d * (c * sg)).astype(du_ref.dtype)
            dc = d * v_ref[pl.ds(s, n), :] * (sg * (1.0 + c * (1.0 - sg)))
            dg_s[pl.ds(s, n), :] = dc
            dw = dw + _dconv_w(load_g, dc, 3, s, n)
        dw_ref[...] = dw
        for s, n in chunks:
            du_ref[0, pl.ds(s, n), :] = _conv_t_at(lambda a, b: dg_s[pl.ds(a, b), :], w, 3, s, n).astype(du_ref.dtype)

    return pl.pallas_call(
        body, name=name, grid=(f // cb,),
        in_specs=[pl.BlockSpec((m, cb), lambda j: (0, j)), pl.BlockSpec((None, m, cb), lambda j: (0, 0, j)),
                  pl.BlockSpec((None, m, cb), lambda j: (1, 0, j)), pl.BlockSpec((SUBLANES, cb), lambda j: (0, j))],
        out_specs=[pl.BlockSpec((2, m, cb), lambda j: (0, 0, j)), pl.BlockSpec((SUBLANES, cb), lambda j: (0, j))],
        out_shape=[jax.ShapeDtypeStruct((2, m, f), MXU_DTYPE), jax.ShapeDtypeStruct((SUBLANES, f), F32)],
        scratch_shapes=[pltpu.VMEM((m + SUBLANES, cb), F32)],
        compiler_params=_params(("parallel",)),
    )(da, u, u, cw)


def _shortconv_fwd(pm, cw, y, *, name):
    _, m, seg = pm.shape
    cb = _pick(seg, (256, 128))
    chunks = _row_chunks(m)

    def body(gi_ref, go_ref, ah_ref, w_ref, y_in, o_ref):
        del y_in
        w = w_ref[...]
        o_ref[pl.ds(0, SUBLANES), :] = jnp.zeros((SUBLANES, cb), o_ref.dtype)
        load_m = lambda a, b: gi_ref[pl.ds(a, b), :] * ah_ref[pl.ds(a, b), :]
        for s, n in chunks:
            o_ref[pl.ds(s, n), :] = (go_ref[pl.ds(s, n), :] * _conv_at(load_m, w, 3, s, n)).astype(o_ref.dtype)

    return pl.pallas_call(
        body, name=name, grid=(seg // cb,),
        in_specs=[pl.BlockSpec((None, m, cb), lambda j: (0, 0, j)), pl.BlockSpec((None, m, cb), lambda j: (1, 0, j)),
                  pl.BlockSpec((None, m, cb), lambda j: (2, 0, j)), pl.BlockSpec((SUBLANES, cb), lambda j: (0, j)),
                  pl.BlockSpec(memory_space=pl.ANY)],
        out_specs=pl.BlockSpec((m, cb), lambda j: (0, j)),
        out_shape=jax.ShapeDtypeStruct(y.shape, y.dtype),
        input_output_aliases={4: 0},
        compiler_params=_params(("parallel",)),
    )(pm, pm, pm, cw, y)


def _shortconv_bwd(dy, pm, cw, dpm, *, name):
    _, m, seg = pm.shape
    cb = LANES
    chunks = _row_chunks(m)

    def body(dy_ref, gi_ref, go_ref, ah_ref, w_ref, dpm_in, dp_ref, dw_ref, dc_s):
        del dpm_in
        w = w_ref[...]
        zeros8 = jnp.zeros((SUBLANES, cb), F32)
        dc_s[pl.ds(0, SUBLANES), :] = zeros8
        dc_s[pl.ds(m, SUBLANES), :] = zeros8
        for t in range(3):
            dp_ref[t, pl.ds(0, SUBLANES), :] = zeros8.astype(dp_ref.dtype)
        load_m = lambda a, b: gi_ref[pl.ds(a, b), :] * ah_ref[pl.ds(a, b), :]
        dw = jnp.zeros((SUBLANES, cb), F32)
        for s, n in chunks:
            d = dy_ref[pl.ds(s, n), :]
            dp_ref[1, pl.ds(s, n), :] = (d * _conv_at(load_m, w, 3, s, n)).astype(dp_ref.dtype)
            dc = d * go_ref[pl.ds(s, n), :]
            dc_s[pl.ds(s, n), :] = dc
            dw = dw + _dconv_w(load_m, dc, 3, s, n)
        dw_ref[...] = dw
        for s, n in chunks:
            dm = _conv_t_at(lambda a, b: dc_s[pl.ds(a, b), :], w, 3, s, n)
            dp_ref[0, pl.ds(s, n), :] = (dm * ah_ref[pl.ds(s, n), :]).astype(dp_ref.dtype)
            dp_ref[2, pl.ds(s, n), :] = (dm * gi_ref[pl.ds(s, n), :]).astype(dp_ref.dtype)

    return pl.pallas_call(
        body, name=name, grid=(seg // cb,),
        in_specs=[pl.BlockSpec((m, cb), lambda j: (0, j)), pl.BlockSpec((None, m, cb), lambda j: (0, 0, j)),
                  pl.BlockSpec((None, m, cb), lambda j: (1, 0, j)), pl.BlockSpec((None, m, cb), lambda j: (2, 0, j)),
                  pl.BlockSpec((SUBLANES, cb), lambda j: (0, j)), pl.BlockSpec(memory_space=pl.ANY)],
        out_specs=[pl.BlockSpec((3, m, cb), lambda j: (0, 0, j)), pl.BlockSpec((SUBLANES, cb), lambda j: (0, j))],
        out_shape=[jax.ShapeDtypeStruct(dpm.shape, dpm.dtype), jax.ShapeDtypeStruct((SUBLANES, seg), F32)],
        scratch_shapes=[pltpu.VMEM((m + SUBLANES, cb), F32)],
        input_output_aliases={5: 0},
        compiler_params=_params(("parallel",)),
    )(dy, pm, pm, pm, cw, dpm)


def _dnpre_fwd(pm, cw, *, name):
    _, m, seg = pm.shape
    cb = _pick(seg, (256, 128))
    per = seg // cb
    chunks = _row_chunks(m)

    def body(x_ref, w_ref, o_ref):
        w = w_ref[...]
        o_ref[pl.ds(0, SUBLANES), :] = jnp.zeros((SUBLANES, cb), F32)
        for s, n in chunks:
            c = _conv_at(lambda a, b: x_ref[pl.ds(a, b), :], w, 4, s, n)
            o_ref[pl.ds(s, n), :] = c * _sigmoid(c)

    return pl.pallas_call(
        body, name=name, grid=(3 * per,),
        in_specs=[pl.BlockSpec((None, m, cb), lambda j: (3 + j // per, 0, j % per)), pl.BlockSpec((SUBLANES, cb), lambda j: (0, j))],
        out_specs=pl.BlockSpec((None, m, cb), lambda j: (j // per, 0, j % per)),
        out_shape=jax.ShapeDtypeStruct((3, m, seg), F32),
        compiler_params=_params(("parallel",)),
    )(pm, cw)


def _dnpre_bwd(dqkv, pm, cw, dpm, *, name):
    _, m, seg = pm.shape
    cb = _pick(seg, (256, 128))
    per = seg // cb
    chunks = _row_chunks(m)

    def body(d_ref, x_ref, w_ref, dpm_in, dp_ref, dw_ref, dc_s):
        del dpm_in
        w = w_ref[...]
        zeros8 = jnp.zeros((SUBLANES, cb), F32)
        dc_s[pl.ds(0, SUBLANES), :] = zeros8
        dc_s[pl.ds(m, SUBLANES), :] = zeros8
        dp_ref[pl.ds(0, SUBLANES), :] = zeros8.astype(dp_ref.dtype)
        load_x = lambda a, b: x_ref[pl.ds(a, b), :]
        dw = jnp.zeros((SUBLANES, cb), F32)
        for s, n in chunks:
            c = _conv_at(load_x, w, 4, s, n)
            sg = _sigmoid(c)
            dc = d_ref[pl.ds(s, n), :] * (sg * (1.0 + c * (1.0 - sg)))
            dc_s[pl.ds(s, n), :] = dc
            dw = dw + _dconv_w(load_x, dc, 4, s, n)
        dw_ref[...] = dw
        for s, n in chunks:
            dp_ref[pl.ds(s, n), :] = _conv_t_at(lambda a, b: dc_s[pl.ds(a, b), :], w, 4, s, n).astype(dp_ref.dtype)

    return pl.pallas_call(
        body, name=name, grid=(3 * per,),
        in_specs=[pl.BlockSpec((None, m, cb), lambda j: (j // per, 0, j % per)),
                  pl.BlockSpec((None, m, cb), lambda j: (3 + j // per, 0, j % per)),
                  pl.BlockSpec((SUBLANES, cb), lambda j: (0, j)), pl.BlockSpec(memory_space=pl.ANY)],
        out_specs=[pl.BlockSpec((None, m, cb), lambda j: (3 + j // per, 0, j % per)), pl.BlockSpec((SUBLANES, cb), lambda j: (0, j))],
        out_shape=[jax.ShapeDtypeStruct(dpm.shape, dpm.dtype), jax.ShapeDtypeStruct((SUBLANES, 3 * seg), F32)],
        scratch_shapes=[pltpu.VMEM((m + SUBLANES, cb), F32)],
        input_output_aliases={3: 0},
        compiler_params=_params(("parallel",)),
    )(dqkv, pm, cw, dpm)


def _mxu_dot_impl(a, b, form):
    a = a.astype(MXU_DTYPE)
    b = b.astype(MXU_DTYPE)
    dims = {"nn": (((1,), (0,)), ((), ())), "nt": (((1,), (1,)), ((), ())), "tn": (((0,), (0,)), ((), ()))}[form]
    return lax.dot_general(a, b, dims, preferred_element_type=F32)


@functools.partial(jax.custom_vjp, nondiff_argnums=(2,))
def _mxu_dot(a, b, form):
    return _mxu_dot_impl(a, b, form)


def _mxu_dot_fwd(a, b, form):
    return _mxu_dot_impl(a, b, form), (a, b)


def _mxu_dot_bwd(form, saved, g):
    a, b = saved
    if form == "nn":
        return _mxu_dot_impl(g, b, "nt"), _mxu_dot_impl(a, g, "tn")
    if form == "nt":
        return _mxu_dot_impl(g, b, "nn"), _mxu_dot_impl(g, a, "tn")
    return _mxu_dot_impl(b, g, "nt"), _mxu_dot_impl(a, g, "nn")


_mxu_dot.defvjp(_mxu_dot_fwd, _mxu_dot_bwd)


_DOT_DIMS = {"nn": (((1,), (0,)), ((), ())), "nt": (((1,), (1,)), ((), ())), "tn": (((0,), (0,)), ((), ()))}


def _split(x):
    hi = x.astype(BF16)
    return hi, (x - hi.astype(F32)).astype(BF16)


def _dot3_impl(a, b, form):
    dg = lambda p, q: lax.dot_general(p, q, _DOT_DIMS[form], preferred_element_type=F32)
    ah, al = _split(a)
    bh, bl = _split(b)
    return dg(ah, bh) + (dg(ah, bl) + dg(al, bh))


@functools.partial(jax.custom_vjp, nondiff_argnums=(2,))
def _dot3(a, b, form):
    return _dot3_impl(a, b, form)


def _dot3_fwd(a, b, form):
    return _dot3_impl(a, b, form), (a, b)


def _dot3_bwd(form, saved, g):
    a, b = saved
    if form == "nn":
        return _dot3_impl(g, b, "nt"), _dot3_impl(a, g, "tn")
    if form == "nt":
        return _dot3_impl(g, b, "nn"), _dot3_impl(g, a, "tn")
    return _dot3_impl(b, g, "nt"), _dot3_impl(a, g, "nn")


_dot3.defvjp(_dot3_fwd, _dot3_bwd)


def _hdot(a, b):
    return _dot3(a, b, "nn")


def _mask_dot(mask, x, form):
    dg = lambda q: lax.dot_general(mask.astype(BF16), q, _DOT_DIMS[form], preferred_element_type=F32)
    x1 = x.astype(BF16)
    r1 = x - x1.astype(F32)
    x2 = r1.astype(BF16)
    x3 = (r1 - x2.astype(F32)).astype(BF16)
    return dg(x1) + (dg(x2) + dg(x3))


def _decay_masks(c):
    row = lax.broadcasted_iota(jnp.int32, (c, c), 0)
    col = lax.broadcasted_iota(jnp.int32, (c, c), 1)
    return (row >= col).astype(F32), row <= col


def _decay_impl(gb):
    lower, upper = _decay_masks(gb.shape[0])
    return _mask_dot(lower, gb, "nn"), _mask_dot(jnp.ones_like(gb), jnp.where(upper, gb, 0.0), "nn")


@jax.custom_vjp
def _decay_matrices(gb):
    return _decay_impl(gb)


def _decay_fwd(gb):
    return _decay_impl(gb), None


def _decay_bwd(_, cts):
    gc, gr = cts
    lower, upper = _decay_masks(gc.shape[0])
    return (_mask_dot(lower, gc, "tn") + jnp.where(upper, _mask_dot(jnp.ones_like(gr), gr, "tn"), 0.0),)


_decay_matrices.defvjp(_decay_fwd, _decay_bwd)


def _softplus(x):
    return jnp.maximum(x, 0.0) + jnp.log(1.0 + jnp.exp(-jnp.abs(x)))


def _heads(f, *lists):
    return [f(*t) for t in zip(*lists)]


def _dn_chunk(qr, kr, v, z, braw, araw, alog, dtb, nw, state, valid):
    c = DN_CHUNK
    row = lax.broadcasted_iota(jnp.int32, (c, c), 0)
    col = lax.broadcasted_iota(jnp.int32, (c, c), 1)
    incl = row >= col
    strict = row > col
    eye = jnp.where(row == col, 1.0, 0.0)
    q = _heads(lambda t: t * lax.rsqrt(jnp.sum(t * t, -1, keepdims=True) + EPS) * (DN_HEAD_DIM ** -0.5), qr)
    k = _heads(lambda t: t * lax.rsqrt(jnp.sum(t * t, -1, keepdims=True) + EPS), kr)
    beta = _heads(lambda t: _sigmoid(t) * valid, braw)
    g = _heads(lambda al, ar, dt: -jnp.exp(al) * _softplus(ar + dt) * valid, alog, araw, dtb)
    decay = _heads(lambda t: _decay_matrices(jnp.broadcast_to(t, (c, c))), g)
    dmask = _heads(lambda d: jnp.where(incl, jnp.exp(jnp.where(incl, d[0] - d[1], 0.0)), 0.0), decay)
    dec = _heads(lambda d: d[0][:, :1], decay)
    dlast = _heads(lambda d: d[0][c - 1:c, :1], decay)
    kk = _heads(lambda t: _mxu_dot(t, t, "nt"), k)
    a = _heads(lambda b, t, d: jnp.where(strict, b * t * d, 0.0), beta, kk, dmask)
    x = _heads(lambda t: eye - t, a)
    p = _heads(_hdot, a, a)
    for it in range(5):
        x = _heads(lambda s, t: s + _hdot(s, t), x, p)
        if it < 4:
            p = _heads(_hdot, p, p)
    u = _heads(lambda s, t, b: _hdot(s, t * b), x, v, beta)
    w = _heads(lambda s, t, b, d: _hdot(s, t * (b * jnp.exp(d))), x, k, beta, dec)
    qk = _heads(lambda s, t, d: _mxu_dot(s, t, "nt") * d, q, k, dmask)
    q_dec = _heads(lambda t, d: t * jnp.exp(d), q, dec)
    k_dec = _heads(lambda t, dl, d: t * jnp.exp(dl - d), k, dlast, dec)
    v_new = _heads(lambda s, t, st: s - _mxu_dot(t, st, "nn"), u, w, state)
    o = _heads(lambda qd, st, s, vn: _mxu_dot(qd, st, "nn") + _mxu_dot(s, vn, "nn"), q_dec, state, qk, v_new)
    new_state = _heads(lambda st, dl, kd, vn: st * jnp.exp(dl) + _mxu_dot(kd, vn, "tn"), state, dlast, k_dec, v_new)
    y = _heads(lambda t, zz: _rms(t, nw) * (zz * _sigmoid(zz)), o, z)
    return y, new_state


def _dn_valid(n):
    row = n * DN_CHUNK + lax.broadcasted_iota(jnp.int32, (DN_CHUNK, 1), 0)
    return (row >= PAD_ROWS).astype(F32)


def _dn_in_specs(rev, nc):
    cn = (lambda n: nc - 1 - n) if rev else (lambda n: n)
    c, hd = DN_CHUNK, DN_HEAD_DIM
    return [
        pl.BlockSpec((None, c, DN_DIM), lambda n: (0, cn(n), 0)),
        pl.BlockSpec((None, c, DN_DIM), lambda n: (1, cn(n), 0)),
        pl.BlockSpec((None, c, DN_DIM), lambda n: (2, cn(n), 0)),
        pl.BlockSpec((None, c, DN_DIM), lambda n: (6, cn(n), 0)),
        pl.BlockSpec((DN_HEADS, 2, c, 1), lambda n: (0, 0, cn(n), 0)),
        pl.BlockSpec((DN_HEADS, SUBLANES, LANES), lambda n: (0, 0, 0)),
        pl.BlockSpec((1, hd), lambda n: (0, 0)),
    ]


def _head(ref_or_val, h):
    return ref_or_val[:, h * DN_HEAD_DIM:(h + 1) * DN_HEAD_DIM]


def _dn_load(q_ref, k_ref, v_ref, z_ref, ba_ref, hp_ref):
    heads = range(DN_HEADS)
    return ([_head(q_ref, h) for h in heads], [_head(k_ref, h) for h in heads], [_head(v_ref, h) for h in heads],
            [_head(z_ref, h) for h in heads], [ba_ref[h, 0] for h in heads], [ba_ref[h, 1] for h in heads],
            [hp_ref[h, 0:1, 0:1] for h in heads], [hp_ref[h, 1:2, 0:1] for h in heads])


def _delta_fwd(qkvc, pm, ba, hp, nw, *, name):
    _, m, _ = qkvc.shape
    nc = m // DN_CHUNK
    hd = DN_HEAD_DIM

    def body(q_ref, k_ref, v_ref, z_ref, ba_ref, hp_ref, nw_ref, y_ref, s_ref, state):
        n = pl.program_id(0)

        @pl.when(n == 0)
        def _():
            state[...] = jnp.zeros_like(state)

        heads = range(DN_HEADS)
        old = [state[h] for h in heads]
        y, new = _dn_chunk(*_dn_load(q_ref, k_ref, v_ref, z_ref, ba_ref, hp_ref), nw_ref[...], old, _dn_valid(n))
        for h in heads:
            s_ref[h] = old[h]
            y_ref[:, h * hd:(h + 1) * hd] = y[h].astype(y_ref.dtype)
            state[h] = new[h]

    return pl.pallas_call(
        body, name=name, grid=(nc,), in_specs=_dn_in_specs(False, nc),
        out_specs=[pl.BlockSpec((DN_CHUNK, DN_DIM), lambda n: (n, 1)), pl.BlockSpec((DN_HEADS, None, hd, hd), lambda n: (0, n, 0, 0))],
        out_shape=[jax.ShapeDtypeStruct((m, D_CONV + DN_DIM), MXU_DTYPE), jax.ShapeDtypeStruct((DN_HEADS, nc, hd, hd), F32)],
        scratch_shapes=[pltpu.VMEM((DN_HEADS, hd, hd), F32)],
        compiler_params=_params(("arbitrary",)),
    )(qkvc, qkvc, qkvc, pm, ba, hp, nw)


def _delta_bwd(dy, qkvc, pm, ba, hp, nw, states, *, name):
    _, m, _ = qkvc.shape
    nc = m // DN_CHUNK
    hd, c = DN_HEAD_DIM, DN_CHUNK

    def body(q_ref, k_ref, v_ref, z_ref, ba_ref, hp_ref, nw_ref, s_ref, dy_ref,
             dz_ref, dqkv_ref, dba_ref, dhp_ref, dnw_ref, dstate):
        step = pl.program_id(0)
        n = nc - 1 - step

        @pl.when(step == 0)
        def _():
            dstate[...] = jnp.zeros_like(dstate)
            dhp_ref[...] = jnp.zeros_like(dhp_ref)
            dnw_ref[...] = jnp.zeros_like(dnw_ref)

        valid = _dn_valid(n)
        heads = range(DN_HEADS)
        fn = lambda *a: _dn_chunk(*a, valid)
        _, vjp = jax.vjp(fn, *_dn_load(q_ref, k_ref, v_ref, z_ref, ba_ref, hp_ref), nw_ref[...], [s_ref[h] for h in heads])
        dq, dk, dv, dz, dbr, dar, dalog, ddtb, dnw, dst = vjp(([_head(dy_ref, h) for h in heads], [dstate[h] for h in heads]))
        for h in heads:
            cols = slice(h * hd, (h + 1) * hd)
            dqkv_ref[0, :, cols] = dq[h]
            dqkv_ref[1, :, cols] = dk[h]
            dqkv_ref[2, :, cols] = dv[h]
            dz_ref[:, cols] = dz[h].astype(dz_ref.dtype)
            dba_ref[h, 0] = dbr[h]
            dba_ref[h, 1] = dar[h]
            dstate[h] = dst[h]
            dhp_ref[h] += jnp.concatenate([jnp.broadcast_to(dalog[h], (1, LANES)), jnp.broadcast_to(ddtb[h], (1, LANES)),
                                           jnp.zeros((SUBLANES - 2, LANES), F32)], 0)
        dnw_ref[...] += dnw

    rn = lambda n: nc - 1 - n
    in_specs = _dn_in_specs(True, nc) + [
        pl.BlockSpec((DN_HEADS, None, hd, hd), lambda n: (0, rn(n), 0, 0)),
        pl.BlockSpec((c, DN_DIM), lambda n: (rn(n), 1)),
    ]
    out_specs = [
        pl.BlockSpec((None, c, DN_DIM), lambda n: (6, rn(n), 0)),
        pl.BlockSpec((3, c, DN_DIM), lambda n: (0, rn(n), 0)),
        pl.BlockSpec((DN_HEADS, 2, c, 1), lambda n: (0, 0, rn(n), 0)),
        pl.BlockSpec((DN_HEADS, SUBLANES, LANES), lambda n: (0, 0, 0)),
        pl.BlockSpec((1, hd), lambda n: (0, 0)),
    ]
    return pl.pallas_call(
        body, name=name, grid=(nc,), in_specs=in_specs, out_specs=out_specs,
        out_shape=[jax.ShapeDtypeStruct(pm.shape, MXU_DTYPE), jax.ShapeDtypeStruct(qkvc.shape, F32),
                   jax.ShapeDtypeStruct(ba.shape, F32), jax.ShapeDtypeStruct(hp.shape, F32),
                   jax.ShapeDtypeStruct((1, hd), F32)],
        scratch_shapes=[pltpu.VMEM((DN_HEADS, hd, hd), F32)],
        compiler_params=_params(("arbitrary",)),
    )(qkvc, qkvc, qkvc, pm, ba, hp, nw, states, dy)


def _attn_block(q4, k0, kp, kc, v0, vp, vc, qw, kw, sink, n):
    g, b, hd = SWA_GROUP, SWA_BLOCK, SWA_HEAD_DIM
    qn = _rms(q4, qw) * (hd ** -0.5)
    kn = _rms(jnp.concatenate([k0, kp, kc], 0), kw)
    vcat = jnp.concatenate([v0, vp, vc], 0)
    s = _mxu_dot(qn.reshape(g * b, hd), kn, "nt").reshape(g, b, 3 * b)
    i = lax.broadcasted_iota(jnp.int32, (b, 3 * b), 0)
    c = lax.broadcasted_iota(jnp.int32, (b, 3 * b), 1)
    in_meta, in_prev, in_cur = c < b, (c >= b) & (c < 2 * b), c >= 2 * b
    j = c - jnp.where(in_meta, 0, jnp.where(in_prev, b, 2 * b))
    meta_lo = jnp.where(n == 0, b, PAD_ROWS)
    cur_lo = jnp.where(n == 0, PAD_ROWS, 0)
    prev_off = jnp.where(n >= 2, 0, 2 * b)
    valid = (in_meta & (j >= meta_lo)) | (in_prev & (j > i + prev_off)) | (in_cur & (j <= i) & (j >= cur_lo))
    s = jnp.where(valid[None], s, NEG)
    m = lax.stop_gradient(jnp.maximum(jnp.max(s, -1, keepdims=True), sink))
    e = jnp.exp(s - m)
    p = e / (jnp.sum(e, -1, keepdims=True) + jnp.exp(sink - m))
    return _mxu_dot(p.reshape(g * b, 3 * b), vcat, "nn").reshape(g, b, hd)


def _attn_in_specs():
    g, b, hd = SWA_GROUP, SWA_BLOCK, SWA_HEAD_DIM
    kv = lambda f: pl.BlockSpec((None, b, hd), lambda kh, n: (kh, f(n), 0))
    blocks = [lambda n: 0, lambda n: jnp.maximum(n - 1, 0), lambda n: n]
    return ([pl.BlockSpec((g, b, hd), lambda kh, n: (kh, n, 0))] + [kv(f) for f in blocks] + [kv(f) for f in blocks]
            + [pl.BlockSpec((1, hd), lambda kh, n: (0, 0)), pl.BlockSpec((1, hd), lambda kh, n: (0, 0)),
               pl.BlockSpec((None, g, 1, 1), lambda kh, n: (kh, 0, 0, 0))])


def _attn_fwd(q3, k3, v3, qw, kw, sink, *, name):
    _, m, hd = q3.shape
    g, b = SWA_GROUP, SWA_BLOCK

    def body(q_ref, k0, kp, kc, v0, vp, vc, qw_ref, kw_ref, s_ref, o_ref):
        o_ref[...] = _attn_block(q_ref[...], k0[...], kp[...], kc[...], v0[...], vp[...], vc[...], qw_ref[...], kw_ref[...],
                                 s_ref[...], pl.program_id(1))

    return pl.pallas_call(
        body, name=name, grid=(SWA_KV_HEADS, m // b), in_specs=_attn_in_specs(),
        out_specs=pl.BlockSpec((g, b, hd), lambda kh, n: (kh, n, 0)),
        out_shape=jax.ShapeDtypeStruct(q3.shape, F32),
        compiler_params=_params(("parallel", "parallel")),
    )(q3, k3, k3, k3, v3, v3, v3, qw, kw, sink)


def _attn_bwd(do3, q3, k3, v3, qw, kw, sink, *, name):
    _, m, hd = q3.shape
    g, b = SWA_GROUP, SWA_BLOCK

    def body(q_ref, k0, kp, kc, v0, vp, vc, qw_ref, kw_ref, s_ref, do_ref, dq_ref, dk_ref, dv_ref, dqw_ref, dkw_ref, ds_ref):
        n = pl.program_id(1)

        @pl.when(n == 0)
        def _():
            for r in (dk_ref, dv_ref, dqw_ref, dkw_ref, ds_ref):
                r[...] = jnp.zeros_like(r)

        fn = lambda *a: _attn_block(*a, n)
        _, vjp = jax.vjp(fn, q_ref[...], k0[...], kp[...], kc[...], v0[...], vp[...], vc[...], qw_ref[...], kw_ref[...], s_ref[...])
        dq, dk0, dkp, dkc, dv0, dvp, dvc, dqw, dkw, dsk = vjp(do_ref[...])
        dq_ref[...] = dq
        prev = pl.multiple_of(jnp.maximum(n - 1, 0) * b, b)
        cur = pl.multiple_of(n * b, b)
        for ref, parts in ((dk_ref, (dk0, dkp, dkc)), (dv_ref, (dv0, dvp, dvc))):
            ref[pl.ds(0, b), :] += parts[0]
            ref[pl.ds(prev, b), :] += parts[1]
            ref[pl.ds(cur, b), :] += parts[2]
        dqw_ref[...] += dqw
        dkw_ref[...] += dkw
        ds_ref[...] += dsk

    acc = lambda shape: pl.BlockSpec((None,) + shape, lambda kh, n: (kh,) + (0,) * len(shape))
    return pl.pallas_call(
        body, name=name, grid=(SWA_KV_HEADS, m // b),
        in_specs=_attn_in_specs() + [pl.BlockSpec((g, b, hd), lambda kh, n: (kh, n, 0))],
        out_specs=[pl.BlockSpec((g, b, hd), lambda kh, n: (kh, n, 0)), acc((m, hd)), acc((m, hd)), acc((1, hd)), acc((1, hd)),
                   acc((g, 1, 1))],
        out_shape=[jax.ShapeDtypeStruct(q3.shape, F32), jax.ShapeDtypeStruct(k3.shape, F32), jax.ShapeDtypeStruct(v3.shape, F32),
                   jax.ShapeDtypeStruct((SWA_KV_HEADS, 1, hd), F32), jax.ShapeDtypeStruct((SWA_KV_HEADS, 1, hd), F32),
                   jax.ShapeDtypeStruct(sink.shape, F32)],
        compiler_params=_params(("parallel", "arbitrary")),
    )(q3, k3, k3, k3, v3, v3, v3, qw, kw, sink, do3)


def _loss_bwd(h, target, *, name):
    m, d = h.shape
    b = SWA_BLOCK

    def body(h_ref, t_ref, l_ref, dh_ref):
        i = pl.program_id(0)

        @pl.when(i == 0)
        def _():
            l_ref[...] = jnp.zeros_like(l_ref)
            dh_ref[...] = jnp.zeros_like(dh_ref)

        @pl.when(i > 0)
        def _():
            e = h_ref[...] - t_ref[...]
            dh_ref[...] = e * (1.0 / d)
            l_ref[...] += jnp.sum(jnp.sum(e * e, 0, keepdims=True), 1, keepdims=True) * (0.5 / d)

    return pl.pallas_call(
        body, name=name, grid=(m // b,),
        in_specs=[pl.BlockSpec((b, d), lambda i: (i, 0)), pl.BlockSpec((b, d), lambda i: (jnp.maximum(i - 1, 0), 0))],
        out_specs=[pl.BlockSpec((1, LANES), lambda i: (0, 0)), pl.BlockSpec((b, d), lambda i: (i, 0))],
        out_shape=[jax.ShapeDtypeStruct((1, LANES), F32), jax.ShapeDtypeStruct((m, d), F32)],
        compiler_params=_params(("arbitrary",)),
    )(h, target)


def _heads_major(a, heads):
    m = a.shape[0]
    return a.reshape(m, heads, SWA_HEAD_DIM).transpose(1, 0, 2)


def _heads_minor(a3):
    heads, m, hd = a3.shape
    return a3.transpose(1, 0, 2).reshape(m, heads * hd)


def _ffn_fwd(h, nw, w_up, cw, w_down, tag):
    u, hn = _norm_matmul(h, nw, w_up, o_seg=D_FF, name=f"ffn_up_{tag}")
    a = _ffn_act_fwd(u, cw, name=f"ffn_act_{tag}")
    return _mm_nn(a, w_down, res=h, name=f"ffn_down_{tag}"), (h, hn, u, a)


def _ffn_bwd(dh, saved, nw, w_up, cw, w_down, tag):
    h, hn, u, a = saved
    da = _mm_nn(dh, w_down, trans_w=True, name=f"ffn_da_{tag}")
    dw_down = _mm_tn(a, dh, name=f"ffn_dwdown_{tag}")
    du, dcw = _ffn_act_bwd(da, u, cw, name=f"ffn_act_bwd_{tag}")
    dhn = _mm_nn(du, w_up, a_seg=True, trans_w=True, name=f"ffn_dhn_{tag}")
    dw_up = _mm_tn(hn, du, b_seg=True, name=f"ffn_dwup_{tag}")
    dh_in, dnw = _rmsnorm_bwd(dhn, h, nw, dh, name=f"ffn_norm_bwd_{tag}")
    return dh_in, dnw, dw_up, dcw, dw_down


def _local_step(x, target, w):
    seq, d = x.shape
    m = PAD_ROWS + N_META + seq
    h0 = jnp.concatenate([jnp.zeros((PAD_ROWS, d), F32), w["meta"], x], 0)

    pm, hn0 = _norm_matmul(h0, w["anw"][0], w["w_in_main"], o_seg=SEG, name="mix_in")
    pba = _mm_nn(hn0, w["w_in_tail"], name="mix_in_tail")
    qkvc = _dnpre_fwd(pm, w["dcw"], name="dn_conv")
    ba = pba[:, :2 * DN_HEADS].T.reshape(2, DN_HEADS, m, 1).transpose(1, 0, 2, 3)
    y, states = _delta_fwd(qkvc, pm, ba, w["hp"], w["dnw"], name="delta")
    y = _shortconv_fwd(pm, w["caw"], y, name="shortconv")
    h1 = _mm_nn(y, w["w_out"], res=h0, name="mix_out")
    h2, ffn0 = _ffn_fwd(h1, w["fnw"][0], w["w_up"][0], w["fcw"][0], w["w_down"][0], "l0")

    qkv, hn2 = _norm_matmul(h2, w["anw"][1], w["wqkv"], name="attn_qkv")
    nq, nkv = SWA_HEADS * SWA_HEAD_DIM, SWA_KV_HEADS * SWA_HEAD_DIM
    q3 = _heads_major(qkv[:, :nq], SWA_HEADS)
    k3 = _heads_major(qkv[:, nq:nq + nkv], SWA_KV_HEADS)
    v3 = _heads_major(qkv[:, nq + nkv:], SWA_KV_HEADS)
    o3 = _attn_fwd(q3, k3, v3, w["qnw"], w["knw"], w["sink"], name="attn")
    o = _heads_minor(o3).astype(MXU_DTYPE)
    h3 = _mm_nn(o, w["wo"], res=h2, name="attn_out")
    h4, ffn1 = _ffn_fwd(h3, w["fnw"][1], w["w_up"][1], w["fcw"][1], w["w_down"][1], "l1")

    loss, dh4 = _loss_bwd(h4, target, name="loss")

    g = {}
    dh3, dfnw1, dwup1, dfcw1, dwdown1 = _ffn_bwd(dh4, ffn1, w["fnw"][1], w["w_up"][1], w["fcw"][1], w["w_down"][1], "l1")

    do = _mm_nn(dh3, w["wo"], trans_w=True, name="attn_do")
    g["wo"] = _mm_tn(o, dh3, name="attn_dwo")
    dq3, dk3, dv3, dqw, dkw, dsink = _attn_bwd(_heads_major(do, SWA_HEADS), q3, k3, v3, w["qnw"], w["knw"], w["sink"], name="attn_bwd")
    dqkv = jnp.concatenate([_heads_minor(dq3), _heads_minor(dk3), _heads_minor(dv3)], 1).astype(MXU_DTYPE)
    dhn2 = _mm_nn(dqkv, w["wqkv"], trans_w=True, name="attn_dhn")
    g["wqkv"] = _mm_tn(hn2, dqkv, name="attn_dwqkv")
    dh2, danw1 = _rmsnorm_bwd(dhn2, h2, w["anw"][1], dh3, name="attn_norm_bwd")

    dh1, dfnw0, dwup0, dfcw0, dwdown0 = _ffn_bwd(dh2, ffn0, w["fnw"][0], w["w_up"][0], w["fcw"][0], w["w_down"][0], "l0")

    dy = _mm_nn(dh1, w["w_out"], trans_w=True, name="mix_dy")
    g["w_out"] = _mm_tn(y, dh1, name="mix_dwout")
    dpm, dqkvc, dba, dhp, ddnw = _delta_bwd(dy, qkvc, pm, ba, w["hp"], w["dnw"], states, name="delta_bwd")
    dpm, ddcw = _dnpre_bwd(dqkvc, pm, w["dcw"], dpm, name="dn_conv_bwd")
    dpm, dcaw = _shortconv_bwd(dy, pm, w["caw"], dpm, name="shortconv_bwd")
    dpba = jnp.pad(dba.transpose(1, 0, 2, 3).reshape(2 * DN_HEADS, m).T, ((0, 0), (0, LANES - 2 * DN_HEADS))).astype(MXU_DTYPE)
    dhn0 = _mm_nn(dpm, w["w_in_main"], a_seg=True, trans_w=True, name="mix_dhn")
    dhn0 = _mm_nn(dpba, w["w_in_tail"], res=dhn0, trans_w=True, name="mix_dhn_tail")
    g["w_in_main"] = _mm_tn(hn0, dpm, b_seg=True, name="mix_dwin")
    g["w_in_tail"] = _mm_tn(hn0, dpba, name="mix_dwin_tail")
    dh0, danw0 = _rmsnorm_bwd(dhn0, h0, w["anw"][0], dh1, name="mix_norm_bwd")

    g.update(
        x=dh0[PAD_ROWS + N_META:], meta=dh0[PAD_ROWS:PAD_ROWS + N_META], anw=[danw0, danw1], fnw=[dfnw0, dfnw1],
        caw=dcaw, dcw=ddcw, hp=dhp, dnw=ddnw, qnw=jnp.sum(dqw, 0), knw=jnp.sum(dkw, 0), sink=dsink,
        w_up=[dwup0, dwup1], fcw=[dfcw0, dfcw1], w_down=[dwdown0, dwdown1])
    return loss, g


N_TAIL = 2 * DN_HEADS


def _prepare_weights(p):
    n_main = N_SEG * SEG
    w_in = p["mix_w_in"][0]
    tail = jnp.pad(w_in[:, n_main:], ((0, 0), (0, LANES - N_TAIL)))
    wqkv = jnp.concatenate([p["swa_wq"][0], p["swa_wk"][0], p["swa_wv"][0]], 1)
    hp = jnp.zeros((DN_HEADS, SUBLANES, LANES), F32)
    hp = hp.at[:, 0, :].set(p["dn_a_log"][0][:, None]).at[:, 1, :].set(p["dn_dt_bias"][0][:, None])
    depth = p["ffn_w_up"].shape[0]
    return dict(
        meta=p["meta_tokens"], anw=[p["attn_norm_w"][i:i + 1] for i in range(depth)],
        fnw=[p["ffn_norm_w"][i:i + 1] for i in range(depth)],
        w_in_main=w_in[:, :n_main], w_in_tail=tail,
        caw=_pad_w(p["conv_a_w"][0]), dcw=_pad_w(p["dn_conv_w"][0]), hp=hp, dnw=p["dn_norm_w"],
        w_out=p["mix_w_out"][0], wqkv=wqkv,
        qnw=p["swa_q_norm_w"], knw=p["swa_k_norm_w"], sink=p["swa_sinks"].reshape(SWA_KV_HEADS, SWA_GROUP, 1, 1),
        wo=p["swa_wo"][0], w_up=[p["ffn_w_up"][i] for i in range(depth)],
        fcw=[_pad_w(p["ffn_conv_w"][i]) for i in range(depth)], w_down=[p["ffn_w_down"][i] for i in range(depth)])


def _reference_named(g):
    nq, nkv = SWA_HEADS * SWA_HEAD_DIM, SWA_KV_HEADS * SWA_HEAD_DIM
    return dict(
        meta_tokens=g["meta"], attn_norm_w=jnp.concatenate(g["anw"], 0), ffn_norm_w=jnp.concatenate(g["fnw"], 0),
        mix_w_in=jnp.concatenate([g["w_in_main"], g["w_in_tail"][:, :N_TAIL]], 1)[None],
        conv_a_w=g["caw"][None, :3], dn_conv_w=g["dcw"][None, :4],
        dn_a_log=g["hp"][None, :, 0, 0], dn_dt_bias=g["hp"][None, :, 1, 0], dn_norm_w=g["dnw"],
        mix_w_out=g["w_out"][None], swa_wq=g["wqkv"][None, :, :nq], swa_wk=g["wqkv"][None, :, nq:nq + nkv],
        swa_wv=g["wqkv"][None, :, nq + nkv:], swa_q_norm_w=g["qnw"], swa_k_norm_w=g["knw"],
        swa_sinks=g["sink"].reshape(1, SWA_HEADS), swa_wo=g["wo"][None],
        ffn_w_up=jnp.stack(g["w_up"]), ffn_conv_w=jnp.stack([c[:3] for c in g["fcw"]]), ffn_w_down=jnp.stack(g["w_down"]))


def _my_index():
    return 4 * lax.axis_index("x") + 2 * lax.axis_index("y") + lax.axis_index("c")


def _all_to_all(arrays, *, name):
    n = len(arrays)

    def body(*refs):
        ins, outs = refs[:n], refs[n:2 * n]
        send_sems, recv_sems, local_sems = refs[2 * n:]
        x, y, c = lax.axis_index("x"), lax.axis_index("y"), lax.axis_index("c")
        me = 4 * x + 2 * y + c
        copies = []
        for i in range(n):
            local = pltpu.make_async_copy(ins[i].at[me], outs[i].at[me], local_sems.at[i])
            local.start()
            copies.append(local)
        for d in range(1, N_DEV):
            px, py, pc = x ^ (d >> 2), y ^ ((d >> 1) & 1), c ^ (d & 1)
            for i in range(n):
                remote = pltpu.make_async_remote_copy(
                    src_ref=ins[i].at[4 * px + 2 * py + pc], dst_ref=outs[i].at[me], send_sem=send_sems.at[i, d],
                    recv_sem=recv_sems.at[i, d], device_id=(px, py, pc), device_id_type=pl.DeviceIdType.MESH)
                remote.start()
                copies.append(remote)
        for cp in copies:
            cp.wait()

    hbm = pl.BlockSpec(memory_space=pl.ANY)
    return pl.pallas_call(
        body, name=name, in_specs=[hbm] * n, out_specs=[hbm] * n,
        out_shape=[jax.ShapeDtypeStruct(a.shape, a.dtype) for a in arrays],
        scratch_shapes=[pltpu.SemaphoreType.DMA((n, N_DEV)), pltpu.SemaphoreType.DMA((n, N_DEV)), pltpu.SemaphoreType.DMA((n,))],
    )(*arrays)


def _all_gather(arrays, *, name):
    n = len(arrays)

    def body(*refs):
        ins, outs = refs[:n], refs[n:2 * n]
        send_sems, recv_sems, local_sems = refs[2 * n:]
        x, y, c = lax.axis_index("x"), lax.axis_index("y"), lax.axis_index("c")
        me, sibling = (x, y, c), (x, y, 1 - c)
        chips = [(1 - x, y), (x, 1 - y), (1 - x, 1 - y)]

        def copy(i, k, block, to, src=None):
            rows = outs[i].at[4 * block[0] + 2 * block[1] + block[2]]
            return pltpu.make_async_remote_copy(
                src_ref=rows if src is None else src, dst_ref=rows, send_sem=send_sems.at[i, k], recv_sem=recv_sems.at[i, k],
                device_id=to, device_id_type=pl.DeviceIdType.MESH)

        mine = [pltpu.make_async_copy(ins[i], outs[i].at[4 * x + 2 * y + c], local_sems.at[i]) for i in range(n)]
        first = []
        for j, chip in enumerate(chips):
            first += [copy(i, 1 + j, me, (*chip, c), src=ins[i]) for i in range(n)]
        first += [copy(i, 0, me, sibling, src=ins[i]) for i in range(n)]
        for cp in first + mine:
            cp.start()
        passed = []
        for j, chip in enumerate(chips):
            for i in range(n):
                copy(i, 1 + j, (*chip, c), me).wait_recv()
                fwd = copy(i, 4 + j, (*chip, c), sibling)
                fwd.start()
                passed.append(fwd)
        for i in range(n):
            copy(i, 0, sibling, me).wait_recv()
            for j, chip in enumerate(chips):
                copy(i, 4 + j, (*chip, 1 - c), me).wait_recv()
        for cp in first + passed:
            cp.wait_send()
        for cp in mine:
            cp.wait()

    hbm = pl.BlockSpec(memory_space=pl.ANY)
    return pl.pallas_call(
        body, name=name, in_specs=[hbm] * n, out_specs=[hbm] * n,
        out_shape=[jax.ShapeDtypeStruct((N_DEV,) + tuple(a.shape), a.dtype) for a in arrays],
        scratch_shapes=[pltpu.SemaphoreType.DMA((n, 7)), pltpu.SemaphoreType.DMA((n, 7)), pltpu.SemaphoreType.DMA((n,))],
    )(*arrays)


def _row_tile(r, c):
    best = None
    for t in range(2 * SUBLANES, r + 1, 2 * SUBLANES):
        if r % t == 0 and N_DEV * t * c * 4 <= 6 * 1024 * 1024:
            best = t
    return best or r


def _adamw(parts, w, m, v, *, name):
    r, c = w.shape
    tr = _row_tile(r, c)

    def body(p_ref, w_ref, m_ref, v_ref, g_ref, d_ref, nm_ref, nv_ref):
        g = p_ref[0].astype(F32)
        for j in range(1, N_DEV):
            g = g + p_ref[j].astype(F32)
        m2 = ADAM_B1 * m_ref[...] + (1.0 - ADAM_B1) * g
        v2 = ADAM_B2 * v_ref[...] + (1.0 - ADAM_B2) * jnp.square(g)
        m_hat = m2 / (1.0 - ADAM_B1 ** ADAM_STEP)
        v_hat = v2 / (1.0 - ADAM_B2 ** ADAM_STEP)
        g_ref[...] = g
        d_ref[...] = -ADAM_LR * (m_hat / (jnp.sqrt(v_hat) + ADAM_EPS) + ADAM_WD * w_ref[...])
        nm_ref[...] = m2
        nv_ref[...] = v2

    blk = pl.BlockSpec((tr, c), lambda i: (i, 0))
    out = jax.ShapeDtypeStruct((r, c), F32)
    return pl.pallas_call(
        body, name=name, grid=(r // tr,),
        in_specs=[pl.BlockSpec((N_DEV, tr, c), lambda i: (0, i, 0)), blk, blk, blk],
        out_specs=[blk, blk, blk, blk], out_shape=[out, out, out, out],
        compiler_params=_params(("parallel",)),
    )(parts, w, m, v)


SHARD_AXIS = dict(
    meta_tokens=1, attn_norm_w=None, ffn_norm_w=None, mix_w_in=2, conv_a_w=2, dn_conv_w=2, dn_a_log=None, dn_dt_bias=None,
    dn_norm_w=None, mix_w_out=1, swa_wq=1, swa_wk=1, swa_wv=1, swa_q_norm_w=None, swa_k_norm_w=None, swa_sinks=None,
    swa_wo=1, ffn_w_up=2, ffn_conv_w=2, ffn_w_down=1)
WEIGHTS = list(SHARD_AXIS)
BIG = ["mix_w_in", "mix_w_out", "swa_wq", "swa_wk", "swa_wv", "swa_wo", "ffn_w_up", "ffn_w_down"]
SMALL = [k for k in WEIGHTS if k not in BIG]
SMALL_SHARDED = [k for k in SMALL if SHARD_AXIS[k] is not None]


def _whole(g8, axis):
    t = jnp.moveaxis(g8, 0, axis)
    return t.reshape(t.shape[:axis] + (t.shape[axis] * t.shape[axis + 1],) + t.shape[axis + 2:])


def _by_owner(a, axis):
    s = a.shape[axis] // N_DEV
    return jnp.moveaxis(a.reshape(a.shape[:axis] + (N_DEV, s) + a.shape[axis + 1:]), axis, 0)


def _pack(arrays, lead=0):
    flat = jnp.concatenate([a.reshape(a.shape[:lead] + (-1,)) for a in arrays], -1)
    n = flat.shape[-1]
    rows = -(-n // (SUBLANES * LANES)) * SUBLANES
    flat = jnp.pad(flat, [(0, 0)] * lead + [(0, rows * LANES - n)])
    return flat.reshape(flat.shape[:lead] + (rows, LANES))


def _unpack(buf, shapes, lead=0):
    flat = buf.reshape(buf.shape[:lead] + (-1,))
    out, o = [], 0
    for s in shapes:
        n = 1
        for e in s:
            n *= e
        out.append(flat[..., o:o + n].reshape(buf.shape[:lead] + tuple(s)))
        o += n
    return out


def kernel(x, meta_tokens, attn_norm_w, ffn_norm_w, mix_w_in, conv_a_w, dn_conv_w, dn_a_log, dn_dt_bias, dn_norm_w, mix_w_out, swa_wq, swa_wk, swa_wv, swa_q_norm_w, swa_k_norm_w, swa_sinks, swa_wo, ffn_w_up, ffn_conv_w, ffn_w_down, loss_target, m_meta_tokens, m_attn_norm_w, m_ffn_norm_w, m_mix_w_in, m_conv_a_w, m_dn_conv_w, m_dn_a_log, m_dn_dt_bias, m_dn_norm_w, m_mix_w_out, m_swa_wq, m_swa_wk, m_swa_wv, m_swa_q_norm_w, m_swa_k_norm_w, m_swa_sinks, m_swa_wo, m_ffn_w_up, m_ffn_conv_w, m_ffn_w_down, v_meta_tokens, v_attn_norm_w, v_ffn_norm_w, v_mix_w_in, v_conv_a_w, v_dn_conv_w, v_dn_a_log, v_dn_dt_bias, v_dn_norm_w, v_mix_w_out, v_swa_wq, v_swa_wk, v_swa_wv, v_swa_q_norm_w, v_swa_k_norm_w, v_swa_sinks, v_swa_wo, v_ffn_w_up, v_ffn_conv_w, v_ffn_w_down):
    w = dict(meta_tokens=meta_tokens, attn_norm_w=attn_norm_w, ffn_norm_w=ffn_norm_w, mix_w_in=mix_w_in, conv_a_w=conv_a_w, dn_conv_w=dn_conv_w, dn_a_log=dn_a_log, dn_dt_bias=dn_dt_bias, dn_norm_w=dn_norm_w, mix_w_out=mix_w_out, swa_wq=swa_wq, swa_wk=swa_wk, swa_wv=swa_wv, swa_q_norm_w=swa_q_norm_w, swa_k_norm_w=swa_k_norm_w, swa_sinks=swa_sinks, swa_wo=swa_wo, ffn_w_up=ffn_w_up, ffn_conv_w=ffn_conv_w, ffn_w_down=ffn_w_down)
    mom = dict(meta_tokens=m_meta_tokens, attn_norm_w=m_attn_norm_w, ffn_norm_w=m_ffn_norm_w, mix_w_in=m_mix_w_in, conv_a_w=m_conv_a_w, dn_conv_w=m_dn_conv_w, dn_a_log=m_dn_a_log, dn_dt_bias=m_dn_dt_bias, dn_norm_w=m_dn_norm_w, mix_w_out=m_mix_w_out, swa_wq=m_swa_wq, swa_wk=m_swa_wk, swa_wv=m_swa_wv, swa_q_norm_w=m_swa_q_norm_w, swa_k_norm_w=m_swa_k_norm_w, swa_sinks=m_swa_sinks, swa_wo=m_swa_wo, ffn_w_up=m_ffn_w_up, ffn_conv_w=m_ffn_conv_w, ffn_w_down=m_ffn_w_down)
    var = dict(meta_tokens=v_meta_tokens, attn_norm_w=v_attn_norm_w, ffn_norm_w=v_ffn_norm_w, mix_w_in=v_mix_w_in, conv_a_w=v_conv_a_w, dn_conv_w=v_dn_conv_w, dn_a_log=v_dn_a_log, dn_dt_bias=v_dn_dt_bias, dn_norm_w=v_dn_norm_w, mix_w_out=v_mix_w_out, swa_wq=v_swa_wq, swa_wk=v_swa_wk, swa_wv=v_swa_wv, swa_q_norm_w=v_swa_q_norm_w, swa_k_norm_w=v_swa_k_norm_w, swa_sinks=v_swa_sinks, swa_wo=v_swa_wo, ffn_w_up=v_ffn_w_up, ffn_conv_w=v_ffn_conv_w, ffn_w_down=v_ffn_w_down)
    me = _my_index()

    small_shard_shapes = [w[k].shape for k in SMALL_SHARDED]
    sent = [w[k].astype(MXU_DTYPE) for k in BIG] + [_pack([w[k] for k in SMALL_SHARDED])]
    got = _all_gather(sent, name="gather_weights")
    whole = {k: _whole(a, SHARD_AXIS[k]) for k, a in zip(BIG, got[:-1])}
    for k, a in zip(SMALL_SHARDED, _unpack(got[-1], small_shard_shapes, lead=1)):
        whole[k] = _whole(a, SHARD_AXIS[k])
    for k in SMALL:
        whole.setdefault(k, w[k])

    loss, g = _local_step(x[0], loss_target[0], _prepare_weights(whole))
    grads = _reference_named(g)

    sent = [_by_owner(grads[k], SHARD_AXIS[k]) for k in BIG]
    got = _all_to_all(sent, name="scatter_grads")
    out_g, out_d, out_m, out_v = {}, {}, {}, {}
    for k, parts in zip(BIG, got):
        shp = w[k].shape
        r, c = shp[0] * shp[1], shp[2]
        res = _adamw(parts.reshape(N_DEV, r, c), w[k].reshape(r, c), mom[k].reshape(r, c), var[k].reshape(r, c), name=f"adamw_{k}")
        out_g[k], out_d[k], out_m[k], out_v[k] = [t.reshape(shp) for t in res]

    small_shapes = [grads[k].shape for k in SMALL]
    (all_small,) = _all_gather([_pack([loss] + [grads[k].astype(F32) for k in SMALL])], name="gather_small_grads")
    loss_parts, *small_parts = _unpack(all_small, [loss.shape] + small_shapes, lead=1)
    mine = []
    for k, p in zip(SMALL, small_parts):
        ax = SHARD_AXIS[k]
        mine.append(p if ax is None else lax.dynamic_slice_in_dim(p, me * w[k].shape[ax], w[k].shape[ax], 1 + ax))
    zero = jnp.zeros(loss.shape, F32)
    packed = [_pack([z] + [d[k] for k in SMALL]) for z, d in ((zero, w), (zero, mom), (zero, var))]
    res = _adamw(_pack([loss_parts] + mine, lead=1), *packed, name="adamw_small")
    shapes = [loss.shape] + [w[k].shape for k in SMALL]
    for t, dst in zip(res, (out_g, out_d, out_m, out_v)):
        parts = _unpack(t, shapes)
        if dst is out_g:
            total_loss = parts[0][0, 0]
        for k, a in zip(SMALL, parts[1:]):
            dst[k] = a

    return (total_loss, g["x"][None], *[out_g[k] for k in WEIGHTS], *[out_d[k] for k in WEIGHTS],
            *[out_m[k] for k in WEIGHTS], *[out_v[k] for k in WEIGHTS])
```

```python
import functools

import jax
import jax.numpy as jnp
from jax import lax
from jax.experimental import pallas as pl
from jax.experimental.pallas import tpu as pltpu

F32 = jnp.float32
BF16 = jnp.bfloat16
MXU_DTYPE = BF16
GRAD_WIRE_DTYPE = BF16

D_MODEL = 1024
N_META = 16
PAD_ROWS = 112
D_CONV = 512
DN_HEADS = 4
DN_HEAD_DIM = 128
DN_DIM = DN_HEADS * DN_HEAD_DIM
DN_CHUNK = 64
SEG = 512
N_SEG = 7
SWA_HEADS = 16
SWA_KV_HEADS = 4
SWA_GROUP = SWA_HEADS // SWA_KV_HEADS
SWA_HEAD_DIM = 64
SWA_BLOCK = 128
D_FF = 2816
EPS = 1e-6
NEG = -1e30
N_DEV = 8

ADAM_LR = 0.001
ADAM_B1 = 0.9
ADAM_B2 = 0.999
ADAM_EPS = 1e-08
ADAM_WD = 0.01
ADAM_STEP = 10

VMEM_LIMIT_BYTES = 52 * 1024 * 1024
SUBLANES = 8
LANES = 128


def _pick(n, prefs):
    for p in prefs:
        if n % p == 0:
            return p
    return n


def _params(sem, vmem=VMEM_LIMIT_BYTES):
    return pltpu.CompilerParams(dimension_semantics=sem, vmem_limit_bytes=vmem)


def _rms(x, w):
    return x * lax.rsqrt(jnp.mean(x * x, -1, keepdims=True) + EPS) * w


def _norm_matmul(h, nw, w, *, o_seg=None, name):
    m, k = h.shape
    n = w.shape[1]
    tm = _pick(m, (1408, 384, 128))
    tn = _pick(o_seg or n, (1408, 1024, 512, 256, 128))

    def body(h_ref, nw_ref, w_ref, o_ref, hn_ref, hn_s):
        @pl.when(pl.program_id(1) == 0)
        def _():
            hn = _rms(h_ref[...], nw_ref[...]).astype(MXU_DTYPE)
            hn_s[...] = hn
            hn_ref[...] = hn

        o_ref[...] = jnp.dot(hn_s[...], w_ref[...], preferred_element_type=F32)

    if o_seg:
        per = o_seg // tn
        o_shape = jax.ShapeDtypeStruct((n // o_seg, m, o_seg), F32)
        o_spec = pl.BlockSpec((None, tm, tn), lambda i, j: (j // per, i, j % per))
    else:
        o_shape = jax.ShapeDtypeStruct((m, n), F32)
        o_spec = pl.BlockSpec((tm, tn), lambda i, j: (i, j))
    return pl.pallas_call(
        body, name=name, grid=(m // tm, n // tn),
        in_specs=[pl.BlockSpec((tm, k), lambda i, j: (i, 0)), pl.BlockSpec((1, k), lambda i, j: (0, 0)),
                  pl.BlockSpec((k, tn), lambda i, j: (0, j))],
        out_specs=[o_spec, pl.BlockSpec((tm, k), lambda i, j: (i, 0))],
        out_shape=[o_shape, jax.ShapeDtypeStruct((m, k), MXU_DTYPE)],
        scratch_shapes=[pltpu.VMEM((tm, k), MXU_DTYPE)],
        compiler_params=_params(("parallel", "arbitrary")),
    )(h, nw, w)


TILE_BUDGET_BYTES = 38 * 1024 * 1024
TILE_SIZES = (2816, 1792, 1536, 1408, 1024, 512, 256, 128)


def _divisor_tiles(n):
    return [t for t in TILE_SIZES if n % t == 0] or [n]


def _mm_nn(a, w, *, res=None, a_seg=False, trans_w=False, name):
    if a_seg:
        s, m, seg = a.shape
    else:
        m, seg = a.shape
        s = 1
    k = s * seg
    n = w.shape[0] if trans_w else w.shape[1]
    tm = _pick(m, (1408, 384, 128))
    ab = a.dtype.itemsize
    k_steps = [(sb, seg) for sb in range(s, 0, -1) if s % sb == 0] if a_seg else [(1, t) for t in _divisor_tiles(seg)]
    best = None
    for tn in _divisor_tiles(n):
        for sb, tk1 in k_steps:
            tk = sb * tk1
            nk = k // tk
            need = 2 * tm * tk * ab + 2 * tk * tn * 2 + 2 * tm * tn * 4 + (tm * tn * 4 if nk > 1 else 0) + (2 * tm * tn * 4 if res is not None else 0)
            if need <= TILE_BUDGET_BYTES and (best is None or tk * tn > best[0]):
                best = (tk * tn, tn, sb, tk1)
    _, tn, sb, tk1 = best
    tk = sb * tk1
    nk = k // tk
    w_dims = _DOT_DIMS["nt" if trans_w else "nn"]

    def body(*refs):
        a_ref, w_ref = refs[:2]
        r_ref = refs[2] if res is not None else None
        o_ref = refs[3 if res is not None else 2]

        def partial_product():
            if not a_seg:
                return lax.dot_general(a_ref[...].astype(MXU_DTYPE), w_ref[...], w_dims, preferred_element_type=F32)
            out = None
            for t in range(sb):
                wt = w_ref[:, t * seg:(t + 1) * seg] if trans_w else w_ref[t * seg:(t + 1) * seg, :]
                d = lax.dot_general(a_ref[t].astype(MXU_DTYPE), wt, w_dims, preferred_element_type=F32)
                out = d if out is None else out + d
            return out

        if nk == 1:
            o_ref[...] = partial_product() if res is None else partial_product() + r_ref[...]
            return
        acc = refs[-1]
        kk = pl.program_id(2)

        @pl.when(kk == 0)
        def _():
            acc[...] = jnp.zeros_like(acc)

        acc[...] += partial_product()

        @pl.when(kk == nk - 1)
        def _():
            o_ref[...] = acc[...] if res is None else acc[...] + r_ref[...]

    a_spec = pl.BlockSpec((sb, tm, seg), lambda i, j, kk: (kk, i, 0)) if a_seg else pl.BlockSpec((tm, tk), lambda i, j, kk: (i, kk))
    w_spec = pl.BlockSpec((tn, tk), lambda i, j, kk: (j, kk)) if trans_w else pl.BlockSpec((tk, tn), lambda i, j, kk: (kk, j))
    in_specs = [a_spec, w_spec]
    args = [a, w]
    if res is not None:
        in_specs.append(pl.BlockSpec((tm, tn), lambda i, j, kk: (i, j)))
        args.append(res)
    return pl.pallas_call(
        body, name=name, grid=(m // tm, n // tn, nk), in_specs=in_specs,
        out_specs=pl.BlockSpec((tm, tn), lambda i, j, kk: (i, j)),
        out_shape=jax.ShapeDtypeStruct((m, n), F32),
        scratch_shapes=[pltpu.VMEM((tm, tn), F32)] if nk > 1 else [],
        compiler_params=_params(("parallel", "parallel", "arbitrary")),
    )(*args)


def _mm_tn(a, b, *, b_seg=False, out_dtype=None, name):
    out_dtype = out_dtype or GRAD_WIRE_DTYPE
    m, ka = a.shape
    if b_seg:
        s, _, seg = b.shape
        n = s * seg
    else:
        n = b.shape[1]
        seg = n
    tmc = _pick(m, (1408, 384, 128))
    best = None
    for tka in _divisor_tiles(ka):
        for tn in _divisor_tiles(seg):
            need = 2 * tmc * tka * a.dtype.itemsize + 2 * tmc * tn * b.dtype.itemsize + tka * tn * 4 + 2 * tka * tn * 4
            if need <= TILE_BUDGET_BYTES and (best is None or (tka * tn, tn) > best[:2]):
                best = (tka * tn, tn, tka)
    _, tn, tka = best
    nm = m // tmc

    def body(a_ref, b_ref, o_ref, acc):
        mm = pl.program_id(2)

        @pl.when(mm == 0)
        def _():
            acc[...] = jnp.zeros_like(acc)

        acc[...] += lax.dot_general(a_ref[...].astype(MXU_DTYPE), b_ref[...].astype(MXU_DTYPE),
                                    (((0,), (0,)), ((), ())), preferred_element_type=F32)

        @pl.when(mm == nm - 1)
        def _():
            o_ref[...] = acc[...].astype(o_ref.dtype)

    if b_seg:
        per = seg // tn
        b_spec = pl.BlockSpec((None, tmc, tn), lambda i, j, mm: (j // per, mm, j % per))
    else:
        b_spec = pl.BlockSpec((tmc, tn), lambda i, j, mm: (mm, j))
    return pl.pallas_call(
        body, name=name, grid=(ka // tka, n // tn, nm),
        in_specs=[pl.BlockSpec((tmc, tka), lambda i, j, mm: (mm, i)), b_spec],
        out_specs=pl.BlockSpec((tka, tn), lambda i, j, mm: (i, j)),
        out_shape=jax.ShapeDtypeStruct((ka, n), out_dtype),
        scratch_shapes=[pltpu.VMEM((tka, tn), F32)],
        compiler_params=_params(("parallel", "parallel", "arbitrary")),
    )(a, b)


def _rmsnorm_bwd(dhn, h, nw, dres, *, name):
    m, d = h.shape
    tm = _pick(m, (384, 128))

    def body(dhn_ref, h_ref, nw_ref, dres_ref, dh_ref, dnw_ref):
        i = pl.program_id(0)
        x = h_ref[...]
        r = lax.rsqrt(jnp.mean(x * x, -1, keepdims=True) + EPS)
        xh = x * r
        dy = dhn_ref[...]
        dxh = dy * nw_ref[...]
        dx = r * (dxh - xh * jnp.mean(dxh * xh, -1, keepdims=True))
        row = i * tm + lax.broadcasted_iota(jnp.int32, (tm, 1), 0)
        dh_ref[...] = jnp.where(row >= PAD_ROWS, dres_ref[...] + dx, 0.0)

        @pl.when(i == 0)
        def _():
            dnw_ref[...] = jnp.zeros_like(dnw_ref)

        dnw_ref[...] += jnp.sum(dy * xh, 0, keepdims=True)

    return pl.pallas_call(
        body, name=name, grid=(m // tm,),
        in_specs=[pl.BlockSpec((tm, d), lambda i: (i, 0)), pl.BlockSpec((tm, d), lambda i: (i, 0)),
                  pl.BlockSpec((1, d), lambda i: (0, 0)), pl.BlockSpec((tm, d), lambda i: (i, 0))],
        out_specs=[pl.BlockSpec((tm, d), lambda i: (i, 0)), pl.BlockSpec((1, d), lambda i: (0, 0))],
        out_shape=[jax.ShapeDtypeStruct((m, d), F32), jax.ShapeDtypeStruct((1, d), F32)],
        compiler_params=_params(("arbitrary",)),
    )(dhn, h, nw, dres)


ROW_CHUNK = 248


def _row_chunks(m):
    out, s = [], SUBLANES
    while s < m:
        n = min(ROW_CHUNK, m - s)
        out.append((s, n))
        s += n
    return out


def _conv_at(load, w, width, s, n):
    acc = w[width - 1:width, :] * load(s, n)
    for j in range(width - 1):
        acc = acc + w[j:j + 1, :] * load(s - (width - 1 - j), n)
    return acc


def _conv_t_at(load, w, width, s, n):
    acc = w[width - 1:width, :] * load(s, n)
    for j in range(width - 1):
        acc = acc + w[j:j + 1, :] * load(s + (width - 1 - j), n)
    return acc


def _dconv_w(load_x, d, width, s, n):
    rows = [jnp.sum(d * load_x(s - (width - 1 - j), n), 0, keepdims=True) for j in range(width)]
    rows.append(jnp.zeros((SUBLANES - width, d.shape[1]), F32))
    return jnp.concatenate(rows, 0)


def _pad_w(w):
    return jnp.concatenate([w, jnp.zeros((SUBLANES - w.shape[0], w.shape[1]), w.dtype)], 0)


def _sigmoid(x):
    return 1.0 / (1.0 + jnp.exp(-x))


def _ffn_act_fwd(u, cw, *, name):
    _, m, f = u.shape
    cb = _pick(f, (256, 128))
    chunks = _row_chunks(m)

    def body(g_ref, v_ref, w_ref, o_ref):
        w = w_ref[...]
        o_ref[pl.ds(0, SUBLANES), :] = jnp.zeros((SUBLANES, cb), o_ref.dtype)
        for s, n in chunks:
            c = _conv_at(lambda a, b: g_ref[pl.ds(a, b), :], w, 3, s, n)
            o_ref[pl.ds(s, n), :] = (c * _sigmoid(c) * v_ref[pl.ds(s, n), :]).astype(o_ref.dtype)

    return pl.pallas_call(
        body, name=name, grid=(f // cb,),
        in_specs=[pl.BlockSpec((None, m, cb), lambda j: (0, 0, j)), pl.BlockSpec((None, m, cb), lambda j: (1, 0, j)),
                  pl.BlockSpec((SUBLANES, cb), lambda j: (0, j))],
        out_specs=pl.BlockSpec((m, cb), lambda j: (0, j)),
        out_shape=jax.ShapeDtypeStruct((m, f), MXU_DTYPE),
        compiler_params=_params(("parallel",)),
    )(u, u, cw)


def _ffn_act_bwd(da, u, cw, *, name):
    _, m, f = u.shape
    cb = LANES
    chunks = _row_chunks(m)

    def body(da_ref, g_ref, v_ref, w_ref, du_ref, dw_ref, dg_s):
        w = w_ref[...]
        zeros8 = jnp.zeros((SUBLANES, cb), F32)
        dg_s[pl.ds(0, SUBLANES), :] = zeros8
        dg_s[pl.ds(m, SUBLANES), :] = zeros8
        du_ref[0, pl.ds(0, SUBLANES), :] = zeros8.astype(du_ref.dtype)
        du_ref[1, pl.ds(0, SUBLANES), :] = zeros8.astype(du_ref.dtype)
        load_g = lambda a, b: g_ref[pl.ds(a, b), :]
        dw = jnp.zeros((SUBLANES, cb), F32)
        for s, n in chunks:
            c = _conv_at(load_g, w, 3, s, n)
            sg = _sigmoid(c)
            d = da_ref[pl.ds(s, n), :]
            du_ref[1, pl.ds(s, n), :] = (d * (c * sg)).astype(du_ref.dtype)
            dc = d * v_ref[pl.ds(s, n), :] * (sg * (1.0 + c * (1.0 - sg)))
            dg_s[pl.ds(s, n), :] = dc
            dw = dw + _dconv_w(load_g, dc, 3, s, n)
        dw_ref[...] = dw
        for s, n in chunks:
            du_ref[0, pl.ds(s, n), :] = _conv_t_at(lambda a, b: dg_s[pl.ds(a, b), :], w, 3, s, n).astype(du_ref.dtype)

    return pl.pallas_call(
        body, name=name, grid=(f // cb,),
        in_specs=[pl.BlockSpec((m, cb), lambda j: (0, j)), pl.BlockSpec((None, m, cb), lambda j: (0, 0, j)),
                  pl.BlockSpec((None, m, cb), lambda j: (1, 0, j)), pl.BlockSpec((SUBLANES, cb), lambda j: (0, j))],
        out_specs=[pl.BlockSpec((2, m, cb), lambda j: (0, 0, j)), pl.BlockSpec((SUBLANES, cb), lambda j: (0, j))],
        out_shape=[jax.ShapeDtypeStruct((2, m, f), MXU_DTYPE), jax.ShapeDtypeStruct((SUBLANES, f), F32)],
        scratch_shapes=[pltpu.VMEM((m + SUBLANES, cb), F32)],
        compiler_params=_params(("parallel",)),
    )(da, u, u, cw)


def _shortconv_fwd(pm, cw, y, *, name):
    _, m, seg = pm.shape
    cb = _pick(seg, (256, 128))
    chunks = _row_chunks(m)

    def body(gi_ref, go_ref, ah_ref, w_ref, y_in, o_ref):
        del y_in
        w = w_ref[...]
        o_ref[pl.ds(0, SUBLANES), :] = jnp.zeros((SUBLANES, cb), o_ref.dtype)
        load_m = lambda a, b: gi_ref[pl.ds(a, b), :] * ah_ref[pl.ds(a, b), :]
        for s, n in chunks:
            o_ref[pl.ds(s, n), :] = (go_ref[pl.ds(s, n), :] * _conv_at(load_m, w, 3, s, n)).astype(o_ref.dtype)

    return pl.pallas_call(
        body, name=name, grid=(seg // cb,),
        in_specs=[pl.BlockSpec((None, m, cb), lambda j: (0, 0, j)), pl.BlockSpec((None, m, cb), lambda j: (1, 0, j)),
                  pl.BlockSpec((None, m, cb), lambda j: (2, 0, j)), pl.BlockSpec((SUBLANES, cb), lambda j: (0, j)),
                  pl.BlockSpec(memory_space=pl.ANY)],
        out_specs=pl.BlockSpec((m, cb), lambda j: (0, j)),
        out_shape=jax.ShapeDtypeStruct(y.shape, y.dtype),
        input_output_aliases={4: 0},
        compiler_params=_params(("parallel",)),
    )(pm, pm, pm, cw, y)


def _shortconv_bwd(dy, pm, cw, dpm, *, name):
    _, m, seg = pm.shape
    cb = LANES
    chunks = _row_chunks(m)

    def body(dy_ref, gi_ref, go_ref, ah_ref, w_ref, dpm_in, dp_ref, dw_ref, dc_s):
        del dpm_in
        w = w_ref[...]
        zeros8 = jnp.zeros((SUBLANES, cb), F32)
        dc_s[pl.ds(0, SUBLANES), :] = zeros8
        dc_s[pl.ds(m, SUBLANES), :] = zeros8
        for t in range(3):
            dp_ref[t, pl.ds(0, SUBLANES), :] = zeros8.astype(dp_ref.dtype)
        load_m = lambda a, b: gi_ref[pl.ds(a, b), :] * ah_ref[pl.ds(a, b), :]
        dw = jnp.zeros((SUBLANES, cb), F32)
        for s, n in chunks:
            d = dy_ref[pl.ds(s, n), :]
            dp_ref[1, pl.ds(s, n), :] = (d * _conv_at(load_m, w, 3, s, n)).astype(dp_ref.dtype)
            dc = d * go_ref[pl.ds(s, n), :]
            dc_s[pl.ds(s, n), :] = dc
            dw = dw + _dconv_w(load_m, dc, 3, s, n)
        dw_ref[...] = dw
        for s, n in chunks:
            dm = _conv_t_at(lambda a, b: dc_s[pl.ds(a, b), :], w, 3, s, n)
            dp_ref[0, pl.ds(s, n), :] = (dm * ah_ref[pl.ds(s, n), :]).astype(dp_ref.dtype)
            dp_ref[2, pl.ds(s, n), :] = (dm * gi_ref[pl.ds(s, n), :]).astype(dp_ref.dtype)

    return pl.pallas_call(
        body, name=name, grid=(seg // cb,),
        in_specs=[pl.BlockSpec((m, cb), lambda j: (0, j)), pl.BlockSpec((None, m, cb), lambda j: (0, 0, j)),
                  pl.BlockSpec((None, m, cb), lambda j: (1, 0, j)), pl.BlockSpec((None, m, cb), lambda j: (2, 0, j)),
                  pl.BlockSpec((SUBLANES, cb), lambda j: (0, j)), pl.BlockSpec(memory_space=pl.ANY)],
        out_specs=[pl.BlockSpec((3, m, cb), lambda j: (0, 0, j)), pl.BlockSpec((SUBLANES, cb), lambda j: (0, j))],
        out_shape=[jax.ShapeDtypeStruct(dpm.shape, dpm.dtype), jax.ShapeDtypeStruct((SUBLANES, seg), F32)],
        scratch_shapes=[pltpu.VMEM((m + SUBLANES, cb), F32)],
        input_output_aliases={5: 0},
        compiler_params=_params(("parallel",)),
    )(dy, pm, pm, pm, cw, dpm)


def _dnpre_fwd(pm, cw, *, name):
    _, m, seg = pm.shape
    cb = _pick(seg, (256, 128))
    per = seg // cb
    chunks = _row_chunks(m)

    def body(x_ref, w_ref, o_ref):
        w = w_ref[...]
        o_ref[pl.ds(0, SUBLANES), :] = jnp.zeros((SUBLANES, cb), F32)
        for s, n in chunks:
            c = _conv_at(lambda a, b: x_ref[pl.ds(a, b), :], w, 4, s, n)
            o_ref[pl.ds(s, n), :] = c * _sigmoid(c)

    return pl.pallas_call(
        body, name=name, grid=(3 * per,),
        in_specs=[pl.BlockSpec((None, m, cb), lambda j: (3 + j // per, 0, j % per)), pl.BlockSpec((SUBLANES, cb), lambda j: (0, j))],
        out_specs=pl.BlockSpec((None, m, cb), lambda j: (j // per, 0, j % per)),
        out_shape=jax.ShapeDtypeStruct((3, m, seg), F32),
        compiler_params=_params(("parallel",)),
    )(pm, cw)


def _dnpre_bwd(dqkv, pm, cw, dpm, *, name):
    _, m, seg = pm.shape
    cb = _pick(seg, (256, 128))
    per = seg // cb
    chunks = _row_chunks(m)

    def body(d_ref, x_ref, w_ref, dpm_in, dp_ref, dw_ref, dc_s):
        del dpm_in
        w = w_ref[...]
        zeros8 = jnp.zeros((SUBLANES, cb), F32)
        dc_s[pl.ds(0, SUBLANES), :] = zeros8
        dc_s[pl.ds(m, SUBLANES), :] = zeros8
        dp_ref[pl.ds(0, SUBLANES), :] = zeros8.astype(dp_ref.dtype)
        load_x = lambda a, b: x_ref[pl.ds(a, b), :]
        dw = jnp.zeros((SUBLANES, cb), F32)
        for s, n in chunks:
            c = _conv_at(load_x, w, 4, s, n)
            sg = _sigmoid(c)
            dc = d_ref[pl.ds(s, n), :] * (sg * (1.0 + c * (1.0 - sg)))
            dc_s[pl.ds(s, n), :] = dc
            dw = dw + _dconv_w(load_x, dc, 4, s, n)
        dw_ref[...] = dw
        for s, n in chunks:
            dp_ref[pl.ds(s, n), :] = _conv_t_at(lambda a, b: dc_s[pl.ds(a, b), :], w, 4, s, n).astype(dp_ref.dtype)

    return pl.pallas_call(
        body, name=name, grid=(3 * per,),
        in_specs=[pl.BlockSpec((None, m, cb), lambda j: (j // per, 0, j % per)),
                  pl.BlockSpec((None, m, cb), lambda j: (3 + j // per, 0, j % per)),
                  pl.BlockSpec((SUBLANES, cb), lambda j: (0, j)), pl.BlockSpec(memory_space=pl.ANY)],
        out_specs=[pl.BlockSpec((None, m, cb), lambda j: (3 + j // per, 0, j % per)), pl.BlockSpec((SUBLANES, cb), lambda j: (0, j))],
        out_shape=[jax.ShapeDtypeStruct(dpm.shape, dpm.dtype), jax.ShapeDtypeStruct((SUBLANES, 3 * seg), F32)],
        scratch_shapes=[pltpu.VMEM((m + SUBLANES, cb), F32)],
        input_output_aliases={3: 0},
        compiler_params=_params(("parallel",)),
    )(dqkv, pm, cw, dpm)


def _mxu_dot_impl(a, b, form):
    a = a.astype(MXU_DTYPE)
    b = b.astype(MXU_DTYPE)
    dims = {"nn": (((1,), (0,)), ((), ())), "nt": (((1,), (1,)), ((), ())), "tn": (((0,), (0,)), ((), ()))}[form]
    return lax.dot_general(a, b, dims, preferred_element_type=F32)


@functools.partial(jax.custom_vjp, nondiff_argnums=(2,))
def _mxu_dot(a, b, form):
    return _mxu_dot_impl(a, b, form)


def _mxu_dot_fwd(a, b, form):
    return _mxu_dot_impl(a, b, form), (a, b)


def _mxu_dot_bwd(form, saved, g):
    a, b = saved
    if form == "nn":
        return _mxu_dot_impl(g, b, "nt"), _mxu_dot_impl(a, g, "tn")
    if form == "nt":
        return _mxu_dot_impl(g, b, "nn"), _mxu_dot_impl(g, a, "tn")
    return _mxu_dot_impl(b, g, "nt"), _mxu_dot_impl(a, g, "nn")


_mxu_dot.defvjp(_mxu_dot_fwd, _mxu_dot_bwd)


_DOT_DIMS = {"nn": (((1,), (0,)), ((), ())), "nt": (((1,), (1,)), ((), ())), "tn": (((0,), (0,)), ((), ()))}


def _split(x):
    hi = x.astype(BF16)
    return hi, (x - hi.astype(F32)).astype(BF16)


def _dot3_impl(a, b, form):
    dg = lambda p, q: lax.dot_general(p, q, _DOT_DIMS[form], preferred_element_type=F32)
    ah, al = _split(a)
    bh, bl = _split(b)
    return dg(ah, bh) + (dg(ah, bl) + dg(al, bh))


@functools.partial(jax.custom_vjp, nondiff_argnums=(2,))
def _dot3(a, b, form):
    return _dot3_impl(a, b, form)


def _dot3_fwd(a, b, form):
    return _dot3_impl(a, b, form), (a, b)


def _dot3_bwd(form, saved, g):
    a, b = saved
    if form == "nn":
        return _dot3_impl(g, b, "nt"), _dot3_impl(a, g, "tn")
    if form == "nt":
        return _dot3_impl(g, b, "nn"), _dot3_impl(g, a, "tn")
    return _dot3_impl(b, g, "nt"), _dot3_impl(a, g, "nn")


_dot3.defvjp(_dot3_fwd, _dot3_bwd)


def _hdot(a, b):
    return _dot3(a, b, "nn")


def _mask_dot(mask, x, form):
    dg = lambda q: lax.dot_general(mask.astype(BF16), q, _DOT_DIMS[form], preferred_element_type=F32)
    x1 = x.astype(BF16)
    r1 = x - x1.astype(F32)
    x2 = r1.astype(BF16)
    x3 = (r1 - x2.astype(F32)).astype(BF16)
    return dg(x1) + (dg(x2) + dg(x3))


def _decay_masks(c):
    row = lax.broadcasted_iota(jnp.int32, (c, c), 0)
    col = lax.broadcasted_iota(jnp.int32, (c, c), 1)
    return (row >= col).astype(F32), row <= col


def _decay_impl(gb):
    lower, upper = _decay_masks(gb.shape[0])
    return _mask_dot(lower, gb, "nn"), _mask_dot(jnp.ones_like(gb), jnp.where(upper, gb, 0.0), "nn")


@jax.custom_vjp
def _decay_matrices(gb):
    return _decay_impl(gb)


def _decay_fwd(gb):
    return _decay_impl(gb), None


def _decay_bwd(_, cts):
    gc, gr = cts
    lower, upper = _decay_masks(gc.shape[0])
    return (_mask_dot(lower, gc, "tn") + jnp.where(upper, _mask_dot(jnp.ones_like(gr), gr, "tn"), 0.0),)


_decay_matrices.defvjp(_decay_fwd, _decay_bwd)


def _softplus(x):
    return jnp.maximum(x, 0.0) + jnp.log(1.0 + jnp.exp(-jnp.abs(x)))


def _heads(f, *lists):
    return [f(*t) for t in zip(*lists)]


def _dn_chunk(qr, kr, v, z, braw, araw, alog, dtb, nw, state, valid):
    c = DN_CHUNK
    row = lax.broadcasted_iota(jnp.int32, (c, c), 0)
    col = lax.broadcasted_iota(jnp.int32, (c, c), 1)
    incl = row >= col
    strict = row > col
    eye = jnp.where(row == col, 1.0, 0.0)
    q = _heads(lambda t: t * lax.rsqrt(jnp.sum(t * t, -1, keepdims=True) + EPS) * (DN_HEAD_DIM ** -0.5), qr)
    k = _heads(lambda t: t * lax.rsqrt(jnp.sum(t * t, -1, keepdims=True) + EPS), kr)
    beta = _heads(lambda t: _sigmoid(t) * valid, braw)
    g = _heads(lambda al, ar, dt: -jnp.exp(al) * _softplus(ar + dt) * valid, alog, araw, dtb)
    decay = _heads(lambda t: _decay_matrices(jnp.broadcast_to(t, (c, c))), g)
    dmask = _heads(lambda d: jnp.where(incl, jnp.exp(jnp.where(incl, d[0] - d[1], 0.0)), 0.0), decay)
    dec = _heads(lambda d: d[0][:, :1], decay)
    dlast = _heads(lambda d: d[0][c - 1:c, :1], decay)
    kk = _heads(lambda t: _mxu_dot(t, t, "nt"), k)
    a = _heads(lambda b, t, d: jnp.where(strict, b * t * d, 0.0), beta, kk, dmask)
    x = _heads(lambda t: eye - t, a)
    p = _heads(_hdot, a, a)
    for it in range(5):
        x = _heads(lambda s, t: s + _hdot(s, t), x, p)
        if it < 4:
            p = _heads(_hdot, p, p)
    u = _heads(lambda s, t, b: _hdot(s, t * b), x, v, beta)
    w = _heads(lambda s, t, b, d: _hdot(s, t * (b * jnp.exp(d))), x, k, beta, dec)
    qk = _heads(lambda s, t, d: _mxu_dot(s, t, "nt") * d, q, k, dmask)
    q_dec = _heads(lambda t, d: t * jnp.exp(d), q, dec)
    k_dec = _heads(lambda t, dl, d: t * jnp.exp(dl - d), k, dlast, dec)
    v_new = _heads(lambda s, t, st: s - _mxu_dot(t, st, "nn"), u, w, state)
    o = _heads(lambda qd, st, s, vn: _mxu_dot(qd, st, "nn") + _mxu_dot(s, vn, "nn"), q_dec, state, qk, v_new)
    new_state = _heads(lambda st, dl, kd, vn: st * jnp.exp(dl) + _mxu_dot(kd, vn, "tn"), state, dlast, k_dec, v_new)
    y = _heads(lambda t, zz: _rms(t, nw) * (zz * _sigmoid(zz)), o, z)
    return y, new_state


def _dn_valid(n):
    row = n * DN_CHUNK + lax.broadcasted_iota(jnp.int32, (DN_CHUNK, 1), 0)
    return (row >= PAD_ROWS).astype(F32)


def _dn_in_specs(rev, nc):
    cn = (lambda n: nc - 1 - n) if rev else (lambda n: n)
    c, hd = DN_CHUNK, DN_HEAD_DIM
    return [
        pl.BlockSpec((None, c, DN_DIM), lambda n: (0, cn(n), 0)),
        pl.BlockSpec((None, c, DN_DIM), lambda n: (1, cn(n), 0)),
        pl.BlockSpec((None, c, DN_DIM), lambda n: (2, cn(n), 0)),
        pl.BlockSpec((None, c, DN_DIM), lambda n: (6, cn(n), 0)),
        pl.BlockSpec((DN_HEADS, 2, c, 1), lambda n: (0, 0, cn(n), 0)),
        pl.BlockSpec((DN_HEADS, SUBLANES, LANES), lambda n: (0, 0, 0)),
        pl.BlockSpec((1, hd), lambda n: (0, 0)),
    ]


def _head(ref_or_val, h):
    return ref_or_val[:, h * DN_HEAD_DIM:(h + 1) * DN_HEAD_DIM]


def _dn_load(q_ref, k_ref, v_ref, z_ref, ba_ref, hp_ref):
    heads = range(DN_HEADS)
    return ([_head(q_ref, h) for h in heads], [_head(k_ref, h) for h in heads], [_head(v_ref, h) for h in heads],
            [_head(z_ref, h) for h in heads], [ba_ref[h, 0] for h in heads], [ba_ref[h, 1] for h in heads],
            [hp_ref[h, 0:1, 0:1] for h in heads], [hp_ref[h, 1:2, 0:1] for h in heads])


def _delta_fwd(qkvc, pm, ba, hp, nw, *, name):
    _, m, _ = qkvc.shape
    nc = m // DN_CHUNK
    hd = DN_HEAD_DIM

    def body(q_ref, k_ref, v_ref, z_ref, ba_ref, hp_ref, nw_ref, y_ref, s_ref, state):
        n = pl.program_id(0)

        @pl.when(n == 0)
        def _():
            state[...] = jnp.zeros_like(state)

        heads = range(DN_HEADS)
        old = [state[h] for h in heads]
        y, new = _dn_chunk(*_dn_load(q_ref, k_ref, v_ref, z_ref, ba_ref, hp_ref), nw_ref[...], old, _dn_valid(n))
        for h in heads:
            s_ref[h] = old[h]
            y_ref[:, h * hd:(h + 1) * hd] = y[h].astype(y_ref.dtype)
            state[h] = new[h]

    return pl.pallas_call(
        body, name=name, grid=(nc,), in_specs=_dn_in_specs(False, nc),
        out_specs=[pl.BlockSpec((DN_CHUNK, DN_DIM), lambda n: (n, 1)), pl.BlockSpec((DN_HEADS, None, hd, hd), lambda n: (0, n, 0, 0))],
        out_shape=[jax.ShapeDtypeStruct((m, D_CONV + DN_DIM), MXU_DTYPE), jax.ShapeDtypeStruct((DN_HEADS, nc, hd, hd), F32)],
        scratch_shapes=[pltpu.VMEM((DN_HEADS, hd, hd), F32)],
        compiler_params=_params(("arbitrary",)),
    )(qkvc, qkvc, qkvc, pm, ba, hp, nw)


def _delta_bwd(dy, qkvc, pm, ba, hp, nw, states, *, name):
    _, m, _ = qkvc.shape
    nc = m // DN_CHUNK
    hd, c = DN_HEAD_DIM, DN_CHUNK

    def body(q_ref, k_ref, v_ref, z_ref, ba_ref, hp_ref, nw_ref, s_ref, dy_ref,
             dz_ref, dqkv_ref, dba_ref, dhp_ref, dnw_ref, dstate):
        step = pl.program_id(0)
        n = nc - 1 - step

        @pl.when(step == 0)
        def _():
            dstate[...] = jnp.zeros_like(dstate)
            dhp_ref[...] = jnp.zeros_like(dhp_ref)
            dnw_ref[...] = jnp.zeros_like(dnw_ref)

        valid = _dn_valid(n)
        heads = range(DN_HEADS)
        fn = lambda *a: _dn_chunk(*a, valid)
        _, vjp = jax.vjp(fn, *_dn_load(q_ref, k_ref, v_ref, z_ref, ba_ref, hp_ref), nw_ref[...], [s_ref[h] for h in heads])
        dq, dk, dv, dz, dbr, dar, dalog, ddtb, dnw, dst = vjp(([_head(dy_ref, h) for h in heads], [dstate[h] for h in heads]))
        for h in heads:
            cols = slice(h * hd, (h + 1) * hd)
            dqkv_ref[0, :, cols] = dq[h]
            dqkv_ref[1, :, cols] = dk[h]
            dqkv_ref[2, :, cols] = dv[h]
            dz_ref[:, cols] = dz[h].astype(dz_ref.dtype)
            dba_ref[h, 0] = dbr[h]
            dba_ref[h, 1] = dar[h]
            dstate[h] = dst[h]
            dhp_ref[h] += jnp.concatenate([jnp.broadcast_to(dalog[h], (1, LANES)), jnp.broadcast_to(ddtb[h], (1, LANES)),
                                           jnp.zeros((SUBLANES - 2, LANES), F32)], 0)
        dnw_ref[...] += dnw

    rn = lambda n: nc - 1 - n
    in_specs = _dn_in_specs(True, nc) + [
        pl.BlockSpec((DN_HEADS, None, hd, hd), lambda n: (0, rn(n), 0, 0)),
        pl.BlockSpec((c, DN_DIM), lambda n: (rn(n), 1)),
    ]
    out_specs = [
        pl.BlockSpec((None, c, DN_DIM), lambda n: (6, rn(n), 0)),
        pl.BlockSpec((3, c, DN_DIM), lambda n: (0, rn(n), 0)),
        pl.BlockSpec((DN_HEADS, 2, c, 1), lambda n: (0, 0, rn(n), 0)),
        pl.BlockSpec((DN_HEADS, SUBLANES, LANES), lambda n: (0, 0, 0)),
        pl.BlockSpec((1, hd), lambda n: (0, 0)),
    ]
    return pl.pallas_call(
        body, name=name, grid=(nc,), in_specs=in_specs, out_specs=out_specs,
        out_shape=[jax.ShapeDtypeStruct(pm.shape, MXU_DTYPE), jax.ShapeDtypeStruct(qkvc.shape, F32),
                   jax.ShapeDtypeStruct(ba.shape, F32), jax.ShapeDtypeStruct(hp.shape, F32),
                   jax.ShapeDtypeStruct((1, hd), F32)],
        scratch_shapes=[pltpu.VMEM((DN_HEADS, hd, hd), F32)],
        compiler_params=_params(("arbitrary",)),
    )(qkvc, qkvc, qkvc, pm, ba, hp, nw, states, dy)


def _attn_block(q4, k0, kp, kc, v0, vp, vc, qw, kw, sink, n):
    g, b, hd = SWA_GROUP, SWA_BLOCK, SWA_HEAD_DIM
    qn = _rms(q4, qw) * (hd ** -0.5)
    kn = _rms(jnp.concatenate([k0, kp, kc], 0), kw)
    vcat = jnp.concatenate([v0, vp, vc], 0)
    s = _mxu_dot(qn.reshape(g * b, hd), kn, "nt").reshape(g, b, 3 * b)
    i = lax.broadcasted_iota(jnp.int32, (b, 3 * b), 0)
    c = lax.broadcasted_iota(jnp.int32, (b, 3 * b), 1)
    in_meta, in_prev, in_cur = c < b, (c >= b) & (c < 2 * b), c >= 2 * b
    j = c - jnp.where(in_meta, 0, jnp.where(in_prev, b, 2 * b))
    meta_lo = jnp.where(n == 0, b, PAD_ROWS)
    cur_lo = jnp.where(n == 0, PAD_ROWS, 0)
    prev_off = jnp.where(n >= 2, 0, 2 * b)
    valid = (in_meta & (j >= meta_lo)) | (in_prev & (j > i + prev_off)) | (in_cur & (j <= i) & (j >= cur_lo))
    s = jnp.where(valid[None], s, NEG)
    m = lax.stop_gradient(jnp.maximum(jnp.max(s, -1, keepdims=True), sink))
    e = jnp.exp(s - m)
    p = e / (jnp.sum(e, -1, keepdims=True) + jnp.exp(sink - m))
    return _mxu_dot(p.reshape(g * b, 3 * b), vcat, "nn").reshape(g, b, hd)


def _attn_in_specs():
    g, b, hd = SWA_GROUP, SWA_BLOCK, SWA_HEAD_DIM
    kv = lambda f: pl.BlockSpec((None, b, hd), lambda kh, n: (kh, f(n), 0))
    blocks = [lambda n: 0, lambda n: jnp.maximum(n - 1, 0), lambda n: n]
    return ([pl.BlockSpec((g, b, hd), lambda kh, n: (kh, n, 0))] + [kv(f) for f in blocks] + [kv(f) for f in blocks]
            + [pl.BlockSpec((1, hd), lambda kh, n: (0, 0)), pl.BlockSpec((1, hd), lambda kh, n: (0, 0)),
               pl.BlockSpec((None, g, 1, 1), lambda kh, n: (kh, 0, 0, 0))])


def _attn_fwd(q3, k3, v3, qw, kw, sink, *, name):
    _, m, hd = q3.shape
    g, b = SWA_GROUP, SWA_BLOCK

    def body(q_ref, k0, kp, kc, v0, vp, vc, qw_ref, kw_ref, s_ref, o_ref):
        o_ref[...] = _attn_block(q_ref[...], k0[...], kp[...], kc[...], v0[...], vp[...], vc[...], qw_ref[...], kw_ref[...],
                                 s_ref[...], pl.program_id(1))

    return pl.pallas_call(
        body, name=name, grid=(SWA_KV_HEADS, m // b), in_specs=_attn_in_specs(),
        out_specs=pl.BlockSpec((g, b, hd), lambda kh, n: (kh, n, 0)),
        out_shape=jax.ShapeDtypeStruct(q3.shape, F32),
        compiler_params=_params(("parallel", "parallel")),
    )(q3, k3, k3, k3, v3, v3, v3, qw, kw, sink)


def _attn_bwd(do3, q3, k3, v3, qw, kw, sink, *, name):
    _, m, hd = q3.shape
    g, b = SWA_GROUP, SWA_BLOCK

    def body(q_ref, k0, kp, kc, v0, vp, vc, qw_ref, kw_ref, s_ref, do_ref, dq_ref, dk_ref, dv_ref, dqw_ref, dkw_ref, ds_ref):
        n = pl.program_id(1)

        @pl.when(n == 0)
        def _():
            for r in (dk_ref, dv_ref, dqw_ref, dkw_ref, ds_ref):
                r[...] = jnp.zeros_like(r)

        fn = lambda *a: _attn_block(*a, n)
        _, vjp = jax.vjp(fn, q_ref[...], k0[...], kp[...], kc[...], v0[...], vp[...], vc[...], qw_ref[...], kw_ref[...], s_ref[...])
        dq, dk0, dkp, dkc, dv0, dvp, dvc, dqw, dkw, dsk = vjp(do_ref[...])
        dq_ref[...] = dq
        prev = pl.multiple_of(jnp.maximum(n - 1, 0) * b, b)
        cur = pl.multiple_of(n * b, b)
        for ref, parts in ((dk_ref, (dk0, dkp, dkc)), (dv_ref, (dv0, dvp, dvc))):
            ref[pl.ds(0, b), :] += parts[0]
            ref[pl.ds(prev, b), :] += parts[1]
            ref[pl.ds(cur, b), :] += parts[2]
        dqw_ref[...] += dqw
        dkw_ref[...] += dkw
        ds_ref[...] += dsk

    acc = lambda shape: pl.BlockSpec((None,) + shape, lambda kh, n: (kh,) + (0,) * len(shape))
    return pl.pallas_call(
        body, name=name, grid=(SWA_KV_HEADS, m // b),
        in_specs=_attn_in_specs() + [pl.BlockSpec((g, b, hd), lambda kh, n: (kh, n, 0))],
        out_specs=[pl.BlockSpec((g, b, hd), lambda kh, n: (kh, n, 0)), acc((m, hd)), acc((m, hd)), acc((1, hd)), acc((1, hd)),
                   acc((g, 1, 1))],
        out_shape=[jax.ShapeDtypeStruct(q3.shape, F32), jax.ShapeDtypeStruct(k3.shape, F32), jax.ShapeDtypeStruct(v3.shape, F32),
                   jax.ShapeDtypeStruct((SWA_KV_HEADS, 1, hd), F32), jax.ShapeDtypeStruct((SWA_KV_HEADS, 1, hd), F32),
                   jax.ShapeDtypeStruct(sink.shape, F32)],
        compiler_params=_params(("parallel", "arbitrary")),
    )(q3, k3, k3, k3, v3, v3, v3, qw, kw, sink, do3)


def _loss_bwd(h, target, *, name):
    m, d = h.shape
    b = SWA_BLOCK

    def body(h_ref, t_ref, l_ref, dh_ref):
        i = pl.program_id(0)

        @pl.when(i == 0)
        def _():
            l_ref[...] = jnp.zeros_like(l_ref)
            dh_ref[...] = jnp.zeros_like(dh_ref)

        @pl.when(i > 0)
        def _():
            e = h_ref[...] - t_ref[...]
            dh_ref[...] = e * (1.0 / d)
            l_ref[...] += jnp.sum(jnp.sum(e * e, 0, keepdims=True), 1, keepdims=True) * (0.5 / d)

    return pl.pallas_call(
        body, name=name, grid=(m // b,),
        in_specs=[pl.BlockSpec((b, d), lambda i: (i, 0)), pl.BlockSpec((b, d), lambda i: (jnp.maximum(i - 1, 0), 0))],
        out_specs=[pl.BlockSpec((1, LANES), lambda i: (0, 0)), pl.BlockSpec((b, d), lambda i: (i, 0))],
        out_shape=[jax.ShapeDtypeStruct((1, LANES), F32), jax.ShapeDtypeStruct((m, d), F32)],
        compiler_params=_params(("arbitrary",)),
    )(h, target)


def _heads_major(a, heads):
    m = a.shape[0]
    return a.reshape(m, heads, SWA_HEAD_DIM).transpose(1, 0, 2)


def _heads_minor(a3):
    heads, m, hd = a3.shape
    return a3.transpose(1, 0, 2).reshape(m, heads * hd)


def _ffn_fwd(h, nw, w_up, cw, w_down, tag):
    u, hn = _norm_matmul(h, nw, w_up, o_seg=D_FF, name=f"ffn_up_{tag}")
    a = _ffn_act_fwd(u, cw, name=f"ffn_act_{tag}")
    return _mm_nn(a, w_down, res=h, name=f"ffn_down_{tag}"), (h, hn, u, a)


def _ffn_bwd(dh, saved, nw, w_up, cw, w_down, tag):
    h, hn, u, a = saved
    da = _mm_nn(dh, w_down, trans_w=True, name=f"ffn_da_{tag}")
    dw_down = _mm_tn(a, dh, name=f"ffn_dwdown_{tag}")
    du, dcw = _ffn_act_bwd(da, u, cw, name=f"ffn_act_bwd_{tag}")
    dhn = _mm_nn(du, w_up, a_seg=True, trans_w=True, name=f"ffn_dhn_{tag}")
    dw_up = _mm_tn(hn, du, b_seg=True, name=f"ffn_dwup_{tag}")
    dh_in, dnw = _rmsnorm_bwd(dhn, h, nw, dh, name=f"ffn_norm_bwd_{tag}")
    return dh_in, dnw, dw_up, dcw, dw_down


def _local_step(x, target, w, fetch=None, push=None):
    fetch = fetch or (lambda stage, after: {})
    push = push or (lambda stage, grads: None)
    plus = lambda a, zero: a if zero is None else a + zero
    seq, d = x.shape
    m = PAD_ROWS + N_META + seq
    h0 = jnp.concatenate([jnp.zeros((PAD_ROWS, d), F32), w["meta"], x], 0)

    pm, hn0 = _norm_matmul(h0, w["anw"][0], w["w_in_main"], o_seg=SEG, name="mix_in")
    pba = _mm_nn(hn0, w["w_in_tail"], name="mix_in_tail")
    qkvc = _dnpre_fwd(pm, w["dcw"], name="dn_conv")
    ba = pba[:, :2 * DN_HEADS].T.reshape(2, DN_HEADS, m, 1).transpose(1, 0, 2, 3)
    y, states = _delta_fwd(qkvc, pm, ba, w["hp"], w["dnw"], name="delta")
    y = _shortconv_fwd(pm, w["caw"], y, name="shortconv")
    w = {**w, **fetch("l0", y)}
    h1 = _mm_nn(y, w["w_out"], res=h0, name="mix_out")
    h2, ffn0 = _ffn_fwd(h1, w["fnw"][0], w["w_up0"], w["fcw"][0], w["w_down0"], "l0")

    w = {**w, **fetch("l1", h2)}
    qkv, hn2 = _norm_matmul(h2, w["anw"][1], w["wqkv"], name="attn_qkv")
    nq, nkv = SWA_HEADS * SWA_HEAD_DIM, SWA_KV_HEADS * SWA_HEAD_DIM
    q3 = _heads_major(qkv[:, :nq], SWA_HEADS)
    k3 = _heads_major(qkv[:, nq:nq + nkv], SWA_KV_HEADS)
    v3 = _heads_major(qkv[:, nq + nkv:], SWA_KV_HEADS)
    o3 = _attn_fwd(q3, k3, v3, w["qnw"], w["knw"], w["sink"], name="attn")
    o = _heads_minor(o3).astype(MXU_DTYPE)
    h3 = _mm_nn(o, w["wo"], res=h2, name="attn_out")
    h4, ffn1 = _ffn_fwd(h3, w["fnw"][1], w["w_up1"], w["fcw"][1], w["w_down1"], "l1")

    loss, dh4 = _loss_bwd(h4, target, name="loss")

    g = {}
    dh3, dfnw1, dwup1, dfcw1, dwdown1 = _ffn_bwd(dh4, ffn1, w["fnw"][1], w["w_up1"], w["fcw"][1], w["w_down1"], "l1")

    do = _mm_nn(dh3, w["wo"], trans_w=True, name="attn_do")
    g["wo"] = _mm_tn(o, dh3, name="attn_dwo")
    dq3, dk3, dv3, dqw, dkw, dsink = _attn_bwd(_heads_major(do, SWA_HEADS), q3, k3, v3, w["qnw"], w["knw"], w["sink"], name="attn_bwd")
    dqkv = jnp.concatenate([_heads_minor(dq3), _heads_minor(dk3), _heads_minor(dv3)], 1).astype(MXU_DTYPE)
    dhn2 = _mm_nn(dqkv, w["wqkv"], trans_w=True, name="attn_dhn")
    g["wqkv"] = _mm_tn(hn2, dqkv, name="attn_dwqkv")
    zero = push("l1", dict(w_up=dwup1, w_down=dwdown1, wo=g["wo"], wqkv=g["wqkv"]))
    dh2, danw1 = _rmsnorm_bwd(dhn2, h2, plus(w["anw"][1], zero), dh3, name="attn_norm_bwd")

    dh1, dfnw0, dwup0, dfcw0, dwdown0 = _ffn_bwd(dh2, ffn0, w["fnw"][0], w["w_up0"], w["fcw"][0], w["w_down0"], "l0")

    dy = _mm_nn(dh1, w["w_out"], trans_w=True, name="mix_dy")
    g["w_out"] = _mm_tn(y, dh1, name="mix_dwout")
    zero = push("l0", dict(w_up=dwup0, w_down=dwdown0, w_out=g["w_out"]))
    dpm, dqkvc, dba, dhp, ddnw = _delta_bwd(dy, qkvc, pm, ba, w["hp"], plus(w["dnw"], zero), states, name="delta_bwd")
    dpm, ddcw = _dnpre_bwd(dqkvc, pm, w["dcw"], dpm, name="dn_conv_bwd")
    dpm, dcaw = _shortconv_bwd(dy, pm, w["caw"], dpm, name="shortconv_bwd")
    dpba = jnp.pad(dba.transpose(1, 0, 2, 3).reshape(2 * DN_HEADS, m).T, ((0, 0), (0, LANES - 2 * DN_HEADS))).astype(MXU_DTYPE)
    dhn0 = _mm_nn(dpm, w["w_in_main"], a_seg=True, trans_w=True, name="mix_dhn")
    dhn0 = _mm_nn(dpba, w["w_in_tail"], res=dhn0, trans_w=True, name="mix_dhn_tail")
    g["w_in_main"] = _mm_tn(hn0, dpm, b_seg=True, name="mix_dwin")
    g["w_in_tail"] = _mm_tn(hn0, dpba, name="mix_dwin_tail")
    dh0, danw0 = _rmsnorm_bwd(dhn0, h0, w["anw"][0], dh1, name="mix_norm_bwd")

    g.update(
        x=dh0[PAD_ROWS + N_META:], meta=dh0[PAD_ROWS:PAD_ROWS + N_META], anw=[danw0, danw1], fnw=[dfnw0, dfnw1],
        caw=dcaw, dcw=ddcw, hp=dhp, dnw=ddnw, qnw=jnp.sum(dqw, 0), knw=jnp.sum(dkw, 0), sink=dsink,
        w_up=[dwup0, dwup1], fcw=[dfcw0, dfcw1], w_down=[dwdown0, dwdown1])
    return loss, g


N_TAIL = 2 * DN_HEADS


def _prepare_early(p):
    n_main = N_SEG * SEG
    w_in = p["mix_w_in"][0]
    tail = jnp.pad(w_in[:, n_main:], ((0, 0), (0, LANES - N_TAIL)))
    hp = jnp.zeros((DN_HEADS, SUBLANES, LANES), F32)
    hp = hp.at[:, 0, :].set(p["dn_a_log"][0][:, None]).at[:, 1, :].set(p["dn_dt_bias"][0][:, None])
    depth = p["ffn_conv_w"].shape[0]
    return dict(
        meta=p["meta_tokens"], anw=[p["attn_norm_w"][i:i + 1] for i in range(depth)],
        fnw=[p["ffn_norm_w"][i:i + 1] for i in range(depth)],
        w_in_main=w_in[:, :n_main], w_in_tail=tail,
        caw=_pad_w(p["conv_a_w"][0]), dcw=_pad_w(p["dn_conv_w"][0]), hp=hp, dnw=p["dn_norm_w"],
        qnw=p["swa_q_norm_w"], knw=p["swa_k_norm_w"], sink=p["swa_sinks"].reshape(SWA_KV_HEADS, SWA_GROUP, 1, 1),
        fcw=[_pad_w(p["ffn_conv_w"][i]) for i in range(depth)])


def _prepare_weights(p):
    return dict(
        _prepare_early(p), w_out=p["mix_w_out"][0], wo=p["swa_wo"][0],
        wqkv=jnp.concatenate([p["swa_wq"][0], p["swa_wk"][0], p["swa_wv"][0]], 1),
        w_up0=p["ffn_w_up"][0], w_up1=p["ffn_w_up"][1], w_down0=p["ffn_w_down"][0], w_down1=p["ffn_w_down"][1])


def _small_named(g):
    return dict(
        meta_tokens=g["meta"], attn_norm_w=jnp.concatenate(g["anw"], 0), ffn_norm_w=jnp.concatenate(g["fnw"], 0),
        conv_a_w=g["caw"][None, :3], dn_conv_w=g["dcw"][None, :4],
        dn_a_log=g["hp"][None, :, 0, 0], dn_dt_bias=g["hp"][None, :, 1, 0], dn_norm_w=g["dnw"],
        swa_q_norm_w=g["qnw"], swa_k_norm_w=g["knw"], swa_sinks=g["sink"].reshape(1, SWA_HEADS),
        ffn_conv_w=jnp.stack([c[:3] for c in g["fcw"]]))


def _reference_named(g):
    nq, nkv = SWA_HEADS * SWA_HEAD_DIM, SWA_KV_HEADS * SWA_HEAD_DIM
    return dict(
        _small_named(g), mix_w_in=jnp.concatenate([g["w_in_main"], g["w_in_tail"][:, :N_TAIL]], 1)[None],
        mix_w_out=g["w_out"][None], swa_wq=g["wqkv"][None, :, :nq], swa_wk=g["wqkv"][None, :, nq:nq + nkv],
        swa_wv=g["wqkv"][None, :, nq + nkv:], swa_wo=g["wo"][None],
        ffn_w_up=jnp.stack(g["w_up"]), ffn_w_down=jnp.stack(g["w_down"]))


def _my_index():
    return 4 * lax.axis_index("x") + 2 * lax.axis_index("y") + lax.axis_index("c")


def _all_to_all(arrays, *, name):
    n = len(arrays)

    def body(*refs):
        ins, outs = refs[:n], refs[n:2 * n]
        send_sems, recv_sems, local_sems = refs[2 * n:]
        x, y, c = lax.axis_index("x"), lax.axis_index("y"), lax.axis_index("c")
        me = 4 * x + 2 * y + c
        copies = []
        for i in range(n):
            local = pltpu.make_async_copy(ins[i].at[me], outs[i].at[me], local_sems.at[i])
            local.start()
            copies.append(local)
        for d in range(1, N_DEV):
            px, py, pc = x ^ (d >> 2), y ^ ((d >> 1) & 1), c ^ (d & 1)
            for i in range(n):
                remote = pltpu.make_async_remote_copy(
                    src_ref=ins[i].at[4 * px + 2 * py + pc], dst_ref=outs[i].at[me], send_sem=send_sems.at[i, d],
                    recv_sem=recv_sems.at[i, d], device_id=(px, py, pc), device_id_type=pl.DeviceIdType.MESH)
                remote.start()
                copies.append(remote)
        for cp in copies:
            cp.wait()

    hbm = pl.BlockSpec(memory_space=pl.ANY)
    return pl.pallas_call(
        body, name=name, in_specs=[hbm] * n, out_specs=[hbm] * n,
        out_shape=[jax.ShapeDtypeStruct(a.shape, a.dtype) for a in arrays],
        scratch_shapes=[pltpu.SemaphoreType.DMA((n, N_DEV)), pltpu.SemaphoreType.DMA((n, N_DEV)), pltpu.SemaphoreType.DMA((n,))],
    )(*arrays)


def _all_gather(arrays, *, name):
    n = len(arrays)

    def body(*refs):
        ins, outs = refs[:n], refs[n:2 * n]
        send_sems, recv_sems, local_sems = refs[2 * n:]
        x, y, c = lax.axis_index("x"), lax.axis_index("y"), lax.axis_index("c")
        me, sibling = (x, y, c), (x, y, 1 - c)
        chips = [(1 - x, y), (x, 1 - y), (1 - x, 1 - y)]

        def copy(i, k, block, to, src=None):
            rows = outs[i].at[4 * block[0] + 2 * block[1] + block[2]]
            return pltpu.make_async_remote_copy(
                src_ref=rows if src is None else src, dst_ref=rows, send_sem=send_sems.at[i, k], recv_sem=recv_sems.at[i, k],
                device_id=to, device_id_type=pl.DeviceIdType.MESH)

        mine = [pltpu.make_async_copy(ins[i], outs[i].at[4 * x + 2 * y + c], local_sems.at[i]) for i in range(n)]
        first = []
        for j, chip in enumerate(chips):
            first += [copy(i, 1 + j, me, (*chip, c), src=ins[i]) for i in range(n)]
        first += [copy(i, 0, me, sibling, src=ins[i]) for i in range(n)]
        for cp in first + mine:
            cp.start()
        passed = []
        for j, chip in enumerate(chips):
            for i in range(n):
                copy(i, 1 + j, (*chip, c), me).wait_recv()
                fwd = copy(i, 4 + j, (*chip, c), sibling)
                fwd.start()
                passed.append(fwd)
        for i in range(n):
            copy(i, 0, sibling, me).wait_recv()
            for j, chip in enumerate(chips):
                copy(i, 4 + j, (*chip, 1 - c), me).wait_recv()
        for cp in first + passed:
            cp.wait_send()
        for cp in mine:
            cp.wait()

    hbm = pl.BlockSpec(memory_space=pl.ANY)
    return pl.pallas_call(
        body, name=name, in_specs=[hbm] * n, out_specs=[hbm] * n,
        out_shape=[jax.ShapeDtypeStruct((N_DEV,) + tuple(a.shape), a.dtype) for a in arrays],
        scratch_shapes=[pltpu.SemaphoreType.DMA((n, 7)), pltpu.SemaphoreType.DMA((n, 7)), pltpu.SemaphoreType.DMA((n,))],
    )(*arrays)


def _peer(d):
    px, py, pc = lax.axis_index("x") ^ (d >> 2), lax.axis_index("y") ^ ((d >> 1) & 1), lax.axis_index("c") ^ (d & 1)
    return (px, py, pc), 4 * px + 2 * py + pc


def _push_copies(mode, srcs, lands, send_sems, recv_sems):
    me = _my_index()
    out = []
    for d in range(1, N_DEV):
        pos, idx = _peer(d)
        for i in range(len(srcs)):
            out.append(pltpu.make_async_remote_copy(
                src_ref=srcs[i] if mode == "gather" else srcs[i].at[idx], dst_ref=lands[i].at[me],
                send_sem=send_sems.at[i * N_DEV + d], recv_sem=recv_sems.at[i * N_DEV + d], device_id=pos,
                device_id_type=pl.DeviceIdType.MESH))
    return out


_HBM = pl.BlockSpec(memory_space=pltpu.HBM)
_SEM = pl.BlockSpec(memory_space=pltpu.SEMAPHORE)


def _push_start(mode, arrays, follows, *, name):
    n = len(arrays)
    blocks = [a.shape if mode == "gather" else a.shape[1:] for a in arrays]
    lands = [lax.empty((N_DEV,) + tuple(b), a.dtype) for a, b in zip(arrays, blocks)]

    def body(*refs):
        srcs, land_refs = refs[:n], refs[n:2 * n]
        send_sems, recv_sems = refs[2 * n + 1], refs[2 * n + 2]
        zero = refs[-1]
        for cp in _push_copies(mode, srcs, land_refs, send_sems, recv_sems):
            cp.start()
        zero[...] = jnp.zeros_like(zero)

    hbm_in = [pltpu.with_memory_space_constraint(a, pltpu.HBM) for a in list(arrays) + lands]
    outs = pl.pallas_call(
        body, name=name,
        out_shape=[pltpu.SemaphoreType.DMA((n * N_DEV,)), pltpu.SemaphoreType.DMA((n * N_DEV,))]
        + [pltpu.HBM(a.shape, a.dtype) for a in hbm_in] + [jax.ShapeDtypeStruct((SUBLANES, LANES), F32)],
        in_specs=[_HBM] * (2 * n) + [pl.BlockSpec(memory_space=pl.ANY)],
        out_specs=[_SEM, _SEM] + [_HBM] * (2 * n) + [pl.BlockSpec(memory_space=pltpu.VMEM)],
        input_output_aliases={i: 2 + i for i in range(2 * n)},
        compiler_params=pltpu.CompilerParams(has_side_effects=pltpu.SideEffectType.DATAFLOW_SIDE_EFFECTING),
    )(*hbm_in, follows)
    return dict(mode=mode, sems=outs[:2], srcs=outs[2:2 + n], lands=outs[2 + n:2 + 2 * n], zero=outs[-1])


def _push_wait(push, follows, *, name):
    n = len(push["srcs"])
    mode = push["mode"]

    def body(*refs):
        srcs, land_refs = refs[:n], refs[n:2 * n]
        send_sems, recv_sems = refs[2 * n], refs[2 * n + 1]
        for cp in _push_copies(mode, srcs, land_refs, send_sems, recv_sems):
            cp.wait_send()
            cp.wait_recv()

    args = list(push["srcs"]) + list(push["lands"])
    outs = pl.pallas_call(
        body, name=name, out_shape=[pltpu.HBM(a.shape, a.dtype) for a in args],
        in_specs=[_HBM] * (2 * n) + [_SEM, _SEM, pl.BlockSpec(memory_space=pl.ANY)], out_specs=[_HBM] * (2 * n),
        input_output_aliases={i: i for i in range(2 * n)},
        compiler_params=pltpu.CompilerParams(has_side_effects=pltpu.SideEffectType.DATAFLOW_SIDE_EFFECTING),
    )(*args, *push["sems"], follows)
    me = _my_index()
    got = []
    for src, land in zip(outs[:n], outs[n:]):
        own = src if mode == "gather" else lax.dynamic_index_in_dim(src, me, 0, keepdims=False)
        got.append(lax.dynamic_update_index_in_dim(land, own, me, 0))
    return got


def _row_tile(r, c):
    best = None
    for t in range(2 * SUBLANES, r + 1, 2 * SUBLANES):
        if r % t == 0 and N_DEV * t * c * 4 <= 6 * 1024 * 1024:
            best = t
    return best or r


def _adamw(parts, w, m, v, *, name):
    r, c = w.shape
    tr = _row_tile(r, c)

    def body(p_ref, w_ref, m_ref, v_ref, g_ref, d_ref, nm_ref, nv_ref):
        g = p_ref[0].astype(F32)
        for j in range(1, N_DEV):
            g = g + p_ref[j].astype(F32)
        m2 = ADAM_B1 * m_ref[...] + (1.0 - ADAM_B1) * g
        v2 = ADAM_B2 * v_ref[...] + (1.0 - ADAM_B2) * jnp.square(g)
        m_hat = m2 / (1.0 - ADAM_B1 ** ADAM_STEP)
        v_hat = v2 / (1.0 - ADAM_B2 ** ADAM_STEP)
        g_ref[...] = g
        d_ref[...] = -ADAM_LR * (m_hat / (jnp.sqrt(v_hat) + ADAM_EPS) + ADAM_WD * w_ref[...])
        nm_ref[...] = m2
        nv_ref[...] = v2

    blk = pl.BlockSpec((tr, c), lambda i: (i, 0))
    out = jax.ShapeDtypeStruct((r, c), F32)
    return pl.pallas_call(
        body, name=name, grid=(r // tr,),
        in_specs=[pl.BlockSpec((N_DEV, tr, c), lambda i: (0, i, 0)), blk, blk, blk],
        out_specs=[blk, blk, blk, blk], out_shape=[out, out, out, out],
        compiler_params=_params(("parallel",)),
    )(parts, w, m, v)


SHARD_AXIS = dict(
    meta_tokens=1, attn_norm_w=None, ffn_norm_w=None, mix_w_in=2, conv_a_w=2, dn_conv_w=2, dn_a_log=None, dn_dt_bias=None,
    dn_norm_w=None, mix_w_out=1, swa_wq=1, swa_wk=1, swa_wv=1, swa_q_norm_w=None, swa_k_norm_w=None, swa_sinks=None,
    swa_wo=1, ffn_w_up=2, ffn_conv_w=2, ffn_w_down=1)
WEIGHTS = list(SHARD_AXIS)
BIG = ["mix_w_in", "mix_w_out", "swa_wq", "swa_wk", "swa_wv", "swa_wo", "ffn_w_up", "ffn_w_down"]
SMALL = [k for k in WEIGHTS if k not in BIG]
SMALL_SHARDED = [k for k in SMALL if SHARD_AXIS[k] is not None]


def _whole(g8, axis):
    t = jnp.moveaxis(g8, 0, axis)
    return t.reshape(t.shape[:axis] + (t.shape[axis] * t.shape[axis + 1],) + t.shape[axis + 2:])


def _by_owner(a, axis):
    s = a.shape[axis] // N_DEV
    return jnp.moveaxis(a.reshape(a.shape[:axis] + (N_DEV, s) + a.shape[axis + 1:]), axis, 0)


def _pack(arrays, lead=0):
    flat = jnp.concatenate([a.reshape(a.shape[:lead] + (-1,)) for a in arrays], -1)
    n = flat.shape[-1]
    rows = -(-n // (SUBLANES * LANES)) * SUBLANES
    flat = jnp.pad(flat, [(0, 0)] * lead + [(0, rows * LANES - n)])
    return flat.reshape(flat.shape[:lead] + (rows, LANES))


def _unpack(buf, shapes, lead=0):
    flat = buf.reshape(buf.shape[:lead] + (-1,))
    out, o = [], 0
    for s in shapes:
        n = 1
        for e in s:
            n *= e
        out.append(flat[..., o:o + n].reshape(buf.shape[:lead] + tuple(s)))
        o += n
    return out


def kernel(x, meta_tokens, attn_norm_w, ffn_norm_w, mix_w_in, conv_a_w, dn_conv_w, dn_a_log, dn_dt_bias, dn_norm_w, mix_w_out, swa_wq, swa_wk, swa_wv, swa_q_norm_w, swa_k_norm_w, swa_sinks, swa_wo, ffn_w_up, ffn_conv_w, ffn_w_down, loss_target, m_meta_tokens, m_attn_norm_w, m_ffn_norm_w, m_mix_w_in, m_conv_a_w, m_dn_conv_w, m_dn_a_log, m_dn_dt_bias, m_dn_norm_w, m_mix_w_out, m_swa_wq, m_swa_wk, m_swa_wv, m_swa_q_norm_w, m_swa_k_norm_w, m_swa_sinks, m_swa_wo, m_ffn_w_up, m_ffn_conv_w, m_ffn_w_down, v_meta_tokens, v_attn_norm_w, v_ffn_norm_w, v_mix_w_in, v_conv_a_w, v_dn_conv_w, v_dn_a_log, v_dn_dt_bias, v_dn_norm_w, v_mix_w_out, v_swa_wq, v_swa_wk, v_swa_wv, v_swa_q_norm_w, v_swa_k_norm_w, v_swa_sinks, v_swa_wo, v_ffn_w_up, v_ffn_conv_w, v_ffn_w_down):
    w = dict(meta_tokens=meta_tokens, attn_norm_w=attn_norm_w, ffn_norm_w=ffn_norm_w, mix_w_in=mix_w_in, conv_a_w=conv_a_w, dn_conv_w=dn_conv_w, dn_a_log=dn_a_log, dn_dt_bias=dn_dt_bias, dn_norm_w=dn_norm_w, mix_w_out=mix_w_out, swa_wq=swa_wq, swa_wk=swa_wk, swa_wv=swa_wv, swa_q_norm_w=swa_q_norm_w, swa_k_norm_w=swa_k_norm_w, swa_sinks=swa_sinks, swa_wo=swa_wo, ffn_w_up=ffn_w_up, ffn_conv_w=ffn_conv_w, ffn_w_down=ffn_w_down)
    mom = dict(meta_tokens=m_meta_tokens, attn_norm_w=m_attn_norm_w, ffn_norm_w=m_ffn_norm_w, mix_w_in=m_mix_w_in, conv_a_w=m_conv_a_w, dn_conv_w=m_dn_conv_w, dn_a_log=m_dn_a_log, dn_dt_bias=m_dn_dt_bias, dn_norm_w=m_dn_norm_w, mix_w_out=m_mix_w_out, swa_wq=m_swa_wq, swa_wk=m_swa_wk, swa_wv=m_swa_wv, swa_q_norm_w=m_swa_q_norm_w, swa_k_norm_w=m_swa_k_norm_w, swa_sinks=m_swa_sinks, swa_wo=m_swa_wo, ffn_w_up=m_ffn_w_up, ffn_conv_w=m_ffn_conv_w, ffn_w_down=m_ffn_w_down)
    var = dict(meta_tokens=v_meta_tokens, attn_norm_w=v_attn_norm_w, ffn_norm_w=v_ffn_norm_w, mix_w_in=v_mix_w_in, conv_a_w=v_conv_a_w, dn_conv_w=v_dn_conv_w, dn_a_log=v_dn_a_log, dn_dt_bias=v_dn_dt_bias, dn_norm_w=v_dn_norm_w, mix_w_out=v_mix_w_out, swa_wq=v_swa_wq, swa_wk=v_swa_wk, swa_wv=v_swa_wv, swa_q_norm_w=v_swa_q_norm_w, swa_k_norm_w=v_swa_k_norm_w, swa_sinks=v_swa_sinks, swa_wo=v_swa_wo, ffn_w_up=v_ffn_w_up, ffn_conv_w=v_ffn_conv_w, ffn_w_down=v_ffn_w_down)
    me = _my_index()

    shard16 = {k: w[k].astype(MXU_DTYPE) for k in BIG}
    small_shard_shapes = [w[k].shape for k in SMALL_SHARDED]
    got = _all_gather([shard16["mix_w_in"], _pack([w[k] for k in SMALL_SHARDED])], name="gather_weights")
    whole = {"mix_w_in": _whole(got[0], SHARD_AXIS["mix_w_in"])}
    for k, a in zip(SMALL_SHARDED, _unpack(got[1], small_shard_shapes, lead=1)):
        whole[k] = _whole(a, SHARD_AXIS[k])
    for k in SMALL:
        whole.setdefault(k, w[k])
    stages = {"l0": [("mix_w_out", 0), ("ffn_w_up", 0), ("ffn_w_down", 0)],
              "l1": [("swa_wq", 0), ("swa_wk", 0), ("swa_wv", 0), ("swa_wo", 0), ("ffn_w_up", 1), ("ffn_w_down", 1)]}
    pushed = {}
    follows = jnp.zeros((SUBLANES, LANES), F32)
    for stage in ("l0", "l1"):
        pushed[stage] = _push_start("gather", [shard16[k][l] for k, l in stages[stage]], follows, name=f"push_weights_{stage}")
        follows = pushed[stage]["zero"]
    early = _prepare_early(whole)
    early["anw"][0] = early["anw"][0] + follows[0, 0]

    def fetch(stage, after):
        got = _push_wait(pushed[stage], after, name=f"wait_weights_{stage}")
        full = {kl: _whole(a, SHARD_AXIS[kl[0]] - 1) for kl, a in zip(stages[stage], got)}
        if stage == "l0":
            return dict(w_out=full["mix_w_out", 0], w_up0=full["ffn_w_up", 0], w_down0=full["ffn_w_down", 0])
        wqkv = jnp.concatenate([full["swa_wq", 0], full["swa_wk", 0], full["swa_wv", 0]], 1)
        return dict(wqkv=wqkv, wo=full["swa_wo", 0], w_up1=full["ffn_w_up", 1], w_down1=full["ffn_w_down", 1])

    nq, nkv = SWA_HEADS * SWA_HEAD_DIM, SWA_KV_HEADS * SWA_HEAD_DIM
    grad_pushes = {}

    def push(stage, gd):
        if stage == "l1":
            layer = 1
            named = {("swa_wq", 0): gd["wqkv"][:, :nq], ("swa_wk", 0): gd["wqkv"][:, nq:nq + nkv],
                     ("swa_wv", 0): gd["wqkv"][:, nq + nkv:], ("swa_wo", 0): gd["wo"]}
        else:
            layer = 0
            named = {("mix_w_out", 0): gd["w_out"]}
        named[("ffn_w_up", layer)] = gd["w_up"]
        named[("ffn_w_down", layer)] = gd["w_down"]
        sent = [_by_owner(named[kl], SHARD_AXIS[kl[0]] - 1) for kl in stages[stage]]
        grad_pushes[stage] = _push_start("scatter", sent, jnp.zeros((SUBLANES, LANES), F32), name=f"push_grads_{stage}")
        return grad_pushes[stage]["zero"][0, 0]

    loss, g = _local_step(x[0], loss_target[0], early, fetch, push)
    grads = _small_named(g)

    g_in = jnp.concatenate([g["w_in_main"], g["w_in_tail"][:, :N_TAIL]], 1)
    (in_parts,) = _all_to_all([_by_owner(g_in, SHARD_AXIS["mix_w_in"] - 1)], name="scatter_grads")
    parts = {("mix_w_in", 0): in_parts}
    follows = in_parts
    for stage in ("l1", "l0"):
        got = _push_wait(grad_pushes[stage], follows, name=f"wait_grads_{stage}")
        parts.update(zip(stages[stage], got))
        follows = got[0]
    out_g, out_d, out_m, out_v = {}, {}, {}, {}
    for k in BIG:
        per_layer = []
        for l in range(w[k].shape[0]):
            r, c = w[k].shape[1:]
            per_layer.append(_adamw(parts[k, l].reshape(N_DEV, r, c), w[k][l], mom[k][l], var[k][l], name=f"adamw_{k}_{l}"))
        out_g[k], out_d[k], out_m[k], out_v[k] = [jnp.stack(t) for t in zip(*per_layer)]

    small_shapes = [grads[k].shape for k in SMALL]
    (all_small,) = _all_gather([_pack([loss] + [grads[k].astype(F32) for k in SMALL])], name="gather_small_grads")
    loss_parts, *small_parts = _unpack(all_small, [loss.shape] + small_shapes, lead=1)
    mine = []
    for k, p in zip(SMALL, small_parts):
        ax = SHARD_AXIS[k]
        mine.append(p if ax is None else lax.dynamic_slice_in_dim(p, me * w[k].shape[ax], w[k].shape[ax], 1 + ax))
    zero = jnp.zeros(loss.shape, F32)
    packed = [_pack([z] + [d[k] for k in SMALL]) for z, d in ((zero, w), (zero, mom), (zero, var))]
    res = _adamw(_pack([loss_parts] + mine, lead=1), *packed, name="adamw_small")
    shapes = [loss.shape] + [w[k].shape for k in SMALL]
    for t, dst in zip(res, (out_g, out_d, out_m, out_v)):
        parts = _unpack(t, shapes)
        if dst is out_g:
            total_loss = parts[0][0, 0]
        for k, a in zip(SMALL, parts[1:]):
            dst[k] = a

    return (total_loss, g["x"][None], *[out_g[k] for k in WEIGHTS], *[out_d[k] for k in WEIGHTS],
            *[out_m[k] for k in WEIGHTS], *[out_v[k] for k in WEIGHTS])
```

```python
import functools

import jax
import jax.numpy as jnp
from jax import lax
from jax.experimental import pallas as pl
from jax.experimental.pallas import tpu as pltpu

F32 = jnp.float32
BF16 = jnp.bfloat16
MXU_DTYPE = BF16
GRAD_WIRE_DTYPE = BF16

D_MODEL = 1024
N_META = 16
PAD_ROWS = 112
D_CONV = 512
DN_HEADS = 4
DN_HEAD_DIM = 128
DN_DIM = DN_HEADS * DN_HEAD_DIM
DN_CHUNK = 64
SEG = 512
N_SEG = 7
SWA_HEADS = 16
SWA_KV_HEADS = 4
SWA_GROUP = SWA_HEADS // SWA_KV_HEADS
SWA_HEAD_DIM = 64
SWA_BLOCK = 128
D_FF = 2816
EPS = 1e-6
NEG = -1e30
N_DEV = 8

ADAM_LR = 0.001
ADAM_B1 = 0.9
ADAM_B2 = 0.999
ADAM_EPS = 1e-08
ADAM_WD = 0.01
ADAM_STEP = 10

VMEM_LIMIT_BYTES = 52 * 1024 * 1024
SUBLANES = 8
LANES = 128


def _pick(n, prefs):
    for p in prefs:
        if n % p == 0:
            return p
    return n


def _params(sem, vmem=VMEM_LIMIT_BYTES):
    return pltpu.CompilerParams(dimension_semantics=sem, vmem_limit_bytes=vmem)


def _rms(x, w):
    return x * lax.rsqrt(jnp.mean(x * x, -1, keepdims=True) + EPS) * w


def _norm_matmul(h, nw, w, *, o_seg=None, name):
    m, k = h.shape
    n = w.shape[1]
    tm = _pick(m, (1408, 384, 128))
    tn = _pick(o_seg or n, (1408, 1024, 512, 256, 128))

    def body(h_ref, nw_ref, w_ref, o_ref, hn_ref, hn_s):
        @pl.when(pl.program_id(1) == 0)
        def _():
            hn = _rms(h_ref[...], nw_ref[...]).astype(MXU_DTYPE)
            hn_s[...] = hn
            hn_ref[...] = hn

        o_ref[...] = jnp.dot(hn_s[...], w_ref[...], preferred_element_type=F32)

    if o_seg:
        per = o_seg // tn
        o_shape = jax.ShapeDtypeStruct((n // o_seg, m, o_seg), F32)
        o_spec = pl.BlockSpec((None, tm, tn), lambda i, j: (j // per, i, j % per))
    else:
        o_shape = jax.ShapeDtypeStruct((m, n), F32)
        o_spec = pl.BlockSpec((tm, tn), lambda i, j: (i, j))
    return pl.pallas_call(
        body, name=name, grid=(m // tm, n // tn),
        in_specs=[pl.BlockSpec((tm, k), lambda i, j: (i, 0)), pl.BlockSpec((1, k), lambda i, j: (0, 0)),
                  pl.BlockSpec((k, tn), lambda i, j: (0, j))],
        out_specs=[o_spec, pl.BlockSpec((tm, k), lambda i, j: (i, 0))],
        out_shape=[o_shape, jax.ShapeDtypeStruct((m, k), MXU_DTYPE)],
        scratch_shapes=[pltpu.VMEM((tm, k), MXU_DTYPE)],
        compiler_params=_params(("parallel", "arbitrary")),
    )(h, nw, w)


TILE_BUDGET_BYTES = 38 * 1024 * 1024
TILE_SIZES = (2816, 1792, 1536, 1408, 1024, 512, 256, 128)


def _divisor_tiles(n):
    return [t for t in TILE_SIZES if n % t == 0] or [n]


def _mm_nn(a, w, *, res=None, a_seg=False, trans_w=False, name):
    if a_seg:
        s, m, seg = a.shape
    else:
        m, seg = a.shape
        s = 1
    k = s * seg
    n = w.shape[0] if trans_w else w.shape[1]
    tm = _pick(m, (1408, 384, 128))
    ab = a.dtype.itemsize
    k_steps = [(sb, seg) for sb in range(s, 0, -1) if s % sb == 0] if a_seg else [(1, t) for t in _divisor_tiles(seg)]
    best = None
    for tn in _divisor_tiles(n):
        for sb, tk1 in k_steps:
            tk = sb * tk1
            nk = k // tk
            need = 2 * tm * tk * ab + 2 * tk * tn * 2 + 2 * tm * tn * 4 + (tm * tn * 4 if nk > 1 else 0) + (2 * tm * tn * 4 if res is not None else 0)
            if need <= TILE_BUDGET_BYTES and (best is None or tk * tn > best[0]):
                best = (tk * tn, tn, sb, tk1)
    _, tn, sb, tk1 = best
    tk = sb * tk1
    nk = k // tk
    w_dims = _DOT_DIMS["nt" if trans_w else "nn"]

    def body(*refs):
        a_ref, w_ref = refs[:2]
        r_ref = refs[2] if res is not None else None
        o_ref = refs[3 if res is not None else 2]

        def partial_product():
            if not a_seg:
                return lax.dot_general(a_ref[...].astype(MXU_DTYPE), w_ref[...], w_dims, preferred_element_type=F32)
            out = None
            for t in range(sb):
                wt = w_ref[:, t * seg:(t + 1) * seg] if trans_w else w_ref[t * seg:(t + 1) * seg, :]
                d = lax.dot_general(a_ref[t].astype(MXU_DTYPE), wt, w_dims, preferred_element_type=F32)
                out = d if out is None else out + d
            return out

        if nk == 1:
            o_ref[...] = partial_product() if res is None else partial_product() + r_ref[...]
            return
        acc = refs[-1]
        kk = pl.program_id(2)

        @pl.when(kk == 0)
        def _():
            acc[...] = jnp.zeros_like(acc)

        acc[...] += partial_product()

        @pl.when(kk == nk - 1)
        def _():
            o_ref[...] = acc[...] if res is None else acc[...] + r_ref[...]

    a_spec = pl.BlockSpec((sb, tm, seg), lambda i, j, kk: (kk, i, 0)) if a_seg else pl.BlockSpec((tm, tk), lambda i, j, kk: (i, kk))
    w_spec = pl.BlockSpec((tn, tk), lambda i, j, kk: (j, kk)) if trans_w else pl.BlockSpec((tk, tn), lambda i, j, kk: (kk, j))
    in_specs = [a_spec, w_spec]
    args = [a, w]
    if res is not None:
        in_specs.append(pl.BlockSpec((tm, tn), lambda i, j, kk: (i, j)))
        args.append(res)
    return pl.pallas_call(
        body, name=name, grid=(m // tm, n // tn, nk), in_specs=in_specs,
        out_specs=pl.BlockSpec((tm, tn), lambda i, j, kk: (i, j)),
        out_shape=jax.ShapeDtypeStruct((m, n), F32),
        scratch_shapes=[pltpu.VMEM((tm, tn), F32)] if nk > 1 else [],
        compiler_params=_params(("parallel", "parallel", "arbitrary")),
    )(*args)


def _mm_tn(a, b, *, b_seg=False, out_dtype=None, name):
    out_dtype = out_dtype or GRAD_WIRE_DTYPE
    m, ka = a.shape
    if b_seg:
        s, _, seg = b.shape
        n = s * seg
    else:
        n = b.shape[1]
        seg = n
    tmc = _pick(m, (1408, 384, 128))
    best = None
    for tka in _divisor_tiles(ka):
        for tn in _divisor_tiles(seg):
            need = 2 * tmc * tka * a.dtype.itemsize + 2 * tmc * tn * b.dtype.itemsize + tka * tn * 4 + 2 * tka * tn * 4
            if need <= TILE_BUDGET_BYTES and (best is None or (tka * tn, tn) > best[:2]):
                best = (tka * tn, tn, tka)
    _, tn, tka = best
    nm = m // tmc

    def body(a_ref, b_ref, o_ref, acc):
        mm = pl.program_id(2)

        @pl.when(mm == 0)
        def _():
            acc[...] = jnp.zeros_like(acc)

        acc[...] += lax.dot_general(a_ref[...].astype(MXU_DTYPE), b_ref[...].astype(MXU_DTYPE),
                                    (((0,), (0,)), ((), ())), preferred_element_type=F32)

        @pl.when(mm == nm - 1)
        def _():
            o_ref[...] = acc[...].astype(o_ref.dtype)

    if b_seg:
        per = seg // tn
        b_spec = pl.BlockSpec((None, tmc, tn), lambda i, j, mm: (j // per, mm, j % per))
    else:
        b_spec = pl.BlockSpec((tmc, tn), lambda i, j, mm: (mm, j))
    return pl.pallas_call(
        body, name=name, grid=(ka // tka, n // tn, nm),
        in_specs=[pl.BlockSpec((tmc, tka), lambda i, j, mm: (mm, i)), b_spec],
        out_specs=pl.BlockSpec((tka, tn), lambda i, j, mm: (i, j)),
        out_shape=jax.ShapeDtypeStruct((ka, n), out_dtype),
        scratch_shapes=[pltpu.VMEM((tka, tn), F32)],
        compiler_params=_params(("parallel", "parallel", "arbitrary")),
    )(a, b)


def _rmsnorm_bwd(dhn, h, nw, dres, *, name):
    m, d = h.shape
    tm = _pick(m, (384, 128))

    def body(dhn_ref, h_ref, nw_ref, dres_ref, dh_ref, dnw_ref):
        i = pl.program_id(0)
        x = h_ref[...]
        r = lax.rsqrt(jnp.mean(x * x, -1, keepdims=True) + EPS)
        xh = x * r
        dy = dhn_ref[...]
        dxh = dy * nw_ref[...]
        dx = r * (dxh - xh * jnp.mean(dxh * xh, -1, keepdims=True))
        row = i * tm + lax.broadcasted_iota(jnp.int32, (tm, 1), 0)
        dh_ref[...] = jnp.where(row >= PAD_ROWS, dres_ref[...] + dx, 0.0)

        @pl.when(i == 0)
        def _():
            dnw_ref[...] = jnp.zeros_like(dnw_ref)

        dnw_ref[...] += jnp.sum(dy * xh, 0, keepdims=True)

    return pl.pallas_call(
        body, name=name, grid=(m // tm,),
        in_specs=[pl.BlockSpec((tm, d), lambda i: (i, 0)), pl.BlockSpec((tm, d), lambda i: (i, 0)),
                  pl.BlockSpec((1, d), lambda i: (0, 0)), pl.BlockSpec((tm, d), lambda i: (i, 0))],
        out_specs=[pl.BlockSpec((tm, d), lambda i: (i, 0)), pl.BlockSpec((1, d), lambda i: (0, 0))],
        out_shape=[jax.ShapeDtypeStruct((m, d), F32), jax.ShapeDtypeStruct((1, d), F32)],
        compiler_params=_params(("arbitrary",)),
    )(dhn, h, nw, dres)


ROW_CHUNK = 248


def _row_chunks(m):
    out, s = [], SUBLANES
    while s < m:
        n = min(ROW_CHUNK, m - s)
        out.append((s, n))
        s += n
    return out


def _conv_at(load, w, width, s, n):
    acc = w[width - 1:width, :] * load(s, n)
    for j in range(width - 1):
        acc = acc + w[j:j + 1, :] * load(s - (width - 1 - j), n)
    return acc


def _conv_t_at(load, w, width, s, n):
    acc = w[width - 1:width, :] * load(s, n)
    for j in range(width - 1):
        acc = acc + w[j:j + 1, :] * load(s + (width - 1 - j), n)
    return acc


def _dconv_w(load_x, d, width, s, n):
    rows = [jnp.sum(d * load_x(s - (width - 1 - j), n), 0, keepdims=True) for j in range(width)]
    rows.append(jnp.zeros((SUBLANES - width, d.shape[1]), F32))
    return jnp.concatenate(rows, 0)


def _pad_w(w):
    return jnp.concatenate([w, jnp.zeros((SUBLANES - w.shape[0], w.shape[1]), w.dtype)], 0)


def _sigmoid(x):
    return 1.0 / (1.0 + jnp.exp(-x))


def _ffn_act_fwd(u, cw, *, name):
    _, m, f = u.shape
    cb = _pick(f, (256, 128))
    chunks = _row_chunks(m)

    def body(g_ref, v_ref, w_ref, o_ref):
        w = w_ref[...]
        o_ref[pl.ds(0, SUBLANES), :] = jnp.zeros((SUBLANES, cb), o_ref.dtype)
        for s, n in chunks:
            c = _conv_at(lambda a, b: g_ref[pl.ds(a, b), :], w, 3, s, n)
            o_ref[pl.ds(s, n), :] = (c * _sigmoid(c) * v_ref[pl.ds(s, n), :]).astype(o_ref.dtype)

    return pl.pallas_call(
        body, name=name, grid=(f // cb,),
        in_specs=[pl.BlockSpec((None, m, cb), lambda j: (0, 0, j)), pl.BlockSpec((None, m, cb), lambda j: (1, 0, j)),
                  pl.BlockSpec((SUBLANES, cb), lambda j: (0, j))],
        out_specs=pl.BlockSpec((m, cb), lambda j: (0, j)),
        out_shape=jax.ShapeDtypeStruct((m, f), MXU_DTYPE),
        compiler_params=_params(("parallel",)),
    )(u, u, cw)


def _ffn_act_bwd(da, u, cw, *, name):
    _, m, f = u.shape
    cb = LANES
    chunks = _row_chunks(m)

    def body(da_ref, g_ref, v_ref, w_ref, du_ref, dw_ref, dg_s):
        w = w_ref[...]
        zeros8 = jnp.zeros((SUBLANES, cb), F32)
        dg_s[pl.ds(0, SUBLANES), :] = zeros8
        dg_s[pl.ds(m, SUBLANES), :] = zeros8
        du_ref[0, pl.ds(0, SUBLANES), :] = zeros8.astype(du_ref.dtype)
        du_ref[1, pl.ds(0, SUBLANES), :] = zeros8.astype(du_ref.dtype)
        load_g = lambda a, b: g_ref[pl.ds(a, b), :]
        dw = jnp.zeros((SUBLANES, cb), F32)
        for s, n in chunks:
            c = _conv_at(load_g, w, 3, s, n)
            sg = _sigmoid(c)
            d = da_ref[pl.ds(s, n), :]
            du_ref[1, pl.ds(s, n), :] = (d * (c * sg)).astype(du_ref.dtype)
            dc = d * v_ref[pl.ds(s, n), :] * (sg * (1.0 + c * (1.0 - sg)))
            dg_s[pl.ds(s, n), :] = dc
            dw = dw + _dconv_w(load_g, dc, 3, s, n)
        dw_ref[...] = dw
        for s, n in chunks:
            du_ref[0, pl.ds(s, n), :] = _conv_t_at(lambda a, b: dg_s[pl.ds(a, b), :], w, 3, s, n).astype(du_ref.dtype)

    return pl.pallas_call(
        body, name=name, grid=(f // cb,),
        in_specs=[pl.BlockSpec((m, cb), lambda j: (0, j)), pl.BlockSpec((None, m, cb), lambda j: (0, 0, j)),
                  pl.BlockSpec((None, m, cb), lambda j: (1, 0, j)), pl.BlockSpec((SUBLANES, cb), lambda j: (0, j))],
        out_specs=[pl.BlockSpec((2, m, cb), lambda j: (0, 0, j)), pl.BlockSpec((SUBLANES, cb), lambda j: (0, j))],
        out_shape=[jax.ShapeDtypeStruct((2, m, f), MXU_DTYPE), jax.ShapeDtypeStruct((SUBLANES, f), F32)],
        scratch_shapes=[pltpu.VMEM((m + SUBLANES, cb), F32)],
        compiler_params=_params(("parallel",)),
    )(da, u, u, cw)


def _shortconv_fwd(pm, cw, y, *, name):
    _, m, seg = pm.shape
    cb = _pick(seg, (256, 128))
    chunks = _row_chunks(m)

    def body(gi_ref, go_ref, ah_ref, w_ref, y_in, o_ref):
        del y_in
        w = w_ref[...]
        o_ref[pl.ds(0, SUBLANES), :] = jnp.zeros((SUBLANES, cb), o_ref.dtype)
        load_m = lambda a, b: gi_ref[pl.ds(a, b), :] * ah_ref[pl.ds(a, b), :]
        for s, n in chunks:
            o_ref[pl.ds(s, n), :] = (go_ref[pl.ds(s, n), :] * _conv_at(load_m, w, 3, s, n)).astype(o_ref.dtype)

    return pl.pallas_call(
        body, name=name, grid=(seg // cb,),
        in_specs=[pl.BlockSpec((None, m, cb), lambda j: (0, 0, j)), pl.BlockSpec((None, m, cb), lambda j: (1, 0, j)),
                  pl.BlockSpec((None, m, cb), lambda j: (2, 0, j)), pl.BlockSpec((SUBLANES, cb), lambda j: (0, j)),
                  pl.BlockSpec(memory_space=pl.ANY)],
        out_specs=pl.BlockSpec((m, cb), lambda j: (0, j)),
        out_shape=jax.ShapeDtypeStruct(y.shape, y.dtype),
        input_output_aliases={4: 0},
        compiler_params=_params(("parallel",)),
    )(pm, pm, pm, cw, y)


def _shortconv_bwd(dy, pm, cw, dpm, *, name):
    _, m, seg = pm.shape
    cb = LANES
    chunks = _row_chunks(m)

    def body(dy_ref, gi_ref, go_ref, ah_ref, w_ref, dpm_in, dp_ref, dw_ref, dc_s):
        del dpm_in
        w = w_ref[...]
        zeros8 = jnp.zeros((SUBLANES, cb), F32)
        dc_s[pl.ds(0, SUBLANES), :] = zeros8
        dc_s[pl.ds(m, SUBLANES), :] = zeros8
        for t in range(3):
            dp_ref[t, pl.ds(0, SUBLANES), :] = zeros8.astype(dp_ref.dtype)
        load_m = lambda a, b: gi_ref[pl.ds(a, b), :] * ah_ref[pl.ds(a, b), :]
        dw = jnp.zeros((SUBLANES, cb), F32)
        for s, n in chunks:
            d = dy_ref[pl.ds(s, n), :]
            dp_ref[1, pl.ds(s, n), :] = (d * _conv_at(load_m, w, 3, s, n)).astype(dp_ref.dtype)
            dc = d * go_ref[pl.ds(s, n), :]
            dc_s[pl.ds(s, n), :] = dc
            dw = dw + _dconv_w(load_m, dc, 3, s, n)
        dw_ref[...] = dw
        for s, n in chunks:
            dm = _conv_t_at(lambda a, b: dc_s[pl.ds(a, b), :], w, 3, s, n)
            dp_ref[0, pl.ds(s, n), :] = (dm * ah_ref[pl.ds(s, n), :]).astype(dp_ref.dtype)
            dp_ref[2, pl.ds(s, n), :] = (dm * gi_ref[pl.ds(s, n), :]).astype(dp_ref.dtype)

    return pl.pallas_call(
        body, name=name, grid=(seg // cb,),
        in_specs=[pl.BlockSpec((m, cb), lambda j: (0, j)), pl.BlockSpec((None, m, cb), lambda j: (0, 0, j)),
                  pl.BlockSpec((None, m, cb), lambda j: (1, 0, j)), pl.BlockSpec((None, m, cb), lambda j: (2, 0, j)),
                  pl.BlockSpec((SUBLANES, cb), lambda j: (0, j)), pl.BlockSpec(memory_space=pl.ANY)],
        out_specs=[pl.BlockSpec((3, m, cb), lambda j: (0, 0, j)), pl.BlockSpec((SUBLANES, cb), lambda j: (0, j))],
        out_shape=[jax.ShapeDtypeStruct(dpm.shape, dpm.dtype), jax.ShapeDtypeStruct((SUBLANES, seg), F32)],
        scratch_shapes=[pltpu.VMEM((m + SUBLANES, cb), F32)],
        input_output_aliases={5: 0},
        compiler_params=_params(("parallel",)),
    )(dy, pm, pm, pm, cw, dpm)


def _dnpre_fwd(pm, cw, *, name):
    _, m, seg = pm.shape
    cb = _pick(seg, (256, 128))
    per = seg // cb
    chunks = _row_chunks(m)

    def body(x_ref, w_ref, o_ref):
        w = w_ref[...]
        o_ref[pl.ds(0, SUBLANES), :] = jnp.zeros((SUBLANES, cb), F32)
        for s, n in chunks:
            c = _conv_at(lambda a, b: x_ref[pl.ds(a, b), :], w, 4, s, n)
            o_ref[pl.ds(s, n), :] = c * _sigmoid(c)

    return pl.pallas_call(
        body, name=name, grid=(3 * per,),
        in_specs=[pl.BlockSpec((None, m, cb), lambda j: (3 + j // per, 0, j % per)), pl.BlockSpec((SUBLANES, cb), lambda j: (0, j))],
        out_specs=pl.BlockSpec((None, m, cb), lambda j: (j // per, 0, j % per)),
        out_shape=jax.ShapeDtypeStruct((3, m, seg), F32),
        compiler_params=_params(("parallel",)),
    )(pm, cw)


def _dnpre_bwd(dqkv, pm, cw, dpm, *, name):
    _, m, seg = pm.shape
    cb = _pick(seg, (256, 128))
    per = seg // cb
    chunks = _row_chunks(m)

    def body(d_ref, x_ref, w_ref, dpm_in, dp_ref, dw_ref, dc_s):
        del dpm_in
        w = w_ref[...]
        zeros8 = jnp.zeros((SUBLANES, cb), F32)
        dc_s[pl.ds(0, SUBLANES), :] = zeros8
        dc_s[pl.ds(m, SUBLANES), :] = zeros8
        dp_ref[pl.ds(0, SUBLANES), :] = zeros8.astype(dp_ref.dtype)
        load_x = lambda a, b: x_ref[pl.ds(a, b), :]
        dw = jnp.zeros((SUBLANES, cb), F32)
        for s, n in chunks:
            c = _conv_at(load_x, w, 4, s, n)
            sg = _sigmoid(c)
            dc = d_ref[pl.ds(s, n), :] * (sg * (1.0 + c * (1.0 - sg)))
            dc_s[pl.ds(s, n), :] = dc
            dw = dw + _dconv_w(load_x, dc, 4, s, n)
        dw_ref[...] = dw
        for s, n in chunks:
            dp_ref[pl.ds(s, n), :] = _conv_t_at(lambda a, b: dc_s[pl.ds(a, b), :], w, 4, s, n).astype(dp_ref.dtype)

    return pl.pallas_call(
        body, name=name, grid=(3 * per,),
        in_specs=[pl.BlockSpec((None, m, cb), lambda j: (j // per, 0, j % per)),
                  pl.BlockSpec((None, m, cb), lambda j: (3 + j // per, 0, j % per)),
                  pl.BlockSpec((SUBLANES, cb), lambda j: (0, j)), pl.BlockSpec(memory_space=pl.ANY)],
        out_specs=[pl.BlockSpec((None, m, cb), lambda j: (3 + j // per, 0, j % per)), pl.BlockSpec((SUBLANES, cb), lambda j: (0, j))],
        out_shape=[jax.ShapeDtypeStruct(dpm.shape, dpm.dtype), jax.ShapeDtypeStruct((SUBLANES, 3 * seg), F32)],
        scratch_shapes=[pltpu.VMEM((m + SUBLANES, cb), F32)],
        input_output_aliases={3: 0},
        compiler_params=_params(("parallel",)),
    )(dqkv, pm, cw, dpm)


def _mxu_dot_impl(a, b, form):
    a = a.astype(MXU_DTYPE)
    b = b.astype(MXU_DTYPE)
    dims = {"nn": (((1,), (0,)), ((), ())), "nt": (((1,), (1,)), ((), ())), "tn": (((0,), (0,)), ((), ()))}[form]
    return lax.dot_general(a, b, dims, preferred_element_type=F32)


@functools.partial(jax.custom_vjp, nondiff_argnums=(2,))
def _mxu_dot(a, b, form):
    return _mxu_dot_impl(a, b, form)


def _mxu_dot_fwd(a, b, form):
    return _mxu_dot_impl(a, b, form), (a, b)


def _mxu_dot_bwd(form, saved, g):
    a, b = saved
    if form == "nn":
        return _mxu_dot_impl(g, b, "nt"), _mxu_dot_impl(a, g, "tn")
    if form == "nt":
        return _mxu_dot_impl(g, b, "nn"), _mxu_dot_impl(g, a, "tn")
    return _mxu_dot_impl(b, g, "nt"), _mxu_dot_impl(a, g, "nn")


_mxu_dot.defvjp(_mxu_dot_fwd, _mxu_dot_bwd)


_DOT_DIMS = {"nn": (((1,), (0,)), ((), ())), "nt": (((1,), (1,)), ((), ())), "tn": (((0,), (0,)), ((), ()))}


def _split(x):
    hi = x.astype(BF16)
    return hi, (x - hi.astype(F32)).astype(BF16)


def _dot3_impl(a, b, form):
    dg = lambda p, q: lax.dot_general(p, q, _DOT_DIMS[form], preferred_element_type=F32)
    ah, al = _split(a)
    bh, bl = _split(b)
    return dg(ah, bh) + (dg(ah, bl) + dg(al, bh))


@functools.partial(jax.custom_vjp, nondiff_argnums=(2,))
def _dot3(a, b, form):
    return _dot3_impl(a, b, form)


def _dot3_fwd(a, b, form):
    return _dot3_impl(a, b, form), (a, b)


def _dot3_bwd(form, saved, g):
    a, b = saved
    if form == "nn":
        return _dot3_impl(g, b, "nt"), _dot3_impl(a, g, "tn")
    if form == "nt":
        return _dot3_impl(g, b, "nn"), _dot3_impl(g, a, "tn")
    return _dot3_impl(b, g, "nt"), _dot3_impl(a, g, "nn")


_dot3.defvjp(_dot3_fwd, _dot3_bwd)


def _hdot(a, b):
    return _dot3(a, b, "nn")


def _mask_dot(mask, x, form):
    dg = lambda q: lax.dot_general(mask.astype(BF16), q, _DOT_DIMS[form], preferred_element_type=F32)
    x1 = x.astype(BF16)
    r1 = x - x1.astype(F32)
    x2 = r1.astype(BF16)
    x3 = (r1 - x2.astype(F32)).astype(BF16)
    return dg(x1) + (dg(x2) + dg(x3))


def _decay_masks(c):
    row = lax.broadcasted_iota(jnp.int32, (c, c), 0)
    col = lax.broadcasted_iota(jnp.int32, (c, c), 1)
    return (row >= col).astype(F32), row <= col


def _decay_impl(gb):
    lower, upper = _decay_masks(gb.shape[0])
    return _mask_dot(lower, gb, "nn"), _mask_dot(jnp.ones_like(gb), jnp.where(upper, gb, 0.0), "nn")


@jax.custom_vjp
def _decay_matrices(gb):
    return _decay_impl(gb)


def _decay_fwd(gb):
    return _decay_impl(gb), None


def _decay_bwd(_, cts):
    gc, gr = cts
    lower, upper = _decay_masks(gc.shape[0])
    return (_mask_dot(lower, gc, "tn") + jnp.where(upper, _mask_dot(jnp.ones_like(gr), gr, "tn"), 0.0),)


_decay_matrices.defvjp(_decay_fwd, _decay_bwd)


def _softplus(x):
    return jnp.maximum(x, 0.0) + jnp.log(1.0 + jnp.exp(-jnp.abs(x)))


def _heads(f, *lists):
    return [f(*t) for t in zip(*lists)]


DN_STEP_CHUNKS = 3


def _dn_chunk(qr, kr, v, z, braw, araw, alog, dtb, nw, state, valid):
    c = DN_CHUNK
    nh = len(state)
    chunks = len(qr) // nh
    alog, dtb, valid_i = alog * chunks, dtb * chunks, [vv for vv in valid for _ in range(nh)]
    row = lax.broadcasted_iota(jnp.int32, (c, c), 0)
    col = lax.broadcasted_iota(jnp.int32, (c, c), 1)
    incl = row >= col
    strict = row > col
    eye = jnp.where(row == col, 1.0, 0.0)
    q = _heads(lambda t: t * lax.rsqrt(jnp.sum(t * t, -1, keepdims=True) + EPS) * (DN_HEAD_DIM ** -0.5), qr)
    k = _heads(lambda t: t * lax.rsqrt(jnp.sum(t * t, -1, keepdims=True) + EPS), kr)
    beta = _heads(lambda t, vv: _sigmoid(t) * vv, braw, valid_i)
    g = _heads(lambda al, ar, dt, vv: -jnp.exp(al) * _softplus(ar + dt) * vv, alog, araw, dtb, valid_i)
    decay = _heads(lambda t: _decay_matrices(jnp.broadcast_to(t, (c, c))), g)
    dmask = _heads(lambda d: jnp.where(incl, jnp.exp(jnp.where(incl, d[0] - d[1], 0.0)), 0.0), decay)
    dec = _heads(lambda d: d[0][:, :1], decay)
    dlast = _heads(lambda d: d[0][c - 1:c, :1], decay)
    kk = _heads(lambda t: _mxu_dot(t, t, "nt"), k)
    a = _heads(lambda b, t, d: jnp.where(strict, b * t * d, 0.0), beta, kk, dmask)
    x = _heads(lambda t: eye - t, a)
    p = _heads(_hdot, a, a)
    for it in range(5):
        x = _heads(lambda s, t: s + _hdot(s, t), x, p)
        if it < 4:
            p = _heads(_hdot, p, p)
    u = _heads(lambda s, t, b: _hdot(s, t * b), x, v, beta)
    w = _heads(lambda s, t, b, d: _hdot(s, t * (b * jnp.exp(d))), x, k, beta, dec)
    qk = _heads(lambda s, t, d: _mxu_dot(s, t, "nt") * d, q, k, dmask)
    q_dec = _heads(lambda t, d: t * jnp.exp(d), q, dec)
    k_dec = _heads(lambda t, dl, d: t * jnp.exp(dl - d), k, dlast, dec)
    o = []
    for ci in range(chunks):
        of = lambda lst: lst[ci * nh:(ci + 1) * nh]
        v_new = _heads(lambda s, t, st: s - _mxu_dot(t, st, "nn"), of(u), of(w), state)
        o += _heads(lambda qd, st, s, vn: _mxu_dot(qd, st, "nn") + _mxu_dot(s, vn, "nn"), of(q_dec), state, of(qk), v_new)
        state = _heads(lambda st, dl, kd, vn: st * jnp.exp(dl) + _mxu_dot(kd, vn, "tn"), state, of(dlast), of(k_dec), v_new)
    y = _heads(lambda t, zz: _rms(t, nw) * (zz * _sigmoid(zz)), o, z)
    return y, state


DN_STEP_ROWS = DN_STEP_CHUNKS * DN_CHUNK
DN_ITEMS = [(ci, h) for ci in range(DN_STEP_CHUNKS) for h in range(DN_HEADS)]


def _dn_valid(n):
    rows = [n * DN_STEP_ROWS + ci * DN_CHUNK + lax.broadcasted_iota(jnp.int32, (DN_CHUNK, 1), 0) for ci in range(DN_STEP_CHUNKS)]
    return [(r >= PAD_ROWS).astype(F32) for r in rows]


def _dn_in_specs(rev, nc):
    cn = (lambda n: nc - 1 - n) if rev else (lambda n: n)
    c, hd = DN_STEP_ROWS, DN_HEAD_DIM
    return [
        pl.BlockSpec((None, c, DN_DIM), lambda n: (0, cn(n), 0)),
        pl.BlockSpec((None, c, DN_DIM), lambda n: (1, cn(n), 0)),
        pl.BlockSpec((None, c, DN_DIM), lambda n: (2, cn(n), 0)),
        pl.BlockSpec((None, c, DN_DIM), lambda n: (6, cn(n), 0)),
        pl.BlockSpec((DN_HEADS, 2, c, 1), lambda n: (0, 0, cn(n), 0)),
        pl.BlockSpec((DN_HEADS, SUBLANES, LANES), lambda n: (0, 0, 0)),
        pl.BlockSpec((1, hd), lambda n: (0, 0)),
    ]


def _rows(ci):
    return slice(ci * DN_CHUNK, (ci + 1) * DN_CHUNK)


def _cols(h):
    return slice(h * DN_HEAD_DIM, (h + 1) * DN_HEAD_DIM)


def _dn_load(q_ref, k_ref, v_ref, z_ref, ba_ref, hp_ref):
    heads = range(DN_HEADS)
    item = lambda ref: [ref[_rows(ci), _cols(h)] for ci, h in DN_ITEMS]
    return (item(q_ref), item(k_ref), item(v_ref), item(z_ref),
            [ba_ref[h, 0, _rows(ci), :] for ci, h in DN_ITEMS], [ba_ref[h, 1, _rows(ci), :] for ci, h in DN_ITEMS],
            [hp_ref[h, 0:1, 0:1] for h in heads], [hp_ref[h, 1:2, 0:1] for h in heads])


def _delta_fwd(qkvc, pm, ba, hp, nw, *, name):
    _, m, _ = qkvc.shape
    nc = m // DN_STEP_ROWS
    hd = DN_HEAD_DIM

    def body(q_ref, k_ref, v_ref, z_ref, ba_ref, hp_ref, nw_ref, y_ref, s_ref, state):
        n = pl.program_id(0)

        @pl.when(n == 0)
        def _():
            state[...] = jnp.zeros_like(state)

        heads = range(DN_HEADS)
        old = [state[h] for h in heads]
        y, new = _dn_chunk(*_dn_load(q_ref, k_ref, v_ref, z_ref, ba_ref, hp_ref), nw_ref[...], old, _dn_valid(n))
        for h in heads:
            s_ref[h] = old[h]
            state[h] = new[h]
        for (ci, h), yy in zip(DN_ITEMS, y):
            y_ref[_rows(ci), _cols(h)] = yy.astype(y_ref.dtype)

    return pl.pallas_call(
        body, name=name, grid=(nc,), in_specs=_dn_in_specs(False, nc),
        out_specs=[pl.BlockSpec((DN_STEP_ROWS, DN_DIM), lambda n: (n, 1)), pl.BlockSpec((DN_HEADS, None, hd, hd), lambda n: (0, n, 0, 0))],
        out_shape=[jax.ShapeDtypeStruct((m, D_CONV + DN_DIM), MXU_DTYPE), jax.ShapeDtypeStruct((DN_HEADS, nc, hd, hd), F32)],
        scratch_shapes=[pltpu.VMEM((DN_HEADS, hd, hd), F32)],
        compiler_params=_params(("arbitrary",)),
    )(qkvc, qkvc, qkvc, pm, ba, hp, nw)


def _delta_bwd(dy, qkvc, pm, ba, hp, nw, states, *, name):
    _, m, _ = qkvc.shape
    nc = m // DN_STEP_ROWS
    hd, c = DN_HEAD_DIM, DN_STEP_ROWS

    def body(q_ref, k_ref, v_ref, z_ref, ba_ref, hp_ref, nw_ref, s_ref, dy_ref,
             dz_ref, dqkv_ref, dba_ref, dhp_ref, dnw_ref, dstate):
        step = pl.program_id(0)
        n = nc - 1 - step

        @pl.when(step == 0)
        def _():
            dstate[...] = jnp.zeros_like(dstate)
            dhp_ref[...] = jnp.zeros_like(dhp_ref)
            dnw_ref[...] = jnp.zeros_like(dnw_ref)

        valid = _dn_valid(n)
        heads = range(DN_HEADS)
        fn = lambda *a: _dn_chunk(*a, valid)
        _, vjp = jax.vjp(fn, *_dn_load(q_ref, k_ref, v_ref, z_ref, ba_ref, hp_ref), nw_ref[...], [s_ref[h] for h in heads])
        dy = [dy_ref[_rows(ci), _cols(h)] for ci, h in DN_ITEMS]
        dq, dk, dv, dz, dbr, dar, dalog, ddtb, dnw, dst = vjp((dy, [dstate[h] for h in heads]))
        for i, (ci, h) in enumerate(DN_ITEMS):
            dqkv_ref[0, _rows(ci), _cols(h)] = dq[i]
            dqkv_ref[1, _rows(ci), _cols(h)] = dk[i]
            dqkv_ref[2, _rows(ci), _cols(h)] = dv[i]
            dz_ref[_rows(ci), _cols(h)] = dz[i].astype(dz_ref.dtype)
            dba_ref[h, 0, _rows(ci), :] = dbr[i]
            dba_ref[h, 1, _rows(ci), :] = dar[i]
        for h in heads:
            dstate[h] = dst[h]
            dhp_ref[h] += jnp.concatenate([jnp.broadcast_to(dalog[h], (1, LANES)), jnp.broadcast_to(ddtb[h], (1, LANES)),
                                           jnp.zeros((SUBLANES - 2, LANES), F32)], 0)
        dnw_ref[...] += dnw

    rn = lambda n: nc - 1 - n
    in_specs = _dn_in_specs(True, nc) + [
        pl.BlockSpec((DN_HEADS, None, hd, hd), lambda n: (0, rn(n), 0, 0)),
        pl.BlockSpec((c, DN_DIM), lambda n: (rn(n), 1)),
    ]
    out_specs = [
        pl.BlockSpec((None, c, DN_DIM), lambda n: (6, rn(n), 0)),
        pl.BlockSpec((3, c, DN_DIM), lambda n: (0, rn(n), 0)),
        pl.BlockSpec((DN_HEADS, 2, c, 1), lambda n: (0, 0, rn(n), 0)),
        pl.BlockSpec((DN_HEADS, SUBLANES, LANES), lambda n: (0, 0, 0)),
        pl.BlockSpec((1, hd), lambda n: (0, 0)),
    ]
    return pl.pallas_call(
        body, name=name, grid=(nc,), in_specs=in_specs, out_specs=out_specs,
        out_shape=[jax.ShapeDtypeStruct(pm.shape, MXU_DTYPE), jax.ShapeDtypeStruct(qkvc.shape, F32),
                   jax.ShapeDtypeStruct(ba.shape, F32), jax.ShapeDtypeStruct(hp.shape, F32),
                   jax.ShapeDtypeStruct((1, hd), F32)],
        scratch_shapes=[pltpu.VMEM((DN_HEADS, hd, hd), F32)],
        compiler_params=_params(("arbitrary",)),
    )(qkvc, qkvc, qkvc, pm, ba, hp, nw, states, dy)


SWA_PAIR = 2


def _attn_block(q, k0, kp, kc, v0, vp, vc, qw, kw, sink, n):
    g, b, hd = SWA_GROUP, SWA_BLOCK, SWA_HEAD_DIM
    pair = list(range(SWA_PAIR))
    lanes = lambda t, e: t[:, e * hd:(e + 1) * hd]
    q4 = [jnp.concatenate([lanes(q, e * g + i)[None] for i in range(g)], 0) for e in pair]
    qn = _heads(lambda t: _rms(t, qw) * (hd ** -0.5), q4)
    kn = [_rms(jnp.concatenate([lanes(k0, e), lanes(kp, e), lanes(kc, e)], 0), kw) for e in pair]
    vcat = [jnp.concatenate([lanes(v0, e), lanes(vp, e), lanes(vc, e)], 0) for e in pair]
    s = _heads(lambda a, k: _mxu_dot(a.reshape(g * b, hd), k, "nt").reshape(g, b, 3 * b), qn, kn)
    i = lax.broadcasted_iota(jnp.int32, (b, 3 * b), 0)
    c = lax.broadcasted_iota(jnp.int32, (b, 3 * b), 1)
    in_meta, in_prev, in_cur = c < b, (c >= b) & (c < 2 * b), c >= 2 * b
    j = c - jnp.where(in_meta, 0, jnp.where(in_prev, b, 2 * b))
    meta_lo = jnp.where(n == 0, b, PAD_ROWS)
    cur_lo = jnp.where(n == 0, PAD_ROWS, 0)
    prev_off = jnp.where(n >= 2, 0, 2 * b)
    valid = (in_meta & (j >= meta_lo)) | (in_prev & (j > i + prev_off)) | (in_cur & (j <= i) & (j >= cur_lo))
    s = _heads(lambda t: jnp.where(valid[None], t, NEG), s)
    m = [lax.stop_gradient(jnp.maximum(jnp.max(t, -1, keepdims=True), sink[e])) for e, t in zip(pair, s)]
    ex = _heads(lambda t, mm: jnp.exp(t - mm), s, m)
    p = [t / (jnp.sum(t, -1, keepdims=True) + jnp.exp(sink[e] - mm)) for e, t, mm in zip(pair, ex, m)]
    o = _heads(lambda t, v: _mxu_dot(t.reshape(g * b, 3 * b), v, "nn").reshape(g, b, hd), p, vcat)
    return jnp.concatenate([o[e][i] for e in pair for i in range(g)], 1)


Q_LANES = SWA_PAIR * SWA_GROUP * SWA_HEAD_DIM
KV_LANES = SWA_PAIR * SWA_HEAD_DIM
K_BLOCK0 = SWA_HEADS * SWA_HEAD_DIM // KV_LANES
V_BLOCK0 = K_BLOCK0 + SWA_KV_HEADS * SWA_HEAD_DIM // KV_LANES


def _attn_in_specs():
    g, b, hd = SWA_GROUP, SWA_BLOCK, SWA_HEAD_DIM
    kv = lambda f, first: pl.BlockSpec((b, KV_LANES), lambda p, n: (f(n), first + p))
    blocks = [lambda n: 0, lambda n: jnp.maximum(n - 1, 0), lambda n: n]
    return ([pl.BlockSpec((b, Q_LANES), lambda p, n: (n, p))] + [kv(f, K_BLOCK0) for f in blocks] + [kv(f, V_BLOCK0) for f in blocks]
            + [pl.BlockSpec((1, hd), lambda p, n: (0, 0)), pl.BlockSpec((1, hd), lambda p, n: (0, 0)),
               pl.BlockSpec((SWA_PAIR, g, 1, 1), lambda p, n: (p, 0, 0, 0))])


def _attn_fwd(qkv, qw, kw, sink, *, name):
    m = qkv.shape[0]
    b = SWA_BLOCK

    def body(q_ref, k0, kp, kc, v0, vp, vc, qw_ref, kw_ref, s_ref, o_ref):
        o_ref[...] = _attn_block(q_ref[...], k0[...], kp[...], kc[...], v0[...], vp[...], vc[...], qw_ref[...], kw_ref[...],
                                 s_ref[...], pl.program_id(1)).astype(o_ref.dtype)

    return pl.pallas_call(
        body, name=name, grid=(SWA_KV_HEADS // SWA_PAIR, m // b), in_specs=_attn_in_specs(),
        out_specs=pl.BlockSpec((b, Q_LANES), lambda p, n: (n, p)),
        out_shape=jax.ShapeDtypeStruct((m, SWA_HEADS * SWA_HEAD_DIM), MXU_DTYPE),
        compiler_params=_params(("parallel", "parallel")),
    )(*([qkv] * 7), qw, kw, sink)


def _attn_bwd(do, qkv, qw, kw, sink, *, name):
    m = qkv.shape[0]
    g, b, hd = SWA_GROUP, SWA_BLOCK, SWA_HEAD_DIM

    def body(q_ref, k0, kp, kc, v0, vp, vc, qw_ref, kw_ref, s_ref, do_ref, dq_ref, dk_ref, dv_ref, dqw_ref, dkw_ref, ds_ref):
        n = pl.program_id(1)

        @pl.when(n == 0)
        def _():
            for r in (dk_ref, dv_ref, dqw_ref, dkw_ref, ds_ref):
                r[...] = jnp.zeros_like(r)

        fn = lambda *a: _attn_block(*a, n)
        _, vjp = jax.vjp(fn, q_ref[...], k0[...], kp[...], kc[...], v0[...], vp[...], vc[...], qw_ref[...], kw_ref[...], s_ref[...])
        dq, dk0, dkp, dkc, dv0, dvp, dvc, dqw, dkw, dsk = vjp(do_ref[...])
        dq_ref[...] = dq
        prev = pl.multiple_of(jnp.maximum(n - 1, 0) * b, b)
        cur = pl.multiple_of(n * b, b)
        for ref, parts in ((dk_ref, (dk0, dkp, dkc)), (dv_ref, (dv0, dvp, dvc))):
            ref[pl.ds(0, b), :] += parts[0]
            ref[pl.ds(prev, b), :] += parts[1]
            ref[pl.ds(cur, b), :] += parts[2]
        dqw_ref[...] += dqw
        dkw_ref[...] += dkw
        ds_ref[...] += dsk

    pairs = SWA_KV_HEADS // SWA_PAIR
    kv_acc = pl.BlockSpec((m, KV_LANES), lambda p, n: (0, p))
    w_acc = pl.BlockSpec((None, 1, hd), lambda p, n: (p, 0, 0))
    kv_shape = jax.ShapeDtypeStruct((m, SWA_KV_HEADS * hd), F32)
    return pl.pallas_call(
        body, name=name, grid=(pairs, m // b),
        in_specs=_attn_in_specs() + [pl.BlockSpec((b, Q_LANES), lambda p, n: (n, p))],
        out_specs=[pl.BlockSpec((b, Q_LANES), lambda p, n: (n, p)), kv_acc, kv_acc, w_acc, w_acc,
                   pl.BlockSpec((SWA_PAIR, g, 1, 1), lambda p, n: (p, 0, 0, 0))],
        out_shape=[jax.ShapeDtypeStruct((m, SWA_HEADS * hd), F32), kv_shape, kv_shape,
                   jax.ShapeDtypeStruct((pairs, 1, hd), F32), jax.ShapeDtypeStruct((pairs, 1, hd), F32),
                   jax.ShapeDtypeStruct(sink.shape, F32)],
        compiler_params=_params(("parallel", "arbitrary")),
    )(*([qkv] * 7), qw, kw, sink, do)


def _loss_bwd(h, target, *, name):
    m, d = h.shape
    b = SWA_BLOCK

    def body(h_ref, t_ref, l_ref, dh_ref):
        i = pl.program_id(0)

        @pl.when(i == 0)
        def _():
            l_ref[...] = jnp.zeros_like(l_ref)
            dh_ref[...] = jnp.zeros_like(dh_ref)

        @pl.when(i > 0)
        def _():
            e = h_ref[...] - t_ref[...]
            dh_ref[...] = e * (1.0 / d)
            l_ref[...] += jnp.sum(jnp.sum(e * e, 0, keepdims=True), 1, keepdims=True) * (0.5 / d)

    return pl.pallas_call(
        body, name=name, grid=(m // b,),
        in_specs=[pl.BlockSpec((b, d), lambda i: (i, 0)), pl.BlockSpec((b, d), lambda i: (jnp.maximum(i - 1, 0), 0))],
        out_specs=[pl.BlockSpec((1, LANES), lambda i: (0, 0)), pl.BlockSpec((b, d), lambda i: (i, 0))],
        out_shape=[jax.ShapeDtypeStruct((1, LANES), F32), jax.ShapeDtypeStruct((m, d), F32)],
        compiler_params=_params(("arbitrary",)),
    )(h, target)


def _ffn_fwd(h, nw, w_up, cw, w_down, tag):
    u, hn = _norm_matmul(h, nw, w_up, o_seg=D_FF, name=f"ffn_up_{tag}")
    a = _ffn_act_fwd(u, cw, name=f"ffn_act_{tag}")
    return _mm_nn(a, w_down, res=h, name=f"ffn_down_{tag}"), (h, hn, u, a)


def _ffn_bwd(dh, saved, nw, w_up, cw, w_down, tag):
    h, hn, u, a = saved
    da = _mm_nn(dh, w_down, trans_w=True, name=f"ffn_da_{tag}")
    dw_down = _mm_tn(a, dh, name=f"ffn_dwdown_{tag}")
    du, dcw = _ffn_act_bwd(da, u, cw, name=f"ffn_act_bwd_{tag}")
    dhn = _mm_nn(du, w_up, a_seg=True, trans_w=True, name=f"ffn_dhn_{tag}")
    dw_up = _mm_tn(hn, du, b_seg=True, name=f"ffn_dwup_{tag}")
    dh_in, dnw = _rmsnorm_bwd(dhn, h, nw, dh, name=f"ffn_norm_bwd_{tag}")
    return dh_in, dnw, dw_up, dcw, dw_down


def _local_step(x, target, w, fetch=None, push=None):
    fetch = fetch or (lambda stage, after: {})
    push = push or (lambda stage, grads: None)
    plus = lambda a, zero: a if zero is None else a + zero
    seq, d = x.shape
    m = PAD_ROWS + N_META + seq
    h0 = jnp.concatenate([jnp.zeros((PAD_ROWS, d), F32), w["meta"], x], 0)

    pm, hn0 = _norm_matmul(h0, w["anw"][0], w["w_in_main"], o_seg=SEG, name="mix_in")
    pba = _mm_nn(hn0, w["w_in_tail"], name="mix_in_tail")
    qkvc = _dnpre_fwd(pm, w["dcw"], name="dn_conv")
    ba = pba[:, :2 * DN_HEADS].T.reshape(2, DN_HEADS, m, 1).transpose(1, 0, 2, 3)
    y, states = _delta_fwd(qkvc, pm, ba, w["hp"], w["dnw"], name="delta")
    y = _shortconv_fwd(pm, w["caw"], y, name="shortconv")
    w = {**w, **fetch("l0", y)}
    h1 = _mm_nn(y, w["w_out"], res=h0, name="mix_out")
    h2, ffn0 = _ffn_fwd(h1, w["fnw"][0], w["w_up0"], w["fcw"][0], w["w_down0"], "l0")

    w = {**w, **fetch("l1", h2)}
    qkv, hn2 = _norm_matmul(h2, w["anw"][1], w["wqkv"], name="attn_qkv")
    o = _attn_fwd(qkv, w["qnw"], w["knw"], w["sink"], name="attn")
    h3 = _mm_nn(o, w["wo"], res=h2, name="attn_out")
    h4, ffn1 = _ffn_fwd(h3, w["fnw"][1], w["w_up1"], w["fcw"][1], w["w_down1"], "l1")

    loss, dh4 = _loss_bwd(h4, target, name="loss")

    g = {}
    dh3, dfnw1, dwup1, dfcw1, dwdown1 = _ffn_bwd(dh4, ffn1, w["fnw"][1], w["w_up1"], w["fcw"][1], w["w_down1"], "l1")

    do = _mm_nn(dh3, w["wo"], trans_w=True, name="attn_do")
    g["wo"] = _mm_tn(o, dh3, name="attn_dwo")
    dq, dk, dv, dqw, dkw, dsink = _attn_bwd(do, qkv, w["qnw"], w["knw"], w["sink"], name="attn_bwd")
    dqkv = jnp.concatenate([dq, dk, dv], 1).astype(MXU_DTYPE)
    dhn2 = _mm_nn(dqkv, w["wqkv"], trans_w=True, name="attn_dhn")
    g["wqkv"] = _mm_tn(hn2, dqkv, name="attn_dwqkv")
    zero = push("l1", dict(w_up=dwup1, w_down=dwdown1, wo=g["wo"], wqkv=g["wqkv"]))
    dh2, danw1 = _rmsnorm_bwd(dhn2, h2, plus(w["anw"][1], zero), dh3, name="attn_norm_bwd")

    dh1, dfnw0, dwup0, dfcw0, dwdown0 = _ffn_bwd(dh2, ffn0, w["fnw"][0], w["w_up0"], w["fcw"][0], w["w_down0"], "l0")

    dy = _mm_nn(dh1, w["w_out"], trans_w=True, name="mix_dy")
    g["w_out"] = _mm_tn(y, dh1, name="mix_dwout")
    zero = push("l0", dict(w_up=dwup0, w_down=dwdown0, w_out=g["w_out"]))
    dpm, dqkvc, dba, dhp, ddnw = _delta_bwd(dy, qkvc, pm, ba, w["hp"], plus(w["dnw"], zero), states, name="delta_bwd")
    dpm, ddcw = _dnpre_bwd(dqkvc, pm, w["dcw"], dpm, name="dn_conv_bwd")
    dpm, dcaw = _shortconv_bwd(dy, pm, w["caw"], dpm, name="shortconv_bwd")
    dpba = jnp.pad(dba.transpose(1, 0, 2, 3).reshape(2 * DN_HEADS, m).T, ((0, 0), (0, LANES - 2 * DN_HEADS))).astype(MXU_DTYPE)
    dhn0 = _mm_nn(dpm, w["w_in_main"], a_seg=True, trans_w=True, name="mix_dhn")
    dhn0 = _mm_nn(dpba, w["w_in_tail"], res=dhn0, trans_w=True, name="mix_dhn_tail")
    g["w_in_main"] = _mm_tn(hn0, dpm, b_seg=True, name="mix_dwin")
    g["w_in_tail"] = _mm_tn(hn0, dpba, name="mix_dwin_tail")
    dh0, danw0 = _rmsnorm_bwd(dhn0, h0, w["anw"][0], dh1, name="mix_norm_bwd")

    g.update(
        x=dh0[PAD_ROWS + N_META:], meta=dh0[PAD_ROWS:PAD_ROWS + N_META], anw=[danw0, danw1], fnw=[dfnw0, dfnw1],
        caw=dcaw, dcw=ddcw, hp=dhp, dnw=ddnw, qnw=jnp.sum(dqw, 0), knw=jnp.sum(dkw, 0), sink=dsink,
        w_up=[dwup0, dwup1], fcw=[dfcw0, dfcw1], w_down=[dwdown0, dwdown1])
    return loss, g


N_TAIL = 2 * DN_HEADS


def _prepare_early(p):
    n_main = N_SEG * SEG
    w_in = p["mix_w_in"][0]
    tail = jnp.pad(w_in[:, n_main:], ((0, 0), (0, LANES - N_TAIL)))
    hp = jnp.zeros((DN_HEADS, SUBLANES, LANES), F32)
    hp = hp.at[:, 0, :].set(p["dn_a_log"][0][:, None]).at[:, 1, :].set(p["dn_dt_bias"][0][:, None])
    depth = p["ffn_conv_w"].shape[0]
    return dict(
        meta=p["meta_tokens"], anw=[p["attn_norm_w"][i:i + 1] for i in range(depth)],
        fnw=[p["ffn_norm_w"][i:i + 1] for i in range(depth)],
        w_in_main=w_in[:, :n_main], w_in_tail=tail,
        caw=_pad_w(p["conv_a_w"][0]), dcw=_pad_w(p["dn_conv_w"][0]), hp=hp, dnw=p["dn_norm_w"],
        qnw=p["swa_q_norm_w"], knw=p["swa_k_norm_w"], sink=p["swa_sinks"].reshape(SWA_KV_HEADS, SWA_GROUP, 1, 1),
        fcw=[_pad_w(p["ffn_conv_w"][i]) for i in range(depth)])


def _prepare_weights(p):
    return dict(
        _prepare_early(p), w_out=p["mix_w_out"][0], wo=p["swa_wo"][0],
        wqkv=jnp.concatenate([p["swa_wq"][0], p["swa_wk"][0], p["swa_wv"][0]], 1),
        w_up0=p["ffn_w_up"][0], w_up1=p["ffn_w_up"][1], w_down0=p["ffn_w_down"][0], w_down1=p["ffn_w_down"][1])


def _small_named(g):
    return dict(
        meta_tokens=g["meta"], attn_norm_w=jnp.concatenate(g["anw"], 0), ffn_norm_w=jnp.concatenate(g["fnw"], 0),
        conv_a_w=g["caw"][None, :3], dn_conv_w=g["dcw"][None, :4],
        dn_a_log=g["hp"][None, :, 0, 0], dn_dt_bias=g["hp"][None, :, 1, 0], dn_norm_w=g["dnw"],
        swa_q_norm_w=g["qnw"], swa_k_norm_w=g["knw"], swa_sinks=g["sink"].reshape(1, SWA_HEADS),
        ffn_conv_w=jnp.stack([c[:3] for c in g["fcw"]]))


def _reference_named(g):
    nq, nkv = SWA_HEADS * SWA_HEAD_DIM, SWA_KV_HEADS * SWA_HEAD_DIM
    return dict(
        _small_named(g), mix_w_in=jnp.concatenate([g["w_in_main"], g["w_in_tail"][:, :N_TAIL]], 1)[None],
        mix_w_out=g["w_out"][None], swa_wq=g["wqkv"][None, :, :nq], swa_wk=g["wqkv"][None, :, nq:nq + nkv],
        swa_wv=g["wqkv"][None, :, nq + nkv:], swa_wo=g["wo"][None],
        ffn_w_up=jnp.stack(g["w_up"]), ffn_w_down=jnp.stack(g["w_down"]))


def _my_index():
    return 4 * lax.axis_index("x") + 2 * lax.axis_index("y") + lax.axis_index("c")


def _all_to_all(arrays, *, name):
    n = len(arrays)

    def body(*refs):
        ins, outs = refs[:n], refs[n:2 * n]
        send_sems, recv_sems, local_sems = refs[2 * n:]
        x, y, c = lax.axis_index("x"), lax.axis_index("y"), lax.axis_index("c")
        me = 4 * x + 2 * y + c
        copies = []
        for i in range(n):
            local = pltpu.make_async_copy(ins[i].at[me], outs[i].at[me], local_sems.at[i])
            local.start()
            copies.append(local)
        for d in range(1, N_DEV):
            px, py, pc = x ^ (d >> 2), y ^ ((d >> 1) & 1), c ^ (d & 1)
            for i in range(n):
                remote = pltpu.make_async_remote_copy(
                    src_ref=ins[i].at[4 * px + 2 * py + pc], dst_ref=outs[i].at[me], send_sem=send_sems.at[i, d],
                    recv_sem=recv_sems.at[i, d], device_id=(px, py, pc), device_id_type=pl.DeviceIdType.MESH)
                remote.start()
                copies.append(remote)
        for cp in copies:
            cp.wait()

    hbm = pl.BlockSpec(memory_space=pl.ANY)
    return pl.pallas_call(
        body, name=name, in_specs=[hbm] * n, out_specs=[hbm] * n,
        out_shape=[jax.ShapeDtypeStruct(a.shape, a.dtype) for a in arrays],
        scratch_shapes=[pltpu.SemaphoreType.DMA((n, N_DEV)), pltpu.SemaphoreType.DMA((n, N_DEV)), pltpu.SemaphoreType.DMA((n,))],
    )(*arrays)


def _all_gather(arrays, *, name):
    n = len(arrays)

    def body(*refs):
        ins, outs = refs[:n], refs[n:2 * n]
        send_sems, recv_sems, local_sems = refs[2 * n:]
        x, y, c = lax.axis_index("x"), lax.axis_index("y"), lax.axis_index("c")
        me, sibling = (x, y, c), (x, y, 1 - c)
        chips = [(1 - x, y), (x, 1 - y), (1 - x, 1 - y)]

        def copy(i, k, block, to, src=None):
            rows = outs[i].at[4 * block[0] + 2 * block[1] + block[2]]
            return pltpu.make_async_remote_copy(
                src_ref=rows if src is None else src, dst_ref=rows, send_sem=send_sems.at[i, k], recv_sem=recv_sems.at[i, k],
                device_id=to, device_id_type=pl.DeviceIdType.MESH)

        mine = [pltpu.make_async_copy(ins[i], outs[i].at[4 * x + 2 * y + c], local_sems.at[i]) for i in range(n)]
        first = []
        for j, chip in enumerate(chips):
            first += [copy(i, 1 + j, me, (*chip, c), src=ins[i]) for i in range(n)]
        first += [copy(i, 0, me, sibling, src=ins[i]) for i in range(n)]
        for cp in first + mine:
            cp.start()
        passed = []
        for j, chip in enumerate(chips):
            for i in range(n):
                copy(i, 1 + j, (*chip, c), me).wait_recv()
                fwd = copy(i, 4 + j, (*chip, c), sibling)
                fwd.start()
                passed.append(fwd)
        for i in range(n):
            copy(i, 0, sibling, me).wait_recv()
            for j, chip in enumerate(chips):
                copy(i, 4 + j, (*chip, 1 - c), me).wait_recv()
        for cp in first + passed:
            cp.wait_send()
        for cp in mine:
            cp.wait()

    hbm = pl.BlockSpec(memory_space=pl.ANY)
    return pl.pallas_call(
        body, name=name, in_specs=[hbm] * n, out_specs=[hbm] * n,
        out_shape=[jax.ShapeDtypeStruct((N_DEV,) + tuple(a.shape), a.dtype) for a in arrays],
        scratch_shapes=[pltpu.SemaphoreType.DMA((n, 7)), pltpu.SemaphoreType.DMA((n, 7)), pltpu.SemaphoreType.DMA((n,))],
    )(*arrays)


def _peer(d):
    px, py, pc = lax.axis_index("x") ^ (d >> 2), lax.axis_index("y") ^ ((d >> 1) & 1), lax.axis_index("c") ^ (d & 1)
    return (px, py, pc), 4 * px + 2 * py + pc


def _push_copies(mode, srcs, lands, send_sems, recv_sems):
    me = _my_index()
    out = []
    for d in range(1, N_DEV):
        pos, idx = _peer(d)
        for i in range(len(srcs)):
            out.append(pltpu.make_async_remote_copy(
                src_ref=srcs[i] if mode == "gather" else srcs[i].at[idx], dst_ref=lands[i].at[me],
                send_sem=send_sems.at[i * N_DEV + d], recv_sem=recv_sems.at[i * N_DEV + d], device_id=pos,
                device_id_type=pl.DeviceIdType.MESH))
    return out


_HBM = pl.BlockSpec(memory_space=pltpu.HBM)
_SEM = pl.BlockSpec(memory_space=pltpu.SEMAPHORE)


def _push_start(mode, arrays, follows, *, name):
    n = len(arrays)
    blocks = [a.shape if mode == "gather" else a.shape[1:] for a in arrays]
    lands = [lax.empty((N_DEV,) + tuple(b), a.dtype) for a, b in zip(arrays, blocks)]

    def body(*refs):
        srcs, land_refs = refs[:n], refs[n:2 * n]
        send_sems, recv_sems = refs[2 * n + 1], refs[2 * n + 2]
        zero = refs[-1]
        for cp in _push_copies(mode, srcs, land_refs, send_sems, recv_sems):
            cp.start()
        zero[...] = jnp.zeros_like(zero)

    hbm_in = [pltpu.with_memory_space_constraint(a, pltpu.HBM) for a in list(arrays) + lands]
    outs = pl.pallas_call(
        body, name=name,
        out_shape=[pltpu.SemaphoreType.DMA((n * N_DEV,)), pltpu.SemaphoreType.DMA((n * N_DEV,))]
        + [pltpu.HBM(a.shape, a.dtype) for a in hbm_in] + [jax.ShapeDtypeStruct((SUBLANES, LANES), F32)],
        in_specs=[_HBM] * (2 * n) + [pl.BlockSpec(memory_space=pl.ANY)],
        out_specs=[_SEM, _SEM] + [_HBM] * (2 * n) + [pl.BlockSpec(memory_space=pltpu.VMEM)],
        input_output_aliases={i: 2 + i for i in range(2 * n)},
        compiler_params=pltpu.CompilerParams(has_side_effects=pltpu.SideEffectType.DATAFLOW_SIDE_EFFECTING),
    )(*hbm_in, follows)
    return dict(mode=mode, sems=outs[:2], srcs=outs[2:2 + n], lands=outs[2 + n:2 + 2 * n], zero=outs[-1])


def _push_wait(push, follows, *, name):
    n = len(push["srcs"])
    mode = push["mode"]

    def body(*refs):
        srcs, land_refs = refs[:n], refs[n:2 * n]
        send_sems, recv_sems = refs[2 * n], refs[2 * n + 1]
        for cp in _push_copies(mode, srcs, land_refs, send_sems, recv_sems):
            cp.wait_send()
            cp.wait_recv()

    args = list(push["srcs"]) + list(push["lands"])
    outs = pl.pallas_call(
        body, name=name, out_shape=[pltpu.HBM(a.shape, a.dtype) for a in args],
        in_specs=[_HBM] * (2 * n) + [_SEM, _SEM, pl.BlockSpec(memory_space=pl.ANY)], out_specs=[_HBM] * (2 * n),
        input_output_aliases={i: i for i in range(2 * n)},
        compiler_params=pltpu.CompilerParams(has_side_effects=pltpu.SideEffectType.DATAFLOW_SIDE_EFFECTING),
    )(*args, *push["sems"], follows)
    me = _my_index()
    got = []
    for src, land in zip(outs[:n], outs[n:]):
        own = src if mode == "gather" else lax.dynamic_index_in_dim(src, me, 0, keepdims=False)
        got.append(lax.dynamic_update_index_in_dim(land, own, me, 0))
    return got


def _row_tile(r, c):
    best = None
    for t in range(2 * SUBLANES, r + 1, 2 * SUBLANES):
        if r % t == 0 and N_DEV * t * c * 4 <= 6 * 1024 * 1024:
            best = t
    return best or r


def _adamw(parts, w, m, v, *, name):
    r, c = w.shape
    tr = _row_tile(r, c)

    def body(p_ref, w_ref, m_ref, v_ref, g_ref, d_ref, nm_ref, nv_ref):
        g = p_ref[0].astype(F32)
        for j in range(1, N_DEV):
            g = g + p_ref[j].astype(F32)
        m2 = ADAM_B1 * m_ref[...] + (1.0 - ADAM_B1) * g
        v2 = ADAM_B2 * v_ref[...] + (1.0 - ADAM_B2) * jnp.square(g)
        m_hat = m2 / (1.0 - ADAM_B1 ** ADAM_STEP)
        v_hat = v2 / (1.0 - ADAM_B2 ** ADAM_STEP)
        g_ref[...] = g
        d_ref[...] = -ADAM_LR * (m_hat / (jnp.sqrt(v_hat) + ADAM_EPS) + ADAM_WD * w_ref[...])
        nm_ref[...] = m2
        nv_ref[...] = v2

    blk = pl.BlockSpec((tr, c), lambda i: (i, 0))
    out = jax.ShapeDtypeStruct((r, c), F32)
    return pl.pallas_call(
        body, name=name, grid=(r // tr,),
        in_specs=[pl.BlockSpec((N_DEV, tr, c), lambda i: (0, i, 0)), blk, blk, blk],
        out_specs=[blk, blk, blk, blk], out_shape=[out, out, out, out],
        compiler_params=_params(("parallel",)),
    )(parts, w, m, v)


SHARD_AXIS = dict(
    meta_tokens=1, attn_norm_w=None, ffn_norm_w=None, mix_w_in=2, conv_a_w=2, dn_conv_w=2, dn_a_log=None, dn_dt_bias=None,
    dn_norm_w=None, mix_w_out=1, swa_wq=1, swa_wk=1, swa_wv=1, swa_q_norm_w=None, swa_k_norm_w=None, swa_sinks=None,
    swa_wo=1, ffn_w_up=2, ffn_conv_w=2, ffn_w_down=1)
WEIGHTS = list(SHARD_AXIS)
BIG = ["mix_w_in", "mix_w_out", "swa_wq", "swa_wk", "swa_wv", "swa_wo", "ffn_w_up", "ffn_w_down"]
SMALL = [k for k in WEIGHTS if k not in BIG]
SMALL_SHARDED = [k for k in SMALL if SHARD_AXIS[k] is not None]


def _whole(g8, axis):
    t = jnp.moveaxis(g8, 0, axis)
    return t.reshape(t.shape[:axis] + (t.shape[axis] * t.shape[axis + 1],) + t.shape[axis + 2:])


def _by_owner(a, axis):
    s = a.shape[axis] // N_DEV
    return jnp.moveaxis(a.reshape(a.shape[:axis] + (N_DEV, s) + a.shape[axis + 1:]), axis, 0)


def _pack(arrays, lead=0):
    flat = jnp.concatenate([a.reshape(a.shape[:lead] + (-1,)) for a in arrays], -1)
    n = flat.shape[-1]
    rows = -(-n // (SUBLANES * LANES)) * SUBLANES
    flat = jnp.pad(flat, [(0, 0)] * lead + [(0, rows * LANES - n)])
    return flat.reshape(flat.shape[:lead] + (rows, LANES))


def _unpack(buf, shapes, lead=0):
    flat = buf.reshape(buf.shape[:lead] + (-1,))
    out, o = [], 0
    for s in shapes:
        n = 1
        for e in s:
            n *= e
        out.append(flat[..., o:o + n].reshape(buf.shape[:lead] + tuple(s)))
        o += n
    return out


def kernel(x, meta_tokens, attn_norm_w, ffn_norm_w, mix_w_in, conv_a_w, dn_conv_w, dn_a_log, dn_dt_bias, dn_norm_w, mix_w_out, swa_wq, swa_wk, swa_wv, swa_q_norm_w, swa_k_norm_w, swa_sinks, swa_wo, ffn_w_up, ffn_conv_w, ffn_w_down, loss_target, m_meta_tokens, m_attn_norm_w, m_ffn_norm_w, m_mix_w_in, m_conv_a_w, m_dn_conv_w, m_dn_a_log, m_dn_dt_bias, m_dn_norm_w, m_mix_w_out, m_swa_wq, m_swa_wk, m_swa_wv, m_swa_q_norm_w, m_swa_k_norm_w, m_swa_sinks, m_swa_wo, m_ffn_w_up, m_ffn_conv_w, m_ffn_w_down, v_meta_tokens, v_attn_norm_w, v_ffn_norm_w, v_mix_w_in, v_conv_a_w, v_dn_conv_w, v_dn_a_log, v_dn_dt_bias, v_dn_norm_w, v_mix_w_out, v_swa_wq, v_swa_wk, v_swa_wv, v_swa_q_norm_w, v_swa_k_norm_w, v_swa_sinks, v_swa_wo, v_ffn_w_up, v_ffn_conv_w, v_ffn_w_down):
    w = dict(meta_tokens=meta_tokens, attn_norm_w=attn_norm_w, ffn_norm_w=ffn_norm_w, mix_w_in=mix_w_in, conv_a_w=conv_a_w, dn_conv_w=dn_conv_w, dn_a_log=dn_a_log, dn_dt_bias=dn_dt_bias, dn_norm_w=dn_norm_w, mix_w_out=mix_w_out, swa_wq=swa_wq, swa_wk=swa_wk, swa_wv=swa_wv, swa_q_norm_w=swa_q_norm_w, swa_k_norm_w=swa_k_norm_w, swa_sinks=swa_sinks, swa_wo=swa_wo, ffn_w_up=ffn_w_up, ffn_conv_w=ffn_conv_w, ffn_w_down=ffn_w_down)
    mom = dict(meta_tokens=m_meta_tokens, attn_norm_w=m_attn_norm_w, ffn_norm_w=m_ffn_norm_w, mix_w_in=m_mix_w_in, conv_a_w=m_conv_a_w, dn_conv_w=m_dn_conv_w, dn_a_log=m_dn_a_log, dn_dt_bias=m_dn_dt_bias, dn_norm_w=m_dn_norm_w, mix_w_out=m_mix_w_out, swa_wq=m_swa_wq, swa_wk=m_swa_wk, swa_wv=m_swa_wv, swa_q_norm_w=m_swa_q_norm_w, swa_k_norm_w=m_swa_k_norm_w, swa_sinks=m_swa_sinks, swa_wo=m_swa_wo, ffn_w_up=m_ffn_w_up, ffn_conv_w=m_ffn_conv_w, ffn_w_down=m_ffn_w_down)
    var = dict(meta_tokens=v_meta_tokens, attn_norm_w=v_attn_norm_w, ffn_norm_w=v_ffn_norm_w, mix_w_in=v_mix_w_in, conv_a_w=v_conv_a_w, dn_conv_w=v_dn_conv_w, dn_a_log=v_dn_a_log, dn_dt_bias=v_dn_dt_bias, dn_norm_w=v_dn_norm_w, mix_w_out=v_mix_w_out, swa_wq=v_swa_wq, swa_wk=v_swa_wk, swa_wv=v_swa_wv, swa_q_norm_w=v_swa_q_norm_w, swa_k_norm_w=v_swa_k_norm_w, swa_sinks=v_swa_sinks, swa_wo=v_swa_wo, ffn_w_up=v_ffn_w_up, ffn_conv_w=v_ffn_conv_w, ffn_w_down=v_ffn_w_down)
    me = _my_index()

    shard16 = {k: w[k].astype(MXU_DTYPE) for k in BIG}
    small_shard_shapes = [w[k].shape for k in SMALL_SHARDED]
    got = _all_gather([shard16["mix_w_in"], _pack([w[k] for k in SMALL_SHARDED])], name="gather_weights")
    whole = {"mix_w_in": _whole(got[0], SHARD_AXIS["mix_w_in"])}
    for k, a in zip(SMALL_SHARDED, _unpack(got[1], small_shard_shapes, lead=1)):
        whole[k] = _whole(a, SHARD_AXIS[k])
    for k in SMALL:
        whole.setdefault(k, w[k])
    stages = {"l0": [("mix_w_out", 0), ("ffn_w_up", 0), ("ffn_w_down", 0)],
              "l1": [("swa_wq", 0), ("swa_wk", 0), ("swa_wv", 0), ("swa_wo", 0), ("ffn_w_up", 1), ("ffn_w_down", 1)]}
    pushed = {}
    follows = jnp.zeros((SUBLANES, LANES), F32)
    for stage in ("l0", "l1"):
        pushed[stage] = _push_start("gather", [shard16[k][l] for k, l in stages[stage]], follows, name=f"push_weights_{stage}")
        follows = pushed[stage]["zero"]
    early = _prepare_early(whole)
    early["anw"][0] = early["anw"][0] + follows[0, 0]

    def fetch(stage, after):
        got = _push_wait(pushed[stage], after, name=f"wait_weights_{stage}")
        full = {kl: _whole(a, SHARD_AXIS[kl[0]] - 1) for kl, a in zip(stages[stage], got)}
        if stage == "l0":
            return dict(w_out=full["mix_w_out", 0], w_up0=full["ffn_w_up", 0], w_down0=full["ffn_w_down", 0])
        wqkv = jnp.concatenate([full["swa_wq", 0], full["swa_wk", 0], full["swa_wv", 0]], 1)
        return dict(wqkv=wqkv, wo=full["swa_wo", 0], w_up1=full["ffn_w_up", 1], w_down1=full["ffn_w_down", 1])

    nq, nkv = SWA_HEADS * SWA_HEAD_DIM, SWA_KV_HEADS * SWA_HEAD_DIM
    grad_pushes = {}

    def push(stage, gd):
        if stage == "l1":
            layer = 1
            named = {("swa_wq", 0): gd["wqkv"][:, :nq], ("swa_wk", 0): gd["wqkv"][:, nq:nq + nkv],
                     ("swa_wv", 0): gd["wqkv"][:, nq + nkv:], ("swa_wo", 0): gd["wo"]}
        else:
            layer = 0
            named = {("mix_w_out", 0): gd["w_out"]}
        named[("ffn_w_up", layer)] = gd["w_up"]
        named[("ffn_w_down", layer)] = gd["w_down"]
        sent = [_by_owner(named[kl], SHARD_AXIS[kl[0]] - 1) for kl in stages[stage]]
        grad_pushes[stage] = _push_start("scatter", sent, jnp.zeros((SUBLANES, LANES), F32), name=f"push_grads_{stage}")
        return grad_pushes[stage]["zero"][0, 0]

    loss, g = _local_step(x[0], loss_target[0], early, fetch, push)
    grads = _small_named(g)

    g_in = jnp.concatenate([g["w_in_main"], g["w_in_tail"][:, :N_TAIL]], 1)
    (in_parts,) = _all_to_all([_by_owner(g_in, SHARD_AXIS["mix_w_in"] - 1)], name="scatter_grads")
    parts = {("mix_w_in", 0): in_parts}
    follows = in_parts
    for stage in ("l1", "l0"):
        got = _push_wait(grad_pushes[stage], follows, name=f"wait_grads_{stage}")
        parts.update(zip(stages[stage], got))
        follows = got[0]
    out_g, out_d, out_m, out_v = {}, {}, {}, {}
    for k in BIG:
        per_layer = []
        for l in range(w[k].shape[0]):
            r, c = w[k].shape[1:]
            per_layer.append(_adamw(parts[k, l].reshape(N_DEV, r, c), w[k][l], mom[k][l], var[k][l], name=f"adamw_{k}_{l}"))
        out_g[k], out_d[k], out_m[k], out_v[k] = [jnp.stack(t) for t in zip(*per_layer)]

    small_shapes = [grads[k].shape for k in SMALL]
    (all_small,) = _all_gather([_pack([loss] + [grads[k].astype(F32) for k in SMALL])], name="gather_small_grads")
    loss_parts, *small_parts = _unpack(all_small, [loss.shape] + small_shapes, lead=1)
    mine = []
    for k, p in zip(SMALL, small_parts):
        ax = SHARD_AXIS[k]
        mine.append(p if ax is None else lax.dynamic_slice_in_dim(p, me * w[k].shape[ax], w[k].shape[ax], 1 + ax))
    zero = jnp.zeros(loss.shape, F32)
    packed = [_pack([z] + [d[k] for k in SMALL]) for z, d in ((zero, w), (zero, mom), (zero, var))]
    res = _adamw(_pack([loss_parts] + mine, lead=1), *packed, name="adamw_small")
    shapes = [loss.shape] + [w[k].shape for k in SMALL]
    for t, dst in zip(res, (out_g, out_d, out_m, out_v)):
        parts = _unpack(t, shapes)
        if dst is out_g:
            total_loss = parts[0][0, 0]
        for k, a in zip(SMALL, parts[1:]):
            dst[k] = a

    return (total_loss, g["x"][None], *[out_g[k] for k in WEIGHTS], *[out_d[k] for k in WEIGHTS],
            *[out_m[k] for k in WEIGHTS], *[out_v[k] for k in WEIGHTS])
```

```python
import functools

import jax
import jax.numpy as jnp
from jax import lax
from jax.experimental import pallas as pl
from jax.experimental.pallas import tpu as pltpu

F32 = jnp.float32
BF16 = jnp.bfloat16
MXU_DTYPE = BF16
GRAD_WIRE_DTYPE = BF16

D_MODEL = 1024
N_META = 16
PAD_ROWS = 112
D_CONV = 512
DN_HEADS = 4
DN_HEAD_DIM = 128
DN_DIM = DN_HEADS * DN_HEAD_DIM
DN_CHUNK = 64
SEG = 512
N_SEG = 7
SWA_HEADS = 16
SWA_KV_HEADS = 4
SWA_GROUP = SWA_HEADS // SWA_KV_HEADS
SWA_HEAD_DIM = 64
SWA_BLOCK = 128
D_FF = 2816
EPS = 1e-6
NEG = -1e30
N_DEV = 8

ADAM_LR = 0.001
ADAM_B1 = 0.9
ADAM_B2 = 0.999
ADAM_EPS = 1e-08
ADAM_WD = 0.01
ADAM_STEP = 10

VMEM_LIMIT_BYTES = 52 * 1024 * 1024
SUBLANES = 8
LANES = 128


def _pick(n, prefs):
    for p in prefs:
        if n % p == 0:
            return p
    return n


def _params(sem, vmem=VMEM_LIMIT_BYTES):
    return pltpu.CompilerParams(dimension_semantics=sem, vmem_limit_bytes=vmem)


def _rms(x, w):
    return x * lax.rsqrt(jnp.mean(x * x, -1, keepdims=True) + EPS) * w


def _norm_matmul(h, nw, w, *, o_seg=None, name):
    m, k = h.shape
    n = w.shape[1]
    tm = _pick(m, (1408, 384, 128))
    tn = _pick(o_seg or n, (1408, 1024, 512, 256, 128))

    def body(h_ref, nw_ref, w_ref, o_ref, hn_ref, hn_s):
        @pl.when(pl.program_id(1) == 0)
        def _():
            hn = _rms(h_ref[...], nw_ref[...]).astype(MXU_DTYPE)
            hn_s[...] = hn
            hn_ref[...] = hn

        o_ref[...] = jnp.dot(hn_s[...], w_ref[...], preferred_element_type=F32)

    if o_seg:
        per = o_seg // tn
        o_shape = jax.ShapeDtypeStruct((n // o_seg, m, o_seg), F32)
        o_spec = pl.BlockSpec((None, tm, tn), lambda i, j: (j // per, i, j % per))
    else:
        o_shape = jax.ShapeDtypeStruct((m, n), F32)
        o_spec = pl.BlockSpec((tm, tn), lambda i, j: (i, j))
    return pl.pallas_call(
        body, name=name, grid=(m // tm, n // tn),
        in_specs=[pl.BlockSpec((tm, k), lambda i, j: (i, 0)), pl.BlockSpec((1, k), lambda i, j: (0, 0)),
                  pl.BlockSpec((k, tn), lambda i, j: (0, j))],
        out_specs=[o_spec, pl.BlockSpec((tm, k), lambda i, j: (i, 0))],
        out_shape=[o_shape, jax.ShapeDtypeStruct((m, k), MXU_DTYPE)],
        scratch_shapes=[pltpu.VMEM((tm, k), MXU_DTYPE)],
        compiler_params=_params(("parallel", "arbitrary")),
    )(h, nw, w)


TILE_BUDGET_BYTES = 38 * 1024 * 1024
TILE_SIZES = (2816, 1792, 1536, 1408, 1024, 512, 256, 128)


def _divisor_tiles(n):
    return [t for t in TILE_SIZES if n % t == 0] or [n]


def _mm_nn(a, w, *, res=None, a_seg=False, trans_w=False, name):
    if a_seg:
        s, m, seg = a.shape
    else:
        m, seg = a.shape
        s = 1
    k = s * seg
    n = w.shape[0] if trans_w else w.shape[1]
    tm = _pick(m, (1408, 384, 128))
    ab = a.dtype.itemsize
    k_steps = [(sb, seg) for sb in range(s, 0, -1) if s % sb == 0] if a_seg else [(1, t) for t in _divisor_tiles(seg)]
    best = None
    for tn in _divisor_tiles(n):
        for sb, tk1 in k_steps:
            tk = sb * tk1
            nk = k // tk
            need = 2 * tm * tk * ab + 2 * tk * tn * 2 + 2 * tm * tn * 4 + (tm * tn * 4 if nk > 1 else 0) + (2 * tm * tn * 4 if res is not None else 0)
            if need <= TILE_BUDGET_BYTES and (best is None or tk * tn > best[0]):
                best = (tk * tn, tn, sb, tk1)
    _, tn, sb, tk1 = best
    tk = sb * tk1
    nk = k // tk
    w_dims = _DOT_DIMS["nt" if trans_w else "nn"]

    def body(*refs):
        a_ref, w_ref = refs[:2]
        r_ref = refs[2] if res is not None else None
        o_ref = refs[3 if res is not None else 2]

        def partial_product():
            if not a_seg:
                return lax.dot_general(a_ref[...].astype(MXU_DTYPE), w_ref[...], w_dims, preferred_element_type=F32)
            out = None
            for t in range(sb):
                wt = w_ref[:, t * seg:(t + 1) * seg] if trans_w else w_ref[t * seg:(t + 1) * seg, :]
                d = lax.dot_general(a_ref[t].astype(MXU_DTYPE), wt, w_dims, preferred_element_type=F32)
                out = d if out is None else out + d
            return out

        if nk == 1:
            o_ref[...] = partial_product() if res is None else partial_product() + r_ref[...]
            return
        acc = refs[-1]
        kk = pl.program_id(2)

        @pl.when(kk == 0)
        def _():
            acc[...] = jnp.zeros_like(acc)

        acc[...] += partial_product()

        @pl.when(kk == nk - 1)
        def _():
            o_ref[...] = acc[...] if res is None else acc[...] + r_ref[...]

    a_spec = pl.BlockSpec((sb, tm, seg), lambda i, j, kk: (kk, i, 0)) if a_seg else pl.BlockSpec((tm, tk), lambda i, j, kk: (i, kk))
    w_spec = pl.BlockSpec((tn, tk), lambda i, j, kk: (j, kk)) if trans_w else pl.BlockSpec((tk, tn), lambda i, j, kk: (kk, j))
    in_specs = [a_spec, w_spec]
    args = [a, w]
    if res is not None:
        in_specs.append(pl.BlockSpec((tm, tn), lambda i, j, kk: (i, j)))
        args.append(res)
    return pl.pallas_call(
        body, name=name, grid=(m // tm, n // tn, nk), in_specs=in_specs,
        out_specs=pl.BlockSpec((tm, tn), lambda i, j, kk: (i, j)),
        out_shape=jax.ShapeDtypeStruct((m, n), F32),
        scratch_shapes=[pltpu.VMEM((tm, tn), F32)] if nk > 1 else [],
        compiler_params=_params(("parallel", "parallel", "arbitrary")),
    )(*args)


def _mm_tn(a, b, *, b_seg=False, out_dtype=None, name):
    out_dtype = out_dtype or GRAD_WIRE_DTYPE
    m, ka = a.shape
    if b_seg:
        s, _, seg = b.shape
        n = s * seg
    else:
        n = b.shape[1]
        seg = n
    tmc = _pick(m, (1408, 384, 128))
    best = None
    for tka in _divisor_tiles(ka):
        for tn in _divisor_tiles(seg):
            need = 2 * tmc * tka * a.dtype.itemsize + 2 * tmc * tn * b.dtype.itemsize + tka * tn * 4 + 2 * tka * tn * 4
            if need <= TILE_BUDGET_BYTES and (best is None or (tka * tn, tn) > best[:2]):
                best = (tka * tn, tn, tka)
    _, tn, tka = best
    nm = m // tmc

    def body(a_ref, b_ref, o_ref, acc):
        mm = pl.program_id(2)

        @pl.when(mm == 0)
        def _():
            acc[...] = jnp.zeros_like(acc)

        acc[...] += lax.dot_general(a_ref[...].astype(MXU_DTYPE), b_ref[...].astype(MXU_DTYPE),
                                    (((0,), (0,)), ((), ())), preferred_element_type=F32)

        @pl.when(mm == nm - 1)
        def _():
            o_ref[...] = acc[...].astype(o_ref.dtype)

    if b_seg:
        per = seg // tn
        b_spec = pl.BlockSpec((None, tmc, tn), lambda i, j, mm: (j // per, mm, j % per))
    else:
        b_spec = pl.BlockSpec((tmc, tn), lambda i, j, mm: (mm, j))
    return pl.pallas_call(
        body, name=name, grid=(ka // tka, n // tn, nm),
        in_specs=[pl.BlockSpec((tmc, tka), lambda i, j, mm: (mm, i)), b_spec],
        out_specs=pl.BlockSpec((tka, tn), lambda i, j, mm: (i, j)),
        out_shape=jax.ShapeDtypeStruct((ka, n), out_dtype),
        scratch_shapes=[pltpu.VMEM((tka, tn), F32)],
        compiler_params=_params(("parallel", "parallel", "arbitrary")),
    )(a, b)


def _rmsnorm_bwd(dhn, h, nw, dres, *, name):
    m, d = h.shape
    tm = _pick(m, (384, 128))

    def body(dhn_ref, h_ref, nw_ref, dres_ref, dh_ref, dnw_ref):
        i = pl.program_id(0)
        x = h_ref[...]
        r = lax.rsqrt(jnp.mean(x * x, -1, keepdims=True) + EPS)
        xh = x * r
        dy = dhn_ref[...]
        dxh = dy * nw_ref[...]
        dx = r * (dxh - xh * jnp.mean(dxh * xh, -1, keepdims=True))
        row = i * tm + lax.broadcasted_iota(jnp.int32, (tm, 1), 0)
        dh_ref[...] = jnp.where(row >= PAD_ROWS, dres_ref[...] + dx, 0.0)

        @pl.when(i == 0)
        def _():
            dnw_ref[...] = jnp.zeros_like(dnw_ref)

        dnw_ref[...] += jnp.sum(dy * xh, 0, keepdims=True)

    return pl.pallas_call(
        body, name=name, grid=(m // tm,),
        in_specs=[pl.BlockSpec((tm, d), lambda i: (i, 0)), pl.BlockSpec((tm, d), lambda i: (i, 0)),
                  pl.BlockSpec((1, d), lambda i: (0, 0)), pl.BlockSpec((tm, d), lambda i: (i, 0))],
        out_specs=[pl.BlockSpec((tm, d), lambda i: (i, 0)), pl.BlockSpec((1, d), lambda i: (0, 0))],
        out_shape=[jax.ShapeDtypeStruct((m, d), F32), jax.ShapeDtypeStruct((1, d), F32)],
        compiler_params=_params(("arbitrary",)),
    )(dhn, h, nw, dres)


ROW_CHUNK = 248


def _row_chunks(m):
    out, s = [], SUBLANES
    while s < m:
        n = min(ROW_CHUNK, m - s)
        out.append((s, n))
        s += n
    return out


def _conv_at(load, w, width, s, n):
    acc = w[width - 1:width, :] * load(s, n)
    for j in range(width - 1):
        acc = acc + w[j:j + 1, :] * load(s - (width - 1 - j), n)
    return acc


def _conv_t_at(load, w, width, s, n):
    acc = w[width - 1:width, :] * load(s, n)
    for j in range(width - 1):
        acc = acc + w[j:j + 1, :] * load(s + (width - 1 - j), n)
    return acc


def _dconv_w(load_x, d, width, s, n):
    rows = [jnp.sum(d * load_x(s - (width - 1 - j), n), 0, keepdims=True) for j in range(width)]
    rows.append(jnp.zeros((SUBLANES - width, d.shape[1]), F32))
    return jnp.concatenate(rows, 0)


def _pad_w(w):
    return jnp.concatenate([w, jnp.zeros((SUBLANES - w.shape[0], w.shape[1]), w.dtype)], 0)


def _sigmoid(x):
    return 1.0 / (1.0 + jnp.exp(-x))


def _ffn_act_fwd(u, cw, *, name):
    _, m, f = u.shape
    cb = _pick(f, (256, 128))
    chunks = _row_chunks(m)

    def body(g_ref, v_ref, w_ref, o_ref):
        w = w_ref[...]
        o_ref[pl.ds(0, SUBLANES), :] = jnp.zeros((SUBLANES, cb), o_ref.dtype)
        for s, n in chunks:
            c = _conv_at(lambda a, b: g_ref[pl.ds(a, b), :], w, 3, s, n)
            o_ref[pl.ds(s, n), :] = (c * _sigmoid(c) * v_ref[pl.ds(s, n), :]).astype(o_ref.dtype)

    return pl.pallas_call(
        body, name=name, grid=(f // cb,),
        in_specs=[pl.BlockSpec((None, m, cb), lambda j: (0, 0, j)), pl.BlockSpec((None, m, cb), lambda j: (1, 0, j)),
                  pl.BlockSpec((SUBLANES, cb), lambda j: (0, j))],
        out_specs=pl.BlockSpec((m, cb), lambda j: (0, j)),
        out_shape=jax.ShapeDtypeStruct((m, f), MXU_DTYPE),
        compiler_params=_params(("parallel",)),
    )(u, u, cw)


def _ffn_act_bwd(da, u, cw, *, name):
    _, m, f = u.shape
    cb = LANES
    chunks = _row_chunks(m)

    def body(da_ref, g_ref, v_ref, w_ref, du_ref, dw_ref, dg_s):
        w = w_ref[...]
        zeros8 = jnp.zeros((SUBLANES, cb), F32)
        dg_s[pl.ds(0, SUBLANES), :] = zeros8
        dg_s[pl.ds(m, SUBLANES), :] = zeros8
        du_ref[0, pl.ds(0, SUBLANES), :] = zeros8.astype(du_ref.dtype)
        du_ref[1, pl.ds(0, SUBLANES), :] = zeros8.astype(du_ref.dtype)
        load_g = lambda a, b: g_ref[pl.ds(a, b), :]
        dw = jnp.zeros((SUBLANES, cb), F32)
        for s, n in chunks:
            c = _conv_at(load_g, w, 3, s, n)
            sg = _sigmoid(c)
            d = da_ref[pl.ds(s, n), :]
            du_ref[1, pl.ds(s, n), :] = (d * (c * sg)).astype(du_ref.dtype)
            dc = d * v_ref[pl.ds(s, n), :] * (sg * (1.0 + c * (1.0 - sg)))
            dg_s[pl.ds(s, n), :] = dc
            dw = dw + _dconv_w(load_g, dc, 3, s, n)
        dw_ref[...] = dw
        for s, n in chunks:
            du_ref[0, pl.ds(s, n), :] = _conv_t_at(lambda a, b: dg_s[pl.ds(a, b), :], w, 3, s, n).astype(du_ref.dtype)

    return pl.pallas_call(
        body, name=name, grid=(f // cb,),
        in_specs=[pl.BlockSpec((m, cb), lambda j: (0, j)), pl.BlockSpec((None, m, cb), lambda j: (0, 0, j)),
                  pl.BlockSpec((None, m, cb), lambda j: (1, 0, j)), pl.BlockSpec((SUBLANES, cb), lambda j: (0, j))],
        out_specs=[pl.BlockSpec((2, m, cb), lambda j: (0, 0, j)), pl.BlockSpec((SUBLANES, cb), lambda j: (0, j))],
        out_shape=[jax.ShapeDtypeStruct((2, m, f), MXU_DTYPE), jax.ShapeDtypeStruct((SUBLANES, f), F32)],
        scratch_shapes=[pltpu.VMEM((m + SUBLANES, cb), F32)],
        compiler_params=_params(("parallel",)),
    )(da, u, u, cw)


def _shortconv_fwd(pm, cw, y, *, name):
    _, m, seg = pm.shape
    cb = _pick(seg, (256, 128))
    chunks = _row_chunks(m)

    def body(gi_ref, go_ref, ah_ref, w_ref, y_in, o_ref):
        del y_in
        w = w_ref[...]
        o_ref[pl.ds(0, SUBLANES), :] = jnp.zeros((SUBLANES, cb), o_ref.dtype)
        load_m = lambda a, b: gi_ref[pl.ds(a, b), :] * ah_ref[pl.ds(a, b), :]
        for s, n in chunks:
            o_ref[pl.ds(s, n), :] = (go_ref[pl.ds(s, n), :] * _conv_at(load_m, w, 3, s, n)).astype(o_ref.dtype)

    return pl.pallas_call(
        body, name=name, grid=(seg // cb,),
        in_specs=[pl.BlockSpec((None, m, cb), lambda j: (0, 0, j)), pl.BlockSpec((None, m, cb), lambda j: (1, 0, j)),
                  pl.BlockSpec((None, m, cb), lambda j: (2, 0, j)), pl.BlockSpec((SUBLANES, cb), lambda j: (0, j)),
                  pl.BlockSpec(memory_space=pl.ANY)],
        out_specs=pl.BlockSpec((m, cb), lambda j: (0, j)),
        out_shape=jax.ShapeDtypeStruct(y.shape, y.dtype),
        input_output_aliases={4: 0},
        compiler_params=_params(("parallel",)),
    )(pm, pm, pm, cw, y)


def _shortconv_bwd(dy, pm, cw, dpm, *, name):
    _, m, seg = pm.shape
    cb = LANES
    chunks = _row_chunks(m)

    def body(dy_ref, gi_ref, go_ref, ah_ref, w_ref, dpm_in, dp_ref, dw_ref, dc_s):
        del dpm_in
        w = w_ref[...]
        zeros8 = jnp.zeros((SUBLANES, cb), F32)
        dc_s[pl.ds(0, SUBLANES), :] = zeros8
        dc_s[pl.ds(m, SUBLANES), :] = zeros8
        for t in range(3):
            dp_ref[t, pl.ds(0, SUBLANES), :] = zeros8.astype(dp_ref.dtype)
        load_m = lambda a, b: gi_ref[pl.ds(a, b), :] * ah_ref[pl.ds(a, b), :]
        dw = jnp.zeros((SUBLANES, cb), F32)
        for s, n in chunks:
            d = dy_ref[pl.ds(s, n), :]
            dp_ref[1, pl.ds(s, n), :] = (d * _conv_at(load_m, w, 3, s, n)).astype(dp_ref.dtype)
            dc = d * go_ref[pl.ds(s, n), :]
            dc_s[pl.ds(s, n), :] = dc
            dw = dw + _dconv_w(load_m, dc, 3, s, n)
        dw_ref[...] = dw
        for s, n in chunks:
            dm = _conv_t_at(lambda a, b: dc_s[pl.ds(a, b), :], w, 3, s, n)
            dp_ref[0, pl.ds(s, n), :] = (dm * ah_ref[pl.ds(s, n), :]).astype(dp_ref.dtype)
            dp_ref[2, pl.ds(s, n), :] = (dm * gi_ref[pl.ds(s, n), :]).astype(dp_ref.dtype)

    return pl.pallas_call(
        body, name=name, grid=(seg // cb,),
        in_specs=[pl.BlockSpec((m, cb), lambda j: (0, j)), pl.BlockSpec((None, m, cb), lambda j: (0, 0, j)),
                  pl.BlockSpec((None, m, cb), lambda j: (1, 0, j)), pl.BlockSpec((None, m, cb), lambda j: (2, 0, j)),
                  pl.BlockSpec((SUBLANES, cb), lambda j: (0, j)), pl.BlockSpec(memory_space=pl.ANY)],
        out_specs=[pl.BlockSpec((3, m, cb), lambda j: (0, 0, j)), pl.BlockSpec((SUBLANES, cb), lambda j: (0, j))],
        out_shape=[jax.ShapeDtypeStruct(dpm.shape, dpm.dtype), jax.ShapeDtypeStruct((SUBLANES, seg), F32)],
        scratch_shapes=[pltpu.VMEM((m + SUBLANES, cb), F32)],
        input_output_aliases={5: 0},
        compiler_params=_params(("parallel",)),
    )(dy, pm, pm, pm, cw, dpm)


def _dnpre_fwd(pm, cw, *, name):
    _, m, seg = pm.shape
    cb = _pick(seg, (256, 128))
    per = seg // cb
    chunks = _row_chunks(m)

    def body(x_ref, w_ref, o_ref):
        w = w_ref[...]
        o_ref[pl.ds(0, SUBLANES), :] = jnp.zeros((SUBLANES, cb), F32)
        for s, n in chunks:
            c = _conv_at(lambda a, b: x_ref[pl.ds(a, b), :], w, 4, s, n)
            o_ref[pl.ds(s, n), :] = c * _sigmoid(c)

    return pl.pallas_call(
        body, name=name, grid=(3 * per,),
        in_specs=[pl.BlockSpec((None, m, cb), lambda j: (3 + j // per, 0, j % per)), pl.BlockSpec((SUBLANES, cb), lambda j: (0, j))],
        out_specs=pl.BlockSpec((None, m, cb), lambda j: (j // per, 0, j % per)),
        out_shape=jax.ShapeDtypeStruct((3, m, seg), F32),
        compiler_params=_params(("parallel",)),
    )(pm, cw)


def _dnpre_bwd(dqkv, pm, cw, dpm, *, name):
    _, m, seg = pm.shape
    cb = _pick(seg, (256, 128))
    per = seg // cb
    chunks = _row_chunks(m)

    def body(d_ref, x_ref, w_ref, dpm_in, dp_ref, dw_ref, dc_s):
        del dpm_in
        w = w_ref[...]
        zeros8 = jnp.zeros((SUBLANES, cb), F32)
        dc_s[pl.ds(0, SUBLANES), :] = zeros8
        dc_s[pl.ds(m, SUBLANES), :] = zeros8
        dp_ref[pl.ds(0, SUBLANES), :] = zeros8.astype(dp_ref.dtype)
        load_x = lambda a, b: x_ref[pl.ds(a, b), :]
        dw = jnp.zeros((SUBLANES, cb), F32)
        for s, n in chunks:
            c = _conv_at(load_x, w, 4, s, n)
            sg = _sigmoid(c)
            dc = d_ref[pl.ds(s, n), :] * (sg * (1.0 + c * (1.0 - sg)))
            dc_s[pl.ds(s, n), :] = dc
            dw = dw + _dconv_w(load_x, dc, 4, s, n)
        dw_ref[...] = dw
        for s, n in chunks:
            dp_ref[pl.ds(s, n), :] = _conv_t_at(lambda a, b: dc_s[pl.ds(a, b), :], w, 4, s, n).astype(dp_ref.dtype)

    return pl.pallas_call(
        body, name=name, grid=(3 * per,),
        in_specs=[pl.BlockSpec((None, m, cb), lambda j: (j // per, 0, j % per)),
                  pl.BlockSpec((None, m, cb), lambda j: (3 + j // per, 0, j % per)),
                  pl.BlockSpec((SUBLANES, cb), lambda j: (0, j)), pl.BlockSpec(memory_space=pl.ANY)],
        out_specs=[pl.BlockSpec((None, m, cb), lambda j: (3 + j // per, 0, j % per)), pl.BlockSpec((SUBLANES, cb), lambda j: (0, j))],
        out_shape=[jax.ShapeDtypeStruct(dpm.shape, dpm.dtype), jax.ShapeDtypeStruct((SUBLANES, 3 * seg), F32)],
        scratch_shapes=[pltpu.VMEM((m + SUBLANES, cb), F32)],
        input_output_aliases={3: 0},
        compiler_params=_params(("parallel",)),
    )(dqkv, pm, cw, dpm)


def _mxu_dot_impl(a, b, form):
    a = a.astype(MXU_DTYPE)
    b = b.astype(MXU_DTYPE)
    dims = {"nn": (((1,), (0,)), ((), ())), "nt": (((1,), (1,)), ((), ())), "tn": (((0,), (0,)), ((), ()))}[form]
    return lax.dot_general(a, b, dims, preferred_element_type=F32)


@functools.partial(jax.custom_vjp, nondiff_argnums=(2,))
def _mxu_dot(a, b, form):
    return _mxu_dot_impl(a, b, form)


def _mxu_dot_fwd(a, b, form):
    return _mxu_dot_impl(a, b, form), (a, b)


def _mxu_dot_bwd(form, saved, g):
    a, b = saved
    if form == "nn":
        return _mxu_dot_impl(g, b, "nt"), _mxu_dot_impl(a, g, "tn")
    if form == "nt":
        return _mxu_dot_impl(g, b, "nn"), _mxu_dot_impl(g, a, "tn")
    return _mxu_dot_impl(b, g, "nt"), _mxu_dot_impl(a, g, "nn")


_mxu_dot.defvjp(_mxu_dot_fwd, _mxu_dot_bwd)


_DOT_DIMS = {"nn": (((1,), (0,)), ((), ())), "nt": (((1,), (1,)), ((), ())), "tn": (((0,), (0,)), ((), ()))}


def _split(x):
    hi = x.astype(BF16)
    return hi, (x - hi.astype(F32)).astype(BF16)


def _dot3_impl(a, b, form):
    dg = lambda p, q: lax.dot_general(p, q, _DOT_DIMS[form], preferred_element_type=F32)
    ah, al = _split(a)
    bh, bl = _split(b)
    return dg(ah, bh) + (dg(ah, bl) + dg(al, bh))


@functools.partial(jax.custom_vjp, nondiff_argnums=(2,))
def _dot3(a, b, form):
    return _dot3_impl(a, b, form)


def _dot3_fwd(a, b, form):
    return _dot3_impl(a, b, form), (a, b)


def _dot3_bwd(form, saved, g):
    a, b = saved
    if form == "nn":
        return _dot3_impl(g, b, "nt"), _dot3_impl(a, g, "tn")
    if form == "nt":
        return _dot3_impl(g, b, "nn"), _dot3_impl(g, a, "tn")
    return _dot3_impl(b, g, "nt"), _dot3_impl(a, g, "nn")


_dot3.defvjp(_dot3_fwd, _dot3_bwd)


def _hdot(a, b):
    return _dot3(a, b, "nn")


def _mask_dot(mask, x, form):
    dg = lambda q: lax.dot_general(mask.astype(BF16), q, _DOT_DIMS[form], preferred_element_type=F32)
    x1 = x.astype(BF16)
    r1 = x - x1.astype(F32)
    x2 = r1.astype(BF16)
    x3 = (r1 - x2.astype(F32)).astype(BF16)
    return dg(x1) + (dg(x2) + dg(x3))


def _decay_masks(c):
    row = lax.broadcasted_iota(jnp.int32, (c, c), 0)
    col = lax.broadcasted_iota(jnp.int32, (c, c), 1)
    return (row >= col).astype(F32), row <= col


def _decay_impl(gb):
    lower, upper = _decay_masks(gb.shape[0])
    return _mask_dot(lower, gb, "nn"), _mask_dot(jnp.ones_like(gb), jnp.where(upper, gb, 0.0), "nn")


@jax.custom_vjp
def _decay_matrices(gb):
    return _decay_impl(gb)


def _decay_fwd(gb):
    return _decay_impl(gb), None


def _decay_bwd(_, cts):
    gc, gr = cts
    lower, upper = _decay_masks(gc.shape[0])
    return (_mask_dot(lower, gc, "tn") + jnp.where(upper, _mask_dot(jnp.ones_like(gr), gr, "tn"), 0.0),)


_decay_matrices.defvjp(_decay_fwd, _decay_bwd)


def _softplus(x):
    return jnp.maximum(x, 0.0) + jnp.log(1.0 + jnp.exp(-jnp.abs(x)))


def _heads(f, *lists):
    return [f(*t) for t in zip(*lists)]


DN_STEP_CHUNKS = 3


def _dn_chunk(qr, kr, v, z, braw, araw, alog, dtb, nw, state, valid):
    c = DN_CHUNK
    nh = len(state)
    chunks = len(qr) // nh
    alog, dtb, valid_i = alog * chunks, dtb * chunks, [vv for vv in valid for _ in range(nh)]
    row = lax.broadcasted_iota(jnp.int32, (c, c), 0)
    col = lax.broadcasted_iota(jnp.int32, (c, c), 1)
    incl = row >= col
    strict = row > col
    eye = jnp.where(row == col, 1.0, 0.0)
    q = _heads(lambda t: t * lax.rsqrt(jnp.sum(t * t, -1, keepdims=True) + EPS) * (DN_HEAD_DIM ** -0.5), qr)
    k = _heads(lambda t: t * lax.rsqrt(jnp.sum(t * t, -1, keepdims=True) + EPS), kr)
    beta = _heads(lambda t, vv: _sigmoid(t) * vv, braw, valid_i)
    g = _heads(lambda al, ar, dt, vv: -jnp.exp(al) * _softplus(ar + dt) * vv, alog, araw, dtb, valid_i)
    decay = _heads(lambda t: _decay_matrices(jnp.broadcast_to(t, (c, c))), g)
    dmask = _heads(lambda d: jnp.where(incl, jnp.exp(jnp.where(incl, d[0] - d[1], 0.0)), 0.0), decay)
    dec = _heads(lambda d: d[0][:, :1], decay)
    dlast = _heads(lambda d: d[0][c - 1:c, :1], decay)
    kk = _heads(lambda t: _mxu_dot(t, t, "nt"), k)
    a = _heads(lambda b, t, d: jnp.where(strict, b * t * d, 0.0), beta, kk, dmask)
    x = _heads(lambda t: eye - t, a)
    p = _heads(_hdot, a, a)
    for it in range(5):
        x = _heads(lambda s, t: s + _hdot(s, t), x, p)
        if it < 4:
            p = _heads(_hdot, p, p)
    u = _heads(lambda s, t, b: _hdot(s, t * b), x, v, beta)
    w = _heads(lambda s, t, b, d: _hdot(s, t * (b * jnp.exp(d))), x, k, beta, dec)
    qk = _heads(lambda s, t, d: _mxu_dot(s, t, "nt") * d, q, k, dmask)
    q_dec = _heads(lambda t, d: t * jnp.exp(d), q, dec)
    k_dec = _heads(lambda t, dl, d: t * jnp.exp(dl - d), k, dlast, dec)
    o = []
    for ci in range(chunks):
        of = lambda lst: lst[ci * nh:(ci + 1) * nh]
        v_new = _heads(lambda s, t, st: s - _mxu_dot(t, st, "nn"), of(u), of(w), state)
        o += _heads(lambda qd, st, s, vn: _mxu_dot(qd, st, "nn") + _mxu_dot(s, vn, "nn"), of(q_dec), state, of(qk), v_new)
        state = _heads(lambda st, dl, kd, vn: st * jnp.exp(dl) + _mxu_dot(kd, vn, "tn"), state, of(dlast), of(k_dec), v_new)
    y = _heads(lambda t, zz: _rms(t, nw) * (zz * _sigmoid(zz)), o, z)
    return y, state


DN_STEP_ROWS = DN_STEP_CHUNKS * DN_CHUNK
DN_ITEMS = [(ci, h) for ci in range(DN_STEP_CHUNKS) for h in range(DN_HEADS)]


def _dn_valid(n):
    rows = [n * DN_STEP_ROWS + ci * DN_CHUNK + lax.broadcasted_iota(jnp.int32, (DN_CHUNK, 1), 0) for ci in range(DN_STEP_CHUNKS)]
    return [(r >= PAD_ROWS).astype(F32) for r in rows]


def _dn_in_specs(rev, nc):
    cn = (lambda n: nc - 1 - n) if rev else (lambda n: n)
    c, hd = DN_STEP_ROWS, DN_HEAD_DIM
    return [
        pl.BlockSpec((None, c, DN_DIM), lambda n: (0, cn(n), 0)),
        pl.BlockSpec((None, c, DN_DIM), lambda n: (1, cn(n), 0)),
        pl.BlockSpec((None, c, DN_DIM), lambda n: (2, cn(n), 0)),
        pl.BlockSpec((None, c, DN_DIM), lambda n: (6, cn(n), 0)),
        pl.BlockSpec((DN_HEADS, 2, c, 1), lambda n: (0, 0, cn(n), 0)),
        pl.BlockSpec((DN_HEADS, SUBLANES, LANES), lambda n: (0, 0, 0)),
        pl.BlockSpec((1, hd), lambda n: (0, 0)),
    ]


def _rows(ci):
    return slice(ci * DN_CHUNK, (ci + 1) * DN_CHUNK)


def _cols(h):
    return slice(h * DN_HEAD_DIM, (h + 1) * DN_HEAD_DIM)


def _dn_load(q_ref, k_ref, v_ref, z_ref, ba_ref, hp_ref):
    heads = range(DN_HEADS)
    item = lambda ref: [ref[_rows(ci), _cols(h)] for ci, h in DN_ITEMS]
    return (item(q_ref), item(k_ref), item(v_ref), item(z_ref),
            [ba_ref[h, 0, _rows(ci), :] for ci, h in DN_ITEMS], [ba_ref[h, 1, _rows(ci), :] for ci, h in DN_ITEMS],
            [hp_ref[h, 0:1, 0:1] for h in heads], [hp_ref[h, 1:2, 0:1] for h in heads])


def _delta_fwd(qkvc, pm, ba, hp, nw, *, name):
    _, m, _ = qkvc.shape
    nc = m // DN_STEP_ROWS
    hd = DN_HEAD_DIM

    def body(q_ref, k_ref, v_ref, z_ref, ba_ref, hp_ref, nw_ref, y_ref, s_ref, state):
        n = pl.program_id(0)

        @pl.when(n == 0)
        def _():
            state[...] = jnp.zeros_like(state)

        heads = range(DN_HEADS)
        old = [state[h] for h in heads]
        y, new = _dn_chunk(*_dn_load(q_ref, k_ref, v_ref, z_ref, ba_ref, hp_ref), nw_ref[...], old, _dn_valid(n))
        for h in heads:
            s_ref[h] = old[h]
            state[h] = new[h]
        for (ci, h), yy in zip(DN_ITEMS, y):
            y_ref[_rows(ci), _cols(h)] = yy.astype(y_ref.dtype)

    return pl.pallas_call(
        body, name=name, grid=(nc,), in_specs=_dn_in_specs(False, nc),
        out_specs=[pl.BlockSpec((DN_STEP_ROWS, DN_DIM), lambda n: (n, 1)), pl.BlockSpec((DN_HEADS, None, hd, hd), lambda n: (0, n, 0, 0))],
        out_shape=[jax.ShapeDtypeStruct((m, D_CONV + DN_DIM), MXU_DTYPE), jax.ShapeDtypeStruct((DN_HEADS, nc, hd, hd), F32)],
        scratch_shapes=[pltpu.VMEM((DN_HEADS, hd, hd), F32)],
        compiler_params=_params(("arbitrary",)),
    )(qkvc, qkvc, qkvc, pm, ba, hp, nw)


def _delta_bwd(dy, qkvc, pm, ba, hp, nw, states, *, name):
    _, m, _ = qkvc.shape
    nc = m // DN_STEP_ROWS
    hd, c = DN_HEAD_DIM, DN_STEP_ROWS

    def body(q_ref, k_ref, v_ref, z_ref, ba_ref, hp_ref, nw_ref, s_ref, dy_ref,
             dz_ref, dqkv_ref, dba_ref, dhp_ref, dnw_ref, dstate):
        step = pl.program_id(0)
        n = nc - 1 - step

        @pl.when(step == 0)
        def _():
            dstate[...] = jnp.zeros_like(dstate)
            dhp_ref[...] = jnp.zeros_like(dhp_ref)
            dnw_ref[...] = jnp.zeros_like(dnw_ref)

        valid = _dn_valid(n)
        heads = range(DN_HEADS)
        fn = lambda *a: _dn_chunk(*a, valid)
        _, vjp = jax.vjp(fn, *_dn_load(q_ref, k_ref, v_ref, z_ref, ba_ref, hp_ref), nw_ref[...], [s_ref[h] for h in heads])
        dy = [dy_ref[_rows(ci), _cols(h)] for ci, h in DN_ITEMS]
        dq, dk, dv, dz, dbr, dar, dalog, ddtb, dnw, dst = vjp((dy, [dstate[h] for h in heads]))
        for i, (ci, h) in enumerate(DN_ITEMS):
            dqkv_ref[0, _rows(ci), _cols(h)] = dq[i]
            dqkv_ref[1, _rows(ci), _cols(h)] = dk[i]
            dqkv_ref[2, _rows(ci), _cols(h)] = dv[i]
            dz_ref[_rows(ci), _cols(h)] = dz[i].astype(dz_ref.dtype)
            dba_ref[h, 0, _rows(ci), :] = dbr[i]
            dba_ref[h, 1, _rows(ci), :] = dar[i]
        for h in heads:
            dstate[h] = dst[h]
            dhp_ref[h] += jnp.concatenate([jnp.broadcast_to(dalog[h], (1, LANES)), jnp.broadcast_to(ddtb[h], (1, LANES)),
                                           jnp.zeros((SUBLANES - 2, LANES), F32)], 0)
        dnw_ref[...] += dnw

    rn = lambda n: nc - 1 - n
    in_specs = _dn_in_specs(True, nc) + [
        pl.BlockSpec((DN_HEADS, None, hd, hd), lambda n: (0, rn(n), 0, 0)),
        pl.BlockSpec((c, DN_DIM), lambda n: (rn(n), 1)),
    ]
    out_specs = [
        pl.BlockSpec((None, c, DN_DIM), lambda n: (6, rn(n), 0)),
        pl.BlockSpec((3, c, DN_DIM), lambda n: (0, rn(n), 0)),
        pl.BlockSpec((DN_HEADS, 2, c, 1), lambda n: (0, 0, rn(n), 0)),
        pl.BlockSpec((DN_HEADS, SUBLANES, LANES), lambda n: (0, 0, 0)),
        pl.BlockSpec((1, hd), lambda n: (0, 0)),
    ]
    return pl.pallas_call(
        body, name=name, grid=(nc,), in_specs=in_specs, out_specs=out_specs,
        out_shape=[jax.ShapeDtypeStruct(pm.shape, MXU_DTYPE), jax.ShapeDtypeStruct(qkvc.shape, F32),
                   jax.ShapeDtypeStruct(ba.shape, F32), jax.ShapeDtypeStruct(hp.shape, F32),
                   jax.ShapeDtypeStruct((1, hd), F32)],
        scratch_shapes=[pltpu.VMEM((DN_HEADS, hd, hd), F32)],
        compiler_params=_params(("arbitrary",)),
    )(qkvc, qkvc, qkvc, pm, ba, hp, nw, states, dy)


SWA_PAIR = 2


def _attn_block(q, k0, kp, kc, v0, vp, vc, qw, kw, sink, n):
    g, b, hd = SWA_GROUP, SWA_BLOCK, SWA_HEAD_DIM
    pair = list(range(SWA_PAIR))
    lanes = lambda t, e: t[:, e * hd:(e + 1) * hd]
    q4 = [jnp.concatenate([lanes(q, e * g + i)[None] for i in range(g)], 0) for e in pair]
    qn = _heads(lambda t: _rms(t, qw) * (hd ** -0.5), q4)
    kn = [_rms(jnp.concatenate([lanes(k0, e), lanes(kp, e), lanes(kc, e)], 0), kw) for e in pair]
    vcat = [jnp.concatenate([lanes(v0, e), lanes(vp, e), lanes(vc, e)], 0) for e in pair]
    s = _heads(lambda a, k: _mxu_dot(a.reshape(g * b, hd), k, "nt").reshape(g, b, 3 * b), qn, kn)
    i = lax.broadcasted_iota(jnp.int32, (b, 3 * b), 0)
    c = lax.broadcasted_iota(jnp.int32, (b, 3 * b), 1)
    in_meta, in_prev, in_cur = c < b, (c >= b) & (c < 2 * b), c >= 2 * b
    j = c - jnp.where(in_meta, 0, jnp.where(in_prev, b, 2 * b))
    meta_lo = jnp.where(n == 0, b, PAD_ROWS)
    cur_lo = jnp.where(n == 0, PAD_ROWS, 0)
    prev_off = jnp.where(n >= 2, 0, 2 * b)
    valid = (in_meta & (j >= meta_lo)) | (in_prev & (j > i + prev_off)) | (in_cur & (j <= i) & (j >= cur_lo))
    s = _heads(lambda t: jnp.where(valid[None], t, NEG), s)
    m = [lax.stop_gradient(jnp.maximum(jnp.max(t, -1, keepdims=True), sink[e])) for e, t in zip(pair, s)]
    ex = _heads(lambda t, mm: jnp.exp(t - mm), s, m)
    p = [t / (jnp.sum(t, -1, keepdims=True) + jnp.exp(sink[e] - mm)) for e, t, mm in zip(pair, ex, m)]
    o = _heads(lambda t, v: _mxu_dot(t.reshape(g * b, 3 * b), v, "nn").reshape(g, b, hd), p, vcat)
    return jnp.concatenate([o[e][i] for e in pair for i in range(g)], 1)


Q_LANES = SWA_PAIR * SWA_GROUP * SWA_HEAD_DIM
KV_LANES = SWA_PAIR * SWA_HEAD_DIM
K_BLOCK0 = SWA_HEADS * SWA_HEAD_DIM // KV_LANES
V_BLOCK0 = K_BLOCK0 + SWA_KV_HEADS * SWA_HEAD_DIM // KV_LANES


def _attn_in_specs():
    g, b, hd = SWA_GROUP, SWA_BLOCK, SWA_HEAD_DIM
    kv = lambda f, first: pl.BlockSpec((b, KV_LANES), lambda p, n: (f(n), first + p))
    blocks = [lambda n: 0, lambda n: jnp.maximum(n - 1, 0), lambda n: n]
    return ([pl.BlockSpec((b, Q_LANES), lambda p, n: (n, p))] + [kv(f, K_BLOCK0) for f in blocks] + [kv(f, V_BLOCK0) for f in blocks]
            + [pl.BlockSpec((1, hd), lambda p, n: (0, 0)), pl.BlockSpec((1, hd), lambda p, n: (0, 0)),
               pl.BlockSpec((SWA_PAIR, g, 1, 1), lambda p, n: (p, 0, 0, 0))])


def _attn_fwd(qkv, qw, kw, sink, *, name):
    m = qkv.shape[0]
    b = SWA_BLOCK

    def body(q_ref, k0, kp, kc, v0, vp, vc, qw_ref, kw_ref, s_ref, o_ref):
        o_ref[...] = _attn_block(q_ref[...], k0[...], kp[...], kc[...], v0[...], vp[...], vc[...], qw_ref[...], kw_ref[...],
                                 s_ref[...], pl.program_id(1)).astype(o_ref.dtype)

    return pl.pallas_call(
        body, name=name, grid=(SWA_KV_HEADS // SWA_PAIR, m // b), in_specs=_attn_in_specs(),
        out_specs=pl.BlockSpec((b, Q_LANES), lambda p, n: (n, p)),
        out_shape=jax.ShapeDtypeStruct((m, SWA_HEADS * SWA_HEAD_DIM), MXU_DTYPE),
        compiler_params=_params(("parallel", "parallel")),
    )(*([qkv] * 7), qw, kw, sink)


def _attn_bwd(do, qkv, qw, kw, sink, *, name):
    m = qkv.shape[0]
    g, b, hd = SWA_GROUP, SWA_BLOCK, SWA_HEAD_DIM

    def body(q_ref, k0, kp, kc, v0, vp, vc, qw_ref, kw_ref, s_ref, do_ref, dq_ref, dk_ref, dv_ref, dqw_ref, dkw_ref, ds_ref):
        n = pl.program_id(1)

        @pl.when(n == 0)
        def _():
            for r in (dk_ref, dv_ref, dqw_ref, dkw_ref, ds_ref):
                r[...] = jnp.zeros_like(r)

        fn = lambda *a: _attn_block(*a, n)
        _, vjp = jax.vjp(fn, q_ref[...], k0[...], kp[...], kc[...], v0[...], vp[...], vc[...], qw_ref[...], kw_ref[...], s_ref[...])
        dq, dk0, dkp, dkc, dv0, dvp, dvc, dqw, dkw, dsk = vjp(do_ref[...])
        dq_ref[...] = dq
        prev = pl.multiple_of(jnp.maximum(n - 1, 0) * b, b)
        cur = pl.multiple_of(n * b, b)
        for ref, parts in ((dk_ref, (dk0, dkp, dkc)), (dv_ref, (dv0, dvp, dvc))):
            ref[pl.ds(0, b), :] += parts[0]
            ref[pl.ds(prev, b), :] += parts[1]
            ref[pl.ds(cur, b), :] += parts[2]
        dqw_ref[...] += dqw
        dkw_ref[...] += dkw
        ds_ref[...] += dsk

    pairs = SWA_KV_HEADS // SWA_PAIR
    kv_acc = pl.BlockSpec((m, KV_LANES), lambda p, n: (0, p))
    w_acc = pl.BlockSpec((None, 1, hd), lambda p, n: (p, 0, 0))
    kv_shape = jax.ShapeDtypeStruct((m, SWA_KV_HEADS * hd), F32)
    return pl.pallas_call(
        body, name=name, grid=(pairs, m // b),
        in_specs=_attn_in_specs() + [pl.BlockSpec((b, Q_LANES), lambda p, n: (n, p))],
        out_specs=[pl.BlockSpec((b, Q_LANES), lambda p, n: (n, p)), kv_acc, kv_acc, w_acc, w_acc,
                   pl.BlockSpec((SWA_PAIR, g, 1, 1), lambda p, n: (p, 0, 0, 0))],
        out_shape=[jax.ShapeDtypeStruct((m, SWA_HEADS * hd), F32), kv_shape, kv_shape,
                   jax.ShapeDtypeStruct((pairs, 1, hd), F32), jax.ShapeDtypeStruct((pairs, 1, hd), F32),
                   jax.ShapeDtypeStruct(sink.shape, F32)],
        compiler_params=_params(("parallel", "arbitrary")),
    )(*([qkv] * 7), qw, kw, sink, do)


def _loss_bwd(h, target, *, name):
    m, d = h.shape
    b = SWA_BLOCK

    def body(h_ref, t_ref, l_ref, dh_ref):
        i = pl.program_id(0)

        @pl.when(i == 0)
        def _():
            l_ref[...] = jnp.zeros_like(l_ref)
            dh_ref[...] = jnp.zeros_like(dh_ref)

        @pl.when(i > 0)
        def _():
            e = h_ref[...] - t_ref[...]
            dh_ref[...] = e * (1.0 / d)
            l_ref[...] += jnp.sum(jnp.sum(e * e, 0, keepdims=True), 1, keepdims=True) * (0.5 / d)

    return pl.pallas_call(
        body, name=name, grid=(m // b,),
        in_specs=[pl.BlockSpec((b, d), lambda i: (i, 0)), pl.BlockSpec((b, d), lambda i: (jnp.maximum(i - 1, 0), 0))],
        out_specs=[pl.BlockSpec((1, LANES), lambda i: (0, 0)), pl.BlockSpec((b, d), lambda i: (i, 0))],
        out_shape=[jax.ShapeDtypeStruct((1, LANES), F32), jax.ShapeDtypeStruct((m, d), F32)],
        compiler_params=_params(("arbitrary",)),
    )(h, target)


def _ffn_fwd(h, nw, w_up, cw, w_down, tag):
    u, hn = _norm_matmul(h, nw, w_up, o_seg=D_FF, name=f"ffn_up_{tag}")
    a = _ffn_act_fwd(u, cw, name=f"ffn_act_{tag}")
    return _mm_nn(a, w_down, res=h, name=f"ffn_down_{tag}"), (h, hn, u, a)


def _ffn_bwd(dh, saved, nw, w_up, cw, w_down, tag):
    h, hn, u, a = saved
    da = _mm_nn(dh, w_down, trans_w=True, name=f"ffn_da_{tag}")
    dw_down = _mm_tn(a, dh, name=f"ffn_dwdown_{tag}")
    du, dcw = _ffn_act_bwd(da, u, cw, name=f"ffn_act_bwd_{tag}")
    dhn = _mm_nn(du, w_up, a_seg=True, trans_w=True, name=f"ffn_dhn_{tag}")
    dw_up = _mm_tn(hn, du, b_seg=True, name=f"ffn_dwup_{tag}")
    dh_in, dnw = _rmsnorm_bwd(dhn, h, nw, dh, name=f"ffn_norm_bwd_{tag}")
    return dh_in, dnw, dw_up, dcw, dw_down


def _local_step(x, target, w, fetch=None, push=None):
    fetch = fetch or (lambda stage, after: {})
    push = push or (lambda stage, grads: None)
    plus = lambda a, zero: a if zero is None else a + zero
    seq, d = x.shape
    m = PAD_ROWS + N_META + seq
    h0 = jnp.concatenate([jnp.zeros((PAD_ROWS, d), F32), w["meta"], x], 0)

    pm, hn0 = _norm_matmul(h0, w["anw"][0], w["w_in_main"], o_seg=SEG, name="mix_in")
    pba = _mm_nn(hn0, w["w_in_tail"], name="mix_in_tail")
    qkvc = _dnpre_fwd(pm, w["dcw"], name="dn_conv")
    ba = pba[:, :2 * DN_HEADS].T.reshape(2, DN_HEADS, m, 1).transpose(1, 0, 2, 3)
    y, states = _delta_fwd(qkvc, pm, ba, w["hp"], w["dnw"], name="delta")
    y = _shortconv_fwd(pm, w["caw"], y, name="shortconv")
    w = {**w, **fetch("l0", y)}
    h1 = _mm_nn(y, w["w_out"], res=h0, name="mix_out")
    h2, ffn0 = _ffn_fwd(h1, w["fnw"][0], w["w_up0"], w["fcw"][0], w["w_down0"], "l0")

    w = {**w, **fetch("l1", h2)}
    qkv, hn2 = _norm_matmul(h2, w["anw"][1], w["wqkv"], name="attn_qkv")
    o = _attn_fwd(qkv, w["qnw"], w["knw"], w["sink"], name="attn")
    h3 = _mm_nn(o, w["wo"], res=h2, name="attn_out")
    h4, ffn1 = _ffn_fwd(h3, w["fnw"][1], w["w_up1"], w["fcw"][1], w["w_down1"], "l1")

    loss, dh4 = _loss_bwd(h4, target, name="loss")

    g = {}
    dh3, dfnw1, dwup1, dfcw1, dwdown1 = _ffn_bwd(dh4, ffn1, w["fnw"][1], w["w_up1"], w["fcw"][1], w["w_down1"], "l1")

    do = _mm_nn(dh3, w["wo"], trans_w=True, name="attn_do")
    g["wo"] = _mm_tn(o, dh3, name="attn_dwo")
    dq, dk, dv, dqw, dkw, dsink = _attn_bwd(do, qkv, w["qnw"], w["knw"], w["sink"], name="attn_bwd")
    dqkv = jnp.concatenate([dq, dk, dv], 1).astype(MXU_DTYPE)
    dhn2 = _mm_nn(dqkv, w["wqkv"], trans_w=True, name="attn_dhn")
    g["wqkv"] = _mm_tn(hn2, dqkv, name="attn_dwqkv")
    zero = push("l1", dict(w_up=dwup1, w_down=dwdown1, wo=g["wo"], wqkv=g["wqkv"]))
    dh2, danw1 = _rmsnorm_bwd(dhn2, h2, plus(w["anw"][1], zero), dh3, name="attn_norm_bwd")

    dh1, dfnw0, dwup0, dfcw0, dwdown0 = _ffn_bwd(dh2, ffn0, w["fnw"][0], w["w_up0"], w["fcw"][0], w["w_down0"], "l0")

    dy = _mm_nn(dh1, w["w_out"], trans_w=True, name="mix_dy")
    g["w_out"] = _mm_tn(y, dh1, name="mix_dwout")
    zero = push("l0", dict(w_up=dwup0, w_down=dwdown0, w_out=g["w_out"]))
    dpm, dqkvc, dba, dhp, ddnw = _delta_bwd(dy, qkvc, pm, ba, w["hp"], plus(w["dnw"], zero), states, name="delta_bwd")
    dpm, ddcw = _dnpre_bwd(dqkvc, pm, w["dcw"], dpm, name="dn_conv_bwd")
    dpm, dcaw = _shortconv_bwd(dy, pm, w["caw"], dpm, name="shortconv_bwd")
    dpba = jnp.pad(dba.transpose(1, 0, 2, 3).reshape(2 * DN_HEADS, m).T, ((0, 0), (0, LANES - 2 * DN_HEADS))).astype(MXU_DTYPE)
    dhn0 = _mm_nn(dpm, w["w_in_main"], a_seg=True, trans_w=True, name="mix_dhn")
    dhn0 = _mm_nn(dpba, w["w_in_tail"], res=dhn0, trans_w=True, name="mix_dhn_tail")
    g["w_in_main"] = _mm_tn(hn0, dpm, b_seg=True, name="mix_dwin")
    g["w_in_tail"] = _mm_tn(hn0, dpba, name="mix_dwin_tail")
    zero = push("in", dict(w_in_main=g["w_in_main"], w_in_tail=g["w_in_tail"]))
    dh0, danw0 = _rmsnorm_bwd(dhn0, h0, plus(w["anw"][0], zero), dh1, name="mix_norm_bwd")

    g.update(
        x=dh0[PAD_ROWS + N_META:], meta=dh0[PAD_ROWS:PAD_ROWS + N_META], anw=[danw0, danw1], fnw=[dfnw0, dfnw1],
        caw=dcaw, dcw=ddcw, hp=dhp, dnw=ddnw, qnw=jnp.sum(dqw, 0), knw=jnp.sum(dkw, 0), sink=dsink,
        w_up=[dwup0, dwup1], fcw=[dfcw0, dfcw1], w_down=[dwdown0, dwdown1])
    return loss, g


N_TAIL = 2 * DN_HEADS


def _prepare_early(p):
    n_main = N_SEG * SEG
    w_in = p["mix_w_in"][0]
    tail = jnp.pad(w_in[:, n_main:], ((0, 0), (0, LANES - N_TAIL)))
    hp = jnp.zeros((DN_HEADS, SUBLANES, LANES), F32)
    hp = hp.at[:, 0, :].set(p["dn_a_log"][0][:, None]).at[:, 1, :].set(p["dn_dt_bias"][0][:, None])
    depth = p["ffn_conv_w"].shape[0]
    return dict(
        meta=p["meta_tokens"], anw=[p["attn_norm_w"][i:i + 1] for i in range(depth)],
        fnw=[p["ffn_norm_w"][i:i + 1] for i in range(depth)],
        w_in_main=w_in[:, :n_main], w_in_tail=tail,
        caw=_pad_w(p["conv_a_w"][0]), dcw=_pad_w(p["dn_conv_w"][0]), hp=hp, dnw=p["dn_norm_w"],
        qnw=p["swa_q_norm_w"], knw=p["swa_k_norm_w"], sink=p["swa_sinks"].reshape(SWA_KV_HEADS, SWA_GROUP, 1, 1),
        fcw=[_pad_w(p["ffn_conv_w"][i]) for i in range(depth)])


def _prepare_weights(p):
    return dict(
        _prepare_early(p), w_out=p["mix_w_out"][0], wo=p["swa_wo"][0],
        wqkv=jnp.concatenate([p["swa_wq"][0], p["swa_wk"][0], p["swa_wv"][0]], 1),
        w_up0=p["ffn_w_up"][0], w_up1=p["ffn_w_up"][1], w_down0=p["ffn_w_down"][0], w_down1=p["ffn_w_down"][1])


def _small_named(g):
    return dict(
        meta_tokens=g["meta"], attn_norm_w=jnp.concatenate(g["anw"], 0), ffn_norm_w=jnp.concatenate(g["fnw"], 0),
        conv_a_w=g["caw"][None, :3], dn_conv_w=g["dcw"][None, :4],
        dn_a_log=g["hp"][None, :, 0, 0], dn_dt_bias=g["hp"][None, :, 1, 0], dn_norm_w=g["dnw"],
        swa_q_norm_w=g["qnw"], swa_k_norm_w=g["knw"], swa_sinks=g["sink"].reshape(1, SWA_HEADS),
        ffn_conv_w=jnp.stack([c[:3] for c in g["fcw"]]))


def _reference_named(g):
    nq, nkv = SWA_HEADS * SWA_HEAD_DIM, SWA_KV_HEADS * SWA_HEAD_DIM
    return dict(
        _small_named(g), mix_w_in=jnp.concatenate([g["w_in_main"], g["w_in_tail"][:, :N_TAIL]], 1)[None],
        mix_w_out=g["w_out"][None], swa_wq=g["wqkv"][None, :, :nq], swa_wk=g["wqkv"][None, :, nq:nq + nkv],
        swa_wv=g["wqkv"][None, :, nq + nkv:], swa_wo=g["wo"][None],
        ffn_w_up=jnp.stack(g["w_up"]), ffn_w_down=jnp.stack(g["w_down"]))


def _my_index():
    return 4 * lax.axis_index("x") + 2 * lax.axis_index("y") + lax.axis_index("c")


def _all_gather(arrays, *, name):
    n = len(arrays)

    def body(*refs):
        ins, outs = refs[:n], refs[n:2 * n]
        send_sems, recv_sems, local_sems = refs[2 * n:]
        x, y, c = lax.axis_index("x"), lax.axis_index("y"), lax.axis_index("c")
        me, sibling = (x, y, c), (x, y, 1 - c)
        chips = [(1 - x, y), (x, 1 - y), (1 - x, 1 - y)]

        def copy(i, k, block, to, src=None):
            rows = outs[i].at[4 * block[0] + 2 * block[1] + block[2]]
            return pltpu.make_async_remote_copy(
                src_ref=rows if src is None else src, dst_ref=rows, send_sem=send_sems.at[i, k], recv_sem=recv_sems.at[i, k],
                device_id=to, device_id_type=pl.DeviceIdType.MESH)

        mine = [pltpu.make_async_copy(ins[i], outs[i].at[4 * x + 2 * y + c], local_sems.at[i]) for i in range(n)]
        first = []
        for j, chip in enumerate(chips):
            first += [copy(i, 1 + j, me, (*chip, c), src=ins[i]) for i in range(n)]
        first += [copy(i, 0, me, sibling, src=ins[i]) for i in range(n)]
        for cp in first + mine:
            cp.start()
        passed = []
        for j, chip in enumerate(chips):
            for i in range(n):
                copy(i, 1 + j, (*chip, c), me).wait_recv()
                fwd = copy(i, 4 + j, (*chip, c), sibling)
                fwd.start()
                passed.append(fwd)
        for i in range(n):
            copy(i, 0, sibling, me).wait_recv()
            for j, chip in enumerate(chips):
                copy(i, 4 + j, (*chip, 1 - c), me).wait_recv()
        for cp in first + passed:
            cp.wait_send()
        for cp in mine:
            cp.wait()

    hbm = pl.BlockSpec(memory_space=pl.ANY)
    return pl.pallas_call(
        body, name=name, in_specs=[hbm] * n, out_specs=[hbm] * n,
        out_shape=[jax.ShapeDtypeStruct((N_DEV,) + tuple(a.shape), a.dtype) for a in arrays],
        scratch_shapes=[pltpu.SemaphoreType.DMA((n, 7)), pltpu.SemaphoreType.DMA((n, 7)), pltpu.SemaphoreType.DMA((n,))],
    )(*arrays)


def _peer(d):
    px, py, pc = lax.axis_index("x") ^ (d >> 2), lax.axis_index("y") ^ ((d >> 1) & 1), lax.axis_index("c") ^ (d & 1)
    return (px, py, pc), 4 * px + 2 * py + pc


def _push_copies(mode, srcs, lands, send_sems, recv_sems):
    me = _my_index()
    out = []
    for d in range(1, N_DEV):
        pos, idx = _peer(d)
        for i in range(len(srcs)):
            out.append(pltpu.make_async_remote_copy(
                src_ref=srcs[i] if mode == "gather" else srcs[i].at[idx], dst_ref=lands[i].at[me],
                send_sem=send_sems.at[i * N_DEV + d], recv_sem=recv_sems.at[i * N_DEV + d], device_id=pos,
                device_id_type=pl.DeviceIdType.MESH))
    return out


_HBM = pl.BlockSpec(memory_space=pltpu.HBM)
_SEM = pl.BlockSpec(memory_space=pltpu.SEMAPHORE)


def _push_start(mode, arrays, follows, *, name):
    n = len(arrays)
    blocks = [a.shape if mode == "gather" else a.shape[1:] for a in arrays]
    lands = [lax.empty((N_DEV,) + tuple(b), a.dtype) for a, b in zip(arrays, blocks)]

    def body(*refs):
        srcs, land_refs = refs[:n], refs[n:2 * n]
        send_sems, recv_sems = refs[2 * n + 1], refs[2 * n + 2]
        zero = refs[-1]
        for cp in _push_copies(mode, srcs, land_refs, send_sems, recv_sems):
            cp.start()
        zero[...] = jnp.zeros_like(zero)

    hbm_in = [pltpu.with_memory_space_constraint(a, pltpu.HBM) for a in list(arrays) + lands]
    outs = pl.pallas_call(
        body, name=name,
        out_shape=[pltpu.SemaphoreType.DMA((n * N_DEV,)), pltpu.SemaphoreType.DMA((n * N_DEV,))]
        + [pltpu.HBM(a.shape, a.dtype) for a in hbm_in] + [jax.ShapeDtypeStruct((SUBLANES, LANES), F32)],
        in_specs=[_HBM] * (2 * n) + [pl.BlockSpec(memory_space=pl.ANY)],
        out_specs=[_SEM, _SEM] + [_HBM] * (2 * n) + [pl.BlockSpec(memory_space=pltpu.VMEM)],
        input_output_aliases={i: 2 + i for i in range(2 * n)},
        compiler_params=pltpu.CompilerParams(has_side_effects=pltpu.SideEffectType.DATAFLOW_SIDE_EFFECTING),
    )(*hbm_in, follows)
    return dict(mode=mode, sems=outs[:2], srcs=outs[2:2 + n], lands=outs[2 + n:2 + 2 * n], zero=outs[-1])


def _push_wait(push, follows, *, name):
    n = len(push["srcs"])
    mode = push["mode"]

    def body(*refs):
        srcs, land_refs = refs[:n], refs[n:2 * n]
        send_sems, recv_sems = refs[2 * n], refs[2 * n + 1]
        for cp in _push_copies(mode, srcs, land_refs, send_sems, recv_sems):
            cp.wait_send()
            cp.wait_recv()

    args = list(push["srcs"]) + list(push["lands"])
    outs = pl.pallas_call(
        body, name=name, out_shape=[pltpu.HBM(a.shape, a.dtype) for a in args],
        in_specs=[_HBM] * (2 * n) + [_SEM, _SEM, pl.BlockSpec(memory_space=pl.ANY)], out_specs=[_HBM] * (2 * n),
        input_output_aliases={i: i for i in range(2 * n)},
        compiler_params=pltpu.CompilerParams(has_side_effects=pltpu.SideEffectType.DATAFLOW_SIDE_EFFECTING),
    )(*args, *push["sems"], follows)
    me = _my_index()
    got = []
    for src, land in zip(outs[:n], outs[n:]):
        own = src if mode == "gather" else lax.dynamic_index_in_dim(src, me, 0, keepdims=False)
        got.append(lax.dynamic_update_index_in_dim(land, own, me, 0))
    return got


def _row_tile(r, c):
    best = None
    for t in range(2 * SUBLANES, r + 1, 2 * SUBLANES):
        if r % t == 0 and N_DEV * t * c * 4 <= 6 * 1024 * 1024:
            best = t
    return best or r


def _adamw(parts, w, m, v, layer, outs=None, *, name):
    nl, r, c = w.shape
    tr = _row_tile(r, c)

    def body(p_ref, w_ref, m_ref, v_ref, *rest):
        g_ref, d_ref, nm_ref, nv_ref = rest[-4:]
        g = p_ref[0].astype(F32)
        for j in range(1, N_DEV):
            g = g + p_ref[j].astype(F32)
        m2 = ADAM_B1 * m_ref[...] + (1.0 - ADAM_B1) * g
        v2 = ADAM_B2 * v_ref[...] + (1.0 - ADAM_B2) * jnp.square(g)
        m_hat = m2 / (1.0 - ADAM_B1 ** ADAM_STEP)
        v_hat = v2 / (1.0 - ADAM_B2 ** ADAM_STEP)
        g_ref[...] = g
        d_ref[...] = -ADAM_LR * (m_hat / (jnp.sqrt(v_hat) + ADAM_EPS) + ADAM_WD * w_ref[...])
        nm_ref[...] = m2
        nv_ref[...] = v2

    blk = pl.BlockSpec((None, tr, c), lambda i: (layer, i, 0))
    out = jax.ShapeDtypeStruct((nl, r, c), F32)
    given = list(outs) if outs is not None else []
    return pl.pallas_call(
        body, name=name, grid=(r // tr,),
        in_specs=[pl.BlockSpec((N_DEV, tr, c), lambda i: (0, i, 0)), blk, blk, blk] + [pl.BlockSpec(memory_space=pl.ANY)] * len(given),
        out_specs=[blk, blk, blk, blk], out_shape=[out, out, out, out],
        input_output_aliases={4 + t: t for t in range(len(given))},
        compiler_params=_params(("parallel",)),
    )(parts, w, m, v, *given)


SHARD_AXIS = dict(
    meta_tokens=1, attn_norm_w=None, ffn_norm_w=None, mix_w_in=2, conv_a_w=2, dn_conv_w=2, dn_a_log=None, dn_dt_bias=None,
    dn_norm_w=None, mix_w_out=1, swa_wq=1, swa_wk=1, swa_wv=1, swa_q_norm_w=None, swa_k_norm_w=None, swa_sinks=None,
    swa_wo=1, ffn_w_up=2, ffn_conv_w=2, ffn_w_down=1)
WEIGHTS = list(SHARD_AXIS)
BIG = ["mix_w_in", "mix_w_out", "swa_wq", "swa_wk", "swa_wv", "swa_wo", "ffn_w_up", "ffn_w_down"]
SMALL = [k for k in WEIGHTS if k not in BIG]
SMALL_SHARDED = [k for k in SMALL if SHARD_AXIS[k] is not None]


def _whole(g8, axis):
    t = jnp.moveaxis(g8, 0, axis)
    return t.reshape(t.shape[:axis] + (t.shape[axis] * t.shape[axis + 1],) + t.shape[axis + 2:])


def _by_owner(a, axis):
    s = a.shape[axis] // N_DEV
    return jnp.moveaxis(a.reshape(a.shape[:axis] + (N_DEV, s) + a.shape[axis + 1:]), axis, 0)


def _pack(arrays, lead=0):
    flat = jnp.concatenate([a.reshape(a.shape[:lead] + (-1,)) for a in arrays], -1)
    n = flat.shape[-1]
    rows = -(-n // (SUBLANES * LANES)) * SUBLANES
    flat = jnp.pad(flat, [(0, 0)] * lead + [(0, rows * LANES - n)])
    return flat.reshape(flat.shape[:lead] + (rows, LANES))


def _unpack(buf, shapes, lead=0):
    flat = buf.reshape(buf.shape[:lead] + (-1,))
    out, o = [], 0
    for s in shapes:
        n = 1
        for e in s:
            n *= e
        out.append(flat[..., o:o + n].reshape(buf.shape[:lead] + tuple(s)))
        o += n
    return out


def kernel(x, meta_tokens, attn_norm_w, ffn_norm_w, mix_w_in, conv_a_w, dn_conv_w, dn_a_log, dn_dt_bias, dn_norm_w, mix_w_out, swa_wq, swa_wk, swa_wv, swa_q_norm_w, swa_k_norm_w, swa_sinks, swa_wo, ffn_w_up, ffn_conv_w, ffn_w_down, loss_target, m_meta_tokens, m_attn_norm_w, m_ffn_norm_w, m_mix_w_in, m_conv_a_w, m_dn_conv_w, m_dn_a_log, m_dn_dt_bias, m_dn_norm_w, m_mix_w_out, m_swa_wq, m_swa_wk, m_swa_wv, m_swa_q_norm_w, m_swa_k_norm_w, m_swa_sinks, m_swa_wo, m_ffn_w_up, m_ffn_conv_w, m_ffn_w_down, v_meta_tokens, v_attn_norm_w, v_ffn_norm_w, v_mix_w_in, v_conv_a_w, v_dn_conv_w, v_dn_a_log, v_dn_dt_bias, v_dn_norm_w, v_mix_w_out, v_swa_wq, v_swa_wk, v_swa_wv, v_swa_q_norm_w, v_swa_k_norm_w, v_swa_sinks, v_swa_wo, v_ffn_w_up, v_ffn_conv_w, v_ffn_w_down):
    w = dict(meta_tokens=meta_tokens, attn_norm_w=attn_norm_w, ffn_norm_w=ffn_norm_w, mix_w_in=mix_w_in, conv_a_w=conv_a_w, dn_conv_w=dn_conv_w, dn_a_log=dn_a_log, dn_dt_bias=dn_dt_bias, dn_norm_w=dn_norm_w, mix_w_out=mix_w_out, swa_wq=swa_wq, swa_wk=swa_wk, swa_wv=swa_wv, swa_q_norm_w=swa_q_norm_w, swa_k_norm_w=swa_k_norm_w, swa_sinks=swa_sinks, swa_wo=swa_wo, ffn_w_up=ffn_w_up, ffn_conv_w=ffn_conv_w, ffn_w_down=ffn_w_down)
    mom = dict(meta_tokens=m_meta_tokens, attn_norm_w=m_attn_norm_w, ffn_norm_w=m_ffn_norm_w, mix_w_in=m_mix_w_in, conv_a_w=m_conv_a_w, dn_conv_w=m_dn_conv_w, dn_a_log=m_dn_a_log, dn_dt_bias=m_dn_dt_bias, dn_norm_w=m_dn_norm_w, mix_w_out=m_mix_w_out, swa_wq=m_swa_wq, swa_wk=m_swa_wk, swa_wv=m_swa_wv, swa_q_norm_w=m_swa_q_norm_w, swa_k_norm_w=m_swa_k_norm_w, swa_sinks=m_swa_sinks, swa_wo=m_swa_wo, ffn_w_up=m_ffn_w_up, ffn_conv_w=m_ffn_conv_w, ffn_w_down=m_ffn_w_down)
    var = dict(meta_tokens=v_meta_tokens, attn_norm_w=v_attn_norm_w, ffn_norm_w=v_ffn_norm_w, mix_w_in=v_mix_w_in, conv_a_w=v_conv_a_w, dn_conv_w=v_dn_conv_w, dn_a_log=v_dn_a_log, dn_dt_bias=v_dn_dt_bias, dn_norm_w=v_dn_norm_w, mix_w_out=v_mix_w_out, swa_wq=v_swa_wq, swa_wk=v_swa_wk, swa_wv=v_swa_wv, swa_q_norm_w=v_swa_q_norm_w, swa_k_norm_w=v_swa_k_norm_w, swa_sinks=v_swa_sinks, swa_wo=v_swa_wo, ffn_w_up=v_ffn_w_up, ffn_conv_w=v_ffn_conv_w, ffn_w_down=v_ffn_w_down)
    me = _my_index()

    shard16 = {k: w[k].astype(MXU_DTYPE) for k in BIG}
    small_shard_shapes = [w[k].shape for k in SMALL_SHARDED]
    got = _all_gather([shard16["mix_w_in"], _pack([w[k] for k in SMALL_SHARDED])], name="gather_weights")
    whole = {"mix_w_in": _whole(got[0], SHARD_AXIS["mix_w_in"])}
    for k, a in zip(SMALL_SHARDED, _unpack(got[1], small_shard_shapes, lead=1)):
        whole[k] = _whole(a, SHARD_AXIS[k])
    for k in SMALL:
        whole.setdefault(k, w[k])
    stages = {"in": [("mix_w_in", 0)], "l0": [("mix_w_out", 0), ("ffn_w_up", 0), ("ffn_w_down", 0)],
              "l1": [("swa_wq", 0), ("swa_wk", 0), ("swa_wv", 0), ("swa_wo", 0), ("ffn_w_up", 1), ("ffn_w_down", 1)]}
    pushed = {}
    follows = jnp.zeros((SUBLANES, LANES), F32)
    for stage in ("l0", "l1"):
        pushed[stage] = _push_start("gather", [shard16[k][l] for k, l in stages[stage]], follows, name=f"push_weights_{stage}")
        follows = pushed[stage]["zero"]
    early = _prepare_early(whole)
    early["anw"][0] = early["anw"][0] + follows[0, 0]

    def fetch(stage, after):
        got = _push_wait(pushed[stage], after, name=f"wait_weights_{stage}")
        full = {kl: _whole(a, SHARD_AXIS[kl[0]] - 1) for kl, a in zip(stages[stage], got)}
        if stage == "l0":
            return dict(w_out=full["mix_w_out", 0], w_up0=full["ffn_w_up", 0], w_down0=full["ffn_w_down", 0])
        wqkv = jnp.concatenate([full["swa_wq", 0], full["swa_wk", 0], full["swa_wv", 0]], 1)
        return dict(wqkv=wqkv, wo=full["swa_wo", 0], w_up1=full["ffn_w_up", 1], w_down1=full["ffn_w_down", 1])

    nq, nkv = SWA_HEADS * SWA_HEAD_DIM, SWA_KV_HEADS * SWA_HEAD_DIM
    grad_pushes = {}

    def push(stage, gd):
        if stage == "in":
            named = {("mix_w_in", 0): jnp.concatenate([gd["w_in_main"], gd["w_in_tail"][:, :N_TAIL]], 1)}
        elif stage == "l1":
            named = {("swa_wq", 0): gd["wqkv"][:, :nq], ("swa_wk", 0): gd["wqkv"][:, nq:nq + nkv],
                     ("swa_wv", 0): gd["wqkv"][:, nq + nkv:], ("swa_wo", 0): gd["wo"],
                     ("ffn_w_up", 1): gd["w_up"], ("ffn_w_down", 1): gd["w_down"]}
        else:
            named = {("mix_w_out", 0): gd["w_out"], ("ffn_w_up", 0): gd["w_up"], ("ffn_w_down", 0): gd["w_down"]}
        sent = [_by_owner(named[kl], SHARD_AXIS[kl[0]] - 1) for kl in stages[stage]]
        grad_pushes[stage] = _push_start("scatter", sent, jnp.zeros((SUBLANES, LANES), F32), name=f"push_grads_{stage}")
        return grad_pushes[stage]["zero"][0, 0]

    loss, g = _local_step(x[0], loss_target[0], early, fetch, push)
    grads = _small_named(g)

    results = {}

    def update(stage, follows):
        got = _push_wait(grad_pushes[stage], follows, name=f"wait_grads_{stage}")
        for (k, l), parts in zip(stages[stage], got):
            r, c = w[k].shape[1:]
            results[k] = _adamw(parts.reshape(N_DEV, r, c), w[k], mom[k], var[k], l, results.get(k), name=f"adamw_{k}_{l}")
        return results[stages[stage][0][0]][0]

    follows = update("l0", update("l1", g["meta"]))

    small_shapes = [grads[k].shape for k in SMALL]
    (all_small,) = _all_gather([_pack([loss] + [grads[k].astype(F32) for k in SMALL])], name="gather_small_grads")
    loss_parts, *small_parts = _unpack(all_small, [loss.shape] + small_shapes, lead=1)
    mine = []
    for k, p in zip(SMALL, small_parts):
        ax = SHARD_AXIS[k]
        mine.append(p if ax is None else lax.dynamic_slice_in_dim(p, me * w[k].shape[ax], w[k].shape[ax], 1 + ax))
    zero = jnp.zeros(loss.shape, F32)
    packed = [_pack([z] + [d[k] for k in SMALL]) for z, d in ((zero, w), (zero, mom), (zero, var))]
    res = _adamw(_pack([loss_parts] + mine, lead=1), *[t[None] for t in packed], 0, name="adamw_small")
    shapes = [loss.shape] + [w[k].shape for k in SMALL]
    for t, which in zip(res, range(4)):
        for k, a in zip(["loss"] + SMALL, _unpack(t[0], shapes)):
            results.setdefault(k, [None] * 4)[which] = a
    update("in", jnp.maximum(follows[0, :1, :1], res[0][0, :1, :1]))

    outs = [[results[k][which] for k in WEIGHTS] for which in range(4)]
    return (results["loss"][0][0, 0], g["x"][None], *outs[0], *outs[1], *outs[2], *outs[3])
```

```python
import functools

import jax
import jax.numpy as jnp
from jax import lax
from jax.experimental import pallas as pl
from jax.experimental.pallas import tpu as pltpu

F32 = jnp.float32
BF16 = jnp.bfloat16
MXU_DTYPE = BF16
GRAD_WIRE_DTYPE = BF16

D_MODEL = 1024
N_META = 16
PAD_ROWS = 112
D_CONV = 512
DN_HEADS = 4
DN_HEAD_DIM = 128
DN_DIM = DN_HEADS * DN_HEAD_DIM
DN_CHUNK = 64
SEG = 512
N_SEG = 7
SWA_HEADS = 16
SWA_KV_HEADS = 4
SWA_GROUP = SWA_HEADS // SWA_KV_HEADS
SWA_HEAD_DIM = 64
SWA_BLOCK = 128
D_FF = 2816
EPS = 1e-6
NEG = -1e30
N_DEV = 8

ADAM_LR = 0.001
ADAM_B1 = 0.9
ADAM_B2 = 0.999
ADAM_EPS = 1e-08
ADAM_WD = 0.01
ADAM_STEP = 10

VMEM_LIMIT_BYTES = 52 * 1024 * 1024
SUBLANES = 8
LANES = 128


def _pick(n, prefs):
    for p in prefs:
        if n % p == 0:
            return p
    return n


def _params(sem, vmem=VMEM_LIMIT_BYTES):
    return pltpu.CompilerParams(dimension_semantics=sem, vmem_limit_bytes=vmem)


def _rms(x, w):
    return x * lax.rsqrt(jnp.mean(x * x, -1, keepdims=True) + EPS) * w


def _norm_matmul(h, nw, w, *, o_seg=None, name):
    m, k = h.shape
    n = w.shape[1]
    tm = _pick(m, (1408, 384, 128))
    tn = _pick(o_seg or n, (1408, 1024, 512, 256, 128))

    def body(h_ref, nw_ref, w_ref, o_ref, hnt_ref, hn_s):
        @pl.when(pl.program_id(1) == 0)
        def _():
            hn = _rms(h_ref[...], nw_ref[...])
            hn_s[...] = hn.astype(MXU_DTYPE)
            hnt_ref[...] = hn.T.astype(MXU_DTYPE)

        o_ref[...] = jnp.dot(hn_s[...], w_ref[...], preferred_element_type=F32)

    if o_seg:
        per = o_seg // tn
        o_shape = jax.ShapeDtypeStruct((n // o_seg, m, o_seg), F32)
        o_spec = pl.BlockSpec((None, tm, tn), lambda i, j: (j // per, i, j % per))
    else:
        o_shape = jax.ShapeDtypeStruct((m, n), F32)
        o_spec = pl.BlockSpec((tm, tn), lambda i, j: (i, j))
    return pl.pallas_call(
        body, name=name, grid=(m // tm, n // tn),
        in_specs=[pl.BlockSpec((tm, k), lambda i, j: (i, 0)), pl.BlockSpec((1, k), lambda i, j: (0, 0)),
                  pl.BlockSpec((k, tn), lambda i, j: (0, j))],
        out_specs=[o_spec, pl.BlockSpec((k, tm), lambda i, j: (0, i))],
        out_shape=[o_shape, jax.ShapeDtypeStruct((k, m), MXU_DTYPE)],
        scratch_shapes=[pltpu.VMEM((tm, k), MXU_DTYPE)],
        compiler_params=_params(("parallel", "arbitrary")),
    )(h, nw, w)


TILE_BUDGET_BYTES = 38 * 1024 * 1024
TILE_SIZES = (4224, 2816, 1792, 1536, 1408, 1024, 512, 256, 128)


def _divisor_tiles(n):
    return [t for t in TILE_SIZES if n % t == 0] or [n]


def _mm_nn(a, w, *, res=None, a_seg=False, trans_w=False, w_seg=False, out_dtype=F32, name):
    if a_seg:
        s, m, seg = a.shape
    else:
        m, seg = a.shape
        s = 1
    k = s * seg
    n = w.shape[0] * w.shape[2] if w_seg else (w.shape[0] if trans_w else w.shape[1])
    n_seg = w.shape[2] if w_seg else n
    tm = _pick(m, (1408, 1024, 512, 384, 256, 128))
    ab = a.dtype.itemsize
    k_steps = [(sb, seg) for sb in range(s, 0, -1) if s % sb == 0] if a_seg else [(1, t) for t in _divisor_tiles(seg)]
    best = None
    for tn in _divisor_tiles(n_seg):
        for sb, tk1 in k_steps:
            tk = sb * tk1
            nk = k // tk
            need = 2 * tm * tk * ab + 2 * tk * tn * 2 + 2 * tm * tn * 4 + (tm * tn * 4 if nk > 1 else 0) + (2 * tm * tn * 4 if res is not None else 0)
            if need <= TILE_BUDGET_BYTES and (best is None or tk * tn > best[0]):
                best = (tk * tn, tn, sb, tk1)
    _, tn, sb, tk1 = best
    tk = sb * tk1
    nk = k // tk
    w_dims = _DOT_DIMS["nt" if trans_w else "nn"]

    def body(*refs):
        a_ref, w_ref = refs[:2]
        r_ref = refs[2] if res is not None else None
        o_ref = refs[3 if res is not None else 2]

        def partial_product():
            if not a_seg:
                return lax.dot_general(a_ref[...].astype(MXU_DTYPE), w_ref[...], w_dims, preferred_element_type=F32)
            out = None
            for t in range(sb):
                wt = w_ref[:, t * seg:(t + 1) * seg] if trans_w else w_ref[t * seg:(t + 1) * seg, :]
                d = lax.dot_general(a_ref[t].astype(MXU_DTYPE), wt, w_dims, preferred_element_type=F32)
                out = d if out is None else out + d
            return out

        if nk == 1:
            o_ref[...] = (partial_product() if res is None else partial_product() + r_ref[...]).astype(o_ref.dtype)
            return
        acc = refs[-1]
        kk = pl.program_id(2)

        @pl.when(kk == 0)
        def _():
            acc[...] = jnp.zeros_like(acc)

        acc[...] += partial_product()

        @pl.when(kk == nk - 1)
        def _():
            o_ref[...] = (acc[...] if res is None else acc[...] + r_ref[...]).astype(o_ref.dtype)

    a_spec = pl.BlockSpec((sb, tm, seg), lambda i, j, kk: (kk, i, 0)) if a_seg else pl.BlockSpec((tm, tk), lambda i, j, kk: (i, kk))
    if w_seg:
        per = n_seg // tn
        w_spec = pl.BlockSpec((None, tk, tn), lambda i, j, kk: (j // per, kk, j % per))
    elif trans_w:
        w_spec = pl.BlockSpec((tn, tk), lambda i, j, kk: (j, kk))
    else:
        w_spec = pl.BlockSpec((tk, tn), lambda i, j, kk: (kk, j))
    in_specs = [a_spec, w_spec]
    args = [a, w]
    if res is not None:
        in_specs.append(pl.BlockSpec((tm, tn), lambda i, j, kk: (i, j)))
        args.append(res)
    return pl.pallas_call(
        body, name=name, grid=(m // tm, n // tn, nk), in_specs=in_specs,
        out_specs=pl.BlockSpec((tm, tn), lambda i, j, kk: (i, j)),
        out_shape=jax.ShapeDtypeStruct((m, n), out_dtype),
        scratch_shapes=[pltpu.VMEM((tm, tn), F32)] if nk > 1 else [],
        compiler_params=_params(("parallel", "parallel", "arbitrary")),
    )(*args)


def _mm_tn(a, b, *, b_seg=False, out_dtype=None, name):
    out_dtype = out_dtype or GRAD_WIRE_DTYPE
    m, ka = a.shape
    if b_seg:
        s, _, seg = b.shape
        n = s * seg
    else:
        n = b.shape[1]
        seg = n
    tmc = _pick(m, (1408, 384, 128))
    best = None
    for tka in _divisor_tiles(ka):
        for tn in _divisor_tiles(seg):
            need = 2 * tmc * tka * a.dtype.itemsize + 2 * tmc * tn * b.dtype.itemsize + tka * tn * 4 + 2 * tka * tn * 4
            if need <= TILE_BUDGET_BYTES and (best is None or (tka * tn, tn) > best[:2]):
                best = (tka * tn, tn, tka)
    _, tn, tka = best
    nm = m // tmc

    def body(a_ref, b_ref, o_ref, acc):
        mm = pl.program_id(2)

        @pl.when(mm == 0)
        def _():
            acc[...] = jnp.zeros_like(acc)

        acc[...] += lax.dot_general(a_ref[...].astype(MXU_DTYPE), b_ref[...].astype(MXU_DTYPE),
                                    (((0,), (0,)), ((), ())), preferred_element_type=F32)

        @pl.when(mm == nm - 1)
        def _():
            o_ref[...] = acc[...].astype(o_ref.dtype)

    if b_seg:
        per = seg // tn
        b_spec = pl.BlockSpec((None, tmc, tn), lambda i, j, mm: (j // per, mm, j % per))
    else:
        b_spec = pl.BlockSpec((tmc, tn), lambda i, j, mm: (mm, j))
    return pl.pallas_call(
        body, name=name, grid=(ka // tka, n // tn, nm),
        in_specs=[pl.BlockSpec((tmc, tka), lambda i, j, mm: (mm, i)), b_spec],
        out_specs=pl.BlockSpec((tka, tn), lambda i, j, mm: (i, j)),
        out_shape=jax.ShapeDtypeStruct((ka, n), out_dtype),
        scratch_shapes=[pltpu.VMEM((tka, tn), F32)],
        compiler_params=_params(("parallel", "parallel", "arbitrary")),
    )(a, b)


def _rmsnorm_bwd(dhn, h, nw, dres, *, name):
    m, d = h.shape
    tm = _pick(m, (384, 128))

    def body(dhn_ref, h_ref, nw_ref, dres_ref, dh_ref, dnw_ref):
        i = pl.program_id(0)
        x = h_ref[...]
        r = lax.rsqrt(jnp.mean(x * x, -1, keepdims=True) + EPS)
        xh = x * r
        dy = dhn_ref[...]
        dxh = dy * nw_ref[...]
        dx = r * (dxh - xh * jnp.mean(dxh * xh, -1, keepdims=True))
        row = i * tm + lax.broadcasted_iota(jnp.int32, (tm, 1), 0)
        dh_ref[...] = jnp.where(row >= PAD_ROWS, dres_ref[...] + dx, 0.0)

        @pl.when(i == 0)
        def _():
            dnw_ref[...] = jnp.zeros_like(dnw_ref)

        dnw_ref[...] += jnp.sum(dy * xh, 0, keepdims=True)

    return pl.pallas_call(
        body, name=name, grid=(m // tm,),
        in_specs=[pl.BlockSpec((tm, d), lambda i: (i, 0)), pl.BlockSpec((tm, d), lambda i: (i, 0)),
                  pl.BlockSpec((1, d), lambda i: (0, 0)), pl.BlockSpec((tm, d), lambda i: (i, 0))],
        out_specs=[pl.BlockSpec((tm, d), lambda i: (i, 0)), pl.BlockSpec((1, d), lambda i: (0, 0))],
        out_shape=[jax.ShapeDtypeStruct((m, d), F32), jax.ShapeDtypeStruct((1, d), F32)],
        compiler_params=_params(("arbitrary",)),
    )(dhn, h, nw, dres)


ROW_CHUNK = 248


def _row_chunks(m):
    out, s = [], SUBLANES
    while s < m:
        n = min(ROW_CHUNK, m - s)
        out.append((s, n))
        s += n
    return out


def _conv_at(load, w, width, s, n):
    acc = w[width - 1:width, :] * load(s, n)
    for j in range(width - 1):
        acc = acc + w[j:j + 1, :] * load(s - (width - 1 - j), n)
    return acc


def _conv_t_at(load, w, width, s, n):
    acc = w[width - 1:width, :] * load(s, n)
    for j in range(width - 1):
        acc = acc + w[j:j + 1, :] * load(s + (width - 1 - j), n)
    return acc


def _dconv_w(load_x, d, width, s, n):
    rows = [jnp.sum(d * load_x(s - (width - 1 - j), n), 0, keepdims=True) for j in range(width)]
    rows.append(jnp.zeros((SUBLANES - width, d.shape[1]), F32))
    return jnp.concatenate(rows, 0)


def _pad_w(w):
    return jnp.concatenate([w, jnp.zeros((SUBLANES - w.shape[0], w.shape[1]), w.dtype)], 0)


def _sigmoid(x):
    return 1.0 / (1.0 + jnp.exp(-x))


def _ffn_act_fwd(u, cw, *, name):
    _, m, f = u.shape
    cb = _pick(f, (256, 128))
    chunks = _row_chunks(m)

    def body(g_ref, v_ref, w_ref, o_ref):
        w = w_ref[...]
        o_ref[pl.ds(0, SUBLANES), :] = jnp.zeros((SUBLANES, cb), o_ref.dtype)
        for s, n in chunks:
            c = _conv_at(lambda a, b: g_ref[pl.ds(a, b), :], w, 3, s, n)
            o_ref[pl.ds(s, n), :] = (c * _sigmoid(c) * v_ref[pl.ds(s, n), :]).astype(o_ref.dtype)

    return pl.pallas_call(
        body, name=name, grid=(f // cb,),
        in_specs=[pl.BlockSpec((None, m, cb), lambda j: (0, 0, j)), pl.BlockSpec((None, m, cb), lambda j: (1, 0, j)),
                  pl.BlockSpec((SUBLANES, cb), lambda j: (0, j))],
        out_specs=pl.BlockSpec((m, cb), lambda j: (0, j)),
        out_shape=jax.ShapeDtypeStruct((m, f), MXU_DTYPE),
        compiler_params=_params(("parallel",)),
    )(u, u, cw)


def _ffn_act_bwd(da, u, cw, *, name):
    _, m, f = u.shape
    cb = LANES
    chunks = _row_chunks(m)

    def body(da_ref, g_ref, v_ref, w_ref, du_ref, dw_ref, dg_s):
        w = w_ref[...]
        zeros8 = jnp.zeros((SUBLANES, cb), F32)
        dg_s[pl.ds(0, SUBLANES), :] = zeros8
        dg_s[pl.ds(m, SUBLANES), :] = zeros8
        du_ref[0, pl.ds(0, SUBLANES), :] = zeros8.astype(du_ref.dtype)
        du_ref[1, pl.ds(0, SUBLANES), :] = zeros8.astype(du_ref.dtype)
        load_g = lambda a, b: g_ref[pl.ds(a, b), :]
        dw = jnp.zeros((SUBLANES, cb), F32)
        for s, n in chunks:
            c = _conv_at(load_g, w, 3, s, n)
            sg = _sigmoid(c)
            d = da_ref[pl.ds(s, n), :]
            du_ref[1, pl.ds(s, n), :] = (d * (c * sg)).astype(du_ref.dtype)
            dc = d * v_ref[pl.ds(s, n), :] * (sg * (1.0 + c * (1.0 - sg)))
            dg_s[pl.ds(s, n), :] = dc
            dw = dw + _dconv_w(load_g, dc, 3, s, n)
        dw_ref[...] = dw
        for s, n in chunks:
            du_ref[0, pl.ds(s, n), :] = _conv_t_at(lambda a, b: dg_s[pl.ds(a, b), :], w, 3, s, n).astype(du_ref.dtype)

    return pl.pallas_call(
        body, name=name, grid=(f // cb,),
        in_specs=[pl.BlockSpec((m, cb), lambda j: (0, j)), pl.BlockSpec((None, m, cb), lambda j: (0, 0, j)),
                  pl.BlockSpec((None, m, cb), lambda j: (1, 0, j)), pl.BlockSpec((SUBLANES, cb), lambda j: (0, j))],
        out_specs=[pl.BlockSpec((2, m, cb), lambda j: (0, 0, j)), pl.BlockSpec((SUBLANES, cb), lambda j: (0, j))],
        out_shape=[jax.ShapeDtypeStruct((2, m, f), MXU_DTYPE), jax.ShapeDtypeStruct((SUBLANES, f), F32)],
        scratch_shapes=[pltpu.VMEM((m + SUBLANES, cb), F32)],
        compiler_params=_params(("parallel",)),
    )(da, u, u, cw)


def _shortconv_fwd(pm, cw, y, *, name):
    _, m, seg = pm.shape
    cb = _pick(seg, (256, 128))
    chunks = _row_chunks(m)

    def body(gi_ref, go_ref, ah_ref, w_ref, y_in, o_ref):
        del y_in
        w = w_ref[...]
        o_ref[pl.ds(0, SUBLANES), :] = jnp.zeros((SUBLANES, cb), o_ref.dtype)
        load_m = lambda a, b: gi_ref[pl.ds(a, b), :] * ah_ref[pl.ds(a, b), :]
        for s, n in chunks:
            o_ref[pl.ds(s, n), :] = (go_ref[pl.ds(s, n), :] * _conv_at(load_m, w, 3, s, n)).astype(o_ref.dtype)

    return pl.pallas_call(
        body, name=name, grid=(seg // cb,),
        in_specs=[pl.BlockSpec((None, m, cb), lambda j: (0, 0, j)), pl.BlockSpec((None, m, cb), lambda j: (1, 0, j)),
                  pl.BlockSpec((None, m, cb), lambda j: (2, 0, j)), pl.BlockSpec((SUBLANES, cb), lambda j: (0, j)),
                  pl.BlockSpec(memory_space=pl.ANY)],
        out_specs=pl.BlockSpec((m, cb), lambda j: (0, j)),
        out_shape=jax.ShapeDtypeStruct(y.shape, y.dtype),
        input_output_aliases={4: 0},
        compiler_params=_params(("parallel",)),
    )(pm, pm, pm, cw, y)


def _shortconv_bwd(dy, pm, cw, dpm, *, name):
    _, m, seg = pm.shape
    cb = LANES
    chunks = _row_chunks(m)

    def body(dy_ref, gi_ref, go_ref, ah_ref, w_ref, dpm_in, dp_ref, dw_ref, dc_s):
        del dpm_in
        w = w_ref[...]
        zeros8 = jnp.zeros((SUBLANES, cb), F32)
        dc_s[pl.ds(0, SUBLANES), :] = zeros8
        dc_s[pl.ds(m, SUBLANES), :] = zeros8
        for t in range(3):
            dp_ref[t, pl.ds(0, SUBLANES), :] = zeros8.astype(dp_ref.dtype)
        load_m = lambda a, b: gi_ref[pl.ds(a, b), :] * ah_ref[pl.ds(a, b), :]
        dw = jnp.zeros((SUBLANES, cb), F32)
        for s, n in chunks:
            d = dy_ref[pl.ds(s, n), :]
            dp_ref[1, pl.ds(s, n), :] = (d * _conv_at(load_m, w, 3, s, n)).astype(dp_ref.dtype)
            dc = d * go_ref[pl.ds(s, n), :]
            dc_s[pl.ds(s, n), :] = dc
            dw = dw + _dconv_w(load_m, dc, 3, s, n)
        dw_ref[...] = dw
        for s, n in chunks:
            dm = _conv_t_at(lambda a, b: dc_s[pl.ds(a, b), :], w, 3, s, n)
            dp_ref[0, pl.ds(s, n), :] = (dm * ah_ref[pl.ds(s, n), :]).astype(dp_ref.dtype)
            dp_ref[2, pl.ds(s, n), :] = (dm * gi_ref[pl.ds(s, n), :]).astype(dp_ref.dtype)

    return pl.pallas_call(
        body, name=name, grid=(seg // cb,),
        in_specs=[pl.BlockSpec((m, cb), lambda j: (0, j)), pl.BlockSpec((None, m, cb), lambda j: (0, 0, j)),
                  pl.BlockSpec((None, m, cb), lambda j: (1, 0, j)), pl.BlockSpec((None, m, cb), lambda j: (2, 0, j)),
                  pl.BlockSpec((SUBLANES, cb), lambda j: (0, j)), pl.BlockSpec(memory_space=pl.ANY)],
        out_specs=[pl.BlockSpec((3, m, cb), lambda j: (0, 0, j)), pl.BlockSpec((SUBLANES, cb), lambda j: (0, j))],
        out_shape=[jax.ShapeDtypeStruct(dpm.shape, dpm.dtype), jax.ShapeDtypeStruct((SUBLANES, seg), F32)],
        scratch_shapes=[pltpu.VMEM((m + SUBLANES, cb), F32)],
        input_output_aliases={5: 0},
        compiler_params=_params(("parallel",)),
    )(dy, pm, pm, pm, cw, dpm)


def _dnpre_fwd(pm, cw, *, name):
    _, m, seg = pm.shape
    cb = _pick(seg, (256, 128))
    per = seg // cb
    chunks = _row_chunks(m)

    def body(x_ref, w_ref, o_ref):
        w = w_ref[...]
        o_ref[pl.ds(0, SUBLANES), :] = jnp.zeros((SUBLANES, cb), F32)
        for s, n in chunks:
            c = _conv_at(lambda a, b: x_ref[pl.ds(a, b), :], w, 4, s, n)
            o_ref[pl.ds(s, n), :] = c * _sigmoid(c)

    return pl.pallas_call(
        body, name=name, grid=(3 * per,),
        in_specs=[pl.BlockSpec((None, m, cb), lambda j: (3 + j // per, 0, j % per)), pl.BlockSpec((SUBLANES, cb), lambda j: (0, j))],
        out_specs=pl.BlockSpec((None, m, cb), lambda j: (j // per, 0, j % per)),
        out_shape=jax.ShapeDtypeStruct((3, m, seg), F32),
        compiler_params=_params(("parallel",)),
    )(pm, cw)


def _dnpre_bwd(dqkv, pm, cw, dpm, *, name):
    _, m, seg = pm.shape
    cb = _pick(seg, (256, 128))
    per = seg // cb
    chunks = _row_chunks(m)

    def body(d_ref, x_ref, w_ref, dpm_in, dp_ref, dw_ref, dc_s):
        del dpm_in
        w = w_ref[...]
        zeros8 = jnp.zeros((SUBLANES, cb), F32)
        dc_s[pl.ds(0, SUBLANES), :] = zeros8
        dc_s[pl.ds(m, SUBLANES), :] = zeros8
        dp_ref[pl.ds(0, SUBLANES), :] = zeros8.astype(dp_ref.dtype)
        load_x = lambda a, b: x_ref[pl.ds(a, b), :]
        dw = jnp.zeros((SUBLANES, cb), F32)
        for s, n in chunks:
            c = _conv_at(load_x, w, 4, s, n)
            sg = _sigmoid(c)
            dc = d_ref[pl.ds(s, n), :] * (sg * (1.0 + c * (1.0 - sg)))
            dc_s[pl.ds(s, n), :] = dc
            dw = dw + _dconv_w(load_x, dc, 4, s, n)
        dw_ref[...] = dw
        for s, n in chunks:
            dp_ref[pl.ds(s, n), :] = _conv_t_at(lambda a, b: dc_s[pl.ds(a, b), :], w, 4, s, n).astype(dp_ref.dtype)

    return pl.pallas_call(
        body, name=name, grid=(3 * per,),
        in_specs=[pl.BlockSpec((None, m, cb), lambda j: (j // per, 0, j % per)),
                  pl.BlockSpec((None, m, cb), lambda j: (3 + j // per, 0, j % per)),
                  pl.BlockSpec((SUBLANES, cb), lambda j: (0, j)), pl.BlockSpec(memory_space=pl.ANY)],
        out_specs=[pl.BlockSpec((None, m, cb), lambda j: (3 + j // per, 0, j % per)), pl.BlockSpec((SUBLANES, cb), lambda j: (0, j))],
        out_shape=[jax.ShapeDtypeStruct(dpm.shape, dpm.dtype), jax.ShapeDtypeStruct((SUBLANES, 3 * seg), F32)],
        scratch_shapes=[pltpu.VMEM((m + SUBLANES, cb), F32)],
        input_output_aliases={3: 0},
        compiler_params=_params(("parallel",)),
    )(dqkv, pm, cw, dpm)


def _mxu_dot_impl(a, b, form):
    a = a.astype(MXU_DTYPE)
    b = b.astype(MXU_DTYPE)
    dims = {"nn": (((1,), (0,)), ((), ())), "nt": (((1,), (1,)), ((), ())), "tn": (((0,), (0,)), ((), ()))}[form]
    return lax.dot_general(a, b, dims, preferred_element_type=F32)


@functools.partial(jax.custom_vjp, nondiff_argnums=(2,))
def _mxu_dot(a, b, form):
    return _mxu_dot_impl(a, b, form)


def _mxu_dot_fwd(a, b, form):
    return _mxu_dot_impl(a, b, form), (a, b)


def _mxu_dot_bwd(form, saved, g):
    a, b = saved
    if form == "nn":
        return _mxu_dot_impl(g, b, "nt"), _mxu_dot_impl(a, g, "tn")
    if form == "nt":
        return _mxu_dot_impl(g, b, "nn"), _mxu_dot_impl(g, a, "tn")
    return _mxu_dot_impl(b, g, "nt"), _mxu_dot_impl(a, g, "nn")


_mxu_dot.defvjp(_mxu_dot_fwd, _mxu_dot_bwd)


_DOT_DIMS = {"nn": (((1,), (0,)), ((), ())), "nt": (((1,), (1,)), ((), ())), "tn": (((0,), (0,)), ((), ()))}


def _split(x):
    hi = x.astype(BF16)
    return hi, (x - hi.astype(F32)).astype(BF16)


def _dot3_impl(a, b, form):
    dg = lambda p, q: lax.dot_general(p, q, _DOT_DIMS[form], preferred_element_type=F32)
    ah, al = _split(a)
    bh, bl = _split(b)
    return dg(ah, bh) + (dg(ah, bl) + dg(al, bh))


@functools.partial(jax.custom_vjp, nondiff_argnums=(2,))
def _dot3(a, b, form):
    return _dot3_impl(a, b, form)


def _dot3_fwd(a, b, form):
    return _dot3_impl(a, b, form), (a, b)


def _dot3_bwd(form, saved, g):
    a, b = saved
    if form == "nn":
        return _dot3_impl(g, b, "nt"), _dot3_impl(a, g, "tn")
    if form == "nt":
        return _dot3_impl(g, b, "nn"), _dot3_impl(g, a, "tn")
    return _dot3_impl(b, g, "nt"), _dot3_impl(a, g, "nn")


_dot3.defvjp(_dot3_fwd, _dot3_bwd)


def _hdot(a, b):
    return _dot3(a, b, "nn")


def _mask_dot(mask, x, form):
    dg = lambda q: lax.dot_general(mask.astype(BF16), q, _DOT_DIMS[form], preferred_element_type=F32)
    x1 = x.astype(BF16)
    r1 = x - x1.astype(F32)
    x2 = r1.astype(BF16)
    x3 = (r1 - x2.astype(F32)).astype(BF16)
    return dg(x1) + (dg(x2) + dg(x3))


def _decay_masks(c):
    row = lax.broadcasted_iota(jnp.int32, (c, c), 0)
    col = lax.broadcasted_iota(jnp.int32, (c, c), 1)
    return (row >= col).astype(F32), row <= col


def _decay_impl(gb):
    lower, upper = _decay_masks(gb.shape[0])
    return _mask_dot(lower, gb, "nn"), _mask_dot(jnp.ones_like(gb), jnp.where(upper, gb, 0.0), "nn")


@jax.custom_vjp
def _decay_matrices(gb):
    return _decay_impl(gb)


def _decay_fwd(gb):
    return _decay_impl(gb), None


def _decay_bwd(_, cts):
    gc, gr = cts
    lower, upper = _decay_masks(gc.shape[0])
    return (_mask_dot(lower, gc, "tn") + jnp.where(upper, _mask_dot(jnp.ones_like(gr), gr, "tn"), 0.0),)


_decay_matrices.defvjp(_decay_fwd, _decay_bwd)


def _softplus(x):
    return jnp.maximum(x, 0.0) + jnp.log(1.0 + jnp.exp(-jnp.abs(x)))


def _heads(f, *lists):
    return [f(*t) for t in zip(*lists)]


DN_STEP_CHUNKS = 3


def _dn_chunk(qr, kr, v, z, braw, araw, alog, dtb, nw, state, valid):
    c = DN_CHUNK
    nh = len(state)
    chunks = len(qr) // nh
    alog, dtb, valid_i = alog * chunks, dtb * chunks, [vv for vv in valid for _ in range(nh)]
    row = lax.broadcasted_iota(jnp.int32, (c, c), 0)
    col = lax.broadcasted_iota(jnp.int32, (c, c), 1)
    incl = row >= col
    strict = row > col
    eye = jnp.where(row == col, 1.0, 0.0)
    q = _heads(lambda t: t * lax.rsqrt(jnp.sum(t * t, -1, keepdims=True) + EPS) * (DN_HEAD_DIM ** -0.5), qr)
    k = _heads(lambda t: t * lax.rsqrt(jnp.sum(t * t, -1, keepdims=True) + EPS), kr)
    beta = _heads(lambda t, vv: _sigmoid(t) * vv, braw, valid_i)
    g = _heads(lambda al, ar, dt, vv: -jnp.exp(al) * _softplus(ar + dt) * vv, alog, araw, dtb, valid_i)
    decay = _heads(lambda t: _decay_matrices(jnp.broadcast_to(t, (c, c))), g)
    dmask = _heads(lambda d: jnp.where(incl, jnp.exp(jnp.where(incl, d[0] - d[1], 0.0)), 0.0), decay)
    dec = _heads(lambda d: d[0][:, :1], decay)
    dlast = _heads(lambda d: d[0][c - 1:c, :1], decay)
    kk = _heads(lambda t: _mxu_dot(t, t, "nt"), k)
    a = _heads(lambda b, t, d: jnp.where(strict, b * t * d, 0.0), beta, kk, dmask)
    x = _heads(lambda t: eye - t, a)
    p = _heads(_hdot, a, a)
    for it in range(5):
        x = _heads(lambda s, t: s + _hdot(s, t), x, p)
        if it < 4:
            p = _heads(_hdot, p, p)
    u = _heads(lambda s, t, b: _hdot(s, t * b), x, v, beta)
    w = _heads(lambda s, t, b, d: _hdot(s, t * (b * jnp.exp(d))), x, k, beta, dec)
    qk = _heads(lambda s, t, d: _mxu_dot(s, t, "nt") * d, q, k, dmask)
    q_dec = _heads(lambda t, d: t * jnp.exp(d), q, dec)
    k_dec = _heads(lambda t, dl, d: t * jnp.exp(dl - d), k, dlast, dec)
    o = []
    for ci in range(chunks):
        of = lambda lst: lst[ci * nh:(ci + 1) * nh]
        v_new = _heads(lambda s, t, st: s - _mxu_dot(t, st, "nn"), of(u), of(w), state)
        o += _heads(lambda qd, st, s, vn: _mxu_dot(qd, st, "nn") + _mxu_dot(s, vn, "nn"), of(q_dec), state, of(qk), v_new)
        state = _heads(lambda st, dl, kd, vn: st * jnp.exp(dl) + _mxu_dot(kd, vn, "tn"), state, of(dlast), of(k_dec), v_new)
    y = _heads(lambda t, zz: _rms(t, nw) * (zz * _sigmoid(zz)), o, z)
    return y, state


DN_STEP_ROWS = DN_STEP_CHUNKS * DN_CHUNK
DN_ITEMS = [(ci, h) for ci in range(DN_STEP_CHUNKS) for h in range(DN_HEADS)]


def _dn_valid(n):
    rows = [n * DN_STEP_ROWS + ci * DN_CHUNK + lax.broadcasted_iota(jnp.int32, (DN_CHUNK, 1), 0) for ci in range(DN_STEP_CHUNKS)]
    return [(r >= PAD_ROWS).astype(F32) for r in rows]


def _dn_in_specs(rev, nc):
    cn = (lambda n: nc - 1 - n) if rev else (lambda n: n)
    c, hd = DN_STEP_ROWS, DN_HEAD_DIM
    return [
        pl.BlockSpec((None, c, DN_DIM), lambda n: (0, cn(n), 0)),
        pl.BlockSpec((None, c, DN_DIM), lambda n: (1, cn(n), 0)),
        pl.BlockSpec((None, c, DN_DIM), lambda n: (2, cn(n), 0)),
        pl.BlockSpec((None, c, DN_DIM), lambda n: (6, cn(n), 0)),
        pl.BlockSpec((DN_HEADS, 2, c, 1), lambda n: (0, 0, cn(n), 0)),
        pl.BlockSpec((DN_HEADS, SUBLANES, LANES), lambda n: (0, 0, 0)),
        pl.BlockSpec((1, hd), lambda n: (0, 0)),
    ]


def _rows(ci):
    return slice(ci * DN_CHUNK, (ci + 1) * DN_CHUNK)


def _cols(h):
    return slice(h * DN_HEAD_DIM, (h + 1) * DN_HEAD_DIM)


def _dn_load(q_ref, k_ref, v_ref, z_ref, ba_ref, hp_ref):
    heads = range(DN_HEADS)
    item = lambda ref: [ref[_rows(ci), _cols(h)] for ci, h in DN_ITEMS]
    return (item(q_ref), item(k_ref), item(v_ref), item(z_ref),
            [ba_ref[h, 0, _rows(ci), :] for ci, h in DN_ITEMS], [ba_ref[h, 1, _rows(ci), :] for ci, h in DN_ITEMS],
            [hp_ref[h, 0:1, 0:1] for h in heads], [hp_ref[h, 1:2, 0:1] for h in heads])


def _delta_fwd(qkvc, pm, ba, hp, nw, *, name):
    _, m, _ = qkvc.shape
    nc = m // DN_STEP_ROWS
    hd = DN_HEAD_DIM

    def body(q_ref, k_ref, v_ref, z_ref, ba_ref, hp_ref, nw_ref, y_ref, s_ref, state):
        n = pl.program_id(0)

        @pl.when(n == 0)
        def _():
            state[...] = jnp.zeros_like(state)

        heads = range(DN_HEADS)
        old = [state[h] for h in heads]
        y, new = _dn_chunk(*_dn_load(q_ref, k_ref, v_ref, z_ref, ba_ref, hp_ref), nw_ref[...], old, _dn_valid(n))
        for h in heads:
            s_ref[h] = old[h]
            state[h] = new[h]
        for (ci, h), yy in zip(DN_ITEMS, y):
            y_ref[_rows(ci), _cols(h)] = yy.astype(y_ref.dtype)

    return pl.pallas_call(
        body, name=name, grid=(nc,), in_specs=_dn_in_specs(False, nc),
        out_specs=[pl.BlockSpec((DN_STEP_ROWS, DN_DIM), lambda n: (n, 1)), pl.BlockSpec((DN_HEADS, None, hd, hd), lambda n: (0, n, 0, 0))],
        out_shape=[jax.ShapeDtypeStruct((m, D_CONV + DN_DIM), MXU_DTYPE), jax.ShapeDtypeStruct((DN_HEADS, nc, hd, hd), F32)],
        scratch_shapes=[pltpu.VMEM((DN_HEADS, hd, hd), F32)],
        compiler_params=_params(("arbitrary",)),
    )(qkvc, qkvc, qkvc, pm, ba, hp, nw)


def _delta_bwd(dy, qkvc, pm, ba, hp, nw, states, *, name):
    _, m, _ = qkvc.shape
    nc = m // DN_STEP_ROWS
    hd, c = DN_HEAD_DIM, DN_STEP_ROWS

    def body(q_ref, k_ref, v_ref, z_ref, ba_ref, hp_ref, nw_ref, s_ref, dy_ref,
             dz_ref, dqkv_ref, dba_ref, dhp_ref, dnw_ref, dstate):
        step = pl.program_id(0)
        n = nc - 1 - step

        @pl.when(step == 0)
        def _():
            dstate[...] = jnp.zeros_like(dstate)
            dhp_ref[...] = jnp.zeros_like(dhp_ref)
            dnw_ref[...] = jnp.zeros_like(dnw_ref)

        valid = _dn_valid(n)
        heads = range(DN_HEADS)
        fn = lambda *a: _dn_chunk(*a, valid)
        _, vjp = jax.vjp(fn, *_dn_load(q_ref, k_ref, v_ref, z_ref, ba_ref, hp_ref), nw_ref[...], [s_ref[h] for h in heads])
        dy = [dy_ref[_rows(ci), _cols(h)] for ci, h in DN_ITEMS]
        dq, dk, dv, dz, dbr, dar, dalog, ddtb, dnw, dst = vjp((dy, [dstate[h] for h in heads]))
        for i, (ci, h) in enumerate(DN_ITEMS):
            dqkv_ref[0, _rows(ci), _cols(h)] = dq[i]
            dqkv_ref[1, _rows(ci), _cols(h)] = dk[i]
            dqkv_ref[2, _rows(ci), _cols(h)] = dv[i]
            dz_ref[_rows(ci), _cols(h)] = dz[i].astype(dz_ref.dtype)
            dba_ref[h, 0, _rows(ci), :] = dbr[i]
            dba_ref[h, 1, _rows(ci), :] = dar[i]
        for h in heads:
            dstate[h] = dst[h]
            dhp_ref[h] += jnp.concatenate([jnp.broadcast_to(dalog[h], (1, LANES)), jnp.broadcast_to(ddtb[h], (1, LANES)),
                                           jnp.zeros((SUBLANES - 2, LANES), F32)], 0)
        dnw_ref[...] += dnw

    rn = lambda n: nc - 1 - n
    in_specs = _dn_in_specs(True, nc) + [
        pl.BlockSpec((DN_HEADS, None, hd, hd), lambda n: (0, rn(n), 0, 0)),
        pl.BlockSpec((c, DN_DIM), lambda n: (rn(n), 1)),
    ]
    out_specs = [
        pl.BlockSpec((None, c, DN_DIM), lambda n: (6, rn(n), 0)),
        pl.BlockSpec((3, c, DN_DIM), lambda n: (0, rn(n), 0)),
        pl.BlockSpec((DN_HEADS, 2, c, 1), lambda n: (0, 0, rn(n), 0)),
        pl.BlockSpec((DN_HEADS, SUBLANES, LANES), lambda n: (0, 0, 0)),
        pl.BlockSpec((1, hd), lambda n: (0, 0)),
    ]
    return pl.pallas_call(
        body, name=name, grid=(nc,), in_specs=in_specs, out_specs=out_specs,
        out_shape=[jax.ShapeDtypeStruct(pm.shape, MXU_DTYPE), jax.ShapeDtypeStruct(qkvc.shape, F32),
                   jax.ShapeDtypeStruct(ba.shape, F32), jax.ShapeDtypeStruct(hp.shape, F32),
                   jax.ShapeDtypeStruct((1, hd), F32)],
        scratch_shapes=[pltpu.VMEM((DN_HEADS, hd, hd), F32)],
        compiler_params=_params(("arbitrary",)),
    )(qkvc, qkvc, qkvc, pm, ba, hp, nw, states, dy)


SWA_PAIR = 2


def _attn_block(q, k0, kp, kc, v0, vp, vc, qw, kw, sink, n):
    g, b, hd = SWA_GROUP, SWA_BLOCK, SWA_HEAD_DIM
    pair = list(range(SWA_PAIR))
    lanes = lambda t, e: t[:, e * hd:(e + 1) * hd]
    q4 = [jnp.concatenate([lanes(q, e * g + i)[None] for i in range(g)], 0) for e in pair]
    qn = _heads(lambda t: _rms(t, qw) * (hd ** -0.5), q4)
    kn = [_rms(jnp.concatenate([lanes(k0, e), lanes(kp, e), lanes(kc, e)], 0), kw) for e in pair]
    vcat = [jnp.concatenate([lanes(v0, e), lanes(vp, e), lanes(vc, e)], 0) for e in pair]
    s = _heads(lambda a, k: _mxu_dot(a.reshape(g * b, hd), k, "nt").reshape(g, b, 3 * b), qn, kn)
    i = lax.broadcasted_iota(jnp.int32, (b, 3 * b), 0)
    c = lax.broadcasted_iota(jnp.int32, (b, 3 * b), 1)
    in_meta, in_prev, in_cur = c < b, (c >= b) & (c < 2 * b), c >= 2 * b
    j = c - jnp.where(in_meta, 0, jnp.where(in_prev, b, 2 * b))
    meta_lo = jnp.where(n == 0, b, PAD_ROWS)
    cur_lo = jnp.where(n == 0, PAD_ROWS, 0)
    prev_off = jnp.where(n >= 2, 0, 2 * b)
    valid = (in_meta & (j >= meta_lo)) | (in_prev & (j > i + prev_off)) | (in_cur & (j <= i) & (j >= cur_lo))
    s = _heads(lambda t: jnp.where(valid[None], t, NEG), s)
    m = [lax.stop_gradient(jnp.maximum(jnp.max(t, -1, keepdims=True), sink[e])) for e, t in zip(pair, s)]
    ex = _heads(lambda t, mm: jnp.exp(t - mm), s, m)
    p = [t / (jnp.sum(t, -1, keepdims=True) + jnp.exp(sink[e] - mm)) for e, t, mm in zip(pair, ex, m)]
    o = _heads(lambda t, v: _mxu_dot(t.reshape(g * b, 3 * b), v, "nn").reshape(g, b, hd), p, vcat)
    return jnp.concatenate([o[e][i] for e in pair for i in range(g)], 1)


Q_LANES = SWA_PAIR * SWA_GROUP * SWA_HEAD_DIM
KV_LANES = SWA_PAIR * SWA_HEAD_DIM
K_BLOCK0 = SWA_HEADS * SWA_HEAD_DIM // KV_LANES
V_BLOCK0 = K_BLOCK0 + SWA_KV_HEADS * SWA_HEAD_DIM // KV_LANES


def _attn_in_specs():
    g, b, hd = SWA_GROUP, SWA_BLOCK, SWA_HEAD_DIM
    kv = lambda f, first: pl.BlockSpec((b, KV_LANES), lambda p, n: (f(n), first + p))
    blocks = [lambda n: 0, lambda n: jnp.maximum(n - 1, 0), lambda n: n]
    return ([pl.BlockSpec((b, Q_LANES), lambda p, n: (n, p))] + [kv(f, K_BLOCK0) for f in blocks] + [kv(f, V_BLOCK0) for f in blocks]
            + [pl.BlockSpec((1, hd), lambda p, n: (0, 0)), pl.BlockSpec((1, hd), lambda p, n: (0, 0)),
               pl.BlockSpec((SWA_PAIR, g, 1, 1), lambda p, n: (p, 0, 0, 0))])


def _attn_fwd(qkv, qw, kw, sink, *, name):
    m = qkv.shape[0]
    b = SWA_BLOCK

    def body(q_ref, k0, kp, kc, v0, vp, vc, qw_ref, kw_ref, s_ref, o_ref):
        o_ref[...] = _attn_block(q_ref[...], k0[...], kp[...], kc[...], v0[...], vp[...], vc[...], qw_ref[...], kw_ref[...],
                                 s_ref[...], pl.program_id(1)).astype(o_ref.dtype)

    return pl.pallas_call(
        body, name=name, grid=(SWA_KV_HEADS // SWA_PAIR, m // b), in_specs=_attn_in_specs(),
        out_specs=pl.BlockSpec((b, Q_LANES), lambda p, n: (n, p)),
        out_shape=jax.ShapeDtypeStruct((m, SWA_HEADS * SWA_HEAD_DIM), MXU_DTYPE),
        compiler_params=_params(("parallel", "parallel")),
    )(*([qkv] * 7), qw, kw, sink)


def _attn_bwd(do, qkv, qw, kw, sink, *, name):
    m = qkv.shape[0]
    g, b, hd = SWA_GROUP, SWA_BLOCK, SWA_HEAD_DIM

    def body(q_ref, k0, kp, kc, v0, vp, vc, qw_ref, kw_ref, s_ref, do_ref, dq_ref, dk_ref, dv_ref, dqw_ref, dkw_ref, ds_ref):
        n = pl.program_id(1)

        @pl.when(n == 0)
        def _():
            for r in (dk_ref, dv_ref, dqw_ref, dkw_ref, ds_ref):
                r[...] = jnp.zeros_like(r)

        fn = lambda *a: _attn_block(*a, n)
        _, vjp = jax.vjp(fn, q_ref[...], k0[...], kp[...], kc[...], v0[...], vp[...], vc[...], qw_ref[...], kw_ref[...], s_ref[...])
        dq, dk0, dkp, dkc, dv0, dvp, dvc, dqw, dkw, dsk = vjp(do_ref[...])
        dq_ref[...] = dq
        prev = pl.multiple_of(jnp.maximum(n - 1, 0) * b, b)
        cur = pl.multiple_of(n * b, b)
        for ref, parts in ((dk_ref, (dk0, dkp, dkc)), (dv_ref, (dv0, dvp, dvc))):
            ref[pl.ds(0, b), :] += parts[0]
            ref[pl.ds(prev, b), :] += parts[1]
            ref[pl.ds(cur, b), :] += parts[2]
        dqw_ref[...] += dqw
        dkw_ref[...] += dkw
        ds_ref[...] += dsk

    pairs = SWA_KV_HEADS // SWA_PAIR
    kv_acc = pl.BlockSpec((m, KV_LANES), lambda p, n: (0, p))
    w_acc = pl.BlockSpec((None, 1, hd), lambda p, n: (p, 0, 0))
    kv_shape = jax.ShapeDtypeStruct((m, SWA_KV_HEADS * hd), F32)
    return pl.pallas_call(
        body, name=name, grid=(pairs, m // b),
        in_specs=_attn_in_specs() + [pl.BlockSpec((b, Q_LANES), lambda p, n: (n, p))],
        out_specs=[pl.BlockSpec((b, Q_LANES), lambda p, n: (n, p)), kv_acc, kv_acc, w_acc, w_acc,
                   pl.BlockSpec((SWA_PAIR, g, 1, 1), lambda p, n: (p, 0, 0, 0))],
        out_shape=[jax.ShapeDtypeStruct((m, SWA_HEADS * hd), F32), kv_shape, kv_shape,
                   jax.ShapeDtypeStruct((pairs, 1, hd), F32), jax.ShapeDtypeStruct((pairs, 1, hd), F32),
                   jax.ShapeDtypeStruct(sink.shape, F32)],
        compiler_params=_params(("parallel", "arbitrary")),
    )(*([qkv] * 7), qw, kw, sink, do)


def _loss_bwd(h, target, *, name):
    m, d = h.shape
    b = SWA_BLOCK

    def body(h_ref, t_ref, l_ref, dh_ref):
        i = pl.program_id(0)

        @pl.when(i == 0)
        def _():
            l_ref[...] = jnp.zeros_like(l_ref)
            dh_ref[...] = jnp.zeros_like(dh_ref)

        @pl.when(i > 0)
        def _():
            e = h_ref[...] - t_ref[...]
            dh_ref[...] = e * (1.0 / d)
            l_ref[...] += jnp.sum(jnp.sum(e * e, 0, keepdims=True), 1, keepdims=True) * (0.5 / d)

    return pl.pallas_call(
        body, name=name, grid=(m // b,),
        in_specs=[pl.BlockSpec((b, d), lambda i: (i, 0)), pl.BlockSpec((b, d), lambda i: (jnp.maximum(i - 1, 0), 0))],
        out_specs=[pl.BlockSpec((1, LANES), lambda i: (0, 0)), pl.BlockSpec((b, d), lambda i: (i, 0))],
        out_shape=[jax.ShapeDtypeStruct((1, LANES), F32), jax.ShapeDtypeStruct((m, d), F32)],
        compiler_params=_params(("arbitrary",)),
    )(h, target)


def _ffn_fwd(h, nw, w_up, cw, w_down, tag):
    u, hn = _norm_matmul(h, nw, w_up, o_seg=D_FF, name=f"ffn_up_{tag}")
    a = _ffn_act_fwd(u, cw, name=f"ffn_act_{tag}")
    return _mm_nn(a, w_down, res=h, name=f"ffn_down_{tag}"), (h, hn, u, a)


def _ffn_bwd(dh, saved, nw, w_up, cw, w_down, tag):
    h, hn, u, a = saved
    da = _mm_nn(dh, w_down, trans_w=True, name=f"ffn_da_{tag}")
    dw_down = _mm_tn(a, dh, name=f"ffn_dwdown_{tag}")
    du, dcw = _ffn_act_bwd(da, u, cw, name=f"ffn_act_bwd_{tag}")
    dhn = _mm_nn(du, w_up, a_seg=True, trans_w=True, name=f"ffn_dhn_{tag}")
    dw_up = _mm_nn(hn, du, w_seg=True, out_dtype=GRAD_WIRE_DTYPE, name=f"ffn_dwup_{tag}")
    dh_in, dnw = _rmsnorm_bwd(dhn, h, nw, dh, name=f"ffn_norm_bwd_{tag}")
    return dh_in, dnw, dw_up, dcw, dw_down


def _local_step(x, target, w, fetch=None, push=None):
    fetch = fetch or (lambda stage, after: {})
    push = push or (lambda stage, grads: None)
    plus = lambda a, zero: a if zero is None else a + zero
    seq, d = x.shape
    m = PAD_ROWS + N_META + seq
    h0 = jnp.concatenate([jnp.zeros((PAD_ROWS, d), F32), w["meta"], x], 0)

    pm, hn0 = _norm_matmul(h0, w["anw"][0], w["w_in_main"], o_seg=SEG, name="mix_in")
    pba = _mm_tn(hn0, w["w_in_tail"], out_dtype=F32, name="mix_in_tail")
    qkvc = _dnpre_fwd(pm, w["dcw"], name="dn_conv")
    ba = pba[:, :2 * DN_HEADS].T.reshape(2, DN_HEADS, m, 1).transpose(1, 0, 2, 3)
    y, states = _delta_fwd(qkvc, pm, ba, w["hp"], w["dnw"], name="delta")
    y = _shortconv_fwd(pm, w["caw"], y, name="shortconv")
    w = {**w, **fetch("l0", y)}
    h1 = _mm_nn(y, w["w_out"], res=h0, name="mix_out")
    h2, ffn0 = _ffn_fwd(h1, w["fnw"][0], w["w_up0"], w["fcw"][0], w["w_down0"], "l0")

    w = {**w, **fetch("l1", h2)}
    qkv, hn2 = _norm_matmul(h2, w["anw"][1], w["wqkv"], name="attn_qkv")
    o = _attn_fwd(qkv, w["qnw"], w["knw"], w["sink"], name="attn")
    h3 = _mm_nn(o, w["wo"], res=h2, name="attn_out")
    h4, ffn1 = _ffn_fwd(h3, w["fnw"][1], w["w_up1"], w["fcw"][1], w["w_down1"], "l1")

    loss, dh4 = _loss_bwd(h4, target, name="loss")

    g = {}
    dh3, dfnw1, dwup1, dfcw1, dwdown1 = _ffn_bwd(dh4, ffn1, w["fnw"][1], w["w_up1"], w["fcw"][1], w["w_down1"], "l1")

    do = _mm_nn(dh3, w["wo"], trans_w=True, name="attn_do")
    g["wo"] = _mm_tn(o, dh3, name="attn_dwo")
    dq, dk, dv, dqw, dkw, dsink = _attn_bwd(do, qkv, w["qnw"], w["knw"], w["sink"], name="attn_bwd")
    dqkv = jnp.concatenate([dq, dk, dv], 1).astype(MXU_DTYPE)
    dhn2 = _mm_nn(dqkv, w["wqkv"], trans_w=True, name="attn_dhn")
    g["wqkv"] = _mm_nn(hn2, dqkv, out_dtype=GRAD_WIRE_DTYPE, name="attn_dwqkv")
    zero = push("l1", dict(w_up=dwup1, w_down=dwdown1, wo=g["wo"], wqkv=g["wqkv"]))
    dh2, danw1 = _rmsnorm_bwd(dhn2, h2, plus(w["anw"][1], zero), dh3, name="attn_norm_bwd")

    dh1, dfnw0, dwup0, dfcw0, dwdown0 = _ffn_bwd(dh2, ffn0, w["fnw"][0], w["w_up0"], w["fcw"][0], w["w_down0"], "l0")

    dy = _mm_nn(dh1, w["w_out"], trans_w=True, name="mix_dy")
    g["w_out"] = _mm_tn(y, dh1, name="mix_dwout")
    zero = push("l0", dict(w_up=dwup0, w_down=dwdown0, w_out=g["w_out"]))
    dpm, dqkvc, dba, dhp, ddnw = _delta_bwd(dy, qkvc, pm, ba, w["hp"], plus(w["dnw"], zero), states, name="delta_bwd")
    dpm, ddcw = _dnpre_bwd(dqkvc, pm, w["dcw"], dpm, name="dn_conv_bwd")
    dpm, dcaw = _shortconv_bwd(dy, pm, w["caw"], dpm, name="shortconv_bwd")
    dpba = jnp.pad(dba.transpose(1, 0, 2, 3).reshape(2 * DN_HEADS, m).T, ((0, 0), (0, LANES - 2 * DN_HEADS))).astype(MXU_DTYPE)
    g["w_in_main"] = _mm_nn(hn0, dpm, w_seg=True, out_dtype=GRAD_WIRE_DTYPE, name="mix_dwin")
    g["w_in_tail"] = _mm_nn(hn0, dpba, out_dtype=GRAD_WIRE_DTYPE, name="mix_dwin_tail")
    zero = push("in", dict(w_in_main=g["w_in_main"], w_in_tail=g["w_in_tail"]))
    dhn0 = _mm_nn(dpba, plus(w["w_in_tail"], None if zero is None else zero.astype(MXU_DTYPE)), trans_w=True, name="mix_dhn_tail")
    dhn0 = _mm_nn(dpm, w["w_in_main"], res=dhn0, a_seg=True, trans_w=True, name="mix_dhn")
    dh0, danw0 = _rmsnorm_bwd(dhn0, h0, w["anw"][0], dh1, name="mix_norm_bwd")

    g.update(
        x=dh0[PAD_ROWS + N_META:], meta=dh0[PAD_ROWS:PAD_ROWS + N_META], anw=[danw0, danw1], fnw=[dfnw0, dfnw1],
        caw=dcaw, dcw=ddcw, hp=dhp, dnw=ddnw, qnw=jnp.sum(dqw, 0), knw=jnp.sum(dkw, 0), sink=dsink,
        w_up=[dwup0, dwup1], fcw=[dfcw0, dfcw1], w_down=[dwdown0, dwdown1])
    return loss, g


N_TAIL = 2 * DN_HEADS


def _prepare_early(p):
    n_main = N_SEG * SEG
    w_in = p["mix_w_in"][0]
    tail = jnp.pad(w_in[:, n_main:], ((0, 0), (0, LANES - N_TAIL)))
    hp = jnp.zeros((DN_HEADS, SUBLANES, LANES), F32)
    hp = hp.at[:, 0, :].set(p["dn_a_log"][0][:, None]).at[:, 1, :].set(p["dn_dt_bias"][0][:, None])
    depth = p["ffn_conv_w"].shape[0]
    return dict(
        meta=p["meta_tokens"], anw=[p["attn_norm_w"][i:i + 1] for i in range(depth)],
        fnw=[p["ffn_norm_w"][i:i + 1] for i in range(depth)],
        w_in_main=w_in[:, :n_main], w_in_tail=tail,
        caw=_pad_w(p["conv_a_w"][0]), dcw=_pad_w(p["dn_conv_w"][0]), hp=hp, dnw=p["dn_norm_w"],
        qnw=p["swa_q_norm_w"], knw=p["swa_k_norm_w"], sink=p["swa_sinks"].reshape(SWA_KV_HEADS, SWA_GROUP, 1, 1),
        fcw=[_pad_w(p["ffn_conv_w"][i]) for i in range(depth)])


def _prepare_weights(p):
    return dict(
        _prepare_early(p), w_out=p["mix_w_out"][0], wo=p["swa_wo"][0],
        wqkv=jnp.concatenate([p["swa_wq"][0], p["swa_wk"][0], p["swa_wv"][0]], 1),
        w_up0=p["ffn_w_up"][0], w_up1=p["ffn_w_up"][1], w_down0=p["ffn_w_down"][0], w_down1=p["ffn_w_down"][1])


def _small_named(g):
    return dict(
        meta_tokens=g["meta"], attn_norm_w=jnp.concatenate(g["anw"], 0), ffn_norm_w=jnp.concatenate(g["fnw"], 0),
        conv_a_w=g["caw"][None, :3], dn_conv_w=g["dcw"][None, :4],
        dn_a_log=g["hp"][None, :, 0, 0], dn_dt_bias=g["hp"][None, :, 1, 0], dn_norm_w=g["dnw"],
        swa_q_norm_w=g["qnw"], swa_k_norm_w=g["knw"], swa_sinks=g["sink"].reshape(1, SWA_HEADS),
        ffn_conv_w=jnp.stack([c[:3] for c in g["fcw"]]))


def _reference_named(g):
    nq, nkv = SWA_HEADS * SWA_HEAD_DIM, SWA_KV_HEADS * SWA_HEAD_DIM
    return dict(
        _small_named(g), mix_w_in=jnp.concatenate([g["w_in_main"], g["w_in_tail"][:, :N_TAIL]], 1)[None],
        mix_w_out=g["w_out"][None], swa_wq=g["wqkv"][None, :, :nq], swa_wk=g["wqkv"][None, :, nq:nq + nkv],
        swa_wv=g["wqkv"][None, :, nq + nkv:], swa_wo=g["wo"][None],
        ffn_w_up=jnp.stack(g["w_up"]), ffn_w_down=jnp.stack(g["w_down"]))


def _my_index():
    return 4 * lax.axis_index("x") + 2 * lax.axis_index("y") + lax.axis_index("c")


def _all_gather(arrays, *, name):
    n = len(arrays)

    def body(*refs):
        ins, outs = refs[:n], refs[n:2 * n]
        send_sems, recv_sems, local_sems = refs[2 * n:]
        x, y, c = lax.axis_index("x"), lax.axis_index("y"), lax.axis_index("c")
        me, sibling = (x, y, c), (x, y, 1 - c)
        chips = [(1 - x, y), (x, 1 - y), (1 - x, 1 - y)]

        def copy(i, k, block, to, src=None):
            rows = outs[i].at[4 * block[0] + 2 * block[1] + block[2]]
            return pltpu.make_async_remote_copy(
                src_ref=rows if src is None else src, dst_ref=rows, send_sem=send_sems.at[i, k], recv_sem=recv_sems.at[i, k],
                device_id=to, device_id_type=pl.DeviceIdType.MESH)

        mine = [pltpu.make_async_copy(ins[i], outs[i].at[4 * x + 2 * y + c], local_sems.at[i]) for i in range(n)]
        first = []
        for j, chip in enumerate(chips):
            first += [copy(i, 1 + j, me, (*chip, c), src=ins[i]) for i in range(n)]
        first += [copy(i, 0, me, sibling, src=ins[i]) for i in range(n)]
        for cp in first + mine:
            cp.start()
        passed = []
        for j, chip in enumerate(chips):
            for i in range(n):
                copy(i, 1 + j, (*chip, c), me).wait_recv()
                fwd = copy(i, 4 + j, (*chip, c), sibling)
                fwd.start()
                passed.append(fwd)
        for i in range(n):
            copy(i, 0, sibling, me).wait_recv()
            for j, chip in enumerate(chips):
                copy(i, 4 + j, (*chip, 1 - c), me).wait_recv()
        for cp in first + passed:
            cp.wait_send()
        for cp in mine:
            cp.wait()

    hbm = pl.BlockSpec(memory_space=pl.ANY)
    return pl.pallas_call(
        body, name=name, in_specs=[hbm] * n, out_specs=[hbm] * n,
        out_shape=[jax.ShapeDtypeStruct((N_DEV,) + tuple(a.shape), a.dtype) for a in arrays],
        scratch_shapes=[pltpu.SemaphoreType.DMA((n, 7)), pltpu.SemaphoreType.DMA((n, 7)), pltpu.SemaphoreType.DMA((n,))],
    )(*arrays)


def _peer(d):
    px, py, pc = lax.axis_index("x") ^ (d >> 2), lax.axis_index("y") ^ ((d >> 1) & 1), lax.axis_index("c") ^ (d & 1)
    return (px, py, pc), 4 * px + 2 * py + pc


def _push_copies(mode, srcs, lands, send_sems, recv_sems):
    me = _my_index()
    out = []
    for d in range(1, N_DEV):
        pos, idx = _peer(d)
        for i in range(len(srcs)):
            out.append(pltpu.make_async_remote_copy(
                src_ref=srcs[i] if mode == "gather" else srcs[i].at[idx], dst_ref=lands[i].at[me],
                send_sem=send_sems.at[i * N_DEV + d], recv_sem=recv_sems.at[i * N_DEV + d], device_id=pos,
                device_id_type=pl.DeviceIdType.MESH))
    return out


_HBM = pl.BlockSpec(memory_space=pltpu.HBM)
_SEM = pl.BlockSpec(memory_space=pltpu.SEMAPHORE)


def _push_start(mode, arrays, follows, *, name):
    n = len(arrays)
    blocks = [a.shape if mode == "gather" else a.shape[1:] for a in arrays]
    lands = [lax.empty((N_DEV,) + tuple(b), a.dtype) for a, b in zip(arrays, blocks)]

    def body(*refs):
        srcs, land_refs = refs[:n], refs[n:2 * n]
        send_sems, recv_sems = refs[2 * n + 1], refs[2 * n + 2]
        zero = refs[-1]
        for cp in _push_copies(mode, srcs, land_refs, send_sems, recv_sems):
            cp.start()
        zero[...] = jnp.zeros_like(zero)

    hbm_in = [pltpu.with_memory_space_constraint(a, pltpu.HBM) for a in list(arrays) + lands]
    outs = pl.pallas_call(
        body, name=name,
        out_shape=[pltpu.SemaphoreType.DMA((n * N_DEV,)), pltpu.SemaphoreType.DMA((n * N_DEV,))]
        + [pltpu.HBM(a.shape, a.dtype) for a in hbm_in] + [jax.ShapeDtypeStruct((SUBLANES, LANES), F32)],
        in_specs=[_HBM] * (2 * n) + [pl.BlockSpec(memory_space=pl.ANY)],
        out_specs=[_SEM, _SEM] + [_HBM] * (2 * n) + [pl.BlockSpec(memory_space=pltpu.VMEM)],
        input_output_aliases={i: 2 + i for i in range(2 * n)},
        compiler_params=pltpu.CompilerParams(has_side_effects=pltpu.SideEffectType.DATAFLOW_SIDE_EFFECTING),
    )(*hbm_in, follows)
    return dict(mode=mode, sems=outs[:2], srcs=outs[2:2 + n], lands=outs[2 + n:2 + 2 * n], zero=outs[-1])


def _push_wait(push, follows, *, name):
    n = len(push["srcs"])
    mode = push["mode"]

    def body(*refs):
        srcs, land_refs = refs[:n], refs[n:2 * n]
        send_sems, recv_sems = refs[2 * n], refs[2 * n + 1]
        for cp in _push_copies(mode, srcs, land_refs, send_sems, recv_sems):
            cp.wait_send()
            cp.wait_recv()

    args = list(push["srcs"]) + list(push["lands"])
    outs = pl.pallas_call(
        body, name=name, out_shape=[pltpu.HBM(a.shape, a.dtype) for a in args],
        in_specs=[_HBM] * (2 * n) + [_SEM, _SEM, pl.BlockSpec(memory_space=pl.ANY)], out_specs=[_HBM] * (2 * n),
        input_output_aliases={i: i for i in range(2 * n)},
        compiler_params=pltpu.CompilerParams(has_side_effects=pltpu.SideEffectType.DATAFLOW_SIDE_EFFECTING),
    )(*args, *push["sems"], follows)
    me = _my_index()
    got = []
    for src, land in zip(outs[:n], outs[n:]):
        own = src if mode == "gather" else lax.dynamic_index_in_dim(src, me, 0, keepdims=False)
        got.append(lax.dynamic_update_index_in_dim(land, own, me, 0))
    return got


def _row_tile(r, c):
    best = None
    for t in range(2 * SUBLANES, r + 1, 2 * SUBLANES):
        if r % t == 0 and N_DEV * t * c * 4 <= 6 * 1024 * 1024:
            best = t
    return best or r


def _adamw(parts, w, m, v, layer, outs=None, *, name):
    nl, r, c = w.shape
    tr = _row_tile(r, c)

    def body(p_ref, w_ref, m_ref, v_ref, *rest):
        g_ref, d_ref, nm_ref, nv_ref = rest[-4:]
        g = p_ref[0].astype(F32)
        for j in range(1, N_DEV):
            g = g + p_ref[j].astype(F32)
        m2 = ADAM_B1 * m_ref[...] + (1.0 - ADAM_B1) * g
        v2 = ADAM_B2 * v_ref[...] + (1.0 - ADAM_B2) * jnp.square(g)
        m_hat = m2 / (1.0 - ADAM_B1 ** ADAM_STEP)
        v_hat = v2 / (1.0 - ADAM_B2 ** ADAM_STEP)
        g_ref[...] = g
        d_ref[...] = -ADAM_LR * (m_hat / (jnp.sqrt(v_hat) + ADAM_EPS) + ADAM_WD * w_ref[...])
        nm_ref[...] = m2
        nv_ref[...] = v2

    blk = pl.BlockSpec((None, tr, c), lambda i: (layer, i, 0))
    out = jax.ShapeDtypeStruct((nl, r, c), F32)
    given = list(outs) if outs is not None else []
    return pl.pallas_call(
        body, name=name, grid=(r // tr,),
        in_specs=[pl.BlockSpec((N_DEV, tr, c), lambda i: (0, i, 0)), blk, blk, blk] + [pl.BlockSpec(memory_space=pl.ANY)] * len(given),
        out_specs=[blk, blk, blk, blk], out_shape=[out, out, out, out],
        input_output_aliases={4 + t: t for t in range(len(given))},
        compiler_params=_params(("parallel",)),
    )(parts, w, m, v, *given)


SHARD_AXIS = dict(
    meta_tokens=1, attn_norm_w=None, ffn_norm_w=None, mix_w_in=2, conv_a_w=2, dn_conv_w=2, dn_a_log=None, dn_dt_bias=None,
    dn_norm_w=None, mix_w_out=1, swa_wq=1, swa_wk=1, swa_wv=1, swa_q_norm_w=None, swa_k_norm_w=None, swa_sinks=None,
    swa_wo=1, ffn_w_up=2, ffn_conv_w=2, ffn_w_down=1)
WEIGHTS = list(SHARD_AXIS)
BIG = ["mix_w_in", "mix_w_out", "swa_wq", "swa_wk", "swa_wv", "swa_wo", "ffn_w_up", "ffn_w_down"]
SMALL = [k for k in WEIGHTS if k not in BIG]
SMALL_SHARDED = [k for k in SMALL if SHARD_AXIS[k] is not None]


def _whole(g8, axis):
    t = jnp.moveaxis(g8, 0, axis)
    return t.reshape(t.shape[:axis] + (t.shape[axis] * t.shape[axis + 1],) + t.shape[axis + 2:])


def _by_owner(a, axis):
    s = a.shape[axis] // N_DEV
    return jnp.moveaxis(a.reshape(a.shape[:axis] + (N_DEV, s) + a.shape[axis + 1:]), axis, 0)


def _pack(arrays, lead=0):
    flat = jnp.concatenate([a.reshape(a.shape[:lead] + (-1,)) for a in arrays], -1)
    n = flat.shape[-1]
    rows = -(-n // (SUBLANES * LANES)) * SUBLANES
    flat = jnp.pad(flat, [(0, 0)] * lead + [(0, rows * LANES - n)])
    return flat.reshape(flat.shape[:lead] + (rows, LANES))


def _unpack(buf, shapes, lead=0):
    flat = buf.reshape(buf.shape[:lead] + (-1,))
    out, o = [], 0
    for s in shapes:
        n = 1
        for e in s:
            n *= e
        out.append(flat[..., o:o + n].reshape(buf.shape[:lead] + tuple(s)))
        o += n
    return out


def kernel(x, meta_tokens, attn_norm_w, ffn_norm_w, mix_w_in, conv_a_w, dn_conv_w, dn_a_log, dn_dt_bias, dn_norm_w, mix_w_out, swa_wq, swa_wk, swa_wv, swa_q_norm_w, swa_k_norm_w, swa_sinks, swa_wo, ffn_w_up, ffn_conv_w, ffn_w_down, loss_target, m_meta_tokens, m_attn_norm_w, m_ffn_norm_w, m_mix_w_in, m_conv_a_w, m_dn_conv_w, m_dn_a_log, m_dn_dt_bias, m_dn_norm_w, m_mix_w_out, m_swa_wq, m_swa_wk, m_swa_wv, m_swa_q_norm_w, m_swa_k_norm_w, m_swa_sinks, m_swa_wo, m_ffn_w_up, m_ffn_conv_w, m_ffn_w_down, v_meta_tokens, v_attn_norm_w, v_ffn_norm_w, v_mix_w_in, v_conv_a_w, v_dn_conv_w, v_dn_a_log, v_dn_dt_bias, v_dn_norm_w, v_mix_w_out, v_swa_wq, v_swa_wk, v_swa_wv, v_swa_q_norm_w, v_swa_k_norm_w, v_swa_sinks, v_swa_wo, v_ffn_w_up, v_ffn_conv_w, v_ffn_w_down):
    w = dict(meta_tokens=meta_tokens, attn_norm_w=attn_norm_w, ffn_norm_w=ffn_norm_w, mix_w_in=mix_w_in, conv_a_w=conv_a_w, dn_conv_w=dn_conv_w, dn_a_log=dn_a_log, dn_dt_bias=dn_dt_bias, dn_norm_w=dn_norm_w, mix_w_out=mix_w_out, swa_wq=swa_wq, swa_wk=swa_wk, swa_wv=swa_wv, swa_q_norm_w=swa_q_norm_w, swa_k_norm_w=swa_k_norm_w, swa_sinks=swa_sinks, swa_wo=swa_wo, ffn_w_up=ffn_w_up, ffn_conv_w=ffn_conv_w, ffn_w_down=ffn_w_down)
    mom = dict(meta_tokens=m_meta_tokens, attn_norm_w=m_attn_norm_w, ffn_norm_w=m_ffn_norm_w, mix_w_in=m_mix_w_in, conv_a_w=m_conv_a_w, dn_conv_w=m_dn_conv_w, dn_a_log=m_dn_a_log, dn_dt_bias=m_dn_dt_bias, dn_norm_w=m_dn_norm_w, mix_w_out=m_mix_w_out, swa_wq=m_swa_wq, swa_wk=m_swa_wk, swa_wv=m_swa_wv, swa_q_norm_w=m_swa_q_norm_w, swa_k_norm_w=m_swa_k_norm_w, swa_sinks=m_swa_sinks, swa_wo=m_swa_wo, ffn_w_up=m_ffn_w_up, ffn_conv_w=m_ffn_conv_w, ffn_w_down=m_ffn_w_down)
    var = dict(meta_tokens=v_meta_tokens, attn_norm_w=v_attn_norm_w, ffn_norm_w=v_ffn_norm_w, mix_w_in=v_mix_w_in, conv_a_w=v_conv_a_w, dn_conv_w=v_dn_conv_w, dn_a_log=v_dn_a_log, dn_dt_bias=v_dn_dt_bias, dn_norm_w=v_dn_norm_w, mix_w_out=v_mix_w_out, swa_wq=v_swa_wq, swa_wk=v_swa_wk, swa_wv=v_swa_wv, swa_q_norm_w=v_swa_q_norm_w, swa_k_norm_w=v_swa_k_norm_w, swa_sinks=v_swa_sinks, swa_wo=v_swa_wo, ffn_w_up=v_ffn_w_up, ffn_conv_w=v_ffn_conv_w, ffn_w_down=v_ffn_w_down)
    me = _my_index()

    shard16 = {k: w[k].astype(MXU_DTYPE) for k in BIG}
    small_shard_shapes = [w[k].shape for k in SMALL_SHARDED]
    got = _all_gather([shard16["mix_w_in"], _pack([w[k] for k in SMALL_SHARDED])], name="gather_weights")
    whole = {"mix_w_in": _whole(got[0], SHARD_AXIS["mix_w_in"])}
    for k, a in zip(SMALL_SHARDED, _unpack(got[1], small_shard_shapes, lead=1)):
        whole[k] = _whole(a, SHARD_AXIS[k])
    for k in SMALL:
        whole.setdefault(k, w[k])
    stages = {"in": [("mix_w_in", 0)], "l0": [("mix_w_out", 0), ("ffn_w_up", 0), ("ffn_w_down", 0)],
              "l1": [("swa_wq", 0), ("swa_wk", 0), ("swa_wv", 0), ("swa_wo", 0), ("ffn_w_up", 1), ("ffn_w_down", 1)]}
    pushed = {}
    follows = jnp.zeros((SUBLANES, LANES), F32)
    for stage in ("l0", "l1"):
        pushed[stage] = _push_start("gather", [shard16[k][l] for k, l in stages[stage]], follows, name=f"push_weights_{stage}")
        follows = pushed[stage]["zero"]
    early = _prepare_early(whole)
    early["anw"][0] = early["anw"][0] + follows[0, 0]

    def fetch(stage, after):
        got = _push_wait(pushed[stage], after, name=f"wait_weights_{stage}")
        full = {kl: _whole(a, SHARD_AXIS[kl[0]] - 1) for kl, a in zip(stages[stage], got)}
        if stage == "l0":
            return dict(w_out=full["mix_w_out", 0], w_up0=full["ffn_w_up", 0], w_down0=full["ffn_w_down", 0])
        wqkv = jnp.concatenate([full["swa_wq", 0], full["swa_wk", 0], full["swa_wv", 0]], 1)
        return dict(wqkv=wqkv, wo=full["swa_wo", 0], w_up1=full["ffn_w_up", 1], w_down1=full["ffn_w_down", 1])

    nq, nkv = SWA_HEADS * SWA_HEAD_DIM, SWA_KV_HEADS * SWA_HEAD_DIM
    grad_pushes = {}

    def push(stage, gd):
        if stage == "in":
            named = {("mix_w_in", 0): jnp.concatenate([gd["w_in_main"], gd["w_in_tail"][:, :N_TAIL]], 1)}
        elif stage == "l1":
            named = {("swa_wq", 0): gd["wqkv"][:, :nq], ("swa_wk", 0): gd["wqkv"][:, nq:nq + nkv],
                     ("swa_wv", 0): gd["wqkv"][:, nq + nkv:], ("swa_wo", 0): gd["wo"],
                     ("ffn_w_up", 1): gd["w_up"], ("ffn_w_down", 1): gd["w_down"]}
        else:
            named = {("mix_w_out", 0): gd["w_out"], ("ffn_w_up", 0): gd["w_up"], ("ffn_w_down", 0): gd["w_down"]}
        sent = [_by_owner(named[kl], SHARD_AXIS[kl[0]] - 1) for kl in stages[stage]]
        grad_pushes[stage] = _push_start("scatter", sent, jnp.zeros((SUBLANES, LANES), F32), name=f"push_grads_{stage}")
        return grad_pushes[stage]["zero"][0, 0]

    loss, g = _local_step(x[0], loss_target[0], early, fetch, push)
    grads = _small_named(g)

    results = {}

    def update(stage, follows):
        got = _push_wait(grad_pushes[stage], follows, name=f"wait_grads_{stage}")
        for (k, l), parts in zip(stages[stage], got):
            r, c = w[k].shape[1:]
            results[k] = _adamw(parts.reshape(N_DEV, r, c), w[k], mom[k], var[k], l, results.get(k), name=f"adamw_{k}_{l}")
        return results[stages[stage][0][0]][0]

    follows = update("l0", update("l1", g["meta"]))

    small_shapes = [grads[k].shape for k in SMALL]
    (all_small,) = _all_gather([_pack([loss] + [grads[k].astype(F32) for k in SMALL])], name="gather_small_grads")
    loss_parts, *small_parts = _unpack(all_small, [loss.shape] + small_shapes, lead=1)
    mine = []
    for k, p in zip(SMALL, small_parts):
        ax = SHARD_AXIS[k]
        mine.append(p if ax is None else lax.dynamic_slice_in_dim(p, me * w[k].shape[ax], w[k].shape[ax], 1 + ax))
    zero = jnp.zeros(loss.shape, F32)
    packed = [_pack([z] + [d[k] for k in SMALL]) for z, d in ((zero, w), (zero, mom), (zero, var))]
    res = _adamw(_pack([loss_parts] + mine, lead=1), *[t[None] for t in packed], 0, name="adamw_small")
    shapes = [loss.shape] + [w[k].shape for k in SMALL]
    for t, which in zip(res, range(4)):
        for k, a in zip(["loss"] + SMALL, _unpack(t[0], shapes)):
            results.setdefault(k, [None] * 4)[which] = a
    update("in", jnp.maximum(follows[0, :1, :1], res[0][0, :1, :1]))

    outs = [[results[k][which] for k in WEIGHTS] for which in range(4)]
    return (results["loss"][0][0, 0], g["x"][None], *outs[0], *outs[1], *outs[2], *outs[3])
```

```python
import functools

import jax
import jax.numpy as jnp
from jax import lax
from jax.experimental import pallas as pl
from jax.experimental.pallas import tpu as pltpu

F32 = jnp.float32
BF16 = jnp.bfloat16
MXU_DTYPE = BF16
GRAD_WIRE_DTYPE = BF16

D_MODEL = 1024
N_META = 16
PAD_ROWS = 112
D_CONV = 512
DN_HEADS = 4
DN_HEAD_DIM = 128
DN_DIM = DN_HEADS * DN_HEAD_DIM
DN_CHUNK = 64
SEG = 512
N_SEG = 7
SWA_HEADS = 16
SWA_KV_HEADS = 4
SWA_GROUP = SWA_HEADS // SWA_KV_HEADS
SWA_HEAD_DIM = 64
SWA_BLOCK = 128
D_FF = 2816
EPS = 1e-6
NEG = -1e30
N_DEV = 8

ADAM_LR = 0.001
ADAM_B1 = 0.9
ADAM_B2 = 0.999
ADAM_EPS = 1e-08
ADAM_WD = 0.01
ADAM_STEP = 10

VMEM_LIMIT_BYTES = 52 * 1024 * 1024
SUBLANES = 8
LANES = 128


def _pick(n, prefs):
    for p in prefs:
        if n % p == 0:
            return p
    return n


def _params(sem, vmem=VMEM_LIMIT_BYTES):
    return pltpu.CompilerParams(dimension_semantics=sem, vmem_limit_bytes=vmem)


def _rms(x, w):
    return x * lax.rsqrt(jnp.mean(x * x, -1, keepdims=True) + EPS) * w


def _norm_matmul(h, nw, w, *, o_seg=None, trans_w=False, n=None, name):
    m, k = h.shape
    n = n or (w.shape[0] if trans_w else w.shape[1])
    tm = _pick(m, (1408, 384, 128))
    tn = _pick(o_seg or n, (1408, 1024, 512, 256, 128))
    dims = _DOT_DIMS["nt" if trans_w else "nn"]

    def body(h_ref, nw_ref, w_ref, o_ref, hn_ref, hn_s):
        @pl.when(pl.program_id(1) == 0)
        def _():
            hn = _rms(h_ref[...], nw_ref[...]).astype(MXU_DTYPE)
            hn_s[...] = hn
            hn_ref[...] = hn

        o_ref[...] = lax.dot_general(hn_s[...], w_ref[...], dims, preferred_element_type=F32)

    if o_seg:
        per = o_seg // tn
        o_shape = jax.ShapeDtypeStruct((n // o_seg, m, o_seg), F32)
        o_spec = pl.BlockSpec((None, tm, tn), lambda i, j: (j // per, i, j % per))
    else:
        o_shape = jax.ShapeDtypeStruct((m, n), F32)
        o_spec = pl.BlockSpec((tm, tn), lambda i, j: (i, j))
    return pl.pallas_call(
        body, name=name, grid=(m // tm, n // tn),
        in_specs=[pl.BlockSpec((tm, k), lambda i, j: (i, 0)), pl.BlockSpec((1, k), lambda i, j: (0, 0)),
                  pl.BlockSpec((tn, k), lambda i, j: (j, 0)) if trans_w else pl.BlockSpec((k, tn), lambda i, j: (0, j))],
        out_specs=[o_spec, pl.BlockSpec((tm, k), lambda i, j: (i, 0))],
        out_shape=[o_shape, jax.ShapeDtypeStruct((m, k), MXU_DTYPE)],
        scratch_shapes=[pltpu.VMEM((tm, k), MXU_DTYPE)],
        compiler_params=_params(("parallel", "arbitrary")),
    )(h, nw, w)


TILE_BUDGET_BYTES = 38 * 1024 * 1024
TILE_SIZES = (4224, 2816, 1792, 1536, 1408, 1024, 512, 256, 128)


def _divisor_tiles(n):
    return [t for t in TILE_SIZES if n % t == 0] or [n]


def _mm_nn(a, w, *, res=None, a_seg=False, trans_w=False, w_seg=False, out_dtype=F32, name):
    if a_seg:
        s, m, seg = a.shape
    else:
        m, seg = a.shape
        s = 1
    k = s * seg
    n = w.shape[0] * w.shape[2] if w_seg else (w.shape[0] if trans_w else w.shape[1])
    n_seg = w.shape[2] if w_seg else n
    tm = _pick(m, (1408, 1024, 512, 384, 256, 128))
    ab = a.dtype.itemsize
    k_steps = [(sb, seg) for sb in range(s, 0, -1) if s % sb == 0] if a_seg else [(1, t) for t in _divisor_tiles(seg)]
    best = None
    for tn in _divisor_tiles(n_seg):
        for sb, tk1 in k_steps:
            tk = sb * tk1
            nk = k // tk
            need = 2 * tm * tk * ab + 2 * tk * tn * 2 + 2 * tm * tn * 4 + (tm * tn * 4 if nk > 1 else 0) + (2 * tm * tn * 4 if res is not None else 0)
            if need <= TILE_BUDGET_BYTES and (best is None or tk * tn > best[0]):
                best = (tk * tn, tn, sb, tk1)
    _, tn, sb, tk1 = best
    tk = sb * tk1
    nk = k // tk
    w_dims = _DOT_DIMS["nt" if trans_w else "nn"]

    def body(*refs):
        a_ref, w_ref = refs[:2]
        r_ref = refs[2] if res is not None else None
        o_ref = refs[3 if res is not None else 2]

        def partial_product():
            if not a_seg:
                return lax.dot_general(a_ref[...].astype(MXU_DTYPE), w_ref[...], w_dims, preferred_element_type=F32)
            out = None
            for t in range(sb):
                wt = w_ref[:, t * seg:(t + 1) * seg] if trans_w else w_ref[t * seg:(t + 1) * seg, :]
                d = lax.dot_general(a_ref[t].astype(MXU_DTYPE), wt, w_dims, preferred_element_type=F32)
                out = d if out is None else out + d
            return out

        if nk == 1:
            o_ref[...] = (partial_product() if res is None else partial_product() + r_ref[...]).astype(o_ref.dtype)
            return
        acc = refs[-1]
        kk = pl.program_id(2)

        @pl.when(kk == 0)
        def _():
            acc[...] = jnp.zeros_like(acc)

        acc[...] += partial_product()

        @pl.when(kk == nk - 1)
        def _():
            o_ref[...] = (acc[...] if res is None else acc[...] + r_ref[...]).astype(o_ref.dtype)

    a_spec = pl.BlockSpec((sb, tm, seg), lambda i, j, kk: (kk, i, 0)) if a_seg else pl.BlockSpec((tm, tk), lambda i, j, kk: (i, kk))
    if w_seg:
        per = n_seg // tn
        w_spec = pl.BlockSpec((None, tk, tn), lambda i, j, kk: (j // per, kk, j % per))
    elif trans_w:
        w_spec = pl.BlockSpec((tn, tk), lambda i, j, kk: (j, kk))
    else:
        w_spec = pl.BlockSpec((tk, tn), lambda i, j, kk: (kk, j))
    in_specs = [a_spec, w_spec]
    args = [a, w]
    if res is not None:
        in_specs.append(pl.BlockSpec((tm, tn), lambda i, j, kk: (i, j)))
        args.append(res)
    return pl.pallas_call(
        body, name=name, grid=(m // tm, n // tn, nk), in_specs=in_specs,
        out_specs=pl.BlockSpec((tm, tn), lambda i, j, kk: (i, j)),
        out_shape=jax.ShapeDtypeStruct((m, n), out_dtype),
        scratch_shapes=[pltpu.VMEM((tm, tn), F32)] if nk > 1 else [],
        compiler_params=_params(("parallel", "parallel", "arbitrary")),
    )(*args)


def _mm_tn(a, b, *, a_seg=False, b_seg=False, out_dtype=None, name):
    out_dtype = out_dtype or GRAD_WIRE_DTYPE
    if a_seg:
        sa, m, a_unit = a.shape
        ka = sa * a_unit
    else:
        m, ka = a.shape
        a_unit = ka
    if b_seg:
        s, _, seg = b.shape
        n = s * seg
    else:
        n = b.shape[1]
        seg = n
    tmc = _pick(m, (1408, 384, 128))
    best = None
    for tka in _divisor_tiles(a_unit):
        for tn in _divisor_tiles(seg):
            need = 2 * tmc * tka * a.dtype.itemsize + 2 * tmc * tn * b.dtype.itemsize + tka * tn * 4 + 2 * tka * tn * 4
            if need <= TILE_BUDGET_BYTES and (best is None or (tka * tn, tn) > best[:2]):
                best = (tka * tn, tn, tka)
    _, tn, tka = best
    nm = m // tmc

    def body(a_ref, b_ref, o_ref, acc):
        mm = pl.program_id(2)

        @pl.when(mm == 0)
        def _():
            acc[...] = jnp.zeros_like(acc)

        acc[...] += lax.dot_general(a_ref[...].astype(MXU_DTYPE), b_ref[...].astype(MXU_DTYPE),
                                    (((0,), (0,)), ((), ())), preferred_element_type=F32)

        @pl.when(mm == nm - 1)
        def _():
            o_ref[...] = acc[...].astype(o_ref.dtype)

    if b_seg:
        per = seg // tn
        b_spec = pl.BlockSpec((None, tmc, tn), lambda i, j, mm: (j // per, mm, j % per))
    else:
        b_spec = pl.BlockSpec((tmc, tn), lambda i, j, mm: (mm, j))
    if a_seg:
        a_per = a_unit // tka
        a_spec = pl.BlockSpec((None, tmc, tka), lambda i, j, mm: (i // a_per, mm, i % a_per))
    else:
        a_spec = pl.BlockSpec((tmc, tka), lambda i, j, mm: (mm, i))
    return pl.pallas_call(
        body, name=name, grid=(ka // tka, n // tn, nm),
        in_specs=[a_spec, b_spec],
        out_specs=pl.BlockSpec((tka, tn), lambda i, j, mm: (i, j)),
        out_shape=jax.ShapeDtypeStruct((ka, n), out_dtype),
        scratch_shapes=[pltpu.VMEM((tka, tn), F32)],
        compiler_params=_params(("parallel", "parallel", "arbitrary")),
    )(a, b)


def _rmsnorm_bwd(dhn, h, nw, dres, *, name):
    m, d = h.shape
    tm = _pick(m, (384, 128))

    def body(dhn_ref, h_ref, nw_ref, dres_ref, dh_ref, dnw_ref):
        i = pl.program_id(0)
        x = h_ref[...]
        r = lax.rsqrt(jnp.mean(x * x, -1, keepdims=True) + EPS)
        xh = x * r
        dy = dhn_ref[...]
        dxh = dy * nw_ref[...]
        dx = r * (dxh - xh * jnp.mean(dxh * xh, -1, keepdims=True))
        row = i * tm + lax.broadcasted_iota(jnp.int32, (tm, 1), 0)
        dh_ref[...] = jnp.where(row >= PAD_ROWS, dres_ref[...] + dx, 0.0)

        @pl.when(i == 0)
        def _():
            dnw_ref[...] = jnp.zeros_like(dnw_ref)

        dnw_ref[...] += jnp.sum(dy * xh, 0, keepdims=True)

    return pl.pallas_call(
        body, name=name, grid=(m // tm,),
        in_specs=[pl.BlockSpec((tm, d), lambda i: (i, 0)), pl.BlockSpec((tm, d), lambda i: (i, 0)),
                  pl.BlockSpec((1, d), lambda i: (0, 0)), pl.BlockSpec((tm, d), lambda i: (i, 0))],
        out_specs=[pl.BlockSpec((tm, d), lambda i: (i, 0)), pl.BlockSpec((1, d), lambda i: (0, 0))],
        out_shape=[jax.ShapeDtypeStruct((m, d), F32), jax.ShapeDtypeStruct((1, d), F32)],
        compiler_params=_params(("arbitrary",)),
    )(dhn, h, nw, dres)


ROW_CHUNK = 248


def _row_chunks(m):
    out, s = [], SUBLANES
    while s < m:
        n = min(ROW_CHUNK, m - s)
        out.append((s, n))
        s += n
    return out


def _conv_at(load, w, width, s, n):
    acc = w[width - 1:width, :] * load(s, n)
    for j in range(width - 1):
        acc = acc + w[j:j + 1, :] * load(s - (width - 1 - j), n)
    return acc


def _conv_t_at(load, w, width, s, n):
    acc = w[width - 1:width, :] * load(s, n)
    for j in range(width - 1):
        acc = acc + w[j:j + 1, :] * load(s + (width - 1 - j), n)
    return acc


def _dconv_w(load_x, d, width, s, n):
    rows = [jnp.sum(d * load_x(s - (width - 1 - j), n), 0, keepdims=True) for j in range(width)]
    rows.append(jnp.zeros((SUBLANES - width, d.shape[1]), F32))
    return jnp.concatenate(rows, 0)


def _pad_w(w):
    return jnp.concatenate([w, jnp.zeros((SUBLANES - w.shape[0], w.shape[1]), w.dtype)], 0)


def _sigmoid(x):
    return 1.0 / (1.0 + jnp.exp(-x))


def _ffn_act_fwd(u, cw, *, name):
    _, m, f = u.shape
    cb = _pick(f, (256, 128))
    chunks = _row_chunks(m)

    def body(g_ref, v_ref, w_ref, o_ref):
        w = w_ref[...]
        o_ref[pl.ds(0, SUBLANES), :] = jnp.zeros((SUBLANES, cb), o_ref.dtype)
        for s, n in chunks:
            c = _conv_at(lambda a, b: g_ref[pl.ds(a, b), :], w, 3, s, n)
            o_ref[pl.ds(s, n), :] = (c * _sigmoid(c) * v_ref[pl.ds(s, n), :]).astype(o_ref.dtype)

    return pl.pallas_call(
        body, name=name, grid=(f // cb,),
        in_specs=[pl.BlockSpec((None, m, cb), lambda j: (0, 0, j)), pl.BlockSpec((None, m, cb), lambda j: (1, 0, j)),
                  pl.BlockSpec((SUBLANES, cb), lambda j: (0, j))],
        out_specs=pl.BlockSpec((m, cb), lambda j: (0, j)),
        out_shape=jax.ShapeDtypeStruct((m, f), MXU_DTYPE),
        compiler_params=_params(("parallel",)),
    )(u, u, cw)


def _ffn_act_bwd(da, u, cw, *, name):
    _, m, f = u.shape
    cb = LANES
    chunks = _row_chunks(m)

    def body(da_ref, g_ref, v_ref, w_ref, du_ref, dw_ref, dg_s):
        w = w_ref[...]
        zeros8 = jnp.zeros((SUBLANES, cb), F32)
        dg_s[pl.ds(0, SUBLANES), :] = zeros8
        dg_s[pl.ds(m, SUBLANES), :] = zeros8
        du_ref[0, pl.ds(0, SUBLANES), :] = zeros8.astype(du_ref.dtype)
        du_ref[1, pl.ds(0, SUBLANES), :] = zeros8.astype(du_ref.dtype)
        load_g = lambda a, b: g_ref[pl.ds(a, b), :]
        dw = jnp.zeros((SUBLANES, cb), F32)
        for s, n in chunks:
            c = _conv_at(load_g, w, 3, s, n)
            sg = _sigmoid(c)
            d = da_ref[pl.ds(s, n), :]
            du_ref[1, pl.ds(s, n), :] = (d * (c * sg)).astype(du_ref.dtype)
            dc = d * v_ref[pl.ds(s, n), :] * (sg * (1.0 + c * (1.0 - sg)))
            dg_s[pl.ds(s, n), :] = dc
            dw = dw + _dconv_w(load_g, dc, 3, s, n)
        dw_ref[...] = dw
        for s, n in chunks:
            du_ref[0, pl.ds(s, n), :] = _conv_t_at(lambda a, b: dg_s[pl.ds(a, b), :], w, 3, s, n).astype(du_ref.dtype)

    return pl.pallas_call(
        body, name=name, grid=(f // cb,),
        in_specs=[pl.BlockSpec((m, cb), lambda j: (0, j)), pl.BlockSpec((None, m, cb), lambda j: (0, 0, j)),
                  pl.BlockSpec((None, m, cb), lambda j: (1, 0, j)), pl.BlockSpec((SUBLANES, cb), lambda j: (0, j))],
        out_specs=[pl.BlockSpec((2, m, cb), lambda j: (0, 0, j)), pl.BlockSpec((SUBLANES, cb), lambda j: (0, j))],
        out_shape=[jax.ShapeDtypeStruct((2, m, f), MXU_DTYPE), jax.ShapeDtypeStruct((SUBLANES, f), F32)],
        scratch_shapes=[pltpu.VMEM((m + SUBLANES, cb), F32)],
        compiler_params=_params(("parallel",)),
    )(da, u, u, cw)


def _shortconv_fwd(pm, cw, y, *, name):
    _, m, seg = pm.shape
    cb = _pick(seg, (256, 128))
    chunks = _row_chunks(m)

    def body(gi_ref, go_ref, ah_ref, w_ref, y_in, o_ref):
        del y_in
        w = w_ref[...]
        o_ref[pl.ds(0, SUBLANES), :] = jnp.zeros((SUBLANES, cb), o_ref.dtype)
        load_m = lambda a, b: gi_ref[pl.ds(a, b), :] * ah_ref[pl.ds(a, b), :]
        for s, n in chunks:
            o_ref[pl.ds(s, n), :] = (go_ref[pl.ds(s, n), :] * _conv_at(load_m, w, 3, s, n)).astype(o_ref.dtype)

    return pl.pallas_call(
        body, name=name, grid=(seg // cb,),
        in_specs=[pl.BlockSpec((None, m, cb), lambda j: (0, 0, j)), pl.BlockSpec((None, m, cb), lambda j: (1, 0, j)),
                  pl.BlockSpec((None, m, cb), lambda j: (2, 0, j)), pl.BlockSpec((SUBLANES, cb), lambda j: (0, j)),
                  pl.BlockSpec(memory_space=pl.ANY)],
        out_specs=pl.BlockSpec((m, cb), lambda j: (0, j)),
        out_shape=jax.ShapeDtypeStruct(y.shape, y.dtype),
        input_output_aliases={4: 0},
        compiler_params=_params(("parallel",)),
    )(pm, pm, pm, cw, y)


def _shortconv_bwd(dy, pm, cw, dpm, *, name):
    _, m, seg = pm.shape
    cb = LANES
    chunks = _row_chunks(m)

    def body(dy_ref, gi_ref, go_ref, ah_ref, w_ref, dpm_in, dp_ref, dw_ref, dc_s):
        del dpm_in
        w = w_ref[...]
        zeros8 = jnp.zeros((SUBLANES, cb), F32)
        dc_s[pl.ds(0, SUBLANES), :] = zeros8
        dc_s[pl.ds(m, SUBLANES), :] = zeros8
        for t in range(3):
            dp_ref[t, pl.ds(0, SUBLANES), :] = zeros8.astype(dp_ref.dtype)
        load_m = lambda a, b: gi_ref[pl.ds(a, b), :] * ah_ref[pl.ds(a, b), :]
        dw = jnp.zeros((SUBLANES, cb), F32)
        for s, n in chunks:
            d = dy_ref[pl.ds(s, n), :]
            dp_ref[1, pl.ds(s, n), :] = (d * _conv_at(load_m, w, 3, s, n)).astype(dp_ref.dtype)
            dc = d * go_ref[pl.ds(s, n), :]
            dc_s[pl.ds(s, n), :] = dc
            dw = dw + _dconv_w(load_m, dc, 3, s, n)
        dw_ref[...] = dw
        for s, n in chunks:
            dm = _conv_t_at(lambda a, b: dc_s[pl.ds(a, b), :], w, 3, s, n)
            dp_ref[0, pl.ds(s, n), :] = (dm * ah_ref[pl.ds(s, n), :]).astype(dp_ref.dtype)
            dp_ref[2, pl.ds(s, n), :] = (dm * gi_ref[pl.ds(s, n), :]).astype(dp_ref.dtype)

    return pl.pallas_call(
        body, name=name, grid=(seg // cb,),
        in_specs=[pl.BlockSpec((m, cb), lambda j: (0, j)), pl.BlockSpec((None, m, cb), lambda j: (0, 0, j)),
                  pl.BlockSpec((None, m, cb), lambda j: (1, 0, j)), pl.BlockSpec((None, m, cb), lambda j: (2, 0, j)),
                  pl.BlockSpec((SUBLANES, cb), lambda j: (0, j)), pl.BlockSpec(memory_space=pl.ANY)],
        out_specs=[pl.BlockSpec((3, m, cb), lambda j: (0, 0, j)), pl.BlockSpec((SUBLANES, cb), lambda j: (0, j))],
        out_shape=[jax.ShapeDtypeStruct(dpm.shape, dpm.dtype), jax.ShapeDtypeStruct((SUBLANES, seg), F32)],
        scratch_shapes=[pltpu.VMEM((m + SUBLANES, cb), F32)],
        input_output_aliases={5: 0},
        compiler_params=_params(("parallel",)),
    )(dy, pm, pm, pm, cw, dpm)


def _dnpre_fwd(pm, cw, *, name):
    _, m, seg = pm.shape
    cb = _pick(seg, (256, 128))
    per = seg // cb
    chunks = _row_chunks(m)

    def body(x_ref, w_ref, o_ref):
        w = w_ref[...]
        o_ref[pl.ds(0, SUBLANES), :] = jnp.zeros((SUBLANES, cb), F32)
        for s, n in chunks:
            c = _conv_at(lambda a, b: x_ref[pl.ds(a, b), :], w, 4, s, n)
            o_ref[pl.ds(s, n), :] = c * _sigmoid(c)

    return pl.pallas_call(
        body, name=name, grid=(3 * per,),
        in_specs=[pl.BlockSpec((None, m, cb), lambda j: (3 + j // per, 0, j % per)), pl.BlockSpec((SUBLANES, cb), lambda j: (0, j))],
        out_specs=pl.BlockSpec((None, m, cb), lambda j: (j // per, 0, j % per)),
        out_shape=jax.ShapeDtypeStruct((3, m, seg), F32),
        compiler_params=_params(("parallel",)),
    )(pm, cw)


def _dnpre_bwd(dqkv, pm, cw, dpm, *, name):
    _, m, seg = pm.shape
    cb = _pick(seg, (256, 128))
    per = seg // cb
    chunks = _row_chunks(m)

    def body(d_ref, x_ref, w_ref, dpm_in, dp_ref, dw_ref, dc_s):
        del dpm_in
        w = w_ref[...]
        zeros8 = jnp.zeros((SUBLANES, cb), F32)
        dc_s[pl.ds(0, SUBLANES), :] = zeros8
        dc_s[pl.ds(m, SUBLANES), :] = zeros8
        dp_ref[pl.ds(0, SUBLANES), :] = zeros8.astype(dp_ref.dtype)
        load_x = lambda a, b: x_ref[pl.ds(a, b), :]
        dw = jnp.zeros((SUBLANES, cb), F32)
        for s, n in chunks:
            c = _conv_at(load_x, w, 4, s, n)
            sg = _sigmoid(c)
            dc = d_ref[pl.ds(s, n), :] * (sg * (1.0 + c * (1.0 - sg)))
            dc_s[pl.ds(s, n), :] = dc
            dw = dw + _dconv_w(load_x, dc, 4, s, n)
        dw_ref[...] = dw
        for s, n in chunks:
            dp_ref[pl.ds(s, n), :] = _conv_t_at(lambda a, b: dc_s[pl.ds(a, b), :], w, 4, s, n).astype(dp_ref.dtype)

    return pl.pallas_call(
        body, name=name, grid=(3 * per,),
        in_specs=[pl.BlockSpec((None, m, cb), lambda j: (j // per, 0, j % per)),
                  pl.BlockSpec((None, m, cb), lambda j: (3 + j // per, 0, j % per)),
                  pl.BlockSpec((SUBLANES, cb), lambda j: (0, j)), pl.BlockSpec(memory_space=pl.ANY)],
        out_specs=[pl.BlockSpec((None, m, cb), lambda j: (3 + j // per, 0, j % per)), pl.BlockSpec((SUBLANES, cb), lambda j: (0, j))],
        out_shape=[jax.ShapeDtypeStruct(dpm.shape, dpm.dtype), jax.ShapeDtypeStruct((SUBLANES, 3 * seg), F32)],
        scratch_shapes=[pltpu.VMEM((m + SUBLANES, cb), F32)],
        input_output_aliases={3: 0},
        compiler_params=_params(("parallel",)),
    )(dqkv, pm, cw, dpm)


def _mxu_dot_impl(a, b, form):
    a = a.astype(MXU_DTYPE)
    b = b.astype(MXU_DTYPE)
    dims = {"nn": (((1,), (0,)), ((), ())), "nt": (((1,), (1,)), ((), ())), "tn": (((0,), (0,)), ((), ()))}[form]
    return lax.dot_general(a, b, dims, preferred_element_type=F32)


@functools.partial(jax.custom_vjp, nondiff_argnums=(2,))
def _mxu_dot(a, b, form):
    return _mxu_dot_impl(a, b, form)


def _mxu_dot_fwd(a, b, form):
    return _mxu_dot_impl(a, b, form), (a, b)


def _mxu_dot_bwd(form, saved, g):
    a, b = saved
    if form == "nn":
        return _mxu_dot_impl(g, b, "nt"), _mxu_dot_impl(a, g, "tn")
    if form == "nt":
        return _mxu_dot_impl(g, b, "nn"), _mxu_dot_impl(g, a, "tn")
    return _mxu_dot_impl(b, g, "nt"), _mxu_dot_impl(a, g, "nn")


_mxu_dot.defvjp(_mxu_dot_fwd, _mxu_dot_bwd)


_DOT_DIMS = {"nn": (((1,), (0,)), ((), ())), "nt": (((1,), (1,)), ((), ())), "tn": (((0,), (0,)), ((), ()))}


def _split(x):
    hi = x.astype(BF16)
    return hi, (x - hi.astype(F32)).astype(BF16)


def _dot3_impl(a, b, form):
    dg = lambda p, q: lax.dot_general(p, q, _DOT_DIMS[form], preferred_element_type=F32)
    ah, al = _split(a)
    bh, bl = _split(b)
    return dg(ah, bh) + (dg(ah, bl) + dg(al, bh))


@functools.partial(jax.custom_vjp, nondiff_argnums=(2,))
def _dot3(a, b, form):
    return _dot3_impl(a, b, form)


def _dot3_fwd(a, b, form):
    return _dot3_impl(a, b, form), (a, b)


def _dot3_bwd(form, saved, g):
    a, b = saved
    if form == "nn":
        return _dot3_impl(g, b, "nt"), _dot3_impl(a, g, "tn")
    if form == "nt":
        return _dot3_impl(g, b, "nn"), _dot3_impl(g, a, "tn")
    return _dot3_impl(b, g, "nt"), _dot3_impl(a, g, "nn")


_dot3.defvjp(_dot3_fwd, _dot3_bwd)


def _hdot(a, b):
    return _dot3(a, b, "nn")


def _mask_dot(mask, x, form):
    dg = lambda q: lax.dot_general(mask.astype(BF16), q, _DOT_DIMS[form], preferred_element_type=F32)
    x1 = x.astype(BF16)
    r1 = x - x1.astype(F32)
    x2 = r1.astype(BF16)
    x3 = (r1 - x2.astype(F32)).astype(BF16)
    return dg(x1) + (dg(x2) + dg(x3))


def _decay_masks(c):
    row = lax.broadcasted_iota(jnp.int32, (c, c), 0)
    col = lax.broadcasted_iota(jnp.int32, (c, c), 1)
    return (row >= col).astype(F32), row <= col


def _decay_impl(gb):
    lower, upper = _decay_masks(gb.shape[0])
    return _mask_dot(lower, gb, "nn"), _mask_dot(jnp.ones_like(gb), jnp.where(upper, gb, 0.0), "nn")


@jax.custom_vjp
def _decay_matrices(gb):
    return _decay_impl(gb)


def _decay_fwd(gb):
    return _decay_impl(gb), None


def _decay_bwd(_, cts):
    gc, gr = cts
    lower, upper = _decay_masks(gc.shape[0])
    return (_mask_dot(lower, gc, "tn") + jnp.where(upper, _mask_dot(jnp.ones_like(gr), gr, "tn"), 0.0),)


_decay_matrices.defvjp(_decay_fwd, _decay_bwd)


def _softplus(x):
    return jnp.maximum(x, 0.0) + jnp.log(1.0 + jnp.exp(-jnp.abs(x)))


def _heads(f, *lists):
    return [f(*t) for t in zip(*lists)]


DN_STEP_CHUNKS = 3


def _dn_chunk(qr, kr, v, z, braw, araw, alog, dtb, nw, state, valid):
    c = DN_CHUNK
    nh = len(state)
    chunks = len(qr) // nh
    alog, dtb, valid_i = alog * chunks, dtb * chunks, [vv for vv in valid for _ in range(nh)]
    row = lax.broadcasted_iota(jnp.int32, (c, c), 0)
    col = lax.broadcasted_iota(jnp.int32, (c, c), 1)
    incl = row >= col
    strict = row > col
    eye = jnp.where(row == col, 1.0, 0.0)
    q = _heads(lambda t: t * lax.rsqrt(jnp.sum(t * t, -1, keepdims=True) + EPS) * (DN_HEAD_DIM ** -0.5), qr)
    k = _heads(lambda t: t * lax.rsqrt(jnp.sum(t * t, -1, keepdims=True) + EPS), kr)
    beta = _heads(lambda t, vv: _sigmoid(t) * vv, braw, valid_i)
    g = _heads(lambda al, ar, dt, vv: -jnp.exp(al) * _softplus(ar + dt) * vv, alog, araw, dtb, valid_i)
    decay = _heads(lambda t: _decay_matrices(jnp.broadcast_to(t, (c, c))), g)
    dmask = _heads(lambda d: jnp.where(incl, jnp.exp(jnp.where(incl, d[0] - d[1], 0.0)), 0.0), decay)
    dec = _heads(lambda d: d[0][:, :1], decay)
    dlast = _heads(lambda d: d[0][c - 1:c, :1], decay)
    kk = _heads(lambda t: _mxu_dot(t, t, "nt"), k)
    a = _heads(lambda b, t, d: jnp.where(strict, b * t * d, 0.0), beta, kk, dmask)
    x = _heads(lambda t: eye - t, a)
    p = _heads(_hdot, a, a)
    for it in range(5):
        x = _heads(lambda s, t: s + _hdot(s, t), x, p)
        if it < 4:
            p = _heads(_hdot, p, p)
    u = _heads(lambda s, t, b: _hdot(s, t * b), x, v, beta)
    w = _heads(lambda s, t, b, d: _hdot(s, t * (b * jnp.exp(d))), x, k, beta, dec)
    qk = _heads(lambda s, t, d: _mxu_dot(s, t, "nt") * d, q, k, dmask)
    q_dec = _heads(lambda t, d: t * jnp.exp(d), q, dec)
    k_dec = _heads(lambda t, dl, d: t * jnp.exp(dl - d), k, dlast, dec)
    o = []
    for ci in range(chunks):
        of = lambda lst: lst[ci * nh:(ci + 1) * nh]
        v_new = _heads(lambda s, t, st: s - _mxu_dot(t, st, "nn"), of(u), of(w), state)
        o += _heads(lambda qd, st, s, vn: _mxu_dot(qd, st, "nn") + _mxu_dot(s, vn, "nn"), of(q_dec), state, of(qk), v_new)
        state = _heads(lambda st, dl, kd, vn: st * jnp.exp(dl) + _mxu_dot(kd, vn, "tn"), state, of(dlast), of(k_dec), v_new)
    y = _heads(lambda t, zz: _rms(t, nw) * (zz * _sigmoid(zz)), o, z)
    return y, state


DN_STEP_ROWS = DN_STEP_CHUNKS * DN_CHUNK
DN_ITEMS = [(ci, h) for ci in range(DN_STEP_CHUNKS) for h in range(DN_HEADS)]


def _dn_valid(n):
    rows = [n * DN_STEP_ROWS + ci * DN_CHUNK + lax.broadcasted_iota(jnp.int32, (DN_CHUNK, 1), 0) for ci in range(DN_STEP_CHUNKS)]
    return [(r >= PAD_ROWS).astype(F32) for r in rows]


def _dn_in_specs(rev, nc):
    cn = (lambda n: nc - 1 - n) if rev else (lambda n: n)
    c, hd = DN_STEP_ROWS, DN_HEAD_DIM
    return [
        pl.BlockSpec((None, c, DN_DIM), lambda n: (0, cn(n), 0)),
        pl.BlockSpec((None, c, DN_DIM), lambda n: (1, cn(n), 0)),
        pl.BlockSpec((None, c, DN_DIM), lambda n: (2, cn(n), 0)),
        pl.BlockSpec((None, c, DN_DIM), lambda n: (6, cn(n), 0)),
        pl.BlockSpec((DN_HEADS, 2, c, 1), lambda n: (0, 0, cn(n), 0)),
        pl.BlockSpec((DN_HEADS, SUBLANES, LANES), lambda n: (0, 0, 0)),
        pl.BlockSpec((1, hd), lambda n: (0, 0)),
    ]


def _rows(ci):
    return slice(ci * DN_CHUNK, (ci + 1) * DN_CHUNK)


def _cols(h):
    return slice(h * DN_HEAD_DIM, (h + 1) * DN_HEAD_DIM)


def _dn_load(q_ref, k_ref, v_ref, z_ref, ba_ref, hp_ref):
    heads = range(DN_HEADS)
    item = lambda ref: [ref[_rows(ci), _cols(h)] for ci, h in DN_ITEMS]
    return (item(q_ref), item(k_ref), item(v_ref), item(z_ref),
            [ba_ref[h, 0, _rows(ci), :] for ci, h in DN_ITEMS], [ba_ref[h, 1, _rows(ci), :] for ci, h in DN_ITEMS],
            [hp_ref[h, 0:1, 0:1] for h in heads], [hp_ref[h, 1:2, 0:1] for h in heads])


def _delta_fwd(qkvc, pm, ba, hp, nw, *, name):
    _, m, _ = qkvc.shape
    nc = m // DN_STEP_ROWS
    hd = DN_HEAD_DIM

    def body(q_ref, k_ref, v_ref, z_ref, ba_ref, hp_ref, nw_ref, y_ref, s_ref, state):
        n = pl.program_id(0)

        @pl.when(n == 0)
        def _():
            state[...] = jnp.zeros_like(state)

        heads = range(DN_HEADS)
        old = [state[h] for h in heads]
        y, new = _dn_chunk(*_dn_load(q_ref, k_ref, v_ref, z_ref, ba_ref, hp_ref), nw_ref[...], old, _dn_valid(n))
        for h in heads:
            s_ref[h] = old[h]
            state[h] = new[h]
        for (ci, h), yy in zip(DN_ITEMS, y):
            y_ref[_rows(ci), _cols(h)] = yy.astype(y_ref.dtype)

    return pl.pallas_call(
        body, name=name, grid=(nc,), in_specs=_dn_in_specs(False, nc),
        out_specs=[pl.BlockSpec((DN_STEP_ROWS, DN_DIM), lambda n: (n, 1)), pl.BlockSpec((DN_HEADS, None, hd, hd), lambda n: (0, n, 0, 0))],
        out_shape=[jax.ShapeDtypeStruct((m, D_CONV + DN_DIM), MXU_DTYPE), jax.ShapeDtypeStruct((DN_HEADS, nc, hd, hd), F32)],
        scratch_shapes=[pltpu.VMEM((DN_HEADS, hd, hd), F32)],
        compiler_params=_params(("arbitrary",)),
    )(qkvc, qkvc, qkvc, pm, ba, hp, nw)


def _delta_bwd(dy, qkvc, pm, ba, hp, nw, states, *, name):
    _, m, _ = qkvc.shape
    nc = m // DN_STEP_ROWS
    hd, c = DN_HEAD_DIM, DN_STEP_ROWS

    def body(q_ref, k_ref, v_ref, z_ref, ba_ref, hp_ref, nw_ref, s_ref, dy_ref,
             dz_ref, dqkv_ref, dba_ref, dhp_ref, dnw_ref, dstate):
        step = pl.program_id(0)
        n = nc - 1 - step

        @pl.when(step == 0)
        def _():
            dstate[...] = jnp.zeros_like(dstate)
            dhp_ref[...] = jnp.zeros_like(dhp_ref)
            dnw_ref[...] = jnp.zeros_like(dnw_ref)

        valid = _dn_valid(n)
        heads = range(DN_HEADS)
        fn = lambda *a: _dn_chunk(*a, valid)
        _, vjp = jax.vjp(fn, *_dn_load(q_ref, k_ref, v_ref, z_ref, ba_ref, hp_ref), nw_ref[...], [s_ref[h] for h in heads])
        dy = [dy_ref[_rows(ci), _cols(h)] for ci, h in DN_ITEMS]
        dq, dk, dv, dz, dbr, dar, dalog, ddtb, dnw, dst = vjp((dy, [dstate[h] for h in heads]))
        for i, (ci, h) in enumerate(DN_ITEMS):
            dqkv_ref[0, _rows(ci), _cols(h)] = dq[i]
            dqkv_ref[1, _rows(ci), _cols(h)] = dk[i]
            dqkv_ref[2, _rows(ci), _cols(h)] = dv[i]
            dz_ref[_rows(ci), _cols(h)] = dz[i].astype(dz_ref.dtype)
            dba_ref[h, 0, _rows(ci), :] = dbr[i]
            dba_ref[h, 1, _rows(ci), :] = dar[i]
        for h in heads:
            dstate[h] = dst[h]
            dhp_ref[h] += jnp.concatenate([jnp.broadcast_to(dalog[h], (1, LANES)), jnp.broadcast_to(ddtb[h], (1, LANES)),
                                           jnp.zeros((SUBLANES - 2, LANES), F32)], 0)
        dnw_ref[...] += dnw

    rn = lambda n: nc - 1 - n
    in_specs = _dn_in_specs(True, nc) + [
        pl.BlockSpec((DN_HEADS, None, hd, hd), lambda n: (0, rn(n), 0, 0)),
        pl.BlockSpec((c, DN_DIM), lambda n: (rn(n), 1)),
    ]
    out_specs = [
        pl.BlockSpec((None, c, DN_DIM), lambda n: (6, rn(n), 0)),
        pl.BlockSpec((3, c, DN_DIM), lambda n: (0, rn(n), 0)),
        pl.BlockSpec((DN_HEADS, 2, c, 1), lambda n: (0, 0, rn(n), 0)),
        pl.BlockSpec((DN_HEADS, SUBLANES, LANES), lambda n: (0, 0, 0)),
        pl.BlockSpec((1, hd), lambda n: (0, 0)),
    ]
    return pl.pallas_call(
        body, name=name, grid=(nc,), in_specs=in_specs, out_specs=out_specs,
        out_shape=[jax.ShapeDtypeStruct(pm.shape, MXU_DTYPE), jax.ShapeDtypeStruct(qkvc.shape, F32),
                   jax.ShapeDtypeStruct(ba.shape, F32), jax.ShapeDtypeStruct(hp.shape, F32),
                   jax.ShapeDtypeStruct((1, hd), F32)],
        scratch_shapes=[pltpu.VMEM((DN_HEADS, hd, hd), F32)],
        compiler_params=_params(("arbitrary",)),
    )(qkvc, qkvc, qkvc, pm, ba, hp, nw, states, dy)


SWA_PAIR = 2


def _attn_block(q, k0, kp, kc, v0, vp, vc, qw, kw, sink, n):
    g, b, hd = SWA_GROUP, SWA_BLOCK, SWA_HEAD_DIM
    pair = list(range(SWA_PAIR))
    lanes = lambda t, e: t[:, e * hd:(e + 1) * hd]
    q4 = [jnp.concatenate([lanes(q, e * g + i)[None] for i in range(g)], 0) for e in pair]
    qn = _heads(lambda t: _rms(t, qw) * (hd ** -0.5), q4)
    kn = [_rms(jnp.concatenate([lanes(k0, e), lanes(kp, e), lanes(kc, e)], 0), kw) for e in pair]
    vcat = [jnp.concatenate([lanes(v0, e), lanes(vp, e), lanes(vc, e)], 0) for e in pair]
    s = _heads(lambda a, k: _mxu_dot(a.reshape(g * b, hd), k, "nt").reshape(g, b, 3 * b), qn, kn)
    i = lax.broadcasted_iota(jnp.int32, (b, 3 * b), 0)
    c = lax.broadcasted_iota(jnp.int32, (b, 3 * b), 1)
    in_meta, in_prev, in_cur = c < b, (c >= b) & (c < 2 * b), c >= 2 * b
    j = c - jnp.where(in_meta, 0, jnp.where(in_prev, b, 2 * b))
    meta_lo = jnp.where(n == 0, b, PAD_ROWS)
    cur_lo = jnp.where(n == 0, PAD_ROWS, 0)
    prev_off = jnp.where(n >= 2, 0, 2 * b)
    valid = (in_meta & (j >= meta_lo)) | (in_prev & (j > i + prev_off)) | (in_cur & (j <= i) & (j >= cur_lo))
    s = _heads(lambda t: jnp.where(valid[None], t, NEG), s)
    m = [lax.stop_gradient(jnp.maximum(jnp.max(t, -1, keepdims=True), sink[e])) for e, t in zip(pair, s)]
    ex = _heads(lambda t, mm: jnp.exp(t - mm), s, m)
    p = [t / (jnp.sum(t, -1, keepdims=True) + jnp.exp(sink[e] - mm)) for e, t, mm in zip(pair, ex, m)]
    o = _heads(lambda t, v: _mxu_dot(t.reshape(g * b, 3 * b), v, "nn").reshape(g, b, hd), p, vcat)
    return jnp.concatenate([o[e][i] for e in pair for i in range(g)], 1)


Q_LANES = SWA_PAIR * SWA_GROUP * SWA_HEAD_DIM
KV_LANES = SWA_PAIR * SWA_HEAD_DIM
K_BLOCK0 = SWA_HEADS * SWA_HEAD_DIM // KV_LANES
V_BLOCK0 = K_BLOCK0 + SWA_KV_HEADS * SWA_HEAD_DIM // KV_LANES


def _attn_in_specs():
    g, b, hd = SWA_GROUP, SWA_BLOCK, SWA_HEAD_DIM
    kv = lambda f, first: pl.BlockSpec((b, KV_LANES), lambda p, n: (f(n), first + p))
    blocks = [lambda n: 0, lambda n: jnp.maximum(n - 1, 0), lambda n: n]
    return ([pl.BlockSpec((b, Q_LANES), lambda p, n: (n, p))] + [kv(f, K_BLOCK0) for f in blocks] + [kv(f, V_BLOCK0) for f in blocks]
            + [pl.BlockSpec((1, hd), lambda p, n: (0, 0)), pl.BlockSpec((1, hd), lambda p, n: (0, 0)),
               pl.BlockSpec((SWA_PAIR, g, 1, 1), lambda p, n: (p, 0, 0, 0))])


def _attn_fwd(qkv, qw, kw, sink, *, name):
    m = qkv.shape[0]
    b = SWA_BLOCK

    def body(q_ref, k0, kp, kc, v0, vp, vc, qw_ref, kw_ref, s_ref, o_ref):
        o_ref[...] = _attn_block(q_ref[...], k0[...], kp[...], kc[...], v0[...], vp[...], vc[...], qw_ref[...], kw_ref[...],
                                 s_ref[...], pl.program_id(1)).astype(o_ref.dtype)

    return pl.pallas_call(
        body, name=name, grid=(SWA_KV_HEADS // SWA_PAIR, m // b), in_specs=_attn_in_specs(),
        out_specs=pl.BlockSpec((b, Q_LANES), lambda p, n: (n, p)),
        out_shape=jax.ShapeDtypeStruct((m, SWA_HEADS * SWA_HEAD_DIM), MXU_DTYPE),
        compiler_params=_params(("parallel", "parallel")),
    )(*([qkv] * 7), qw, kw, sink)


def _attn_bwd(do, qkv, qw, kw, sink, *, name):
    m = qkv.shape[0]
    g, b, hd = SWA_GROUP, SWA_BLOCK, SWA_HEAD_DIM

    def body(q_ref, k0, kp, kc, v0, vp, vc, qw_ref, kw_ref, s_ref, do_ref, dq_ref, dk_ref, dv_ref, dqw_ref, dkw_ref, ds_ref):
        n = pl.program_id(1)

        @pl.when(n == 0)
        def _():
            for r in (dk_ref, dv_ref, dqw_ref, dkw_ref, ds_ref):
                r[...] = jnp.zeros_like(r)

        fn = lambda *a: _attn_block(*a, n)
        _, vjp = jax.vjp(fn, q_ref[...], k0[...], kp[...], kc[...], v0[...], vp[...], vc[...], qw_ref[...], kw_ref[...], s_ref[...])
        dq, dk0, dkp, dkc, dv0, dvp, dvc, dqw, dkw, dsk = vjp(do_ref[...])
        dq_ref[...] = dq
        prev = pl.multiple_of(jnp.maximum(n - 1, 0) * b, b)
        cur = pl.multiple_of(n * b, b)
        for ref, parts in ((dk_ref, (dk0, dkp, dkc)), (dv_ref, (dv0, dvp, dvc))):
            ref[pl.ds(0, b), :] += parts[0]
            ref[pl.ds(prev, b), :] += parts[1]
            ref[pl.ds(cur, b), :] += parts[2]
        dqw_ref[...] += dqw
        dkw_ref[...] += dkw
        ds_ref[...] += dsk

    pairs = SWA_KV_HEADS // SWA_PAIR
    kv_acc = pl.BlockSpec((m, KV_LANES), lambda p, n: (0, p))
    w_acc = pl.BlockSpec((None, 1, hd), lambda p, n: (p, 0, 0))
    kv_shape = jax.ShapeDtypeStruct((m, SWA_KV_HEADS * hd), F32)
    return pl.pallas_call(
        body, name=name, grid=(pairs, m // b),
        in_specs=_attn_in_specs() + [pl.BlockSpec((b, Q_LANES), lambda p, n: (n, p))],
        out_specs=[pl.BlockSpec((b, Q_LANES), lambda p, n: (n, p)), kv_acc, kv_acc, w_acc, w_acc,
                   pl.BlockSpec((SWA_PAIR, g, 1, 1), lambda p, n: (p, 0, 0, 0))],
        out_shape=[jax.ShapeDtypeStruct((m, SWA_HEADS * hd), F32), kv_shape, kv_shape,
                   jax.ShapeDtypeStruct((pairs, 1, hd), F32), jax.ShapeDtypeStruct((pairs, 1, hd), F32),
                   jax.ShapeDtypeStruct(sink.shape, F32)],
        compiler_params=_params(("parallel", "arbitrary")),
    )(*([qkv] * 7), qw, kw, sink, do)


def _loss_bwd(h, target, *, name):
    m, d = h.shape
    b = SWA_BLOCK

    def body(h_ref, t_ref, l_ref, dh_ref):
        i = pl.program_id(0)

        @pl.when(i == 0)
        def _():
            l_ref[...] = jnp.zeros_like(l_ref)
            dh_ref[...] = jnp.zeros_like(dh_ref)

        @pl.when(i > 0)
        def _():
            e = h_ref[...] - t_ref[...]
            dh_ref[...] = e * (1.0 / d)
            l_ref[...] += jnp.sum(jnp.sum(e * e, 0, keepdims=True), 1, keepdims=True) * (0.5 / d)

    return pl.pallas_call(
        body, name=name, grid=(m // b,),
        in_specs=[pl.BlockSpec((b, d), lambda i: (i, 0)), pl.BlockSpec((b, d), lambda i: (jnp.maximum(i - 1, 0), 0))],
        out_specs=[pl.BlockSpec((1, LANES), lambda i: (0, 0)), pl.BlockSpec((b, d), lambda i: (i, 0))],
        out_shape=[jax.ShapeDtypeStruct((1, LANES), F32), jax.ShapeDtypeStruct((m, d), F32)],
        compiler_params=_params(("arbitrary",)),
    )(h, target)


def _ffn_fwd(h, nw, w_up_t, cw, w_down, tag):
    u, hn = _norm_matmul(h, nw, w_up_t, o_seg=D_FF, trans_w=True, name=f"ffn_up_{tag}")
    a = _ffn_act_fwd(u, cw, name=f"ffn_act_{tag}")
    return _mm_nn(a, w_down, res=h, name=f"ffn_down_{tag}"), (h, hn, u, a)


def _ffn_bwd(dh, saved, nw, w_up_t, cw, w_down, tag):
    h, hn, u, a = saved
    da = _mm_nn(dh, w_down, trans_w=True, name=f"ffn_da_{tag}")
    dw_down = _mm_tn(a, dh, name=f"ffn_dwdown_{tag}")
    du, dcw = _ffn_act_bwd(da, u, cw, name=f"ffn_act_bwd_{tag}")
    dhn = _mm_nn(du, w_up_t, a_seg=True, name=f"ffn_dhn_{tag}")
    dw_up_t = _mm_tn(du, hn, a_seg=True, name=f"ffn_dwup_{tag}")
    dh_in, dnw = _rmsnorm_bwd(dhn, h, nw, dh, name=f"ffn_norm_bwd_{tag}")
    return dh_in, dnw, dw_up_t, dcw, dw_down


def _local_step(x, target, w, fetch=None, push=None):
    fetch = fetch or (lambda stage, after: {})
    push = push or (lambda stage, grads: None)
    plus = lambda a, zero: a if zero is None else a + zero
    seq, d = x.shape
    m = PAD_ROWS + N_META + seq
    h0 = jnp.concatenate([jnp.zeros((PAD_ROWS, d), F32), w["meta"], x], 0)

    pm, hn0 = _norm_matmul(h0, w["anw"][0], w["w_in_t"], o_seg=SEG, trans_w=True, n=N_SEG * SEG, name="mix_in")
    pba = _mm_nn(hn0, w["w_in_tail_t"], trans_w=True, name="mix_in_tail")
    qkvc = _dnpre_fwd(pm, w["dcw"], name="dn_conv")
    ba = pba[:, :2 * DN_HEADS].T.reshape(2, DN_HEADS, m, 1).transpose(1, 0, 2, 3)
    y, states = _delta_fwd(qkvc, pm, ba, w["hp"], w["dnw"], name="delta")
    y = _shortconv_fwd(pm, w["caw"], y, name="shortconv")
    w = {**w, **fetch("l0", y)}
    h1 = _mm_nn(y, w["w_out"], res=h0, name="mix_out")
    h2, ffn0 = _ffn_fwd(h1, w["fnw"][0], w["w_up0_t"], w["fcw"][0], w["w_down0"], "l0")

    w = {**w, **fetch("l1", h2)}
    qkv, hn2 = _norm_matmul(h2, w["anw"][1], w["wqkv"], name="attn_qkv")
    o = _attn_fwd(qkv, w["qnw"], w["knw"], w["sink"], name="attn")
    h3 = _mm_nn(o, w["wo"], res=h2, name="attn_out")
    h4, ffn1 = _ffn_fwd(h3, w["fnw"][1], w["w_up1_t"], w["fcw"][1], w["w_down1"], "l1")

    loss, dh4 = _loss_bwd(h4, target, name="loss")

    g = {}
    dh3, dfnw1, dwup1, dfcw1, dwdown1 = _ffn_bwd(dh4, ffn1, w["fnw"][1], w["w_up1_t"], w["fcw"][1], w["w_down1"], "l1")

    do = _mm_nn(dh3, w["wo"], trans_w=True, name="attn_do")
    g["wo"] = _mm_tn(o, dh3, name="attn_dwo")
    dq, dk, dv, dqw, dkw, dsink = _attn_bwd(do, qkv, w["qnw"], w["knw"], w["sink"], name="attn_bwd")
    dqkv = jnp.concatenate([dq, dk, dv], 1).astype(MXU_DTYPE)
    dhn2 = _mm_nn(dqkv, w["wqkv"], trans_w=True, name="attn_dhn")
    g["wqkv"] = _mm_tn(hn2, dqkv, name="attn_dwqkv")
    zero = push("l1", dict(w_up_t=dwup1, w_down=dwdown1, wo=g["wo"], wqkv=g["wqkv"]))
    dh2, danw1 = _rmsnorm_bwd(dhn2, h2, plus(w["anw"][1], zero), dh3, name="attn_norm_bwd")

    dh1, dfnw0, dwup0, dfcw0, dwdown0 = _ffn_bwd(dh2, ffn0, w["fnw"][0], w["w_up0_t"], w["fcw"][0], w["w_down0"], "l0")

    dy = _mm_nn(dh1, w["w_out"], trans_w=True, name="mix_dy")
    g["w_out"] = _mm_tn(y, dh1, name="mix_dwout")
    zero = push("l0", dict(w_up_t=dwup0, w_down=dwdown0, w_out=g["w_out"]))
    dpm, dqkvc, dba, dhp, ddnw = _delta_bwd(dy, qkvc, pm, ba, w["hp"], plus(w["dnw"], zero), states, name="delta_bwd")
    dpm, ddcw = _dnpre_bwd(dqkvc, pm, w["dcw"], dpm, name="dn_conv_bwd")
    dpm, dcaw = _shortconv_bwd(dy, pm, w["caw"], dpm, name="shortconv_bwd")
    dpba = jnp.pad(dba.transpose(1, 0, 2, 3).reshape(2 * DN_HEADS, m).T, ((0, 0), (0, LANES - 2 * DN_HEADS))).astype(MXU_DTYPE)
    g["w_in_t"] = jnp.concatenate([_mm_tn(dpm, hn0, a_seg=True, out_dtype=F32, name="mix_dwin"),
                                   _mm_tn(dpba, hn0, out_dtype=F32, name="mix_dwin_tail")[:N_TAIL]], 0).astype(GRAD_WIRE_DTYPE)
    zero = push("in", dict(w_in_t=g["w_in_t"]))
    dhn0 = _mm_nn(dpba, plus(w["w_in_tail_t"], None if zero is None else zero.astype(MXU_DTYPE)), name="mix_dhn_tail")
    dhn0 = _mm_nn(dpm, w["w_in_t"], res=dhn0, a_seg=True, name="mix_dhn")
    dh0, danw0 = _rmsnorm_bwd(dhn0, h0, w["anw"][0], dh1, name="mix_norm_bwd")

    g.update(
        x=dh0[PAD_ROWS + N_META:], meta=dh0[PAD_ROWS:PAD_ROWS + N_META], anw=[danw0, danw1], fnw=[dfnw0, dfnw1],
        caw=dcaw, dcw=ddcw, hp=dhp, dnw=ddnw, qnw=jnp.sum(dqw, 0), knw=jnp.sum(dkw, 0), sink=dsink,
        w_up_t=[dwup0, dwup1], fcw=[dfcw0, dfcw1], w_down=[dwdown0, dwdown1])
    return loss, g


N_TAIL = 2 * DN_HEADS


def _prepare_early(p):
    n_main = N_SEG * SEG
    w_in_t = p["mix_w_in_t"]
    tail_t = jnp.pad(w_in_t[n_main:], ((0, LANES - N_TAIL), (0, 0)))
    hp = jnp.zeros((DN_HEADS, SUBLANES, LANES), F32)
    hp = hp.at[:, 0, :].set(p["dn_a_log"][0][:, None]).at[:, 1, :].set(p["dn_dt_bias"][0][:, None])
    depth = p["ffn_conv_w"].shape[0]
    return dict(
        meta=p["meta_tokens"], anw=[p["attn_norm_w"][i:i + 1] for i in range(depth)],
        fnw=[p["ffn_norm_w"][i:i + 1] for i in range(depth)],
        w_in_t=w_in_t, w_in_tail_t=tail_t,
        caw=_pad_w(p["conv_a_w"][0]), dcw=_pad_w(p["dn_conv_w"][0]), hp=hp, dnw=p["dn_norm_w"],
        qnw=p["swa_q_norm_w"], knw=p["swa_k_norm_w"], sink=p["swa_sinks"].reshape(SWA_KV_HEADS, SWA_GROUP, 1, 1),
        fcw=[_pad_w(p["ffn_conv_w"][i]) for i in range(depth)])


def _prepare_weights(p):
    return dict(
        _prepare_early(dict(p, mix_w_in_t=p["mix_w_in"][0].T)), w_out=p["mix_w_out"][0], wo=p["swa_wo"][0],
        wqkv=jnp.concatenate([p["swa_wq"][0], p["swa_wk"][0], p["swa_wv"][0]], 1),
        w_up0_t=p["ffn_w_up"][0].T, w_up1_t=p["ffn_w_up"][1].T, w_down0=p["ffn_w_down"][0], w_down1=p["ffn_w_down"][1])


def _small_named(g):
    return dict(
        meta_tokens=g["meta"], attn_norm_w=jnp.concatenate(g["anw"], 0), ffn_norm_w=jnp.concatenate(g["fnw"], 0),
        conv_a_w=g["caw"][None, :3], dn_conv_w=g["dcw"][None, :4],
        dn_a_log=g["hp"][None, :, 0, 0], dn_dt_bias=g["hp"][None, :, 1, 0], dn_norm_w=g["dnw"],
        swa_q_norm_w=g["qnw"], swa_k_norm_w=g["knw"], swa_sinks=g["sink"].reshape(1, SWA_HEADS),
        ffn_conv_w=jnp.stack([c[:3] for c in g["fcw"]]))


def _reference_named(g):
    nq, nkv = SWA_HEADS * SWA_HEAD_DIM, SWA_KV_HEADS * SWA_HEAD_DIM
    return dict(
        _small_named(g), mix_w_in=g["w_in_t"].T[None],
        mix_w_out=g["w_out"][None], swa_wq=g["wqkv"][None, :, :nq], swa_wk=g["wqkv"][None, :, nq:nq + nkv],
        swa_wv=g["wqkv"][None, :, nq + nkv:], swa_wo=g["wo"][None],
        ffn_w_up=jnp.stack([t.T for t in g["w_up_t"]]), ffn_w_down=jnp.stack(g["w_down"]))


def _my_index():
    return 4 * lax.axis_index("x") + 2 * lax.axis_index("y") + lax.axis_index("c")


def _all_gather(arrays, *, name):
    n = len(arrays)

    def body(*refs):
        ins, outs = refs[:n], refs[n:2 * n]
        send_sems, recv_sems, local_sems = refs[2 * n:]
        x, y, c = lax.axis_index("x"), lax.axis_index("y"), lax.axis_index("c")
        me, sibling = (x, y, c), (x, y, 1 - c)
        chips = [(1 - x, y), (x, 1 - y), (1 - x, 1 - y)]

        def copy(i, k, block, to, src=None):
            rows = outs[i].at[4 * block[0] + 2 * block[1] + block[2]]
            return pltpu.make_async_remote_copy(
                src_ref=rows if src is None else src, dst_ref=rows, send_sem=send_sems.at[i, k], recv_sem=recv_sems.at[i, k],
                device_id=to, device_id_type=pl.DeviceIdType.MESH)

        mine = [pltpu.make_async_copy(ins[i], outs[i].at[4 * x + 2 * y + c], local_sems.at[i]) for i in range(n)]
        first = []
        for j, chip in enumerate(chips):
            first += [copy(i, 1 + j, me, (*chip, c), src=ins[i]) for i in range(n)]
        first += [copy(i, 0, me, sibling, src=ins[i]) for i in range(n)]
        for cp in first + mine:
            cp.start()
        passed = []
        for j, chip in enumerate(chips):
            for i in range(n):
                copy(i, 1 + j, (*chip, c), me).wait_recv()
                fwd = copy(i, 4 + j, (*chip, c), sibling)
                fwd.start()
                passed.append(fwd)
        for i in range(n):
            copy(i, 0, sibling, me).wait_recv()
            for j, chip in enumerate(chips):
                copy(i, 4 + j, (*chip, 1 - c), me).wait_recv()
        for cp in first + passed:
            cp.wait_send()
        for cp in mine:
            cp.wait()

    hbm = pl.BlockSpec(memory_space=pl.ANY)
    return pl.pallas_call(
        body, name=name, in_specs=[hbm] * n, out_specs=[hbm] * n,
        out_shape=[jax.ShapeDtypeStruct((N_DEV,) + tuple(a.shape), a.dtype) for a in arrays],
        scratch_shapes=[pltpu.SemaphoreType.DMA((n, 7)), pltpu.SemaphoreType.DMA((n, 7)), pltpu.SemaphoreType.DMA((n,))],
    )(*arrays)


def _peer(d):
    px, py, pc = lax.axis_index("x") ^ (d >> 2), lax.axis_index("y") ^ ((d >> 1) & 1), lax.axis_index("c") ^ (d & 1)
    return (px, py, pc), 4 * px + 2 * py + pc


def _push_copies(mode, srcs, lands, send_sems, recv_sems):
    me = _my_index()
    out = []
    for d in range(1, N_DEV):
        pos, idx = _peer(d)
        for i in range(len(srcs)):
            out.append(pltpu.make_async_remote_copy(
                src_ref=srcs[i] if mode == "gather" else srcs[i].at[idx], dst_ref=lands[i].at[me],
                send_sem=send_sems.at[i * N_DEV + d], recv_sem=recv_sems.at[i * N_DEV + d], device_id=pos,
                device_id_type=pl.DeviceIdType.MESH))
    return out


_HBM = pl.BlockSpec(memory_space=pltpu.HBM)
_SEM = pl.BlockSpec(memory_space=pltpu.SEMAPHORE)


def _push_start(mode, arrays, follows, *, name):
    n = len(arrays)
    blocks = [a.shape if mode == "gather" else a.shape[1:] for a in arrays]
    lands = [lax.empty((N_DEV,) + tuple(b), a.dtype) for a, b in zip(arrays, blocks)]

    def body(*refs):
        srcs, land_refs = refs[:n], refs[n:2 * n]
        send_sems, recv_sems = refs[2 * n + 1], refs[2 * n + 2]
        zero = refs[-1]
        for cp in _push_copies(mode, srcs, land_refs, send_sems, recv_sems):
            cp.start()
        zero[...] = jnp.zeros_like(zero)

    hbm_in = [pltpu.with_memory_space_constraint(a, pltpu.HBM) for a in list(arrays) + lands]
    outs = pl.pallas_call(
        body, name=name,
        out_shape=[pltpu.SemaphoreType.DMA((n * N_DEV,)), pltpu.SemaphoreType.DMA((n * N_DEV,))]
        + [pltpu.HBM(a.shape, a.dtype) for a in hbm_in] + [jax.ShapeDtypeStruct((SUBLANES, LANES), F32)],
        in_specs=[_HBM] * (2 * n) + [pl.BlockSpec(memory_space=pl.ANY)],
        out_specs=[_SEM, _SEM] + [_HBM] * (2 * n) + [pl.BlockSpec(memory_space=pltpu.VMEM)],
        input_output_aliases={i: 2 + i for i in range(2 * n)},
        compiler_params=pltpu.CompilerParams(has_side_effects=pltpu.SideEffectType.DATAFLOW_SIDE_EFFECTING),
    )(*hbm_in, follows)
    return dict(mode=mode, sems=outs[:2], srcs=outs[2:2 + n], lands=outs[2 + n:2 + 2 * n], zero=outs[-1])


def _push_wait(push, follows, *, name):
    n = len(push["srcs"])
    mode = push["mode"]

    def body(*refs):
        srcs, land_refs = refs[:n], refs[n:2 * n]
        send_sems, recv_sems = refs[2 * n], refs[2 * n + 1]
        for cp in _push_copies(mode, srcs, land_refs, send_sems, recv_sems):
            cp.wait_send()
            cp.wait_recv()

    args = list(push["srcs"]) + list(push["lands"])
    outs = pl.pallas_call(
        body, name=name, out_shape=[pltpu.HBM(a.shape, a.dtype) for a in args],
        in_specs=[_HBM] * (2 * n) + [_SEM, _SEM, pl.BlockSpec(memory_space=pl.ANY)], out_specs=[_HBM] * (2 * n),
        input_output_aliases={i: i for i in range(2 * n)},
        compiler_params=pltpu.CompilerParams(has_side_effects=pltpu.SideEffectType.DATAFLOW_SIDE_EFFECTING),
    )(*args, *push["sems"], follows)
    me = _my_index()
    got = []
    for src, land in zip(outs[:n], outs[n:]):
        own = src if mode == "gather" else lax.dynamic_index_in_dim(src, me, 0, keepdims=False)
        got.append(lax.dynamic_update_index_in_dim(land, own, me, 0))
    return got


ADAMW_BLOCK_BYTES = 6 * 1024 * 1024


def _adamw_tile(r, c):
    fits = lambda tr, tc: N_DEV * tr * tc * 4 <= ADAMW_BLOCK_BYTES
    rows = [t for t in range(2 * SUBLANES, r + 1, 2 * SUBLANES) if r % t == 0 and fits(t, c)]
    if rows or fits(r, c):
        return (max(rows) if rows else r), c
    cols = [t for t in range(LANES, c + 1, LANES) if c % t == 0 and fits(r, t)]
    return r, max(cols)


def _adamw(parts, w, m, v, layer, outs=None, *, name):
    nl, r, c = w.shape
    tr, tc = _adamw_tile(r, c)

    def body(p_ref, w_ref, m_ref, v_ref, *rest):
        g_ref, d_ref, nm_ref, nv_ref = rest[-4:]
        g = p_ref[0].astype(F32)
        for j in range(1, N_DEV):
            g = g + p_ref[j].astype(F32)
        m2 = ADAM_B1 * m_ref[...] + (1.0 - ADAM_B1) * g
        v2 = ADAM_B2 * v_ref[...] + (1.0 - ADAM_B2) * jnp.square(g)
        m_hat = m2 / (1.0 - ADAM_B1 ** ADAM_STEP)
        v_hat = v2 / (1.0 - ADAM_B2 ** ADAM_STEP)
        g_ref[...] = g
        d_ref[...] = -ADAM_LR * (m_hat / (jnp.sqrt(v_hat) + ADAM_EPS) + ADAM_WD * w_ref[...])
        nm_ref[...] = m2
        nv_ref[...] = v2

    blk = pl.BlockSpec((None, tr, tc), lambda i, j: (layer, i, j))
    out = jax.ShapeDtypeStruct((nl, r, c), F32)
    given = list(outs) if outs is not None else []
    return pl.pallas_call(
        body, name=name, grid=(r // tr, c // tc),
        in_specs=[pl.BlockSpec((N_DEV, tr, tc), lambda i, j: (0, i, j)), blk, blk, blk] + [pl.BlockSpec(memory_space=pl.ANY)] * len(given),
        out_specs=[blk, blk, blk, blk], out_shape=[out, out, out, out],
        input_output_aliases={4 + t: t for t in range(len(given))},
        compiler_params=_params(("parallel", "parallel")),
    )(parts, w, m, v, *given)


SHARD_AXIS = dict(
    meta_tokens=1, attn_norm_w=None, ffn_norm_w=None, mix_w_in=2, conv_a_w=2, dn_conv_w=2, dn_a_log=None, dn_dt_bias=None,
    dn_norm_w=None, mix_w_out=1, swa_wq=1, swa_wk=1, swa_wv=1, swa_q_norm_w=None, swa_k_norm_w=None, swa_sinks=None,
    swa_wo=1, ffn_w_up=2, ffn_conv_w=2, ffn_w_down=1)
WEIGHTS = list(SHARD_AXIS)
BIG = ["mix_w_in", "mix_w_out", "swa_wq", "swa_wk", "swa_wv", "swa_wo", "ffn_w_up", "ffn_w_down"]
SMALL = [k for k in WEIGHTS if k not in BIG]
SMALL_SHARDED = [k for k in SMALL if SHARD_AXIS[k] is not None]


def _whole(g8, axis):
    t = jnp.moveaxis(g8, 0, axis)
    return t.reshape(t.shape[:axis] + (t.shape[axis] * t.shape[axis + 1],) + t.shape[axis + 2:])


def _by_owner(a, axis):
    s = a.shape[axis] // N_DEV
    return jnp.moveaxis(a.reshape(a.shape[:axis] + (N_DEV, s) + a.shape[axis + 1:]), axis, 0)


def _pack(arrays, lead=0):
    flat = jnp.concatenate([a.reshape(a.shape[:lead] + (-1,)) for a in arrays], -1)
    n = flat.shape[-1]
    rows = -(-n // (SUBLANES * LANES)) * SUBLANES
    flat = jnp.pad(flat, [(0, 0)] * lead + [(0, rows * LANES - n)])
    return flat.reshape(flat.shape[:lead] + (rows, LANES))


def _unpack(buf, shapes, lead=0):
    flat = buf.reshape(buf.shape[:lead] + (-1,))
    out, o = [], 0
    for s in shapes:
        n = 1
        for e in s:
            n *= e
        out.append(flat[..., o:o + n].reshape(buf.shape[:lead] + tuple(s)))
        o += n
    return out


def kernel(x, meta_tokens, attn_norm_w, ffn_norm_w, mix_w_in, conv_a_w, dn_conv_w, dn_a_log, dn_dt_bias, dn_norm_w, mix_w_out, swa_wq, swa_wk, swa_wv, swa_q_norm_w, swa_k_norm_w, swa_sinks, swa_wo, ffn_w_up, ffn_conv_w, ffn_w_down, loss_target, m_meta_tokens, m_attn_norm_w, m_ffn_norm_w, m_mix_w_in, m_conv_a_w, m_dn_conv_w, m_dn_a_log, m_dn_dt_bias, m_dn_norm_w, m_mix_w_out, m_swa_wq, m_swa_wk, m_swa_wv, m_swa_q_norm_w, m_swa_k_norm_w, m_swa_sinks, m_swa_wo, m_ffn_w_up, m_ffn_conv_w, m_ffn_w_down, v_meta_tokens, v_attn_norm_w, v_ffn_norm_w, v_mix_w_in, v_conv_a_w, v_dn_conv_w, v_dn_a_log, v_dn_dt_bias, v_dn_norm_w, v_mix_w_out, v_swa_wq, v_swa_wk, v_swa_wv, v_swa_q_norm_w, v_swa_k_norm_w, v_swa_sinks, v_swa_wo, v_ffn_w_up, v_ffn_conv_w, v_ffn_w_down):
    w = dict(meta_tokens=meta_tokens, attn_norm_w=attn_norm_w, ffn_norm_w=ffn_norm_w, mix_w_in=mix_w_in, conv_a_w=conv_a_w, dn_conv_w=dn_conv_w, dn_a_log=dn_a_log, dn_dt_bias=dn_dt_bias, dn_norm_w=dn_norm_w, mix_w_out=mix_w_out, swa_wq=swa_wq, swa_wk=swa_wk, swa_wv=swa_wv, swa_q_norm_w=swa_q_norm_w, swa_k_norm_w=swa_k_norm_w, swa_sinks=swa_sinks, swa_wo=swa_wo, ffn_w_up=ffn_w_up, ffn_conv_w=ffn_conv_w, ffn_w_down=ffn_w_down)
    mom = dict(meta_tokens=m_meta_tokens, attn_norm_w=m_attn_norm_w, ffn_norm_w=m_ffn_norm_w, mix_w_in=m_mix_w_in, conv_a_w=m_conv_a_w, dn_conv_w=m_dn_conv_w, dn_a_log=m_dn_a_log, dn_dt_bias=m_dn_dt_bias, dn_norm_w=m_dn_norm_w, mix_w_out=m_mix_w_out, swa_wq=m_swa_wq, swa_wk=m_swa_wk, swa_wv=m_swa_wv, swa_q_norm_w=m_swa_q_norm_w, swa_k_norm_w=m_swa_k_norm_w, swa_sinks=m_swa_sinks, swa_wo=m_swa_wo, ffn_w_up=m_ffn_w_up, ffn_conv_w=m_ffn_conv_w, ffn_w_down=m_ffn_w_down)
    var = dict(meta_tokens=v_meta_tokens, attn_norm_w=v_attn_norm_w, ffn_norm_w=v_ffn_norm_w, mix_w_in=v_mix_w_in, conv_a_w=v_conv_a_w, dn_conv_w=v_dn_conv_w, dn_a_log=v_dn_a_log, dn_dt_bias=v_dn_dt_bias, dn_norm_w=v_dn_norm_w, mix_w_out=v_mix_w_out, swa_wq=v_swa_wq, swa_wk=v_swa_wk, swa_wv=v_swa_wv, swa_q_norm_w=v_swa_q_norm_w, swa_k_norm_w=v_swa_k_norm_w, swa_sinks=v_swa_sinks, swa_wo=v_swa_wo, ffn_w_up=v_ffn_w_up, ffn_conv_w=v_ffn_conv_w, ffn_w_down=v_ffn_w_down)
    me = _my_index()

    transposed = ("mix_w_in", "ffn_w_up")
    view = lambda k, a: jnp.swapaxes(a, 1, 2) if k in transposed else a
    axis2d = {k: 0 if k in transposed else SHARD_AXIS[k] - 1 for k in BIG}
    shard16 = {k: view(k, w[k]).astype(MXU_DTYPE) for k in BIG}
    small_shard_shapes = [w[k].shape for k in SMALL_SHARDED]
    got = _all_gather([shard16["mix_w_in"][0], _pack([w[k] for k in SMALL_SHARDED])], name="gather_weights")
    whole = {"mix_w_in_t": _whole(got[0], 0)}
    for k, a in zip(SMALL_SHARDED, _unpack(got[1], small_shard_shapes, lead=1)):
        whole[k] = _whole(a, SHARD_AXIS[k])
    for k in SMALL:
        whole.setdefault(k, w[k])
    stages = {"in": [("mix_w_in", 0)], "l0": [("mix_w_out", 0), ("ffn_w_up", 0), ("ffn_w_down", 0)],
              "l1": [("swa_wq", 0), ("swa_wk", 0), ("swa_wv", 0), ("swa_wo", 0), ("ffn_w_up", 1), ("ffn_w_down", 1)]}
    pushed = {}
    follows = jnp.zeros((SUBLANES, LANES), F32)
    for stage in ("l0", "l1"):
        pushed[stage] = _push_start("gather", [shard16[k][l] for k, l in stages[stage]], follows, name=f"push_weights_{stage}")
        follows = pushed[stage]["zero"]
    early = _prepare_early(whole)
    early["anw"][0] = early["anw"][0] + follows[0, 0]

    def fetch(stage, after):
        got = _push_wait(pushed[stage], after, name=f"wait_weights_{stage}")
        full = {kl: _whole(a, axis2d[kl[0]]) for kl, a in zip(stages[stage], got)}
        if stage == "l0":
            return dict(w_out=full["mix_w_out", 0], w_up0_t=full["ffn_w_up", 0], w_down0=full["ffn_w_down", 0])
        wqkv = jnp.concatenate([full["swa_wq", 0], full["swa_wk", 0], full["swa_wv", 0]], 1)
        return dict(wqkv=wqkv, wo=full["swa_wo", 0], w_up1_t=full["ffn_w_up", 1], w_down1=full["ffn_w_down", 1])

    nq, nkv = SWA_HEADS * SWA_HEAD_DIM, SWA_KV_HEADS * SWA_HEAD_DIM
    grad_pushes = {}

    def push(stage, gd):
        if stage == "in":
            named = {("mix_w_in", 0): gd["w_in_t"]}
        elif stage == "l1":
            named = {("swa_wq", 0): gd["wqkv"][:, :nq], ("swa_wk", 0): gd["wqkv"][:, nq:nq + nkv],
                     ("swa_wv", 0): gd["wqkv"][:, nq + nkv:], ("swa_wo", 0): gd["wo"],
                     ("ffn_w_up", 1): gd["w_up_t"], ("ffn_w_down", 1): gd["w_down"]}
        else:
            named = {("mix_w_out", 0): gd["w_out"], ("ffn_w_up", 0): gd["w_up_t"], ("ffn_w_down", 0): gd["w_down"]}
        sent = [_by_owner(named[kl], axis2d[kl[0]]) for kl in stages[stage]]
        grad_pushes[stage] = _push_start("scatter", sent, jnp.zeros((SUBLANES, LANES), F32), name=f"push_grads_{stage}")
        return grad_pushes[stage]["zero"][0, 0]

    loss, g = _local_step(x[0], loss_target[0], early, fetch, push)
    grads = _small_named(g)

    results = {}

    def update(stage, follows):
        got = _push_wait(grad_pushes[stage], follows, name=f"wait_grads_{stage}")
        for (k, l), parts in zip(stages[stage], got):
            w3, m3, v3 = view(k, w[k]), view(k, mom[k]), view(k, var[k])
            results[k] = _adamw(parts.reshape((N_DEV,) + w3.shape[1:]), w3, m3, v3, l, results.get(k), name=f"adamw_{k}_{l}")
        return results[stages[stage][0][0]][0]

    follows = update("l0", update("l1", g["meta"]))

    small_shapes = [grads[k].shape for k in SMALL]
    (all_small,) = _all_gather([_pack([loss] + [grads[k].astype(F32) for k in SMALL])], name="gather_small_grads")
    loss_parts, *small_parts = _unpack(all_small, [loss.shape] + small_shapes, lead=1)
    mine = []
    for k, p in zip(SMALL, small_parts):
        ax = SHARD_AXIS[k]
        mine.append(p if ax is None else lax.dynamic_slice_in_dim(p, me * w[k].shape[ax], w[k].shape[ax], 1 + ax))
    zero = jnp.zeros(loss.shape, F32)
    packed = [_pack([z] + [d[k] for k in SMALL]) for z, d in ((zero, w), (zero, mom), (zero, var))]
    res = _adamw(_pack([loss_parts] + mine, lead=1), *[t[None] for t in packed], 0, name="adamw_small")
    shapes = [loss.shape] + [w[k].shape for k in SMALL]
    for t, which in zip(res, range(4)):
        for k, a in zip(["loss"] + SMALL, _unpack(t[0], shapes)):
            results.setdefault(k, [None] * 4)[which] = a
    update("in", jnp.maximum(follows[0, :1, :1], res[0][0, :1, :1]))

    outs = [[view(k, results[k][which]) for k in WEIGHTS] for which in range(4)]
    return (results["loss"][0][0, 0], g["x"][None], *outs[0], *outs[1], *outs[2], *outs[3])
```

```python
import functools

import jax
import jax.numpy as jnp
from jax import lax
from jax.experimental import pallas as pl
from jax.experimental.pallas import tpu as pltpu

F32 = jnp.float32
BF16 = jnp.bfloat16
MXU_DTYPE = BF16
GRAD_WIRE_DTYPE = BF16

D_MODEL = 1024
N_META = 16
PAD_ROWS = 112
D_CONV = 512
DN_HEADS = 4
DN_HEAD_DIM = 128
DN_DIM = DN_HEADS * DN_HEAD_DIM
DN_CHUNK = 64
SEG = 512
N_SEG = 7
SWA_HEADS = 16
SWA_KV_HEADS = 4
SWA_GROUP = SWA_HEADS // SWA_KV_HEADS
SWA_HEAD_DIM = 64
SWA_BLOCK = 128
D_FF = 2816
EPS = 1e-6
NEG = -1e30
N_DEV = 8

ADAM_LR = 0.001
ADAM_B1 = 0.9
ADAM_B2 = 0.999
ADAM_EPS = 1e-08
ADAM_WD = 0.01
ADAM_STEP = 10

VMEM_LIMIT_BYTES = 52 * 1024 * 1024
SUBLANES = 8
LANES = 128


def _pick(n, prefs):
    for p in prefs:
        if n % p == 0:
            return p
    return n


def _params(sem, vmem=VMEM_LIMIT_BYTES):
    return pltpu.CompilerParams(dimension_semantics=sem, vmem_limit_bytes=vmem)


def _rms(x, w):
    return x * lax.rsqrt(jnp.mean(x * x, -1, keepdims=True) + EPS) * w


def _norm_matmul(h, nw, w, *, o_seg=None, trans_w=False, n=None, name):
    m, k = h.shape
    n = n or (w.shape[0] if trans_w else w.shape[1])
    tm = _pick(m, (1408, 384, 128))
    tn = _pick(o_seg or n, (1408, 1024, 512, 256, 128))
    dims = _DOT_DIMS["nt" if trans_w else "nn"]

    def body(h_ref, nw_ref, w_ref, o_ref, hn_ref, hn_s):
        @pl.when(pl.program_id(1) == 0)
        def _():
            hn = _rms(h_ref[...], nw_ref[...]).astype(MXU_DTYPE)
            hn_s[...] = hn
            hn_ref[...] = hn

        o_ref[...] = lax.dot_general(hn_s[...], w_ref[...], dims, preferred_element_type=F32)

    if o_seg:
        per = o_seg // tn
        o_shape = jax.ShapeDtypeStruct((n // o_seg, m, o_seg), F32)
        o_spec = pl.BlockSpec((None, tm, tn), lambda i, j: (j // per, i, j % per))
    else:
        o_shape = jax.ShapeDtypeStruct((m, n), F32)
        o_spec = pl.BlockSpec((tm, tn), lambda i, j: (i, j))
    return pl.pallas_call(
        body, name=name, grid=(m // tm, n // tn),
        in_specs=[pl.BlockSpec((tm, k), lambda i, j: (i, 0)), pl.BlockSpec((1, k), lambda i, j: (0, 0)),
                  pl.BlockSpec((tn, k), lambda i, j: (j, 0)) if trans_w else pl.BlockSpec((k, tn), lambda i, j: (0, j))],
        out_specs=[o_spec, pl.BlockSpec((tm, k), lambda i, j: (i, 0))],
        out_shape=[o_shape, jax.ShapeDtypeStruct((m, k), MXU_DTYPE)],
        scratch_shapes=[pltpu.VMEM((tm, k), MXU_DTYPE)],
        compiler_params=_params(("parallel", "arbitrary")),
    )(h, nw, w)


TILE_BUDGET_BYTES = 38 * 1024 * 1024
TILE_SIZES = (4224, 2816, 1792, 1536, 1408, 1024, 512, 256, 128)


def _divisor_tiles(n):
    return [t for t in TILE_SIZES if n % t == 0] or [n]


def _mm_nn(a, w, *, res=None, a_seg=False, trans_w=False, w_seg=False, out_dtype=F32, name):
    if a_seg:
        s, m, seg = a.shape
    else:
        m, seg = a.shape
        s = 1
    k = s * seg
    n = w.shape[0] * w.shape[2] if w_seg else (w.shape[0] if trans_w else w.shape[1])
    n_seg = w.shape[2] if w_seg else n
    tm = _pick(m, (1408, 1024, 512, 384, 256, 128))
    ab = a.dtype.itemsize
    k_steps = [(sb, seg) for sb in range(s, 0, -1) if s % sb == 0] if a_seg else [(1, t) for t in _divisor_tiles(seg)]
    best = None
    for tn in _divisor_tiles(n_seg):
        for sb, tk1 in k_steps:
            tk = sb * tk1
            nk = k // tk
            need = 2 * tm * tk * ab + 2 * tk * tn * 2 + 2 * tm * tn * 4 + (tm * tn * 4 if nk > 1 else 0) + (2 * tm * tn * 4 if res is not None else 0)
            if need <= TILE_BUDGET_BYTES and (best is None or tk * tn > best[0]):
                best = (tk * tn, tn, sb, tk1)
    _, tn, sb, tk1 = best
    tk = sb * tk1
    nk = k // tk
    w_dims = _DOT_DIMS["nt" if trans_w else "nn"]

    def body(*refs):
        a_ref, w_ref = refs[:2]
        r_ref = refs[2] if res is not None else None
        o_ref = refs[3 if res is not None else 2]

        def partial_product():
            if not a_seg:
                return lax.dot_general(a_ref[...].astype(MXU_DTYPE), w_ref[...], w_dims, preferred_element_type=F32)
            out = None
            for t in range(sb):
                wt = w_ref[:, t * seg:(t + 1) * seg] if trans_w else w_ref[t * seg:(t + 1) * seg, :]
                d = lax.dot_general(a_ref[t].astype(MXU_DTYPE), wt, w_dims, preferred_element_type=F32)
                out = d if out is None else out + d
            return out

        if nk == 1:
            o_ref[...] = (partial_product() if res is None else partial_product() + r_ref[...]).astype(o_ref.dtype)
            return
        acc = refs[-1]
        kk = pl.program_id(2)

        @pl.when(kk == 0)
        def _():
            acc[...] = jnp.zeros_like(acc)

        acc[...] += partial_product()

        @pl.when(kk == nk - 1)
        def _():
            o_ref[...] = (acc[...] if res is None else acc[...] + r_ref[...]).astype(o_ref.dtype)

    a_spec = pl.BlockSpec((sb, tm, seg), lambda i, j, kk: (kk, i, 0)) if a_seg else pl.BlockSpec((tm, tk), lambda i, j, kk: (i, kk))
    if w_seg:
        per = n_seg // tn
        w_spec = pl.BlockSpec((None, tk, tn), lambda i, j, kk: (j // per, kk, j % per))
    elif trans_w:
        w_spec = pl.BlockSpec((tn, tk), lambda i, j, kk: (j, kk))
    else:
        w_spec = pl.BlockSpec((tk, tn), lambda i, j, kk: (kk, j))
    in_specs = [a_spec, w_spec]
    args = [a, w]
    if res is not None:
        in_specs.append(pl.BlockSpec((tm, tn), lambda i, j, kk: (i, j)))
        args.append(res)
    return pl.pallas_call(
        body, name=name, grid=(m // tm, n // tn, nk), in_specs=in_specs,
        out_specs=pl.BlockSpec((tm, tn), lambda i, j, kk: (i, j)),
        out_shape=jax.ShapeDtypeStruct((m, n), out_dtype),
        scratch_shapes=[pltpu.VMEM((tm, tn), F32)] if nk > 1 else [],
        compiler_params=_params(("parallel", "parallel", "arbitrary")),
    )(*args)


def _mm_tn(a, b, *, a_seg=False, b_seg=False, out_dtype=None, name):
    out_dtype = out_dtype or GRAD_WIRE_DTYPE
    if a_seg:
        sa, m, a_unit = a.shape
        ka = sa * a_unit
    else:
        m, ka = a.shape
        a_unit = ka
    if b_seg:
        s, _, seg = b.shape
        n = s * seg
    else:
        n = b.shape[1]
        seg = n
    tmc = _pick(m, (1408, 384, 128))
    best = None
    for tka in _divisor_tiles(a_unit):
        for tn in _divisor_tiles(seg):
            need = 2 * tmc * tka * a.dtype.itemsize + 2 * tmc * tn * b.dtype.itemsize + tka * tn * 4 + 2 * tka * tn * 4
            if need <= TILE_BUDGET_BYTES and (best is None or (tka * tn, tn) > best[:2]):
                best = (tka * tn, tn, tka)
    _, tn, tka = best
    nm = m // tmc

    def body(a_ref, b_ref, o_ref, acc):
        mm = pl.program_id(2)

        @pl.when(mm == 0)
        def _():
            acc[...] = jnp.zeros_like(acc)

        acc[...] += lax.dot_general(a_ref[...].astype(MXU_DTYPE), b_ref[...].astype(MXU_DTYPE),
                                    (((0,), (0,)), ((), ())), preferred_element_type=F32)

        @pl.when(mm == nm - 1)
        def _():
            o_ref[...] = acc[...].astype(o_ref.dtype)

    if b_seg:
        per = seg // tn
        b_spec = pl.BlockSpec((None, tmc, tn), lambda i, j, mm: (j // per, mm, j % per))
    else:
        b_spec = pl.BlockSpec((tmc, tn), lambda i, j, mm: (mm, j))
    if a_seg:
        a_per = a_unit // tka
        a_spec = pl.BlockSpec((None, tmc, tka), lambda i, j, mm: (i // a_per, mm, i % a_per))
    else:
        a_spec = pl.BlockSpec((tmc, tka), lambda i, j, mm: (mm, i))
    return pl.pallas_call(
        body, name=name, grid=(ka // tka, n // tn, nm),
        in_specs=[a_spec, b_spec],
        out_specs=pl.BlockSpec((tka, tn), lambda i, j, mm: (i, j)),
        out_shape=jax.ShapeDtypeStruct((ka, n), out_dtype),
        scratch_shapes=[pltpu.VMEM((tka, tn), F32)],
        compiler_params=_params(("parallel", "parallel", "arbitrary")),
    )(a, b)


def _rmsnorm_bwd(dhn, h, nw, dres, *, name):
    m, d = h.shape
    tm = _pick(m, (384, 128))

    def body(dhn_ref, h_ref, nw_ref, dres_ref, dh_ref, dnw_ref):
        i = pl.program_id(0)
        x = h_ref[...]
        r = lax.rsqrt(jnp.mean(x * x, -1, keepdims=True) + EPS)
        xh = x * r
        dy = dhn_ref[...]
        dxh = dy * nw_ref[...]
        dx = r * (dxh - xh * jnp.mean(dxh * xh, -1, keepdims=True))
        row = i * tm + lax.broadcasted_iota(jnp.int32, (tm, 1), 0)
        dh_ref[...] = jnp.where(row >= PAD_ROWS, dres_ref[...] + dx, 0.0)

        @pl.when(i == 0)
        def _():
            dnw_ref[...] = jnp.zeros_like(dnw_ref)

        dnw_ref[...] += jnp.sum(dy * xh, 0, keepdims=True)

    return pl.pallas_call(
        body, name=name, grid=(m // tm,),
        in_specs=[pl.BlockSpec((tm, d), lambda i: (i, 0)), pl.BlockSpec((tm, d), lambda i: (i, 0)),
                  pl.BlockSpec((1, d), lambda i: (0, 0)), pl.BlockSpec((tm, d), lambda i: (i, 0))],
        out_specs=[pl.BlockSpec((tm, d), lambda i: (i, 0)), pl.BlockSpec((1, d), lambda i: (0, 0))],
        out_shape=[jax.ShapeDtypeStruct((m, d), F32), jax.ShapeDtypeStruct((1, d), F32)],
        compiler_params=_params(("arbitrary",)),
    )(dhn, h, nw, dres)


ROW_CHUNK = 248


def _row_chunks(m):
    out, s = [], SUBLANES
    while s < m:
        n = min(ROW_CHUNK, m - s)
        out.append((s, n))
        s += n
    return out


def _conv_at(load, w, width, s, n):
    acc = w[width - 1:width, :] * load(s, n)
    for j in range(width - 1):
        acc = acc + w[j:j + 1, :] * load(s - (width - 1 - j), n)
    return acc


def _conv_t_at(load, w, width, s, n):
    acc = w[width - 1:width, :] * load(s, n)
    for j in range(width - 1):
        acc = acc + w[j:j + 1, :] * load(s + (width - 1 - j), n)
    return acc


def _dconv_w(load_x, d, width, s, n):
    rows = [jnp.sum(d * load_x(s - (width - 1 - j), n), 0, keepdims=True) for j in range(width)]
    rows.append(jnp.zeros((SUBLANES - width, d.shape[1]), F32))
    return jnp.concatenate(rows, 0)


def _pad_w(w):
    return jnp.concatenate([w, jnp.zeros((SUBLANES - w.shape[0], w.shape[1]), w.dtype)], 0)


def _sigmoid(x):
    return 1.0 / (1.0 + jnp.exp(-x))


def _ffn_act_fwd(u, cw, *, name):
    _, m, f = u.shape
    cb = _pick(f, (256, 128))
    chunks = _row_chunks(m)

    def body(g_ref, v_ref, w_ref, o_ref):
        w = w_ref[...]
        o_ref[pl.ds(0, SUBLANES), :] = jnp.zeros((SUBLANES, cb), o_ref.dtype)
        for s, n in chunks:
            c = _conv_at(lambda a, b: g_ref[pl.ds(a, b), :], w, 3, s, n)
            o_ref[pl.ds(s, n), :] = (c * _sigmoid(c) * v_ref[pl.ds(s, n), :]).astype(o_ref.dtype)

    return pl.pallas_call(
        body, name=name, grid=(f // cb,),
        in_specs=[pl.BlockSpec((None, m, cb), lambda j: (0, 0, j)), pl.BlockSpec((None, m, cb), lambda j: (1, 0, j)),
                  pl.BlockSpec((SUBLANES, cb), lambda j: (0, j))],
        out_specs=pl.BlockSpec((m, cb), lambda j: (0, j)),
        out_shape=jax.ShapeDtypeStruct((m, f), MXU_DTYPE),
        compiler_params=_params(("parallel",)),
    )(u, u, cw)


def _ffn_act_bwd(da, u, cw, *, name):
    _, m, f = u.shape
    cb = LANES
    chunks = _row_chunks(m)

    def body(da_ref, g_ref, v_ref, w_ref, du_ref, dw_ref, dg_s):
        w = w_ref[...]
        zeros8 = jnp.zeros((SUBLANES, cb), F32)
        dg_s[pl.ds(0, SUBLANES), :] = zeros8
        dg_s[pl.ds(m, SUBLANES), :] = zeros8
        du_ref[0, pl.ds(0, SUBLANES), :] = zeros8.astype(du_ref.dtype)
        du_ref[1, pl.ds(0, SUBLANES), :] = zeros8.astype(du_ref.dtype)
        load_g = lambda a, b: g_ref[pl.ds(a, b), :]
        dw = jnp.zeros((SUBLANES, cb), F32)
        for s, n in chunks:
            c = _conv_at(load_g, w, 3, s, n)
            sg = _sigmoid(c)
            d = da_ref[pl.ds(s, n), :]
            du_ref[1, pl.ds(s, n), :] = (d * (c * sg)).astype(du_ref.dtype)
            dc = d * v_ref[pl.ds(s, n), :] * (sg * (1.0 + c * (1.0 - sg)))
            dg_s[pl.ds(s, n), :] = dc
            dw = dw + _dconv_w(load_g, dc, 3, s, n)
        dw_ref[...] = dw
        for s, n in chunks:
            du_ref[0, pl.ds(s, n), :] = _conv_t_at(lambda a, b: dg_s[pl.ds(a, b), :], w, 3, s, n).astype(du_ref.dtype)

    return pl.pallas_call(
        body, name=name, grid=(f // cb,),
        in_specs=[pl.BlockSpec((m, cb), lambda j: (0, j)), pl.BlockSpec((None, m, cb), lambda j: (0, 0, j)),
                  pl.BlockSpec((None, m, cb), lambda j: (1, 0, j)), pl.BlockSpec((SUBLANES, cb), lambda j: (0, j))],
        out_specs=[pl.BlockSpec((2, m, cb), lambda j: (0, 0, j)), pl.BlockSpec((SUBLANES, cb), lambda j: (0, j))],
        out_shape=[jax.ShapeDtypeStruct((2, m, f), MXU_DTYPE), jax.ShapeDtypeStruct((SUBLANES, f), F32)],
        scratch_shapes=[pltpu.VMEM((m + SUBLANES, cb), F32)],
        compiler_params=_params(("parallel",)),
    )(da, u, u, cw)


def _shortconv_fwd(pm, cw, y, *, name):
    _, m, seg = pm.shape
    cb = _pick(seg, (256, 128))
    chunks = _row_chunks(m)

    def body(gi_ref, go_ref, ah_ref, w_ref, y_in, o_ref):
        del y_in
        w = w_ref[...]
        o_ref[pl.ds(0, SUBLANES), :] = jnp.zeros((SUBLANES, cb), o_ref.dtype)
        load_m = lambda a, b: gi_ref[pl.ds(a, b), :] * ah_ref[pl.ds(a, b), :]
        for s, n in chunks:
            o_ref[pl.ds(s, n), :] = (go_ref[pl.ds(s, n), :] * _conv_at(load_m, w, 3, s, n)).astype(o_ref.dtype)

    return pl.pallas_call(
        body, name=name, grid=(seg // cb,),
        in_specs=[pl.BlockSpec((None, m, cb), lambda j: (0, 0, j)), pl.BlockSpec((None, m, cb), lambda j: (1, 0, j)),
                  pl.BlockSpec((None, m, cb), lambda j: (2, 0, j)), pl.BlockSpec((SUBLANES, cb), lambda j: (0, j)),
                  pl.BlockSpec(memory_space=pl.ANY)],
        out_specs=pl.BlockSpec((m, cb), lambda j: (0, j)),
        out_shape=jax.ShapeDtypeStruct(y.shape, y.dtype),
        input_output_aliases={4: 0},
        compiler_params=_params(("parallel",)),
    )(pm, pm, pm, cw, y)


def _shortconv_bwd(dy, pm, cw, dpm, *, name):
    _, m, seg = pm.shape
    cb = LANES
    chunks = _row_chunks(m)

    def body(dy_ref, gi_ref, go_ref, ah_ref, w_ref, dpm_in, dp_ref, dw_ref, dc_s):
        del dpm_in
        w = w_ref[...]
        zeros8 = jnp.zeros((SUBLANES, cb), F32)
        dc_s[pl.ds(0, SUBLANES), :] = zeros8
        dc_s[pl.ds(m, SUBLANES), :] = zeros8
        for t in range(3):
            dp_ref[t, pl.ds(0, SUBLANES), :] = zeros8.astype(dp_ref.dtype)
        load_m = lambda a, b: gi_ref[pl.ds(a, b), :] * ah_ref[pl.ds(a, b), :]
        dw = jnp.zeros((SUBLANES, cb), F32)
        for s, n in chunks:
            d = dy_ref[pl.ds(s, n), :]
            dp_ref[1, pl.ds(s, n), :] = (d * _conv_at(load_m, w, 3, s, n)).astype(dp_ref.dtype)
            dc = d * go_ref[pl.ds(s, n), :]
            dc_s[pl.ds(s, n), :] = dc
            dw = dw + _dconv_w(load_m, dc, 3, s, n)
        dw_ref[...] = dw
        for s, n in chunks:
            dm = _conv_t_at(lambda a, b: dc_s[pl.ds(a, b), :], w, 3, s, n)
            dp_ref[0, pl.ds(s, n), :] = (dm * ah_ref[pl.ds(s, n), :]).astype(dp_ref.dtype)
            dp_ref[2, pl.ds(s, n), :] = (dm * gi_ref[pl.ds(s, n), :]).astype(dp_ref.dtype)

    return pl.pallas_call(
        body, name=name, grid=(seg // cb,),
        in_specs=[pl.BlockSpec((m, cb), lambda j: (0, j)), pl.BlockSpec((None, m, cb), lambda j: (0, 0, j)),
                  pl.BlockSpec((None, m, cb), lambda j: (1, 0, j)), pl.BlockSpec((None, m, cb), lambda j: (2, 0, j)),
                  pl.BlockSpec((SUBLANES, cb), lambda j: (0, j)), pl.BlockSpec(memory_space=pl.ANY)],
        out_specs=[pl.BlockSpec((3, m, cb), lambda j: (0, 0, j)), pl.BlockSpec((SUBLANES, cb), lambda j: (0, j))],
        out_shape=[jax.ShapeDtypeStruct(dpm.shape, dpm.dtype), jax.ShapeDtypeStruct((SUBLANES, seg), F32)],
        scratch_shapes=[pltpu.VMEM((m + SUBLANES, cb), F32)],
        input_output_aliases={5: 0},
        compiler_params=_params(("parallel",)),
    )(dy, pm, pm, pm, cw, dpm)


def _dnpre_fwd(pm, cw, *, name):
    _, m, seg = pm.shape
    cb = _pick(seg, (256, 128))
    per = seg // cb
    chunks = _row_chunks(m)

    def body(x_ref, w_ref, o_ref):
        w = w_ref[...]
        o_ref[pl.ds(0, SUBLANES), :] = jnp.zeros((SUBLANES, cb), F32)
        for s, n in chunks:
            c = _conv_at(lambda a, b: x_ref[pl.ds(a, b), :], w, 4, s, n)
            o_ref[pl.ds(s, n), :] = c * _sigmoid(c)

    return pl.pallas_call(
        body, name=name, grid=(3 * per,),
        in_specs=[pl.BlockSpec((None, m, cb), lambda j: (3 + j // per, 0, j % per)), pl.BlockSpec((SUBLANES, cb), lambda j: (0, j))],
        out_specs=pl.BlockSpec((None, m, cb), lambda j: (j // per, 0, j % per)),
        out_shape=jax.ShapeDtypeStruct((3, m, seg), F32),
        compiler_params=_params(("parallel",)),
    )(pm, cw)


def _dnpre_bwd(dqkv, pm, cw, dpm, *, name):
    _, m, seg = pm.shape
    cb = _pick(seg, (256, 128))
    per = seg // cb
    chunks = _row_chunks(m)

    def body(d_ref, x_ref, w_ref, dpm_in, dp_ref, dw_ref, dc_s):
        del dpm_in
        w = w_ref[...]
        zeros8 = jnp.zeros((SUBLANES, cb), F32)
        dc_s[pl.ds(0, SUBLANES), :] = zeros8
        dc_s[pl.ds(m, SUBLANES), :] = zeros8
        dp_ref[pl.ds(0, SUBLANES), :] = zeros8.astype(dp_ref.dtype)
        load_x = lambda a, b: x_ref[pl.ds(a, b), :]
        dw = jnp.zeros((SUBLANES, cb), F32)
        for s, n in chunks:
            c = _conv_at(load_x, w, 4, s, n)
            sg = _sigmoid(c)
            dc = d_ref[pl.ds(s, n), :] * (sg * (1.0 + c * (1.0 - sg)))
            dc_s[pl.ds(s, n), :] = dc
            dw = dw + _dconv_w(load_x, dc, 4, s, n)
        dw_ref[...] = dw
        for s, n in chunks:
            dp_ref[pl.ds(s, n), :] = _conv_t_at(lambda a, b: dc_s[pl.ds(a, b), :], w, 4, s, n).astype(dp_ref.dtype)

    return pl.pallas_call(
        body, name=name, grid=(3 * per,),
        in_specs=[pl.BlockSpec((None, m, cb), lambda j: (j // per, 0, j % per)),
                  pl.BlockSpec((None, m, cb), lambda j: (3 + j // per, 0, j % per)),
                  pl.BlockSpec((SUBLANES, cb), lambda j: (0, j)), pl.BlockSpec(memory_space=pl.ANY)],
        out_specs=[pl.BlockSpec((None, m, cb), lambda j: (3 + j // per, 0, j % per)), pl.BlockSpec((SUBLANES, cb), lambda j: (0, j))],
        out_shape=[jax.ShapeDtypeStruct(dpm.shape, dpm.dtype), jax.ShapeDtypeStruct((SUBLANES, 3 * seg), F32)],
        scratch_shapes=[pltpu.VMEM((m + SUBLANES, cb), F32)],
        input_output_aliases={3: 0},
        compiler_params=_params(("parallel",)),
    )(dqkv, pm, cw, dpm)


def _mxu_dot_impl(a, b, form):
    a = a.astype(MXU_DTYPE)
    b = b.astype(MXU_DTYPE)
    dims = {"nn": (((1,), (0,)), ((), ())), "nt": (((1,), (1,)), ((), ())), "tn": (((0,), (0,)), ((), ()))}[form]
    return lax.dot_general(a, b, dims, preferred_element_type=F32)


@functools.partial(jax.custom_vjp, nondiff_argnums=(2,))
def _mxu_dot(a, b, form):
    return _mxu_dot_impl(a, b, form)


def _mxu_dot_fwd(a, b, form):
    return _mxu_dot_impl(a, b, form), (a, b)


def _mxu_dot_bwd(form, saved, g):
    a, b = saved
    if form == "nn":
        return _mxu_dot_impl(g, b, "nt"), _mxu_dot_impl(a, g, "tn")
    if form == "nt":
        return _mxu_dot_impl(g, b, "nn"), _mxu_dot_impl(g, a, "tn")
    return _mxu_dot_impl(b, g, "nt"), _mxu_dot_impl(a, g, "nn")


_mxu_dot.defvjp(_mxu_dot_fwd, _mxu_dot_bwd)


_DOT_DIMS = {"nn": (((1,), (0,)), ((), ())), "nt": (((1,), (1,)), ((), ())), "tn": (((0,), (0,)), ((), ()))}


def _split(x):
    hi = x.astype(BF16)
    return hi, (x - hi.astype(F32)).astype(BF16)


def _dot3_impl(a, b, form):
    dg = lambda p, q: lax.dot_general(p, q, _DOT_DIMS[form], preferred_element_type=F32)
    ah, al = _split(a)
    bh, bl = _split(b)
    return dg(ah, bh) + (dg(ah, bl) + dg(al, bh))


@functools.partial(jax.custom_vjp, nondiff_argnums=(2,))
def _dot3(a, b, form):
    return _dot3_impl(a, b, form)


def _dot3_fwd(a, b, form):
    return _dot3_impl(a, b, form), (a, b)


def _dot3_bwd(form, saved, g):
    a, b = saved
    if form == "nn":
        return _dot3_impl(g, b, "nt"), _dot3_impl(a, g, "tn")
    if form == "nt":
        return _dot3_impl(g, b, "nn"), _dot3_impl(g, a, "tn")
    return _dot3_impl(b, g, "nt"), _dot3_impl(a, g, "nn")


_dot3.defvjp(_dot3_fwd, _dot3_bwd)


def _hdot(a, b):
    return _dot3(a, b, "nn")


def _mask_dot(mask, x, form):
    dg = lambda q: lax.dot_general(mask.astype(BF16), q, _DOT_DIMS[form], preferred_element_type=F32)
    x1 = x.astype(BF16)
    r1 = x - x1.astype(F32)
    x2 = r1.astype(BF16)
    x3 = (r1 - x2.astype(F32)).astype(BF16)
    return dg(x1) + (dg(x2) + dg(x3))


def _decay_masks(c):
    row = lax.broadcasted_iota(jnp.int32, (c, c), 0)
    col = lax.broadcasted_iota(jnp.int32, (c, c), 1)
    return (row >= col).astype(F32), row <= col


def _decay_impl(gb):
    lower, upper = _decay_masks(gb.shape[0])
    return _mask_dot(lower, gb, "nn"), _mask_dot(jnp.ones_like(gb), jnp.where(upper, gb, 0.0), "nn")


@jax.custom_vjp
def _decay_matrices(gb):
    return _decay_impl(gb)


def _decay_fwd(gb):
    return _decay_impl(gb), None


def _decay_bwd(_, cts):
    gc, gr = cts
    lower, upper = _decay_masks(gc.shape[0])
    return (_mask_dot(lower, gc, "tn") + jnp.where(upper, _mask_dot(jnp.ones_like(gr), gr, "tn"), 0.0),)


_decay_matrices.defvjp(_decay_fwd, _decay_bwd)


def _softplus(x):
    return jnp.maximum(x, 0.0) + jnp.log(1.0 + jnp.exp(-jnp.abs(x)))


def _heads(f, *lists):
    return [f(*t) for t in zip(*lists)]


DN_STEP_CHUNKS = 3


def _dn_chunk(qr, kr, v, z, braw, araw, alog, dtb, nw, state, valid):
    c = DN_CHUNK
    nh = len(state)
    chunks = len(qr) // nh
    alog, dtb, valid_i = alog * chunks, dtb * chunks, [vv for vv in valid for _ in range(nh)]
    row = lax.broadcasted_iota(jnp.int32, (c, c), 0)
    col = lax.broadcasted_iota(jnp.int32, (c, c), 1)
    incl = row >= col
    strict = row > col
    eye = jnp.where(row == col, 1.0, 0.0)
    q = _heads(lambda t: t * lax.rsqrt(jnp.sum(t * t, -1, keepdims=True) + EPS) * (DN_HEAD_DIM ** -0.5), qr)
    k = _heads(lambda t: t * lax.rsqrt(jnp.sum(t * t, -1, keepdims=True) + EPS), kr)
    beta = _heads(lambda t, vv: _sigmoid(t) * vv, braw, valid_i)
    g = _heads(lambda al, ar, dt, vv: -jnp.exp(al) * _softplus(ar + dt) * vv, alog, araw, dtb, valid_i)
    decay = _heads(lambda t: _decay_matrices(jnp.broadcast_to(t, (c, c))), g)
    dmask = _heads(lambda d: jnp.where(incl, jnp.exp(jnp.where(incl, d[0] - d[1], 0.0)), 0.0), decay)
    dec = _heads(lambda d: d[0][:, :1], decay)
    dlast = _heads(lambda d: d[0][c - 1:c, :1], decay)
    kk = _heads(lambda t: _mxu_dot(t, t, "nt"), k)
    a = _heads(lambda b, t, d: jnp.where(strict, b * t * d, 0.0), beta, kk, dmask)
    x = _heads(lambda t: eye - t, a)
    p = _heads(_hdot, a, a)
    for it in range(5):
        x = _heads(lambda s, t: s + _hdot(s, t), x, p)
        if it < 4:
            p = _heads(_hdot, p, p)
    u = _heads(lambda s, t, b: _hdot(s, t * b), x, v, beta)
    w = _heads(lambda s, t, b, d: _hdot(s, t * (b * jnp.exp(d))), x, k, beta, dec)
    qk = _heads(lambda s, t, d: _mxu_dot(s, t, "nt") * d, q, k, dmask)
    q_dec = _heads(lambda t, d: t * jnp.exp(d), q, dec)
    k_dec = _heads(lambda t, dl, d: t * jnp.exp(dl - d), k, dlast, dec)
    o = []
    for ci in range(chunks):
        of = lambda lst: lst[ci * nh:(ci + 1) * nh]
        v_new = _heads(lambda s, t, st: s - _mxu_dot(t, st, "nn"), of(u), of(w), state)
        o += _heads(lambda qd, st, s, vn: _mxu_dot(qd, st, "nn") + _mxu_dot(s, vn, "nn"), of(q_dec), state, of(qk), v_new)
        state = _heads(lambda st, dl, kd, vn: st * jnp.exp(dl) + _mxu_dot(kd, vn, "tn"), state, of(dlast), of(k_dec), v_new)
    y = _heads(lambda t, zz: _rms(t, nw) * (zz * _sigmoid(zz)), o, z)
    return y, state


DN_STEP_ROWS = DN_STEP_CHUNKS * DN_CHUNK
DN_ITEMS = [(ci, h) for ci in range(DN_STEP_CHUNKS) for h in range(DN_HEADS)]


def _dn_valid(n):
    rows = [n * DN_STEP_ROWS + ci * DN_CHUNK + lax.broadcasted_iota(jnp.int32, (DN_CHUNK, 1), 0) for ci in range(DN_STEP_CHUNKS)]
    return [(r >= PAD_ROWS).astype(F32) for r in rows]


def _dn_in_specs(rev, nc):
    cn = (lambda n: nc - 1 - n) if rev else (lambda n: n)
    c, hd = DN_STEP_ROWS, DN_HEAD_DIM
    return [
        pl.BlockSpec((None, c, DN_DIM), lambda n: (0, cn(n), 0)),
        pl.BlockSpec((None, c, DN_DIM), lambda n: (1, cn(n), 0)),
        pl.BlockSpec((None, c, DN_DIM), lambda n: (2, cn(n), 0)),
        pl.BlockSpec((None, c, DN_DIM), lambda n: (6, cn(n), 0)),
        pl.BlockSpec((DN_HEADS, 2, c, 1), lambda n: (0, 0, cn(n), 0)),
        pl.BlockSpec((DN_HEADS, SUBLANES, LANES), lambda n: (0, 0, 0)),
        pl.BlockSpec((1, hd), lambda n: (0, 0)),
    ]


def _rows(ci):
    return slice(ci * DN_CHUNK, (ci + 1) * DN_CHUNK)


def _cols(h):
    return slice(h * DN_HEAD_DIM, (h + 1) * DN_HEAD_DIM)


def _dn_load(q_ref, k_ref, v_ref, z_ref, ba_ref, hp_ref):
    heads = range(DN_HEADS)
    item = lambda ref: [ref[_rows(ci), _cols(h)] for ci, h in DN_ITEMS]
    return (item(q_ref), item(k_ref), item(v_ref), item(z_ref),
            [ba_ref[h, 0, _rows(ci), :] for ci, h in DN_ITEMS], [ba_ref[h, 1, _rows(ci), :] for ci, h in DN_ITEMS],
            [hp_ref[h, 0:1, 0:1] for h in heads], [hp_ref[h, 1:2, 0:1] for h in heads])


def _delta_fwd(qkvc, pm, ba, hp, nw, *, name):
    _, m, _ = qkvc.shape
    nc = m // DN_STEP_ROWS
    hd = DN_HEAD_DIM

    def body(q_ref, k_ref, v_ref, z_ref, ba_ref, hp_ref, nw_ref, y_ref, s_ref, state):
        n = pl.program_id(0)

        @pl.when(n == 0)
        def _():
            state[...] = jnp.zeros_like(state)

        heads = range(DN_HEADS)
        old = [state[h] for h in heads]
        y, new = _dn_chunk(*_dn_load(q_ref, k_ref, v_ref, z_ref, ba_ref, hp_ref), nw_ref[...], old, _dn_valid(n))
        for h in heads:
            s_ref[h] = old[h]
            state[h] = new[h]
        for (ci, h), yy in zip(DN_ITEMS, y):
            y_ref[_rows(ci), _cols(h)] = yy.astype(y_ref.dtype)

    return pl.pallas_call(
        body, name=name, grid=(nc,), in_specs=_dn_in_specs(False, nc),
        out_specs=[pl.BlockSpec((DN_STEP_ROWS, DN_DIM), lambda n: (n, 1)), pl.BlockSpec((DN_HEADS, None, hd, hd), lambda n: (0, n, 0, 0))],
        out_shape=[jax.ShapeDtypeStruct((m, D_CONV + DN_DIM), MXU_DTYPE), jax.ShapeDtypeStruct((DN_HEADS, nc, hd, hd), F32)],
        scratch_shapes=[pltpu.VMEM((DN_HEADS, hd, hd), F32)],
        compiler_params=_params(("arbitrary",)),
    )(qkvc, qkvc, qkvc, pm, ba, hp, nw)


def _delta_bwd(dy, qkvc, pm, ba, hp, nw, states, *, name):
    _, m, _ = qkvc.shape
    nc = m // DN_STEP_ROWS
    hd, c = DN_HEAD_DIM, DN_STEP_ROWS

    def body(q_ref, k_ref, v_ref, z_ref, ba_ref, hp_ref, nw_ref, s_ref, dy_ref,
             dz_ref, dqkv_ref, dba_ref, dhp_ref, dnw_ref, dstate):
        step = pl.program_id(0)
        n = nc - 1 - step

        @pl.when(step == 0)
        def _():
            dstate[...] = jnp.zeros_like(dstate)
            dhp_ref[...] = jnp.zeros_like(dhp_ref)
            dnw_ref[...] = jnp.zeros_like(dnw_ref)

        valid = _dn_valid(n)
        heads = range(DN_HEADS)
        fn = lambda *a: _dn_chunk(*a, valid)
        _, vjp = jax.vjp(fn, *_dn_load(q_ref, k_ref, v_ref, z_ref, ba_ref, hp_ref), nw_ref[...], [s_ref[h] for h in heads])
        dy = [dy_ref[_rows(ci), _cols(h)] for ci, h in DN_ITEMS]
        dq, dk, dv, dz, dbr, dar, dalog, ddtb, dnw, dst = vjp((dy, [dstate[h] for h in heads]))
        for i, (ci, h) in enumerate(DN_ITEMS):
            dqkv_ref[0, _rows(ci), _cols(h)] = dq[i]
            dqkv_ref[1, _rows(ci), _cols(h)] = dk[i]
            dqkv_ref[2, _rows(ci), _cols(h)] = dv[i]
            dz_ref[_rows(ci), _cols(h)] = dz[i].astype(dz_ref.dtype)
            dba_ref[h, 0, _rows(ci), :] = dbr[i]
            dba_ref[h, 1, _rows(ci), :] = dar[i]
        for h in heads:
            dstate[h] = dst[h]
            dhp_ref[h] += jnp.concatenate([jnp.broadcast_to(dalog[h], (1, LANES)), jnp.broadcast_to(ddtb[h], (1, LANES)),
                                           jnp.zeros((SUBLANES - 2, LANES), F32)], 0)
        dnw_ref[...] += dnw

    rn = lambda n: nc - 1 - n
    in_specs = _dn_in_specs(True, nc) + [
        pl.BlockSpec((DN_HEADS, None, hd, hd), lambda n: (0, rn(n), 0, 0)),
        pl.BlockSpec((c, DN_DIM), lambda n: (rn(n), 1)),
    ]
    out_specs = [
        pl.BlockSpec((None, c, DN_DIM), lambda n: (6, rn(n), 0)),
        pl.BlockSpec((3, c, DN_DIM), lambda n: (0, rn(n), 0)),
        pl.BlockSpec((DN_HEADS, 2, c, 1), lambda n: (0, 0, rn(n), 0)),
        pl.BlockSpec((DN_HEADS, SUBLANES, LANES), lambda n: (0, 0, 0)),
        pl.BlockSpec((1, hd), lambda n: (0, 0)),
    ]
    return pl.pallas_call(
        body, name=name, grid=(nc,), in_specs=in_specs, out_specs=out_specs,
        out_shape=[jax.ShapeDtypeStruct(pm.shape, MXU_DTYPE), jax.ShapeDtypeStruct(qkvc.shape, F32),
                   jax.ShapeDtypeStruct(ba.shape, F32), jax.ShapeDtypeStruct(hp.shape, F32),
                   jax.ShapeDtypeStruct((1, hd), F32)],
        scratch_shapes=[pltpu.VMEM((DN_HEADS, hd, hd), F32)],
        compiler_params=_params(("arbitrary",)),
    )(qkvc, qkvc, qkvc, pm, ba, hp, nw, states, dy)


SWA_PAIR = 2


def _attn_block(q, k0, kp, kc, v0, vp, vc, qw, kw, sink, n):
    g, b, hd = SWA_GROUP, SWA_BLOCK, SWA_HEAD_DIM
    pair = list(range(SWA_PAIR))
    lanes = lambda t, e: t[:, e * hd:(e + 1) * hd]
    q4 = [jnp.concatenate([lanes(q, e * g + i)[None] for i in range(g)], 0) for e in pair]
    qn = _heads(lambda t: _rms(t, qw) * (hd ** -0.5), q4)
    kn = [_rms(jnp.concatenate([lanes(k0, e), lanes(kp, e), lanes(kc, e)], 0), kw) for e in pair]
    vcat = [jnp.concatenate([lanes(v0, e), lanes(vp, e), lanes(vc, e)], 0) for e in pair]
    s = _heads(lambda a, k: _mxu_dot(a.reshape(g * b, hd), k, "nt").reshape(g, b, 3 * b), qn, kn)
    i = lax.broadcasted_iota(jnp.int32, (b, 3 * b), 0)
    c = lax.broadcasted_iota(jnp.int32, (b, 3 * b), 1)
    in_meta, in_prev, in_cur = c < b, (c >= b) & (c < 2 * b), c >= 2 * b
    j = c - jnp.where(in_meta, 0, jnp.where(in_prev, b, 2 * b))
    meta_lo = jnp.where(n == 0, b, PAD_ROWS)
    cur_lo = jnp.where(n == 0, PAD_ROWS, 0)
    prev_off = jnp.where(n >= 2, 0, 2 * b)
    valid = (in_meta & (j >= meta_lo)) | (in_prev & (j > i + prev_off)) | (in_cur & (j <= i) & (j >= cur_lo))
    s = _heads(lambda t: jnp.where(valid[None], t, NEG), s)
    m = [lax.stop_gradient(jnp.maximum(jnp.max(t, -1, keepdims=True), sink[e])) for e, t in zip(pair, s)]
    ex = _heads(lambda t, mm: jnp.exp(t - mm), s, m)
    p = [t / (jnp.sum(t, -1, keepdims=True) + jnp.exp(sink[e] - mm)) for e, t, mm in zip(pair, ex, m)]
    o = _heads(lambda t, v: _mxu_dot(t.reshape(g * b, 3 * b), v, "nn").reshape(g, b, hd), p, vcat)
    return jnp.concatenate([o[e][i] for e in pair for i in range(g)], 1)


Q_LANES = SWA_PAIR * SWA_GROUP * SWA_HEAD_DIM
KV_LANES = SWA_PAIR * SWA_HEAD_DIM
K_BLOCK0 = SWA_HEADS * SWA_HEAD_DIM // KV_LANES
V_BLOCK0 = K_BLOCK0 + SWA_KV_HEADS * SWA_HEAD_DIM // KV_LANES


def _attn_in_specs():
    g, b, hd = SWA_GROUP, SWA_BLOCK, SWA_HEAD_DIM
    kv = lambda f, first: pl.BlockSpec((b, KV_LANES), lambda p, n: (f(n), first + p))
    blocks = [lambda n: 0, lambda n: jnp.maximum(n - 1, 0), lambda n: n]
    return ([pl.BlockSpec((b, Q_LANES), lambda p, n: (n, p))] + [kv(f, K_BLOCK0) for f in blocks] + [kv(f, V_BLOCK0) for f in blocks]
            + [pl.BlockSpec((1, hd), lambda p, n: (0, 0)), pl.BlockSpec((1, hd), lambda p, n: (0, 0)),
               pl.BlockSpec((SWA_PAIR, g, 1, 1), lambda p, n: (p, 0, 0, 0))])


def _attn_fwd(qkv, qw, kw, sink, *, name):
    m = qkv.shape[0]
    b = SWA_BLOCK

    def body(q_ref, k0, kp, kc, v0, vp, vc, qw_ref, kw_ref, s_ref, o_ref):
        o_ref[...] = _attn_block(q_ref[...], k0[...], kp[...], kc[...], v0[...], vp[...], vc[...], qw_ref[...], kw_ref[...],
                                 s_ref[...], pl.program_id(1)).astype(o_ref.dtype)

    return pl.pallas_call(
        body, name=name, grid=(SWA_KV_HEADS // SWA_PAIR, m // b), in_specs=_attn_in_specs(),
        out_specs=pl.BlockSpec((b, Q_LANES), lambda p, n: (n, p)),
        out_shape=jax.ShapeDtypeStruct((m, SWA_HEADS * SWA_HEAD_DIM), MXU_DTYPE),
        compiler_params=_params(("parallel", "parallel")),
    )(*([qkv] * 7), qw, kw, sink)


def _attn_bwd(do, qkv, qw, kw, sink, *, name):
    m = qkv.shape[0]
    g, b, hd = SWA_GROUP, SWA_BLOCK, SWA_HEAD_DIM

    def body(q_ref, k0, kp, kc, v0, vp, vc, qw_ref, kw_ref, s_ref, do_ref, dq_ref, dk_ref, dv_ref, dqw_ref, dkw_ref, ds_ref):
        n = pl.program_id(1)

        @pl.when(n == 0)
        def _():
            for r in (dk_ref, dv_ref, dqw_ref, dkw_ref, ds_ref):
                r[...] = jnp.zeros_like(r)

        fn = lambda *a: _attn_block(*a, n)
        _, vjp = jax.vjp(fn, q_ref[...], k0[...], kp[...], kc[...], v0[...], vp[...], vc[...], qw_ref[...], kw_ref[...], s_ref[...])
        dq, dk0, dkp, dkc, dv0, dvp, dvc, dqw, dkw, dsk = vjp(do_ref[...])
        dq_ref[...] = dq
        prev = pl.multiple_of(jnp.maximum(n - 1, 0) * b, b)
        cur = pl.multiple_of(n * b, b)
        for ref, parts in ((dk_ref, (dk0, dkp, dkc)), (dv_ref, (dv0, dvp, dvc))):
            ref[pl.ds(0, b), :] += parts[0]
            ref[pl.ds(prev, b), :] += parts[1]
            ref[pl.ds(cur, b), :] += parts[2]
        dqw_ref[...] += dqw
        dkw_ref[...] += dkw
        ds_ref[...] += dsk

    pairs = SWA_KV_HEADS // SWA_PAIR
    kv_acc = pl.BlockSpec((m, KV_LANES), lambda p, n: (0, p))
    w_acc = pl.BlockSpec((None, 1, hd), lambda p, n: (p, 0, 0))
    kv_shape = jax.ShapeDtypeStruct((m, SWA_KV_HEADS * hd), F32)
    return pl.pallas_call(
        body, name=name, grid=(pairs, m // b),
        in_specs=_attn_in_specs() + [pl.BlockSpec((b, Q_LANES), lambda p, n: (n, p))],
        out_specs=[pl.BlockSpec((b, Q_LANES), lambda p, n: (n, p)), kv_acc, kv_acc, w_acc, w_acc,
                   pl.BlockSpec((SWA_PAIR, g, 1, 1), lambda p, n: (p, 0, 0, 0))],
        out_shape=[jax.ShapeDtypeStruct((m, SWA_HEADS * hd), F32), kv_shape, kv_shape,
                   jax.ShapeDtypeStruct((pairs, 1, hd), F32), jax.ShapeDtypeStruct((pairs, 1, hd), F32),
                   jax.ShapeDtypeStruct(sink.shape, F32)],
        compiler_params=_params(("parallel", "arbitrary")),
    )(*([qkv] * 7), qw, kw, sink, do)


def _loss_bwd(h, target, *, name):
    m, d = h.shape
    b = SWA_BLOCK

    def body(h_ref, t_ref, l_ref, dh_ref):
        i = pl.program_id(0)

        @pl.when(i == 0)
        def _():
            l_ref[...] = jnp.zeros_like(l_ref)
            dh_ref[...] = jnp.zeros_like(dh_ref)

        @pl.when(i > 0)
        def _():
            e = h_ref[...] - t_ref[...]
            dh_ref[...] = e * (1.0 / d)
            l_ref[...] += jnp.sum(jnp.sum(e * e, 0, keepdims=True), 1, keepdims=True) * (0.5 / d)

    return pl.pallas_call(
        body, name=name, grid=(m // b,),
        in_specs=[pl.BlockSpec((b, d), lambda i: (i, 0)), pl.BlockSpec((b, d), lambda i: (jnp.maximum(i - 1, 0), 0))],
        out_specs=[pl.BlockSpec((1, LANES), lambda i: (0, 0)), pl.BlockSpec((b, d), lambda i: (i, 0))],
        out_shape=[jax.ShapeDtypeStruct((1, LANES), F32), jax.ShapeDtypeStruct((m, d), F32)],
        compiler_params=_params(("arbitrary",)),
    )(h, target)


def _ffn_fwd(h, nw, w_up_t, cw, w_down, tag):
    u, hn = _norm_matmul(h, nw, w_up_t, o_seg=D_FF, trans_w=True, name=f"ffn_up_{tag}")
    a = _ffn_act_fwd(u, cw, name=f"ffn_act_{tag}")
    return _mm_nn(a, w_down, res=h, name=f"ffn_down_{tag}"), (h, hn, u, a)


def _ffn_bwd(dh, saved, nw, w_up_t, cw, w_down, tag):
    h, hn, u, a = saved
    da = _mm_nn(dh, w_down, trans_w=True, name=f"ffn_da_{tag}")
    dw_down = _mm_tn(a, dh, name=f"ffn_dwdown_{tag}")
    du, dcw = _ffn_act_bwd(da, u, cw, name=f"ffn_act_bwd_{tag}")
    dhn = _mm_nn(du, w_up_t, a_seg=True, name=f"ffn_dhn_{tag}")
    dw_up_t = _mm_tn(du, hn, a_seg=True, name=f"ffn_dwup_{tag}")
    dh_in, dnw = _rmsnorm_bwd(dhn, h, nw, dh, name=f"ffn_norm_bwd_{tag}")
    return dh_in, dnw, dw_up_t, dcw, dw_down


def _local_step(x, target, w, fetch=None, push=None):
    fetch = fetch or (lambda stage, after: {})
    push = push or (lambda stage, grads: None)
    plus = lambda a, zero: a if zero is None else a + zero
    seq, d = x.shape
    m = PAD_ROWS + N_META + seq
    h0 = jnp.concatenate([jnp.zeros((PAD_ROWS, d), F32), w["meta"], x], 0)

    pm, hn0 = _norm_matmul(h0, w["anw"][0], w["w_in_t"], o_seg=SEG, trans_w=True, n=N_SEG * SEG, name="mix_in")
    pba = _mm_nn(hn0, w["w_in_tail_t"], trans_w=True, name="mix_in_tail")
    qkvc = _dnpre_fwd(pm, w["dcw"], name="dn_conv")
    ba = pba[:, :2 * DN_HEADS].T.reshape(2, DN_HEADS, m, 1).transpose(1, 0, 2, 3)
    y, states = _delta_fwd(qkvc, pm, ba, w["hp"], w["dnw"], name="delta")
    y = _shortconv_fwd(pm, w["caw"], y, name="shortconv")
    w = {**w, **fetch("l0", y)}
    h1 = _mm_nn(y, w["w_out"], res=h0, name="mix_out")
    h2, ffn0 = _ffn_fwd(h1, w["fnw"][0], w["w_up0_t"], w["fcw"][0], w["w_down0"], "l0")

    w = {**w, **fetch("l1", h2)}
    qkv, hn2 = _norm_matmul(h2, w["anw"][1], w["wqkv"], name="attn_qkv")
    o = _attn_fwd(qkv, w["qnw"], w["knw"], w["sink"], name="attn")
    h3 = _mm_nn(o, w["wo"], res=h2, name="attn_out")
    h4, ffn1 = _ffn_fwd(h3, w["fnw"][1], w["w_up1_t"], w["fcw"][1], w["w_down1"], "l1")

    loss, dh4 = _loss_bwd(h4, target, name="loss")

    g = {}
    dh3, dfnw1, dwup1, dfcw1, dwdown1 = _ffn_bwd(dh4, ffn1, w["fnw"][1], w["w_up1_t"], w["fcw"][1], w["w_down1"], "l1")

    do = _mm_nn(dh3, w["wo"], trans_w=True, name="attn_do")
    g["wo"] = _mm_tn(o, dh3, name="attn_dwo")
    dq, dk, dv, dqw, dkw, dsink = _attn_bwd(do, qkv, w["qnw"], w["knw"], w["sink"], name="attn_bwd")
    dqkv = jnp.concatenate([dq, dk, dv], 1).astype(MXU_DTYPE)
    dhn2 = _mm_nn(dqkv, w["wqkv"], trans_w=True, name="attn_dhn")
    g["wqkv"] = _mm_tn(hn2, dqkv, name="attn_dwqkv")
    zero = push("l1", dict(w_up_t=dwup1, w_down=dwdown1, wo=g["wo"], wqkv=g["wqkv"]))
    dh2, danw1 = _rmsnorm_bwd(dhn2, h2, plus(w["anw"][1], zero), dh3, name="attn_norm_bwd")

    dh1, dfnw0, dwup0, dfcw0, dwdown0 = _ffn_bwd(dh2, ffn0, w["fnw"][0], w["w_up0_t"], w["fcw"][0], w["w_down0"], "l0")

    dy = _mm_nn(dh1, w["w_out"], trans_w=True, name="mix_dy")
    g["w_out"] = _mm_tn(y, dh1, name="mix_dwout")
    zero = push("l0", dict(w_up_t=dwup0, w_down=dwdown0, w_out=g["w_out"]))
    dpm, dqkvc, dba, dhp, ddnw = _delta_bwd(dy, qkvc, pm, ba, w["hp"], plus(w["dnw"], zero), states, name="delta_bwd")
    dpm, ddcw = _dnpre_bwd(dqkvc, pm, w["dcw"], dpm, name="dn_conv_bwd")
    dpm, dcaw = _shortconv_bwd(dy, pm, w["caw"], dpm, name="shortconv_bwd")
    dpba = jnp.pad(dba.transpose(1, 0, 2, 3).reshape(2 * DN_HEADS, m).T, ((0, 0), (0, LANES - 2 * DN_HEADS))).astype(MXU_DTYPE)
    g["w_in_t"] = jnp.concatenate([_mm_tn(dpm, hn0, a_seg=True, out_dtype=F32, name="mix_dwin"),
                                   _mm_tn(dpba, hn0, out_dtype=F32, name="mix_dwin_tail")[:N_TAIL]], 0).astype(GRAD_WIRE_DTYPE)
    zero = push("in", dict(w_in_t=g["w_in_t"]))
    dhn0 = _mm_nn(dpba, plus(w["w_in_tail_t"], None if zero is None else zero.astype(MXU_DTYPE)), name="mix_dhn_tail")
    dhn0 = _mm_nn(dpm, w["w_in_t"], res=dhn0, a_seg=True, name="mix_dhn")
    dh0, danw0 = _rmsnorm_bwd(dhn0, h0, w["anw"][0], dh1, name="mix_norm_bwd")

    g.update(
        x=dh0[PAD_ROWS + N_META:], meta=dh0[PAD_ROWS:PAD_ROWS + N_META], anw=[danw0, danw1], fnw=[dfnw0, dfnw1],
        caw=dcaw, dcw=ddcw, hp=dhp, dnw=ddnw, qnw=jnp.sum(dqw, 0), knw=jnp.sum(dkw, 0), sink=dsink,
        w_up_t=[dwup0, dwup1], fcw=[dfcw0, dfcw1], w_down=[dwdown0, dwdown1])
    return loss, g


N_TAIL = 2 * DN_HEADS


def _prepare_early(p):
    n_main = N_SEG * SEG
    w_in_t = p["mix_w_in_t"]
    tail_t = jnp.pad(w_in_t[n_main:], ((0, LANES - N_TAIL), (0, 0)))
    hp = jnp.zeros((DN_HEADS, SUBLANES, LANES), F32)
    hp = hp.at[:, 0, :].set(p["dn_a_log"][0][:, None]).at[:, 1, :].set(p["dn_dt_bias"][0][:, None])
    depth = p["ffn_conv_w"].shape[0]
    return dict(
        meta=p["meta_tokens"], anw=[p["attn_norm_w"][i:i + 1] for i in range(depth)],
        fnw=[p["ffn_norm_w"][i:i + 1] for i in range(depth)],
        w_in_t=w_in_t, w_in_tail_t=tail_t,
        caw=_pad_w(p["conv_a_w"][0]), dcw=_pad_w(p["dn_conv_w"][0]), hp=hp, dnw=p["dn_norm_w"],
        qnw=p["swa_q_norm_w"], knw=p["swa_k_norm_w"], sink=p["swa_sinks"].reshape(SWA_KV_HEADS, SWA_GROUP, 1, 1),
        fcw=[_pad_w(p["ffn_conv_w"][i]) for i in range(depth)])


def _prepare_weights(p):
    return dict(
        _prepare_early(dict(p, mix_w_in_t=p["mix_w_in"][0].T)), w_out=p["mix_w_out"][0], wo=p["swa_wo"][0],
        wqkv=jnp.concatenate([p["swa_wq"][0], p["swa_wk"][0], p["swa_wv"][0]], 1),
        w_up0_t=p["ffn_w_up"][0].T, w_up1_t=p["ffn_w_up"][1].T, w_down0=p["ffn_w_down"][0], w_down1=p["ffn_w_down"][1])


def _small_named(g):
    return dict(
        meta_tokens=g["meta"], attn_norm_w=jnp.concatenate(g["anw"], 0), ffn_norm_w=jnp.concatenate(g["fnw"], 0),
        conv_a_w=g["caw"][None, :3], dn_conv_w=g["dcw"][None, :4],
        dn_a_log=g["hp"][None, :, 0, 0], dn_dt_bias=g["hp"][None, :, 1, 0], dn_norm_w=g["dnw"],
        swa_q_norm_w=g["qnw"], swa_k_norm_w=g["knw"], swa_sinks=g["sink"].reshape(1, SWA_HEADS),
        ffn_conv_w=jnp.stack([c[:3] for c in g["fcw"]]))


def _reference_named(g):
    nq, nkv = SWA_HEADS * SWA_HEAD_DIM, SWA_KV_HEADS * SWA_HEAD_DIM
    return dict(
        _small_named(g), mix_w_in=g["w_in_t"].T[None],
        mix_w_out=g["w_out"][None], swa_wq=g["wqkv"][None, :, :nq], swa_wk=g["wqkv"][None, :, nq:nq + nkv],
        swa_wv=g["wqkv"][None, :, nq + nkv:], swa_wo=g["wo"][None],
        ffn_w_up=jnp.stack([t.T for t in g["w_up_t"]]), ffn_w_down=jnp.stack(g["w_down"]))


def _my_index():
    return 4 * lax.axis_index("x") + 2 * lax.axis_index("y") + lax.axis_index("c")


def _all_gather(arrays, *, name):
    n = len(arrays)

    def body(*refs):
        ins, outs = refs[:n], refs[n:2 * n]
        send_sems, recv_sems, local_sems = refs[2 * n:]
        x, y, c = lax.axis_index("x"), lax.axis_index("y"), lax.axis_index("c")
        me, sibling = (x, y, c), (x, y, 1 - c)
        chips = [(1 - x, y), (x, 1 - y), (1 - x, 1 - y)]

        def copy(i, k, block, to, src=None):
            rows = outs[i].at[4 * block[0] + 2 * block[1] + block[2]]
            return pltpu.make_async_remote_copy(
                src_ref=rows if src is None else src, dst_ref=rows, send_sem=send_sems.at[i, k], recv_sem=recv_sems.at[i, k],
                device_id=to, device_id_type=pl.DeviceIdType.MESH)

        mine = [pltpu.make_async_copy(ins[i], outs[i].at[4 * x + 2 * y + c], local_sems.at[i]) for i in range(n)]
        first = []
        for j, chip in enumerate(chips):
            first += [copy(i, 1 + j, me, (*chip, c), src=ins[i]) for i in range(n)]
        first += [copy(i, 0, me, sibling, src=ins[i]) for i in range(n)]
        for cp in first + mine:
            cp.start()
        passed = []
        for j, chip in enumerate(chips):
            for i in range(n):
                copy(i, 1 + j, (*chip, c), me).wait_recv()
                fwd = copy(i, 4 + j, (*chip, c), sibling)
                fwd.start()
                passed.append(fwd)
        for i in range(n):
            copy(i, 0, sibling, me).wait_recv()
            for j, chip in enumerate(chips):
                copy(i, 4 + j, (*chip, 1 - c), me).wait_recv()
        for cp in first + passed:
            cp.wait_send()
        for cp in mine:
            cp.wait()

    hbm = pl.BlockSpec(memory_space=pl.ANY)
    return pl.pallas_call(
        body, name=name, in_specs=[hbm] * n, out_specs=[hbm] * n,
        out_shape=[jax.ShapeDtypeStruct((N_DEV,) + tuple(a.shape), a.dtype) for a in arrays],
        scratch_shapes=[pltpu.SemaphoreType.DMA((n, 7)), pltpu.SemaphoreType.DMA((n, 7)), pltpu.SemaphoreType.DMA((n,))],
    )(*arrays)


def _peer(d):
    px, py, pc = lax.axis_index("x") ^ (d >> 2), lax.axis_index("y") ^ ((d >> 1) & 1), lax.axis_index("c") ^ (d & 1)
    return (px, py, pc), 4 * px + 2 * py + pc


def _push_copies(mode, srcs, lands, send_sems, recv_sems):
    me = _my_index()
    out = []
    for d in range(1, N_DEV):
        pos, idx = _peer(d)
        for i in range(len(srcs)):
            out.append(pltpu.make_async_remote_copy(
                src_ref=srcs[i] if mode == "gather" else srcs[i].at[idx], dst_ref=lands[i].at[me],
                send_sem=send_sems.at[i * N_DEV + d], recv_sem=recv_sems.at[i * N_DEV + d], device_id=pos,
                device_id_type=pl.DeviceIdType.MESH))
    return out


_HBM = pl.BlockSpec(memory_space=pltpu.HBM)
_SEM = pl.BlockSpec(memory_space=pltpu.SEMAPHORE)


def _push_start(mode, arrays, follows, *, name):
    n = len(arrays)
    blocks = [a.shape if mode == "gather" else a.shape[1:] for a in arrays]
    lands = [lax.empty((N_DEV,) + tuple(b), a.dtype) for a, b in zip(arrays, blocks)]

    def body(*refs):
        srcs, land_refs = refs[:n], refs[n:2 * n]
        send_sems, recv_sems = refs[2 * n + 1], refs[2 * n + 2]
        zero = refs[-1]
        for cp in _push_copies(mode, srcs, land_refs, send_sems, recv_sems):
            cp.start()
        zero[...] = jnp.zeros_like(zero)

    hbm_in = [pltpu.with_memory_space_constraint(a, pltpu.HBM) for a in list(arrays) + lands]
    outs = pl.pallas_call(
        body, name=name,
        out_shape=[pltpu.SemaphoreType.DMA((n * N_DEV,)), pltpu.SemaphoreType.DMA((n * N_DEV,))]
        + [pltpu.HBM(a.shape, a.dtype) for a in hbm_in] + [jax.ShapeDtypeStruct((SUBLANES, LANES), F32)],
        in_specs=[_HBM] * (2 * n) + [pl.BlockSpec(memory_space=pl.ANY)],
        out_specs=[_SEM, _SEM] + [_HBM] * (2 * n) + [pl.BlockSpec(memory_space=pltpu.VMEM)],
        input_output_aliases={i: 2 + i for i in range(2 * n)},
        compiler_params=pltpu.CompilerParams(has_side_effects=pltpu.SideEffectType.DATAFLOW_SIDE_EFFECTING),
    )(*hbm_in, follows)
    return dict(mode=mode, sems=outs[:2], srcs=outs[2:2 + n], lands=outs[2 + n:2 + 2 * n], zero=outs[-1])


def _push_wait(push, follows, *, name):
    n = len(push["srcs"])
    mode = push["mode"]

    def body(*refs):
        srcs, land_refs = refs[:n], refs[n:2 * n]
        send_sems, recv_sems = refs[2 * n], refs[2 * n + 1]
        for cp in _push_copies(mode, srcs, land_refs, send_sems, recv_sems):
            cp.wait_send()
            cp.wait_recv()

    args = list(push["srcs"]) + list(push["lands"])
    outs = pl.pallas_call(
        body, name=name, out_shape=[pltpu.HBM(a.shape, a.dtype) for a in args],
        in_specs=[_HBM] * (2 * n) + [_SEM, _SEM, pl.BlockSpec(memory_space=pl.ANY)], out_specs=[_HBM] * (2 * n),
        input_output_aliases={i: i for i in range(2 * n)},
        compiler_params=pltpu.CompilerParams(has_side_effects=pltpu.SideEffectType.DATAFLOW_SIDE_EFFECTING),
    )(*args, *push["sems"], follows)
    me = _my_index()
    got = []
    for src, land in zip(outs[:n], outs[n:]):
        own = src if mode == "gather" else lax.dynamic_index_in_dim(src, me, 0, keepdims=False)
        got.append(lax.dynamic_update_index_in_dim(land, own, me, 0))
    return got


ADAMW_BLOCK_BYTES = 6 * 1024 * 1024


def _adamw_tile(r, c):
    fits = lambda tr, tc: N_DEV * tr * tc * 4 <= ADAMW_BLOCK_BYTES
    rows = [t for t in range(2 * SUBLANES, r + 1, 2 * SUBLANES) if r % t == 0 and fits(t, c)]
    if rows or fits(r, c):
        return (max(rows) if rows else r), c
    cols = [t for t in range(LANES, c + 1, LANES) if c % t == 0 and fits(r, t)]
    return r, max(cols)


def _adamw(parts, w, m, v, layer, outs=None, *, name):
    nl, r, c = w.shape
    tr, tc = _adamw_tile(r, c)

    def body(p_ref, w_ref, m_ref, v_ref, *rest):
        g_ref, d_ref, nm_ref, nv_ref = rest[-4:]
        g = p_ref[0].astype(F32)
        for j in range(1, N_DEV):
            g = g + p_ref[j].astype(F32)
        m2 = ADAM_B1 * m_ref[...] + (1.0 - ADAM_B1) * g
        v2 = ADAM_B2 * v_ref[...] + (1.0 - ADAM_B2) * jnp.square(g)
        m_hat = m2 / (1.0 - ADAM_B1 ** ADAM_STEP)
        v_hat = v2 / (1.0 - ADAM_B2 ** ADAM_STEP)
        g_ref[...] = g
        d_ref[...] = -ADAM_LR * (m_hat / (jnp.sqrt(v_hat) + ADAM_EPS) + ADAM_WD * w_ref[...])
        nm_ref[...] = m2
        nv_ref[...] = v2

    blk = pl.BlockSpec((None, tr, tc), lambda i, j: (layer, i, j))
    out = jax.ShapeDtypeStruct((nl, r, c), F32)
    given = list(outs) if outs is not None else []
    return pl.pallas_call(
        body, name=name, grid=(r // tr, c // tc),
        in_specs=[pl.BlockSpec((N_DEV, tr, tc), lambda i, j: (0, i, j)), blk, blk, blk] + [pl.BlockSpec(memory_space=pl.ANY)] * len(given),
        out_specs=[blk, blk, blk, blk], out_shape=[out, out, out, out],
        input_output_aliases={4 + t: t for t in range(len(given))},
        compiler_params=_params(("parallel", "parallel")),
    )(parts, w, m, v, *given)


SHARD_AXIS = dict(
    meta_tokens=1, attn_norm_w=None, ffn_norm_w=None, mix_w_in=2, conv_a_w=2, dn_conv_w=2, dn_a_log=None, dn_dt_bias=None,
    dn_norm_w=None, mix_w_out=1, swa_wq=1, swa_wk=1, swa_wv=1, swa_q_norm_w=None, swa_k_norm_w=None, swa_sinks=None,
    swa_wo=1, ffn_w_up=2, ffn_conv_w=2, ffn_w_down=1)
WEIGHTS = list(SHARD_AXIS)
BIG = ["mix_w_in", "mix_w_out", "swa_wq", "swa_wk", "swa_wv", "swa_wo", "ffn_w_up", "ffn_w_down"]
SMALL = [k for k in WEIGHTS if k not in BIG]
SMALL_SHARDED = [k for k in SMALL if SHARD_AXIS[k] is not None]


def _whole(g8, axis):
    t = jnp.moveaxis(g8, 0, axis)
    return t.reshape(t.shape[:axis] + (t.shape[axis] * t.shape[axis + 1],) + t.shape[axis + 2:])


def _by_owner(a, axis):
    s = a.shape[axis] // N_DEV
    return jnp.moveaxis(a.reshape(a.shape[:axis] + (N_DEV, s) + a.shape[axis + 1:]), axis, 0)


def _pack(arrays, lead=0):
    flat = jnp.concatenate([a.reshape(a.shape[:lead] + (-1,)) for a in arrays], -1)
    n = flat.shape[-1]
    rows = -(-n // (SUBLANES * LANES)) * SUBLANES
    flat = jnp.pad(flat, [(0, 0)] * lead + [(0, rows * LANES - n)])
    return flat.reshape(flat.shape[:lead] + (rows, LANES))


def _unpack(buf, shapes, lead=0):
    flat = buf.reshape(buf.shape[:lead] + (-1,))
    out, o = [], 0
    for s in shapes:
        n = 1
        for e in s:
            n *= e
        out.append(flat[..., o:o + n].reshape(buf.shape[:lead] + tuple(s)))
        o += n
    return out


def kernel(x, meta_tokens, attn_norm_w, ffn_norm_w, mix_w_in, conv_a_w, dn_conv_w, dn_a_log, dn_dt_bias, dn_norm_w, mix_w_out, swa_wq, swa_wk, swa_wv, swa_q_norm_w, swa_k_norm_w, swa_sinks, swa_wo, ffn_w_up, ffn_conv_w, ffn_w_down, loss_target, m_meta_tokens, m_attn_norm_w, m_ffn_norm_w, m_mix_w_in, m_conv_a_w, m_dn_conv_w, m_dn_a_log, m_dn_dt_bias, m_dn_norm_w, m_mix_w_out, m_swa_wq, m_swa_wk, m_swa_wv, m_swa_q_norm_w, m_swa_k_norm_w, m_swa_sinks, m_swa_wo, m_ffn_w_up, m_ffn_conv_w, m_ffn_w_down, v_meta_tokens, v_attn_norm_w, v_ffn_norm_w, v_mix_w_in, v_conv_a_w, v_dn_conv_w, v_dn_a_log, v_dn_dt_bias, v_dn_norm_w, v_mix_w_out, v_swa_wq, v_swa_wk, v_swa_wv, v_swa_q_norm_w, v_swa_k_norm_w, v_swa_sinks, v_swa_wo, v_ffn_w_up, v_ffn_conv_w, v_ffn_w_down):
    w = dict(meta_tokens=meta_tokens, attn_norm_w=attn_norm_w, ffn_norm_w=ffn_norm_w, mix_w_in=mix_w_in, conv_a_w=conv_a_w, dn_conv_w=dn_conv_w, dn_a_log=dn_a_log, dn_dt_bias=dn_dt_bias, dn_norm_w=dn_norm_w, mix_w_out=mix_w_out, swa_wq=swa_wq, swa_wk=swa_wk, swa_wv=swa_wv, swa_q_norm_w=swa_q_norm_w, swa_k_norm_w=swa_k_norm_w, swa_sinks=swa_sinks, swa_wo=swa_wo, ffn_w_up=ffn_w_up, ffn_conv_w=ffn_conv_w, ffn_w_down=ffn_w_down)
    mom = dict(meta_tokens=m_meta_tokens, attn_norm_w=m_attn_norm_w, ffn_norm_w=m_ffn_norm_w, mix_w_in=m_mix_w_in, conv_a_w=m_conv_a_w, dn_conv_w=m_dn_conv_w, dn_a_log=m_dn_a_log, dn_dt_bias=m_dn_dt_bias, dn_norm_w=m_dn_norm_w, mix_w_out=m_mix_w_out, swa_wq=m_swa_wq, swa_wk=m_swa_wk, swa_wv=m_swa_wv, swa_q_norm_w=m_swa_q_norm_w, swa_k_norm_w=m_swa_k_norm_w, swa_sinks=m_swa_sinks, swa_wo=m_swa_wo, ffn_w_up=m_ffn_w_up, ffn_conv_w=m_ffn_conv_w, ffn_w_down=m_ffn_w_down)
    var = dict(meta_tokens=v_meta_tokens, attn_norm_w=v_attn_norm_w, ffn_norm_w=v_ffn_norm_w, mix_w_in=v_mix_w_in, conv_a_w=v_conv_a_w, dn_conv_w=v_dn_conv_w, dn_a_log=v_dn_a_log, dn_dt_bias=v_dn_dt_bias, dn_norm_w=v_dn_norm_w, mix_w_out=v_mix_w_out, swa_wq=v_swa_wq, swa_wk=v_swa_wk, swa_wv=v_swa_wv, swa_q_norm_w=v_swa_q_norm_w, swa_k_norm_w=v_swa_k_norm_w, swa_sinks=v_swa_sinks, swa_wo=v_swa_wo, ffn_w_up=v_ffn_w_up, ffn_conv_w=v_ffn_conv_w, ffn_w_down=v_ffn_w_down)
    me = _my_index()

    transposed = ("mix_w_in", "ffn_w_up")
    view = lambda k, a: jnp.swapaxes(a, 1, 2) if k in transposed else a
    axis2d = {k: 0 if k in transposed else SHARD_AXIS[k] - 1 for k in BIG}
    shard16 = {k: view(k, w[k]).astype(MXU_DTYPE) for k in BIG}
    small_shard_shapes = [w[k].shape for k in SMALL_SHARDED]
    rows_in = shard16["mix_w_in"].shape[1]
    sent_in = jnp.pad(shard16["mix_w_in"][0], ((0, -rows_in % (2 * SUBLANES)), (0, 0)))
    got = _all_gather([sent_in, _pack([w[k] for k in SMALL_SHARDED])], name="gather_weights")
    whole = {"mix_w_in_t": _whole(got[0][:, :rows_in], 0)}
    for k, a in zip(SMALL_SHARDED, _unpack(got[1], small_shard_shapes, lead=1)):
        whole[k] = _whole(a, SHARD_AXIS[k])
    for k in SMALL:
        whole.setdefault(k, w[k])
    stages = {"in": [("mix_w_in", 0)], "l0": [("mix_w_out", 0), ("ffn_w_up", 0), ("ffn_w_down", 0)],
              "l1": [("swa_wq", 0), ("swa_wk", 0), ("swa_wv", 0), ("swa_wo", 0), ("ffn_w_up", 1), ("ffn_w_down", 1)]}
    pushed = {}
    follows = jnp.zeros((SUBLANES, LANES), F32)
    for stage in ("l0", "l1"):
        pushed[stage] = _push_start("gather", [shard16[k][l] for k, l in stages[stage]], follows, name=f"push_weights_{stage}")
        follows = pushed[stage]["zero"]
    early = _prepare_early(whole)
    early["anw"][0] = early["anw"][0] + follows[0, 0]

    def fetch(stage, after):
        got = _push_wait(pushed[stage], after, name=f"wait_weights_{stage}")
        full = {kl: _whole(a, axis2d[kl[0]]) for kl, a in zip(stages[stage], got)}
        if stage == "l0":
            return dict(w_out=full["mix_w_out", 0], w_up0_t=full["ffn_w_up", 0], w_down0=full["ffn_w_down", 0])
        wqkv = jnp.concatenate([full["swa_wq", 0], full["swa_wk", 0], full["swa_wv", 0]], 1)
        return dict(wqkv=wqkv, wo=full["swa_wo", 0], w_up1_t=full["ffn_w_up", 1], w_down1=full["ffn_w_down", 1])

    nq, nkv = SWA_HEADS * SWA_HEAD_DIM, SWA_KV_HEADS * SWA_HEAD_DIM
    grad_pushes = {}

    def push(stage, gd):
        if stage == "in":
            named = {("mix_w_in", 0): gd["w_in_t"]}
        elif stage == "l1":
            named = {("swa_wq", 0): gd["wqkv"][:, :nq], ("swa_wk", 0): gd["wqkv"][:, nq:nq + nkv],
                     ("swa_wv", 0): gd["wqkv"][:, nq + nkv:], ("swa_wo", 0): gd["wo"],
                     ("ffn_w_up", 1): gd["w_up_t"], ("ffn_w_down", 1): gd["w_down"]}
        else:
            named = {("mix_w_out", 0): gd["w_out"], ("ffn_w_up", 0): gd["w_up_t"], ("ffn_w_down", 0): gd["w_down"]}
        sent = [_by_owner(named[kl], axis2d[kl[0]]) for kl in stages[stage]]
        grad_pushes[stage] = _push_start("scatter", sent, jnp.zeros((SUBLANES, LANES), F32), name=f"push_grads_{stage}")
        return grad_pushes[stage]["zero"][0, 0]

    loss, g = _local_step(x[0], loss_target[0], early, fetch, push)
    grads = _small_named(g)

    results = {}

    def update(stage, follows):
        got = _push_wait(grad_pushes[stage], follows, name=f"wait_grads_{stage}")
        for (k, l), parts in zip(stages[stage], got):
            w3, m3, v3 = view(k, w[k]), view(k, mom[k]), view(k, var[k])
            results[k] = _adamw(parts.reshape((N_DEV,) + w3.shape[1:]), w3, m3, v3, l, results.get(k), name=f"adamw_{k}_{l}")
        return results[stages[stage][0][0]][0]

    follows = update("l0", update("l1", g["meta"]))

    small_shapes = [grads[k].shape for k in SMALL]
    (all_small,) = _all_gather([_pack([loss] + [grads[k].astype(F32) for k in SMALL])], name="gather_small_grads")
    loss_parts, *small_parts = _unpack(all_small, [loss.shape] + small_shapes, lead=1)
    mine = []
    for k, p in zip(SMALL, small_parts):
        ax = SHARD_AXIS[k]
        mine.append(p if ax is None else lax.dynamic_slice_in_dim(p, me * w[k].shape[ax], w[k].shape[ax], 1 + ax))
    zero = jnp.zeros(loss.shape, F32)
    packed = [_pack([z] + [d[k] for k in SMALL]) for z, d in ((zero, w), (zero, mom), (zero, var))]
    res = _adamw(_pack([loss_parts] + mine, lead=1), *[t[None] for t in packed], 0, name="adamw_small")
    shapes = [loss.shape] + [w[k].shape for k in SMALL]
    for t, which in zip(res, range(4)):
        for k, a in zip(["loss"] + SMALL, _unpack(t[0], shapes)):
            results.setdefault(k, [None] * 4)[which] = a
    update("in", jnp.maximum(follows[0, :1, :1], res[0][0, :1, :1]))

    outs = [[view(k, results[k][which]) for k in WEIGHTS] for which in range(4)]
    return (results["loss"][0][0, 0], g["x"][None], *outs[0], *outs[1], *outs[2], *outs[3])
```

```python
import functools

import jax
import jax.numpy as jnp
from jax import lax
from jax.experimental import pallas as pl
from jax.experimental.pallas import tpu as pltpu

F32 = jnp.float32
BF16 = jnp.bfloat16
MXU_DTYPE = BF16
GRAD_WIRE_DTYPE = BF16

D_MODEL = 1024
N_META = 16
PAD_ROWS = 112
D_CONV = 512
DN_HEADS = 4
DN_HEAD_DIM = 128
DN_DIM = DN_HEADS * DN_HEAD_DIM
DN_CHUNK = 64
SEG = 512
N_SEG = 7
SWA_HEADS = 16
SWA_KV_HEADS = 4
SWA_GROUP = SWA_HEADS // SWA_KV_HEADS
SWA_HEAD_DIM = 64
SWA_BLOCK = 128
D_FF = 2816
EPS = 1e-6
NEG = -1e30
N_DEV = 8

ADAM_LR = 0.001
ADAM_B1 = 0.9
ADAM_B2 = 0.999
ADAM_EPS = 1e-08
ADAM_WD = 0.01
ADAM_STEP = 10

VMEM_LIMIT_BYTES = 52 * 1024 * 1024
SUBLANES = 8
LANES = 128


def _pick(n, prefs):
    for p in prefs:
        if n % p == 0:
            return p
    return n


def _params(sem, vmem=VMEM_LIMIT_BYTES):
    return pltpu.CompilerParams(dimension_semantics=sem, vmem_limit_bytes=vmem)


def _rms(x, w):
    return x * lax.rsqrt(jnp.mean(x * x, -1, keepdims=True) + EPS) * w


def _norm_matmul(h, nw, w, *, o_seg=None, trans_w=False, n=None, name):
    m, k = h.shape
    n = n or (w.shape[0] if trans_w else w.shape[1])
    tm = _pick(m, (1408, 384, 128))
    tn = _pick(o_seg or n, (1408, 1024, 512, 256, 128))
    dims = _DOT_DIMS["nt" if trans_w else "nn"]

    def body(h_ref, nw_ref, w_ref, o_ref, hn_ref, hn_s):
        @pl.when(pl.program_id(1) == 0)
        def _():
            hn = _rms(h_ref[...], nw_ref[...]).astype(MXU_DTYPE)
            hn_s[...] = hn
            hn_ref[...] = hn

        o_ref[...] = lax.dot_general(hn_s[...], w_ref[...], dims, preferred_element_type=F32)

    if o_seg:
        per = o_seg // tn
        o_shape = jax.ShapeDtypeStruct((n // o_seg, m, o_seg), F32)
        o_spec = pl.BlockSpec((None, tm, tn), lambda i, j: (j // per, i, j % per))
    else:
        o_shape = jax.ShapeDtypeStruct((m, n), F32)
        o_spec = pl.BlockSpec((tm, tn), lambda i, j: (i, j))
    return pl.pallas_call(
        body, name=name, grid=(m // tm, n // tn),
        in_specs=[pl.BlockSpec((tm, k), lambda i, j: (i, 0)), pl.BlockSpec((1, k), lambda i, j: (0, 0)),
                  pl.BlockSpec((tn, k), lambda i, j: (j, 0)) if trans_w else pl.BlockSpec((k, tn), lambda i, j: (0, j))],
        out_specs=[o_spec, pl.BlockSpec((tm, k), lambda i, j: (i, 0))],
        out_shape=[o_shape, jax.ShapeDtypeStruct((m, k), MXU_DTYPE)],
        scratch_shapes=[pltpu.VMEM((tm, k), MXU_DTYPE)],
        compiler_params=_params(("parallel", "arbitrary")),
    )(h, nw, w)


TILE_BUDGET_BYTES = 38 * 1024 * 1024
TILE_SIZES = (4224, 2816, 1792, 1536, 1408, 1024, 512, 256, 128)


def _divisor_tiles(n):
    return [t for t in TILE_SIZES if n % t == 0] or [n]


def _mm_nn(a, w, *, res=None, a_seg=False, trans_w=False, w_seg=False, out_dtype=F32, name):
    if a_seg:
        s, m, seg = a.shape
    else:
        m, seg = a.shape
        s = 1
    k = s * seg
    n = w.shape[0] * w.shape[2] if w_seg else (w.shape[0] if trans_w else w.shape[1])
    n_seg = w.shape[2] if w_seg else n
    tm = _pick(m, (1408, 1024, 512, 384, 256, 128))
    ab = a.dtype.itemsize
    k_steps = [(sb, seg) for sb in range(s, 0, -1) if s % sb == 0] if a_seg else [(1, t) for t in _divisor_tiles(seg)]
    best = None
    for tn in _divisor_tiles(n_seg):
        for sb, tk1 in k_steps:
            tk = sb * tk1
            nk = k // tk
            need = 2 * tm * tk * ab + 2 * tk * tn * 2 + 2 * tm * tn * 4 + (tm * tn * 4 if nk > 1 else 0) + (2 * tm * tn * 4 if res is not None else 0)
            if need <= TILE_BUDGET_BYTES and (best is None or tk * tn > best[0]):
                best = (tk * tn, tn, sb, tk1)
    _, tn, sb, tk1 = best
    tk = sb * tk1
    nk = k // tk
    w_dims = _DOT_DIMS["nt" if trans_w else "nn"]

    def body(*refs):
        a_ref, w_ref = refs[:2]
        r_ref = refs[2] if res is not None else None
        o_ref = refs[3 if res is not None else 2]

        def partial_product():
            if not a_seg:
                return lax.dot_general(a_ref[...].astype(MXU_DTYPE), w_ref[...], w_dims, preferred_element_type=F32)
            out = None
            for t in range(sb):
                wt = w_ref[:, t * seg:(t + 1) * seg] if trans_w else w_ref[t * seg:(t + 1) * seg, :]
                d = lax.dot_general(a_ref[t].astype(MXU_DTYPE), wt, w_dims, preferred_element_type=F32)
                out = d if out is None else out + d
            return out

        if nk == 1:
            o_ref[...] = (partial_product() if res is None else partial_product() + r_ref[...]).astype(o_ref.dtype)
            return
        acc = refs[-1]
        kk = pl.program_id(2)

        @pl.when(kk == 0)
        def _():
            acc[...] = jnp.zeros_like(acc)

        acc[...] += partial_product()

        @pl.when(kk == nk - 1)
        def _():
            o_ref[...] = (acc[...] if res is None else acc[...] + r_ref[...]).astype(o_ref.dtype)

    a_spec = pl.BlockSpec((sb, tm, seg), lambda i, j, kk: (kk, i, 0)) if a_seg else pl.BlockSpec((tm, tk), lambda i, j, kk: (i, kk))
    if w_seg:
        per = n_seg // tn
        w_spec = pl.BlockSpec((None, tk, tn), lambda i, j, kk: (j // per, kk, j % per))
    elif trans_w:
        w_spec = pl.BlockSpec((tn, tk), lambda i, j, kk: (j, kk))
    else:
        w_spec = pl.BlockSpec((tk, tn), lambda i, j, kk: (kk, j))
    in_specs = [a_spec, w_spec]
    args = [a, w]
    if res is not None:
        in_specs.append(pl.BlockSpec((tm, tn), lambda i, j, kk: (i, j)))
        args.append(res)
    return pl.pallas_call(
        body, name=name, grid=(m // tm, n // tn, nk), in_specs=in_specs,
        out_specs=pl.BlockSpec((tm, tn), lambda i, j, kk: (i, j)),
        out_shape=jax.ShapeDtypeStruct((m, n), out_dtype),
        scratch_shapes=[pltpu.VMEM((tm, tn), F32)] if nk > 1 else [],
        compiler_params=_params(("parallel", "parallel", "arbitrary")),
    )(*args)


def _mm_tn(a, b, *, a_seg=False, b_seg=False, out_dtype=None, name):
    out_dtype = out_dtype or GRAD_WIRE_DTYPE
    if a_seg:
        sa, m, a_unit = a.shape
        ka = sa * a_unit
    else:
        m, ka = a.shape
        a_unit = ka
    if b_seg:
        s, _, seg = b.shape
        n = s * seg
    else:
        n = b.shape[1]
        seg = n
    tmc = _pick(m, (1408, 384, 128))
    best = None
    for tka in _divisor_tiles(a_unit):
        for tn in _divisor_tiles(seg):
            need = 2 * tmc * tka * a.dtype.itemsize + 2 * tmc * tn * b.dtype.itemsize + tka * tn * 4 + 2 * tka * tn * 4
            if need <= TILE_BUDGET_BYTES and (best is None or (tka * tn, tn) > best[:2]):
                best = (tka * tn, tn, tka)
    _, tn, tka = best
    nm = m // tmc

    def body(a_ref, b_ref, o_ref, acc):
        mm = pl.program_id(2)

        @pl.when(mm == 0)
        def _():
            acc[...] = jnp.zeros_like(acc)

        acc[...] += lax.dot_general(a_ref[...].astype(MXU_DTYPE), b_ref[...].astype(MXU_DTYPE),
                                    (((0,), (0,)), ((), ())), preferred_element_type=F32)

        @pl.when(mm == nm - 1)
        def _():
            o_ref[...] = acc[...].astype(o_ref.dtype)

    if b_seg:
        per = seg // tn
        b_spec = pl.BlockSpec((None, tmc, tn), lambda i, j, mm: (j // per, mm, j % per))
    else:
        b_spec = pl.BlockSpec((tmc, tn), lambda i, j, mm: (mm, j))
    if a_seg:
        a_per = a_unit // tka
        a_spec = pl.BlockSpec((None, tmc, tka), lambda i, j, mm: (i // a_per, mm, i % a_per))
    else:
        a_spec = pl.BlockSpec((tmc, tka), lambda i, j, mm: (mm, i))
    return pl.pallas_call(
        body, name=name, grid=(ka // tka, n // tn, nm),
        in_specs=[a_spec, b_spec],
        out_specs=pl.BlockSpec((tka, tn), lambda i, j, mm: (i, j)),
        out_shape=jax.ShapeDtypeStruct((ka, n), out_dtype),
        scratch_shapes=[pltpu.VMEM((tka, tn), F32)],
        compiler_params=_params(("parallel", "parallel", "arbitrary")),
    )(a, b)


def _rmsnorm_bwd(dhn, h, nw, dres, *, name):
    m, d = h.shape
    tm = _pick(m, (384, 128))

    def body(dhn_ref, h_ref, nw_ref, dres_ref, dh_ref, dnw_ref):
        i = pl.program_id(0)
        x = h_ref[...]
        r = lax.rsqrt(jnp.mean(x * x, -1, keepdims=True) + EPS)
        xh = x * r
        dy = dhn_ref[...]
        dxh = dy * nw_ref[...]
        dx = r * (dxh - xh * jnp.mean(dxh * xh, -1, keepdims=True))
        row = i * tm + lax.broadcasted_iota(jnp.int32, (tm, 1), 0)
        dh_ref[...] = jnp.where(row >= PAD_ROWS, dres_ref[...] + dx, 0.0)

        @pl.when(i == 0)
        def _():
            dnw_ref[...] = jnp.zeros_like(dnw_ref)

        dnw_ref[...] += jnp.sum(dy * xh, 0, keepdims=True)

    return pl.pallas_call(
        body, name=name, grid=(m // tm,),
        in_specs=[pl.BlockSpec((tm, d), lambda i: (i, 0)), pl.BlockSpec((tm, d), lambda i: (i, 0)),
                  pl.BlockSpec((1, d), lambda i: (0, 0)), pl.BlockSpec((tm, d), lambda i: (i, 0))],
        out_specs=[pl.BlockSpec((tm, d), lambda i: (i, 0)), pl.BlockSpec((1, d), lambda i: (0, 0))],
        out_shape=[jax.ShapeDtypeStruct((m, d), F32), jax.ShapeDtypeStruct((1, d), F32)],
        compiler_params=_params(("arbitrary",)),
    )(dhn, h, nw, dres)


ROW_CHUNK = 248


def _row_chunks(m):
    out, s = [], SUBLANES
    while s < m:
        n = min(ROW_CHUNK, m - s)
        out.append((s, n))
        s += n
    return out


def _conv_at(load, w, width, s, n):
    acc = w[width - 1:width, :] * load(s, n)
    for j in range(width - 1):
        acc = acc + w[j:j + 1, :] * load(s - (width - 1 - j), n)
    return acc


def _conv_t_at(load, w, width, s, n):
    acc = w[width - 1:width, :] * load(s, n)
    for j in range(width - 1):
        acc = acc + w[j:j + 1, :] * load(s + (width - 1 - j), n)
    return acc


def _dconv_w(load_x, d, width, s, n):
    rows = [jnp.sum(d * load_x(s - (width - 1 - j), n), 0, keepdims=True) for j in range(width)]
    rows.append(jnp.zeros((SUBLANES - width, d.shape[1]), F32))
    return jnp.concatenate(rows, 0)


def _pad_w(w):
    return jnp.concatenate([w, jnp.zeros((SUBLANES - w.shape[0], w.shape[1]), w.dtype)], 0)


def _sigmoid(x):
    return 1.0 / (1.0 + jnp.exp(-x))


def _ffn_act_fwd(u, cw, *, name):
    _, m, f = u.shape
    cb = _pick(f, (256, 128))
    chunks = _row_chunks(m)

    def body(g_ref, v_ref, w_ref, o_ref):
        w = w_ref[...]
        o_ref[pl.ds(0, SUBLANES), :] = jnp.zeros((SUBLANES, cb), o_ref.dtype)
        for s, n in chunks:
            c = _conv_at(lambda a, b: g_ref[pl.ds(a, b), :], w, 3, s, n)
            o_ref[pl.ds(s, n), :] = (c * _sigmoid(c) * v_ref[pl.ds(s, n), :]).astype(o_ref.dtype)

    return pl.pallas_call(
        body, name=name, grid=(f // cb,),
        in_specs=[pl.BlockSpec((None, m, cb), lambda j: (0, 0, j)), pl.BlockSpec((None, m, cb), lambda j: (1, 0, j)),
                  pl.BlockSpec((SUBLANES, cb), lambda j: (0, j))],
        out_specs=pl.BlockSpec((m, cb), lambda j: (0, j)),
        out_shape=jax.ShapeDtypeStruct((m, f), MXU_DTYPE),
        compiler_params=_params(("parallel",)),
    )(u, u, cw)


def _ffn_act_bwd(da, u, cw, *, name):
    _, m, f = u.shape
    cb = LANES
    chunks = _row_chunks(m)

    def body(da_ref, g_ref, v_ref, w_ref, du_ref, dw_ref, dg_s):
        w = w_ref[...]
        zeros8 = jnp.zeros((SUBLANES, cb), F32)
        dg_s[pl.ds(0, SUBLANES), :] = zeros8
        dg_s[pl.ds(m, SUBLANES), :] = zeros8
        du_ref[0, pl.ds(0, SUBLANES), :] = zeros8.astype(du_ref.dtype)
        du_ref[1, pl.ds(0, SUBLANES), :] = zeros8.astype(du_ref.dtype)
        load_g = lambda a, b: g_ref[pl.ds(a, b), :]
        dw = jnp.zeros((SUBLANES, cb), F32)
        for s, n in chunks:
            c = _conv_at(load_g, w, 3, s, n)
            sg = _sigmoid(c)
            d = da_ref[pl.ds(s, n), :]
            du_ref[1, pl.ds(s, n), :] = (d * (c * sg)).astype(du_ref.dtype)
            dc = d * v_ref[pl.ds(s, n), :] * (sg * (1.0 + c * (1.0 - sg)))
            dg_s[pl.ds(s, n), :] = dc
            dw = dw + _dconv_w(load_g, dc, 3, s, n)
        dw_ref[...] = dw
        for s, n in chunks:
            du_ref[0, pl.ds(s, n), :] = _conv_t_at(lambda a, b: dg_s[pl.ds(a, b), :], w, 3, s, n).astype(du_ref.dtype)

    return pl.pallas_call(
        body, name=name, grid=(f // cb,),
        in_specs=[pl.BlockSpec((m, cb), lambda j: (0, j)), pl.BlockSpec((None, m, cb), lambda j: (0, 0, j)),
                  pl.BlockSpec((None, m, cb), lambda j: (1, 0, j)), pl.BlockSpec((SUBLANES, cb), lambda j: (0, j))],
        out_specs=[pl.BlockSpec((2, m, cb), lambda j: (0, 0, j)), pl.BlockSpec((SUBLANES, cb), lambda j: (0, j))],
        out_shape=[jax.ShapeDtypeStruct((2, m, f), MXU_DTYPE), jax.ShapeDtypeStruct((SUBLANES, f), F32)],
        scratch_shapes=[pltpu.VMEM((m + SUBLANES, cb), F32)],
        compiler_params=_params(("parallel",)),
    )(da, u, u, cw)


def _shortconv_fwd(pm, cw, y, *, name):
    _, m, seg = pm.shape
    cb = _pick(seg, (256, 128))
    chunks = _row_chunks(m)

    def body(gi_ref, go_ref, ah_ref, w_ref, y_in, o_ref):
        del y_in
        w = w_ref[...]
        o_ref[pl.ds(0, SUBLANES), :] = jnp.zeros((SUBLANES, cb), o_ref.dtype)
        load_m = lambda a, b: gi_ref[pl.ds(a, b), :] * ah_ref[pl.ds(a, b), :]
        for s, n in chunks:
            o_ref[pl.ds(s, n), :] = (go_ref[pl.ds(s, n), :] * _conv_at(load_m, w, 3, s, n)).astype(o_ref.dtype)

    return pl.pallas_call(
        body, name=name, grid=(seg // cb,),
        in_specs=[pl.BlockSpec((None, m, cb), lambda j: (0, 0, j)), pl.BlockSpec((None, m, cb), lambda j: (1, 0, j)),
                  pl.BlockSpec((None, m, cb), lambda j: (2, 0, j)), pl.BlockSpec((SUBLANES, cb), lambda j: (0, j)),
                  pl.BlockSpec(memory_space=pl.ANY)],
        out_specs=pl.BlockSpec((m, cb), lambda j: (0, j)),
        out_shape=jax.ShapeDtypeStruct(y.shape, y.dtype),
        input_output_aliases={4: 0},
        compiler_params=_params(("parallel",)),
    )(pm, pm, pm, cw, y)


def _shortconv_bwd(dy, pm, cw, dpm, *, name):
    _, m, seg = pm.shape
    cb = LANES
    chunks = _row_chunks(m)

    def body(dy_ref, gi_ref, go_ref, ah_ref, w_ref, dpm_in, dp_ref, dw_ref, dc_s):
        del dpm_in
        w = w_ref[...]
        zeros8 = jnp.zeros((SUBLANES, cb), F32)
        dc_s[pl.ds(0, SUBLANES), :] = zeros8
        dc_s[pl.ds(m, SUBLANES), :] = zeros8
        for t in range(3):
            dp_ref[t, pl.ds(0, SUBLANES), :] = zeros8.astype(dp_ref.dtype)
        load_m = lambda a, b: gi_ref[pl.ds(a, b), :] * ah_ref[pl.ds(a, b), :]
        dw = jnp.zeros((SUBLANES, cb), F32)
        for s, n in chunks:
            d = dy_ref[pl.ds(s, n), :]
            dp_ref[1, pl.ds(s, n), :] = (d * _conv_at(load_m, w, 3, s, n)).astype(dp_ref.dtype)
            dc = d * go_ref[pl.ds(s, n), :]
            dc_s[pl.ds(s, n), :] = dc
            dw = dw + _dconv_w(load_m, dc, 3, s, n)
        dw_ref[...] = dw
        for s, n in chunks:
            dm = _conv_t_at(lambda a, b: dc_s[pl.ds(a, b), :], w, 3, s, n)
            dp_ref[0, pl.ds(s, n), :] = (dm * ah_ref[pl.ds(s, n), :]).astype(dp_ref.dtype)
            dp_ref[2, pl.ds(s, n), :] = (dm * gi_ref[pl.ds(s, n), :]).astype(dp_ref.dtype)

    return pl.pallas_call(
        body, name=name, grid=(seg // cb,),
        in_specs=[pl.BlockSpec((m, cb), lambda j: (0, j)), pl.BlockSpec((None, m, cb), lambda j: (0, 0, j)),
                  pl.BlockSpec((None, m, cb), lambda j: (1, 0, j)), pl.BlockSpec((None, m, cb), lambda j: (2, 0, j)),
                  pl.BlockSpec((SUBLANES, cb), lambda j: (0, j)), pl.BlockSpec(memory_space=pl.ANY)],
        out_specs=[pl.BlockSpec((3, m, cb), lambda j: (0, 0, j)), pl.BlockSpec((SUBLANES, cb), lambda j: (0, j))],
        out_shape=[jax.ShapeDtypeStruct(dpm.shape, dpm.dtype), jax.ShapeDtypeStruct((SUBLANES, seg), F32)],
        scratch_shapes=[pltpu.VMEM((m + SUBLANES, cb), F32)],
        input_output_aliases={5: 0},
        compiler_params=_params(("parallel",)),
    )(dy, pm, pm, pm, cw, dpm)


def _dnpre_fwd(pm, cw, *, name):
    _, m, seg = pm.shape
    cb = _pick(seg, (256, 128))
    per = seg // cb
    chunks = _row_chunks(m)

    def body(x_ref, w_ref, o_ref):
        w = w_ref[...]
        o_ref[pl.ds(0, SUBLANES), :] = jnp.zeros((SUBLANES, cb), F32)
        for s, n in chunks:
            c = _conv_at(lambda a, b: x_ref[pl.ds(a, b), :], w, 4, s, n)
            o_ref[pl.ds(s, n), :] = c * _sigmoid(c)

    return pl.pallas_call(
        body, name=name, grid=(3 * per,),
        in_specs=[pl.BlockSpec((None, m, cb), lambda j: (3 + j // per, 0, j % per)), pl.BlockSpec((SUBLANES, cb), lambda j: (0, j))],
        out_specs=pl.BlockSpec((None, m, cb), lambda j: (j // per, 0, j % per)),
        out_shape=jax.ShapeDtypeStruct((3, m, seg), F32),
        compiler_params=_params(("parallel",)),
    )(pm, cw)


def _dnpre_bwd(dqkv, pm, cw, dpm, *, name):
    _, m, seg = pm.shape
    cb = _pick(seg, (256, 128))
    per = seg // cb
    chunks = _row_chunks(m)

    def body(d_ref, x_ref, w_ref, dpm_in, dp_ref, dw_ref, dc_s):
        del dpm_in
        w = w_ref[...]
        zeros8 = jnp.zeros((SUBLANES, cb), F32)
        dc_s[pl.ds(0, SUBLANES), :] = zeros8
        dc_s[pl.ds(m, SUBLANES), :] = zeros8
        dp_ref[pl.ds(0, SUBLANES), :] = zeros8.astype(dp_ref.dtype)
        load_x = lambda a, b: x_ref[pl.ds(a, b), :]
        dw = jnp.zeros((SUBLANES, cb), F32)
        for s, n in chunks:
            c = _conv_at(load_x, w, 4, s, n)
            sg = _sigmoid(c)
            dc = d_ref[pl.ds(s, n), :] * (sg * (1.0 + c * (1.0 - sg)))
            dc_s[pl.ds(s, n), :] = dc
            dw = dw + _dconv_w(load_x, dc, 4, s, n)
        dw_ref[...] = dw
        for s, n in chunks:
            dp_ref[pl.ds(s, n), :] = _conv_t_at(lambda a, b: dc_s[pl.ds(a, b), :], w, 4, s, n).astype(dp_ref.dtype)

    return pl.pallas_call(
        body, name=name, grid=(3 * per,),
        in_specs=[pl.BlockSpec((None, m, cb), lambda j: (j // per, 0, j % per)),
                  pl.BlockSpec((None, m, cb), lambda j: (3 + j // per, 0, j % per)),
                  pl.BlockSpec((SUBLANES, cb), lambda j: (0, j)), pl.BlockSpec(memory_space=pl.ANY)],
        out_specs=[pl.BlockSpec((None, m, cb), lambda j: (3 + j // per, 0, j % per)), pl.BlockSpec((SUBLANES, cb), lambda j: (0, j))],
        out_shape=[jax.ShapeDtypeStruct(dpm.shape, dpm.dtype), jax.ShapeDtypeStruct((SUBLANES, 3 * seg), F32)],
        scratch_shapes=[pltpu.VMEM((m + SUBLANES, cb), F32)],
        input_output_aliases={3: 0},
        compiler_params=_params(("parallel",)),
    )(dqkv, pm, cw, dpm)


def _mxu_dot_impl(a, b, form):
    a = a.astype(MXU_DTYPE)
    b = b.astype(MXU_DTYPE)
    dims = {"nn": (((1,), (0,)), ((), ())), "nt": (((1,), (1,)), ((), ())), "tn": (((0,), (0,)), ((), ()))}[form]
    return lax.dot_general(a, b, dims, preferred_element_type=F32)


@functools.partial(jax.custom_vjp, nondiff_argnums=(2,))
def _mxu_dot(a, b, form):
    return _mxu_dot_impl(a, b, form)


def _mxu_dot_fwd(a, b, form):
    return _mxu_dot_impl(a, b, form), (a, b)


def _mxu_dot_bwd(form, saved, g):
    a, b = saved
    if form == "nn":
        return _mxu_dot_impl(g, b, "nt"), _mxu_dot_impl(a, g, "tn")
    if form == "nt":
        return _mxu_dot_impl(g, b, "nn"), _mxu_dot_impl(g, a, "tn")
    return _mxu_dot_impl(b, g, "nt"), _mxu_dot_impl(a, g, "nn")


_mxu_dot.defvjp(_mxu_dot_fwd, _mxu_dot_bwd)


_DOT_DIMS = {"nn": (((1,), (0,)), ((), ())), "nt": (((1,), (1,)), ((), ())), "tn": (((0,), (0,)), ((), ()))}


def _split(x):
    hi = x.astype(BF16)
    return hi, (x - hi.astype(F32)).astype(BF16)


def _dot3_impl(a, b, form):
    dg = lambda p, q: lax.dot_general(p, q, _DOT_DIMS[form], preferred_element_type=F32)
    ah, al = _split(a)
    bh, bl = _split(b)
    return dg(ah, bh) + (dg(ah, bl) + dg(al, bh))


@functools.partial(jax.custom_vjp, nondiff_argnums=(2,))
def _dot3(a, b, form):
    return _dot3_impl(a, b, form)


def _dot3_fwd(a, b, form):
    return _dot3_impl(a, b, form), (a, b)


def _dot3_bwd(form, saved, g):
    a, b = saved
    if form == "nn":
        return _dot3_impl(g, b, "nt"), _dot3_impl(a, g, "tn")
    if form == "nt":
        return _dot3_impl(g, b, "nn"), _dot3_impl(g, a, "tn")
    return _dot3_impl(b, g, "nt"), _dot3_impl(a, g, "nn")


_dot3.defvjp(_dot3_fwd, _dot3_bwd)


def _hdot(a, b):
    return _dot3(a, b, "nn")


def _mask_dot(mask, x, form):
    dg = lambda q: lax.dot_general(mask.astype(BF16), q, _DOT_DIMS[form], preferred_element_type=F32)
    x1 = x.astype(BF16)
    r1 = x - x1.astype(F32)
    x2 = r1.astype(BF16)
    x3 = (r1 - x2.astype(F32)).astype(BF16)
    return dg(x1) + (dg(x2) + dg(x3))


def _decay_masks(c):
    row = lax.broadcasted_iota(jnp.int32, (c, c), 0)
    col = lax.broadcasted_iota(jnp.int32, (c, c), 1)
    return (row >= col).astype(F32), row <= col


def _decay_impl(gb):
    lower, upper = _decay_masks(gb.shape[0])
    return _mask_dot(lower, gb, "nn"), _mask_dot(jnp.ones_like(gb), jnp.where(upper, gb, 0.0), "nn")


@jax.custom_vjp
def _decay_matrices(gb):
    return _decay_impl(gb)


def _decay_fwd(gb):
    return _decay_impl(gb), None


def _decay_bwd(_, cts):
    gc, gr = cts
    lower, upper = _decay_masks(gc.shape[0])
    return (_mask_dot(lower, gc, "tn") + jnp.where(upper, _mask_dot(jnp.ones_like(gr), gr, "tn"), 0.0),)


_decay_matrices.defvjp(_decay_fwd, _decay_bwd)


def _softplus(x):
    return jnp.maximum(x, 0.0) + jnp.log(1.0 + jnp.exp(-jnp.abs(x)))


def _heads(f, *lists):
    return [f(*t) for t in zip(*lists)]


DN_STEP_CHUNKS = 3


def _dn_chunk(qr, kr, v, z, braw, araw, alog, dtb, nw, state, valid):
    c = DN_CHUNK
    nh = len(state)
    chunks = len(qr) // nh
    alog, dtb, valid_i = alog * chunks, dtb * chunks, [vv for vv in valid for _ in range(nh)]
    row = lax.broadcasted_iota(jnp.int32, (c, c), 0)
    col = lax.broadcasted_iota(jnp.int32, (c, c), 1)
    incl = row >= col
    strict = row > col
    eye = jnp.where(row == col, 1.0, 0.0)
    q = _heads(lambda t: t * lax.rsqrt(jnp.sum(t * t, -1, keepdims=True) + EPS) * (DN_HEAD_DIM ** -0.5), qr)
    k = _heads(lambda t: t * lax.rsqrt(jnp.sum(t * t, -1, keepdims=True) + EPS), kr)
    beta = _heads(lambda t, vv: _sigmoid(t) * vv, braw, valid_i)
    g = _heads(lambda al, ar, dt, vv: -jnp.exp(al) * _softplus(ar + dt) * vv, alog, araw, dtb, valid_i)
    decay = _heads(lambda t: _decay_matrices(jnp.broadcast_to(t, (c, c))), g)
    dmask = _heads(lambda d: jnp.where(incl, jnp.exp(jnp.where(incl, d[0] - d[1], 0.0)), 0.0), decay)
    dec = _heads(lambda d: d[0][:, :1], decay)
    dlast = _heads(lambda d: d[0][c - 1:c, :1], decay)
    kk = _heads(lambda t: _mxu_dot(t, t, "nt"), k)
    a = _heads(lambda b, t, d: jnp.where(strict, b * t * d, 0.0), beta, kk, dmask)
    x = _heads(lambda t: eye - t, a)
    p = _heads(_hdot, a, a)
    for it in range(5):
        x = _heads(lambda s, t: s + _hdot(s, t), x, p)
        if it < 4:
            p = _heads(_hdot, p, p)
    u = _heads(lambda s, t, b: _hdot(s, t * b), x, v, beta)
    w = _heads(lambda s, t, b, d: _hdot(s, t * (b * jnp.exp(d))), x, k, beta, dec)
    qk = _heads(lambda s, t, d: _mxu_dot(s, t, "nt") * d, q, k, dmask)
    q_dec = _heads(lambda t, d: t * jnp.exp(d), q, dec)
    k_dec = _heads(lambda t, dl, d: t * jnp.exp(dl - d), k, dlast, dec)
    o = []
    for ci in range(chunks):
        of = lambda lst: lst[ci * nh:(ci + 1) * nh]
        v_new = _heads(lambda s, t, st: s - _mxu_dot(t, st, "nn"), of(u), of(w), state)
        o += _heads(lambda qd, st, s, vn: _mxu_dot(qd, st, "nn") + _mxu_dot(s, vn, "nn"), of(q_dec), state, of(qk), v_new)
        state = _heads(lambda st, dl, kd, vn: st * jnp.exp(dl) + _mxu_dot(kd, vn, "tn"), state, of(dlast), of(k_dec), v_new)
    y = _heads(lambda t, zz: _rms(t, nw) * (zz * _sigmoid(zz)), o, z)
    return y, state


DN_STEP_ROWS = DN_STEP_CHUNKS * DN_CHUNK
DN_ITEMS = [(ci, h) for ci in range(DN_STEP_CHUNKS) for h in range(DN_HEADS)]


def _dn_valid(n):
    rows = [n * DN_STEP_ROWS + ci * DN_CHUNK + lax.broadcasted_iota(jnp.int32, (DN_CHUNK, 1), 0) for ci in range(DN_STEP_CHUNKS)]
    return [(r >= PAD_ROWS).astype(F32) for r in rows]


def _dn_in_specs(rev, nc):
    cn = (lambda n: nc - 1 - n) if rev else (lambda n: n)
    c, hd = DN_STEP_ROWS, DN_HEAD_DIM
    return [
        pl.BlockSpec((None, c, DN_DIM), lambda n: (0, cn(n), 0)),
        pl.BlockSpec((None, c, DN_DIM), lambda n: (1, cn(n), 0)),
        pl.BlockSpec((None, c, DN_DIM), lambda n: (2, cn(n), 0)),
        pl.BlockSpec((None, c, DN_DIM), lambda n: (6, cn(n), 0)),
        pl.BlockSpec((DN_HEADS, 2, c, 1), lambda n: (0, 0, cn(n), 0)),
        pl.BlockSpec((DN_HEADS, SUBLANES, LANES), lambda n: (0, 0, 0)),
        pl.BlockSpec((1, hd), lambda n: (0, 0)),
    ]


def _rows(ci):
    return slice(ci * DN_CHUNK, (ci + 1) * DN_CHUNK)


def _cols(h):
    return slice(h * DN_HEAD_DIM, (h + 1) * DN_HEAD_DIM)


def _dn_load(q_ref, k_ref, v_ref, z_ref, ba_ref, hp_ref):
    heads = range(DN_HEADS)
    item = lambda ref: [ref[_rows(ci), _cols(h)] for ci, h in DN_ITEMS]
    return (item(q_ref), item(k_ref), item(v_ref), item(z_ref),
            [ba_ref[h, 0, _rows(ci), :] for ci, h in DN_ITEMS], [ba_ref[h, 1, _rows(ci), :] for ci, h in DN_ITEMS],
            [hp_ref[h, 0:1, 0:1] for h in heads], [hp_ref[h, 1:2, 0:1] for h in heads])


def _delta_fwd(qkvc, pm, ba, hp, nw, *, name):
    _, m, _ = qkvc.shape
    nc = m // DN_STEP_ROWS
    hd = DN_HEAD_DIM

    def body(q_ref, k_ref, v_ref, z_ref, ba_ref, hp_ref, nw_ref, y_ref, s_ref, state):
        n = pl.program_id(0)

        @pl.when(n == 0)
        def _():
            state[...] = jnp.zeros_like(state)

        heads = range(DN_HEADS)
        old = [state[h] for h in heads]
        y, new = _dn_chunk(*_dn_load(q_ref, k_ref, v_ref, z_ref, ba_ref, hp_ref), nw_ref[...], old, _dn_valid(n))
        for h in heads:
            s_ref[h] = old[h]
            state[h] = new[h]
        for (ci, h), yy in zip(DN_ITEMS, y):
            y_ref[_rows(ci), _cols(h)] = yy.astype(y_ref.dtype)

    return pl.pallas_call(
        body, name=name, grid=(nc,), in_specs=_dn_in_specs(False, nc),
        out_specs=[pl.BlockSpec((DN_STEP_ROWS, DN_DIM), lambda n: (n, 1)), pl.BlockSpec((DN_HEADS, None, hd, hd), lambda n: (0, n, 0, 0))],
        out_shape=[jax.ShapeDtypeStruct((m, D_CONV + DN_DIM), MXU_DTYPE), jax.ShapeDtypeStruct((DN_HEADS, nc, hd, hd), F32)],
        scratch_shapes=[pltpu.VMEM((DN_HEADS, hd, hd), F32)],
        compiler_params=_params(("arbitrary",)),
    )(qkvc, qkvc, qkvc, pm, ba, hp, nw)


def _delta_bwd(dy, qkvc, pm, ba, hp, nw, states, *, name):
    _, m, _ = qkvc.shape
    nc = m // DN_STEP_ROWS
    hd, c = DN_HEAD_DIM, DN_STEP_ROWS

    def body(q_ref, k_ref, v_ref, z_ref, ba_ref, hp_ref, nw_ref, s_ref, dy_ref,
             dz_ref, dqkv_ref, dba_ref, dhp_ref, dnw_ref, dstate):
        step = pl.program_id(0)
        n = nc - 1 - step

        @pl.when(step == 0)
        def _():
            dstate[...] = jnp.zeros_like(dstate)
            dhp_ref[...] = jnp.zeros_like(dhp_ref)
            dnw_ref[...] = jnp.zeros_like(dnw_ref)

        valid = _dn_valid(n)
        heads = range(DN_HEADS)
        fn = lambda *a: _dn_chunk(*a, valid)
        _, vjp = jax.vjp(fn, *_dn_load(q_ref, k_ref, v_ref, z_ref, ba_ref, hp_ref), nw_ref[...], [s_ref[h] for h in heads])
        dy = [dy_ref[_rows(ci), _cols(h)] for ci, h in DN_ITEMS]
        dq, dk, dv, dz, dbr, dar, dalog, ddtb, dnw, dst = vjp((dy, [dstate[h] for h in heads]))
        for i, (ci, h) in enumerate(DN_ITEMS):
            dqkv_ref[0, _rows(ci), _cols(h)] = dq[i]
            dqkv_ref[1, _rows(ci), _cols(h)] = dk[i]
            dqkv_ref[2, _rows(ci), _cols(h)] = dv[i]
            dz_ref[_rows(ci), _cols(h)] = dz[i].astype(dz_ref.dtype)
            dba_ref[h, 0, _rows(ci), :] = dbr[i]
            dba_ref[h, 1, _rows(ci), :] = dar[i]
        for h in heads:
            dstate[h] = dst[h]
            dhp_ref[h] += jnp.concatenate([jnp.broadcast_to(dalog[h], (1, LANES)), jnp.broadcast_to(ddtb[h], (1, LANES)),
                                           jnp.zeros((SUBLANES - 2, LANES), F32)], 0)
        dnw_ref[...] += dnw

    rn = lambda n: nc - 1 - n
    in_specs = _dn_in_specs(True, nc) + [
        pl.BlockSpec((DN_HEADS, None, hd, hd), lambda n: (0, rn(n), 0, 0)),
        pl.BlockSpec((c, DN_DIM), lambda n: (rn(n), 1)),
    ]
    out_specs = [
        pl.BlockSpec((None, c, DN_DIM), lambda n: (6, rn(n), 0)),
        pl.BlockSpec((3, c, DN_DIM), lambda n: (0, rn(n), 0)),
        pl.BlockSpec((DN_HEADS, 2, c, 1), lambda n: (0, 0, rn(n), 0)),
        pl.BlockSpec((DN_HEADS, SUBLANES, LANES), lambda n: (0, 0, 0)),
        pl.BlockSpec((1, hd), lambda n: (0, 0)),
    ]
    return pl.pallas_call(
        body, name=name, grid=(nc,), in_specs=in_specs, out_specs=out_specs,
        out_shape=[jax.ShapeDtypeStruct(pm.shape, MXU_DTYPE), jax.ShapeDtypeStruct(qkvc.shape, F32),
                   jax.ShapeDtypeStruct(ba.shape, F32), jax.ShapeDtypeStruct(hp.shape, F32),
                   jax.ShapeDtypeStruct((1, hd), F32)],
        scratch_shapes=[pltpu.VMEM((DN_HEADS, hd, hd), F32)],
        compiler_params=_params(("arbitrary",)),
    )(qkvc, qkvc, qkvc, pm, ba, hp, nw, states, dy)


SWA_PAIR = 2


def _attn_block(q, k0, kp, kc, v0, vp, vc, qw, kw, sink, n):
    g, b, hd = SWA_GROUP, SWA_BLOCK, SWA_HEAD_DIM
    pair = list(range(SWA_PAIR))
    lanes = lambda t, e: t[:, e * hd:(e + 1) * hd]
    q4 = [jnp.concatenate([lanes(q, e * g + i)[None] for i in range(g)], 0) for e in pair]
    qn = _heads(lambda t: _rms(t, qw) * (hd ** -0.5), q4)
    kn = [_rms(jnp.concatenate([lanes(k0, e), lanes(kp, e), lanes(kc, e)], 0), kw) for e in pair]
    vcat = [jnp.concatenate([lanes(v0, e), lanes(vp, e), lanes(vc, e)], 0) for e in pair]
    s = _heads(lambda a, k: _mxu_dot(a.reshape(g * b, hd), k, "nt").reshape(g, b, 3 * b), qn, kn)
    i = lax.broadcasted_iota(jnp.int32, (b, 3 * b), 0)
    c = lax.broadcasted_iota(jnp.int32, (b, 3 * b), 1)
    in_meta, in_prev, in_cur = c < b, (c >= b) & (c < 2 * b), c >= 2 * b
    j = c - jnp.where(in_meta, 0, jnp.where(in_prev, b, 2 * b))
    meta_lo = jnp.where(n == 0, b, PAD_ROWS)
    cur_lo = jnp.where(n == 0, PAD_ROWS, 0)
    prev_off = jnp.where(n >= 2, 0, 2 * b)
    valid = (in_meta & (j >= meta_lo)) | (in_prev & (j > i + prev_off)) | (in_cur & (j <= i) & (j >= cur_lo))
    s = _heads(lambda t: jnp.where(valid[None], t, NEG), s)
    m = [lax.stop_gradient(jnp.maximum(jnp.max(t, -1, keepdims=True), sink[e])) for e, t in zip(pair, s)]
    ex = _heads(lambda t, mm: jnp.exp(t - mm), s, m)
    p = [t / (jnp.sum(t, -1, keepdims=True) + jnp.exp(sink[e] - mm)) for e, t, mm in zip(pair, ex, m)]
    o = _heads(lambda t, v: _mxu_dot(t.reshape(g * b, 3 * b), v, "nn").reshape(g, b, hd), p, vcat)
    return jnp.concatenate([o[e][i] for e in pair for i in range(g)], 1)


Q_LANES = SWA_PAIR * SWA_GROUP * SWA_HEAD_DIM
KV_LANES = SWA_PAIR * SWA_HEAD_DIM
K_BLOCK0 = SWA_HEADS * SWA_HEAD_DIM // KV_LANES
V_BLOCK0 = K_BLOCK0 + SWA_KV_HEADS * SWA_HEAD_DIM // KV_LANES


def _attn_in_specs():
    g, b, hd = SWA_GROUP, SWA_BLOCK, SWA_HEAD_DIM
    kv = lambda f, first: pl.BlockSpec((b, KV_LANES), lambda p, n: (f(n), first + p))
    blocks = [lambda n: 0, lambda n: jnp.maximum(n - 1, 0), lambda n: n]
    return ([pl.BlockSpec((b, Q_LANES), lambda p, n: (n, p))] + [kv(f, K_BLOCK0) for f in blocks] + [kv(f, V_BLOCK0) for f in blocks]
            + [pl.BlockSpec((1, hd), lambda p, n: (0, 0)), pl.BlockSpec((1, hd), lambda p, n: (0, 0)),
               pl.BlockSpec((SWA_PAIR, g, 1, 1), lambda p, n: (p, 0, 0, 0))])


def _attn_fwd(qkv, qw, kw, sink, *, name):
    m = qkv.shape[0]
    b = SWA_BLOCK

    def body(q_ref, k0, kp, kc, v0, vp, vc, qw_ref, kw_ref, s_ref, o_ref):
        o_ref[...] = _attn_block(q_ref[...], k0[...], kp[...], kc[...], v0[...], vp[...], vc[...], qw_ref[...], kw_ref[...],
                                 s_ref[...], pl.program_id(1)).astype(o_ref.dtype)

    return pl.pallas_call(
        body, name=name, grid=(SWA_KV_HEADS // SWA_PAIR, m // b), in_specs=_attn_in_specs(),
        out_specs=pl.BlockSpec((b, Q_LANES), lambda p, n: (n, p)),
        out_shape=jax.ShapeDtypeStruct((m, SWA_HEADS * SWA_HEAD_DIM), MXU_DTYPE),
        compiler_params=_params(("parallel", "parallel")),
    )(*([qkv] * 7), qw, kw, sink)


def _attn_bwd(do, qkv, qw, kw, sink, *, name):
    m = qkv.shape[0]
    g, b, hd = SWA_GROUP, SWA_BLOCK, SWA_HEAD_DIM

    def body(q_ref, k0, kp, kc, v0, vp, vc, qw_ref, kw_ref, s_ref, do_ref, dq_ref, dk_ref, dv_ref, dqw_ref, dkw_ref, ds_ref):
        n = pl.program_id(1)

        @pl.when(n == 0)
        def _():
            for r in (dk_ref, dv_ref, dqw_ref, dkw_ref, ds_ref):
                r[...] = jnp.zeros_like(r)

        fn = lambda *a: _attn_block(*a, n)
        _, vjp = jax.vjp(fn, q_ref[...], k0[...], kp[...], kc[...], v0[...], vp[...], vc[...], qw_ref[...], kw_ref[...], s_ref[...])
        dq, dk0, dkp, dkc, dv0, dvp, dvc, dqw, dkw, dsk = vjp(do_ref[...])
        dq_ref[...] = dq
        prev = pl.multiple_of(jnp.maximum(n - 1, 0) * b, b)
        cur = pl.multiple_of(n * b, b)
        for ref, parts in ((dk_ref, (dk0, dkp, dkc)), (dv_ref, (dv0, dvp, dvc))):
            ref[pl.ds(0, b), :] += parts[0]
            ref[pl.ds(prev, b), :] += parts[1]
            ref[pl.ds(cur, b), :] += parts[2]
        dqw_ref[...] += dqw
        dkw_ref[...] += dkw
        ds_ref[...] += dsk

    pairs = SWA_KV_HEADS // SWA_PAIR
    kv_acc = pl.BlockSpec((m, KV_LANES), lambda p, n: (0, p))
    w_acc = pl.BlockSpec((None, 1, hd), lambda p, n: (p, 0, 0))
    kv_shape = jax.ShapeDtypeStruct((m, SWA_KV_HEADS * hd), F32)
    return pl.pallas_call(
        body, name=name, grid=(pairs, m // b),
        in_specs=_attn_in_specs() + [pl.BlockSpec((b, Q_LANES), lambda p, n: (n, p))],
        out_specs=[pl.BlockSpec((b, Q_LANES), lambda p, n: (n, p)), kv_acc, kv_acc, w_acc, w_acc,
                   pl.BlockSpec((SWA_PAIR, g, 1, 1), lambda p, n: (p, 0, 0, 0))],
        out_shape=[jax.ShapeDtypeStruct((m, SWA_HEADS * hd), F32), kv_shape, kv_shape,
                   jax.ShapeDtypeStruct((pairs, 1, hd), F32), jax.ShapeDtypeStruct((pairs, 1, hd), F32),
                   jax.ShapeDtypeStruct(sink.shape, F32)],
        compiler_params=_params(("parallel", "arbitrary")),
    )(*([qkv] * 7), qw, kw, sink, do)


def _loss_bwd(h, target, *, name):
    m, d = h.shape
    b = SWA_BLOCK

    def body(h_ref, t_ref, l_ref, dh_ref):
        i = pl.program_id(0)

        @pl.when(i == 0)
        def _():
            l_ref[...] = jnp.zeros_like(l_ref)
            dh_ref[...] = jnp.zeros_like(dh_ref)

        @pl.when(i > 0)
        def _():
            e = h_ref[...] - t_ref[...]
            dh_ref[...] = e * (1.0 / d)
            l_ref[...] += jnp.sum(jnp.sum(e * e, 0, keepdims=True), 1, keepdims=True) * (0.5 / d)

    return pl.pallas_call(
        body, name=name, grid=(m // b,),
        in_specs=[pl.BlockSpec((b, d), lambda i: (i, 0)), pl.BlockSpec((b, d), lambda i: (jnp.maximum(i - 1, 0), 0))],
        out_specs=[pl.BlockSpec((1, LANES), lambda i: (0, 0)), pl.BlockSpec((b, d), lambda i: (i, 0))],
        out_shape=[jax.ShapeDtypeStruct((1, LANES), F32), jax.ShapeDtypeStruct((m, d), F32)],
        compiler_params=_params(("arbitrary",)),
    )(h, target)


def _ffn_fwd(h, nw, w_up_t, cw, w_down, tag):
    u, hn = _norm_matmul(h, nw, w_up_t, o_seg=D_FF, trans_w=True, name=f"ffn_up_{tag}")
    a = _ffn_act_fwd(u, cw, name=f"ffn_act_{tag}")
    return _mm_nn(a, w_down, res=h, name=f"ffn_down_{tag}"), (h, hn, u, a)


def _ffn_bwd(dh, saved, nw, w_up_t, cw, w_down, tag):
    h, hn, u, a = saved
    da = _mm_nn(dh, w_down, trans_w=True, name=f"ffn_da_{tag}")
    dw_down = _mm_tn(a, dh, name=f"ffn_dwdown_{tag}")
    du, dcw = _ffn_act_bwd(da, u, cw, name=f"ffn_act_bwd_{tag}")
    dhn = _mm_nn(du, w_up_t, a_seg=True, name=f"ffn_dhn_{tag}")
    dw_up_t = _mm_tn(du, hn, a_seg=True, name=f"ffn_dwup_{tag}")
    dh_in, dnw = _rmsnorm_bwd(dhn, h, nw, dh, name=f"ffn_norm_bwd_{tag}")
    return dh_in, dnw, dw_up_t, dcw, dw_down


def _local_step(x, target, w, fetch=None, push=None):
    fetch = fetch or (lambda stage, after: {})
    push = push or (lambda stage, grads: None)
    plus = lambda a, zero: a if zero is None else a + zero
    seq, d = x.shape
    m = PAD_ROWS + N_META + seq
    h0 = jnp.concatenate([jnp.zeros((PAD_ROWS, d), F32), w["meta"], x], 0)

    pm, hn0 = _norm_matmul(h0, w["anw"][0], w["w_in_t"], o_seg=SEG, trans_w=True, n=N_SEG * SEG, name="mix_in")
    pba = _mm_nn(hn0, w["w_in_tail_t"], trans_w=True, name="mix_in_tail")
    qkvc = _dnpre_fwd(pm, w["dcw"], name="dn_conv")
    ba = pba[:, :2 * DN_HEADS].T.reshape(2, DN_HEADS, m, 1).transpose(1, 0, 2, 3)
    y, states = _delta_fwd(qkvc, pm, ba, w["hp"], w["dnw"], name="delta")
    y = _shortconv_fwd(pm, w["caw"], y, name="shortconv")
    w = {**w, **fetch("l0", y)}
    h1 = _mm_nn(y, w["w_out"], res=h0, name="mix_out")
    h2, ffn0 = _ffn_fwd(h1, w["fnw"][0], w["w_up0_t"], w["fcw"][0], w["w_down0"], "l0")

    w = {**w, **fetch("l1", h2)}
    qkv, hn2 = _norm_matmul(h2, w["anw"][1], w["wqkv"], name="attn_qkv")
    o = _attn_fwd(qkv, w["qnw"], w["knw"], w["sink"], name="attn")
    h3 = _mm_nn(o, w["wo"], res=h2, name="attn_out")
    h4, ffn1 = _ffn_fwd(h3, w["fnw"][1], w["w_up1_t"], w["fcw"][1], w["w_down1"], "l1")

    loss, dh4 = _loss_bwd(h4, target, name="loss")

    g = {}
    dh3, dfnw1, dwup1, dfcw1, dwdown1 = _ffn_bwd(dh4, ffn1, w["fnw"][1], w["w_up1_t"], w["fcw"][1], w["w_down1"], "l1")

    do = _mm_nn(dh3, w["wo"], trans_w=True, name="attn_do")
    g["wo"] = _mm_tn(o, dh3, name="attn_dwo")
    dq, dk, dv, dqw, dkw, dsink = _attn_bwd(do, qkv, w["qnw"], w["knw"], w["sink"], name="attn_bwd")
    dqkv = jnp.concatenate([dq, dk, dv], 1).astype(MXU_DTYPE)
    dhn2 = _mm_nn(dqkv, w["wqkv"], trans_w=True, name="attn_dhn")
    g["wqkv"] = _mm_tn(hn2, dqkv, name="attn_dwqkv")
    zero = push("l1", dict(w_up_t=dwup1, w_down=dwdown1, wo=g["wo"], wqkv=g["wqkv"]))
    dh2, danw1 = _rmsnorm_bwd(dhn2, h2, plus(w["anw"][1], zero), dh3, name="attn_norm_bwd")

    dh1, dfnw0, dwup0, dfcw0, dwdown0 = _ffn_bwd(dh2, ffn0, w["fnw"][0], w["w_up0_t"], w["fcw"][0], w["w_down0"], "l0")

    dy = _mm_nn(dh1, w["w_out"], trans_w=True, name="mix_dy")
    g["w_out"] = _mm_tn(y, dh1, name="mix_dwout")
    zero = push("l0", dict(w_up_t=dwup0, w_down=dwdown0, w_out=g["w_out"]))
    dpm, dqkvc, dba, dhp, ddnw = _delta_bwd(dy, qkvc, pm, ba, w["hp"], plus(w["dnw"], zero), states, name="delta_bwd")
    dpm, ddcw = _dnpre_bwd(dqkvc, pm, w["dcw"], dpm, name="dn_conv_bwd")
    dpm, dcaw = _shortconv_bwd(dy, pm, w["caw"], dpm, name="shortconv_bwd")
    dpba = jnp.pad(dba.transpose(1, 0, 2, 3).reshape(2 * DN_HEADS, m).T, ((0, 0), (0, LANES - 2 * DN_HEADS))).astype(MXU_DTYPE)
    g["w_in_t"] = jnp.concatenate([_mm_tn(dpm, hn0, a_seg=True, out_dtype=F32, name="mix_dwin"),
                                   _mm_tn(dpba, hn0, out_dtype=F32, name="mix_dwin_tail")[:N_TAIL]], 0).astype(GRAD_WIRE_DTYPE)
    zero = push("in", dict(w_in_t=g["w_in_t"]))
    dhn0 = _mm_nn(dpba, plus(w["w_in_tail_t"], None if zero is None else zero.astype(MXU_DTYPE)), name="mix_dhn_tail")
    dhn0 = _mm_nn(dpm, w["w_in_t"], res=dhn0, a_seg=True, name="mix_dhn")
    dh0, danw0 = _rmsnorm_bwd(dhn0, h0, w["anw"][0], dh1, name="mix_norm_bwd")

    g.update(
        x=dh0[PAD_ROWS + N_META:], meta=dh0[PAD_ROWS:PAD_ROWS + N_META], anw=[danw0, danw1], fnw=[dfnw0, dfnw1],
        caw=dcaw, dcw=ddcw, hp=dhp, dnw=ddnw, qnw=jnp.sum(dqw, 0), knw=jnp.sum(dkw, 0), sink=dsink,
        w_up_t=[dwup0, dwup1], fcw=[dfcw0, dfcw1], w_down=[dwdown0, dwdown1])
    return loss, g


N_TAIL = 2 * DN_HEADS


def _prepare_early(p):
    n_main = N_SEG * SEG
    w_in_t = p["mix_w_in_t"]
    tail_t = jnp.pad(w_in_t[n_main:], ((0, LANES - N_TAIL), (0, 0)))
    hp = jnp.zeros((DN_HEADS, SUBLANES, LANES), F32)
    hp = hp.at[:, 0, :].set(p["dn_a_log"][0][:, None]).at[:, 1, :].set(p["dn_dt_bias"][0][:, None])
    depth = p["ffn_conv_w"].shape[0]
    return dict(
        meta=p["meta_tokens"], anw=[p["attn_norm_w"][i:i + 1] for i in range(depth)],
        fnw=[p["ffn_norm_w"][i:i + 1] for i in range(depth)],
        w_in_t=w_in_t, w_in_tail_t=tail_t,
        caw=_pad_w(p["conv_a_w"][0]), dcw=_pad_w(p["dn_conv_w"][0]), hp=hp, dnw=p["dn_norm_w"],
        qnw=p["swa_q_norm_w"], knw=p["swa_k_norm_w"], sink=p["swa_sinks"].reshape(SWA_KV_HEADS, SWA_GROUP, 1, 1),
        fcw=[_pad_w(p["ffn_conv_w"][i]) for i in range(depth)])


def _prepare_weights(p):
    return dict(
        _prepare_early(dict(p, mix_w_in_t=p["mix_w_in"][0].T)), w_out=p["mix_w_out"][0], wo=p["swa_wo"][0],
        wqkv=jnp.concatenate([p["swa_wq"][0], p["swa_wk"][0], p["swa_wv"][0]], 1),
        w_up0_t=p["ffn_w_up"][0].T, w_up1_t=p["ffn_w_up"][1].T, w_down0=p["ffn_w_down"][0], w_down1=p["ffn_w_down"][1])


def _small_named(g):
    return dict(
        meta_tokens=g["meta"], attn_norm_w=jnp.concatenate(g["anw"], 0), ffn_norm_w=jnp.concatenate(g["fnw"], 0),
        conv_a_w=g["caw"][None, :3], dn_conv_w=g["dcw"][None, :4],
        dn_a_log=g["hp"][None, :, 0, 0], dn_dt_bias=g["hp"][None, :, 1, 0], dn_norm_w=g["dnw"],
        swa_q_norm_w=g["qnw"], swa_k_norm_w=g["knw"], swa_sinks=g["sink"].reshape(1, SWA_HEADS),
        ffn_conv_w=jnp.stack([c[:3] for c in g["fcw"]]))


def _reference_named(g):
    nq, nkv = SWA_HEADS * SWA_HEAD_DIM, SWA_KV_HEADS * SWA_HEAD_DIM
    return dict(
        _small_named(g), mix_w_in=g["w_in_t"].T[None],
        mix_w_out=g["w_out"][None], swa_wq=g["wqkv"][None, :, :nq], swa_wk=g["wqkv"][None, :, nq:nq + nkv],
        swa_wv=g["wqkv"][None, :, nq + nkv:], swa_wo=g["wo"][None],
        ffn_w_up=jnp.stack([t.T for t in g["w_up_t"]]), ffn_w_down=jnp.stack(g["w_down"]))


def _my_index():
    return 4 * lax.axis_index("x") + 2 * lax.axis_index("y") + lax.axis_index("c")


def _all_gather(arrays, *, name):
    n = len(arrays)

    def body(*refs):
        ins, outs = refs[:n], refs[n:2 * n]
        send_sems, recv_sems, local_sems = refs[2 * n:]
        x, y, c = lax.axis_index("x"), lax.axis_index("y"), lax.axis_index("c")
        me, sibling = (x, y, c), (x, y, 1 - c)
        chips = [(1 - x, y), (x, 1 - y), (1 - x, 1 - y)]

        def copy(i, k, block, to, src=None):
            rows = outs[i].at[4 * block[0] + 2 * block[1] + block[2]]
            return pltpu.make_async_remote_copy(
                src_ref=rows if src is None else src, dst_ref=rows, send_sem=send_sems.at[i, k], recv_sem=recv_sems.at[i, k],
                device_id=to, device_id_type=pl.DeviceIdType.MESH)

        mine = [pltpu.make_async_copy(ins[i], outs[i].at[4 * x + 2 * y + c], local_sems.at[i]) for i in range(n)]
        first = []
        for j, chip in enumerate(chips):
            first += [copy(i, 1 + j, me, (*chip, c), src=ins[i]) for i in range(n)]
        first += [copy(i, 0, me, sibling, src=ins[i]) for i in range(n)]
        for cp in first + mine:
            cp.start()
        passed = []
        for j, chip in enumerate(chips):
            for i in range(n):
                copy(i, 1 + j, (*chip, c), me).wait_recv()
                fwd = copy(i, 4 + j, (*chip, c), sibling)
                fwd.start()
                passed.append(fwd)
        for i in range(n):
            copy(i, 0, sibling, me).wait_recv()
            for j, chip in enumerate(chips):
                copy(i, 4 + j, (*chip, 1 - c), me).wait_recv()
        for cp in first + passed:
            cp.wait_send()
        for cp in mine:
            cp.wait()

    hbm = pl.BlockSpec(memory_space=pl.ANY)
    return pl.pallas_call(
        body, name=name, in_specs=[hbm] * n, out_specs=[hbm] * n,
        out_shape=[jax.ShapeDtypeStruct((N_DEV,) + tuple(a.shape), a.dtype) for a in arrays],
        scratch_shapes=[pltpu.SemaphoreType.DMA((n, 7)), pltpu.SemaphoreType.DMA((n, 7)), pltpu.SemaphoreType.DMA((n,))],
    )(*arrays)


def _peer(d):
    px, py, pc = lax.axis_index("x") ^ (d >> 2), lax.axis_index("y") ^ ((d >> 1) & 1), lax.axis_index("c") ^ (d & 1)
    return (px, py, pc), 4 * px + 2 * py + pc


def _push_copies(mode, srcs, lands, send_sems, recv_sems):
    me = _my_index()
    out = []
    for d in range(1, N_DEV):
        pos, idx = _peer(d)
        for i in range(len(srcs)):
            out.append(pltpu.make_async_remote_copy(
                src_ref=srcs[i] if mode == "gather" else srcs[i].at[idx], dst_ref=lands[i].at[me],
                send_sem=send_sems.at[i * N_DEV + d], recv_sem=recv_sems.at[i * N_DEV + d], device_id=pos,
                device_id_type=pl.DeviceIdType.MESH))
    return out


_HBM = pl.BlockSpec(memory_space=pltpu.HBM)
_SEM = pl.BlockSpec(memory_space=pltpu.SEMAPHORE)


def _push_start(mode, arrays, follows, *, name):
    n = len(arrays)
    blocks = [a.shape if mode == "gather" else a.shape[1:] for a in arrays]
    lands = [lax.empty((N_DEV,) + tuple(b), a.dtype) for a, b in zip(arrays, blocks)]

    def body(*refs):
        srcs, land_refs = refs[:n], refs[n:2 * n]
        send_sems, recv_sems = refs[2 * n + 1], refs[2 * n + 2]
        zero = refs[-1]
        for cp in _push_copies(mode, srcs, land_refs, send_sems, recv_sems):
            cp.start()
        zero[...] = jnp.zeros_like(zero)

    hbm_in = [pltpu.with_memory_space_constraint(a, pltpu.HBM) for a in list(arrays) + lands]
    outs = pl.pallas_call(
        body, name=name,
        out_shape=[pltpu.SemaphoreType.DMA((n * N_DEV,)), pltpu.SemaphoreType.DMA((n * N_DEV,))]
        + [pltpu.HBM(a.shape, a.dtype) for a in hbm_in] + [jax.ShapeDtypeStruct((SUBLANES, LANES), F32)],
        in_specs=[_HBM] * (2 * n) + [pl.BlockSpec(memory_space=pl.ANY)],
        out_specs=[_SEM, _SEM] + [_HBM] * (2 * n) + [pl.BlockSpec(memory_space=pltpu.VMEM)],
        input_output_aliases={i: 2 + i for i in range(2 * n)},
        compiler_params=pltpu.CompilerParams(has_side_effects=pltpu.SideEffectType.DATAFLOW_SIDE_EFFECTING),
    )(*hbm_in, follows)
    return dict(mode=mode, sems=outs[:2], srcs=outs[2:2 + n], lands=outs[2 + n:2 + 2 * n], zero=outs[-1])


def _push_wait(push, follows, *, name):
    n = len(push["srcs"])
    mode = push["mode"]

    def body(*refs):
        srcs, land_refs = refs[:n], refs[n:2 * n]
        send_sems, recv_sems = refs[2 * n], refs[2 * n + 1]
        for cp in _push_copies(mode, srcs, land_refs, send_sems, recv_sems):
            cp.wait_send()
            cp.wait_recv()

    args = list(push["srcs"]) + list(push["lands"])
    outs = pl.pallas_call(
        body, name=name, out_shape=[pltpu.HBM(a.shape, a.dtype) for a in args],
        in_specs=[_HBM] * (2 * n) + [_SEM, _SEM, pl.BlockSpec(memory_space=pl.ANY)], out_specs=[_HBM] * (2 * n),
        input_output_aliases={i: i for i in range(2 * n)},
        compiler_params=pltpu.CompilerParams(has_side_effects=pltpu.SideEffectType.DATAFLOW_SIDE_EFFECTING),
    )(*args, *push["sems"], follows)
    me = _my_index()
    got = []
    for src, land in zip(outs[:n], outs[n:]):
        own = src if mode == "gather" else lax.dynamic_index_in_dim(src, me, 0, keepdims=False)
        got.append(lax.dynamic_update_index_in_dim(land, own, me, 0))
    return got


ADAMW_BLOCK_BYTES = 6 * 1024 * 1024


def _adamw_tile(r, c):
    fits = lambda tr, tc: N_DEV * tr * tc * 4 <= ADAMW_BLOCK_BYTES
    rows = [t for t in range(2 * SUBLANES, r + 1, 2 * SUBLANES) if r % t == 0 and fits(t, c)]
    if rows or fits(r, c):
        return (max(rows) if rows else r), c
    cols = [t for t in range(LANES, c + 1, LANES) if c % t == 0 and fits(r, t)]
    return r, max(cols)


def _adamw(parts, w, m, v, layer, outs=None, *, name):
    nl, r, c = w.shape
    tr, tc = _adamw_tile(r, c)

    def body(p_ref, w_ref, m_ref, v_ref, *rest):
        g_ref, d_ref, nm_ref, nv_ref = rest[-4:]
        g = p_ref[0].astype(F32)
        for j in range(1, N_DEV):
            g = g + p_ref[j].astype(F32)
        m2 = ADAM_B1 * m_ref[...] + (1.0 - ADAM_B1) * g
        v2 = ADAM_B2 * v_ref[...] + (1.0 - ADAM_B2) * jnp.square(g)
        m_hat = m2 / (1.0 - ADAM_B1 ** ADAM_STEP)
        v_hat = v2 / (1.0 - ADAM_B2 ** ADAM_STEP)
        g_ref[...] = g
        d_ref[...] = -ADAM_LR * (m_hat / (jnp.sqrt(v_hat) + ADAM_EPS) + ADAM_WD * w_ref[...])
        nm_ref[...] = m2
        nv_ref[...] = v2

    blk = pl.BlockSpec((None, tr, tc), lambda i, j: (layer, i, j))
    out = jax.ShapeDtypeStruct((nl, r, c), F32)
    given = list(outs) if outs is not None else []
    return pl.pallas_call(
        body, name=name, grid=(r // tr, c // tc),
        in_specs=[pl.BlockSpec((N_DEV, tr, tc), lambda i, j: (0, i, j)), blk, blk, blk] + [pl.BlockSpec(memory_space=pl.ANY)] * len(given),
        out_specs=[blk, blk, blk, blk], out_shape=[out, out, out, out],
        input_output_aliases={4 + t: t for t in range(len(given))},
        compiler_params=_params(("parallel", "parallel")),
    )(parts, w, m, v, *given)


SHARD_AXIS = dict(
    meta_tokens=1, attn_norm_w=None, ffn_norm_w=None, mix_w_in=2, conv_a_w=2, dn_conv_w=2, dn_a_log=None, dn_dt_bias=None,
    dn_norm_w=None, mix_w_out=1, swa_wq=1, swa_wk=1, swa_wv=1, swa_q_norm_w=None, swa_k_norm_w=None, swa_sinks=None,
    swa_wo=1, ffn_w_up=2, ffn_conv_w=2, ffn_w_down=1)
WEIGHTS = list(SHARD_AXIS)
BIG = ["mix_w_in", "mix_w_out", "swa_wq", "swa_wk", "swa_wv", "swa_wo", "ffn_w_up", "ffn_w_down"]
SMALL = [k for k in WEIGHTS if k not in BIG]
SMALL_SHARDED = [k for k in SMALL if SHARD_AXIS[k] is not None]


def _whole(g8, axis):
    t = jnp.moveaxis(g8, 0, axis)
    return t.reshape(t.shape[:axis] + (t.shape[axis] * t.shape[axis + 1],) + t.shape[axis + 2:])


def _by_owner(a, axis):
    s = a.shape[axis] // N_DEV
    return jnp.moveaxis(a.reshape(a.shape[:axis] + (N_DEV, s) + a.shape[axis + 1:]), axis, 0)


def _pack(arrays, lead=0):
    flat = jnp.concatenate([a.reshape(a.shape[:lead] + (-1,)) for a in arrays], -1)
    n = flat.shape[-1]
    rows = -(-n // (SUBLANES * LANES)) * SUBLANES
    flat = jnp.pad(flat, [(0, 0)] * lead + [(0, rows * LANES - n)])
    return flat.reshape(flat.shape[:lead] + (rows, LANES))


def _unpack(buf, shapes, lead=0):
    flat = buf.reshape(buf.shape[:lead] + (-1,))
    out, o = [], 0
    for s in shapes:
        n = 1
        for e in s:
            n *= e
        out.append(flat[..., o:o + n].reshape(buf.shape[:lead] + tuple(s)))
        o += n
    return out


def kernel(x, meta_tokens, attn_norm_w, ffn_norm_w, mix_w_in, conv_a_w, dn_conv_w, dn_a_log, dn_dt_bias, dn_norm_w, mix_w_out, swa_wq, swa_wk, swa_wv, swa_q_norm_w, swa_k_norm_w, swa_sinks, swa_wo, ffn_w_up, ffn_conv_w, ffn_w_down, loss_target, m_meta_tokens, m_attn_norm_w, m_ffn_norm_w, m_mix_w_in, m_conv_a_w, m_dn_conv_w, m_dn_a_log, m_dn_dt_bias, m_dn_norm_w, m_mix_w_out, m_swa_wq, m_swa_wk, m_swa_wv, m_swa_q_norm_w, m_swa_k_norm_w, m_swa_sinks, m_swa_wo, m_ffn_w_up, m_ffn_conv_w, m_ffn_w_down, v_meta_tokens, v_attn_norm_w, v_ffn_norm_w, v_mix_w_in, v_conv_a_w, v_dn_conv_w, v_dn_a_log, v_dn_dt_bias, v_dn_norm_w, v_mix_w_out, v_swa_wq, v_swa_wk, v_swa_wv, v_swa_q_norm_w, v_swa_k_norm_w, v_swa_sinks, v_swa_wo, v_ffn_w_up, v_ffn_conv_w, v_ffn_w_down):
    w = dict(meta_tokens=meta_tokens, attn_norm_w=attn_norm_w, ffn_norm_w=ffn_norm_w, mix_w_in=mix_w_in, conv_a_w=conv_a_w, dn_conv_w=dn_conv_w, dn_a_log=dn_a_log, dn_dt_bias=dn_dt_bias, dn_norm_w=dn_norm_w, mix_w_out=mix_w_out, swa_wq=swa_wq, swa_wk=swa_wk, swa_wv=swa_wv, swa_q_norm_w=swa_q_norm_w, swa_k_norm_w=swa_k_norm_w, swa_sinks=swa_sinks, swa_wo=swa_wo, ffn_w_up=ffn_w_up, ffn_conv_w=ffn_conv_w, ffn_w_down=ffn_w_down)
    mom = dict(meta_tokens=m_meta_tokens, attn_norm_w=m_attn_norm_w, ffn_norm_w=m_ffn_norm_w, mix_w_in=m_mix_w_in, conv_a_w=m_conv_a_w, dn_conv_w=m_dn_conv_w, dn_a_log=m_dn_a_log, dn_dt_bias=m_dn_dt_bias, dn_norm_w=m_dn_norm_w, mix_w_out=m_mix_w_out, swa_wq=m_swa_wq, swa_wk=m_swa_wk, swa_wv=m_swa_wv, swa_q_norm_w=m_swa_q_norm_w, swa_k_norm_w=m_swa_k_norm_w, swa_sinks=m_swa_sinks, swa_wo=m_swa_wo, ffn_w_up=m_ffn_w_up, ffn_conv_w=m_ffn_conv_w, ffn_w_down=m_ffn_w_down)
    var = dict(meta_tokens=v_meta_tokens, attn_norm_w=v_attn_norm_w, ffn_norm_w=v_ffn_norm_w, mix_w_in=v_mix_w_in, conv_a_w=v_conv_a_w, dn_conv_w=v_dn_conv_w, dn_a_log=v_dn_a_log, dn_dt_bias=v_dn_dt_bias, dn_norm_w=v_dn_norm_w, mix_w_out=v_mix_w_out, swa_wq=v_swa_wq, swa_wk=v_swa_wk, swa_wv=v_swa_wv, swa_q_norm_w=v_swa_q_norm_w, swa_k_norm_w=v_swa_k_norm_w, swa_sinks=v_swa_sinks, swa_wo=v_swa_wo, ffn_w_up=v_ffn_w_up, ffn_conv_w=v_ffn_conv_w, ffn_w_down=v_ffn_w_down)
    me = _my_index()

    transposed = ("mix_w_in", "ffn_w_up")
    view = lambda k, a: jnp.swapaxes(a, 1, 2) if k in transposed else a
    axis2d = {k: 0 if k in transposed else SHARD_AXIS[k] - 1 for k in BIG}
    shard16 = {k: view(k, w[k]).astype(MXU_DTYPE) for k in BIG}
    small_shard_shapes = [w[k].shape for k in SMALL_SHARDED]
    rows_in = shard16["mix_w_in"].shape[1]
    sent_in = jnp.pad(shard16["mix_w_in"][0], ((0, -rows_in % (2 * SUBLANES)), (0, 0)))
    got = _all_gather([sent_in, _pack([w[k] for k in SMALL_SHARDED])], name="gather_weights")
    whole = {"mix_w_in_t": _whole(got[0][:, :rows_in], 0)}
    for k, a in zip(SMALL_SHARDED, _unpack(got[1], small_shard_shapes, lead=1)):
        whole[k] = _whole(a, SHARD_AXIS[k])
    for k in SMALL:
        whole.setdefault(k, w[k])
    stages = {"in": [("mix_w_in", 0)], "l0": [("mix_w_out", 0), ("ffn_w_up", 0), ("ffn_w_down", 0)],
              "l1": [("swa_wq", 0), ("swa_wk", 0), ("swa_wv", 0), ("swa_wo", 0), ("ffn_w_up", 1), ("ffn_w_down", 1)]}
    pushed = {}
    follows = got[1]
    for stage in ("l0", "l1"):
        pushed[stage] = _push_start("gather", [shard16[k][l] for k, l in stages[stage]], follows, name=f"push_weights_{stage}")
        follows = pushed[stage]["zero"]
    early = _prepare_early(whole)
    early["anw"][0] = early["anw"][0] + follows[0, 0]

    def fetch(stage, after):
        got = _push_wait(pushed[stage], after, name=f"wait_weights_{stage}")
        full = {kl: _whole(a, axis2d[kl[0]]) for kl, a in zip(stages[stage], got)}
        if stage == "l0":
            return dict(w_out=full["mix_w_out", 0], w_up0_t=full["ffn_w_up", 0], w_down0=full["ffn_w_down", 0])
        wqkv = jnp.concatenate([full["swa_wq", 0], full["swa_wk", 0], full["swa_wv", 0]], 1)
        return dict(wqkv=wqkv, wo=full["swa_wo", 0], w_up1_t=full["ffn_w_up", 1], w_down1=full["ffn_w_down", 1])

    nq, nkv = SWA_HEADS * SWA_HEAD_DIM, SWA_KV_HEADS * SWA_HEAD_DIM
    grad_pushes = {}

    def push(stage, gd):
        if stage == "in":
            named = {("mix_w_in", 0): gd["w_in_t"]}
        elif stage == "l1":
            named = {("swa_wq", 0): gd["wqkv"][:, :nq], ("swa_wk", 0): gd["wqkv"][:, nq:nq + nkv],
                     ("swa_wv", 0): gd["wqkv"][:, nq + nkv:], ("swa_wo", 0): gd["wo"],
                     ("ffn_w_up", 1): gd["w_up_t"], ("ffn_w_down", 1): gd["w_down"]}
        else:
            named = {("mix_w_out", 0): gd["w_out"], ("ffn_w_up", 0): gd["w_up_t"], ("ffn_w_down", 0): gd["w_down"]}
        sent = [_by_owner(named[kl], axis2d[kl[0]]) for kl in stages[stage]]
        grad_pushes[stage] = _push_start("scatter", sent, jnp.zeros((SUBLANES, LANES), F32), name=f"push_grads_{stage}")
        return grad_pushes[stage]["zero"][0, 0]

    loss, g = _local_step(x[0], loss_target[0], early, fetch, push)
    grads = _small_named(g)

    results = {}

    def update(stage, follows):
        got = _push_wait(grad_pushes[stage], follows, name=f"wait_grads_{stage}")
        for (k, l), parts in zip(stages[stage], got):
            w3, m3, v3 = view(k, w[k]), view(k, mom[k]), view(k, var[k])
            results[k] = _adamw(parts.reshape((N_DEV,) + w3.shape[1:]), w3, m3, v3, l, results.get(k), name=f"adamw_{k}_{l}")
        return results[stages[stage][0][0]][0]

    follows = update("l0", update("l1", g["meta"]))

    small_shapes = [grads[k].shape for k in SMALL]
    (all_small,) = _all_gather([_pack([loss] + [grads[k].astype(F32) for k in SMALL])], name="gather_small_grads")
    loss_parts, *small_parts = _unpack(all_small, [loss.shape] + small_shapes, lead=1)
    mine = []
    for k, p in zip(SMALL, small_parts):
        ax = SHARD_AXIS[k]
        mine.append(p if ax is None else lax.dynamic_slice_in_dim(p, me * w[k].shape[ax], w[k].shape[ax], 1 + ax))
    zero = jnp.zeros(loss.shape, F32)
    packed = [_pack([z] + [d[k] for k in SMALL]) for z, d in ((zero, w), (zero, mom), (zero, var))]
    res = _adamw(_pack([loss_parts] + mine, lead=1), *[t[None] for t in packed], 0, name="adamw_small")
    shapes = [loss.shape] + [w[k].shape for k in SMALL]
    for t, which in zip(res, range(4)):
        for k, a in zip(["loss"] + SMALL, _unpack(t[0], shapes)):
            results.setdefault(k, [None] * 4)[which] = a
    update("in", jnp.maximum(follows[0, :1, :1], res[0][0, :1, :1]))

    outs = [[view(k, results[k][which]) for k in WEIGHTS] for which in range(4)]
    return (results["loss"][0][0, 0], g["x"][None], *outs[0], *outs[1], *outs[2], *outs[3])
```

```python
import functools

import jax
import jax.numpy as jnp
from jax import lax
from jax.experimental import pallas as pl
from jax.experimental.pallas import tpu as pltpu

F32 = jnp.float32
BF16 = jnp.bfloat16
MXU_DTYPE = BF16
GRAD_WIRE_DTYPE = BF16

D_MODEL = 1024
N_META = 16
PAD_ROWS = 112
D_CONV = 512
DN_HEADS = 4
DN_HEAD_DIM = 128
DN_DIM = DN_HEADS * DN_HEAD_DIM
DN_CHUNK = 64
SEG = 512
N_SEG = 7
SWA_HEADS = 16
SWA_KV_HEADS = 4
SWA_GROUP = SWA_HEADS // SWA_KV_HEADS
SWA_HEAD_DIM = 64
SWA_BLOCK = 128
D_FF = 2816
EPS = 1e-6
NEG = -1e30
N_DEV = 8

ADAM_LR = 0.001
ADAM_B1 = 0.9
ADAM_B2 = 0.999
ADAM_EPS = 1e-08
ADAM_WD = 0.01
ADAM_STEP = 10

VMEM_LIMIT_BYTES = 52 * 1024 * 1024
SUBLANES = 8
LANES = 128


def _pick(n, prefs):
    for p in prefs:
        if n % p == 0:
            return p
    return n


def _params(sem, vmem=VMEM_LIMIT_BYTES):
    return pltpu.CompilerParams(dimension_semantics=sem, vmem_limit_bytes=vmem)


def _rms(x, w):
    return x * lax.rsqrt(jnp.mean(x * x, -1, keepdims=True) + EPS) * w


def _norm_matmul(h, nw, w, *, o_seg=None, trans_w=False, n=None, name):
    m, k = h.shape
    n = n or (w.shape[0] if trans_w else w.shape[1])
    tm = _pick(m, (1408, 384, 128))
    tn = _pick(o_seg or n, (1408, 1024, 512, 256, 128))
    dims = _DOT_DIMS["nt" if trans_w else "nn"]

    def body(h_ref, nw_ref, w_ref, o_ref, hn_ref, hn_s):
        @pl.when(pl.program_id(1) == 0)
        def _():
            hn = _rms(h_ref[...], nw_ref[...]).astype(MXU_DTYPE)
            hn_s[...] = hn
            hn_ref[...] = hn

        o_ref[...] = lax.dot_general(hn_s[...], w_ref[...], dims, preferred_element_type=F32)

    if o_seg:
        per = o_seg // tn
        o_shape = jax.ShapeDtypeStruct((n // o_seg, m, o_seg), F32)
        o_spec = pl.BlockSpec((None, tm, tn), lambda i, j: (j // per, i, j % per))
    else:
        o_shape = jax.ShapeDtypeStruct((m, n), F32)
        o_spec = pl.BlockSpec((tm, tn), lambda i, j: (i, j))
    return pl.pallas_call(
        body, name=name, grid=(m // tm, n // tn),
        in_specs=[pl.BlockSpec((tm, k), lambda i, j: (i, 0)), pl.BlockSpec((1, k), lambda i, j: (0, 0)),
                  pl.BlockSpec((tn, k), lambda i, j: (j, 0)) if trans_w else pl.BlockSpec((k, tn), lambda i, j: (0, j))],
        out_specs=[o_spec, pl.BlockSpec((tm, k), lambda i, j: (i, 0))],
        out_shape=[o_shape, jax.ShapeDtypeStruct((m, k), MXU_DTYPE)],
        scratch_shapes=[pltpu.VMEM((tm, k), MXU_DTYPE)],
        compiler_params=_params(("parallel", "arbitrary")),
    )(h, nw, w)


TILE_BUDGET_BYTES = 38 * 1024 * 1024
TILE_SIZES = (4224, 2816, 1792, 1536, 1408, 1024, 512, 256, 128)


def _divisor_tiles(n):
    return [t for t in TILE_SIZES if n % t == 0] or [n]


def _mm_nn(a, w, *, res=None, a_seg=False, trans_w=False, w_seg=False, out_dtype=F32, name):
    if a_seg:
        s, m, seg = a.shape
    else:
        m, seg = a.shape
        s = 1
    k = s * seg
    n = w.shape[0] * w.shape[2] if w_seg else (w.shape[0] if trans_w else w.shape[1])
    n_seg = w.shape[2] if w_seg else n
    tm = _pick(m, (1408, 1024, 512, 384, 256, 128))
    ab = a.dtype.itemsize
    k_steps = [(sb, seg) for sb in range(s, 0, -1) if s % sb == 0] if a_seg else [(1, t) for t in _divisor_tiles(seg)]
    best = None
    for tn in _divisor_tiles(n_seg):
        for sb, tk1 in k_steps:
            tk = sb * tk1
            nk = k // tk
            need = 2 * tm * tk * ab + 2 * tk * tn * 2 + 2 * tm * tn * 4 + (tm * tn * 4 if nk > 1 else 0) + (2 * tm * tn * 4 if res is not None else 0)
            if need <= TILE_BUDGET_BYTES and (best is None or tk * tn > best[0]):
                best = (tk * tn, tn, sb, tk1)
    _, tn, sb, tk1 = best
    tk = sb * tk1
    nk = k // tk
    w_dims = _DOT_DIMS["nt" if trans_w else "nn"]

    def body(*refs):
        a_ref, w_ref = refs[:2]
        r_ref = refs[2] if res is not None else None
        o_ref = refs[3 if res is not None else 2]

        def partial_product():
            if not a_seg:
                return lax.dot_general(a_ref[...].astype(MXU_DTYPE), w_ref[...], w_dims, preferred_element_type=F32)
            out = None
            for t in range(sb):
                wt = w_ref[:, t * seg:(t + 1) * seg] if trans_w else w_ref[t * seg:(t + 1) * seg, :]
                d = lax.dot_general(a_ref[t].astype(MXU_DTYPE), wt, w_dims, preferred_element_type=F32)
                out = d if out is None else out + d
            return out

        if nk == 1:
            o_ref[...] = (partial_product() if res is None else partial_product() + r_ref[...]).astype(o_ref.dtype)
            return
        acc = refs[-1]
        kk = pl.program_id(2)

        @pl.when(kk == 0)
        def _():
            acc[...] = jnp.zeros_like(acc)

        acc[...] += partial_product()

        @pl.when(kk == nk - 1)
        def _():
            o_ref[...] = (acc[...] if res is None else acc[...] + r_ref[...]).astype(o_ref.dtype)

    a_spec = pl.BlockSpec((sb, tm, seg), lambda i, j, kk: (kk, i, 0)) if a_seg else pl.BlockSpec((tm, tk), lambda i, j, kk: (i, kk))
    if w_seg:
        per = n_seg // tn
        w_spec = pl.BlockSpec((None, tk, tn), lambda i, j, kk: (j // per, kk, j % per))
    elif trans_w:
        w_spec = pl.BlockSpec((tn, tk), lambda i, j, kk: (j, kk))
    else:
        w_spec = pl.BlockSpec((tk, tn), lambda i, j, kk: (kk, j))
    in_specs = [a_spec, w_spec]
    args = [a, w]
    if res is not None:
        in_specs.append(pl.BlockSpec((tm, tn), lambda i, j, kk: (i, j)))
        args.append(res)
    return pl.pallas_call(
        body, name=name, grid=(m // tm, n // tn, nk), in_specs=in_specs,
        out_specs=pl.BlockSpec((tm, tn), lambda i, j, kk: (i, j)),
        out_shape=jax.ShapeDtypeStruct((m, n), out_dtype),
        scratch_shapes=[pltpu.VMEM((tm, tn), F32)] if nk > 1 else [],
        compiler_params=_params(("parallel", "parallel", "arbitrary")),
    )(*args)


def _mm_tn(a, b, *, a_seg=False, b_seg=False, out_dtype=None, name):
    out_dtype = out_dtype or GRAD_WIRE_DTYPE
    if a_seg:
        sa, m, a_unit = a.shape
        ka = sa * a_unit
    else:
        m, ka = a.shape
        a_unit = ka
    if b_seg:
        s, _, seg = b.shape
        n = s * seg
    else:
        n = b.shape[1]
        seg = n
    tmc = _pick(m, (1408, 384, 128))
    best = None
    for tka in _divisor_tiles(a_unit):
        for tn in _divisor_tiles(seg):
            need = 2 * tmc * tka * a.dtype.itemsize + 2 * tmc * tn * b.dtype.itemsize + tka * tn * 4 + 2 * tka * tn * 4
            if need <= TILE_BUDGET_BYTES and (best is None or (tka * tn, tn) > best[:2]):
                best = (tka * tn, tn, tka)
    _, tn, tka = best
    nm = m // tmc

    def body(a_ref, b_ref, o_ref, acc):
        mm = pl.program_id(2)

        @pl.when(mm == 0)
        def _():
            acc[...] = jnp.zeros_like(acc)

        acc[...] += lax.dot_general(a_ref[...].astype(MXU_DTYPE), b_ref[...].astype(MXU_DTYPE),
                                    (((0,), (0,)), ((), ())), preferred_element_type=F32)

        @pl.when(mm == nm - 1)
        def _():
            o_ref[...] = acc[...].astype(o_ref.dtype)

    if b_seg:
        per = seg // tn
        b_spec = pl.BlockSpec((None, tmc, tn), lambda i, j, mm: (j // per, mm, j % per))
    else:
        b_spec = pl.BlockSpec((tmc, tn), lambda i, j, mm: (mm, j))
    if a_seg:
        a_per = a_unit // tka
        a_spec = pl.BlockSpec((None, tmc, tka), lambda i, j, mm: (i // a_per, mm, i % a_per))
    else:
        a_spec = pl.BlockSpec((tmc, tka), lambda i, j, mm: (mm, i))
    return pl.pallas_call(
        body, name=name, grid=(ka // tka, n // tn, nm),
        in_specs=[a_spec, b_spec],
        out_specs=pl.BlockSpec((tka, tn), lambda i, j, mm: (i, j)),
        out_shape=jax.ShapeDtypeStruct((ka, n), out_dtype),
        scratch_shapes=[pltpu.VMEM((tka, tn), F32)],
        compiler_params=_params(("parallel", "parallel", "arbitrary")),
    )(a, b)


def _rmsnorm_bwd(dhn, h, nw, dres, *, name):
    m, d = h.shape
    tm = _pick(m, (384, 128))

    def body(dhn_ref, h_ref, nw_ref, dres_ref, dh_ref, dnw_ref):
        i = pl.program_id(0)
        x = h_ref[...]
        r = lax.rsqrt(jnp.mean(x * x, -1, keepdims=True) + EPS)
        xh = x * r
        dy = dhn_ref[...]
        dxh = dy * nw_ref[...]
        dx = r * (dxh - xh * jnp.mean(dxh * xh, -1, keepdims=True))
        row = i * tm + lax.broadcasted_iota(jnp.int32, (tm, 1), 0)
        dh_ref[...] = jnp.where(row >= PAD_ROWS, dres_ref[...] + dx, 0.0)

        @pl.when(i == 0)
        def _():
            dnw_ref[...] = jnp.zeros_like(dnw_ref)

        dnw_ref[...] += jnp.sum(dy * xh, 0, keepdims=True)

    return pl.pallas_call(
        body, name=name, grid=(m // tm,),
        in_specs=[pl.BlockSpec((tm, d), lambda i: (i, 0)), pl.BlockSpec((tm, d), lambda i: (i, 0)),
                  pl.BlockSpec((1, d), lambda i: (0, 0)), pl.BlockSpec((tm, d), lambda i: (i, 0))],
        out_specs=[pl.BlockSpec((tm, d), lambda i: (i, 0)), pl.BlockSpec((1, d), lambda i: (0, 0))],
        out_shape=[jax.ShapeDtypeStruct((m, d), F32), jax.ShapeDtypeStruct((1, d), F32)],
        compiler_params=_params(("arbitrary",)),
    )(dhn, h, nw, dres)


ROW_CHUNK = 248


def _row_chunks(m):
    out, s = [], SUBLANES
    while s < m:
        n = min(ROW_CHUNK, m - s)
        out.append((s, n))
        s += n
    return out


def _conv_at(load, w, width, s, n):
    acc = w[width - 1:width, :] * load(s, n)
    for j in range(width - 1):
        acc = acc + w[j:j + 1, :] * load(s - (width - 1 - j), n)
    return acc


def _conv_t_at(load, w, width, s, n):
    acc = w[width - 1:width, :] * load(s, n)
    for j in range(width - 1):
        acc = acc + w[j:j + 1, :] * load(s + (width - 1 - j), n)
    return acc


def _dconv_w(load_x, d, width, s, n):
    rows = [jnp.sum(d * load_x(s - (width - 1 - j), n), 0, keepdims=True) for j in range(width)]
    rows.append(jnp.zeros((SUBLANES - width, d.shape[1]), F32))
    return jnp.concatenate(rows, 0)


def _pad_w(w):
    return jnp.concatenate([w, jnp.zeros((SUBLANES - w.shape[0], w.shape[1]), w.dtype)], 0)


def _sigmoid(x):
    return 1.0 / (1.0 + jnp.exp(-x))


def _ffn_act_fwd(u, cw, *, name):
    _, m, f = u.shape
    cb = _pick(f, (256, 128))
    chunks = _row_chunks(m)

    def body(g_ref, v_ref, w_ref, o_ref):
        w = w_ref[...]
        o_ref[pl.ds(0, SUBLANES), :] = jnp.zeros((SUBLANES, cb), o_ref.dtype)
        for s, n in chunks:
            c = _conv_at(lambda a, b: g_ref[pl.ds(a, b), :], w, 3, s, n)
            o_ref[pl.ds(s, n), :] = (c * _sigmoid(c) * v_ref[pl.ds(s, n), :]).astype(o_ref.dtype)

    return pl.pallas_call(
        body, name=name, grid=(f // cb,),
        in_specs=[pl.BlockSpec((None, m, cb), lambda j: (0, 0, j)), pl.BlockSpec((None, m, cb), lambda j: (1, 0, j)),
                  pl.BlockSpec((SUBLANES, cb), lambda j: (0, j))],
        out_specs=pl.BlockSpec((m, cb), lambda j: (0, j)),
        out_shape=jax.ShapeDtypeStruct((m, f), MXU_DTYPE),
        compiler_params=_params(("parallel",)),
    )(u, u, cw)


def _ffn_act_bwd(da, u, cw, *, name):
    _, m, f = u.shape
    cb = LANES
    chunks = _row_chunks(m)

    def body(da_ref, g_ref, v_ref, w_ref, du_ref, dw_ref, dg_s):
        w = w_ref[...]
        zeros8 = jnp.zeros((SUBLANES, cb), F32)
        dg_s[pl.ds(0, SUBLANES), :] = zeros8
        dg_s[pl.ds(m, SUBLANES), :] = zeros8
        du_ref[0, pl.ds(0, SUBLANES), :] = zeros8.astype(du_ref.dtype)
        du_ref[1, pl.ds(0, SUBLANES), :] = zeros8.astype(du_ref.dtype)
        load_g = lambda a, b: g_ref[pl.ds(a, b), :]
        dw = jnp.zeros((SUBLANES, cb), F32)
        for s, n in chunks:
            c = _conv_at(load_g, w, 3, s, n)
            sg = _sigmoid(c)
            d = da_ref[pl.ds(s, n), :]
            du_ref[1, pl.ds(s, n), :] = (d * (c * sg)).astype(du_ref.dtype)
            dc = d * v_ref[pl.ds(s, n), :] * (sg * (1.0 + c * (1.0 - sg)))
            dg_s[pl.ds(s, n), :] = dc
            dw = dw + _dconv_w(load_g, dc, 3, s, n)
        dw_ref[...] = dw
        for s, n in chunks:
            du_ref[0, pl.ds(s, n), :] = _conv_t_at(lambda a, b: dg_s[pl.ds(a, b), :], w, 3, s, n).astype(du_ref.dtype)

    return pl.pallas_call(
        body, name=name, grid=(f // cb,),
        in_specs=[pl.BlockSpec((m, cb), lambda j: (0, j)), pl.BlockSpec((None, m, cb), lambda j: (0, 0, j)),
                  pl.BlockSpec((None, m, cb), lambda j: (1, 0, j)), pl.BlockSpec((SUBLANES, cb), lambda j: (0, j))],
        out_specs=[pl.BlockSpec((2, m, cb), lambda j: (0, 0, j)), pl.BlockSpec((SUBLANES, cb), lambda j: (0, j))],
        out_shape=[jax.ShapeDtypeStruct((2, m, f), MXU_DTYPE), jax.ShapeDtypeStruct((SUBLANES, f), F32)],
        scratch_shapes=[pltpu.VMEM((m + SUBLANES, cb), F32)],
        compiler_params=_params(("parallel",)),
    )(da, u, u, cw)


def _shortconv_fwd(pm, cw, y, *, name):
    _, m, seg = pm.shape
    cb = _pick(seg, (256, 128))
    chunks = _row_chunks(m)

    def body(gi_ref, go_ref, ah_ref, w_ref, y_in, o_ref):
        del y_in
        w = w_ref[...]
        o_ref[pl.ds(0, SUBLANES), :] = jnp.zeros((SUBLANES, cb), o_ref.dtype)
        load_m = lambda a, b: gi_ref[pl.ds(a, b), :] * ah_ref[pl.ds(a, b), :]
        for s, n in chunks:
            o_ref[pl.ds(s, n), :] = (go_ref[pl.ds(s, n), :] * _conv_at(load_m, w, 3, s, n)).astype(o_ref.dtype)

    return pl.pallas_call(
        body, name=name, grid=(seg // cb,),
        in_specs=[pl.BlockSpec((None, m, cb), lambda j: (0, 0, j)), pl.BlockSpec((None, m, cb), lambda j: (1, 0, j)),
                  pl.BlockSpec((None, m, cb), lambda j: (2, 0, j)), pl.BlockSpec((SUBLANES, cb), lambda j: (0, j)),
                  pl.BlockSpec(memory_space=pl.ANY)],
        out_specs=pl.BlockSpec((m, cb), lambda j: (0, j)),
        out_shape=jax.ShapeDtypeStruct(y.shape, y.dtype),
        input_output_aliases={4: 0},
        compiler_params=_params(("parallel",)),
    )(pm, pm, pm, cw, y)


def _shortconv_bwd(dy, pm, cw, dpm, *, name):
    _, m, seg = pm.shape
    cb = LANES
    chunks = _row_chunks(m)

    def body(dy_ref, gi_ref, go_ref, ah_ref, w_ref, dpm_in, dp_ref, dw_ref, dc_s):
        del dpm_in
        w = w_ref[...]
        zeros8 = jnp.zeros((SUBLANES, cb), F32)
        dc_s[pl.ds(0, SUBLANES), :] = zeros8
        dc_s[pl.ds(m, SUBLANES), :] = zeros8
        for t in range(3):
            dp_ref[t, pl.ds(0, SUBLANES), :] = zeros8.astype(dp_ref.dtype)
        load_m = lambda a, b: gi_ref[pl.ds(a, b), :] * ah_ref[pl.ds(a, b), :]
        dw = jnp.zeros((SUBLANES, cb), F32)
        for s, n in chunks:
            d = dy_ref[pl.ds(s, n), :]
            dp_ref[1, pl.ds(s, n), :] = (d * _conv_at(load_m, w, 3, s, n)).astype(dp_ref.dtype)
            dc = d * go_ref[pl.ds(s, n), :]
            dc_s[pl.ds(s, n), :] = dc
            dw = dw + _dconv_w(load_m, dc, 3, s, n)
        dw_ref[...] = dw
        for s, n in chunks:
            dm = _conv_t_at(lambda a, b: dc_s[pl.ds(a, b), :], w, 3, s, n)
            dp_ref[0, pl.ds(s, n), :] = (dm * ah_ref[pl.ds(s, n), :]).astype(dp_ref.dtype)
            dp_ref[2, pl.ds(s, n), :] = (dm * gi_ref[pl.ds(s, n), :]).astype(dp_ref.dtype)

    return pl.pallas_call(
        body, name=name, grid=(seg // cb,),
        in_specs=[pl.BlockSpec((m, cb), lambda j: (0, j)), pl.BlockSpec((None, m, cb), lambda j: (0, 0, j)),
                  pl.BlockSpec((None, m, cb), lambda j: (1, 0, j)), pl.BlockSpec((None, m, cb), lambda j: (2, 0, j)),
                  pl.BlockSpec((SUBLANES, cb), lambda j: (0, j)), pl.BlockSpec(memory_space=pl.ANY)],
        out_specs=[pl.BlockSpec((3, m, cb), lambda j: (0, 0, j)), pl.BlockSpec((SUBLANES, cb), lambda j: (0, j))],
        out_shape=[jax.ShapeDtypeStruct(dpm.shape, dpm.dtype), jax.ShapeDtypeStruct((SUBLANES, seg), F32)],
        scratch_shapes=[pltpu.VMEM((m + SUBLANES, cb), F32)],
        input_output_aliases={5: 0},
        compiler_params=_params(("parallel",)),
    )(dy, pm, pm, pm, cw, dpm)


def _dnpre_fwd(pm, cw, *, name):
    _, m, seg = pm.shape
    cb = _pick(seg, (256, 128))
    per = seg // cb
    chunks = _row_chunks(m)

    def body(x_ref, w_ref, o_ref):
        w = w_ref[...]
        o_ref[pl.ds(0, SUBLANES), :] = jnp.zeros((SUBLANES, cb), F32)
        for s, n in chunks:
            c = _conv_at(lambda a, b: x_ref[pl.ds(a, b), :], w, 4, s, n)
            o_ref[pl.ds(s, n), :] = c * _sigmoid(c)

    return pl.pallas_call(
        body, name=name, grid=(3 * per,),
        in_specs=[pl.BlockSpec((None, m, cb), lambda j: (3 + j // per, 0, j % per)), pl.BlockSpec((SUBLANES, cb), lambda j: (0, j))],
        out_specs=pl.BlockSpec((None, m, cb), lambda j: (j // per, 0, j % per)),
        out_shape=jax.ShapeDtypeStruct((3, m, seg), F32),
        compiler_params=_params(("parallel",)),
    )(pm, cw)


def _dnpre_bwd(dqkv, pm, cw, dpm, *, name):
    _, m, seg = pm.shape
    cb = _pick(seg, (256, 128))
    per = seg // cb
    chunks = _row_chunks(m)

    def body(d_ref, x_ref, w_ref, dpm_in, dp_ref, dw_ref, dc_s):
        del dpm_in
        w = w_ref[...]
        zeros8 = jnp.zeros((SUBLANES, cb), F32)
        dc_s[pl.ds(0, SUBLANES), :] = zeros8
        dc_s[pl.ds(m, SUBLANES), :] = zeros8
        dp_ref[pl.ds(0, SUBLANES), :] = zeros8.astype(dp_ref.dtype)
        load_x = lambda a, b: x_ref[pl.ds(a, b), :]
        dw = jnp.zeros((SUBLANES, cb), F32)
        for s, n in chunks:
            c = _conv_at(load_x, w, 4, s, n)
            sg = _sigmoid(c)
            dc = d_ref[pl.ds(s, n), :] * (sg * (1.0 + c * (1.0 - sg)))
            dc_s[pl.ds(s, n), :] = dc
            dw = dw + _dconv_w(load_x, dc, 4, s, n)
        dw_ref[...] = dw
        for s, n in chunks:
            dp_ref[pl.ds(s, n), :] = _conv_t_at(lambda a, b: dc_s[pl.ds(a, b), :], w, 4, s, n).astype(dp_ref.dtype)

    return pl.pallas_call(
        body, name=name, grid=(3 * per,),
        in_specs=[pl.BlockSpec((None, m, cb), lambda j: (j // per, 0, j % per)),
                  pl.BlockSpec((None, m, cb), lambda j: (3 + j // per, 0, j % per)),
                  pl.BlockSpec((SUBLANES, cb), lambda j: (0, j)), pl.BlockSpec(memory_space=pl.ANY)],
        out_specs=[pl.BlockSpec((None, m, cb), lambda j: (3 + j // per, 0, j % per)), pl.BlockSpec((SUBLANES, cb), lambda j: (0, j))],
        out_shape=[jax.ShapeDtypeStruct(dpm.shape, dpm.dtype), jax.ShapeDtypeStruct((SUBLANES, 3 * seg), F32)],
        scratch_shapes=[pltpu.VMEM((m + SUBLANES, cb), F32)],
        input_output_aliases={3: 0},
        compiler_params=_params(("parallel",)),
    )(dqkv, pm, cw, dpm)


def _mxu_dot_impl(a, b, form):
    a = a.astype(MXU_DTYPE)
    b = b.astype(MXU_DTYPE)
    dims = {"nn": (((1,), (0,)), ((), ())), "nt": (((1,), (1,)), ((), ())), "tn": (((0,), (0,)), ((), ()))}[form]
    return lax.dot_general(a, b, dims, preferred_element_type=F32)


@functools.partial(jax.custom_vjp, nondiff_argnums=(2,))
def _mxu_dot(a, b, form):
    return _mxu_dot_impl(a, b, form)


def _mxu_dot_fwd(a, b, form):
    return _mxu_dot_impl(a, b, form), (a, b)


def _mxu_dot_bwd(form, saved, g):
    a, b = saved
    if form == "nn":
        return _mxu_dot_impl(g, b, "nt"), _mxu_dot_impl(a, g, "tn")
    if form == "nt":
        return _mxu_dot_impl(g, b, "nn"), _mxu_dot_impl(g, a, "tn")
    return _mxu_dot_impl(b, g, "nt"), _mxu_dot_impl(a, g, "nn")


_mxu_dot.defvjp(_mxu_dot_fwd, _mxu_dot_bwd)


_DOT_DIMS = {"nn": (((1,), (0,)), ((), ())), "nt": (((1,), (1,)), ((), ())), "tn": (((0,), (0,)), ((), ()))}


def _split(x):
    hi = x.astype(BF16)
    return hi, (x - hi.astype(F32)).astype(BF16)


def _dot3_impl(a, b, form):
    dg = lambda p, q: lax.dot_general(p, q, _DOT_DIMS[form], preferred_element_type=F32)
    ah, al = _split(a)
    bh, bl = _split(b)
    return dg(ah, bh) + (dg(ah, bl) + dg(al, bh))


@functools.partial(jax.custom_vjp, nondiff_argnums=(2,))
def _dot3(a, b, form):
    return _dot3_impl(a, b, form)


def _dot3_fwd(a, b, form):
    return _dot3_impl(a, b, form), (a, b)


def _dot3_bwd(form, saved, g):
    a, b = saved
    if form == "nn":
        return _dot3_impl(g, b, "nt"), _dot3_impl(a, g, "tn")
    if form == "nt":
        return _dot3_impl(g, b, "nn"), _dot3_impl(g, a, "tn")
    return _dot3_impl(b, g, "nt"), _dot3_impl(a, g, "nn")


_dot3.defvjp(_dot3_fwd, _dot3_bwd)


def _hdot(a, b):
    return _dot3(a, b, "nn")


def _mask_dot(mask, x, form):
    dg = lambda q: lax.dot_general(mask.astype(BF16), q, _DOT_DIMS[form], preferred_element_type=F32)
    x1 = x.astype(BF16)
    r1 = x - x1.astype(F32)
    x2 = r1.astype(BF16)
    x3 = (r1 - x2.astype(F32)).astype(BF16)
    return dg(x1) + (dg(x2) + dg(x3))


def _decay_masks(c):
    row = lax.broadcasted_iota(jnp.int32, (c, c), 0)
    col = lax.broadcasted_iota(jnp.int32, (c, c), 1)
    return (row >= col).astype(F32), row <= col


def _decay_impl(gb):
    lower, upper = _decay_masks(gb.shape[0])
    return _mask_dot(lower, gb, "nn"), _mask_dot(jnp.ones_like(gb), jnp.where(upper, gb, 0.0), "nn")


@jax.custom_vjp
def _decay_matrices(gb):
    return _decay_impl(gb)


def _decay_fwd(gb):
    return _decay_impl(gb), None


def _decay_bwd(_, cts):
    gc, gr = cts
    lower, upper = _decay_masks(gc.shape[0])
    return (_mask_dot(lower, gc, "tn") + jnp.where(upper, _mask_dot(jnp.ones_like(gr), gr, "tn"), 0.0),)


_decay_matrices.defvjp(_decay_fwd, _decay_bwd)


def _heads(f, *lists):
    return [f(*t) for t in zip(*lists)]


def _inverses_impl(a):
    c = a[0].shape[0]
    row = lax.broadcasted_iota(jnp.int32, (c, c), 0)
    col = lax.broadcasted_iota(jnp.int32, (c, c), 1)
    eye = jnp.where(row == col, 1.0, 0.0)
    x = _heads(lambda t: eye - t, a)
    p = _heads(lambda t: _dot3_impl(t, t, "nn"), a)
    power = 2
    while power < c:
        x = _heads(lambda s, t: s + _dot3_impl(s, t, "nn"), x, p)
        power *= 2
        if power < c:
            p = _heads(lambda t: _dot3_impl(t, t, "nn"), p)
    return x


@jax.custom_vjp
def _unit_lower_inverses(a):
    return _inverses_impl(a)


def _unit_lower_inverses_fwd(a):
    x = _inverses_impl(a)
    return x, x


def _unit_lower_inverses_bwd(x, g):
    t = _heads(lambda s, u: _dot3_impl(s, u, "tn"), x, g)
    return (_heads(lambda u, s: -_dot3_impl(u, s, "nt"), t, x),)


_unit_lower_inverses.defvjp(_unit_lower_inverses_fwd, _unit_lower_inverses_bwd)


def _softplus(x):
    return jnp.maximum(x, 0.0) + jnp.log(1.0 + jnp.exp(-jnp.abs(x)))


DN_STEP_CHUNKS = 3


def _dn_chunk(qr, kr, v, z, braw, araw, alog, dtb, nw, state, valid):
    c = DN_CHUNK
    nh = len(state)
    chunks = len(qr) // nh
    alog, dtb, valid_i = alog * chunks, dtb * chunks, [vv for vv in valid for _ in range(nh)]
    row = lax.broadcasted_iota(jnp.int32, (c, c), 0)
    col = lax.broadcasted_iota(jnp.int32, (c, c), 1)
    incl = row >= col
    strict = row > col
    q = _heads(lambda t: t * lax.rsqrt(jnp.sum(t * t, -1, keepdims=True) + EPS) * (DN_HEAD_DIM ** -0.5), qr)
    k = _heads(lambda t: t * lax.rsqrt(jnp.sum(t * t, -1, keepdims=True) + EPS), kr)
    beta = _heads(lambda t, vv: _sigmoid(t) * vv, braw, valid_i)
    g = _heads(lambda al, ar, dt, vv: -jnp.exp(al) * _softplus(ar + dt) * vv, alog, araw, dtb, valid_i)
    decay = _heads(lambda t: _decay_matrices(jnp.broadcast_to(t, (c, c))), g)
    dmask = _heads(lambda d: jnp.where(incl, jnp.exp(jnp.where(incl, d[0] - d[1], 0.0)), 0.0), decay)
    dec = _heads(lambda d: d[0][:, :1], decay)
    dlast = _heads(lambda d: d[0][c - 1:c, :1], decay)
    kk = _heads(lambda t: _mxu_dot(t, t, "nt"), k)
    a = _heads(lambda b, t, d: jnp.where(strict, b * t * d, 0.0), beta, kk, dmask)
    x = _unit_lower_inverses(a)
    u = _heads(lambda s, t, b: _hdot(s, t * b), x, v, beta)
    w = _heads(lambda s, t, b, d: _hdot(s, t * (b * jnp.exp(d))), x, k, beta, dec)
    qk = _heads(lambda s, t, d: _mxu_dot(s, t, "nt") * d, q, k, dmask)
    q_dec = _heads(lambda t, d: t * jnp.exp(d), q, dec)
    k_dec = _heads(lambda t, dl, d: t * jnp.exp(dl - d), k, dlast, dec)
    o = []
    for ci in range(chunks):
        of = lambda lst: lst[ci * nh:(ci + 1) * nh]
        v_new = _heads(lambda s, t, st: s - _mxu_dot(t, st, "nn"), of(u), of(w), state)
        o += _heads(lambda qd, st, s, vn: _mxu_dot(qd, st, "nn") + _mxu_dot(s, vn, "nn"), of(q_dec), state, of(qk), v_new)
        state = _heads(lambda st, dl, kd, vn: st * jnp.exp(dl) + _mxu_dot(kd, vn, "tn"), state, of(dlast), of(k_dec), v_new)
    y = _heads(lambda t, zz: _rms(t, nw) * (zz * _sigmoid(zz)), o, z)
    return y, state


DN_STEP_ROWS = DN_STEP_CHUNKS * DN_CHUNK
DN_ITEMS = [(ci, h) for ci in range(DN_STEP_CHUNKS) for h in range(DN_HEADS)]


def _dn_valid(n):
    rows = [n * DN_STEP_ROWS + ci * DN_CHUNK + lax.broadcasted_iota(jnp.int32, (DN_CHUNK, 1), 0) for ci in range(DN_STEP_CHUNKS)]
    return [(r >= PAD_ROWS).astype(F32) for r in rows]


def _dn_in_specs(rev, nc):
    cn = (lambda n: nc - 1 - n) if rev else (lambda n: n)
    c, hd = DN_STEP_ROWS, DN_HEAD_DIM
    return [
        pl.BlockSpec((None, c, DN_DIM), lambda n: (0, cn(n), 0)),
        pl.BlockSpec((None, c, DN_DIM), lambda n: (1, cn(n), 0)),
        pl.BlockSpec((None, c, DN_DIM), lambda n: (2, cn(n), 0)),
        pl.BlockSpec((None, c, DN_DIM), lambda n: (6, cn(n), 0)),
        pl.BlockSpec((DN_HEADS, 2, c, 1), lambda n: (0, 0, cn(n), 0)),
        pl.BlockSpec((DN_HEADS, SUBLANES, LANES), lambda n: (0, 0, 0)),
        pl.BlockSpec((1, hd), lambda n: (0, 0)),
    ]


def _rows(ci):
    return slice(ci * DN_CHUNK, (ci + 1) * DN_CHUNK)


def _cols(h):
    return slice(h * DN_HEAD_DIM, (h + 1) * DN_HEAD_DIM)


def _dn_load(q_ref, k_ref, v_ref, z_ref, ba_ref, hp_ref):
    heads = range(DN_HEADS)
    item = lambda ref: [ref[_rows(ci), _cols(h)] for ci, h in DN_ITEMS]
    return (item(q_ref), item(k_ref), item(v_ref), item(z_ref),
            [ba_ref[h, 0, _rows(ci), :] for ci, h in DN_ITEMS], [ba_ref[h, 1, _rows(ci), :] for ci, h in DN_ITEMS],
            [hp_ref[h, 0:1, 0:1] for h in heads], [hp_ref[h, 1:2, 0:1] for h in heads])


def _delta_fwd(qkvc, pm, ba, hp, nw, *, name):
    _, m, _ = qkvc.shape
    nc = m // DN_STEP_ROWS
    hd = DN_HEAD_DIM

    def body(q_ref, k_ref, v_ref, z_ref, ba_ref, hp_ref, nw_ref, y_ref, s_ref, state):
        n = pl.program_id(0)

        @pl.when(n == 0)
        def _():
            state[...] = jnp.zeros_like(state)

        heads = range(DN_HEADS)
        old = [state[h] for h in heads]
        y, new = _dn_chunk(*_dn_load(q_ref, k_ref, v_ref, z_ref, ba_ref, hp_ref), nw_ref[...], old, _dn_valid(n))
        for h in heads:
            s_ref[h] = old[h]
            state[h] = new[h]
        for (ci, h), yy in zip(DN_ITEMS, y):
            y_ref[_rows(ci), _cols(h)] = yy.astype(y_ref.dtype)

    return pl.pallas_call(
        body, name=name, grid=(nc,), in_specs=_dn_in_specs(False, nc),
        out_specs=[pl.BlockSpec((DN_STEP_ROWS, DN_DIM), lambda n: (n, 1)), pl.BlockSpec((DN_HEADS, None, hd, hd), lambda n: (0, n, 0, 0))],
        out_shape=[jax.ShapeDtypeStruct((m, D_CONV + DN_DIM), MXU_DTYPE), jax.ShapeDtypeStruct((DN_HEADS, nc, hd, hd), F32)],
        scratch_shapes=[pltpu.VMEM((DN_HEADS, hd, hd), F32)],
        compiler_params=_params(("arbitrary",)),
    )(qkvc, qkvc, qkvc, pm, ba, hp, nw)


def _delta_bwd(dy, qkvc, pm, ba, hp, nw, states, *, name):
    _, m, _ = qkvc.shape
    nc = m // DN_STEP_ROWS
    hd, c = DN_HEAD_DIM, DN_STEP_ROWS

    def body(q_ref, k_ref, v_ref, z_ref, ba_ref, hp_ref, nw_ref, s_ref, dy_ref,
             dz_ref, dqkv_ref, dba_ref, dhp_ref, dnw_ref, dstate):
        step = pl.program_id(0)
        n = nc - 1 - step

        @pl.when(step == 0)
        def _():
            dstate[...] = jnp.zeros_like(dstate)
            dhp_ref[...] = jnp.zeros_like(dhp_ref)
            dnw_ref[...] = jnp.zeros_like(dnw_ref)

        valid = _dn_valid(n)
        heads = range(DN_HEADS)
        fn = lambda *a: _dn_chunk(*a, valid)
        _, vjp = jax.vjp(fn, *_dn_load(q_ref, k_ref, v_ref, z_ref, ba_ref, hp_ref), nw_ref[...], [s_ref[h] for h in heads])
        dy = [dy_ref[_rows(ci), _cols(h)] for ci, h in DN_ITEMS]
        dq, dk, dv, dz, dbr, dar, dalog, ddtb, dnw, dst = vjp((dy, [dstate[h] for h in heads]))
        for i, (ci, h) in enumerate(DN_ITEMS):
            dqkv_ref[0, _rows(ci), _cols(h)] = dq[i]
            dqkv_ref[1, _rows(ci), _cols(h)] = dk[i]
            dqkv_ref[2, _rows(ci), _cols(h)] = dv[i]
            dz_ref[_rows(ci), _cols(h)] = dz[i].astype(dz_ref.dtype)
            dba_ref[h, 0, _rows(ci), :] = dbr[i]
            dba_ref[h, 1, _rows(ci), :] = dar[i]
        for h in heads:
            dstate[h] = dst[h]
            dhp_ref[h] += jnp.concatenate([jnp.broadcast_to(dalog[h], (1, LANES)), jnp.broadcast_to(ddtb[h], (1, LANES)),
                                           jnp.zeros((SUBLANES - 2, LANES), F32)], 0)
        dnw_ref[...] += dnw

    rn = lambda n: nc - 1 - n
    in_specs = _dn_in_specs(True, nc) + [
        pl.BlockSpec((DN_HEADS, None, hd, hd), lambda n: (0, rn(n), 0, 0)),
        pl.BlockSpec((c, DN_DIM), lambda n: (rn(n), 1)),
    ]
    out_specs = [
        pl.BlockSpec((None, c, DN_DIM), lambda n: (6, rn(n), 0)),
        pl.BlockSpec((3, c, DN_DIM), lambda n: (0, rn(n), 0)),
        pl.BlockSpec((DN_HEADS, 2, c, 1), lambda n: (0, 0, rn(n), 0)),
        pl.BlockSpec((DN_HEADS, SUBLANES, LANES), lambda n: (0, 0, 0)),
        pl.BlockSpec((1, hd), lambda n: (0, 0)),
    ]
    return pl.pallas_call(
        body, name=name, grid=(nc,), in_specs=in_specs, out_specs=out_specs,
        out_shape=[jax.ShapeDtypeStruct(pm.shape, MXU_DTYPE), jax.ShapeDtypeStruct(qkvc.shape, F32),
                   jax.ShapeDtypeStruct(ba.shape, F32), jax.ShapeDtypeStruct(hp.shape, F32),
                   jax.ShapeDtypeStruct((1, hd), F32)],
        scratch_shapes=[pltpu.VMEM((DN_HEADS, hd, hd), F32)],
        compiler_params=_params(("arbitrary",)),
    )(qkvc, qkvc, qkvc, pm, ba, hp, nw, states, dy)


SWA_PAIR = 2


def _attn_block(q, k0, kp, kc, v0, vp, vc, qw, kw, sink, n):
    g, b, hd = SWA_GROUP, SWA_BLOCK, SWA_HEAD_DIM
    pair = list(range(SWA_PAIR))
    lanes = lambda t, e: t[:, e * hd:(e + 1) * hd]
    q4 = [jnp.concatenate([lanes(q, e * g + i)[None] for i in range(g)], 0) for e in pair]
    qn = _heads(lambda t: _rms(t, qw) * (hd ** -0.5), q4)
    kn = [_rms(jnp.concatenate([lanes(k0, e), lanes(kp, e), lanes(kc, e)], 0), kw) for e in pair]
    vcat = [jnp.concatenate([lanes(v0, e), lanes(vp, e), lanes(vc, e)], 0) for e in pair]
    s = _heads(lambda a, k: _mxu_dot(a.reshape(g * b, hd), k, "nt").reshape(g, b, 3 * b), qn, kn)
    i = lax.broadcasted_iota(jnp.int32, (b, 3 * b), 0)
    c = lax.broadcasted_iota(jnp.int32, (b, 3 * b), 1)
    in_meta, in_prev, in_cur = c < b, (c >= b) & (c < 2 * b), c >= 2 * b
    j = c - jnp.where(in_meta, 0, jnp.where(in_prev, b, 2 * b))
    meta_lo = jnp.where(n == 0, b, PAD_ROWS)
    cur_lo = jnp.where(n == 0, PAD_ROWS, 0)
    prev_off = jnp.where(n >= 2, 0, 2 * b)
    valid = (in_meta & (j >= meta_lo)) | (in_prev & (j > i + prev_off)) | (in_cur & (j <= i) & (j >= cur_lo))
    s = _heads(lambda t: jnp.where(valid[None], t, NEG), s)
    m = [lax.stop_gradient(jnp.maximum(jnp.max(t, -1, keepdims=True), sink[e])) for e, t in zip(pair, s)]
    ex = _heads(lambda t, mm: jnp.exp(t - mm), s, m)
    p = [t / (jnp.sum(t, -1, keepdims=True) + jnp.exp(sink[e] - mm)) for e, t, mm in zip(pair, ex, m)]
    o = _heads(lambda t, v: _mxu_dot(t.reshape(g * b, 3 * b), v, "nn").reshape(g, b, hd), p, vcat)
    return jnp.concatenate([o[e][i] for e in pair for i in range(g)], 1)


Q_LANES = SWA_PAIR * SWA_GROUP * SWA_HEAD_DIM
KV_LANES = SWA_PAIR * SWA_HEAD_DIM
K_BLOCK0 = SWA_HEADS * SWA_HEAD_DIM // KV_LANES
V_BLOCK0 = K_BLOCK0 + SWA_KV_HEADS * SWA_HEAD_DIM // KV_LANES


def _attn_in_specs():
    g, b, hd = SWA_GROUP, SWA_BLOCK, SWA_HEAD_DIM
    kv = lambda f, first: pl.BlockSpec((b, KV_LANES), lambda p, n: (f(n), first + p))
    blocks = [lambda n: 0, lambda n: jnp.maximum(n - 1, 0), lambda n: n]
    return ([pl.BlockSpec((b, Q_LANES), lambda p, n: (n, p))] + [kv(f, K_BLOCK0) for f in blocks] + [kv(f, V_BLOCK0) for f in blocks]
            + [pl.BlockSpec((1, hd), lambda p, n: (0, 0)), pl.BlockSpec((1, hd), lambda p, n: (0, 0)),
               pl.BlockSpec((SWA_PAIR, g, 1, 1), lambda p, n: (p, 0, 0, 0))])


def _attn_fwd(qkv, qw, kw, sink, *, name):
    m = qkv.shape[0]
    b = SWA_BLOCK

    def body(q_ref, k0, kp, kc, v0, vp, vc, qw_ref, kw_ref, s_ref, o_ref):
        o_ref[...] = _attn_block(q_ref[...], k0[...], kp[...], kc[...], v0[...], vp[...], vc[...], qw_ref[...], kw_ref[...],
                                 s_ref[...], pl.program_id(1)).astype(o_ref.dtype)

    return pl.pallas_call(
        body, name=name, grid=(SWA_KV_HEADS // SWA_PAIR, m // b), in_specs=_attn_in_specs(),
        out_specs=pl.BlockSpec((b, Q_LANES), lambda p, n: (n, p)),
        out_shape=jax.ShapeDtypeStruct((m, SWA_HEADS * SWA_HEAD_DIM), MXU_DTYPE),
        compiler_params=_params(("parallel", "parallel")),
    )(*([qkv] * 7), qw, kw, sink)


def _attn_bwd(do, qkv, qw, kw, sink, *, name):
    m = qkv.shape[0]
    g, b, hd = SWA_GROUP, SWA_BLOCK, SWA_HEAD_DIM

    def body(q_ref, k0, kp, kc, v0, vp, vc, qw_ref, kw_ref, s_ref, do_ref, dq_ref, dk_ref, dv_ref, dqw_ref, dkw_ref, ds_ref):
        n = pl.program_id(1)

        @pl.when(n == 0)
        def _():
            for r in (dk_ref, dv_ref, dqw_ref, dkw_ref, ds_ref):
                r[...] = jnp.zeros_like(r)

        fn = lambda *a: _attn_block(*a, n)
        _, vjp = jax.vjp(fn, q_ref[...], k0[...], kp[...], kc[...], v0[...], vp[...], vc[...], qw_ref[...], kw_ref[...], s_ref[...])
        dq, dk0, dkp, dkc, dv0, dvp, dvc, dqw, dkw, dsk = vjp(do_ref[...])
        dq_ref[...] = dq
        prev = pl.multiple_of(jnp.maximum(n - 1, 0) * b, b)
        cur = pl.multiple_of(n * b, b)
        for ref, parts in ((dk_ref, (dk0, dkp, dkc)), (dv_ref, (dv0, dvp, dvc))):
            ref[pl.ds(0, b), :] += parts[0]
            ref[pl.ds(prev, b), :] += parts[1]
            ref[pl.ds(cur, b), :] += parts[2]
        dqw_ref[...] += dqw
        dkw_ref[...] += dkw
        ds_ref[...] += dsk

    pairs = SWA_KV_HEADS // SWA_PAIR
    kv_acc = pl.BlockSpec((m, KV_LANES), lambda p, n: (0, p))
    w_acc = pl.BlockSpec((None, 1, hd), lambda p, n: (p, 0, 0))
    kv_shape = jax.ShapeDtypeStruct((m, SWA_KV_HEADS * hd), F32)
    return pl.pallas_call(
        body, name=name, grid=(pairs, m // b),
        in_specs=_attn_in_specs() + [pl.BlockSpec((b, Q_LANES), lambda p, n: (n, p))],
        out_specs=[pl.BlockSpec((b, Q_LANES), lambda p, n: (n, p)), kv_acc, kv_acc, w_acc, w_acc,
                   pl.BlockSpec((SWA_PAIR, g, 1, 1), lambda p, n: (p, 0, 0, 0))],
        out_shape=[jax.ShapeDtypeStruct((m, SWA_HEADS * hd), F32), kv_shape, kv_shape,
                   jax.ShapeDtypeStruct((pairs, 1, hd), F32), jax.ShapeDtypeStruct((pairs, 1, hd), F32),
                   jax.ShapeDtypeStruct(sink.shape, F32)],
        compiler_params=_params(("parallel", "arbitrary")),
    )(*([qkv] * 7), qw, kw, sink, do)


def _loss_bwd(h, target, *, name):
    m, d = h.shape
    b = SWA_BLOCK

    def body(h_ref, t_ref, l_ref, dh_ref):
        i = pl.program_id(0)

        @pl.when(i == 0)
        def _():
            l_ref[...] = jnp.zeros_like(l_ref)
            dh_ref[...] = jnp.zeros_like(dh_ref)

        @pl.when(i > 0)
        def _():
            e = h_ref[...] - t_ref[...]
            dh_ref[...] = e * (1.0 / d)
            l_ref[...] += jnp.sum(jnp.sum(e * e, 0, keepdims=True), 1, keepdims=True) * (0.5 / d)

    return pl.pallas_call(
        body, name=name, grid=(m // b,),
        in_specs=[pl.BlockSpec((b, d), lambda i: (i, 0)), pl.BlockSpec((b, d), lambda i: (jnp.maximum(i - 1, 0), 0))],
        out_specs=[pl.BlockSpec((1, LANES), lambda i: (0, 0)), pl.BlockSpec((b, d), lambda i: (i, 0))],
        out_shape=[jax.ShapeDtypeStruct((1, LANES), F32), jax.ShapeDtypeStruct((m, d), F32)],
        compiler_params=_params(("arbitrary",)),
    )(h, target)


def _ffn_fwd(h, nw, w_up_t, cw, w_down, tag):
    u, hn = _norm_matmul(h, nw, w_up_t, o_seg=D_FF, trans_w=True, name=f"ffn_up_{tag}")
    a = _ffn_act_fwd(u, cw, name=f"ffn_act_{tag}")
    return _mm_nn(a, w_down, res=h, name=f"ffn_down_{tag}"), (h, hn, u, a)


def _ffn_bwd(dh, saved, nw, w_up_t, cw, w_down, tag):
    h, hn, u, a = saved
    da = _mm_nn(dh, w_down, trans_w=True, name=f"ffn_da_{tag}")
    dw_down = _mm_tn(a, dh, name=f"ffn_dwdown_{tag}")
    du, dcw = _ffn_act_bwd(da, u, cw, name=f"ffn_act_bwd_{tag}")
    dhn = _mm_nn(du, w_up_t, a_seg=True, name=f"ffn_dhn_{tag}")
    dw_up_t = _mm_tn(du, hn, a_seg=True, name=f"ffn_dwup_{tag}")
    dh_in, dnw = _rmsnorm_bwd(dhn, h, nw, dh, name=f"ffn_norm_bwd_{tag}")
    return dh_in, dnw, dw_up_t, dcw, dw_down


def _local_step(x, target, w, fetch=None, push=None):
    fetch = fetch or (lambda stage, after: {})
    push = push or (lambda stage, grads: None)
    plus = lambda a, zero: a if zero is None else a + zero
    seq, d = x.shape
    m = PAD_ROWS + N_META + seq
    h0 = jnp.concatenate([jnp.zeros((PAD_ROWS, d), F32), w["meta"], x], 0)

    pm, hn0 = _norm_matmul(h0, w["anw"][0], w["w_in_t"], o_seg=SEG, trans_w=True, n=N_SEG * SEG, name="mix_in")
    pba = _mm_nn(hn0, w["w_in_tail_t"], trans_w=True, name="mix_in_tail")
    qkvc = _dnpre_fwd(pm, w["dcw"], name="dn_conv")
    ba = pba[:, :2 * DN_HEADS].T.reshape(2, DN_HEADS, m, 1).transpose(1, 0, 2, 3)
    y, states = _delta_fwd(qkvc, pm, ba, w["hp"], w["dnw"], name="delta")
    y = _shortconv_fwd(pm, w["caw"], y, name="shortconv")
    w = {**w, **fetch("l0", y)}
    h1 = _mm_nn(y, w["w_out"], res=h0, name="mix_out")
    h2, ffn0 = _ffn_fwd(h1, w["fnw"][0], w["w_up0_t"], w["fcw"][0], w["w_down0"], "l0")

    w = {**w, **fetch("l1", h2)}
    qkv, hn2 = _norm_matmul(h2, w["anw"][1], w["wqkv"], name="attn_qkv")
    o = _attn_fwd(qkv, w["qnw"], w["knw"], w["sink"], name="attn")
    h3 = _mm_nn(o, w["wo"], res=h2, name="attn_out")
    h4, ffn1 = _ffn_fwd(h3, w["fnw"][1], w["w_up1_t"], w["fcw"][1], w["w_down1"], "l1")

    loss, dh4 = _loss_bwd(h4, target, name="loss")

    g = {}
    dh3, dfnw1, dwup1, dfcw1, dwdown1 = _ffn_bwd(dh4, ffn1, w["fnw"][1], w["w_up1_t"], w["fcw"][1], w["w_down1"], "l1")

    do = _mm_nn(dh3, w["wo"], trans_w=True, name="attn_do")
    g["wo"] = _mm_tn(o, dh3, name="attn_dwo")
    dq, dk, dv, dqw, dkw, dsink = _attn_bwd(do, qkv, w["qnw"], w["knw"], w["sink"], name="attn_bwd")
    dqkv = jnp.concatenate([dq, dk, dv], 1).astype(MXU_DTYPE)
    dhn2 = _mm_nn(dqkv, w["wqkv"], trans_w=True, name="attn_dhn")
    g["wqkv"] = _mm_tn(hn2, dqkv, name="attn_dwqkv")
    zero = push("l1", dict(w_up_t=dwup1, w_down=dwdown1, wo=g["wo"], wqkv=g["wqkv"]))
    dh2, danw1 = _rmsnorm_bwd(dhn2, h2, plus(w["anw"][1], zero), dh3, name="attn_norm_bwd")

    dh1, dfnw0, dwup0, dfcw0, dwdown0 = _ffn_bwd(dh2, ffn0, w["fnw"][0], w["w_up0_t"], w["fcw"][0], w["w_down0"], "l0")

    dy = _mm_nn(dh1, w["w_out"], trans_w=True, name="mix_dy")
    g["w_out"] = _mm_tn(y, dh1, name="mix_dwout")
    zero = push("l0", dict(w_up_t=dwup0, w_down=dwdown0, w_out=g["w_out"]))
    dpm, dqkvc, dba, dhp, ddnw = _delta_bwd(dy, qkvc, pm, ba, w["hp"], plus(w["dnw"], zero), states, name="delta_bwd")
    dpm, ddcw = _dnpre_bwd(dqkvc, pm, w["dcw"], dpm, name="dn_conv_bwd")
    dpm, dcaw = _shortconv_bwd(dy, pm, w["caw"], dpm, name="shortconv_bwd")
    dpba = jnp.pad(dba.transpose(1, 0, 2, 3).reshape(2 * DN_HEADS, m).T, ((0, 0), (0, LANES - 2 * DN_HEADS))).astype(MXU_DTYPE)
    g["w_in_t"] = jnp.concatenate([_mm_tn(dpm, hn0, a_seg=True, out_dtype=F32, name="mix_dwin"),
                                   _mm_tn(dpba, hn0, out_dtype=F32, name="mix_dwin_tail")[:N_TAIL]], 0).astype(GRAD_WIRE_DTYPE)
    zero = push("in", dict(w_in_t=g["w_in_t"]))
    dhn0 = _mm_nn(dpba, plus(w["w_in_tail_t"], None if zero is None else zero.astype(MXU_DTYPE)), name="mix_dhn_tail")
    dhn0 = _mm_nn(dpm, w["w_in_t"], res=dhn0, a_seg=True, name="mix_dhn")
    dh0, danw0 = _rmsnorm_bwd(dhn0, h0, w["anw"][0], dh1, name="mix_norm_bwd")

    g.update(
        x=dh0[PAD_ROWS + N_META:], meta=dh0[PAD_ROWS:PAD_ROWS + N_META], anw=[danw0, danw1], fnw=[dfnw0, dfnw1],
        caw=dcaw, dcw=ddcw, hp=dhp, dnw=ddnw, qnw=jnp.sum(dqw, 0), knw=jnp.sum(dkw, 0), sink=dsink,
        w_up_t=[dwup0, dwup1], fcw=[dfcw0, dfcw1], w_down=[dwdown0, dwdown1])
    return loss, g


N_TAIL = 2 * DN_HEADS


def _prepare_early(p):
    n_main = N_SEG * SEG
    w_in_t = p["mix_w_in_t"]
    tail_t = jnp.pad(w_in_t[n_main:], ((0, LANES - N_TAIL), (0, 0)))
    hp = jnp.zeros((DN_HEADS, SUBLANES, LANES), F32)
    hp = hp.at[:, 0, :].set(p["dn_a_log"][0][:, None]).at[:, 1, :].set(p["dn_dt_bias"][0][:, None])
    depth = p["ffn_conv_w"].shape[0]
    return dict(
        meta=p["meta_tokens"], anw=[p["attn_norm_w"][i:i + 1] for i in range(depth)],
        fnw=[p["ffn_norm_w"][i:i + 1] for i in range(depth)],
        w_in_t=w_in_t, w_in_tail_t=tail_t,
        caw=_pad_w(p["conv_a_w"][0]), dcw=_pad_w(p["dn_conv_w"][0]), hp=hp, dnw=p["dn_norm_w"],
        qnw=p["swa_q_norm_w"], knw=p["swa_k_norm_w"], sink=p["swa_sinks"].reshape(SWA_KV_HEADS, SWA_GROUP, 1, 1),
        fcw=[_pad_w(p["ffn_conv_w"][i]) for i in range(depth)])


def _prepare_weights(p):
    return dict(
        _prepare_early(dict(p, mix_w_in_t=p["mix_w_in"][0].T)), w_out=p["mix_w_out"][0], wo=p["swa_wo"][0],
        wqkv=jnp.concatenate([p["swa_wq"][0], p["swa_wk"][0], p["swa_wv"][0]], 1),
        w_up0_t=p["ffn_w_up"][0].T, w_up1_t=p["ffn_w_up"][1].T, w_down0=p["ffn_w_down"][0], w_down1=p["ffn_w_down"][1])


def _small_named(g):
    return dict(
        meta_tokens=g["meta"], attn_norm_w=jnp.concatenate(g["anw"], 0), ffn_norm_w=jnp.concatenate(g["fnw"], 0),
        conv_a_w=g["caw"][None, :3], dn_conv_w=g["dcw"][None, :4],
        dn_a_log=g["hp"][None, :, 0, 0], dn_dt_bias=g["hp"][None, :, 1, 0], dn_norm_w=g["dnw"],
        swa_q_norm_w=g["qnw"], swa_k_norm_w=g["knw"], swa_sinks=g["sink"].reshape(1, SWA_HEADS),
        ffn_conv_w=jnp.stack([c[:3] for c in g["fcw"]]))


def _reference_named(g):
    nq, nkv = SWA_HEADS * SWA_HEAD_DIM, SWA_KV_HEADS * SWA_HEAD_DIM
    return dict(
        _small_named(g), mix_w_in=g["w_in_t"].T[None],
        mix_w_out=g["w_out"][None], swa_wq=g["wqkv"][None, :, :nq], swa_wk=g["wqkv"][None, :, nq:nq + nkv],
        swa_wv=g["wqkv"][None, :, nq + nkv:], swa_wo=g["wo"][None],
        ffn_w_up=jnp.stack([t.T for t in g["w_up_t"]]), ffn_w_down=jnp.stack(g["w_down"]))


def _my_index():
    return 4 * lax.axis_index("x") + 2 * lax.axis_index("y") + lax.axis_index("c")


def _all_gather(arrays, *, name):
    n = len(arrays)

    def body(*refs):
        ins, outs = refs[:n], refs[n:2 * n]
        send_sems, recv_sems, local_sems = refs[2 * n:]
        x, y, c = lax.axis_index("x"), lax.axis_index("y"), lax.axis_index("c")
        me, sibling = (x, y, c), (x, y, 1 - c)
        chips = [(1 - x, y), (x, 1 - y), (1 - x, 1 - y)]

        def copy(i, k, block, to, src=None):
            rows = outs[i].at[4 * block[0] + 2 * block[1] + block[2]]
            return pltpu.make_async_remote_copy(
                src_ref=rows if src is None else src, dst_ref=rows, send_sem=send_sems.at[i, k], recv_sem=recv_sems.at[i, k],
                device_id=to, device_id_type=pl.DeviceIdType.MESH)

        mine = [pltpu.make_async_copy(ins[i], outs[i].at[4 * x + 2 * y + c], local_sems.at[i]) for i in range(n)]
        first = []
        for j, chip in enumerate(chips):
            first += [copy(i, 1 + j, me, (*chip, c), src=ins[i]) for i in range(n)]
        first += [copy(i, 0, me, sibling, src=ins[i]) for i in range(n)]
        for cp in first + mine:
            cp.start()
        passed = []
        for j, chip in enumerate(chips):
            for i in range(n):
                copy(i, 1 + j, (*chip, c), me).wait_recv()
                fwd = copy(i, 4 + j, (*chip, c), sibling)
                fwd.start()
                passed.append(fwd)
        for i in range(n):
            copy(i, 0, sibling, me).wait_recv()
            for j, chip in enumerate(chips):
                copy(i, 4 + j, (*chip, 1 - c), me).wait_recv()
        for cp in first + passed:
            cp.wait_send()
        for cp in mine:
            cp.wait()

    hbm = pl.BlockSpec(memory_space=pl.ANY)
    return pl.pallas_call(
        body, name=name, in_specs=[hbm] * n, out_specs=[hbm] * n,
        out_shape=[jax.ShapeDtypeStruct((N_DEV,) + tuple(a.shape), a.dtype) for a in arrays],
        scratch_shapes=[pltpu.SemaphoreType.DMA((n, 7)), pltpu.SemaphoreType.DMA((n, 7)), pltpu.SemaphoreType.DMA((n,))],
    )(*arrays)


def _peer(d):
    px, py, pc = lax.axis_index("x") ^ (d >> 2), lax.axis_index("y") ^ ((d >> 1) & 1), lax.axis_index("c") ^ (d & 1)
    return (px, py, pc), 4 * px + 2 * py + pc


def _push_copies(mode, srcs, lands, send_sems, recv_sems):
    me = _my_index()
    out = []
    for d in range(1, N_DEV):
        pos, idx = _peer(d)
        for i in range(len(srcs)):
            out.append(pltpu.make_async_remote_copy(
                src_ref=srcs[i] if mode == "gather" else srcs[i].at[idx], dst_ref=lands[i].at[me],
                send_sem=send_sems.at[i * N_DEV + d], recv_sem=recv_sems.at[i * N_DEV + d], device_id=pos,
                device_id_type=pl.DeviceIdType.MESH))
    return out


_HBM = pl.BlockSpec(memory_space=pltpu.HBM)
_SEM = pl.BlockSpec(memory_space=pltpu.SEMAPHORE)


def _push_start(mode, arrays, follows, *, name):
    n = len(arrays)
    blocks = [a.shape if mode == "gather" else a.shape[1:] for a in arrays]
    lands = [lax.empty((N_DEV,) + tuple(b), a.dtype) for a, b in zip(arrays, blocks)]

    def body(*refs):
        srcs, land_refs = refs[:n], refs[n:2 * n]
        send_sems, recv_sems = refs[2 * n + 1], refs[2 * n + 2]
        zero = refs[-1]
        for cp in _push_copies(mode, srcs, land_refs, send_sems, recv_sems):
            cp.start()
        zero[...] = jnp.zeros_like(zero)

    hbm_in = [pltpu.with_memory_space_constraint(a, pltpu.HBM) for a in list(arrays) + lands]
    outs = pl.pallas_call(
        body, name=name,
        out_shape=[pltpu.SemaphoreType.DMA((n * N_DEV,)), pltpu.SemaphoreType.DMA((n * N_DEV,))]
        + [pltpu.HBM(a.shape, a.dtype) for a in hbm_in] + [jax.ShapeDtypeStruct((SUBLANES, LANES), F32)],
        in_specs=[_HBM] * (2 * n) + [pl.BlockSpec(memory_space=pl.ANY)],
        out_specs=[_SEM, _SEM] + [_HBM] * (2 * n) + [pl.BlockSpec(memory_space=pltpu.VMEM)],
        input_output_aliases={i: 2 + i for i in range(2 * n)},
        compiler_params=pltpu.CompilerParams(has_side_effects=pltpu.SideEffectType.DATAFLOW_SIDE_EFFECTING),
    )(*hbm_in, follows)
    return dict(mode=mode, sems=outs[:2], srcs=outs[2:2 + n], lands=outs[2 + n:2 + 2 * n], zero=outs[-1])


def _push_wait(push, follows, *, name):
    n = len(push["srcs"])
    mode = push["mode"]

    def body(*refs):
        srcs, land_refs = refs[:n], refs[n:2 * n]
        send_sems, recv_sems = refs[2 * n], refs[2 * n + 1]
        for cp in _push_copies(mode, srcs, land_refs, send_sems, recv_sems):
            cp.wait_send()
            cp.wait_recv()

    args = list(push["srcs"]) + list(push["lands"])
    outs = pl.pallas_call(
        body, name=name, out_shape=[pltpu.HBM(a.shape, a.dtype) for a in args],
        in_specs=[_HBM] * (2 * n) + [_SEM, _SEM, pl.BlockSpec(memory_space=pl.ANY)], out_specs=[_HBM] * (2 * n),
        input_output_aliases={i: i for i in range(2 * n)},
        compiler_params=pltpu.CompilerParams(has_side_effects=pltpu.SideEffectType.DATAFLOW_SIDE_EFFECTING),
    )(*args, *push["sems"], follows)
    me = _my_index()
    got = []
    for src, land in zip(outs[:n], outs[n:]):
        own = src if mode == "gather" else lax.dynamic_index_in_dim(src, me, 0, keepdims=False)
        got.append(lax.dynamic_update_index_in_dim(land, own, me, 0))
    return got


ADAMW_BLOCK_BYTES = 6 * 1024 * 1024


def _adamw_tile(r, c):
    fits = lambda tr, tc: N_DEV * tr * tc * 4 <= ADAMW_BLOCK_BYTES
    rows = [t for t in range(2 * SUBLANES, r + 1, 2 * SUBLANES) if r % t == 0 and fits(t, c)]
    if rows or fits(r, c):
        return (max(rows) if rows else r), c
    cols = [t for t in range(LANES, c + 1, LANES) if c % t == 0 and fits(r, t)]
    return r, max(cols)


def _adamw(parts, w, m, v, layer, outs=None, *, name):
    nl, r, c = w.shape
    tr, tc = _adamw_tile(r, c)

    def body(p_ref, w_ref, m_ref, v_ref, *rest):
        g_ref, d_ref, nm_ref, nv_ref = rest[-4:]
        g = p_ref[0].astype(F32)
        for j in range(1, N_DEV):
            g = g + p_ref[j].astype(F32)
        m2 = ADAM_B1 * m_ref[...] + (1.0 - ADAM_B1) * g
        v2 = ADAM_B2 * v_ref[...] + (1.0 - ADAM_B2) * jnp.square(g)
        m_hat = m2 / (1.0 - ADAM_B1 ** ADAM_STEP)
        v_hat = v2 / (1.0 - ADAM_B2 ** ADAM_STEP)
        g_ref[...] = g
        d_ref[...] = -ADAM_LR * (m_hat / (jnp.sqrt(v_hat) + ADAM_EPS) + ADAM_WD * w_ref[...])
        nm_ref[...] = m2
        nv_ref[...] = v2

    blk = pl.BlockSpec((None, tr, tc), lambda i, j: (layer, i, j))
    out = jax.ShapeDtypeStruct((nl, r, c), F32)
    given = list(outs) if outs is not None else []
    return pl.pallas_call(
        body, name=name, grid=(r // tr, c // tc),
        in_specs=[pl.BlockSpec((N_DEV, tr, tc), lambda i, j: (0, i, j)), blk, blk, blk] + [pl.BlockSpec(memory_space=pl.ANY)] * len(given),
        out_specs=[blk, blk, blk, blk], out_shape=[out, out, out, out],
        input_output_aliases={4 + t: t for t in range(len(given))},
        compiler_params=_params(("parallel", "parallel")),
    )(parts, w, m, v, *given)


SHARD_AXIS = dict(
    meta_tokens=1, attn_norm_w=None, ffn_norm_w=None, mix_w_in=2, conv_a_w=2, dn_conv_w=2, dn_a_log=None, dn_dt_bias=None,
    dn_norm_w=None, mix_w_out=1, swa_wq=1, swa_wk=1, swa_wv=1, swa_q_norm_w=None, swa_k_norm_w=None, swa_sinks=None,
    swa_wo=1, ffn_w_up=2, ffn_conv_w=2, ffn_w_down=1)
WEIGHTS = list(SHARD_AXIS)
BIG = ["mix_w_in", "mix_w_out", "swa_wq", "swa_wk", "swa_wv", "swa_wo", "ffn_w_up", "ffn_w_down"]
SMALL = [k for k in WEIGHTS if k not in BIG]
SMALL_SHARDED = [k for k in SMALL if SHARD_AXIS[k] is not None]


def _whole(g8, axis):
    t = jnp.moveaxis(g8, 0, axis)
    return t.reshape(t.shape[:axis] + (t.shape[axis] * t.shape[axis + 1],) + t.shape[axis + 2:])


def _by_owner(a, axis):
    s = a.shape[axis] // N_DEV
    return jnp.moveaxis(a.reshape(a.shape[:axis] + (N_DEV, s) + a.shape[axis + 1:]), axis, 0)


def _pack(arrays, lead=0):
    flat = jnp.concatenate([a.reshape(a.shape[:lead] + (-1,)) for a in arrays], -1)
    n = flat.shape[-1]
    rows = -(-n // (SUBLANES * LANES)) * SUBLANES
    flat = jnp.pad(flat, [(0, 0)] * lead + [(0, rows * LANES - n)])
    return flat.reshape(flat.shape[:lead] + (rows, LANES))


def _unpack(buf, shapes, lead=0):
    flat = buf.reshape(buf.shape[:lead] + (-1,))
    out, o = [], 0
    for s in shapes:
        n = 1
        for e in s:
            n *= e
        out.append(flat[..., o:o + n].reshape(buf.shape[:lead] + tuple(s)))
        o += n
    return out


def kernel(x, meta_tokens, attn_norm_w, ffn_norm_w, mix_w_in, conv_a_w, dn_conv_w, dn_a_log, dn_dt_bias, dn_norm_w, mix_w_out, swa_wq, swa_wk, swa_wv, swa_q_norm_w, swa_k_norm_w, swa_sinks, swa_wo, ffn_w_up, ffn_conv_w, ffn_w_down, loss_target, m_meta_tokens, m_attn_norm_w, m_ffn_norm_w, m_mix_w_in, m_conv_a_w, m_dn_conv_w, m_dn_a_log, m_dn_dt_bias, m_dn_norm_w, m_mix_w_out, m_swa_wq, m_swa_wk, m_swa_wv, m_swa_q_norm_w, m_swa_k_norm_w, m_swa_sinks, m_swa_wo, m_ffn_w_up, m_ffn_conv_w, m_ffn_w_down, v_meta_tokens, v_attn_norm_w, v_ffn_norm_w, v_mix_w_in, v_conv_a_w, v_dn_conv_w, v_dn_a_log, v_dn_dt_bias, v_dn_norm_w, v_mix_w_out, v_swa_wq, v_swa_wk, v_swa_wv, v_swa_q_norm_w, v_swa_k_norm_w, v_swa_sinks, v_swa_wo, v_ffn_w_up, v_ffn_conv_w, v_ffn_w_down):
    w = dict(meta_tokens=meta_tokens, attn_norm_w=attn_norm_w, ffn_norm_w=ffn_norm_w, mix_w_in=mix_w_in, conv_a_w=conv_a_w, dn_conv_w=dn_conv_w, dn_a_log=dn_a_log, dn_dt_bias=dn_dt_bias, dn_norm_w=dn_norm_w, mix_w_out=mix_w_out, swa_wq=swa_wq, swa_wk=swa_wk, swa_wv=swa_wv, swa_q_norm_w=swa_q_norm_w, swa_k_norm_w=swa_k_norm_w, swa_sinks=swa_sinks, swa_wo=swa_wo, ffn_w_up=ffn_w_up, ffn_conv_w=ffn_conv_w, ffn_w_down=ffn_w_down)
    mom = dict(meta_tokens=m_meta_tokens, attn_norm_w=m_attn_norm_w, ffn_norm_w=m_ffn_norm_w, mix_w_in=m_mix_w_in, conv_a_w=m_conv_a_w, dn_conv_w=m_dn_conv_w, dn_a_log=m_dn_a_log, dn_dt_bias=m_dn_dt_bias, dn_norm_w=m_dn_norm_w, mix_w_out=m_mix_w_out, swa_wq=m_swa_wq, swa_wk=m_swa_wk, swa_wv=m_swa_wv, swa_q_norm_w=m_swa_q_norm_w, swa_k_norm_w=m_swa_k_norm_w, swa_sinks=m_swa_sinks, swa_wo=m_swa_wo, ffn_w_up=m_ffn_w_up, ffn_conv_w=m_ffn_conv_w, ffn_w_down=m_ffn_w_down)
    var = dict(meta_tokens=v_meta_tokens, attn_norm_w=v_attn_norm_w, ffn_norm_w=v_ffn_norm_w, mix_w_in=v_mix_w_in, conv_a_w=v_conv_a_w, dn_conv_w=v_dn_conv_w, dn_a_log=v_dn_a_log, dn_dt_bias=v_dn_dt_bias, dn_norm_w=v_dn_norm_w, mix_w_out=v_mix_w_out, swa_wq=v_swa_wq, swa_wk=v_swa_wk, swa_wv=v_swa_wv, swa_q_norm_w=v_swa_q_norm_w, swa_k_norm_w=v_swa_k_norm_w, swa_sinks=v_swa_sinks, swa_wo=v_swa_wo, ffn_w_up=v_ffn_w_up, ffn_conv_w=v_ffn_conv_w, ffn_w_down=v_ffn_w_down)
    me = _my_index()

    transposed = ("mix_w_in", "ffn_w_up")
    view = lambda k, a: jnp.swapaxes(a, 1, 2) if k in transposed else a
    axis2d = {k: 0 if k in transposed else SHARD_AXIS[k] - 1 for k in BIG}
    shard16 = {k: view(k, w[k]).astype(MXU_DTYPE) for k in BIG}
    small_shard_shapes = [w[k].shape for k in SMALL_SHARDED]
    rows_in = shard16["mix_w_in"].shape[1]
    sent_in = jnp.pad(shard16["mix_w_in"][0], ((0, -rows_in % (2 * SUBLANES)), (0, 0)))
    got = _all_gather([sent_in, _pack([w[k] for k in SMALL_SHARDED])], name="gather_weights")
    whole = {"mix_w_in_t": _whole(got[0][:, :rows_in], 0)}
    for k, a in zip(SMALL_SHARDED, _unpack(got[1], small_shard_shapes, lead=1)):
        whole[k] = _whole(a, SHARD_AXIS[k])
    for k in SMALL:
        whole.setdefault(k, w[k])
    stages = {"in": [("mix_w_in", 0)], "l0": [("mix_w_out", 0), ("ffn_w_up", 0), ("ffn_w_down", 0)],
              "l1": [("swa_wq", 0), ("swa_wk", 0), ("swa_wv", 0), ("swa_wo", 0), ("ffn_w_up", 1), ("ffn_w_down", 1)]}
    pushed = {}
    follows = got[1]
    for stage in ("l0", "l1"):
        pushed[stage] = _push_start("gather", [shard16[k][l] for k, l in stages[stage]], follows, name=f"push_weights_{stage}")
        follows = pushed[stage]["zero"]
    early = _prepare_early(whole)
    early["anw"][0] = early["anw"][0] + follows[0, 0]

    def fetch(stage, after):
        got = _push_wait(pushed[stage], after, name=f"wait_weights_{stage}")
        full = {kl: _whole(a, axis2d[kl[0]]) for kl, a in zip(stages[stage], got)}
        if stage == "l0":
            return dict(w_out=full["mix_w_out", 0], w_up0_t=full["ffn_w_up", 0], w_down0=full["ffn_w_down", 0])
        wqkv = jnp.concatenate([full["swa_wq", 0], full["swa_wk", 0], full["swa_wv", 0]], 1)
        return dict(wqkv=wqkv, wo=full["swa_wo", 0], w_up1_t=full["ffn_w_up", 1], w_down1=full["ffn_w_down", 1])

    nq, nkv = SWA_HEADS * SWA_HEAD_DIM, SWA_KV_HEADS * SWA_HEAD_DIM
    grad_pushes = {}

    def push(stage, gd):
        if stage == "in":
            named = {("mix_w_in", 0): gd["w_in_t"]}
        elif stage == "l1":
            named = {("swa_wq", 0): gd["wqkv"][:, :nq], ("swa_wk", 0): gd["wqkv"][:, nq:nq + nkv],
                     ("swa_wv", 0): gd["wqkv"][:, nq + nkv:], ("swa_wo", 0): gd["wo"],
                     ("ffn_w_up", 1): gd["w_up_t"], ("ffn_w_down", 1): gd["w_down"]}
        else:
            named = {("mix_w_out", 0): gd["w_out"], ("ffn_w_up", 0): gd["w_up_t"], ("ffn_w_down", 0): gd["w_down"]}
        sent = [_by_owner(named[kl], axis2d[kl[0]]) for kl in stages[stage]]
        grad_pushes[stage] = _push_start("scatter", sent, jnp.zeros((SUBLANES, LANES), F32), name=f"push_grads_{stage}")
        return grad_pushes[stage]["zero"][0, 0]

    loss, g = _local_step(x[0], loss_target[0], early, fetch, push)
    grads = _small_named(g)

    results = {}

    def update(stage, follows):
        got = _push_wait(grad_pushes[stage], follows, name=f"wait_grads_{stage}")
        for (k, l), parts in zip(stages[stage], got):
            w3, m3, v3 = view(k, w[k]), view(k, mom[k]), view(k, var[k])
            results[k] = _adamw(parts.reshape((N_DEV,) + w3.shape[1:]), w3, m3, v3, l, results.get(k), name=f"adamw_{k}_{l}")
        return results[stages[stage][0][0]][0]

    follows = update("l0", update("l1", g["meta"]))

    small_shapes = [grads[k].shape for k in SMALL]
    (all_small,) = _all_gather([_pack([loss] + [grads[k].astype(F32) for k in SMALL])], name="gather_small_grads")
    loss_parts, *small_parts = _unpack(all_small, [loss.shape] + small_shapes, lead=1)
    mine = []
    for k, p in zip(SMALL, small_parts):
        ax = SHARD_AXIS[k]
        mine.append(p if ax is None else lax.dynamic_slice_in_dim(p, me * w[k].shape[ax], w[k].shape[ax], 1 + ax))
    zero = jnp.zeros(loss.shape, F32)
    packed = [_pack([z] + [d[k] for k in SMALL]) for z, d in ((zero, w), (zero, mom), (zero, var))]
    res = _adamw(_pack([loss_parts] + mine, lead=1), *[t[None] for t in packed], 0, name="adamw_small")
    shapes = [loss.shape] + [w[k].shape for k in SMALL]
    for t, which in zip(res, range(4)):
        for k, a in zip(["loss"] + SMALL, _unpack(t[0], shapes)):
            results.setdefault(k, [None] * 4)[which] = a
    update("in", jnp.maximum(follows[0, :1, :1], res[0][0, :1, :1]))

    outs = [[view(k, results[k][which]) for k in WEIGHTS] for which in range(4)]
    return (results["loss"][0][0, 0], g["x"][None], *outs[0], *outs[1], *outs[2], *outs[3])
```

```python
import functools

import jax
import jax.numpy as jnp
from jax import lax
from jax.experimental import pallas as pl
from jax.experimental.pallas import tpu as pltpu

F32 = jnp.float32
BF16 = jnp.bfloat16
MXU_DTYPE = BF16
GRAD_WIRE_DTYPE = BF16

D_MODEL = 1024
N_META = 16
PAD_ROWS = 112
D_CONV = 512
DN_HEADS = 4
DN_HEAD_DIM = 128
DN_DIM = DN_HEADS * DN_HEAD_DIM
DN_CHUNK = 64
SEG = 512
N_SEG = 7
SWA_HEADS = 16
SWA_KV_HEADS = 4
SWA_GROUP = SWA_HEADS // SWA_KV_HEADS
SWA_HEAD_DIM = 64
SWA_BLOCK = 128
D_FF = 2816
EPS = 1e-6
NEG = -1e30
N_DEV = 8

ADAM_LR = 0.001
ADAM_B1 = 0.9
ADAM_B2 = 0.999
ADAM_EPS = 1e-08
ADAM_WD = 0.01
ADAM_STEP = 10

VMEM_LIMIT_BYTES = 52 * 1024 * 1024
SUBLANES = 8
LANES = 128


def _pick(n, prefs):
    for p in prefs:
        if n % p == 0:
            return p
    return n


def _params(sem, vmem=VMEM_LIMIT_BYTES):
    return pltpu.CompilerParams(dimension_semantics=sem, vmem_limit_bytes=vmem)


def _rms(x, w):
    return x * lax.rsqrt(jnp.mean(x * x, -1, keepdims=True) + EPS) * w


def _norm_matmul(h, nw, w, *, o_seg=None, trans_w=False, n=None, name):
    m, k = h.shape
    n = n or (w.shape[0] if trans_w else w.shape[1])
    tm = _pick(m, (1408, 384, 128))
    tn = _pick(o_seg or n, (1408, 1024, 512, 256, 128))
    dims = _DOT_DIMS["nt" if trans_w else "nn"]

    def body(h_ref, nw_ref, w_ref, o_ref, hn_ref, hn_s):
        @pl.when(pl.program_id(1) == 0)
        def _():
            hn = _rms(h_ref[...], nw_ref[...]).astype(MXU_DTYPE)
            hn_s[...] = hn
            hn_ref[...] = hn

        o_ref[...] = lax.dot_general(hn_s[...], w_ref[...], dims, preferred_element_type=F32)

    if o_seg:
        per = o_seg // tn
        o_shape = jax.ShapeDtypeStruct((n // o_seg, m, o_seg), F32)
        o_spec = pl.BlockSpec((None, tm, tn), lambda i, j: (j // per, i, j % per))
    else:
        o_shape = jax.ShapeDtypeStruct((m, n), F32)
        o_spec = pl.BlockSpec((tm, tn), lambda i, j: (i, j))
    return pl.pallas_call(
        body, name=name, grid=(m // tm, n // tn),
        in_specs=[pl.BlockSpec((tm, k), lambda i, j: (i, 0)), pl.BlockSpec((1, k), lambda i, j: (0, 0)),
                  pl.BlockSpec((tn, k), lambda i, j: (j, 0)) if trans_w else pl.BlockSpec((k, tn), lambda i, j: (0, j))],
        out_specs=[o_spec, pl.BlockSpec((tm, k), lambda i, j: (i, 0))],
        out_shape=[o_shape, jax.ShapeDtypeStruct((m, k), MXU_DTYPE)],
        scratch_shapes=[pltpu.VMEM((tm, k), MXU_DTYPE)],
        compiler_params=_params(("parallel", "arbitrary")),
    )(h, nw, w)


TILE_BUDGET_BYTES = 38 * 1024 * 1024
TILE_SIZES = (4224, 2816, 1792, 1536, 1408, 1024, 512, 256, 128)


def _divisor_tiles(n):
    return [t for t in TILE_SIZES if n % t == 0] or [n]


def _mm_nn(a, w, *, res=None, a_seg=False, trans_w=False, w_seg=False, out_dtype=F32, name):
    if a_seg:
        s, m, seg = a.shape
    else:
        m, seg = a.shape
        s = 1
    k = s * seg
    n = w.shape[0] * w.shape[2] if w_seg else (w.shape[0] if trans_w else w.shape[1])
    n_seg = w.shape[2] if w_seg else n
    tm = _pick(m, (1408, 1024, 512, 384, 256, 128))
    ab = a.dtype.itemsize
    k_steps = [(sb, seg) for sb in range(s, 0, -1) if s % sb == 0] if a_seg else [(1, t) for t in _divisor_tiles(seg)]
    best = None
    for tn in _divisor_tiles(n_seg):
        for sb, tk1 in k_steps:
            tk = sb * tk1
            nk = k // tk
            need = 2 * tm * tk * ab + 2 * tk * tn * 2 + 2 * tm * tn * 4 + (tm * tn * 4 if nk > 1 else 0) + (2 * tm * tn * 4 if res is not None else 0)
            if need <= TILE_BUDGET_BYTES and (best is None or tk * tn > best[0]):
                best = (tk * tn, tn, sb, tk1)
    _, tn, sb, tk1 = best
    tk = sb * tk1
    nk = k // tk
    w_dims = _DOT_DIMS["nt" if trans_w else "nn"]

    def body(*refs):
        a_ref, w_ref = refs[:2]
        r_ref = refs[2] if res is not None else None
        o_ref = refs[3 if res is not None else 2]

        def partial_product():
            if not a_seg:
                return lax.dot_general(a_ref[...].astype(MXU_DTYPE), w_ref[...], w_dims, preferred_element_type=F32)
            out = None
            for t in range(sb):
                wt = w_ref[:, t * seg:(t + 1) * seg] if trans_w else w_ref[t * seg:(t + 1) * seg, :]
                d = lax.dot_general(a_ref[t].astype(MXU_DTYPE), wt, w_dims, preferred_element_type=F32)
                out = d if out is None else out + d
            return out

        if nk == 1:
            o_ref[...] = (partial_product() if res is None else partial_product() + r_ref[...]).astype(o_ref.dtype)
            return
        acc = refs[-1]
        kk = pl.program_id(2)

        @pl.when(kk == 0)
        def _():
            acc[...] = jnp.zeros_like(acc)

        acc[...] += partial_product()

        @pl.when(kk == nk - 1)
        def _():
            o_ref[...] = (acc[...] if res is None else acc[...] + r_ref[...]).astype(o_ref.dtype)

    a_spec = pl.BlockSpec((sb, tm, seg), lambda i, j, kk: (kk, i, 0)) if a_seg else pl.BlockSpec((tm, tk), lambda i, j, kk: (i, kk))
    if w_seg:
        per = n_seg // tn
        w_spec = pl.BlockSpec((None, tk, tn), lambda i, j, kk: (j // per, kk, j % per))
    elif trans_w:
        w_spec = pl.BlockSpec((tn, tk), lambda i, j, kk: (j, kk))
    else:
        w_spec = pl.BlockSpec((tk, tn), lambda i, j, kk: (kk, j))
    in_specs = [a_spec, w_spec]
    args = [a, w]
    if res is not None:
        in_specs.append(pl.BlockSpec((tm, tn), lambda i, j, kk: (i, j)))
        args.append(res)
    return pl.pallas_call(
        body, name=name, grid=(m // tm, n // tn, nk), in_specs=in_specs,
        out_specs=pl.BlockSpec((tm, tn), lambda i, j, kk: (i, j)),
        out_shape=jax.ShapeDtypeStruct((m, n), out_dtype),
        scratch_shapes=[pltpu.VMEM((tm, tn), F32)] if nk > 1 else [],
        compiler_params=_params(("parallel", "parallel", "arbitrary")),
    )(*args)


def _mm_tn(a, b, *, a_seg=False, b_seg=False, out_dtype=None, name):
    out_dtype = out_dtype or GRAD_WIRE_DTYPE
    if a_seg:
        sa, m, a_unit = a.shape
        ka = sa * a_unit
    else:
        m, ka = a.shape
        a_unit = ka
    if b_seg:
        s, _, seg = b.shape
        n = s * seg
    else:
        n = b.shape[1]
        seg = n
    tmc = _pick(m, (1408, 384, 128))
    best = None
    for tka in _divisor_tiles(a_unit):
        for tn in _divisor_tiles(seg):
            need = 2 * tmc * tka * a.dtype.itemsize + 2 * tmc * tn * b.dtype.itemsize + tka * tn * 4 + 2 * tka * tn * 4
            if need <= TILE_BUDGET_BYTES and (best is None or (tka * tn, tn) > best[:2]):
                best = (tka * tn, tn, tka)
    _, tn, tka = best
    nm = m // tmc

    def body(a_ref, b_ref, o_ref, acc):
        mm = pl.program_id(2)

        @pl.when(mm == 0)
        def _():
            acc[...] = jnp.zeros_like(acc)

        acc[...] += lax.dot_general(a_ref[...].astype(MXU_DTYPE), b_ref[...].astype(MXU_DTYPE),
                                    (((0,), (0,)), ((), ())), preferred_element_type=F32)

        @pl.when(mm == nm - 1)
        def _():
            o_ref[...] = acc[...].astype(o_ref.dtype)

    if b_seg:
        per = seg // tn
        b_spec = pl.BlockSpec((None, tmc, tn), lambda i, j, mm: (j // per, mm, j % per))
    else:
        b_spec = pl.BlockSpec((tmc, tn), lambda i, j, mm: (mm, j))
    if a_seg:
        a_per = a_unit // tka
        a_spec = pl.BlockSpec((None, tmc, tka), lambda i, j, mm: (i // a_per, mm, i % a_per))
    else:
        a_spec = pl.BlockSpec((tmc, tka), lambda i, j, mm: (mm, i))
    return pl.pallas_call(
        body, name=name, grid=(ka // tka, n // tn, nm),
        in_specs=[a_spec, b_spec],
        out_specs=pl.BlockSpec((tka, tn), lambda i, j, mm: (i, j)),
        out_shape=jax.ShapeDtypeStruct((ka, n), out_dtype),
        scratch_shapes=[pltpu.VMEM((tka, tn), F32)],
        compiler_params=_params(("parallel", "parallel", "arbitrary")),
    )(a, b)


def _rmsnorm_bwd(dhn, h, nw, dres, *, name):
    m, d = h.shape
    tm = _pick(m, (384, 128))

    def body(dhn_ref, h_ref, nw_ref, dres_ref, dh_ref, dnw_ref):
        i = pl.program_id(0)
        x = h_ref[...]
        r = lax.rsqrt(jnp.mean(x * x, -1, keepdims=True) + EPS)
        xh = x * r
        dy = dhn_ref[...]
        dxh = dy * nw_ref[...]
        dx = r * (dxh - xh * jnp.mean(dxh * xh, -1, keepdims=True))
        row = i * tm + lax.broadcasted_iota(jnp.int32, (tm, 1), 0)
        dh_ref[...] = jnp.where(row >= PAD_ROWS, dres_ref[...] + dx, 0.0)

        @pl.when(i == 0)
        def _():
            dnw_ref[...] = jnp.zeros_like(dnw_ref)

        dnw_ref[...] += jnp.sum(dy * xh, 0, keepdims=True)

    return pl.pallas_call(
        body, name=name, grid=(m // tm,),
        in_specs=[pl.BlockSpec((tm, d), lambda i: (i, 0)), pl.BlockSpec((tm, d), lambda i: (i, 0)),
                  pl.BlockSpec((1, d), lambda i: (0, 0)), pl.BlockSpec((tm, d), lambda i: (i, 0))],
        out_specs=[pl.BlockSpec((tm, d), lambda i: (i, 0)), pl.BlockSpec((1, d), lambda i: (0, 0))],
        out_shape=[jax.ShapeDtypeStruct((m, d), F32), jax.ShapeDtypeStruct((1, d), F32)],
        compiler_params=_params(("arbitrary",)),
    )(dhn, h, nw, dres)


ROW_CHUNK = 248


def _row_chunks(m):
    out, s = [], SUBLANES
    while s < m:
        n = min(ROW_CHUNK, m - s)
        out.append((s, n))
        s += n
    return out


def _conv_at(load, w, width, s, n):
    acc = w[width - 1:width, :] * load(s, n)
    for j in range(width - 1):
        acc = acc + w[j:j + 1, :] * load(s - (width - 1 - j), n)
    return acc


def _conv_t_at(load, w, width, s, n):
    acc = w[width - 1:width, :] * load(s, n)
    for j in range(width - 1):
        acc = acc + w[j:j + 1, :] * load(s + (width - 1 - j), n)
    return acc


def _dconv_w(load_x, d, width, s, n):
    rows = [jnp.sum(d * load_x(s - (width - 1 - j), n), 0, keepdims=True) for j in range(width)]
    rows.append(jnp.zeros((SUBLANES - width, d.shape[1]), F32))
    return jnp.concatenate(rows, 0)


def _pad_w(w):
    return jnp.concatenate([w, jnp.zeros((SUBLANES - w.shape[0], w.shape[1]), w.dtype)], 0)


def _sigmoid(x):
    return 1.0 / (1.0 + jnp.exp(-x))


def _ffn_act_fwd(u, cw, *, name):
    _, m, f = u.shape
    cb = _pick(f, (256, 128))
    chunks = _row_chunks(m)

    def body(g_ref, v_ref, w_ref, o_ref):
        w = w_ref[...]
        o_ref[pl.ds(0, SUBLANES), :] = jnp.zeros((SUBLANES, cb), o_ref.dtype)
        for s, n in chunks:
            c = _conv_at(lambda a, b: g_ref[pl.ds(a, b), :], w, 3, s, n)
            o_ref[pl.ds(s, n), :] = (c * _sigmoid(c) * v_ref[pl.ds(s, n), :]).astype(o_ref.dtype)

    return pl.pallas_call(
        body, name=name, grid=(f // cb,),
        in_specs=[pl.BlockSpec((None, m, cb), lambda j: (0, 0, j)), pl.BlockSpec((None, m, cb), lambda j: (1, 0, j)),
                  pl.BlockSpec((SUBLANES, cb), lambda j: (0, j))],
        out_specs=pl.BlockSpec((m, cb), lambda j: (0, j)),
        out_shape=jax.ShapeDtypeStruct((m, f), MXU_DTYPE),
        compiler_params=_params(("parallel",)),
    )(u, u, cw)


def _ffn_act_bwd(da, u, cw, *, name):
    _, m, f = u.shape
    cb = LANES
    chunks = _row_chunks(m)

    def body(da_ref, g_ref, v_ref, w_ref, du_ref, dw_ref, dg_s):
        w = w_ref[...]
        zeros8 = jnp.zeros((SUBLANES, cb), F32)
        dg_s[pl.ds(0, SUBLANES), :] = zeros8
        dg_s[pl.ds(m, SUBLANES), :] = zeros8
        du_ref[0, pl.ds(0, SUBLANES), :] = zeros8.astype(du_ref.dtype)
        du_ref[1, pl.ds(0, SUBLANES), :] = zeros8.astype(du_ref.dtype)
        load_g = lambda a, b: g_ref[pl.ds(a, b), :]
        dw = jnp.zeros((SUBLANES, cb), F32)
        for s, n in chunks:
            c = _conv_at(load_g, w, 3, s, n)
            sg = _sigmoid(c)
            d = da_ref[pl.ds(s, n), :]
            du_ref[1, pl.ds(s, n), :] = (d * (c * sg)).astype(du_ref.dtype)
            dc = d * v_ref[pl.ds(s, n), :] * (sg * (1.0 + c * (1.0 - sg)))
            dg_s[pl.ds(s, n), :] = dc
            dw = dw + _dconv_w(load_g, dc, 3, s, n)
        dw_ref[...] = dw
        for s, n in chunks:
            du_ref[0, pl.ds(s, n), :] = _conv_t_at(lambda a, b: dg_s[pl.ds(a, b), :], w, 3, s, n).astype(du_ref.dtype)

    return pl.pallas_call(
        body, name=name, grid=(f // cb,),
        in_specs=[pl.BlockSpec((m, cb), lambda j: (0, j)), pl.BlockSpec((None, m, cb), lambda j: (0, 0, j)),
                  pl.BlockSpec((None, m, cb), lambda j: (1, 0, j)), pl.BlockSpec((SUBLANES, cb), lambda j: (0, j))],
        out_specs=[pl.BlockSpec((2, m, cb), lambda j: (0, 0, j)), pl.BlockSpec((SUBLANES, cb), lambda j: (0, j))],
        out_shape=[jax.ShapeDtypeStruct((2, m, f), MXU_DTYPE), jax.ShapeDtypeStruct((SUBLANES, f), F32)],
        scratch_shapes=[pltpu.VMEM((m + SUBLANES, cb), F32)],
        compiler_params=_params(("parallel",)),
    )(da, u, u, cw)


def _shortconv_fwd(pm, cw, y, *, name):
    _, m, seg = pm.shape
    cb = _pick(seg, (256, 128))
    chunks = _row_chunks(m)

    def body(gi_ref, go_ref, ah_ref, w_ref, y_in, o_ref):
        del y_in
        w = w_ref[...]
        o_ref[pl.ds(0, SUBLANES), :] = jnp.zeros((SUBLANES, cb), o_ref.dtype)
        load_m = lambda a, b: gi_ref[pl.ds(a, b), :] * ah_ref[pl.ds(a, b), :]
        for s, n in chunks:
            o_ref[pl.ds(s, n), :] = (go_ref[pl.ds(s, n), :] * _conv_at(load_m, w, 3, s, n)).astype(o_ref.dtype)

    return pl.pallas_call(
        body, name=name, grid=(seg // cb,),
        in_specs=[pl.BlockSpec((None, m, cb), lambda j: (0, 0, j)), pl.BlockSpec((None, m, cb), lambda j: (1, 0, j)),
                  pl.BlockSpec((None, m, cb), lambda j: (2, 0, j)), pl.BlockSpec((SUBLANES, cb), lambda j: (0, j)),
                  pl.BlockSpec(memory_space=pl.ANY)],
        out_specs=pl.BlockSpec((m, cb), lambda j: (0, j)),
        out_shape=jax.ShapeDtypeStruct(y.shape, y.dtype),
        input_output_aliases={4: 0},
        compiler_params=_params(("parallel",)),
    )(pm, pm, pm, cw, y)


def _shortconv_bwd(dy, pm, cw, dpm, *, name):
    _, m, seg = pm.shape
    cb = LANES
    chunks = _row_chunks(m)

    def body(dy_ref, gi_ref, go_ref, ah_ref, w_ref, dpm_in, dp_ref, dw_ref, dc_s):
        del dpm_in
        w = w_ref[...]
        zeros8 = jnp.zeros((SUBLANES, cb), F32)
        dc_s[pl.ds(0, SUBLANES), :] = zeros8
        dc_s[pl.ds(m, SUBLANES), :] = zeros8
        for t in range(3):
            dp_ref[t, pl.ds(0, SUBLANES), :] = zeros8.astype(dp_ref.dtype)
        load_m = lambda a, b: gi_ref[pl.ds(a, b), :] * ah_ref[pl.ds(a, b), :]
        dw = jnp.zeros((SUBLANES, cb), F32)
        for s, n in chunks:
            d = dy_ref[pl.ds(s, n), :]
            dp_ref[1, pl.ds(s, n), :] = (d * _conv_at(load_m, w, 3, s, n)).astype(dp_ref.dtype)
            dc = d * go_ref[pl.ds(s, n), :]
            dc_s[pl.ds(s, n), :] = dc
            dw = dw + _dconv_w(load_m, dc, 3, s, n)
        dw_ref[...] = dw
        for s, n in chunks:
            dm = _conv_t_at(lambda a, b: dc_s[pl.ds(a, b), :], w, 3, s, n)
            dp_ref[0, pl.ds(s, n), :] = (dm * ah_ref[pl.ds(s, n), :]).astype(dp_ref.dtype)
            dp_ref[2, pl.ds(s, n), :] = (dm * gi_ref[pl.ds(s, n), :]).astype(dp_ref.dtype)

    return pl.pallas_call(
        body, name=name, grid=(seg // cb,),
        in_specs=[pl.BlockSpec((m, cb), lambda j: (0, j)), pl.BlockSpec((None, m, cb), lambda j: (0, 0, j)),
                  pl.BlockSpec((None, m, cb), lambda j: (1, 0, j)), pl.BlockSpec((None, m, cb), lambda j: (2, 0, j)),
                  pl.BlockSpec((SUBLANES, cb), lambda j: (0, j)), pl.BlockSpec(memory_space=pl.ANY)],
        out_specs=[pl.BlockSpec((3, m, cb), lambda j: (0, 0, j)), pl.BlockSpec((SUBLANES, cb), lambda j: (0, j))],
        out_shape=[jax.ShapeDtypeStruct(dpm.shape, dpm.dtype), jax.ShapeDtypeStruct((SUBLANES, seg), F32)],
        scratch_shapes=[pltpu.VMEM((m + SUBLANES, cb), F32)],
        input_output_aliases={5: 0},
        compiler_params=_params(("parallel",)),
    )(dy, pm, pm, pm, cw, dpm)


def _dnpre_fwd(pm, cw, *, name):
    _, m, seg = pm.shape
    cb = _pick(seg, (256, 128))
    per = seg // cb
    chunks = _row_chunks(m)

    def body(x_ref, w_ref, o_ref):
        w = w_ref[...]
        o_ref[pl.ds(0, SUBLANES), :] = jnp.zeros((SUBLANES, cb), F32)
        for s, n in chunks:
            c = _conv_at(lambda a, b: x_ref[pl.ds(a, b), :], w, 4, s, n)
            o_ref[pl.ds(s, n), :] = c * _sigmoid(c)

    return pl.pallas_call(
        body, name=name, grid=(3 * per,),
        in_specs=[pl.BlockSpec((None, m, cb), lambda j: (3 + j // per, 0, j % per)), pl.BlockSpec((SUBLANES, cb), lambda j: (0, j))],
        out_specs=pl.BlockSpec((None, m, cb), lambda j: (j // per, 0, j % per)),
        out_shape=jax.ShapeDtypeStruct((3, m, seg), F32),
        compiler_params=_params(("parallel",)),
    )(pm, cw)


def _dnpre_bwd(dqkv, pm, cw, dpm, *, name):
    _, m, seg = pm.shape
    cb = _pick(seg, (256, 128))
    per = seg // cb
    chunks = _row_chunks(m)

    def body(d_ref, x_ref, w_ref, dpm_in, dp_ref, dw_ref, dc_s):
        del dpm_in
        w = w_ref[...]
        zeros8 = jnp.zeros((SUBLANES, cb), F32)
        dc_s[pl.ds(0, SUBLANES), :] = zeros8
        dc_s[pl.ds(m, SUBLANES), :] = zeros8
        dp_ref[pl.ds(0, SUBLANES), :] = zeros8.astype(dp_ref.dtype)
        load_x = lambda a, b: x_ref[pl.ds(a, b), :]
        dw = jnp.zeros((SUBLANES, cb), F32)
        for s, n in chunks:
            c = _conv_at(load_x, w, 4, s, n)
            sg = _sigmoid(c)
            dc = d_ref[pl.ds(s, n), :] * (sg * (1.0 + c * (1.0 - sg)))
            dc_s[pl.ds(s, n), :] = dc
            dw = dw + _dconv_w(load_x, dc, 4, s, n)
        dw_ref[...] = dw
        for s, n in chunks:
            dp_ref[pl.ds(s, n), :] = _conv_t_at(lambda a, b: dc_s[pl.ds(a, b), :], w, 4, s, n).astype(dp_ref.dtype)

    return pl.pallas_call(
        body, name=name, grid=(3 * per,),
        in_specs=[pl.BlockSpec((None, m, cb), lambda j: (j // per, 0, j % per)),
                  pl.BlockSpec((None, m, cb), lambda j: (3 + j // per, 0, j % per)),
                  pl.BlockSpec((SUBLANES, cb), lambda j: (0, j)), pl.BlockSpec(memory_space=pl.ANY)],
        out_specs=[pl.BlockSpec((None, m, cb), lambda j: (3 + j // per, 0, j % per)), pl.BlockSpec((SUBLANES, cb), lambda j: (0, j))],
        out_shape=[jax.ShapeDtypeStruct(dpm.shape, dpm.dtype), jax.ShapeDtypeStruct((SUBLANES, 3 * seg), F32)],
        scratch_shapes=[pltpu.VMEM((m + SUBLANES, cb), F32)],
        input_output_aliases={3: 0},
        compiler_params=_params(("parallel",)),
    )(dqkv, pm, cw, dpm)


def _mxu_dot_impl(a, b, form):
    a = a.astype(MXU_DTYPE)
    b = b.astype(MXU_DTYPE)
    dims = {"nn": (((1,), (0,)), ((), ())), "nt": (((1,), (1,)), ((), ())), "tn": (((0,), (0,)), ((), ()))}[form]
    return lax.dot_general(a, b, dims, preferred_element_type=F32)


@functools.partial(jax.custom_vjp, nondiff_argnums=(2,))
def _mxu_dot(a, b, form):
    return _mxu_dot_impl(a, b, form)


def _mxu_dot_fwd(a, b, form):
    return _mxu_dot_impl(a, b, form), (a, b)


def _mxu_dot_bwd(form, saved, g):
    a, b = saved
    if form == "nn":
        return _mxu_dot_impl(g, b, "nt"), _mxu_dot_impl(a, g, "tn")
    if form == "nt":
        return _mxu_dot_impl(g, b, "nn"), _mxu_dot_impl(g, a, "tn")
    return _mxu_dot_impl(b, g, "nt"), _mxu_dot_impl(a, g, "nn")


_mxu_dot.defvjp(_mxu_dot_fwd, _mxu_dot_bwd)


_DOT_DIMS = {"nn": (((1,), (0,)), ((), ())), "nt": (((1,), (1,)), ((), ())), "tn": (((0,), (0,)), ((), ()))}


def _split(x):
    hi = x.astype(BF16)
    return hi, (x - hi.astype(F32)).astype(BF16)


def _dot3_impl(a, b, form):
    dg = lambda p, q: lax.dot_general(p, q, _DOT_DIMS[form], preferred_element_type=F32)
    ah, al = _split(a)
    bh, bl = _split(b)
    return dg(ah, bh) + (dg(ah, bl) + dg(al, bh))


@functools.partial(jax.custom_vjp, nondiff_argnums=(2,))
def _dot3(a, b, form):
    return _dot3_impl(a, b, form)


def _dot3_fwd(a, b, form):
    return _dot3_impl(a, b, form), (a, b)


def _dot3_bwd(form, saved, g):
    a, b = saved
    if form == "nn":
        return _dot3_impl(g, b, "nt"), _dot3_impl(a, g, "tn")
    if form == "nt":
        return _dot3_impl(g, b, "nn"), _dot3_impl(g, a, "tn")
    return _dot3_impl(b, g, "nt"), _dot3_impl(a, g, "nn")


_dot3.defvjp(_dot3_fwd, _dot3_bwd)


def _hdot(a, b):
    return _dot3(a, b, "nn")


def _mask_dot(mask, x, form):
    dg = lambda q: lax.dot_general(mask.astype(BF16), q, _DOT_DIMS[form], preferred_element_type=F32)
    x1 = x.astype(BF16)
    r1 = x - x1.astype(F32)
    x2 = r1.astype(BF16)
    x3 = (r1 - x2.astype(F32)).astype(BF16)
    return dg(x1) + (dg(x2) + dg(x3))


def _decay_masks(c):
    row = lax.broadcasted_iota(jnp.int32, (c, c), 0)
    col = lax.broadcasted_iota(jnp.int32, (c, c), 1)
    return (row >= col).astype(F32), row <= col


def _decay_impl(gb):
    lower, upper = _decay_masks(gb.shape[0])
    return _mask_dot(lower, gb, "nn"), _mask_dot(jnp.ones_like(gb), jnp.where(upper, gb, 0.0), "nn")


@jax.custom_vjp
def _decay_matrices(gb):
    return _decay_impl(gb)


def _decay_fwd(gb):
    return _decay_impl(gb), None


def _decay_bwd(_, cts):
    gc, gr = cts
    lower, upper = _decay_masks(gc.shape[0])
    return (_mask_dot(lower, gc, "tn") + jnp.where(upper, _mask_dot(jnp.ones_like(gr), gr, "tn"), 0.0),)


_decay_matrices.defvjp(_decay_fwd, _decay_bwd)


def _heads(f, *lists):
    return [f(*t) for t in zip(*lists)]


def _inverses_impl(a):
    c = a[0].shape[0]
    row = lax.broadcasted_iota(jnp.int32, (c, c), 0)
    col = lax.broadcasted_iota(jnp.int32, (c, c), 1)
    eye = jnp.where(row == col, 1.0, 0.0)
    x = _heads(lambda t: eye - t, a)
    p = _heads(lambda t: _dot3_impl(t, t, "nn"), a)
    power = 2
    while power < c:
        x = _heads(lambda s, t: s + _dot3_impl(s, t, "nn"), x, p)
        power *= 2
        if power < c:
            p = _heads(lambda t: _dot3_impl(t, t, "nn"), p)
    return x


@jax.custom_vjp
def _unit_lower_inverses(a):
    return _inverses_impl(a)


def _unit_lower_inverses_fwd(a):
    x = _inverses_impl(a)
    return x, x


def _unit_lower_inverses_bwd(x, g):
    t = _heads(lambda s, u: _dot3_impl(s, u, "tn"), x, g)
    return (_heads(lambda u, s: -_dot3_impl(u, s, "nt"), t, x),)


_unit_lower_inverses.defvjp(_unit_lower_inverses_fwd, _unit_lower_inverses_bwd)


def _softplus(x):
    return jnp.maximum(x, 0.0) + jnp.log(1.0 + jnp.exp(-jnp.abs(x)))


DN_STEP_CHUNKS = 3


def _dn_chunk(qr, kr, v, z, braw, araw, alog, dtb, nw, state, valid):
    c = DN_CHUNK
    nh = len(state)
    chunks = len(qr) // nh
    alog, dtb, valid_i = alog * chunks, dtb * chunks, [vv for vv in valid for _ in range(nh)]
    row = lax.broadcasted_iota(jnp.int32, (c, c), 0)
    col = lax.broadcasted_iota(jnp.int32, (c, c), 1)
    incl = row >= col
    strict = row > col
    q = _heads(lambda t: t * lax.rsqrt(jnp.sum(t * t, -1, keepdims=True) + EPS) * (DN_HEAD_DIM ** -0.5), qr)
    k = _heads(lambda t: t * lax.rsqrt(jnp.sum(t * t, -1, keepdims=True) + EPS), kr)
    beta = _heads(lambda t, vv: _sigmoid(t) * vv, braw, valid_i)
    g = _heads(lambda al, ar, dt, vv: -jnp.exp(al) * _softplus(ar + dt) * vv, alog, araw, dtb, valid_i)
    decay = _heads(lambda t: _decay_matrices(jnp.broadcast_to(t, (c, c))), g)
    dmask = _heads(lambda d: jnp.where(incl, jnp.exp(jnp.where(incl, d[0] - d[1], 0.0)), 0.0), decay)
    dec = _heads(lambda d: d[0][:, :1], decay)
    dlast = _heads(lambda d: d[0][c - 1:c, :1], decay)
    kk = _heads(lambda t: _mxu_dot(t, t, "nt"), k)
    a = _heads(lambda b, t, d: jnp.where(strict, b * t * d, 0.0), beta, kk, dmask)
    x = _unit_lower_inverses(a)
    uw = _heads(lambda s, vv, kk_, b, d: _hdot(s, jnp.concatenate([vv * b, kk_ * (b * jnp.exp(d))], 1)), x, v, k, beta, dec)
    u = _heads(lambda t: t[:, :DN_HEAD_DIM], uw)
    w = _heads(lambda t: t[:, DN_HEAD_DIM:], uw)
    qk = _heads(lambda s, t, d: _mxu_dot(s, t, "nt") * d, q, k, dmask)
    q_dec = _heads(lambda t, d: t * jnp.exp(d), q, dec)
    k_dec = _heads(lambda t, dl, d: t * jnp.exp(dl - d), k, dlast, dec)
    o = []
    for ci in range(chunks):
        of = lambda lst: lst[ci * nh:(ci + 1) * nh]
        v_new = _heads(lambda s, t, st: s - _mxu_dot(t, st, "nn"), of(u), of(w), state)
        o += _heads(lambda qd, st, s, vn: _mxu_dot(qd, st, "nn") + _mxu_dot(s, vn, "nn"), of(q_dec), state, of(qk), v_new)
        state = _heads(lambda st, dl, kd, vn: st * jnp.exp(dl) + _mxu_dot(kd, vn, "tn"), state, of(dlast), of(k_dec), v_new)
    y = _heads(lambda t, zz: _rms(t, nw) * (zz * _sigmoid(zz)), o, z)
    return y, state


DN_STEP_ROWS = DN_STEP_CHUNKS * DN_CHUNK
DN_ITEMS = [(ci, h) for ci in range(DN_STEP_CHUNKS) for h in range(DN_HEADS)]


def _dn_valid(n):
    rows = [n * DN_STEP_ROWS + ci * DN_CHUNK + lax.broadcasted_iota(jnp.int32, (DN_CHUNK, 1), 0) for ci in range(DN_STEP_CHUNKS)]
    return [(r >= PAD_ROWS).astype(F32) for r in rows]


def _dn_in_specs(rev, nc):
    cn = (lambda n: nc - 1 - n) if rev else (lambda n: n)
    c, hd = DN_STEP_ROWS, DN_HEAD_DIM
    return [
        pl.BlockSpec((None, c, DN_DIM), lambda n: (0, cn(n), 0)),
        pl.BlockSpec((None, c, DN_DIM), lambda n: (1, cn(n), 0)),
        pl.BlockSpec((None, c, DN_DIM), lambda n: (2, cn(n), 0)),
        pl.BlockSpec((None, c, DN_DIM), lambda n: (6, cn(n), 0)),
        pl.BlockSpec((DN_HEADS, 2, c, 1), lambda n: (0, 0, cn(n), 0)),
        pl.BlockSpec((DN_HEADS, SUBLANES, LANES), lambda n: (0, 0, 0)),
        pl.BlockSpec((1, hd), lambda n: (0, 0)),
    ]


def _rows(ci):
    return slice(ci * DN_CHUNK, (ci + 1) * DN_CHUNK)


def _cols(h):
    return slice(h * DN_HEAD_DIM, (h + 1) * DN_HEAD_DIM)


def _dn_load(q_ref, k_ref, v_ref, z_ref, ba_ref, hp_ref):
    heads = range(DN_HEADS)
    item = lambda ref: [ref[_rows(ci), _cols(h)] for ci, h in DN_ITEMS]
    return (item(q_ref), item(k_ref), item(v_ref), item(z_ref),
            [ba_ref[h, 0, _rows(ci), :] for ci, h in DN_ITEMS], [ba_ref[h, 1, _rows(ci), :] for ci, h in DN_ITEMS],
            [hp_ref[h, 0:1, 0:1] for h in heads], [hp_ref[h, 1:2, 0:1] for h in heads])


def _delta_fwd(qkvc, pm, ba, hp, nw, *, name):
    _, m, _ = qkvc.shape
    nc = m // DN_STEP_ROWS
    hd = DN_HEAD_DIM

    def body(q_ref, k_ref, v_ref, z_ref, ba_ref, hp_ref, nw_ref, y_ref, s_ref, state):
        n = pl.program_id(0)

        @pl.when(n == 0)
        def _():
            state[...] = jnp.zeros_like(state)

        heads = range(DN_HEADS)
        old = [state[h] for h in heads]
        y, new = _dn_chunk(*_dn_load(q_ref, k_ref, v_ref, z_ref, ba_ref, hp_ref), nw_ref[...], old, _dn_valid(n))
        for h in heads:
            s_ref[h] = old[h]
            state[h] = new[h]
        for (ci, h), yy in zip(DN_ITEMS, y):
            y_ref[_rows(ci), _cols(h)] = yy.astype(y_ref.dtype)

    return pl.pallas_call(
        body, name=name, grid=(nc,), in_specs=_dn_in_specs(False, nc),
        out_specs=[pl.BlockSpec((DN_STEP_ROWS, DN_DIM), lambda n: (n, 1)), pl.BlockSpec((DN_HEADS, None, hd, hd), lambda n: (0, n, 0, 0))],
        out_shape=[jax.ShapeDtypeStruct((m, D_CONV + DN_DIM), MXU_DTYPE), jax.ShapeDtypeStruct((DN_HEADS, nc, hd, hd), F32)],
        scratch_shapes=[pltpu.VMEM((DN_HEADS, hd, hd), F32)],
        compiler_params=_params(("arbitrary",)),
    )(qkvc, qkvc, qkvc, pm, ba, hp, nw)


def _delta_bwd(dy, qkvc, pm, ba, hp, nw, states, *, name):
    _, m, _ = qkvc.shape
    nc = m // DN_STEP_ROWS
    hd, c = DN_HEAD_DIM, DN_STEP_ROWS

    def body(q_ref, k_ref, v_ref, z_ref, ba_ref, hp_ref, nw_ref, s_ref, dy_ref,
             dz_ref, dqkv_ref, dba_ref, dhp_ref, dnw_ref, dstate):
        step = pl.program_id(0)
        n = nc - 1 - step

        @pl.when(step == 0)
        def _():
            dstate[...] = jnp.zeros_like(dstate)
            dhp_ref[...] = jnp.zeros_like(dhp_ref)
            dnw_ref[...] = jnp.zeros_like(dnw_ref)

        valid = _dn_valid(n)
        heads = range(DN_HEADS)
        fn = lambda *a: _dn_chunk(*a, valid)
        _, vjp = jax.vjp(fn, *_dn_load(q_ref, k_ref, v_ref, z_ref, ba_ref, hp_ref), nw_ref[...], [s_ref[h] for h in heads])
        dy = [dy_ref[_rows(ci), _cols(h)] for ci, h in DN_ITEMS]
        dq, dk, dv, dz, dbr, dar, dalog, ddtb, dnw, dst = vjp((dy, [dstate[h] for h in heads]))
        for i, (ci, h) in enumerate(DN_ITEMS):
            dqkv_ref[0, _rows(ci), _cols(h)] = dq[i]
            dqkv_ref[1, _rows(ci), _cols(h)] = dk[i]
            dqkv_ref[2, _rows(ci), _cols(h)] = dv[i]
            dz_ref[_rows(ci), _cols(h)] = dz[i].astype(dz_ref.dtype)
            dba_ref[h, 0, _rows(ci), :] = dbr[i]
            dba_ref[h, 1, _rows(ci), :] = dar[i]
        for h in heads:
            dstate[h] = dst[h]
            dhp_ref[h] += jnp.concatenate([jnp.broadcast_to(dalog[h], (1, LANES)), jnp.broadcast_to(ddtb[h], (1, LANES)),
                                           jnp.zeros((SUBLANES - 2, LANES), F32)], 0)
        dnw_ref[...] += dnw

    rn = lambda n: nc - 1 - n
    in_specs = _dn_in_specs(True, nc) + [
        pl.BlockSpec((DN_HEADS, None, hd, hd), lambda n: (0, rn(n), 0, 0)),
        pl.BlockSpec((c, DN_DIM), lambda n: (rn(n), 1)),
    ]
    out_specs = [
        pl.BlockSpec((None, c, DN_DIM), lambda n: (6, rn(n), 0)),
        pl.BlockSpec((3, c, DN_DIM), lambda n: (0, rn(n), 0)),
        pl.BlockSpec((DN_HEADS, 2, c, 1), lambda n: (0, 0, rn(n), 0)),
        pl.BlockSpec((DN_HEADS, SUBLANES, LANES), lambda n: (0, 0, 0)),
        pl.BlockSpec((1, hd), lambda n: (0, 0)),
    ]
    return pl.pallas_call(
        body, name=name, grid=(nc,), in_specs=in_specs, out_specs=out_specs,
        out_shape=[jax.ShapeDtypeStruct(pm.shape, MXU_DTYPE), jax.ShapeDtypeStruct(qkvc.shape, F32),
                   jax.ShapeDtypeStruct(ba.shape, F32), jax.ShapeDtypeStruct(hp.shape, F32),
                   jax.ShapeDtypeStruct((1, hd), F32)],
        scratch_shapes=[pltpu.VMEM((DN_HEADS, hd, hd), F32)],
        compiler_params=_params(("arbitrary",)),
    )(qkvc, qkvc, qkvc, pm, ba, hp, nw, states, dy)


SWA_PAIR = 2


def _attn_block(q, k0, kp, kc, v0, vp, vc, qw, kw, sink, n):
    g, b, hd = SWA_GROUP, SWA_BLOCK, SWA_HEAD_DIM
    pair = list(range(SWA_PAIR))
    lanes = lambda t, e: t[:, e * hd:(e + 1) * hd]
    q4 = [jnp.concatenate([lanes(q, e * g + i)[None] for i in range(g)], 0) for e in pair]
    qn = _heads(lambda t: _rms(t, qw) * (hd ** -0.5), q4)
    kn = [_rms(jnp.concatenate([lanes(k0, e), lanes(kp, e), lanes(kc, e)], 0), kw) for e in pair]
    vcat = [jnp.concatenate([lanes(v0, e), lanes(vp, e), lanes(vc, e)], 0) for e in pair]
    s = _heads(lambda a, k: _mxu_dot(a.reshape(g * b, hd), k, "nt").reshape(g, b, 3 * b), qn, kn)
    i = lax.broadcasted_iota(jnp.int32, (b, 3 * b), 0)
    c = lax.broadcasted_iota(jnp.int32, (b, 3 * b), 1)
    in_meta, in_prev, in_cur = c < b, (c >= b) & (c < 2 * b), c >= 2 * b
    j = c - jnp.where(in_meta, 0, jnp.where(in_prev, b, 2 * b))
    meta_lo = jnp.where(n == 0, b, PAD_ROWS)
    cur_lo = jnp.where(n == 0, PAD_ROWS, 0)
    prev_off = jnp.where(n >= 2, 0, 2 * b)
    valid = (in_meta & (j >= meta_lo)) | (in_prev & (j > i + prev_off)) | (in_cur & (j <= i) & (j >= cur_lo))
    s = _heads(lambda t: jnp.where(valid[None], t, NEG), s)
    m = [lax.stop_gradient(jnp.maximum(jnp.max(t, -1, keepdims=True), sink[e])) for e, t in zip(pair, s)]
    ex = _heads(lambda t, mm: jnp.exp(t - mm), s, m)
    p = [t / (jnp.sum(t, -1, keepdims=True) + jnp.exp(sink[e] - mm)) for e, t, mm in zip(pair, ex, m)]
    o = _heads(lambda t, v: _mxu_dot(t.reshape(g * b, 3 * b), v, "nn").reshape(g, b, hd), p, vcat)
    return jnp.concatenate([o[e][i] for e in pair for i in range(g)], 1)


Q_LANES = SWA_PAIR * SWA_GROUP * SWA_HEAD_DIM
KV_LANES = SWA_PAIR * SWA_HEAD_DIM
K_BLOCK0 = SWA_HEADS * SWA_HEAD_DIM // KV_LANES
V_BLOCK0 = K_BLOCK0 + SWA_KV_HEADS * SWA_HEAD_DIM // KV_LANES


def _attn_in_specs():
    g, b, hd = SWA_GROUP, SWA_BLOCK, SWA_HEAD_DIM
    kv = lambda f, first: pl.BlockSpec((b, KV_LANES), lambda p, n: (f(n), first + p))
    blocks = [lambda n: 0, lambda n: jnp.maximum(n - 1, 0), lambda n: n]
    return ([pl.BlockSpec((b, Q_LANES), lambda p, n: (n, p))] + [kv(f, K_BLOCK0) for f in blocks] + [kv(f, V_BLOCK0) for f in blocks]
            + [pl.BlockSpec((1, hd), lambda p, n: (0, 0)), pl.BlockSpec((1, hd), lambda p, n: (0, 0)),
               pl.BlockSpec((SWA_PAIR, g, 1, 1), lambda p, n: (p, 0, 0, 0))])


def _attn_fwd(qkv, qw, kw, sink, *, name):
    m = qkv.shape[0]
    b = SWA_BLOCK

    def body(q_ref, k0, kp, kc, v0, vp, vc, qw_ref, kw_ref, s_ref, o_ref):
        o_ref[...] = _attn_block(q_ref[...], k0[...], kp[...], kc[...], v0[...], vp[...], vc[...], qw_ref[...], kw_ref[...],
                                 s_ref[...], pl.program_id(1)).astype(o_ref.dtype)

    return pl.pallas_call(
        body, name=name, grid=(SWA_KV_HEADS // SWA_PAIR, m // b), in_specs=_attn_in_specs(),
        out_specs=pl.BlockSpec((b, Q_LANES), lambda p, n: (n, p)),
        out_shape=jax.ShapeDtypeStruct((m, SWA_HEADS * SWA_HEAD_DIM), MXU_DTYPE),
        compiler_params=_params(("parallel", "parallel")),
    )(*([qkv] * 7), qw, kw, sink)


def _attn_bwd(do, qkv, qw, kw, sink, *, name):
    m = qkv.shape[0]
    g, b, hd = SWA_GROUP, SWA_BLOCK, SWA_HEAD_DIM

    def body(q_ref, k0, kp, kc, v0, vp, vc, qw_ref, kw_ref, s_ref, do_ref, dq_ref, dk_ref, dv_ref, dqw_ref, dkw_ref, ds_ref):
        n = pl.program_id(1)

        @pl.when(n == 0)
        def _():
            for r in (dk_ref, dv_ref, dqw_ref, dkw_ref, ds_ref):
                r[...] = jnp.zeros_like(r)

        fn = lambda *a: _attn_block(*a, n)
        _, vjp = jax.vjp(fn, q_ref[...], k0[...], kp[...], kc[...], v0[...], vp[...], vc[...], qw_ref[...], kw_ref[...], s_ref[...])
        dq, dk0, dkp, dkc, dv0, dvp, dvc, dqw, dkw, dsk = vjp(do_ref[...])
        dq_ref[...] = dq
        prev = pl.multiple_of(jnp.maximum(n - 1, 0) * b, b)
        cur = pl.multiple_of(n * b, b)
        for ref, parts in ((dk_ref, (dk0, dkp, dkc)), (dv_ref, (dv0, dvp, dvc))):
            ref[pl.ds(0, b), :] += parts[0]
            ref[pl.ds(prev, b), :] += parts[1]
            ref[pl.ds(cur, b), :] += parts[2]
        dqw_ref[...] += dqw
        dkw_ref[...] += dkw
        ds_ref[...] += dsk

    pairs = SWA_KV_HEADS // SWA_PAIR
    kv_acc = pl.BlockSpec((m, KV_LANES), lambda p, n: (0, p))
    w_acc = pl.BlockSpec((None, 1, hd), lambda p, n: (p, 0, 0))
    kv_shape = jax.ShapeDtypeStruct((m, SWA_KV_HEADS * hd), F32)
    return pl.pallas_call(
        body, name=name, grid=(pairs, m // b),
        in_specs=_attn_in_specs() + [pl.BlockSpec((b, Q_LANES), lambda p, n: (n, p))],
        out_specs=[pl.BlockSpec((b, Q_LANES), lambda p, n: (n, p)), kv_acc, kv_acc, w_acc, w_acc,
                   pl.BlockSpec((SWA_PAIR, g, 1, 1), lambda p, n: (p, 0, 0, 0))],
        out_shape=[jax.ShapeDtypeStruct((m, SWA_HEADS * hd), F32), kv_shape, kv_shape,
                   jax.ShapeDtypeStruct((pairs, 1, hd), F32), jax.ShapeDtypeStruct((pairs, 1, hd), F32),
                   jax.ShapeDtypeStruct(sink.shape, F32)],
        compiler_params=_params(("parallel", "arbitrary")),
    )(*([qkv] * 7), qw, kw, sink, do)


def _loss_bwd(h, target, *, name):
    m, d = h.shape
    b = SWA_BLOCK

    def body(h_ref, t_ref, l_ref, dh_ref):
        i = pl.program_id(0)

        @pl.when(i == 0)
        def _():
            l_ref[...] = jnp.zeros_like(l_ref)
            dh_ref[...] = jnp.zeros_like(dh_ref)

        @pl.when(i > 0)
        def _():
            e = h_ref[...] - t_ref[...]
            dh_ref[...] = e * (1.0 / d)
            l_ref[...] += jnp.sum(jnp.sum(e * e, 0, keepdims=True), 1, keepdims=True) * (0.5 / d)

    return pl.pallas_call(
        body, name=name, grid=(m // b,),
        in_specs=[pl.BlockSpec((b, d), lambda i: (i, 0)), pl.BlockSpec((b, d), lambda i: (jnp.maximum(i - 1, 0), 0))],
        out_specs=[pl.BlockSpec((1, LANES), lambda i: (0, 0)), pl.BlockSpec((b, d), lambda i: (i, 0))],
        out_shape=[jax.ShapeDtypeStruct((1, LANES), F32), jax.ShapeDtypeStruct((m, d), F32)],
        compiler_params=_params(("arbitrary",)),
    )(h, target)


def _ffn_fwd(h, nw, w_up_t, cw, w_down, tag):
    u, hn = _norm_matmul(h, nw, w_up_t, o_seg=D_FF, trans_w=True, name=f"ffn_up_{tag}")
    a = _ffn_act_fwd(u, cw, name=f"ffn_act_{tag}")
    return _mm_nn(a, w_down, res=h, name=f"ffn_down_{tag}"), (h, hn, u, a)


def _ffn_bwd(dh, saved, nw, w_up_t, cw, w_down, tag):
    h, hn, u, a = saved
    da = _mm_nn(dh, w_down, trans_w=True, name=f"ffn_da_{tag}")
    dw_down = _mm_tn(a, dh, name=f"ffn_dwdown_{tag}")
    du, dcw = _ffn_act_bwd(da, u, cw, name=f"ffn_act_bwd_{tag}")
    dhn = _mm_nn(du, w_up_t, a_seg=True, name=f"ffn_dhn_{tag}")
    dw_up_t = _mm_tn(du, hn, a_seg=True, name=f"ffn_dwup_{tag}")
    dh_in, dnw = _rmsnorm_bwd(dhn, h, nw, dh, name=f"ffn_norm_bwd_{tag}")
    return dh_in, dnw, dw_up_t, dcw, dw_down


def _local_step(x, target, w, fetch=None, push=None):
    fetch = fetch or (lambda stage, after: {})
    push = push or (lambda stage, grads: None)
    plus = lambda a, zero: a if zero is None else a + zero
    seq, d = x.shape
    m = PAD_ROWS + N_META + seq
    h0 = jnp.concatenate([jnp.zeros((PAD_ROWS, d), F32), w["meta"], x], 0)

    pm, hn0 = _norm_matmul(h0, w["anw"][0], w["w_in_t"], o_seg=SEG, trans_w=True, n=N_SEG * SEG, name="mix_in")
    pba = _mm_nn(hn0, w["w_in_tail_t"], trans_w=True, name="mix_in_tail")
    qkvc = _dnpre_fwd(pm, w["dcw"], name="dn_conv")
    ba = pba[:, :2 * DN_HEADS].T.reshape(2, DN_HEADS, m, 1).transpose(1, 0, 2, 3)
    y, states = _delta_fwd(qkvc, pm, ba, w["hp"], w["dnw"], name="delta")
    y = _shortconv_fwd(pm, w["caw"], y, name="shortconv")
    w = {**w, **fetch("l0", y)}
    h1 = _mm_nn(y, w["w_out"], res=h0, name="mix_out")
    h2, ffn0 = _ffn_fwd(h1, w["fnw"][0], w["w_up0_t"], w["fcw"][0], w["w_down0"], "l0")

    w = {**w, **fetch("l1", h2)}
    qkv, hn2 = _norm_matmul(h2, w["anw"][1], w["wqkv"], name="attn_qkv")
    o = _attn_fwd(qkv, w["qnw"], w["knw"], w["sink"], name="attn")
    h3 = _mm_nn(o, w["wo"], res=h2, name="attn_out")
    h4, ffn1 = _ffn_fwd(h3, w["fnw"][1], w["w_up1_t"], w["fcw"][1], w["w_down1"], "l1")

    loss, dh4 = _loss_bwd(h4, target, name="loss")

    g = {}
    dh3, dfnw1, dwup1, dfcw1, dwdown1 = _ffn_bwd(dh4, ffn1, w["fnw"][1], w["w_up1_t"], w["fcw"][1], w["w_down1"], "l1")

    do = _mm_nn(dh3, w["wo"], trans_w=True, name="attn_do")
    g["wo"] = _mm_tn(o, dh3, name="attn_dwo")
    dq, dk, dv, dqw, dkw, dsink = _attn_bwd(do, qkv, w["qnw"], w["knw"], w["sink"], name="attn_bwd")
    dqkv = jnp.concatenate([dq, dk, dv], 1).astype(MXU_DTYPE)
    dhn2 = _mm_nn(dqkv, w["wqkv"], trans_w=True, name="attn_dhn")
    g["wqkv"] = _mm_tn(hn2, dqkv, name="attn_dwqkv")
    zero = push("l1", dict(w_up_t=dwup1, w_down=dwdown1, wo=g["wo"], wqkv=g["wqkv"]))
    dh2, danw1 = _rmsnorm_bwd(dhn2, h2, plus(w["anw"][1], zero), dh3, name="attn_norm_bwd")

    dh1, dfnw0, dwup0, dfcw0, dwdown0 = _ffn_bwd(dh2, ffn0, w["fnw"][0], w["w_up0_t"], w["fcw"][0], w["w_down0"], "l0")

    dy = _mm_nn(dh1, w["w_out"], trans_w=True, name="mix_dy")
    g["w_out"] = _mm_tn(y, dh1, name="mix_dwout")
    zero = push("l0", dict(w_up_t=dwup0, w_down=dwdown0, w_out=g["w_out"]))
    dpm, dqkvc, dba, dhp, ddnw = _delta_bwd(dy, qkvc, pm, ba, w["hp"], plus(w["dnw"], zero), states, name="delta_bwd")
    dpm, ddcw = _dnpre_bwd(dqkvc, pm, w["dcw"], dpm, name="dn_conv_bwd")
    dpm, dcaw = _shortconv_bwd(dy, pm, w["caw"], dpm, name="shortconv_bwd")
    dpba = jnp.pad(dba.transpose(1, 0, 2, 3).reshape(2 * DN_HEADS, m).T, ((0, 0), (0, LANES - 2 * DN_HEADS))).astype(MXU_DTYPE)
    g["w_in_t"] = jnp.concatenate([_mm_tn(dpm, hn0, a_seg=True, out_dtype=F32, name="mix_dwin"),
                                   _mm_tn(dpba, hn0, out_dtype=F32, name="mix_dwin_tail")[:N_TAIL]], 0).astype(GRAD_WIRE_DTYPE)
    zero = push("in", dict(w_in_t=g["w_in_t"]))
    dhn0 = _mm_nn(dpba, plus(w["w_in_tail_t"], None if zero is None else zero.astype(MXU_DTYPE)), name="mix_dhn_tail")
    dhn0 = _mm_nn(dpm, w["w_in_t"], res=dhn0, a_seg=True, name="mix_dhn")
    dh0, danw0 = _rmsnorm_bwd(dhn0, h0, w["anw"][0], dh1, name="mix_norm_bwd")

    g.update(
        x=dh0[PAD_ROWS + N_META:], meta=dh0[PAD_ROWS:PAD_ROWS + N_META], anw=[danw0, danw1], fnw=[dfnw0, dfnw1],
        caw=dcaw, dcw=ddcw, hp=dhp, dnw=ddnw, qnw=jnp.sum(dqw, 0), knw=jnp.sum(dkw, 0), sink=dsink,
        w_up_t=[dwup0, dwup1], fcw=[dfcw0, dfcw1], w_down=[dwdown0, dwdown1])
    return loss, g


N_TAIL = 2 * DN_HEADS


def _prepare_early(p):
    n_main = N_SEG * SEG
    w_in_t = p["mix_w_in_t"]
    tail_t = jnp.pad(w_in_t[n_main:], ((0, LANES - N_TAIL), (0, 0)))
    hp = jnp.zeros((DN_HEADS, SUBLANES, LANES), F32)
    hp = hp.at[:, 0, :].set(p["dn_a_log"][0][:, None]).at[:, 1, :].set(p["dn_dt_bias"][0][:, None])
    depth = p["ffn_conv_w"].shape[0]
    return dict(
        meta=p["meta_tokens"], anw=[p["attn_norm_w"][i:i + 1] for i in range(depth)],
        fnw=[p["ffn_norm_w"][i:i + 1] for i in range(depth)],
        w_in_t=w_in_t, w_in_tail_t=tail_t,
        caw=_pad_w(p["conv_a_w"][0]), dcw=_pad_w(p["dn_conv_w"][0]), hp=hp, dnw=p["dn_norm_w"],
        qnw=p["swa_q_norm_w"], knw=p["swa_k_norm_w"], sink=p["swa_sinks"].reshape(SWA_KV_HEADS, SWA_GROUP, 1, 1),
        fcw=[_pad_w(p["ffn_conv_w"][i]) for i in range(depth)])


def _prepare_weights(p):
    return dict(
        _prepare_early(dict(p, mix_w_in_t=p["mix_w_in"][0].T)), w_out=p["mix_w_out"][0], wo=p["swa_wo"][0],
        wqkv=jnp.concatenate([p["swa_wq"][0], p["swa_wk"][0], p["swa_wv"][0]], 1),
        w_up0_t=p["ffn_w_up"][0].T, w_up1_t=p["ffn_w_up"][1].T, w_down0=p["ffn_w_down"][0], w_down1=p["ffn_w_down"][1])


def _small_named(g):
    return dict(
        meta_tokens=g["meta"], attn_norm_w=jnp.concatenate(g["anw"], 0), ffn_norm_w=jnp.concatenate(g["fnw"], 0),
        conv_a_w=g["caw"][None, :3], dn_conv_w=g["dcw"][None, :4],
        dn_a_log=g["hp"][None, :, 0, 0], dn_dt_bias=g["hp"][None, :, 1, 0], dn_norm_w=g["dnw"],
        swa_q_norm_w=g["qnw"], swa_k_norm_w=g["knw"], swa_sinks=g["sink"].reshape(1, SWA_HEADS),
        ffn_conv_w=jnp.stack([c[:3] for c in g["fcw"]]))


def _reference_named(g):
    nq, nkv = SWA_HEADS * SWA_HEAD_DIM, SWA_KV_HEADS * SWA_HEAD_DIM
    return dict(
        _small_named(g), mix_w_in=g["w_in_t"].T[None],
        mix_w_out=g["w_out"][None], swa_wq=g["wqkv"][None, :, :nq], swa_wk=g["wqkv"][None, :, nq:nq + nkv],
        swa_wv=g["wqkv"][None, :, nq + nkv:], swa_wo=g["wo"][None],
        ffn_w_up=jnp.stack([t.T for t in g["w_up_t"]]), ffn_w_down=jnp.stack(g["w_down"]))


def _my_index():
    return 4 * lax.axis_index("x") + 2 * lax.axis_index("y") + lax.axis_index("c")


def _all_gather(arrays, *, name):
    n = len(arrays)

    def body(*refs):
        ins, outs = refs[:n], refs[n:2 * n]
        send_sems, recv_sems, local_sems = refs[2 * n:]
        x, y, c = lax.axis_index("x"), lax.axis_index("y"), lax.axis_index("c")
        me, sibling = (x, y, c), (x, y, 1 - c)
        chips = [(1 - x, y), (x, 1 - y), (1 - x, 1 - y)]

        def copy(i, k, block, to, src=None):
            rows = outs[i].at[4 * block[0] + 2 * block[1] + block[2]]
            return pltpu.make_async_remote_copy(
                src_ref=rows if src is None else src, dst_ref=rows, send_sem=send_sems.at[i, k], recv_sem=recv_sems.at[i, k],
                device_id=to, device_id_type=pl.DeviceIdType.MESH)

        mine = [pltpu.make_async_copy(ins[i], outs[i].at[4 * x + 2 * y + c], local_sems.at[i]) for i in range(n)]
        first = []
        for j, chip in enumerate(chips):
            first += [copy(i, 1 + j, me, (*chip, c), src=ins[i]) for i in range(n)]
        first += [copy(i, 0, me, sibling, src=ins[i]) for i in range(n)]
        for cp in first + mine:
            cp.start()
        passed = []
        for j, chip in enumerate(chips):
            for i in range(n):
                copy(i, 1 + j, (*chip, c), me).wait_recv()
                fwd = copy(i, 4 + j, (*chip, c), sibling)
                fwd.start()
                passed.append(fwd)
        for i in range(n):
            copy(i, 0, sibling, me).wait_recv()
            for j, chip in enumerate(chips):
                copy(i, 4 + j, (*chip, 1 - c), me).wait_recv()
        for cp in first + passed:
            cp.wait_send()
        for cp in mine:
            cp.wait()

    hbm = pl.BlockSpec(memory_space=pl.ANY)
    return pl.pallas_call(
        body, name=name, in_specs=[hbm] * n, out_specs=[hbm] * n,
        out_shape=[jax.ShapeDtypeStruct((N_DEV,) + tuple(a.shape), a.dtype) for a in arrays],
        scratch_shapes=[pltpu.SemaphoreType.DMA((n, 7)), pltpu.SemaphoreType.DMA((n, 7)), pltpu.SemaphoreType.DMA((n,))],
    )(*arrays)


def _peer(d):
    px, py, pc = lax.axis_index("x") ^ (d >> 2), lax.axis_index("y") ^ ((d >> 1) & 1), lax.axis_index("c") ^ (d & 1)
    return (px, py, pc), 4 * px + 2 * py + pc


def _push_copies(mode, srcs, lands, send_sems, recv_sems):
    me = _my_index()
    out = []
    for d in range(1, N_DEV):
        pos, idx = _peer(d)
        for i in range(len(srcs)):
            out.append(pltpu.make_async_remote_copy(
                src_ref=srcs[i] if mode == "gather" else srcs[i].at[idx], dst_ref=lands[i].at[me],
                send_sem=send_sems.at[i * N_DEV + d], recv_sem=recv_sems.at[i * N_DEV + d], device_id=pos,
                device_id_type=pl.DeviceIdType.MESH))
    return out


_HBM = pl.BlockSpec(memory_space=pltpu.HBM)
_SEM = pl.BlockSpec(memory_space=pltpu.SEMAPHORE)


def _push_start(mode, arrays, follows, *, name):
    n = len(arrays)
    blocks = [a.shape if mode == "gather" else a.shape[1:] for a in arrays]
    lands = [lax.empty((N_DEV,) + tuple(b), a.dtype) for a, b in zip(arrays, blocks)]

    def body(*refs):
        srcs, land_refs = refs[:n], refs[n:2 * n]
        send_sems, recv_sems = refs[2 * n + 1], refs[2 * n + 2]
        zero = refs[-1]
        for cp in _push_copies(mode, srcs, land_refs, send_sems, recv_sems):
            cp.start()
        zero[...] = jnp.zeros_like(zero)

    hbm_in = [pltpu.with_memory_space_constraint(a, pltpu.HBM) for a in list(arrays) + lands]
    outs = pl.pallas_call(
        body, name=name,
        out_shape=[pltpu.SemaphoreType.DMA((n * N_DEV,)), pltpu.SemaphoreType.DMA((n * N_DEV,))]
        + [pltpu.HBM(a.shape, a.dtype) for a in hbm_in] + [jax.ShapeDtypeStruct((SUBLANES, LANES), F32)],
        in_specs=[_HBM] * (2 * n) + [pl.BlockSpec(memory_space=pl.ANY)],
        out_specs=[_SEM, _SEM] + [_HBM] * (2 * n) + [pl.BlockSpec(memory_space=pltpu.VMEM)],
        input_output_aliases={i: 2 + i for i in range(2 * n)},
        compiler_params=pltpu.CompilerParams(has_side_effects=pltpu.SideEffectType.DATAFLOW_SIDE_EFFECTING),
    )(*hbm_in, follows)
    return dict(mode=mode, sems=outs[:2], srcs=outs[2:2 + n], lands=outs[2 + n:2 + 2 * n], zero=outs[-1])


def _push_wait(push, follows, *, name):
    n = len(push["srcs"])
    mode = push["mode"]

    def body(*refs):
        srcs, land_refs = refs[:n], refs[n:2 * n]
        send_sems, recv_sems = refs[2 * n], refs[2 * n + 1]
        for cp in _push_copies(mode, srcs, land_refs, send_sems, recv_sems):
            cp.wait_send()
            cp.wait_recv()

    args = list(push["srcs"]) + list(push["lands"])
    outs = pl.pallas_call(
        body, name=name, out_shape=[pltpu.HBM(a.shape, a.dtype) for a in args],
        in_specs=[_HBM] * (2 * n) + [_SEM, _SEM, pl.BlockSpec(memory_space=pl.ANY)], out_specs=[_HBM] * (2 * n),
        input_output_aliases={i: i for i in range(2 * n)},
        compiler_params=pltpu.CompilerParams(has_side_effects=pltpu.SideEffectType.DATAFLOW_SIDE_EFFECTING),
    )(*args, *push["sems"], follows)
    me = _my_index()
    got = []
    for src, land in zip(outs[:n], outs[n:]):
        own = src if mode == "gather" else lax.dynamic_index_in_dim(src, me, 0, keepdims=False)
        got.append(lax.dynamic_update_index_in_dim(land, own, me, 0))
    return got


ADAMW_BLOCK_BYTES = 6 * 1024 * 1024


def _adamw_tile(r, c):
    fits = lambda tr, tc: N_DEV * tr * tc * 4 <= ADAMW_BLOCK_BYTES
    rows = [t for t in range(2 * SUBLANES, r + 1, 2 * SUBLANES) if r % t == 0 and fits(t, c)]
    if rows or fits(r, c):
        return (max(rows) if rows else r), c
    cols = [t for t in range(LANES, c + 1, LANES) if c % t == 0 and fits(r, t)]
    return r, max(cols)


def _adamw(parts, w, m, v, layer, outs=None, *, name):
    nl, r, c = w.shape
    tr, tc = _adamw_tile(r, c)

    def body(p_ref, w_ref, m_ref, v_ref, *rest):
        g_ref, d_ref, nm_ref, nv_ref = rest[-4:]
        g = p_ref[0].astype(F32)
        for j in range(1, N_DEV):
            g = g + p_ref[j].astype(F32)
        m2 = ADAM_B1 * m_ref[...] + (1.0 - ADAM_B1) * g
        v2 = ADAM_B2 * v_ref[...] + (1.0 - ADAM_B2) * jnp.square(g)
        m_hat = m2 / (1.0 - ADAM_B1 ** ADAM_STEP)
        v_hat = v2 / (1.0 - ADAM_B2 ** ADAM_STEP)
        g_ref[...] = g
        d_ref[...] = -ADAM_LR * (m_hat / (jnp.sqrt(v_hat) + ADAM_EPS) + ADAM_WD * w_ref[...])
        nm_ref[...] = m2
        nv_ref[...] = v2

    blk = pl.BlockSpec((None, tr, tc), lambda i, j: (layer, i, j))
    out = jax.ShapeDtypeStruct((nl, r, c), F32)
    given = list(outs) if outs is not None else []
    return pl.pallas_call(
        body, name=name, grid=(r // tr, c // tc),
        in_specs=[pl.BlockSpec((N_DEV, tr, tc), lambda i, j: (0, i, j)), blk, blk, blk] + [pl.BlockSpec(memory_space=pl.ANY)] * len(given),
        out_specs=[blk, blk, blk, blk], out_shape=[out, out, out, out],
        input_output_aliases={4 + t: t for t in range(len(given))},
        compiler_params=_params(("parallel", "parallel")),
    )(parts, w, m, v, *given)


SHARD_AXIS = dict(
    meta_tokens=1, attn_norm_w=None, ffn_norm_w=None, mix_w_in=2, conv_a_w=2, dn_conv_w=2, dn_a_log=None, dn_dt_bias=None,
    dn_norm_w=None, mix_w_out=1, swa_wq=1, swa_wk=1, swa_wv=1, swa_q_norm_w=None, swa_k_norm_w=None, swa_sinks=None,
    swa_wo=1, ffn_w_up=2, ffn_conv_w=2, ffn_w_down=1)
WEIGHTS = list(SHARD_AXIS)
BIG = ["mix_w_in", "mix_w_out", "swa_wq", "swa_wk", "swa_wv", "swa_wo", "ffn_w_up", "ffn_w_down"]
SMALL = [k for k in WEIGHTS if k not in BIG]
SMALL_SHARDED = [k for k in SMALL if SHARD_AXIS[k] is not None]


def _whole(g8, axis):
    t = jnp.moveaxis(g8, 0, axis)
    return t.reshape(t.shape[:axis] + (t.shape[axis] * t.shape[axis + 1],) + t.shape[axis + 2:])


def _by_owner(a, axis):
    s = a.shape[axis] // N_DEV
    return jnp.moveaxis(a.reshape(a.shape[:axis] + (N_DEV, s) + a.shape[axis + 1:]), axis, 0)


def _pack(arrays, lead=0):
    flat = jnp.concatenate([a.reshape(a.shape[:lead] + (-1,)) for a in arrays], -1)
    n = flat.shape[-1]
    rows = -(-n // (SUBLANES * LANES)) * SUBLANES
    flat = jnp.pad(flat, [(0, 0)] * lead + [(0, rows * LANES - n)])
    return flat.reshape(flat.shape[:lead] + (rows, LANES))


def _unpack(buf, shapes, lead=0):
    flat = buf.reshape(buf.shape[:lead] + (-1,))
    out, o = [], 0
    for s in shapes:
        n = 1
        for e in s:
            n *= e
        out.append(flat[..., o:o + n].reshape(buf.shape[:lead] + tuple(s)))
        o += n
    return out


def kernel(x, meta_tokens, attn_norm_w, ffn_norm_w, mix_w_in, conv_a_w, dn_conv_w, dn_a_log, dn_dt_bias, dn_norm_w, mix_w_out, swa_wq, swa_wk, swa_wv, swa_q_norm_w, swa_k_norm_w, swa_sinks, swa_wo, ffn_w_up, ffn_conv_w, ffn_w_down, loss_target, m_meta_tokens, m_attn_norm_w, m_ffn_norm_w, m_mix_w_in, m_conv_a_w, m_dn_conv_w, m_dn_a_log, m_dn_dt_bias, m_dn_norm_w, m_mix_w_out, m_swa_wq, m_swa_wk, m_swa_wv, m_swa_q_norm_w, m_swa_k_norm_w, m_swa_sinks, m_swa_wo, m_ffn_w_up, m_ffn_conv_w, m_ffn_w_down, v_meta_tokens, v_attn_norm_w, v_ffn_norm_w, v_mix_w_in, v_conv_a_w, v_dn_conv_w, v_dn_a_log, v_dn_dt_bias, v_dn_norm_w, v_mix_w_out, v_swa_wq, v_swa_wk, v_swa_wv, v_swa_q_norm_w, v_swa_k_norm_w, v_swa_sinks, v_swa_wo, v_ffn_w_up, v_ffn_conv_w, v_ffn_w_down):
    w = dict(meta_tokens=meta_tokens, attn_norm_w=attn_norm_w, ffn_norm_w=ffn_norm_w, mix_w_in=mix_w_in, conv_a_w=conv_a_w, dn_conv_w=dn_conv_w, dn_a_log=dn_a_log, dn_dt_bias=dn_dt_bias, dn_norm_w=dn_norm_w, mix_w_out=mix_w_out, swa_wq=swa_wq, swa_wk=swa_wk, swa_wv=swa_wv, swa_q_norm_w=swa_q_norm_w, swa_k_norm_w=swa_k_norm_w, swa_sinks=swa_sinks, swa_wo=swa_wo, ffn_w_up=ffn_w_up, ffn_conv_w=ffn_conv_w, ffn_w_down=ffn_w_down)
    mom = dict(meta_tokens=m_meta_tokens, attn_norm_w=m_attn_norm_w, ffn_norm_w=m_ffn_norm_w, mix_w_in=m_mix_w_in, conv_a_w=m_conv_a_w, dn_conv_w=m_dn_conv_w, dn_a_log=m_dn_a_log, dn_dt_bias=m_dn_dt_bias, dn_norm_w=m_dn_norm_w, mix_w_out=m_mix_w_out, swa_wq=m_swa_wq, swa_wk=m_swa_wk, swa_wv=m_swa_wv, swa_q_norm_w=m_swa_q_norm_w, swa_k_norm_w=m_swa_k_norm_w, swa_sinks=m_swa_sinks, swa_wo=m_swa_wo, ffn_w_up=m_ffn_w_up, ffn_conv_w=m_ffn_conv_w, ffn_w_down=m_ffn_w_down)
    var = dict(meta_tokens=v_meta_tokens, attn_norm_w=v_attn_norm_w, ffn_norm_w=v_ffn_norm_w, mix_w_in=v_mix_w_in, conv_a_w=v_conv_a_w, dn_conv_w=v_dn_conv_w, dn_a_log=v_dn_a_log, dn_dt_bias=v_dn_dt_bias, dn_norm_w=v_dn_norm_w, mix_w_out=v_mix_w_out, swa_wq=v_swa_wq, swa_wk=v_swa_wk, swa_wv=v_swa_wv, swa_q_norm_w=v_swa_q_norm_w, swa_k_norm_w=v_swa_k_norm_w, swa_sinks=v_swa_sinks, swa_wo=v_swa_wo, ffn_w_up=v_ffn_w_up, ffn_conv_w=v_ffn_conv_w, ffn_w_down=v_ffn_w_down)
    me = _my_index()

    transposed = ("mix_w_in", "ffn_w_up")
    view = lambda k, a: jnp.swapaxes(a, 1, 2) if k in transposed else a
    axis2d = {k: 0 if k in transposed else SHARD_AXIS[k] - 1 for k in BIG}
    shard16 = {k: view(k, w[k]).astype(MXU_DTYPE) for k in BIG}
    small_shard_shapes = [w[k].shape for k in SMALL_SHARDED]
    rows_in = shard16["mix_w_in"].shape[1]
    sent_in = jnp.pad(shard16["mix_w_in"][0], ((0, -rows_in % (2 * SUBLANES)), (0, 0)))
    got = _all_gather([sent_in, _pack([w[k] for k in SMALL_SHARDED])], name="gather_weights")
    whole = {"mix_w_in_t": _whole(got[0][:, :rows_in], 0)}
    for k, a in zip(SMALL_SHARDED, _unpack(got[1], small_shard_shapes, lead=1)):
        whole[k] = _whole(a, SHARD_AXIS[k])
    for k in SMALL:
        whole.setdefault(k, w[k])
    stages = {"in": [("mix_w_in", 0)], "l0": [("mix_w_out", 0), ("ffn_w_up", 0), ("ffn_w_down", 0)],
              "l1": [("swa_wq", 0), ("swa_wk", 0), ("swa_wv", 0), ("swa_wo", 0), ("ffn_w_up", 1), ("ffn_w_down", 1)]}
    pushed = {}
    follows = got[1]
    for stage in ("l0", "l1"):
        pushed[stage] = _push_start("gather", [shard16[k][l] for k, l in stages[stage]], follows, name=f"push_weights_{stage}")
        follows = pushed[stage]["zero"]
    early = _prepare_early(whole)
    early["anw"][0] = early["anw"][0] + follows[0, 0]

    def fetch(stage, after):
        got = _push_wait(pushed[stage], after, name=f"wait_weights_{stage}")
        full = {kl: _whole(a, axis2d[kl[0]]) for kl, a in zip(stages[stage], got)}
        if stage == "l0":
            return dict(w_out=full["mix_w_out", 0], w_up0_t=full["ffn_w_up", 0], w_down0=full["ffn_w_down", 0])
        wqkv = jnp.concatenate([full["swa_wq", 0], full["swa_wk", 0], full["swa_wv", 0]], 1)
        return dict(wqkv=wqkv, wo=full["swa_wo", 0], w_up1_t=full["ffn_w_up", 1], w_down1=full["ffn_w_down", 1])

    nq, nkv = SWA_HEADS * SWA_HEAD_DIM, SWA_KV_HEADS * SWA_HEAD_DIM
    grad_pushes = {}

    def push(stage, gd):
        if stage == "in":
            named = {("mix_w_in", 0): gd["w_in_t"]}
        elif stage == "l1":
            named = {("swa_wq", 0): gd["wqkv"][:, :nq], ("swa_wk", 0): gd["wqkv"][:, nq:nq + nkv],
                     ("swa_wv", 0): gd["wqkv"][:, nq + nkv:], ("swa_wo", 0): gd["wo"],
                     ("ffn_w_up", 1): gd["w_up_t"], ("ffn_w_down", 1): gd["w_down"]}
        else:
            named = {("mix_w_out", 0): gd["w_out"], ("ffn_w_up", 0): gd["w_up_t"], ("ffn_w_down", 0): gd["w_down"]}
        sent = [_by_owner(named[kl], axis2d[kl[0]]) for kl in stages[stage]]
        grad_pushes[stage] = _push_start("scatter", sent, jnp.zeros((SUBLANES, LANES), F32), name=f"push_grads_{stage}")
        return grad_pushes[stage]["zero"][0, 0]

    loss, g = _local_step(x[0], loss_target[0], early, fetch, push)
    grads = _small_named(g)

    results = {}

    def update(stage, follows):
        got = _push_wait(grad_pushes[stage], follows, name=f"wait_grads_{stage}")
        for (k, l), parts in zip(stages[stage], got):
            w3, m3, v3 = view(k, w[k]), view(k, mom[k]), view(k, var[k])
            results[k] = _adamw(parts.reshape((N_DEV,) + w3.shape[1:]), w3, m3, v3, l, results.get(k), name=f"adamw_{k}_{l}")
        return results[stages[stage][0][0]][0]

    follows = update("l0", update("l1", g["meta"]))

    small_shapes = [grads[k].shape for k in SMALL]
    (all_small,) = _all_gather([_pack([loss] + [grads[k].astype(F32) for k in SMALL])], name="gather_small_grads")
    loss_parts, *small_parts = _unpack(all_small, [loss.shape] + small_shapes, lead=1)
    mine = []
    for k, p in zip(SMALL, small_parts):
        ax = SHARD_AXIS[k]
        mine.append(p if ax is None else lax.dynamic_slice_in_dim(p, me * w[k].shape[ax], w[k].shape[ax], 1 + ax))
    zero = jnp.zeros(loss.shape, F32)
    packed = [_pack([z] + [d[k] for k in SMALL]) for z, d in ((zero, w), (zero, mom), (zero, var))]
    res = _adamw(_pack([loss_parts] + mine, lead=1), *[t[None] for t in packed], 0, name="adamw_small")
    shapes = [loss.shape] + [w[k].shape for k in SMALL]
    for t, which in zip(res, range(4)):
        for k, a in zip(["loss"] + SMALL, _unpack(t[0], shapes)):
            results.setdefault(k, [None] * 4)[which] = a
    update("in", jnp.maximum(follows[0, :1, :1], res[0][0, :1, :1]))

    outs = [[view(k, results[k][which]) for k in WEIGHTS] for which in range(4)]
    return (results["loss"][0][0, 0], g["x"][None], *outs[0], *outs[1], *outs[2], *outs[3])
```

```python
import functools

import jax
import jax.numpy as jnp
from jax import lax
from jax.experimental import pallas as pl
from jax.experimental.pallas import tpu as pltpu

F32 = jnp.float32
BF16 = jnp.bfloat16
MXU_DTYPE = BF16
GRAD_WIRE_DTYPE = BF16

D_MODEL = 1024
N_META = 16
PAD_ROWS = 112
D_CONV = 512
DN_HEADS = 4
DN_HEAD_DIM = 128
DN_DIM = DN_HEADS * DN_HEAD_DIM
DN_CHUNK = 64
SEG = 512
N_SEG = 7
SWA_HEADS = 16
SWA_KV_HEADS = 4
SWA_GROUP = SWA_HEADS // SWA_KV_HEADS
SWA_HEAD_DIM = 64
SWA_BLOCK = 128
D_FF = 2816
EPS = 1e-6
NEG = -1e30
N_DEV = 8

ADAM_LR = 0.001
ADAM_B1 = 0.9
ADAM_B2 = 0.999
ADAM_EPS = 1e-08
ADAM_WD = 0.01
ADAM_STEP = 10

VMEM_LIMIT_BYTES = 52 * 1024 * 1024
SUBLANES = 8
LANES = 128


def _pick(n, prefs):
    for p in prefs:
        if n % p == 0:
            return p
    return n


def _params(sem, vmem=VMEM_LIMIT_BYTES):
    return pltpu.CompilerParams(dimension_semantics=sem, vmem_limit_bytes=vmem)


def _rms(x, w):
    return x * lax.rsqrt(jnp.mean(x * x, -1, keepdims=True) + EPS) * w


def _norm_matmul(h, nw, w, *, o_seg=None, trans_w=False, n=None, name):
    m, k = h.shape
    n = n or (w.shape[0] if trans_w else w.shape[1])
    tm = _pick(m, (1408, 384, 128))
    tn = _pick(o_seg or n, (1408, 1024, 512, 256, 128))
    dims = _DOT_DIMS["nt" if trans_w else "nn"]

    def body(h_ref, nw_ref, w_ref, o_ref, hn_ref, hn_s):
        @pl.when(pl.program_id(1) == 0)
        def _():
            hn = _rms(h_ref[...], nw_ref[...]).astype(MXU_DTYPE)
            hn_s[...] = hn
            hn_ref[...] = hn

        o_ref[...] = lax.dot_general(hn_s[...], w_ref[...], dims, preferred_element_type=F32)

    if o_seg:
        per = o_seg // tn
        o_shape = jax.ShapeDtypeStruct((n // o_seg, m, o_seg), F32)
        o_spec = pl.BlockSpec((None, tm, tn), lambda i, j: (j // per, i, j % per))
    else:
        o_shape = jax.ShapeDtypeStruct((m, n), F32)
        o_spec = pl.BlockSpec((tm, tn), lambda i, j: (i, j))
    return pl.pallas_call(
        body, name=name, grid=(m // tm, n // tn),
        in_specs=[pl.BlockSpec((tm, k), lambda i, j: (i, 0)), pl.BlockSpec((1, k), lambda i, j: (0, 0)),
                  pl.BlockSpec((tn, k), lambda i, j: (j, 0)) if trans_w else pl.BlockSpec((k, tn), lambda i, j: (0, j))],
        out_specs=[o_spec, pl.BlockSpec((tm, k), lambda i, j: (i, 0))],
        out_shape=[o_shape, jax.ShapeDtypeStruct((m, k), MXU_DTYPE)],
        scratch_shapes=[pltpu.VMEM((tm, k), MXU_DTYPE)],
        compiler_params=_params(("parallel", "arbitrary")),
    )(h, nw, w)


TILE_BUDGET_BYTES = 38 * 1024 * 1024
TILE_SIZES = (4224, 2816, 1792, 1536, 1408, 1024, 512, 256, 128)


def _divisor_tiles(n):
    return [t for t in TILE_SIZES if n % t == 0] or [n]


def _mm_nn(a, w, *, res=None, a_seg=False, trans_w=False, w_seg=False, out_dtype=F32, name):
    if a_seg:
        s, m, seg = a.shape
    else:
        m, seg = a.shape
        s = 1
    k = s * seg
    n = w.shape[0] * w.shape[2] if w_seg else (w.shape[0] if trans_w else w.shape[1])
    n_seg = w.shape[2] if w_seg else n
    tm = _pick(m, (1408, 1024, 512, 384, 256, 128))
    ab = a.dtype.itemsize
    k_steps = [(sb, seg) for sb in range(s, 0, -1) if s % sb == 0] if a_seg else [(1, t) for t in _divisor_tiles(seg)]
    best = None
    for tn in _divisor_tiles(n_seg):
        for sb, tk1 in k_steps:
            tk = sb * tk1
            nk = k // tk
            need = 2 * tm * tk * ab + 2 * tk * tn * 2 + 2 * tm * tn * 4 + (tm * tn * 4 if nk > 1 else 0) + (2 * tm * tn * 4 if res is not None else 0)
            if need <= TILE_BUDGET_BYTES and (best is None or tk * tn > best[0]):
                best = (tk * tn, tn, sb, tk1)
    _, tn, sb, tk1 = best
    tk = sb * tk1
    nk = k // tk
    w_dims = _DOT_DIMS["nt" if trans_w else "nn"]

    def body(*refs):
        a_ref, w_ref = refs[:2]
        r_ref = refs[2] if res is not None else None
        o_ref = refs[3 if res is not None else 2]

        def partial_product():
            if not a_seg:
                return lax.dot_general(a_ref[...].astype(MXU_DTYPE), w_ref[...], w_dims, preferred_element_type=F32)
            out = None
            for t in range(sb):
                wt = w_ref[:, t * seg:(t + 1) * seg] if trans_w else w_ref[t * seg:(t + 1) * seg, :]
                d = lax.dot_general(a_ref[t].astype(MXU_DTYPE), wt, w_dims, preferred_element_type=F32)
                out = d if out is None else out + d
            return out

        if nk == 1:
            o_ref[...] = (partial_product() if res is None else partial_product() + r_ref[...]).astype(o_ref.dtype)
            return
        acc = refs[-1]
        kk = pl.program_id(2)

        @pl.when(kk == 0)
        def _():
            acc[...] = jnp.zeros_like(acc)

        acc[...] += partial_product()

        @pl.when(kk == nk - 1)
        def _():
            o_ref[...] = (acc[...] if res is None else acc[...] + r_ref[...]).astype(o_ref.dtype)

    a_spec = pl.BlockSpec((sb, tm, seg), lambda i, j, kk: (kk, i, 0)) if a_seg else pl.BlockSpec((tm, tk), lambda i, j, kk: (i, kk))
    if w_seg:
        per = n_seg // tn
        w_spec = pl.BlockSpec((None, tk, tn), lambda i, j, kk: (j // per, kk, j % per))
    elif trans_w:
        w_spec = pl.BlockSpec((tn, tk), lambda i, j, kk: (j, kk))
    else:
        w_spec = pl.BlockSpec((tk, tn), lambda i, j, kk: (kk, j))
    in_specs = [a_spec, w_spec]
    args = [a, w]
    if res is not None:
        in_specs.append(pl.BlockSpec((tm, tn), lambda i, j, kk: (i, j)))
        args.append(res)
    return pl.pallas_call(
        body, name=name, grid=(m // tm, n // tn, nk), in_specs=in_specs,
        out_specs=pl.BlockSpec((tm, tn), lambda i, j, kk: (i, j)),
        out_shape=jax.ShapeDtypeStruct((m, n), out_dtype),
        scratch_shapes=[pltpu.VMEM((tm, tn), F32)] if nk > 1 else [],
        compiler_params=_params(("parallel", "parallel", "arbitrary")),
    )(*args)


def _mm_tn(a, b, *, a_seg=False, b_seg=False, out_dtype=None, name):
    out_dtype = out_dtype or GRAD_WIRE_DTYPE
    if a_seg:
        sa, m, a_unit = a.shape
        ka = sa * a_unit
    else:
        m, ka = a.shape
        a_unit = ka
    if b_seg:
        s, _, seg = b.shape
        n = s * seg
    else:
        n = b.shape[1]
        seg = n
    tmc = _pick(m, (1408, 384, 128))
    best = None
    for tka in _divisor_tiles(a_unit):
        for tn in _divisor_tiles(seg):
            need = 2 * tmc * tka * a.dtype.itemsize + 2 * tmc * tn * b.dtype.itemsize + tka * tn * 4 + 2 * tka * tn * 4
            if need <= TILE_BUDGET_BYTES and (best is None or (tka * tn, tn) > best[:2]):
                best = (tka * tn, tn, tka)
    _, tn, tka = best
    nm = m // tmc

    def body(a_ref, b_ref, o_ref, acc):
        mm = pl.program_id(2)

        @pl.when(mm == 0)
        def _():
            acc[...] = jnp.zeros_like(acc)

        acc[...] += lax.dot_general(a_ref[...].astype(MXU_DTYPE), b_ref[...].astype(MXU_DTYPE),
                                    (((0,), (0,)), ((), ())), preferred_element_type=F32)

        @pl.when(mm == nm - 1)
        def _():
            o_ref[...] = acc[...].astype(o_ref.dtype)

    if b_seg:
        per = seg // tn
        b_spec = pl.BlockSpec((None, tmc, tn), lambda i, j, mm: (j // per, mm, j % per))
    else:
        b_spec = pl.BlockSpec((tmc, tn), lambda i, j, mm: (mm, j))
    if a_seg:
        a_per = a_unit // tka
        a_spec = pl.BlockSpec((None, tmc, tka), lambda i, j, mm: (i // a_per, mm, i % a_per))
    else:
        a_spec = pl.BlockSpec((tmc, tka), lambda i, j, mm: (mm, i))
    return pl.pallas_call(
        body, name=name, grid=(ka // tka, n // tn, nm),
        in_specs=[a_spec, b_spec],
        out_specs=pl.BlockSpec((tka, tn), lambda i, j, mm: (i, j)),
        out_shape=jax.ShapeDtypeStruct((ka, n), out_dtype),
        scratch_shapes=[pltpu.VMEM((tka, tn), F32)],
        compiler_params=_params(("parallel", "parallel", "arbitrary")),
    )(a, b)


def _rmsnorm_bwd(dhn, h, nw, dres, *, name):
    m, d = h.shape
    tm = _pick(m, (384, 128))

    def body(dhn_ref, h_ref, nw_ref, dres_ref, dh_ref, dnw_ref):
        i = pl.program_id(0)
        x = h_ref[...]
        r = lax.rsqrt(jnp.mean(x * x, -1, keepdims=True) + EPS)
        xh = x * r
        dy = dhn_ref[...]
        dxh = dy * nw_ref[...]
        dx = r * (dxh - xh * jnp.mean(dxh * xh, -1, keepdims=True))
        row = i * tm + lax.broadcasted_iota(jnp.int32, (tm, 1), 0)
        dh_ref[...] = jnp.where(row >= PAD_ROWS, dres_ref[...] + dx, 0.0)

        @pl.when(i == 0)
        def _():
            dnw_ref[...] = jnp.zeros_like(dnw_ref)

        dnw_ref[...] += jnp.sum(dy * xh, 0, keepdims=True)

    return pl.pallas_call(
        body, name=name, grid=(m // tm,),
        in_specs=[pl.BlockSpec((tm, d), lambda i: (i, 0)), pl.BlockSpec((tm, d), lambda i: (i, 0)),
                  pl.BlockSpec((1, d), lambda i: (0, 0)), pl.BlockSpec((tm, d), lambda i: (i, 0))],
        out_specs=[pl.BlockSpec((tm, d), lambda i: (i, 0)), pl.BlockSpec((1, d), lambda i: (0, 0))],
        out_shape=[jax.ShapeDtypeStruct((m, d), F32), jax.ShapeDtypeStruct((1, d), F32)],
        compiler_params=_params(("arbitrary",)),
    )(dhn, h, nw, dres)


ROW_CHUNK = 248


def _row_chunks(m):
    out, s = [], SUBLANES
    while s < m:
        n = min(ROW_CHUNK, m - s)
        out.append((s, n))
        s += n
    return out


def _conv_at(load, w, width, s, n):
    acc = w[width - 1:width, :] * load(s, n)
    for j in range(width - 1):
        acc = acc + w[j:j + 1, :] * load(s - (width - 1 - j), n)
    return acc


def _conv_t_at(load, w, width, s, n):
    acc = w[width - 1:width, :] * load(s, n)
    for j in range(width - 1):
        acc = acc + w[j:j + 1, :] * load(s + (width - 1 - j), n)
    return acc


def _dconv_w(load_x, d, width, s, n):
    rows = [jnp.sum(d * load_x(s - (width - 1 - j), n), 0, keepdims=True) for j in range(width)]
    rows.append(jnp.zeros((SUBLANES - width, d.shape[1]), F32))
    return jnp.concatenate(rows, 0)


def _pad_w(w):
    return jnp.concatenate([w, jnp.zeros((SUBLANES - w.shape[0], w.shape[1]), w.dtype)], 0)


def _sigmoid(x):
    return 1.0 / (1.0 + jnp.exp(-x))


def _ffn_act_fwd(u, cw, *, name):
    _, m, f = u.shape
    cb = _pick(f, (256, 128))
    chunks = _row_chunks(m)

    def body(g_ref, v_ref, w_ref, o_ref):
        w = w_ref[...]
        o_ref[pl.ds(0, SUBLANES), :] = jnp.zeros((SUBLANES, cb), o_ref.dtype)
        for s, n in chunks:
            c = _conv_at(lambda a, b: g_ref[pl.ds(a, b), :], w, 3, s, n)
            o_ref[pl.ds(s, n), :] = (c * _sigmoid(c) * v_ref[pl.ds(s, n), :]).astype(o_ref.dtype)

    return pl.pallas_call(
        body, name=name, grid=(f // cb,),
        in_specs=[pl.BlockSpec((None, m, cb), lambda j: (0, 0, j)), pl.BlockSpec((None, m, cb), lambda j: (1, 0, j)),
                  pl.BlockSpec((SUBLANES, cb), lambda j: (0, j))],
        out_specs=pl.BlockSpec((m, cb), lambda j: (0, j)),
        out_shape=jax.ShapeDtypeStruct((m, f), MXU_DTYPE),
        compiler_params=_params(("parallel",)),
    )(u, u, cw)


def _ffn_act_bwd(da, u, cw, *, name):
    _, m, f = u.shape
    cb = LANES
    chunks = _row_chunks(m)

    def body(da_ref, g_ref, v_ref, w_ref, du_ref, dw_ref, dg_s):
        w = w_ref[...]
        zeros8 = jnp.zeros((SUBLANES, cb), F32)
        dg_s[pl.ds(0, SUBLANES), :] = zeros8
        dg_s[pl.ds(m, SUBLANES), :] = zeros8
        du_ref[0, pl.ds(0, SUBLANES), :] = zeros8.astype(du_ref.dtype)
        du_ref[1, pl.ds(0, SUBLANES), :] = zeros8.astype(du_ref.dtype)
        load_g = lambda a, b: g_ref[pl.ds(a, b), :]
        dw = jnp.zeros((SUBLANES, cb), F32)
        for s, n in chunks:
            c = _conv_at(load_g, w, 3, s, n)
            sg = _sigmoid(c)
            d = da_ref[pl.ds(s, n), :]
            du_ref[1, pl.ds(s, n), :] = (d * (c * sg)).astype(du_ref.dtype)
            dc = d * v_ref[pl.ds(s, n), :] * (sg * (1.0 + c * (1.0 - sg)))
            dg_s[pl.ds(s, n), :] = dc
            dw = dw + _dconv_w(load_g, dc, 3, s, n)
        dw_ref[...] = dw
        for s, n in chunks:
            du_ref[0, pl.ds(s, n), :] = _conv_t_at(lambda a, b: dg_s[pl.ds(a, b), :], w, 3, s, n).astype(du_ref.dtype)

    return pl.pallas_call(
        body, name=name, grid=(f // cb,),
        in_specs=[pl.BlockSpec((m, cb), lambda j: (0, j)), pl.BlockSpec((None, m, cb), lambda j: (0, 0, j)),
                  pl.BlockSpec((None, m, cb), lambda j: (1, 0, j)), pl.BlockSpec((SUBLANES, cb), lambda j: (0, j))],
        out_specs=[pl.BlockSpec((2, m, cb), lambda j: (0, 0, j)), pl.BlockSpec((SUBLANES, cb), lambda j: (0, j))],
        out_shape=[jax.ShapeDtypeStruct((2, m, f), MXU_DTYPE), jax.ShapeDtypeStruct((SUBLANES, f), F32)],
        scratch_shapes=[pltpu.VMEM((m + SUBLANES, cb), F32)],
        compiler_params=_params(("parallel",)),
    )(da, u, u, cw)


def _shortconv_fwd(pm, cw, y, *, name):
    _, m, seg = pm.shape
    cb = _pick(seg, (256, 128))
    chunks = _row_chunks(m)

    def body(gi_ref, go_ref, ah_ref, w_ref, y_in, o_ref):
        del y_in
        w = w_ref[...]
        o_ref[pl.ds(0, SUBLANES), :] = jnp.zeros((SUBLANES, cb), o_ref.dtype)
        load_m = lambda a, b: gi_ref[pl.ds(a, b), :] * ah_ref[pl.ds(a, b), :]
        for s, n in chunks:
            o_ref[pl.ds(s, n), :] = (go_ref[pl.ds(s, n), :] * _conv_at(load_m, w, 3, s, n)).astype(o_ref.dtype)

    return pl.pallas_call(
        body, name=name, grid=(seg // cb,),
        in_specs=[pl.BlockSpec((None, m, cb), lambda j: (0, 0, j)), pl.BlockSpec((None, m, cb), lambda j: (1, 0, j)),
                  pl.BlockSpec((None, m, cb), lambda j: (2, 0, j)), pl.BlockSpec((SUBLANES, cb), lambda j: (0, j)),
                  pl.BlockSpec(memory_space=pl.ANY)],
        out_specs=pl.BlockSpec((m, cb), lambda j: (0, j)),
        out_shape=jax.ShapeDtypeStruct(y.shape, y.dtype),
        input_output_aliases={4: 0},
        compiler_params=_params(("parallel",)),
    )(pm, pm, pm, cw, y)


def _shortconv_bwd(dy, pm, cw, dpm, *, name):
    _, m, seg = pm.shape
    cb = LANES
    chunks = _row_chunks(m)

    def body(dy_ref, gi_ref, go_ref, ah_ref, w_ref, dpm_in, dp_ref, dw_ref, dc_s):
        del dpm_in
        w = w_ref[...]
        zeros8 = jnp.zeros((SUBLANES, cb), F32)
        dc_s[pl.ds(0, SUBLANES), :] = zeros8
        dc_s[pl.ds(m, SUBLANES), :] = zeros8
        for t in range(3):
            dp_ref[t, pl.ds(0, SUBLANES), :] = zeros8.astype(dp_ref.dtype)
        load_m = lambda a, b: gi_ref[pl.ds(a, b), :] * ah_ref[pl.ds(a, b), :]
        dw = jnp.zeros((SUBLANES, cb), F32)
        for s, n in chunks:
            d = dy_ref[pl.ds(s, n), :]
            dp_ref[1, pl.ds(s, n), :] = (d * _conv_at(load_m, w, 3, s, n)).astype(dp_ref.dtype)
            dc = d * go_ref[pl.ds(s, n), :]
            dc_s[pl.ds(s, n), :] = dc
            dw = dw + _dconv_w(load_m, dc, 3, s, n)
        dw_ref[...] = dw
        for s, n in chunks:
            dm = _conv_t_at(lambda a, b: dc_s[pl.ds(a, b), :], w, 3, s, n)
            dp_ref[0, pl.ds(s, n), :] = (dm * ah_ref[pl.ds(s, n), :]).astype(dp_ref.dtype)
            dp_ref[2, pl.ds(s, n), :] = (dm * gi_ref[pl.ds(s, n), :]).astype(dp_ref.dtype)

    return pl.pallas_call(
        body, name=name, grid=(seg // cb,),
        in_specs=[pl.BlockSpec((m, cb), lambda j: (0, j)), pl.BlockSpec((None, m, cb), lambda j: (0, 0, j)),
                  pl.BlockSpec((None, m, cb), lambda j: (1, 0, j)), pl.BlockSpec((None, m, cb), lambda j: (2, 0, j)),
                  pl.BlockSpec((SUBLANES, cb), lambda j: (0, j)), pl.BlockSpec(memory_space=pl.ANY)],
        out_specs=[pl.BlockSpec((3, m, cb), lambda j: (0, 0, j)), pl.BlockSpec((SUBLANES, cb), lambda j: (0, j))],
        out_shape=[jax.ShapeDtypeStruct(dpm.shape, dpm.dtype), jax.ShapeDtypeStruct((SUBLANES, seg), F32)],
        scratch_shapes=[pltpu.VMEM((m + SUBLANES, cb), F32)],
        input_output_aliases={5: 0},
        compiler_params=_params(("parallel",)),
    )(dy, pm, pm, pm, cw, dpm)


def _dnpre_fwd(pm, cw, *, name):
    _, m, seg = pm.shape
    cb = _pick(seg, (256, 128))
    per = seg // cb
    chunks = _row_chunks(m)

    def body(x_ref, w_ref, o_ref):
        w = w_ref[...]
        o_ref[pl.ds(0, SUBLANES), :] = jnp.zeros((SUBLANES, cb), F32)
        for s, n in chunks:
            c = _conv_at(lambda a, b: x_ref[pl.ds(a, b), :], w, 4, s, n)
            o_ref[pl.ds(s, n), :] = c * _sigmoid(c)

    return pl.pallas_call(
        body, name=name, grid=(3 * per,),
        in_specs=[pl.BlockSpec((None, m, cb), lambda j: (3 + j // per, 0, j % per)), pl.BlockSpec((SUBLANES, cb), lambda j: (0, j))],
        out_specs=pl.BlockSpec((None, m, cb), lambda j: (j // per, 0, j % per)),
        out_shape=jax.ShapeDtypeStruct((3, m, seg), F32),
        compiler_params=_params(("parallel",)),
    )(pm, cw)


def _dnpre_bwd(dqkv, pm, cw, dpm, *, name):
    _, m, seg = pm.shape
    cb = _pick(seg, (256, 128))
    per = seg // cb
    chunks = _row_chunks(m)

    def body(d_ref, x_ref, w_ref, dpm_in, dp_ref, dw_ref, dc_s):
        del dpm_in
        w = w_ref[...]
        zeros8 = jnp.zeros((SUBLANES, cb), F32)
        dc_s[pl.ds(0, SUBLANES), :] = zeros8
        dc_s[pl.ds(m, SUBLANES), :] = zeros8
        dp_ref[pl.ds(0, SUBLANES), :] = zeros8.astype(dp_ref.dtype)
        load_x = lambda a, b: x_ref[pl.ds(a, b), :]
        dw = jnp.zeros((SUBLANES, cb), F32)
        for s, n in chunks:
            c = _conv_at(load_x, w, 4, s, n)
            sg = _sigmoid(c)
            dc = d_ref[pl.ds(s, n), :] * (sg * (1.0 + c * (1.0 - sg)))
            dc_s[pl.ds(s, n), :] = dc
            dw = dw + _dconv_w(load_x, dc, 4, s, n)
        dw_ref[...] = dw
        for s, n in chunks:
            dp_ref[pl.ds(s, n), :] = _conv_t_at(lambda a, b: dc_s[pl.ds(a, b), :], w, 4, s, n).astype(dp_ref.dtype)

    return pl.pallas_call(
        body, name=name, grid=(3 * per,),
        in_specs=[pl.BlockSpec((None, m, cb), lambda j: (j // per, 0, j % per)),
                  pl.BlockSpec((None, m, cb), lambda j: (3 + j // per, 0, j % per)),
                  pl.BlockSpec((SUBLANES, cb), lambda j: (0, j)), pl.BlockSpec(memory_space=pl.ANY)],
        out_specs=[pl.BlockSpec((None, m, cb), lambda j: (3 + j // per, 0, j % per)), pl.BlockSpec((SUBLANES, cb), lambda j: (0, j))],
        out_shape=[jax.ShapeDtypeStruct(dpm.shape, dpm.dtype), jax.ShapeDtypeStruct((SUBLANES, 3 * seg), F32)],
        scratch_shapes=[pltpu.VMEM((m + SUBLANES, cb), F32)],
        input_output_aliases={3: 0},
        compiler_params=_params(("parallel",)),
    )(dqkv, pm, cw, dpm)


def _mxu_dot_impl(a, b, form):
    a = a.astype(MXU_DTYPE)
    b = b.astype(MXU_DTYPE)
    dims = {"nn": (((1,), (0,)), ((), ())), "nt": (((1,), (1,)), ((), ())), "tn": (((0,), (0,)), ((), ()))}[form]
    return lax.dot_general(a, b, dims, preferred_element_type=F32)


@functools.partial(jax.custom_vjp, nondiff_argnums=(2,))
def _mxu_dot(a, b, form):
    return _mxu_dot_impl(a, b, form)


def _mxu_dot_fwd(a, b, form):
    return _mxu_dot_impl(a, b, form), (a, b)


def _mxu_dot_bwd(form, saved, g):
    a, b = saved
    if form == "nn":
        return _mxu_dot_impl(g, b, "nt"), _mxu_dot_impl(a, g, "tn")
    if form == "nt":
        return _mxu_dot_impl(g, b, "nn"), _mxu_dot_impl(g, a, "tn")
    return _mxu_dot_impl(b, g, "nt"), _mxu_dot_impl(a, g, "nn")


_mxu_dot.defvjp(_mxu_dot_fwd, _mxu_dot_bwd)


_DOT_DIMS = {"nn": (((1,), (0,)), ((), ())), "nt": (((1,), (1,)), ((), ())), "tn": (((0,), (0,)), ((), ()))}


def _split(x):
    hi = x.astype(BF16)
    return hi, (x - hi.astype(F32)).astype(BF16)


def _dot3_impl(a, b, form):
    dg = lambda p, q: lax.dot_general(p, q, _DOT_DIMS[form], preferred_element_type=F32)
    ah, al = _split(a)
    bh, bl = _split(b)
    return dg(ah, bh) + (dg(ah, bl) + dg(al, bh))


@functools.partial(jax.custom_vjp, nondiff_argnums=(2,))
def _dot3(a, b, form):
    return _dot3_impl(a, b, form)


def _dot3_fwd(a, b, form):
    return _dot3_impl(a, b, form), (a, b)


def _dot3_bwd(form, saved, g):
    a, b = saved
    if form == "nn":
        return _dot3_impl(g, b, "nt"), _dot3_impl(a, g, "tn")
    if form == "nt":
        return _dot3_impl(g, b, "nn"), _dot3_impl(g, a, "tn")
    return _dot3_impl(b, g, "nt"), _dot3_impl(a, g, "nn")


_dot3.defvjp(_dot3_fwd, _dot3_bwd)


def _hdot(a, b):
    return _dot3(a, b, "nn")


def _mask_dot(mask, x, form):
    dg = lambda q: lax.dot_general(mask.astype(BF16), q, _DOT_DIMS[form], preferred_element_type=F32)
    x1 = x.astype(BF16)
    r1 = x - x1.astype(F32)
    x2 = r1.astype(BF16)
    x3 = (r1 - x2.astype(F32)).astype(BF16)
    return dg(x1) + (dg(x2) + dg(x3))


def _decay_masks(c):
    row = lax.broadcasted_iota(jnp.int32, (c, c), 0)
    col = lax.broadcasted_iota(jnp.int32, (c, c), 1)
    return (row >= col).astype(F32), row <= col


def _decay_impl(gb):
    lower, upper = _decay_masks(gb.shape[0])
    return _mask_dot(lower, gb, "nn"), _mask_dot(jnp.ones_like(gb), jnp.where(upper, gb, 0.0), "nn")


@jax.custom_vjp
def _decay_matrices(gb):
    return _decay_impl(gb)


def _decay_fwd(gb):
    return _decay_impl(gb), None


def _decay_bwd(_, cts):
    gc, gr = cts
    lower, upper = _decay_masks(gc.shape[0])
    return (_mask_dot(lower, gc, "tn") + jnp.where(upper, _mask_dot(jnp.ones_like(gr), gr, "tn"), 0.0),)


_decay_matrices.defvjp(_decay_fwd, _decay_bwd)


def _heads(f, *lists):
    return [f(*t) for t in zip(*lists)]


def _inverses_impl(a):
    c = a[0].shape[0]
    row = lax.broadcasted_iota(jnp.int32, (c, c), 0)
    col = lax.broadcasted_iota(jnp.int32, (c, c), 1)
    eye = jnp.where(row == col, 1.0, 0.0)
    x = _heads(lambda t: eye - t, a)
    p = _heads(lambda t: _dot3_impl(t, t, "nn"), a)
    power = 2
    while power < c:
        x = _heads(lambda s, t: s + _dot3_impl(s, t, "nn"), x, p)
        power *= 2
        if power < c:
            p = _heads(lambda t: _dot3_impl(t, t, "nn"), p)
    return x


@jax.custom_vjp
def _unit_lower_inverses(a):
    return _inverses_impl(a)


def _unit_lower_inverses_fwd(a):
    x = _inverses_impl(a)
    return x, x


def _unit_lower_inverses_bwd(x, g):
    t = _heads(lambda s, u: _dot3_impl(s, u, "tn"), x, g)
    return (_heads(lambda u, s: -_dot3_impl(u, s, "nt"), t, x),)


_unit_lower_inverses.defvjp(_unit_lower_inverses_fwd, _unit_lower_inverses_bwd)


def _softplus(x):
    return jnp.maximum(x, 0.0) + jnp.log(1.0 + jnp.exp(-jnp.abs(x)))


DN_STEP_CHUNKS = 3


def _dn_chunk(qr, kr, v, z, braw, araw, alog, dtb, nw, state, valid):
    c = DN_CHUNK
    nh = len(state)
    chunks = len(qr) // nh
    alog, dtb, valid_i = alog * chunks, dtb * chunks, [vv for vv in valid for _ in range(nh)]
    row = lax.broadcasted_iota(jnp.int32, (c, c), 0)
    col = lax.broadcasted_iota(jnp.int32, (c, c), 1)
    incl = row >= col
    strict = row > col
    q = _heads(lambda t: t * lax.rsqrt(jnp.sum(t * t, -1, keepdims=True) + EPS) * (DN_HEAD_DIM ** -0.5), qr)
    k = _heads(lambda t: t * lax.rsqrt(jnp.sum(t * t, -1, keepdims=True) + EPS), kr)
    beta = _heads(lambda t, vv: _sigmoid(t) * vv, braw, valid_i)
    g = _heads(lambda al, ar, dt, vv: -jnp.exp(al) * _softplus(ar + dt) * vv, alog, araw, dtb, valid_i)
    decay = _heads(lambda t: _decay_matrices(jnp.broadcast_to(t, (c, c))), g)
    dmask = _heads(lambda d: jnp.where(incl, jnp.exp(jnp.where(incl, d[0] - d[1], 0.0)), 0.0), decay)
    dec = _heads(lambda d: d[0][:, :1], decay)
    dlast = _heads(lambda d: d[0][c - 1:c, :1], decay)
    kk = _heads(lambda t: _mxu_dot(t, t, "nt"), k)
    a = _heads(lambda b, t, d: jnp.where(strict, b * t * d, 0.0), beta, kk, dmask)
    x = _unit_lower_inverses(a)
    uw = _heads(lambda s, vv, kk_, b, d: _hdot(s, jnp.concatenate([vv * b, kk_ * (b * jnp.exp(d))], 1)), x, v, k, beta, dec)
    u = _heads(lambda t: t[:, :DN_HEAD_DIM], uw)
    w = _heads(lambda t: t[:, DN_HEAD_DIM:], uw)
    qk = _heads(lambda s, t, d: _mxu_dot(s, t, "nt") * d, q, k, dmask)
    q_dec = _heads(lambda t, d: t * jnp.exp(d), q, dec)
    k_dec = _heads(lambda t, dl, d: t * jnp.exp(dl - d), k, dlast, dec)
    o = []
    for ci in range(chunks):
        of = lambda lst: lst[ci * nh:(ci + 1) * nh]
        v_new = _heads(lambda s, t, st: s - _mxu_dot(t, st, "nn"), of(u), of(w), state)
        o += _heads(lambda qd, st, s, vn: _mxu_dot(qd, st, "nn") + _mxu_dot(s, vn, "nn"), of(q_dec), state, of(qk), v_new)
        state = _heads(lambda st, dl, kd, vn: st * jnp.exp(dl) + _mxu_dot(kd, vn, "tn"), state, of(dlast), of(k_dec), v_new)
    y = _heads(lambda t, zz: _rms(t, nw) * (zz * _sigmoid(zz)), o, z)
    return y, state


DN_STEP_ROWS = DN_STEP_CHUNKS * DN_CHUNK
DN_ITEMS = [(ci, h) for ci in range(DN_STEP_CHUNKS) for h in range(DN_HEADS)]


def _dn_valid(n):
    rows = [n * DN_STEP_ROWS + ci * DN_CHUNK + lax.broadcasted_iota(jnp.int32, (DN_CHUNK, 1), 0) for ci in range(DN_STEP_CHUNKS)]
    return [(r >= PAD_ROWS).astype(F32) for r in rows]


def _dn_in_specs(rev, nc):
    cn = (lambda n: nc - 1 - n) if rev else (lambda n: n)
    c, hd = DN_STEP_ROWS, DN_HEAD_DIM
    return [
        pl.BlockSpec((None, c, DN_DIM), lambda n: (0, cn(n), 0)),
        pl.BlockSpec((None, c, DN_DIM), lambda n: (1, cn(n), 0)),
        pl.BlockSpec((None, c, DN_DIM), lambda n: (2, cn(n), 0)),
        pl.BlockSpec((None, c, DN_DIM), lambda n: (6, cn(n), 0)),
        pl.BlockSpec((DN_HEADS, 2, c, 1), lambda n: (0, 0, cn(n), 0)),
        pl.BlockSpec((DN_HEADS, SUBLANES, LANES), lambda n: (0, 0, 0)),
        pl.BlockSpec((1, hd), lambda n: (0, 0)),
    ]


def _rows(ci):
    return slice(ci * DN_CHUNK, (ci + 1) * DN_CHUNK)


def _cols(h):
    return slice(h * DN_HEAD_DIM, (h + 1) * DN_HEAD_DIM)


def _dn_load(q_ref, k_ref, v_ref, z_ref, ba_ref, hp_ref):
    heads = range(DN_HEADS)
    item = lambda ref: [ref[_rows(ci), _cols(h)] for ci, h in DN_ITEMS]
    return (item(q_ref), item(k_ref), item(v_ref), item(z_ref),
            [ba_ref[h, 0, _rows(ci), :] for ci, h in DN_ITEMS], [ba_ref[h, 1, _rows(ci), :] for ci, h in DN_ITEMS],
            [hp_ref[h, 0:1, 0:1] for h in heads], [hp_ref[h, 1:2, 0:1] for h in heads])


def _delta_fwd(qkvc, pm, ba, hp, nw, *, name):
    _, m, _ = qkvc.shape
    nc = m // DN_STEP_ROWS
    hd = DN_HEAD_DIM

    def body(q_ref, k_ref, v_ref, z_ref, ba_ref, hp_ref, nw_ref, y_ref, s_ref, state):
        n = pl.program_id(0)

        @pl.when(n == 0)
        def _():
            state[...] = jnp.zeros_like(state)

        heads = range(DN_HEADS)
        old = [state[h] for h in heads]
        y, new = _dn_chunk(*_dn_load(q_ref, k_ref, v_ref, z_ref, ba_ref, hp_ref), nw_ref[...], old, _dn_valid(n))
        for h in heads:
            s_ref[h] = old[h]
            state[h] = new[h]
        for (ci, h), yy in zip(DN_ITEMS, y):
            y_ref[_rows(ci), _cols(h)] = yy.astype(y_ref.dtype)

    return pl.pallas_call(
        body, name=name, grid=(nc,), in_specs=_dn_in_specs(False, nc),
        out_specs=[pl.BlockSpec((DN_STEP_ROWS, DN_DIM), lambda n: (n, 1)), pl.BlockSpec((DN_HEADS, None, hd, hd), lambda n: (0, n, 0, 0))],
        out_shape=[jax.ShapeDtypeStruct((m, D_CONV + DN_DIM), MXU_DTYPE), jax.ShapeDtypeStruct((DN_HEADS, nc, hd, hd), F32)],
        scratch_shapes=[pltpu.VMEM((DN_HEADS, hd, hd), F32)],
        compiler_params=_params(("arbitrary",)),
    )(qkvc, qkvc, qkvc, pm, ba, hp, nw)


def _delta_bwd(dy, qkvc, pm, ba, hp, nw, states, *, name):
    _, m, _ = qkvc.shape
    nc = m // DN_STEP_ROWS
    hd, c = DN_HEAD_DIM, DN_STEP_ROWS

    def body(q_ref, k_ref, v_ref, z_ref, ba_ref, hp_ref, nw_ref, s_ref, dy_ref,
             dz_ref, dqkv_ref, dba_ref, dhp_ref, dnw_ref, dstate):
        step = pl.program_id(0)
        n = nc - 1 - step

        @pl.when(step == 0)
        def _():
            dstate[...] = jnp.zeros_like(dstate)
            dhp_ref[...] = jnp.zeros_like(dhp_ref)
            dnw_ref[...] = jnp.zeros_like(dnw_ref)

        valid = _dn_valid(n)
        heads = range(DN_HEADS)
        fn = lambda *a: _dn_chunk(*a, valid)
        _, vjp = jax.vjp(fn, *_dn_load(q_ref, k_ref, v_ref, z_ref, ba_ref, hp_ref), nw_ref[...], [s_ref[h] for h in heads])
        dy = [dy_ref[_rows(ci), _cols(h)] for ci, h in DN_ITEMS]
        dq, dk, dv, dz, dbr, dar, dalog, ddtb, dnw, dst = vjp((dy, [dstate[h] for h in heads]))
        for i, (ci, h) in enumerate(DN_ITEMS):
            dqkv_ref[0, _rows(ci), _cols(h)] = dq[i]
            dqkv_ref[1, _rows(ci), _cols(h)] = dk[i]
            dqkv_ref[2, _rows(ci), _cols(h)] = dv[i]
            dz_ref[_rows(ci), _cols(h)] = dz[i].astype(dz_ref.dtype)
            dba_ref[h, 0, _rows(ci), :] = dbr[i]
            dba_ref[h, 1, _rows(ci), :] = dar[i]
        for h in heads:
            dstate[h] = dst[h]
            dhp_ref[h] += jnp.concatenate([jnp.broadcast_to(dalog[h], (1, LANES)), jnp.broadcast_to(ddtb[h], (1, LANES)),
                                           jnp.zeros((SUBLANES - 2, LANES), F32)], 0)
        dnw_ref[...] += dnw

    rn = lambda n: nc - 1 - n
    in_specs = _dn_in_specs(True, nc) + [
        pl.BlockSpec((DN_HEADS, None, hd, hd), lambda n: (0, rn(n), 0, 0)),
        pl.BlockSpec((c, DN_DIM), lambda n: (rn(n), 1)),
    ]
    out_specs = [
        pl.BlockSpec((None, c, DN_DIM), lambda n: (6, rn(n), 0)),
        pl.BlockSpec((3, c, DN_DIM), lambda n: (0, rn(n), 0)),
        pl.BlockSpec((DN_HEADS, 2, c, 1), lambda n: (0, 0, rn(n), 0)),
        pl.BlockSpec((DN_HEADS, SUBLANES, LANES), lambda n: (0, 0, 0)),
        pl.BlockSpec((1, hd), lambda n: (0, 0)),
    ]
    return pl.pallas_call(
        body, name=name, grid=(nc,), in_specs=in_specs, out_specs=out_specs,
        out_shape=[jax.ShapeDtypeStruct(pm.shape, MXU_DTYPE), jax.ShapeDtypeStruct(qkvc.shape, F32),
                   jax.ShapeDtypeStruct(ba.shape, F32), jax.ShapeDtypeStruct(hp.shape, F32),
                   jax.ShapeDtypeStruct((1, hd), F32)],
        scratch_shapes=[pltpu.VMEM((DN_HEADS, hd, hd), F32)],
        compiler_params=_params(("arbitrary",)),
    )(qkvc, qkvc, qkvc, pm, ba, hp, nw, states, dy)


SWA_PAIR = 4


def _attn_block(q, k0, kp, kc, v0, vp, vc, qw, kw, sink, n):
    g, b, hd = SWA_GROUP, SWA_BLOCK, SWA_HEAD_DIM
    pair = list(range(SWA_PAIR))
    lanes = lambda t, e: t[:, e * hd:(e + 1) * hd]
    q4 = [jnp.concatenate([lanes(q, e * g + i)[None] for i in range(g)], 0) for e in pair]
    qn = _heads(lambda t: _rms(t, qw) * (hd ** -0.5), q4)
    kn = [_rms(jnp.concatenate([lanes(k0, e), lanes(kp, e), lanes(kc, e)], 0), kw) for e in pair]
    vcat = [jnp.concatenate([lanes(v0, e), lanes(vp, e), lanes(vc, e)], 0) for e in pair]
    s = _heads(lambda a, k: _mxu_dot(a.reshape(g * b, hd), k, "nt").reshape(g, b, 3 * b), qn, kn)
    i = lax.broadcasted_iota(jnp.int32, (b, 3 * b), 0)
    c = lax.broadcasted_iota(jnp.int32, (b, 3 * b), 1)
    in_meta, in_prev, in_cur = c < b, (c >= b) & (c < 2 * b), c >= 2 * b
    j = c - jnp.where(in_meta, 0, jnp.where(in_prev, b, 2 * b))
    meta_lo = jnp.where(n == 0, b, PAD_ROWS)
    cur_lo = jnp.where(n == 0, PAD_ROWS, 0)
    prev_off = jnp.where(n >= 2, 0, 2 * b)
    valid = (in_meta & (j >= meta_lo)) | (in_prev & (j > i + prev_off)) | (in_cur & (j <= i) & (j >= cur_lo))
    s = _heads(lambda t: jnp.where(valid[None], t, NEG), s)
    m = [lax.stop_gradient(jnp.maximum(jnp.max(t, -1, keepdims=True), sink[e])) for e, t in zip(pair, s)]
    ex = _heads(lambda t, mm: jnp.exp(t - mm), s, m)
    p = [t / (jnp.sum(t, -1, keepdims=True) + jnp.exp(sink[e] - mm)) for e, t, mm in zip(pair, ex, m)]
    o = _heads(lambda t, v: _mxu_dot(t.reshape(g * b, 3 * b), v, "nn").reshape(g, b, hd), p, vcat)
    return jnp.concatenate([o[e][i] for e in pair for i in range(g)], 1)


Q_LANES = SWA_PAIR * SWA_GROUP * SWA_HEAD_DIM
KV_LANES = SWA_PAIR * SWA_HEAD_DIM
K_BLOCK0 = SWA_HEADS * SWA_HEAD_DIM // KV_LANES
V_BLOCK0 = K_BLOCK0 + SWA_KV_HEADS * SWA_HEAD_DIM // KV_LANES


def _attn_in_specs():
    g, b, hd = SWA_GROUP, SWA_BLOCK, SWA_HEAD_DIM
    kv = lambda f, first: pl.BlockSpec((b, KV_LANES), lambda p, n: (f(n), first + p))
    blocks = [lambda n: 0, lambda n: jnp.maximum(n - 1, 0), lambda n: n]
    return ([pl.BlockSpec((b, Q_LANES), lambda p, n: (n, p))] + [kv(f, K_BLOCK0) for f in blocks] + [kv(f, V_BLOCK0) for f in blocks]
            + [pl.BlockSpec((1, hd), lambda p, n: (0, 0)), pl.BlockSpec((1, hd), lambda p, n: (0, 0)),
               pl.BlockSpec((SWA_PAIR, g, 1, 1), lambda p, n: (p, 0, 0, 0))])


def _attn_fwd(qkv, qw, kw, sink, *, name):
    m = qkv.shape[0]
    b = SWA_BLOCK

    def body(q_ref, k0, kp, kc, v0, vp, vc, qw_ref, kw_ref, s_ref, o_ref):
        o_ref[...] = _attn_block(q_ref[...], k0[...], kp[...], kc[...], v0[...], vp[...], vc[...], qw_ref[...], kw_ref[...],
                                 s_ref[...], pl.program_id(1)).astype(o_ref.dtype)

    return pl.pallas_call(
        body, name=name, grid=(SWA_KV_HEADS // SWA_PAIR, m // b), in_specs=_attn_in_specs(),
        out_specs=pl.BlockSpec((b, Q_LANES), lambda p, n: (n, p)),
        out_shape=jax.ShapeDtypeStruct((m, SWA_HEADS * SWA_HEAD_DIM), MXU_DTYPE),
        compiler_params=_params(("parallel", "parallel")),
    )(*([qkv] * 7), qw, kw, sink)


def _attn_bwd(do, qkv, qw, kw, sink, *, name):
    m = qkv.shape[0]
    g, b, hd = SWA_GROUP, SWA_BLOCK, SWA_HEAD_DIM

    def body(q_ref, k0, kp, kc, v0, vp, vc, qw_ref, kw_ref, s_ref, do_ref, dq_ref, dk_ref, dv_ref, dqw_ref, dkw_ref, ds_ref):
        n = pl.program_id(1)

        @pl.when(n == 0)
        def _():
            for r in (dk_ref, dv_ref, dqw_ref, dkw_ref, ds_ref):
                r[...] = jnp.zeros_like(r)

        fn = lambda *a: _attn_block(*a, n)
        _, vjp = jax.vjp(fn, q_ref[...], k0[...], kp[...], kc[...], v0[...], vp[...], vc[...], qw_ref[...], kw_ref[...], s_ref[...])
        dq, dk0, dkp, dkc, dv0, dvp, dvc, dqw, dkw, dsk = vjp(do_ref[...])
        dq_ref[...] = dq
        prev = pl.multiple_of(jnp.maximum(n - 1, 0) * b, b)
        cur = pl.multiple_of(n * b, b)
        for ref, parts in ((dk_ref, (dk0, dkp, dkc)), (dv_ref, (dv0, dvp, dvc))):
            ref[pl.ds(0, b), :] += parts[0]
            ref[pl.ds(prev, b), :] += parts[1]
            ref[pl.ds(cur, b), :] += parts[2]
        dqw_ref[...] += dqw
        dkw_ref[...] += dkw
        ds_ref[...] += dsk

    pairs = SWA_KV_HEADS // SWA_PAIR
    kv_acc = pl.BlockSpec((m, KV_LANES), lambda p, n: (0, p))
    w_acc = pl.BlockSpec((None, 1, hd), lambda p, n: (p, 0, 0))
    kv_shape = jax.ShapeDtypeStruct((m, SWA_KV_HEADS * hd), F32)
    return pl.pallas_call(
        body, name=name, grid=(pairs, m // b),
        in_specs=_attn_in_specs() + [pl.BlockSpec((b, Q_LANES), lambda p, n: (n, p))],
        out_specs=[pl.BlockSpec((b, Q_LANES), lambda p, n: (n, p)), kv_acc, kv_acc, w_acc, w_acc,
                   pl.BlockSpec((SWA_PAIR, g, 1, 1), lambda p, n: (p, 0, 0, 0))],
        out_shape=[jax.ShapeDtypeStruct((m, SWA_HEADS * hd), F32), kv_shape, kv_shape,
                   jax.ShapeDtypeStruct((pairs, 1, hd), F32), jax.ShapeDtypeStruct((pairs, 1, hd), F32),
                   jax.ShapeDtypeStruct(sink.shape, F32)],
        compiler_params=_params(("parallel", "arbitrary")),
    )(*([qkv] * 7), qw, kw, sink, do)


def _loss_bwd(h, target, *, name):
    m, d = h.shape
    b = SWA_BLOCK

    def body(h_ref, t_ref, l_ref, dh_ref):
        i = pl.program_id(0)

        @pl.when(i == 0)
        def _():
            l_ref[...] = jnp.zeros_like(l_ref)
            dh_ref[...] = jnp.zeros_like(dh_ref)

        @pl.when(i > 0)
        def _():
            e = h_ref[...] - t_ref[...]
            dh_ref[...] = e * (1.0 / d)
            l_ref[...] += jnp.sum(jnp.sum(e * e, 0, keepdims=True), 1, keepdims=True) * (0.5 / d)

    return pl.pallas_call(
        body, name=name, grid=(m // b,),
        in_specs=[pl.BlockSpec((b, d), lambda i: (i, 0)), pl.BlockSpec((b, d), lambda i: (jnp.maximum(i - 1, 0), 0))],
        out_specs=[pl.BlockSpec((1, LANES), lambda i: (0, 0)), pl.BlockSpec((b, d), lambda i: (i, 0))],
        out_shape=[jax.ShapeDtypeStruct((1, LANES), F32), jax.ShapeDtypeStruct((m, d), F32)],
        compiler_params=_params(("arbitrary",)),
    )(h, target)


def _ffn_fwd(h, nw, w_up_t, cw, w_down, tag):
    u, hn = _norm_matmul(h, nw, w_up_t, o_seg=D_FF, trans_w=True, name=f"ffn_up_{tag}")
    a = _ffn_act_fwd(u, cw, name=f"ffn_act_{tag}")
    return _mm_nn(a, w_down, res=h, name=f"ffn_down_{tag}"), (h, hn, u, a)


def _ffn_bwd(dh, saved, nw, w_up_t, cw, w_down, tag):
    h, hn, u, a = saved
    da = _mm_nn(dh, w_down, trans_w=True, name=f"ffn_da_{tag}")
    dw_down = _mm_tn(a, dh, name=f"ffn_dwdown_{tag}")
    du, dcw = _ffn_act_bwd(da, u, cw, name=f"ffn_act_bwd_{tag}")
    dhn = _mm_nn(du, w_up_t, a_seg=True, name=f"ffn_dhn_{tag}")
    dw_up_t = _mm_tn(du, hn, a_seg=True, name=f"ffn_dwup_{tag}")
    dh_in, dnw = _rmsnorm_bwd(dhn, h, nw, dh, name=f"ffn_norm_bwd_{tag}")
    return dh_in, dnw, dw_up_t, dcw, dw_down


def _local_step(x, target, w, fetch=None, push=None):
    fetch = fetch or (lambda stage, after: {})
    push = push or (lambda stage, grads: None)
    plus = lambda a, zero: a if zero is None else a + zero
    seq, d = x.shape
    m = PAD_ROWS + N_META + seq
    h0 = jnp.concatenate([jnp.zeros((PAD_ROWS, d), F32), w["meta"], x], 0)

    pm, hn0 = _norm_matmul(h0, w["anw"][0], w["w_in_t"], o_seg=SEG, trans_w=True, n=N_SEG * SEG, name="mix_in")
    pba = _mm_nn(hn0, w["w_in_tail_t"], trans_w=True, name="mix_in_tail")
    qkvc = _dnpre_fwd(pm, w["dcw"], name="dn_conv")
    ba = pba[:, :2 * DN_HEADS].T.reshape(2, DN_HEADS, m, 1).transpose(1, 0, 2, 3)
    y, states = _delta_fwd(qkvc, pm, ba, w["hp"], w["dnw"], name="delta")
    y = _shortconv_fwd(pm, w["caw"], y, name="shortconv")
    w = {**w, **fetch("l0", y)}
    h1 = _mm_nn(y, w["w_out"], res=h0, name="mix_out")
    h2, ffn0 = _ffn_fwd(h1, w["fnw"][0], w["w_up0_t"], w["fcw"][0], w["w_down0"], "l0")

    w = {**w, **fetch("l1", h2)}
    qkv, hn2 = _norm_matmul(h2, w["anw"][1], w["wqkv"], name="attn_qkv")
    o = _attn_fwd(qkv, w["qnw"], w["knw"], w["sink"], name="attn")
    h3 = _mm_nn(o, w["wo"], res=h2, name="attn_out")
    h4, ffn1 = _ffn_fwd(h3, w["fnw"][1], w["w_up1_t"], w["fcw"][1], w["w_down1"], "l1")

    loss, dh4 = _loss_bwd(h4, target, name="loss")

    g = {}
    dh3, dfnw1, dwup1, dfcw1, dwdown1 = _ffn_bwd(dh4, ffn1, w["fnw"][1], w["w_up1_t"], w["fcw"][1], w["w_down1"], "l1")

    do = _mm_nn(dh3, w["wo"], trans_w=True, name="attn_do")
    g["wo"] = _mm_tn(o, dh3, name="attn_dwo")
    dq, dk, dv, dqw, dkw, dsink = _attn_bwd(do, qkv, w["qnw"], w["knw"], w["sink"], name="attn_bwd")
    dqkv = jnp.concatenate([dq, dk, dv], 1).astype(MXU_DTYPE)
    dhn2 = _mm_nn(dqkv, w["wqkv"], trans_w=True, name="attn_dhn")
    g["wqkv"] = _mm_tn(hn2, dqkv, name="attn_dwqkv")
    zero = push("l1", dict(w_up_t=dwup1, w_down=dwdown1, wo=g["wo"], wqkv=g["wqkv"]))
    dh2, danw1 = _rmsnorm_bwd(dhn2, h2, plus(w["anw"][1], zero), dh3, name="attn_norm_bwd")

    dh1, dfnw0, dwup0, dfcw0, dwdown0 = _ffn_bwd(dh2, ffn0, w["fnw"][0], w["w_up0_t"], w["fcw"][0], w["w_down0"], "l0")

    dy = _mm_nn(dh1, w["w_out"], trans_w=True, name="mix_dy")
    g["w_out"] = _mm_tn(y, dh1, name="mix_dwout")
    zero = push("l0", dict(w_up_t=dwup0, w_down=dwdown0, w_out=g["w_out"]))
    dpm, dqkvc, dba, dhp, ddnw = _delta_bwd(dy, qkvc, pm, ba, w["hp"], plus(w["dnw"], zero), states, name="delta_bwd")
    dpm, ddcw = _dnpre_bwd(dqkvc, pm, w["dcw"], dpm, name="dn_conv_bwd")
    dpm, dcaw = _shortconv_bwd(dy, pm, w["caw"], dpm, name="shortconv_bwd")
    dpba = jnp.pad(dba.transpose(1, 0, 2, 3).reshape(2 * DN_HEADS, m).T, ((0, 0), (0, LANES - 2 * DN_HEADS))).astype(MXU_DTYPE)
    g["w_in_t"] = jnp.concatenate([_mm_tn(dpm, hn0, a_seg=True, out_dtype=F32, name="mix_dwin"),
                                   _mm_tn(dpba, hn0, out_dtype=F32, name="mix_dwin_tail")[:N_TAIL]], 0).astype(GRAD_WIRE_DTYPE)
    zero = push("in", dict(w_in_t=g["w_in_t"]))
    dhn0 = _mm_nn(dpba, plus(w["w_in_tail_t"], None if zero is None else zero.astype(MXU_DTYPE)), name="mix_dhn_tail")
    dhn0 = _mm_nn(dpm, w["w_in_t"], res=dhn0, a_seg=True, name="mix_dhn")
    dh0, danw0 = _rmsnorm_bwd(dhn0, h0, w["anw"][0], dh1, name="mix_norm_bwd")

    g.update(
        x=dh0[PAD_ROWS + N_META:], meta=dh0[PAD_ROWS:PAD_ROWS + N_META], anw=[danw0, danw1], fnw=[dfnw0, dfnw1],
        caw=dcaw, dcw=ddcw, hp=dhp, dnw=ddnw, qnw=jnp.sum(dqw, 0), knw=jnp.sum(dkw, 0), sink=dsink,
        w_up_t=[dwup0, dwup1], fcw=[dfcw0, dfcw1], w_down=[dwdown0, dwdown1])
    return loss, g


N_TAIL = 2 * DN_HEADS


def _prepare_early(p):
    n_main = N_SEG * SEG
    w_in_t = p["mix_w_in_t"]
    tail_t = jnp.pad(w_in_t[n_main:], ((0, LANES - N_TAIL), (0, 0)))
    hp = jnp.zeros((DN_HEADS, SUBLANES, LANES), F32)
    hp = hp.at[:, 0, :].set(p["dn_a_log"][0][:, None]).at[:, 1, :].set(p["dn_dt_bias"][0][:, None])
    depth = p["ffn_conv_w"].shape[0]
    return dict(
        meta=p["meta_tokens"], anw=[p["attn_norm_w"][i:i + 1] for i in range(depth)],
        fnw=[p["ffn_norm_w"][i:i + 1] for i in range(depth)],
        w_in_t=w_in_t, w_in_tail_t=tail_t,
        caw=_pad_w(p["conv_a_w"][0]), dcw=_pad_w(p["dn_conv_w"][0]), hp=hp, dnw=p["dn_norm_w"],
        qnw=p["swa_q_norm_w"], knw=p["swa_k_norm_w"], sink=p["swa_sinks"].reshape(SWA_KV_HEADS, SWA_GROUP, 1, 1),
        fcw=[_pad_w(p["ffn_conv_w"][i]) for i in range(depth)])


def _prepare_weights(p):
    return dict(
        _prepare_early(dict(p, mix_w_in_t=p["mix_w_in"][0].T)), w_out=p["mix_w_out"][0], wo=p["swa_wo"][0],
        wqkv=jnp.concatenate([p["swa_wq"][0], p["swa_wk"][0], p["swa_wv"][0]], 1),
        w_up0_t=p["ffn_w_up"][0].T, w_up1_t=p["ffn_w_up"][1].T, w_down0=p["ffn_w_down"][0], w_down1=p["ffn_w_down"][1])


def _small_named(g):
    return dict(
        meta_tokens=g["meta"], attn_norm_w=jnp.concatenate(g["anw"], 0), ffn_norm_w=jnp.concatenate(g["fnw"], 0),
        conv_a_w=g["caw"][None, :3], dn_conv_w=g["dcw"][None, :4],
        dn_a_log=g["hp"][None, :, 0, 0], dn_dt_bias=g["hp"][None, :, 1, 0], dn_norm_w=g["dnw"],
        swa_q_norm_w=g["qnw"], swa_k_norm_w=g["knw"], swa_sinks=g["sink"].reshape(1, SWA_HEADS),
        ffn_conv_w=jnp.stack([c[:3] for c in g["fcw"]]))


def _reference_named(g):
    nq, nkv = SWA_HEADS * SWA_HEAD_DIM, SWA_KV_HEADS * SWA_HEAD_DIM
    return dict(
        _small_named(g), mix_w_in=g["w_in_t"].T[None],
        mix_w_out=g["w_out"][None], swa_wq=g["wqkv"][None, :, :nq], swa_wk=g["wqkv"][None, :, nq:nq + nkv],
        swa_wv=g["wqkv"][None, :, nq + nkv:], swa_wo=g["wo"][None],
        ffn_w_up=jnp.stack([t.T for t in g["w_up_t"]]), ffn_w_down=jnp.stack(g["w_down"]))


def _my_index():
    return 4 * lax.axis_index("x") + 2 * lax.axis_index("y") + lax.axis_index("c")


def _all_gather(arrays, *, name):
    n = len(arrays)

    def body(*refs):
        ins, outs = refs[:n], refs[n:2 * n]
        send_sems, recv_sems, local_sems = refs[2 * n:]
        x, y, c = lax.axis_index("x"), lax.axis_index("y"), lax.axis_index("c")
        me, sibling = (x, y, c), (x, y, 1 - c)
        chips = [(1 - x, y), (x, 1 - y), (1 - x, 1 - y)]

        def copy(i, k, block, to, src=None):
            rows = outs[i].at[4 * block[0] + 2 * block[1] + block[2]]
            return pltpu.make_async_remote_copy(
                src_ref=rows if src is None else src, dst_ref=rows, send_sem=send_sems.at[i, k], recv_sem=recv_sems.at[i, k],
                device_id=to, device_id_type=pl.DeviceIdType.MESH)

        mine = [pltpu.make_async_copy(ins[i], outs[i].at[4 * x + 2 * y + c], local_sems.at[i]) for i in range(n)]
        first = []
        for j, chip in enumerate(chips):
            first += [copy(i, 1 + j, me, (*chip, c), src=ins[i]) for i in range(n)]
        first += [copy(i, 0, me, sibling, src=ins[i]) for i in range(n)]
        for cp in first + mine:
            cp.start()
        passed = []
        for j, chip in enumerate(chips):
            for i in range(n):
                copy(i, 1 + j, (*chip, c), me).wait_recv()
                fwd = copy(i, 4 + j, (*chip, c), sibling)
                fwd.start()
                passed.append(fwd)
        for i in range(n):
            copy(i, 0, sibling, me).wait_recv()
            for j, chip in enumerate(chips):
                copy(i, 4 + j, (*chip, 1 - c), me).wait_recv()
        for cp in first + passed:
            cp.wait_send()
        for cp in mine:
            cp.wait()

    hbm = pl.BlockSpec(memory_space=pl.ANY)
    return pl.pallas_call(
        body, name=name, in_specs=[hbm] * n, out_specs=[hbm] * n,
        out_shape=[jax.ShapeDtypeStruct((N_DEV,) + tuple(a.shape), a.dtype) for a in arrays],
        scratch_shapes=[pltpu.SemaphoreType.DMA((n, 7)), pltpu.SemaphoreType.DMA((n, 7)), pltpu.SemaphoreType.DMA((n,))],
    )(*arrays)


def _peer(d):
    px, py, pc = lax.axis_index("x") ^ (d >> 2), lax.axis_index("y") ^ ((d >> 1) & 1), lax.axis_index("c") ^ (d & 1)
    return (px, py, pc), 4 * px + 2 * py + pc


def _push_copies(mode, srcs, lands, send_sems, recv_sems):
    me = _my_index()
    out = []
    for d in range(1, N_DEV):
        pos, idx = _peer(d)
        for i in range(len(srcs)):
            out.append(pltpu.make_async_remote_copy(
                src_ref=srcs[i] if mode == "gather" else srcs[i].at[idx], dst_ref=lands[i].at[me],
                send_sem=send_sems.at[i * N_DEV + d], recv_sem=recv_sems.at[i * N_DEV + d], device_id=pos,
                device_id_type=pl.DeviceIdType.MESH))
    return out


_HBM = pl.BlockSpec(memory_space=pltpu.HBM)
_SEM = pl.BlockSpec(memory_space=pltpu.SEMAPHORE)


def _push_start(mode, arrays, follows, *, name):
    n = len(arrays)
    blocks = [a.shape if mode == "gather" else a.shape[1:] for a in arrays]
    lands = [lax.empty((N_DEV,) + tuple(b), a.dtype) for a, b in zip(arrays, blocks)]

    def body(*refs):
        srcs, land_refs = refs[:n], refs[n:2 * n]
        send_sems, recv_sems = refs[2 * n + 1], refs[2 * n + 2]
        zero = refs[-1]
        for cp in _push_copies(mode, srcs, land_refs, send_sems, recv_sems):
            cp.start()
        zero[...] = jnp.zeros_like(zero)

    hbm_in = [pltpu.with_memory_space_constraint(a, pltpu.HBM) for a in list(arrays) + lands]
    outs = pl.pallas_call(
        body, name=name,
        out_shape=[pltpu.SemaphoreType.DMA((n * N_DEV,)), pltpu.SemaphoreType.DMA((n * N_DEV,))]
        + [pltpu.HBM(a.shape, a.dtype) for a in hbm_in] + [jax.ShapeDtypeStruct((SUBLANES, LANES), F32)],
        in_specs=[_HBM] * (2 * n) + [pl.BlockSpec(memory_space=pl.ANY)],
        out_specs=[_SEM, _SEM] + [_HBM] * (2 * n) + [pl.BlockSpec(memory_space=pltpu.VMEM)],
        input_output_aliases={i: 2 + i for i in range(2 * n)},
        compiler_params=pltpu.CompilerParams(has_side_effects=pltpu.SideEffectType.DATAFLOW_SIDE_EFFECTING),
    )(*hbm_in, follows)
    return dict(mode=mode, sems=outs[:2], srcs=outs[2:2 + n], lands=outs[2 + n:2 + 2 * n], zero=outs[-1])


def _push_wait(push, follows, *, name):
    n = len(push["srcs"])
    mode = push["mode"]

    def body(*refs):
        srcs, land_refs = refs[:n], refs[n:2 * n]
        send_sems, recv_sems = refs[2 * n], refs[2 * n + 1]
        for cp in _push_copies(mode, srcs, land_refs, send_sems, recv_sems):
            cp.wait_send()
            cp.wait_recv()

    args = list(push["srcs"]) + list(push["lands"])
    outs = pl.pallas_call(
        body, name=name, out_shape=[pltpu.HBM(a.shape, a.dtype) for a in args],
        in_specs=[_HBM] * (2 * n) + [_SEM, _SEM, pl.BlockSpec(memory_space=pl.ANY)], out_specs=[_HBM] * (2 * n),
        input_output_aliases={i: i for i in range(2 * n)},
        compiler_params=pltpu.CompilerParams(has_side_effects=pltpu.SideEffectType.DATAFLOW_SIDE_EFFECTING),
    )(*args, *push["sems"], follows)
    me = _my_index()
    got = []
    for src, land in zip(outs[:n], outs[n:]):
        own = src if mode == "gather" else lax.dynamic_index_in_dim(src, me, 0, keepdims=False)
        got.append(lax.dynamic_update_index_in_dim(land, own, me, 0))
    return got


ADAMW_BLOCK_BYTES = 6 * 1024 * 1024


def _adamw_tile(r, c):
    fits = lambda tr, tc: N_DEV * tr * tc * 4 <= ADAMW_BLOCK_BYTES
    rows = [t for t in range(2 * SUBLANES, r + 1, 2 * SUBLANES) if r % t == 0 and fits(t, c)]
    if rows or fits(r, c):
        return (max(rows) if rows else r), c
    cols = [t for t in range(LANES, c + 1, LANES) if c % t == 0 and fits(r, t)]
    return r, max(cols)


def _adamw(parts, w, m, v, layer, outs=None, *, name):
    nl, r, c = w.shape
    tr, tc = _adamw_tile(r, c)

    def body(p_ref, w_ref, m_ref, v_ref, *rest):
        g_ref, d_ref, nm_ref, nv_ref = rest[-4:]
        g = p_ref[0].astype(F32)
        for j in range(1, N_DEV):
            g = g + p_ref[j].astype(F32)
        m2 = ADAM_B1 * m_ref[...] + (1.0 - ADAM_B1) * g
        v2 = ADAM_B2 * v_ref[...] + (1.0 - ADAM_B2) * jnp.square(g)
        m_hat = m2 / (1.0 - ADAM_B1 ** ADAM_STEP)
        v_hat = v2 / (1.0 - ADAM_B2 ** ADAM_STEP)
        g_ref[...] = g
        d_ref[...] = -ADAM_LR * (m_hat / (jnp.sqrt(v_hat) + ADAM_EPS) + ADAM_WD * w_ref[...])
        nm_ref[...] = m2
        nv_ref[...] = v2

    blk = pl.BlockSpec((None, tr, tc), lambda i, j: (layer, i, j))
    out = jax.ShapeDtypeStruct((nl, r, c), F32)
    given = list(outs) if outs is not None else []
    return pl.pallas_call(
        body, name=name, grid=(r // tr, c // tc),
        in_specs=[pl.BlockSpec((N_DEV, tr, tc), lambda i, j: (0, i, j)), blk, blk, blk] + [pl.BlockSpec(memory_space=pl.ANY)] * len(given),
        out_specs=[blk, blk, blk, blk], out_shape=[out, out, out, out],
        input_output_aliases={4 + t: t for t in range(len(given))},
        compiler_params=_params(("parallel", "parallel")),
    )(parts, w, m, v, *given)


SHARD_AXIS = dict(
    meta_tokens=1, attn_norm_w=None, ffn_norm_w=None, mix_w_in=2, conv_a_w=2, dn_conv_w=2, dn_a_log=None, dn_dt_bias=None,
    dn_norm_w=None, mix_w_out=1, swa_wq=1, swa_wk=1, swa_wv=1, swa_q_norm_w=None, swa_k_norm_w=None, swa_sinks=None,
    swa_wo=1, ffn_w_up=2, ffn_conv_w=2, ffn_w_down=1)
WEIGHTS = list(SHARD_AXIS)
BIG = ["mix_w_in", "mix_w_out", "swa_wq", "swa_wk", "swa_wv", "swa_wo", "ffn_w_up", "ffn_w_down"]
SMALL = [k for k in WEIGHTS if k not in BIG]
SMALL_SHARDED = [k for k in SMALL if SHARD_AXIS[k] is not None]


def _whole(g8, axis):
    t = jnp.moveaxis(g8, 0, axis)
    return t.reshape(t.shape[:axis] + (t.shape[axis] * t.shape[axis + 1],) + t.shape[axis + 2:])


def _by_owner(a, axis):
    s = a.shape[axis] // N_DEV
    return jnp.moveaxis(a.reshape(a.shape[:axis] + (N_DEV, s) + a.shape[axis + 1:]), axis, 0)


def _pack(arrays, lead=0):
    flat = jnp.concatenate([a.reshape(a.shape[:lead] + (-1,)) for a in arrays], -1)
    n = flat.shape[-1]
    rows = -(-n // (SUBLANES * LANES)) * SUBLANES
    flat = jnp.pad(flat, [(0, 0)] * lead + [(0, rows * LANES - n)])
    return flat.reshape(flat.shape[:lead] + (rows, LANES))


def _unpack(buf, shapes, lead=0):
    flat = buf.reshape(buf.shape[:lead] + (-1,))
    out, o = [], 0
    for s in shapes:
        n = 1
        for e in s:
            n *= e
        out.append(flat[..., o:o + n].reshape(buf.shape[:lead] + tuple(s)))
        o += n
    return out


def kernel(x, meta_tokens, attn_norm_w, ffn_norm_w, mix_w_in, conv_a_w, dn_conv_w, dn_a_log, dn_dt_bias, dn_norm_w, mix_w_out, swa_wq, swa_wk, swa_wv, swa_q_norm_w, swa_k_norm_w, swa_sinks, swa_wo, ffn_w_up, ffn_conv_w, ffn_w_down, loss_target, m_meta_tokens, m_attn_norm_w, m_ffn_norm_w, m_mix_w_in, m_conv_a_w, m_dn_conv_w, m_dn_a_log, m_dn_dt_bias, m_dn_norm_w, m_mix_w_out, m_swa_wq, m_swa_wk, m_swa_wv, m_swa_q_norm_w, m_swa_k_norm_w, m_swa_sinks, m_swa_wo, m_ffn_w_up, m_ffn_conv_w, m_ffn_w_down, v_meta_tokens, v_attn_norm_w, v_ffn_norm_w, v_mix_w_in, v_conv_a_w, v_dn_conv_w, v_dn_a_log, v_dn_dt_bias, v_dn_norm_w, v_mix_w_out, v_swa_wq, v_swa_wk, v_swa_wv, v_swa_q_norm_w, v_swa_k_norm_w, v_swa_sinks, v_swa_wo, v_ffn_w_up, v_ffn_conv_w, v_ffn_w_down):
    w = dict(meta_tokens=meta_tokens, attn_norm_w=attn_norm_w, ffn_norm_w=ffn_norm_w, mix_w_in=mix_w_in, conv_a_w=conv_a_w, dn_conv_w=dn_conv_w, dn_a_log=dn_a_log, dn_dt_bias=dn_dt_bias, dn_norm_w=dn_norm_w, mix_w_out=mix_w_out, swa_wq=swa_wq, swa_wk=swa_wk, swa_wv=swa_wv, swa_q_norm_w=swa_q_norm_w, swa_k_norm_w=swa_k_norm_w, swa_sinks=swa_sinks, swa_wo=swa_wo, ffn_w_up=ffn_w_up, ffn_conv_w=ffn_conv_w, ffn_w_down=ffn_w_down)
    mom = dict(meta_tokens=m_meta_tokens, attn_norm_w=m_attn_norm_w, ffn_norm_w=m_ffn_norm_w, mix_w_in=m_mix_w_in, conv_a_w=m_conv_a_w, dn_conv_w=m_dn_conv_w, dn_a_log=m_dn_a_log, dn_dt_bias=m_dn_dt_bias, dn_norm_w=m_dn_norm_w, mix_w_out=m_mix_w_out, swa_wq=m_swa_wq, swa_wk=m_swa_wk, swa_wv=m_swa_wv, swa_q_norm_w=m_swa_q_norm_w, swa_k_norm_w=m_swa_k_norm_w, swa_sinks=m_swa_sinks, swa_wo=m_swa_wo, ffn_w_up=m_ffn_w_up, ffn_conv_w=m_ffn_conv_w, ffn_w_down=m_ffn_w_down)
    var = dict(meta_tokens=v_meta_tokens, attn_norm_w=v_attn_norm_w, ffn_norm_w=v_ffn_norm_w, mix_w_in=v_mix_w_in, conv_a_w=v_conv_a_w, dn_conv_w=v_dn_conv_w, dn_a_log=v_dn_a_log, dn_dt_bias=v_dn_dt_bias, dn_norm_w=v_dn_norm_w, mix_w_out=v_mix_w_out, swa_wq=v_swa_wq, swa_wk=v_swa_wk, swa_wv=v_swa_wv, swa_q_norm_w=v_swa_q_norm_w, swa_k_norm_w=v_swa_k_norm_w, swa_sinks=v_swa_sinks, swa_wo=v_swa_wo, ffn_w_up=v_ffn_w_up, ffn_conv_w=v_ffn_conv_w, ffn_w_down=v_ffn_w_down)
    me = _my_index()

    transposed = ("mix_w_in", "ffn_w_up")
    view = lambda k, a: jnp.swapaxes(a, 1, 2) if k in transposed else a
    axis2d = {k: 0 if k in transposed else SHARD_AXIS[k] - 1 for k in BIG}
    shard16 = {k: view(k, w[k]).astype(MXU_DTYPE) for k in BIG}
    small_shard_shapes = [w[k].shape for k in SMALL_SHARDED]
    rows_in = shard16["mix_w_in"].shape[1]
    sent_in = jnp.pad(shard16["mix_w_in"][0], ((0, -rows_in % (2 * SUBLANES)), (0, 0)))
    got = _all_gather([sent_in, _pack([w[k] for k in SMALL_SHARDED])], name="gather_weights")
    whole = {"mix_w_in_t": _whole(got[0][:, :rows_in], 0)}
    for k, a in zip(SMALL_SHARDED, _unpack(got[1], small_shard_shapes, lead=1)):
        whole[k] = _whole(a, SHARD_AXIS[k])
    for k in SMALL:
        whole.setdefault(k, w[k])
    stages = {"in": [("mix_w_in", 0)], "l0": [("mix_w_out", 0), ("ffn_w_up", 0), ("ffn_w_down", 0)],
              "l1": [("swa_wq", 0), ("swa_wk", 0), ("swa_wv", 0), ("swa_wo", 0), ("ffn_w_up", 1), ("ffn_w_down", 1)]}
    pushed = {}
    follows = got[1]
    for stage in ("l0", "l1"):
        pushed[stage] = _push_start("gather", [shard16[k][l] for k, l in stages[stage]], follows, name=f"push_weights_{stage}")
        follows = pushed[stage]["zero"]
    early = _prepare_early(whole)
    early["anw"][0] = early["anw"][0] + follows[0, 0]

    def fetch(stage, after):
        got = _push_wait(pushed[stage], after, name=f"wait_weights_{stage}")
        full = {kl: _whole(a, axis2d[kl[0]]) for kl, a in zip(stages[stage], got)}
        if stage == "l0":
            return dict(w_out=full["mix_w_out", 0], w_up0_t=full["ffn_w_up", 0], w_down0=full["ffn_w_down", 0])
        wqkv = jnp.concatenate([full["swa_wq", 0], full["swa_wk", 0], full["swa_wv", 0]], 1)
        return dict(wqkv=wqkv, wo=full["swa_wo", 0], w_up1_t=full["ffn_w_up", 1], w_down1=full["ffn_w_down", 1])

    nq, nkv = SWA_HEADS * SWA_HEAD_DIM, SWA_KV_HEADS * SWA_HEAD_DIM
    grad_pushes = {}

    def push(stage, gd):
        if stage == "in":
            named = {("mix_w_in", 0): gd["w_in_t"]}
        elif stage == "l1":
            named = {("swa_wq", 0): gd["wqkv"][:, :nq], ("swa_wk", 0): gd["wqkv"][:, nq:nq + nkv],
                     ("swa_wv", 0): gd["wqkv"][:, nq + nkv:], ("swa_wo", 0): gd["wo"],
                     ("ffn_w_up", 1): gd["w_up_t"], ("ffn_w_down", 1): gd["w_down"]}
        else:
            named = {("mix_w_out", 0): gd["w_out"], ("ffn_w_up", 0): gd["w_up_t"], ("ffn_w_down", 0): gd["w_down"]}
        sent = [_by_owner(named[kl], axis2d[kl[0]]) for kl in stages[stage]]
        grad_pushes[stage] = _push_start("scatter", sent, jnp.zeros((SUBLANES, LANES), F32), name=f"push_grads_{stage}")
        return grad_pushes[stage]["zero"][0, 0]

    loss, g = _local_step(x[0], loss_target[0], early, fetch, push)
    grads = _small_named(g)

    results = {}

    def update(stage, follows):
        got = _push_wait(grad_pushes[stage], follows, name=f"wait_grads_{stage}")
        for (k, l), parts in zip(stages[stage], got):
            w3, m3, v3 = view(k, w[k]), view(k, mom[k]), view(k, var[k])
            results[k] = _adamw(parts.reshape((N_DEV,) + w3.shape[1:]), w3, m3, v3, l, results.get(k), name=f"adamw_{k}_{l}")
        return results[stages[stage][0][0]][0]

    follows = update("l0", update("l1", g["meta"]))

    small_shapes = [grads[k].shape for k in SMALL]
    (all_small,) = _all_gather([_pack([loss] + [grads[k].astype(F32) for k in SMALL])], name="gather_small_grads")
    loss_parts, *small_parts = _unpack(all_small, [loss.shape] + small_shapes, lead=1)
    mine = []
    for k, p in zip(SMALL, small_parts):
        ax = SHARD_AXIS[k]
        mine.append(p if ax is None else lax.dynamic_slice_in_dim(p, me * w[k].shape[ax], w[k].shape[ax], 1 + ax))
    zero = jnp.zeros(loss.shape, F32)
    packed = [_pack([z] + [d[k] for k in SMALL]) for z, d in ((zero, w), (zero, mom), (zero, var))]
    res = _adamw(_pack([loss_parts] + mine, lead=1), *[t[None] for t in packed], 0, name="adamw_small")
    shapes = [loss.shape] + [w[k].shape for k in SMALL]
    for t, which in zip(res, range(4)):
        for k, a in zip(["loss"] + SMALL, _unpack(t[0], shapes)):
            results.setdefault(k, [None] * 4)[which] = a
    update("in", jnp.maximum(follows[0, :1, :1], res[0][0, :1, :1]))

    outs = [[view(k, results[k][which]) for k in WEIGHTS] for which in range(4)]
    return (results["loss"][0][0, 0], g["x"][None], *outs[0], *outs[1], *outs[2], *outs[3])
```

```python
import functools

import jax
import jax.numpy as jnp
from jax import lax
from jax.experimental import pallas as pl
from jax.experimental.pallas import tpu as pltpu

F32 = jnp.float32
BF16 = jnp.bfloat16
MXU_DTYPE = BF16
GRAD_WIRE_DTYPE = BF16

D_MODEL = 1024
N_META = 16
PAD_ROWS = 112
D_CONV = 512
DN_HEADS = 4
DN_HEAD_DIM = 128
DN_DIM = DN_HEADS * DN_HEAD_DIM
DN_CHUNK = 64
SEG = 512
N_SEG = 7
SWA_HEADS = 16
SWA_KV_HEADS = 4
SWA_GROUP = SWA_HEADS // SWA_KV_HEADS
SWA_HEAD_DIM = 64
SWA_BLOCK = 128
D_FF = 2816
EPS = 1e-6
NEG = -1e30
N_DEV = 8

ADAM_LR = 0.001
ADAM_B1 = 0.9
ADAM_B2 = 0.999
ADAM_EPS = 1e-08
ADAM_WD = 0.01
ADAM_STEP = 10

VMEM_LIMIT_BYTES = 52 * 1024 * 1024
SUBLANES = 8
LANES = 128


def _pick(n, prefs):
    for p in prefs:
        if n % p == 0:
            return p
    return n


def _params(sem, vmem=VMEM_LIMIT_BYTES):
    return pltpu.CompilerParams(dimension_semantics=sem, vmem_limit_bytes=vmem)


def _rms(x, w):
    return x * lax.rsqrt(jnp.mean(x * x, -1, keepdims=True) + EPS) * w


def _norm_matmul(h, nw, w, *, o_seg=None, trans_w=False, n=None, name):
    m, k = h.shape
    n = n or (w.shape[0] if trans_w else w.shape[1])
    tm = _pick(m, (1408, 384, 128))
    tn = _pick(o_seg or n, (1408, 1024, 512, 256, 128))
    dims = _DOT_DIMS["nt" if trans_w else "nn"]

    def body(h_ref, nw_ref, w_ref, o_ref, hn_ref, hn_s):
        @pl.when(pl.program_id(1) == 0)
        def _():
            hn = _rms(h_ref[...], nw_ref[...]).astype(MXU_DTYPE)
            hn_s[...] = hn
            hn_ref[...] = hn

        o_ref[...] = lax.dot_general(hn_s[...], w_ref[...], dims, preferred_element_type=F32)

    if o_seg:
        per = o_seg // tn
        o_shape = jax.ShapeDtypeStruct((n // o_seg, m, o_seg), F32)
        o_spec = pl.BlockSpec((None, tm, tn), lambda i, j: (j // per, i, j % per))
    else:
        o_shape = jax.ShapeDtypeStruct((m, n), F32)
        o_spec = pl.BlockSpec((tm, tn), lambda i, j: (i, j))
    return pl.pallas_call(
        body, name=name, grid=(m // tm, n // tn),
        in_specs=[pl.BlockSpec((tm, k), lambda i, j: (i, 0)), pl.BlockSpec((1, k), lambda i, j: (0, 0)),
                  pl.BlockSpec((tn, k), lambda i, j: (j, 0)) if trans_w else pl.BlockSpec((k, tn), lambda i, j: (0, j))],
        out_specs=[o_spec, pl.BlockSpec((tm, k), lambda i, j: (i, 0))],
        out_shape=[o_shape, jax.ShapeDtypeStruct((m, k), MXU_DTYPE)],
        scratch_shapes=[pltpu.VMEM((tm, k), MXU_DTYPE)],
        compiler_params=_params(("parallel", "arbitrary")),
    )(h, nw, w)


TILE_BUDGET_BYTES = 38 * 1024 * 1024
TILE_SIZES = (4224, 2816, 1792, 1536, 1408, 1024, 512, 256, 128)


def _divisor_tiles(n):
    return [t for t in TILE_SIZES if n % t == 0] or [n]


def _mm_nn(a, w, *, res=None, a_seg=False, trans_w=False, w_seg=False, out_dtype=F32, name):
    if a_seg:
        s, m, seg = a.shape
    else:
        m, seg = a.shape
        s = 1
    k = s * seg
    n = w.shape[0] * w.shape[2] if w_seg else (w.shape[0] if trans_w else w.shape[1])
    n_seg = w.shape[2] if w_seg else n
    tm = _pick(m, (1408, 1024, 512, 384, 256, 128))
    ab = a.dtype.itemsize
    k_steps = [(sb, seg) for sb in range(s, 0, -1) if s % sb == 0] if a_seg else [(1, t) for t in _divisor_tiles(seg)]
    best = None
    for tn in _divisor_tiles(n_seg):
        for sb, tk1 in k_steps:
            tk = sb * tk1
            nk = k // tk
            need = 2 * tm * tk * ab + 2 * tk * tn * 2 + 2 * tm * tn * 4 + (tm * tn * 4 if nk > 1 else 0) + (2 * tm * tn * 4 if res is not None else 0)
            if need <= TILE_BUDGET_BYTES and (best is None or tk * tn > best[0]):
                best = (tk * tn, tn, sb, tk1)
    _, tn, sb, tk1 = best
    tk = sb * tk1
    nk = k // tk
    w_dims = _DOT_DIMS["nt" if trans_w else "nn"]

    def body(*refs):
        a_ref, w_ref = refs[:2]
        r_ref = refs[2] if res is not None else None
        o_ref = refs[3 if res is not None else 2]

        def partial_product():
            if not a_seg:
                return lax.dot_general(a_ref[...].astype(MXU_DTYPE), w_ref[...], w_dims, preferred_element_type=F32)
            out = None
            for t in range(sb):
                wt = w_ref[:, t * seg:(t + 1) * seg] if trans_w else w_ref[t * seg:(t + 1) * seg, :]
                d = lax.dot_general(a_ref[t].astype(MXU_DTYPE), wt, w_dims, preferred_element_type=F32)
                out = d if out is None else out + d
            return out

        if nk == 1:
            o_ref[...] = (partial_product() if res is None else partial_product() + r_ref[...]).astype(o_ref.dtype)
            return
        acc = refs[-1]
        kk = pl.program_id(2)

        @pl.when(kk == 0)
        def _():
            acc[...] = jnp.zeros_like(acc)

        acc[...] += partial_product()

        @pl.when(kk == nk - 1)
        def _():
            o_ref[...] = (acc[...] if res is None else acc[...] + r_ref[...]).astype(o_ref.dtype)

    a_spec = pl.BlockSpec((sb, tm, seg), lambda i, j, kk: (kk, i, 0)) if a_seg else pl.BlockSpec((tm, tk), lambda i, j, kk: (i, kk))
    if w_seg:
        per = n_seg // tn
        w_spec = pl.BlockSpec((None, tk, tn), lambda i, j, kk: (j // per, kk, j % per))
    elif trans_w:
        w_spec = pl.BlockSpec((tn, tk), lambda i, j, kk: (j, kk))
    else:
        w_spec = pl.BlockSpec((tk, tn), lambda i, j, kk: (kk, j))
    in_specs = [a_spec, w_spec]
    args = [a, w]
    if res is not None:
        in_specs.append(pl.BlockSpec((tm, tn), lambda i, j, kk: (i, j)))
        args.append(res)
    return pl.pallas_call(
        body, name=name, grid=(m // tm, n // tn, nk), in_specs=in_specs,
        out_specs=pl.BlockSpec((tm, tn), lambda i, j, kk: (i, j)),
        out_shape=jax.ShapeDtypeStruct((m, n), out_dtype),
        scratch_shapes=[pltpu.VMEM((tm, tn), F32)] if nk > 1 else [],
        compiler_params=_params(("parallel", "parallel", "arbitrary")),
    )(*args)


def _mm_tn(a, b, *, a_seg=False, b_seg=False, out_dtype=None, name):
    out_dtype = out_dtype or GRAD_WIRE_DTYPE
    if a_seg:
        sa, m, a_unit = a.shape
        ka = sa * a_unit
    else:
        m, ka = a.shape
        a_unit = ka
    if b_seg:
        s, _, seg = b.shape
        n = s * seg
    else:
        n = b.shape[1]
        seg = n
    tmc = _pick(m, (1408, 384, 128))
    best = None
    for tka in _divisor_tiles(a_unit):
        for tn in _divisor_tiles(seg):
            need = 2 * tmc * tka * a.dtype.itemsize + 2 * tmc * tn * b.dtype.itemsize + tka * tn * 4 + 2 * tka * tn * 4
            if need <= TILE_BUDGET_BYTES and (best is None or (tka * tn, tn) > best[:2]):
                best = (tka * tn, tn, tka)
    _, tn, tka = best
    nm = m // tmc

    def body(a_ref, b_ref, o_ref, acc):
        mm = pl.program_id(2)

        @pl.when(mm == 0)
        def _():
            acc[...] = jnp.zeros_like(acc)

        acc[...] += lax.dot_general(a_ref[...].astype(MXU_DTYPE), b_ref[...].astype(MXU_DTYPE),
                                    (((0,), (0,)), ((), ())), preferred_element_type=F32)

        @pl.when(mm == nm - 1)
        def _():
            o_ref[...] = acc[...].astype(o_ref.dtype)

    if b_seg:
        per = seg // tn
        b_spec = pl.BlockSpec((None, tmc, tn), lambda i, j, mm: (j // per, mm, j % per))
    else:
        b_spec = pl.BlockSpec((tmc, tn), lambda i, j, mm: (mm, j))
    if a_seg:
        a_per = a_unit // tka
        a_spec = pl.BlockSpec((None, tmc, tka), lambda i, j, mm: (i // a_per, mm, i % a_per))
    else:
        a_spec = pl.BlockSpec((tmc, tka), lambda i, j, mm: (mm, i))
    return pl.pallas_call(
        body, name=name, grid=(ka // tka, n // tn, nm),
        in_specs=[a_spec, b_spec],
        out_specs=pl.BlockSpec((tka, tn), lambda i, j, mm: (i, j)),
        out_shape=jax.ShapeDtypeStruct((ka, n), out_dtype),
        scratch_shapes=[pltpu.VMEM((tka, tn), F32)],
        compiler_params=_params(("parallel", "parallel", "arbitrary")),
    )(a, b)


def _rmsnorm_bwd(dhn, h, nw, dres, *, name):
    m, d = h.shape
    tm = _pick(m, (384, 128))

    def body(dhn_ref, h_ref, nw_ref, dres_ref, dh_ref, dnw_ref):
        i = pl.program_id(0)
        x = h_ref[...]
        r = lax.rsqrt(jnp.mean(x * x, -1, keepdims=True) + EPS)
        xh = x * r
        dy = dhn_ref[...]
        dxh = dy * nw_ref[...]
        dx = r * (dxh - xh * jnp.mean(dxh * xh, -1, keepdims=True))
        row = i * tm + lax.broadcasted_iota(jnp.int32, (tm, 1), 0)
        dh_ref[...] = jnp.where(row >= PAD_ROWS, dres_ref[...] + dx, 0.0)

        @pl.when(i == 0)
        def _():
            dnw_ref[...] = jnp.zeros_like(dnw_ref)

        dnw_ref[...] += jnp.sum(dy * xh, 0, keepdims=True)

    return pl.pallas_call(
        body, name=name, grid=(m // tm,),
        in_specs=[pl.BlockSpec((tm, d), lambda i: (i, 0)), pl.BlockSpec((tm, d), lambda i: (i, 0)),
                  pl.BlockSpec((1, d), lambda i: (0, 0)), pl.BlockSpec((tm, d), lambda i: (i, 0))],
        out_specs=[pl.BlockSpec((tm, d), lambda i: (i, 0)), pl.BlockSpec((1, d), lambda i: (0, 0))],
        out_shape=[jax.ShapeDtypeStruct((m, d), F32), jax.ShapeDtypeStruct((1, d), F32)],
        compiler_params=_params(("arbitrary",)),
    )(dhn, h, nw, dres)


ROW_CHUNK = 248


def _row_chunks(m):
    out, s = [], SUBLANES
    while s < m:
        n = min(ROW_CHUNK, m - s)
        out.append((s, n))
        s += n
    return out


def _conv_at(load, w, width, s, n):
    acc = w[width - 1:width, :] * load(s, n)
    for j in range(width - 1):
        acc = acc + w[j:j + 1, :] * load(s - (width - 1 - j), n)
    return acc


def _conv_t_at(load, w, width, s, n):
    acc = w[width - 1:width, :] * load(s, n)
    for j in range(width - 1):
        acc = acc + w[j:j + 1, :] * load(s + (width - 1 - j), n)
    return acc


def _dconv_w(load_x, d, width, s, n):
    rows = [jnp.sum(d * load_x(s - (width - 1 - j), n), 0, keepdims=True) for j in range(width)]
    rows.append(jnp.zeros((SUBLANES - width, d.shape[1]), F32))
    return jnp.concatenate(rows, 0)


def _pad_w(w):
    return jnp.concatenate([w, jnp.zeros((SUBLANES - w.shape[0], w.shape[1]), w.dtype)], 0)


def _sigmoid(x):
    return 1.0 / (1.0 + jnp.exp(-x))


def _ffn_act_fwd(u, cw, *, name):
    _, m, f = u.shape
    cb = _pick(f, (256, 128))
    chunks = _row_chunks(m)

    def body(g_ref, v_ref, w_ref, o_ref):
        w = w_ref[...]
        o_ref[pl.ds(0, SUBLANES), :] = jnp.zeros((SUBLANES, cb), o_ref.dtype)
        for s, n in chunks:
            c = _conv_at(lambda a, b: g_ref[pl.ds(a, b), :], w, 3, s, n)
            o_ref[pl.ds(s, n), :] = (c * _sigmoid(c) * v_ref[pl.ds(s, n), :]).astype(o_ref.dtype)

    return pl.pallas_call(
        body, name=name, grid=(f // cb,),
        in_specs=[pl.BlockSpec((None, m, cb), lambda j: (0, 0, j)), pl.BlockSpec((None, m, cb), lambda j: (1, 0, j)),
                  pl.BlockSpec((SUBLANES, cb), lambda j: (0, j))],
        out_specs=pl.BlockSpec((m, cb), lambda j: (0, j)),
        out_shape=jax.ShapeDtypeStruct((m, f), MXU_DTYPE),
        compiler_params=_params(("parallel",)),
    )(u, u, cw)


def _ffn_act_bwd(da, u, cw, *, name):
    _, m, f = u.shape
    cb = LANES
    chunks = _row_chunks(m)

    def body(da_ref, g_ref, v_ref, w_ref, du_ref, dw_ref, dg_s):
        w = w_ref[...]
        zeros8 = jnp.zeros((SUBLANES, cb), F32)
        dg_s[pl.ds(0, SUBLANES), :] = zeros8
        dg_s[pl.ds(m, SUBLANES), :] = zeros8
        du_ref[0, pl.ds(0, SUBLANES), :] = zeros8.astype(du_ref.dtype)
        du_ref[1, pl.ds(0, SUBLANES), :] = zeros8.astype(du_ref.dtype)
        load_g = lambda a, b: g_ref[pl.ds(a, b), :]
        dw = jnp.zeros((SUBLANES, cb), F32)
        for s, n in chunks:
            c = _conv_at(load_g, w, 3, s, n)
            sg = _sigmoid(c)
            d = da_ref[pl.ds(s, n), :]
            du_ref[1, pl.ds(s, n), :] = (d * (c * sg)).astype(du_ref.dtype)
            dc = d * v_ref[pl.ds(s, n), :] * (sg * (1.0 + c * (1.0 - sg)))
            dg_s[pl.ds(s, n), :] = dc
            dw = dw + _dconv_w(load_g, dc, 3, s, n)
        dw_ref[...] = dw
        for s, n in chunks:
            du_ref[0, pl.ds(s, n), :] = _conv_t_at(lambda a, b: dg_s[pl.ds(a, b), :], w, 3, s, n).astype(du_ref.dtype)

    return pl.pallas_call(
        body, name=name, grid=(f // cb,),
        in_specs=[pl.BlockSpec((m, cb), lambda j: (0, j)), pl.BlockSpec((None, m, cb), lambda j: (0, 0, j)),
                  pl.BlockSpec((None, m, cb), lambda j: (1, 0, j)), pl.BlockSpec((SUBLANES, cb), lambda j: (0, j))],
        out_specs=[pl.BlockSpec((2, m, cb), lambda j: (0, 0, j)), pl.BlockSpec((SUBLANES, cb), lambda j: (0, j))],
        out_shape=[jax.ShapeDtypeStruct((2, m, f), MXU_DTYPE), jax.ShapeDtypeStruct((SUBLANES, f), F32)],
        scratch_shapes=[pltpu.VMEM((m + SUBLANES, cb), F32)],
        compiler_params=_params(("parallel",)),
    )(da, u, u, cw)


def _shortconv_fwd(pm, cw, y, *, name):
    _, m, seg = pm.shape
    cb = _pick(seg, (256, 128))
    chunks = _row_chunks(m)

    def body(gi_ref, go_ref, ah_ref, w_ref, y_in, o_ref):
        del y_in
        w = w_ref[...]
        o_ref[pl.ds(0, SUBLANES), :] = jnp.zeros((SUBLANES, cb), o_ref.dtype)
        load_m = lambda a, b: gi_ref[pl.ds(a, b), :] * ah_ref[pl.ds(a, b), :]
        for s, n in chunks:
            o_ref[pl.ds(s, n), :] = (go_ref[pl.ds(s, n), :] * _conv_at(load_m, w, 3, s, n)).astype(o_ref.dtype)

    return pl.pallas_call(
        body, name=name, grid=(seg // cb,),
        in_specs=[pl.BlockSpec((None, m, cb), lambda j: (0, 0, j)), pl.BlockSpec((None, m, cb), lambda j: (1, 0, j)),
                  pl.BlockSpec((None, m, cb), lambda j: (2, 0, j)), pl.BlockSpec((SUBLANES, cb), lambda j: (0, j)),
                  pl.BlockSpec(memory_space=pl.ANY)],
        out_specs=pl.BlockSpec((m, cb), lambda j: (0, j)),
        out_shape=jax.ShapeDtypeStruct(y.shape, y.dtype),
        input_output_aliases={4: 0},
        compiler_params=_params(("parallel",)),
    )(pm, pm, pm, cw, y)


def _shortconv_bwd(dy, pm, cw, dpm, *, name):
    _, m, seg = pm.shape
    cb = LANES
    chunks = _row_chunks(m)

    def body(dy_ref, gi_ref, go_ref, ah_ref, w_ref, dpm_in, dp_ref, dw_ref, dc_s):
        del dpm_in
        w = w_ref[...]
        zeros8 = jnp.zeros((SUBLANES, cb), F32)
        dc_s[pl.ds(0, SUBLANES), :] = zeros8
        dc_s[pl.ds(m, SUBLANES), :] = zeros8
        for t in range(3):
            dp_ref[t, pl.ds(0, SUBLANES), :] = zeros8.astype(dp_ref.dtype)
        load_m = lambda a, b: gi_ref[pl.ds(a, b), :] * ah_ref[pl.ds(a, b), :]
        dw = jnp.zeros((SUBLANES, cb), F32)
        for s, n in chunks:
            d = dy_ref[pl.ds(s, n), :]
            dp_ref[1, pl.ds(s, n), :] = (d * _conv_at(load_m, w, 3, s, n)).astype(dp_ref.dtype)
            dc = d * go_ref[pl.ds(s, n), :]
            dc_s[pl.ds(s, n), :] = dc
            dw = dw + _dconv_w(load_m, dc, 3, s, n)
        dw_ref[...] = dw
        for s, n in chunks:
            dm = _conv_t_at(lambda a, b: dc_s[pl.ds(a, b), :], w, 3, s, n)
            dp_ref[0, pl.ds(s, n), :] = (dm * ah_ref[pl.ds(s, n), :]).astype(dp_ref.dtype)
            dp_ref[2, pl.ds(s, n), :] = (dm * gi_ref[pl.ds(s, n), :]).astype(dp_ref.dtype)

    return pl.pallas_call(
        body, name=name, grid=(seg // cb,),
        in_specs=[pl.BlockSpec((m, cb), lambda j: (0, j)), pl.BlockSpec((None, m, cb), lambda j: (0, 0, j)),
                  pl.BlockSpec((None, m, cb), lambda j: (1, 0, j)), pl.BlockSpec((None, m, cb), lambda j: (2, 0, j)),
                  pl.BlockSpec((SUBLANES, cb), lambda j: (0, j)), pl.BlockSpec(memory_space=pl.ANY)],
        out_specs=[pl.BlockSpec((3, m, cb), lambda j: (0, 0, j)), pl.BlockSpec((SUBLANES, cb), lambda j: (0, j))],
        out_shape=[jax.ShapeDtypeStruct(dpm.shape, dpm.dtype), jax.ShapeDtypeStruct((SUBLANES, seg), F32)],
        scratch_shapes=[pltpu.VMEM((m + SUBLANES, cb), F32)],
        input_output_aliases={5: 0},
        compiler_params=_params(("parallel",)),
    )(dy, pm, pm, pm, cw, dpm)


def _dnpre_fwd(pm, cw, *, name):
    _, m, seg = pm.shape
    cb = _pick(seg, (256, 128))
    per = seg // cb
    chunks = _row_chunks(m)

    def body(x_ref, w_ref, o_ref):
        w = w_ref[...]
        o_ref[pl.ds(0, SUBLANES), :] = jnp.zeros((SUBLANES, cb), F32)
        for s, n in chunks:
            c = _conv_at(lambda a, b: x_ref[pl.ds(a, b), :], w, 4, s, n)
            o_ref[pl.ds(s, n), :] = c * _sigmoid(c)

    return pl.pallas_call(
        body, name=name, grid=(3 * per,),
        in_specs=[pl.BlockSpec((None, m, cb), lambda j: (3 + j // per, 0, j % per)), pl.BlockSpec((SUBLANES, cb), lambda j: (0, j))],
        out_specs=pl.BlockSpec((None, m, cb), lambda j: (j // per, 0, j % per)),
        out_shape=jax.ShapeDtypeStruct((3, m, seg), F32),
        compiler_params=_params(("parallel",)),
    )(pm, cw)


def _dnpre_bwd(dqkv, pm, cw, dpm, *, name):
    _, m, seg = pm.shape
    cb = _pick(seg, (256, 128))
    per = seg // cb
    chunks = _row_chunks(m)

    def body(d_ref, x_ref, w_ref, dpm_in, dp_ref, dw_ref, dc_s):
        del dpm_in
        w = w_ref[...]
        zeros8 = jnp.zeros((SUBLANES, cb), F32)
        dc_s[pl.ds(0, SUBLANES), :] = zeros8
        dc_s[pl.ds(m, SUBLANES), :] = zeros8
        dp_ref[pl.ds(0, SUBLANES), :] = zeros8.astype(dp_ref.dtype)
        load_x = lambda a, b: x_ref[pl.ds(a, b), :]
        dw = jnp.zeros((SUBLANES, cb), F32)
        for s, n in chunks:
            c = _conv_at(load_x, w, 4, s, n)
            sg = _sigmoid(c)
            dc = d_ref[pl.ds(s, n), :] * (sg * (1.0 + c * (1.0 - sg)))
            dc_s[pl.ds(s, n), :] = dc
            dw = dw + _dconv_w(load_x, dc, 4, s, n)
        dw_ref[...] = dw
        for s, n in chunks:
            dp_ref[pl.ds(s, n), :] = _conv_t_at(lambda a, b: dc_s[pl.ds(a, b), :], w, 4, s, n).astype(dp_ref.dtype)

    return pl.pallas_call(
        body, name=name, grid=(3 * per,),
        in_specs=[pl.BlockSpec((None, m, cb), lambda j: (j // per, 0, j % per)),
                  pl.BlockSpec((None, m, cb), lambda j: (3 + j // per, 0, j % per)),
                  pl.BlockSpec((SUBLANES, cb), lambda j: (0, j)), pl.BlockSpec(memory_space=pl.ANY)],
        out_specs=[pl.BlockSpec((None, m, cb), lambda j: (3 + j // per, 0, j % per)), pl.BlockSpec((SUBLANES, cb), lambda j: (0, j))],
        out_shape=[jax.ShapeDtypeStruct(dpm.shape, dpm.dtype), jax.ShapeDtypeStruct((SUBLANES, 3 * seg), F32)],
        scratch_shapes=[pltpu.VMEM((m + SUBLANES, cb), F32)],
        input_output_aliases={3: 0},
        compiler_params=_params(("parallel",)),
    )(dqkv, pm, cw, dpm)


def _mxu_dot_impl(a, b, form):
    a = a.astype(MXU_DTYPE)
    b = b.astype(MXU_DTYPE)
    dims = {"nn": (((1,), (0,)), ((), ())), "nt": (((1,), (1,)), ((), ())), "tn": (((0,), (0,)), ((), ()))}[form]
    return lax.dot_general(a, b, dims, preferred_element_type=F32)


@functools.partial(jax.custom_vjp, nondiff_argnums=(2,))
def _mxu_dot(a, b, form):
    return _mxu_dot_impl(a, b, form)


def _mxu_dot_fwd(a, b, form):
    return _mxu_dot_impl(a, b, form), (a, b)


def _mxu_dot_bwd(form, saved, g):
    a, b = saved
    if form == "nn":
        return _mxu_dot_impl(g, b, "nt"), _mxu_dot_impl(a, g, "tn")
    if form == "nt":
        return _mxu_dot_impl(g, b, "nn"), _mxu_dot_impl(g, a, "tn")
    return _mxu_dot_impl(b, g, "nt"), _mxu_dot_impl(a, g, "nn")


_mxu_dot.defvjp(_mxu_dot_fwd, _mxu_dot_bwd)


_DOT_DIMS = {"nn": (((1,), (0,)), ((), ())), "nt": (((1,), (1,)), ((), ())), "tn": (((0,), (0,)), ((), ()))}


def _split(x):
    hi = x.astype(BF16)
    return hi, (x - hi.astype(F32)).astype(BF16)


def _dot3_impl(a, b, form):
    dg = lambda p, q: lax.dot_general(p, q, _DOT_DIMS[form], preferred_element_type=F32)
    ah, al = _split(a)
    bh, bl = _split(b)
    return dg(ah, bh) + (dg(ah, bl) + dg(al, bh))


@functools.partial(jax.custom_vjp, nondiff_argnums=(2,))
def _dot3(a, b, form):
    return _dot3_impl(a, b, form)


def _dot3_fwd(a, b, form):
    return _dot3_impl(a, b, form), (a, b)


def _dot3_bwd(form, saved, g):
    a, b = saved
    if form == "nn":
        return _dot3_impl(g, b, "nt"), _dot3_impl(a, g, "tn")
    if form == "nt":
        return _dot3_impl(g, b, "nn"), _dot3_impl(g, a, "tn")
    return _dot3_impl(b, g, "nt"), _dot3_impl(a, g, "nn")


_dot3.defvjp(_dot3_fwd, _dot3_bwd)


def _hdot(a, b):
    return _dot3(a, b, "nn")


def _mask_dot(mask, x, form):
    dg = lambda q: lax.dot_general(mask.astype(BF16), q, _DOT_DIMS[form], preferred_element_type=F32)
    x1 = x.astype(BF16)
    r1 = x - x1.astype(F32)
    x2 = r1.astype(BF16)
    x3 = (r1 - x2.astype(F32)).astype(BF16)
    return dg(x1) + (dg(x2) + dg(x3))


def _decay_masks(c):
    row = lax.broadcasted_iota(jnp.int32, (c, c), 0)
    col = lax.broadcasted_iota(jnp.int32, (c, c), 1)
    return (row >= col).astype(F32), row <= col


def _decay_impl(gb):
    lower, upper = _decay_masks(gb.shape[0])
    return _mask_dot(lower, gb, "nn"), _mask_dot(jnp.ones_like(gb), jnp.where(upper, gb, 0.0), "nn")


@jax.custom_vjp
def _decay_matrices(gb):
    return _decay_impl(gb)


def _decay_fwd(gb):
    return _decay_impl(gb), None


def _decay_bwd(_, cts):
    gc, gr = cts
    lower, upper = _decay_masks(gc.shape[0])
    return (_mask_dot(lower, gc, "tn") + jnp.where(upper, _mask_dot(jnp.ones_like(gr), gr, "tn"), 0.0),)


_decay_matrices.defvjp(_decay_fwd, _decay_bwd)


def _heads(f, *lists):
    return [f(*t) for t in zip(*lists)]


def _inverses_impl(a):
    c = a[0].shape[0]
    row = lax.broadcasted_iota(jnp.int32, (c, c), 0)
    col = lax.broadcasted_iota(jnp.int32, (c, c), 1)
    eye = jnp.where(row == col, 1.0, 0.0)
    x = _heads(lambda t: eye - t, a)
    p = _heads(lambda t: _dot3_impl(t, t, "nn"), a)
    power = 2
    while power < c:
        x = _heads(lambda s, t: s + _dot3_impl(s, t, "nn"), x, p)
        power *= 2
        if power < c:
            p = _heads(lambda t: _dot3_impl(t, t, "nn"), p)
    return x


@jax.custom_vjp
def _unit_lower_inverses(a):
    return _inverses_impl(a)


def _unit_lower_inverses_fwd(a):
    x = _inverses_impl(a)
    return x, x


def _unit_lower_inverses_bwd(x, g):
    t = _heads(lambda s, u: _dot3_impl(s, u, "tn"), x, g)
    return (_heads(lambda u, s: -_dot3_impl(u, s, "nt"), t, x),)


_unit_lower_inverses.defvjp(_unit_lower_inverses_fwd, _unit_lower_inverses_bwd)


def _softplus(x):
    return jnp.maximum(x, 0.0) + jnp.log(1.0 + jnp.exp(-jnp.abs(x)))


DN_STEP_CHUNKS = 6


def _dn_chunk(qr, kr, v, z, braw, araw, alog, dtb, nw, state, valid):
    c = DN_CHUNK
    nh = len(state)
    chunks = len(qr) // nh
    alog, dtb, valid_i = alog * chunks, dtb * chunks, [vv for vv in valid for _ in range(nh)]
    row = lax.broadcasted_iota(jnp.int32, (c, c), 0)
    col = lax.broadcasted_iota(jnp.int32, (c, c), 1)
    incl = row >= col
    strict = row > col
    q = _heads(lambda t: t * lax.rsqrt(jnp.sum(t * t, -1, keepdims=True) + EPS) * (DN_HEAD_DIM ** -0.5), qr)
    k = _heads(lambda t: t * lax.rsqrt(jnp.sum(t * t, -1, keepdims=True) + EPS), kr)
    beta = _heads(lambda t, vv: _sigmoid(t) * vv, braw, valid_i)
    g = _heads(lambda al, ar, dt, vv: -jnp.exp(al) * _softplus(ar + dt) * vv, alog, araw, dtb, valid_i)
    decay = _heads(lambda t: _decay_matrices(jnp.broadcast_to(t, (c, c))), g)
    dmask = _heads(lambda d: jnp.where(incl, jnp.exp(jnp.where(incl, d[0] - d[1], 0.0)), 0.0), decay)
    dec = _heads(lambda d: d[0][:, :1], decay)
    dlast = _heads(lambda d: d[0][c - 1:c, :1], decay)
    kk = _heads(lambda t: _mxu_dot(t, t, "nt"), k)
    a = _heads(lambda b, t, d: jnp.where(strict, b * t * d, 0.0), beta, kk, dmask)
    x = _unit_lower_inverses(a)
    uw = _heads(lambda s, vv, kk_, b, d: _hdot(s, jnp.concatenate([vv * b, kk_ * (b * jnp.exp(d))], 1)), x, v, k, beta, dec)
    u = _heads(lambda t: t[:, :DN_HEAD_DIM], uw)
    w = _heads(lambda t: t[:, DN_HEAD_DIM:], uw)
    qk = _heads(lambda s, t, d: _mxu_dot(s, t, "nt") * d, q, k, dmask)
    q_dec = _heads(lambda t, d: t * jnp.exp(d), q, dec)
    k_dec = _heads(lambda t, dl, d: t * jnp.exp(dl - d), k, dlast, dec)
    o = []
    for ci in range(chunks):
        of = lambda lst: lst[ci * nh:(ci + 1) * nh]
        v_new = _heads(lambda s, t, st: s - _mxu_dot(t, st, "nn"), of(u), of(w), state)
        o += _heads(lambda qd, st, s, vn: _mxu_dot(qd, st, "nn") + _mxu_dot(s, vn, "nn"), of(q_dec), state, of(qk), v_new)
        state = _heads(lambda st, dl, kd, vn: st * jnp.exp(dl) + _mxu_dot(kd, vn, "tn"), state, of(dlast), of(k_dec), v_new)
    y = _heads(lambda t, zz: _rms(t, nw) * (zz * _sigmoid(zz)), o, z)
    return y, state


DN_STEP_ROWS = DN_STEP_CHUNKS * DN_CHUNK
DN_ITEMS = [(ci, h) for ci in range(DN_STEP_CHUNKS) for h in range(DN_HEADS)]


def _dn_valid(n):
    rows = [n * DN_STEP_ROWS + ci * DN_CHUNK + lax.broadcasted_iota(jnp.int32, (DN_CHUNK, 1), 0) for ci in range(DN_STEP_CHUNKS)]
    return [(r >= PAD_ROWS).astype(F32) for r in rows]


def _dn_in_specs(rev, nc):
    cn = (lambda n: nc - 1 - n) if rev else (lambda n: n)
    c, hd = DN_STEP_ROWS, DN_HEAD_DIM
    return [
        pl.BlockSpec((None, c, DN_DIM), lambda n: (0, cn(n), 0)),
        pl.BlockSpec((None, c, DN_DIM), lambda n: (1, cn(n), 0)),
        pl.BlockSpec((None, c, DN_DIM), lambda n: (2, cn(n), 0)),
        pl.BlockSpec((None, c, DN_DIM), lambda n: (6, cn(n), 0)),
        pl.BlockSpec((DN_HEADS, 2, c, 1), lambda n: (0, 0, cn(n), 0)),
        pl.BlockSpec((DN_HEADS, SUBLANES, LANES), lambda n: (0, 0, 0)),
        pl.BlockSpec((1, hd), lambda n: (0, 0)),
    ]


def _rows(ci):
    return slice(ci * DN_CHUNK, (ci + 1) * DN_CHUNK)


def _cols(h):
    return slice(h * DN_HEAD_DIM, (h + 1) * DN_HEAD_DIM)


def _dn_load(q_ref, k_ref, v_ref, z_ref, ba_ref, hp_ref):
    heads = range(DN_HEADS)
    item = lambda ref: [ref[_rows(ci), _cols(h)] for ci, h in DN_ITEMS]
    return (item(q_ref), item(k_ref), item(v_ref), item(z_ref),
            [ba_ref[h, 0, _rows(ci), :] for ci, h in DN_ITEMS], [ba_ref[h, 1, _rows(ci), :] for ci, h in DN_ITEMS],
            [hp_ref[h, 0:1, 0:1] for h in heads], [hp_ref[h, 1:2, 0:1] for h in heads])


def _delta_fwd(qkvc, pm, ba, hp, nw, *, name):
    _, m, _ = qkvc.shape
    nc = m // DN_STEP_ROWS
    hd = DN_HEAD_DIM

    def body(q_ref, k_ref, v_ref, z_ref, ba_ref, hp_ref, nw_ref, y_ref, s_ref, state):
        n = pl.program_id(0)

        @pl.when(n == 0)
        def _():
            state[...] = jnp.zeros_like(state)

        heads = range(DN_HEADS)
        old = [state[h] for h in heads]
        y, new = _dn_chunk(*_dn_load(q_ref, k_ref, v_ref, z_ref, ba_ref, hp_ref), nw_ref[...], old, _dn_valid(n))
        for h in heads:
            s_ref[h] = old[h]
            state[h] = new[h]
        for (ci, h), yy in zip(DN_ITEMS, y):
            y_ref[_rows(ci), _cols(h)] = yy.astype(y_ref.dtype)

    return pl.pallas_call(
        body, name=name, grid=(nc,), in_specs=_dn_in_specs(False, nc),
        out_specs=[pl.BlockSpec((DN_STEP_ROWS, DN_DIM), lambda n: (n, 1)), pl.BlockSpec((DN_HEADS, None, hd, hd), lambda n: (0, n, 0, 0))],
        out_shape=[jax.ShapeDtypeStruct((m, D_CONV + DN_DIM), MXU_DTYPE), jax.ShapeDtypeStruct((DN_HEADS, nc, hd, hd), F32)],
        scratch_shapes=[pltpu.VMEM((DN_HEADS, hd, hd), F32)],
        compiler_params=_params(("arbitrary",)),
    )(qkvc, qkvc, qkvc, pm, ba, hp, nw)


def _delta_bwd(dy, qkvc, pm, ba, hp, nw, states, *, name):
    _, m, _ = qkvc.shape
    nc = m // DN_STEP_ROWS
    hd, c = DN_HEAD_DIM, DN_STEP_ROWS

    def body(q_ref, k_ref, v_ref, z_ref, ba_ref, hp_ref, nw_ref, s_ref, dy_ref,
             dz_ref, dqkv_ref, dba_ref, dhp_ref, dnw_ref, dstate):
        step = pl.program_id(0)
        n = nc - 1 - step

        @pl.when(step == 0)
        def _():
            dstate[...] = jnp.zeros_like(dstate)
            dhp_ref[...] = jnp.zeros_like(dhp_ref)
            dnw_ref[...] = jnp.zeros_like(dnw_ref)

        valid = _dn_valid(n)
        heads = range(DN_HEADS)
        fn = lambda *a: _dn_chunk(*a, valid)
        _, vjp = jax.vjp(fn, *_dn_load(q_ref, k_ref, v_ref, z_ref, ba_ref, hp_ref), nw_ref[...], [s_ref[h] for h in heads])
        dy = [dy_ref[_rows(ci), _cols(h)] for ci, h in DN_ITEMS]
        dq, dk, dv, dz, dbr, dar, dalog, ddtb, dnw, dst = vjp((dy, [dstate[h] for h in heads]))
        for i, (ci, h) in enumerate(DN_ITEMS):
            dqkv_ref[0, _rows(ci), _cols(h)] = dq[i]
            dqkv_ref[1, _rows(ci), _cols(h)] = dk[i]
            dqkv_ref[2, _rows(ci), _cols(h)] = dv[i]
            dz_ref[_rows(ci), _cols(h)] = dz[i].astype(dz_ref.dtype)
            dba_ref[h, 0, _rows(ci), :] = dbr[i]
            dba_ref[h, 1, _rows(ci), :] = dar[i]
        for h in heads:
            dstate[h] = dst[h]
            dhp_ref[h] += jnp.concatenate([jnp.broadcast_to(dalog[h], (1, LANES)), jnp.broadcast_to(ddtb[h], (1, LANES)),
                                           jnp.zeros((SUBLANES - 2, LANES), F32)], 0)
        dnw_ref[...] += dnw

    rn = lambda n: nc - 1 - n
    in_specs = _dn_in_specs(True, nc) + [
        pl.BlockSpec((DN_HEADS, None, hd, hd), lambda n: (0, rn(n), 0, 0)),
        pl.BlockSpec((c, DN_DIM), lambda n: (rn(n), 1)),
    ]
    out_specs = [
        pl.BlockSpec((None, c, DN_DIM), lambda n: (6, rn(n), 0)),
        pl.BlockSpec((3, c, DN_DIM), lambda n: (0, rn(n), 0)),
        pl.BlockSpec((DN_HEADS, 2, c, 1), lambda n: (0, 0, rn(n), 0)),
        pl.BlockSpec((DN_HEADS, SUBLANES, LANES), lambda n: (0, 0, 0)),
        pl.BlockSpec((1, hd), lambda n: (0, 0)),
    ]
    return pl.pallas_call(
        body, name=name, grid=(nc,), in_specs=in_specs, out_specs=out_specs,
        out_shape=[jax.ShapeDtypeStruct(pm.shape, MXU_DTYPE), jax.ShapeDtypeStruct(qkvc.shape, F32),
                   jax.ShapeDtypeStruct(ba.shape, F32), jax.ShapeDtypeStruct(hp.shape, F32),
                   jax.ShapeDtypeStruct((1, hd), F32)],
        scratch_shapes=[pltpu.VMEM((DN_HEADS, hd, hd), F32)],
        compiler_params=_params(("arbitrary",)),
    )(qkvc, qkvc, qkvc, pm, ba, hp, nw, states, dy)


SWA_PAIR = 4


def _attn_block(q, k0, kp, kc, v0, vp, vc, qw, kw, sink, n):
    g, b, hd = SWA_GROUP, SWA_BLOCK, SWA_HEAD_DIM
    pair = list(range(SWA_PAIR))
    lanes = lambda t, e: t[:, e * hd:(e + 1) * hd]
    q4 = [jnp.concatenate([lanes(q, e * g + i)[None] for i in range(g)], 0) for e in pair]
    qn = _heads(lambda t: _rms(t, qw) * (hd ** -0.5), q4)
    kn = [_rms(jnp.concatenate([lanes(k0, e), lanes(kp, e), lanes(kc, e)], 0), kw) for e in pair]
    vcat = [jnp.concatenate([lanes(v0, e), lanes(vp, e), lanes(vc, e)], 0) for e in pair]
    s = _heads(lambda a, k: _mxu_dot(a.reshape(g * b, hd), k, "nt").reshape(g, b, 3 * b), qn, kn)
    i = lax.broadcasted_iota(jnp.int32, (b, 3 * b), 0)
    c = lax.broadcasted_iota(jnp.int32, (b, 3 * b), 1)
    in_meta, in_prev, in_cur = c < b, (c >= b) & (c < 2 * b), c >= 2 * b
    j = c - jnp.where(in_meta, 0, jnp.where(in_prev, b, 2 * b))
    meta_lo = jnp.where(n == 0, b, PAD_ROWS)
    cur_lo = jnp.where(n == 0, PAD_ROWS, 0)
    prev_off = jnp.where(n >= 2, 0, 2 * b)
    valid = (in_meta & (j >= meta_lo)) | (in_prev & (j > i + prev_off)) | (in_cur & (j <= i) & (j >= cur_lo))
    s = _heads(lambda t: jnp.where(valid[None], t, NEG), s)
    m = [lax.stop_gradient(jnp.maximum(jnp.max(t, -1, keepdims=True), sink[e])) for e, t in zip(pair, s)]
    ex = _heads(lambda t, mm: jnp.exp(t - mm), s, m)
    p = [t / (jnp.sum(t, -1, keepdims=True) + jnp.exp(sink[e] - mm)) for e, t, mm in zip(pair, ex, m)]
    o = _heads(lambda t, v: _mxu_dot(t.reshape(g * b, 3 * b), v, "nn").reshape(g, b, hd), p, vcat)
    return jnp.concatenate([o[e][i] for e in pair for i in range(g)], 1)


Q_LANES = SWA_PAIR * SWA_GROUP * SWA_HEAD_DIM
KV_LANES = SWA_PAIR * SWA_HEAD_DIM
K_BLOCK0 = SWA_HEADS * SWA_HEAD_DIM // KV_LANES
V_BLOCK0 = K_BLOCK0 + SWA_KV_HEADS * SWA_HEAD_DIM // KV_LANES


def _attn_in_specs():
    g, b, hd = SWA_GROUP, SWA_BLOCK, SWA_HEAD_DIM
    kv = lambda f, first: pl.BlockSpec((b, KV_LANES), lambda p, n: (f(n), first + p))
    blocks = [lambda n: 0, lambda n: jnp.maximum(n - 1, 0), lambda n: n]
    return ([pl.BlockSpec((b, Q_LANES), lambda p, n: (n, p))] + [kv(f, K_BLOCK0) for f in blocks] + [kv(f, V_BLOCK0) for f in blocks]
            + [pl.BlockSpec((1, hd), lambda p, n: (0, 0)), pl.BlockSpec((1, hd), lambda p, n: (0, 0)),
               pl.BlockSpec((SWA_PAIR, g, 1, 1), lambda p, n: (p, 0, 0, 0))])


def _attn_fwd(qkv, qw, kw, sink, *, name):
    m = qkv.shape[0]
    b = SWA_BLOCK

    def body(q_ref, k0, kp, kc, v0, vp, vc, qw_ref, kw_ref, s_ref, o_ref):
        o_ref[...] = _attn_block(q_ref[...], k0[...], kp[...], kc[...], v0[...], vp[...], vc[...], qw_ref[...], kw_ref[...],
                                 s_ref[...], pl.program_id(1)).astype(o_ref.dtype)

    return pl.pallas_call(
        body, name=name, grid=(SWA_KV_HEADS // SWA_PAIR, m // b), in_specs=_attn_in_specs(),
        out_specs=pl.BlockSpec((b, Q_LANES), lambda p, n: (n, p)),
        out_shape=jax.ShapeDtypeStruct((m, SWA_HEADS * SWA_HEAD_DIM), MXU_DTYPE),
        compiler_params=_params(("parallel", "parallel")),
    )(*([qkv] * 7), qw, kw, sink)


def _attn_bwd(do, qkv, qw, kw, sink, *, name):
    m = qkv.shape[0]
    g, b, hd = SWA_GROUP, SWA_BLOCK, SWA_HEAD_DIM

    def body(q_ref, k0, kp, kc, v0, vp, vc, qw_ref, kw_ref, s_ref, do_ref, dq_ref, dk_ref, dv_ref, dqw_ref, dkw_ref, ds_ref):
        n = pl.program_id(1)

        @pl.when(n == 0)
        def _():
            for r in (dk_ref, dv_ref, dqw_ref, dkw_ref, ds_ref):
                r[...] = jnp.zeros_like(r)

        fn = lambda *a: _attn_block(*a, n)
        _, vjp = jax.vjp(fn, q_ref[...], k0[...], kp[...], kc[...], v0[...], vp[...], vc[...], qw_ref[...], kw_ref[...], s_ref[...])
        dq, dk0, dkp, dkc, dv0, dvp, dvc, dqw, dkw, dsk = vjp(do_ref[...])
        dq_ref[...] = dq
        prev = pl.multiple_of(jnp.maximum(n - 1, 0) * b, b)
        cur = pl.multiple_of(n * b, b)
        for ref, parts in ((dk_ref, (dk0, dkp, dkc)), (dv_ref, (dv0, dvp, dvc))):
            ref[pl.ds(0, b), :] += parts[0]
            ref[pl.ds(prev, b), :] += parts[1]
            ref[pl.ds(cur, b), :] += parts[2]
        dqw_ref[...] += dqw
        dkw_ref[...] += dkw
        ds_ref[...] += dsk

    pairs = SWA_KV_HEADS // SWA_PAIR
    kv_acc = pl.BlockSpec((m, KV_LANES), lambda p, n: (0, p))
    w_acc = pl.BlockSpec((None, 1, hd), lambda p, n: (p, 0, 0))
    kv_shape = jax.ShapeDtypeStruct((m, SWA_KV_HEADS * hd), F32)
    return pl.pallas_call(
        body, name=name, grid=(pairs, m // b),
        in_specs=_attn_in_specs() + [pl.BlockSpec((b, Q_LANES), lambda p, n: (n, p))],
        out_specs=[pl.BlockSpec((b, Q_LANES), lambda p, n: (n, p)), kv_acc, kv_acc, w_acc, w_acc,
                   pl.BlockSpec((SWA_PAIR, g, 1, 1), lambda p, n: (p, 0, 0, 0))],
        out_shape=[jax.ShapeDtypeStruct((m, SWA_HEADS * hd), F32), kv_shape, kv_shape,
                   jax.ShapeDtypeStruct((pairs, 1, hd), F32), jax.ShapeDtypeStruct((pairs, 1, hd), F32),
                   jax.ShapeDtypeStruct(sink.shape, F32)],
        compiler_params=_params(("parallel", "arbitrary")),
    )(*([qkv] * 7), qw, kw, sink, do)


def _loss_bwd(h, target, *, name):
    m, d = h.shape
    b = SWA_BLOCK

    def body(h_ref, t_ref, l_ref, dh_ref):
        i = pl.program_id(0)

        @pl.when(i == 0)
        def _():
            l_ref[...] = jnp.zeros_like(l_ref)
            dh_ref[...] = jnp.zeros_like(dh_ref)

        @pl.when(i > 0)
        def _():
            e = h_ref[...] - t_ref[...]
            dh_ref[...] = e * (1.0 / d)
            l_ref[...] += jnp.sum(jnp.sum(e * e, 0, keepdims=True), 1, keepdims=True) * (0.5 / d)

    return pl.pallas_call(
        body, name=name, grid=(m // b,),
        in_specs=[pl.BlockSpec((b, d), lambda i: (i, 0)), pl.BlockSpec((b, d), lambda i: (jnp.maximum(i - 1, 0), 0))],
        out_specs=[pl.BlockSpec((1, LANES), lambda i: (0, 0)), pl.BlockSpec((b, d), lambda i: (i, 0))],
        out_shape=[jax.ShapeDtypeStruct((1, LANES), F32), jax.ShapeDtypeStruct((m, d), F32)],
        compiler_params=_params(("arbitrary",)),
    )(h, target)


def _ffn_fwd(h, nw, w_up_t, cw, w_down, tag):
    u, hn = _norm_matmul(h, nw, w_up_t, o_seg=D_FF, trans_w=True, name=f"ffn_up_{tag}")
    a = _ffn_act_fwd(u, cw, name=f"ffn_act_{tag}")
    return _mm_nn(a, w_down, res=h, name=f"ffn_down_{tag}"), (h, hn, u, a)


def _ffn_bwd(dh, saved, nw, w_up_t, cw, w_down, tag):
    h, hn, u, a = saved
    da = _mm_nn(dh, w_down, trans_w=True, name=f"ffn_da_{tag}")
    dw_down = _mm_tn(a, dh, name=f"ffn_dwdown_{tag}")
    du, dcw = _ffn_act_bwd(da, u, cw, name=f"ffn_act_bwd_{tag}")
    dhn = _mm_nn(du, w_up_t, a_seg=True, name=f"ffn_dhn_{tag}")
    dw_up_t = _mm_tn(du, hn, a_seg=True, name=f"ffn_dwup_{tag}")
    dh_in, dnw = _rmsnorm_bwd(dhn, h, nw, dh, name=f"ffn_norm_bwd_{tag}")
    return dh_in, dnw, dw_up_t, dcw, dw_down


def _local_step(x, target, w, fetch=None, push=None):
    fetch = fetch or (lambda stage, after: {})
    push = push or (lambda stage, grads: None)
    plus = lambda a, zero: a if zero is None else a + zero
    seq, d = x.shape
    m = PAD_ROWS + N_META + seq
    h0 = jnp.concatenate([jnp.zeros((PAD_ROWS, d), F32), w["meta"], x], 0)

    pm, hn0 = _norm_matmul(h0, w["anw"][0], w["w_in_t"], o_seg=SEG, trans_w=True, n=N_SEG * SEG, name="mix_in")
    pba = _mm_nn(hn0, w["w_in_tail_t"], trans_w=True, name="mix_in_tail")
    qkvc = _dnpre_fwd(pm, w["dcw"], name="dn_conv")
    ba = pba[:, :2 * DN_HEADS].T.reshape(2, DN_HEADS, m, 1).transpose(1, 0, 2, 3)
    y, states = _delta_fwd(qkvc, pm, ba, w["hp"], w["dnw"], name="delta")
    y = _shortconv_fwd(pm, w["caw"], y, name="shortconv")
    w = {**w, **fetch("l0", y)}
    h1 = _mm_nn(y, w["w_out"], res=h0, name="mix_out")
    h2, ffn0 = _ffn_fwd(h1, w["fnw"][0], w["w_up0_t"], w["fcw"][0], w["w_down0"], "l0")

    w = {**w, **fetch("l1", h2)}
    qkv, hn2 = _norm_matmul(h2, w["anw"][1], w["wqkv"], name="attn_qkv")
    o = _attn_fwd(qkv, w["qnw"], w["knw"], w["sink"], name="attn")
    h3 = _mm_nn(o, w["wo"], res=h2, name="attn_out")
    h4, ffn1 = _ffn_fwd(h3, w["fnw"][1], w["w_up1_t"], w["fcw"][1], w["w_down1"], "l1")

    loss, dh4 = _loss_bwd(h4, target, name="loss")

    g = {}
    dh3, dfnw1, dwup1, dfcw1, dwdown1 = _ffn_bwd(dh4, ffn1, w["fnw"][1], w["w_up1_t"], w["fcw"][1], w["w_down1"], "l1")

    do = _mm_nn(dh3, w["wo"], trans_w=True, name="attn_do")
    g["wo"] = _mm_tn(o, dh3, name="attn_dwo")
    dq, dk, dv, dqw, dkw, dsink = _attn_bwd(do, qkv, w["qnw"], w["knw"], w["sink"], name="attn_bwd")
    dqkv = jnp.concatenate([dq, dk, dv], 1).astype(MXU_DTYPE)
    dhn2 = _mm_nn(dqkv, w["wqkv"], trans_w=True, name="attn_dhn")
    g["wqkv"] = _mm_tn(hn2, dqkv, name="attn_dwqkv")
    zero = push("l1", dict(w_up_t=dwup1, w_down=dwdown1, wo=g["wo"], wqkv=g["wqkv"]))
    dh2, danw1 = _rmsnorm_bwd(dhn2, h2, plus(w["anw"][1], zero), dh3, name="attn_norm_bwd")

    dh1, dfnw0, dwup0, dfcw0, dwdown0 = _ffn_bwd(dh2, ffn0, w["fnw"][0], w["w_up0_t"], w["fcw"][0], w["w_down0"], "l0")

    dy = _mm_nn(dh1, w["w_out"], trans_w=True, name="mix_dy")
    g["w_out"] = _mm_tn(y, dh1, name="mix_dwout")
    zero = push("l0", dict(w_up_t=dwup0, w_down=dwdown0, w_out=g["w_out"]))
    dpm, dqkvc, dba, dhp, ddnw = _delta_bwd(dy, qkvc, pm, ba, w["hp"], plus(w["dnw"], zero), states, name="delta_bwd")
    dpm, ddcw = _dnpre_bwd(dqkvc, pm, w["dcw"], dpm, name="dn_conv_bwd")
    dpm, dcaw = _shortconv_bwd(dy, pm, w["caw"], dpm, name="shortconv_bwd")
    dpba = jnp.pad(dba.transpose(1, 0, 2, 3).reshape(2 * DN_HEADS, m).T, ((0, 0), (0, LANES - 2 * DN_HEADS))).astype(MXU_DTYPE)
    g["w_in_t"] = jnp.concatenate([_mm_tn(dpm, hn0, a_seg=True, out_dtype=F32, name="mix_dwin"),
                                   _mm_tn(dpba, hn0, out_dtype=F32, name="mix_dwin_tail")[:N_TAIL]], 0).astype(GRAD_WIRE_DTYPE)
    zero = push("in", dict(w_in_t=g["w_in_t"]))
    dhn0 = _mm_nn(dpba, plus(w["w_in_tail_t"], None if zero is None else zero.astype(MXU_DTYPE)), name="mix_dhn_tail")
    dhn0 = _mm_nn(dpm, w["w_in_t"], res=dhn0, a_seg=True, name="mix_dhn")
    dh0, danw0 = _rmsnorm_bwd(dhn0, h0, w["anw"][0], dh1, name="mix_norm_bwd")

    g.update(
        x=dh0[PAD_ROWS + N_META:], meta=dh0[PAD_ROWS:PAD_ROWS + N_META], anw=[danw0, danw1], fnw=[dfnw0, dfnw1],
        caw=dcaw, dcw=ddcw, hp=dhp, dnw=ddnw, qnw=jnp.sum(dqw, 0), knw=jnp.sum(dkw, 0), sink=dsink,
        w_up_t=[dwup0, dwup1], fcw=[dfcw0, dfcw1], w_down=[dwdown0, dwdown1])
    return loss, g


N_TAIL = 2 * DN_HEADS


def _prepare_early(p):
    n_main = N_SEG * SEG
    w_in_t = p["mix_w_in_t"]
    tail_t = jnp.pad(w_in_t[n_main:], ((0, LANES - N_TAIL), (0, 0)))
    hp = jnp.zeros((DN_HEADS, SUBLANES, LANES), F32)
    hp = hp.at[:, 0, :].set(p["dn_a_log"][0][:, None]).at[:, 1, :].set(p["dn_dt_bias"][0][:, None])
    depth = p["ffn_conv_w"].shape[0]
    return dict(
        meta=p["meta_tokens"], anw=[p["attn_norm_w"][i:i + 1] for i in range(depth)],
        fnw=[p["ffn_norm_w"][i:i + 1] for i in range(depth)],
        w_in_t=w_in_t, w_in_tail_t=tail_t,
        caw=_pad_w(p["conv_a_w"][0]), dcw=_pad_w(p["dn_conv_w"][0]), hp=hp, dnw=p["dn_norm_w"],
        qnw=p["swa_q_norm_w"], knw=p["swa_k_norm_w"], sink=p["swa_sinks"].reshape(SWA_KV_HEADS, SWA_GROUP, 1, 1),
        fcw=[_pad_w(p["ffn_conv_w"][i]) for i in range(depth)])


def _prepare_weights(p):
    return dict(
        _prepare_early(dict(p, mix_w_in_t=p["mix_w_in"][0].T)), w_out=p["mix_w_out"][0], wo=p["swa_wo"][0],
        wqkv=jnp.concatenate([p["swa_wq"][0], p["swa_wk"][0], p["swa_wv"][0]], 1),
        w_up0_t=p["ffn_w_up"][0].T, w_up1_t=p["ffn_w_up"][1].T, w_down0=p["ffn_w_down"][0], w_down1=p["ffn_w_down"][1])


def _small_named(g):
    return dict(
        meta_tokens=g["meta"], attn_norm_w=jnp.concatenate(g["anw"], 0), ffn_norm_w=jnp.concatenate(g["fnw"], 0),
        conv_a_w=g["caw"][None, :3], dn_conv_w=g["dcw"][None, :4],
        dn_a_log=g["hp"][None, :, 0, 0], dn_dt_bias=g["hp"][None, :, 1, 0], dn_norm_w=g["dnw"],
        swa_q_norm_w=g["qnw"], swa_k_norm_w=g["knw"], swa_sinks=g["sink"].reshape(1, SWA_HEADS),
        ffn_conv_w=jnp.stack([c[:3] for c in g["fcw"]]))


def _reference_named(g):
    nq, nkv = SWA_HEADS * SWA_HEAD_DIM, SWA_KV_HEADS * SWA_HEAD_DIM
    return dict(
        _small_named(g), mix_w_in=g["w_in_t"].T[None],
        mix_w_out=g["w_out"][None], swa_wq=g["wqkv"][None, :, :nq], swa_wk=g["wqkv"][None, :, nq:nq + nkv],
        swa_wv=g["wqkv"][None, :, nq + nkv:], swa_wo=g["wo"][None],
        ffn_w_up=jnp.stack([t.T for t in g["w_up_t"]]), ffn_w_down=jnp.stack(g["w_down"]))


def _my_index():
    return 4 * lax.axis_index("x") + 2 * lax.axis_index("y") + lax.axis_index("c")


def _all_gather(arrays, *, name):
    n = len(arrays)

    def body(*refs):
        ins, outs = refs[:n], refs[n:2 * n]
        send_sems, recv_sems, local_sems = refs[2 * n:]
        x, y, c = lax.axis_index("x"), lax.axis_index("y"), lax.axis_index("c")
        me, sibling = (x, y, c), (x, y, 1 - c)
        chips = [(1 - x, y), (x, 1 - y), (1 - x, 1 - y)]

        def copy(i, k, block, to, src=None):
            rows = outs[i].at[4 * block[0] + 2 * block[1] + block[2]]
            return pltpu.make_async_remote_copy(
                src_ref=rows if src is None else src, dst_ref=rows, send_sem=send_sems.at[i, k], recv_sem=recv_sems.at[i, k],
                device_id=to, device_id_type=pl.DeviceIdType.MESH)

        mine = [pltpu.make_async_copy(ins[i], outs[i].at[4 * x + 2 * y + c], local_sems.at[i]) for i in range(n)]
        first = []
        for j, chip in enumerate(chips):
            first += [copy(i, 1 + j, me, (*chip, c), src=ins[i]) for i in range(n)]
        first += [copy(i, 0, me, sibling, src=ins[i]) for i in range(n)]
        for cp in first + mine:
            cp.start()
        passed = []
        for j, chip in enumerate(chips):
            for i in range(n):
                copy(i, 1 + j, (*chip, c), me).wait_recv()
                fwd = copy(i, 4 + j, (*chip, c), sibling)
                fwd.start()
                passed.append(fwd)
        for i in range(n):
            copy(i, 0, sibling, me).wait_recv()
            for j, chip in enumerate(chips):
                copy(i, 4 + j, (*chip, 1 - c), me).wait_recv()
        for cp in first + passed:
            cp.wait_send()
        for cp in mine:
            cp.wait()

    hbm = pl.BlockSpec(memory_space=pl.ANY)
    return pl.pallas_call(
        body, name=name, in_specs=[hbm] * n, out_specs=[hbm] * n,
        out_shape=[jax.ShapeDtypeStruct((N_DEV,) + tuple(a.shape), a.dtype) for a in arrays],
        scratch_shapes=[pltpu.SemaphoreType.DMA((n, 7)), pltpu.SemaphoreType.DMA((n, 7)), pltpu.SemaphoreType.DMA((n,))],
    )(*arrays)


def _peer(d):
    px, py, pc = lax.axis_index("x") ^ (d >> 2), lax.axis_index("y") ^ ((d >> 1) & 1), lax.axis_index("c") ^ (d & 1)
    return (px, py, pc), 4 * px + 2 * py + pc


def _push_copies(mode, srcs, lands, send_sems, recv_sems):
    me = _my_index()
    out = []
    for d in range(1, N_DEV):
        pos, idx = _peer(d)
        for i in range(len(srcs)):
            out.append(pltpu.make_async_remote_copy(
                src_ref=srcs[i] if mode == "gather" else srcs[i].at[idx], dst_ref=lands[i].at[me],
                send_sem=send_sems.at[i * N_DEV + d], recv_sem=recv_sems.at[i * N_DEV + d], device_id=pos,
                device_id_type=pl.DeviceIdType.MESH))
    return out


_HBM = pl.BlockSpec(memory_space=pltpu.HBM)
_SEM = pl.BlockSpec(memory_space=pltpu.SEMAPHORE)


def _push_start(mode, arrays, follows, *, name):
    n = len(arrays)
    blocks = [a.shape if mode == "gather" else a.shape[1:] for a in arrays]
    lands = [lax.empty((N_DEV,) + tuple(b), a.dtype) for a, b in zip(arrays, blocks)]

    def body(*refs):
        srcs, land_refs = refs[:n], refs[n:2 * n]
        send_sems, recv_sems = refs[2 * n + 1], refs[2 * n + 2]
        zero = refs[-1]
        for cp in _push_copies(mode, srcs, land_refs, send_sems, recv_sems):
            cp.start()
        zero[...] = jnp.zeros_like(zero)

    hbm_in = [pltpu.with_memory_space_constraint(a, pltpu.HBM) for a in list(arrays) + lands]
    outs = pl.pallas_call(
        body, name=name,
        out_shape=[pltpu.SemaphoreType.DMA((n * N_DEV,)), pltpu.SemaphoreType.DMA((n * N_DEV,))]
        + [pltpu.HBM(a.shape, a.dtype) for a in hbm_in] + [jax.ShapeDtypeStruct((SUBLANES, LANES), F32)],
        in_specs=[_HBM] * (2 * n) + [pl.BlockSpec(memory_space=pl.ANY)],
        out_specs=[_SEM, _SEM] + [_HBM] * (2 * n) + [pl.BlockSpec(memory_space=pltpu.VMEM)],
        input_output_aliases={i: 2 + i for i in range(2 * n)},
        compiler_params=pltpu.CompilerParams(has_side_effects=pltpu.SideEffectType.DATAFLOW_SIDE_EFFECTING),
    )(*hbm_in, follows)
    return dict(mode=mode, sems=outs[:2], srcs=outs[2:2 + n], lands=outs[2 + n:2 + 2 * n], zero=outs[-1])


def _push_wait(push, follows, *, name):
    n = len(push["srcs"])
    mode = push["mode"]

    def body(*refs):
        srcs, land_refs = refs[:n], refs[n:2 * n]
        send_sems, recv_sems = refs[2 * n], refs[2 * n + 1]
        for cp in _push_copies(mode, srcs, land_refs, send_sems, recv_sems):
            cp.wait_send()
            cp.wait_recv()

    args = list(push["srcs"]) + list(push["lands"])
    outs = pl.pallas_call(
        body, name=name, out_shape=[pltpu.HBM(a.shape, a.dtype) for a in args],
        in_specs=[_HBM] * (2 * n) + [_SEM, _SEM, pl.BlockSpec(memory_space=pl.ANY)], out_specs=[_HBM] * (2 * n),
        input_output_aliases={i: i for i in range(2 * n)},
        compiler_params=pltpu.CompilerParams(has_side_effects=pltpu.SideEffectType.DATAFLOW_SIDE_EFFECTING),
    )(*args, *push["sems"], follows)
    me = _my_index()
    got = []
    for src, land in zip(outs[:n], outs[n:]):
        own = src if mode == "gather" else lax.dynamic_index_in_dim(src, me, 0, keepdims=False)
        got.append(lax.dynamic_update_index_in_dim(land, own, me, 0))
    return got


ADAMW_BLOCK_BYTES = 6 * 1024 * 1024


def _adamw_tile(r, c):
    fits = lambda tr, tc: N_DEV * tr * tc * 4 <= ADAMW_BLOCK_BYTES
    rows = [t for t in range(2 * SUBLANES, r + 1, 2 * SUBLANES) if r % t == 0 and fits(t, c)]
    if rows or fits(r, c):
        return (max(rows) if rows else r), c
    cols = [t for t in range(LANES, c + 1, LANES) if c % t == 0 and fits(r, t)]
    return r, max(cols)


def _adamw(parts, w, m, v, layer, outs=None, *, name):
    nl, r, c = w.shape
    tr, tc = _adamw_tile(r, c)

    def body(p_ref, w_ref, m_ref, v_ref, *rest):
        g_ref, d_ref, nm_ref, nv_ref = rest[-4:]
        g = p_ref[0].astype(F32)
        for j in range(1, N_DEV):
            g = g + p_ref[j].astype(F32)
        m2 = ADAM_B1 * m_ref[...] + (1.0 - ADAM_B1) * g
        v2 = ADAM_B2 * v_ref[...] + (1.0 - ADAM_B2) * jnp.square(g)
        m_hat = m2 / (1.0 - ADAM_B1 ** ADAM_STEP)
        v_hat = v2 / (1.0 - ADAM_B2 ** ADAM_STEP)
        g_ref[...] = g
        d_ref[...] = -ADAM_LR * (m_hat / (jnp.sqrt(v_hat) + ADAM_EPS) + ADAM_WD * w_ref[...])
        nm_ref[...] = m2
        nv_ref[...] = v2

    blk = pl.BlockSpec((None, tr, tc), lambda i, j: (layer, i, j))
    out = jax.ShapeDtypeStruct((nl, r, c), F32)
    given = list(outs) if outs is not None else []
    return pl.pallas_call(
        body, name=name, grid=(r // tr, c // tc),
        in_specs=[pl.BlockSpec((N_DEV, tr, tc), lambda i, j: (0, i, j)), blk, blk, blk] + [pl.BlockSpec(memory_space=pl.ANY)] * len(given),
        out_specs=[blk, blk, blk, blk], out_shape=[out, out, out, out],
        input_output_aliases={4 + t: t for t in range(len(given))},
        compiler_params=_params(("parallel", "parallel")),
    )(parts, w, m, v, *given)


SHARD_AXIS = dict(
    meta_tokens=1, attn_norm_w=None, ffn_norm_w=None, mix_w_in=2, conv_a_w=2, dn_conv_w=2, dn_a_log=None, dn_dt_bias=None,
    dn_norm_w=None, mix_w_out=1, swa_wq=1, swa_wk=1, swa_wv=1, swa_q_norm_w=None, swa_k_norm_w=None, swa_sinks=None,
    swa_wo=1, ffn_w_up=2, ffn_conv_w=2, ffn_w_down=1)
WEIGHTS = list(SHARD_AXIS)
BIG = ["mix_w_in", "mix_w_out", "swa_wq", "swa_wk", "swa_wv", "swa_wo", "ffn_w_up", "ffn_w_down"]
SMALL = [k for k in WEIGHTS if k not in BIG]
SMALL_SHARDED = [k for k in SMALL if SHARD_AXIS[k] is not None]


def _whole(g8, axis):
    t = jnp.moveaxis(g8, 0, axis)
    return t.reshape(t.shape[:axis] + (t.shape[axis] * t.shape[axis + 1],) + t.shape[axis + 2:])


def _by_owner(a, axis):
    s = a.shape[axis] // N_DEV
    return jnp.moveaxis(a.reshape(a.shape[:axis] + (N_DEV, s) + a.shape[axis + 1:]), axis, 0)


def _pack(arrays, lead=0):
    flat = jnp.concatenate([a.reshape(a.shape[:lead] + (-1,)) for a in arrays], -1)
    n = flat.shape[-1]
    rows = -(-n // (SUBLANES * LANES)) * SUBLANES
    flat = jnp.pad(flat, [(0, 0)] * lead + [(0, rows * LANES - n)])
    return flat.reshape(flat.shape[:lead] + (rows, LANES))


def _unpack(buf, shapes, lead=0):
    flat = buf.reshape(buf.shape[:lead] + (-1,))
    out, o = [], 0
    for s in shapes:
        n = 1
        for e in s:
            n *= e
        out.append(flat[..., o:o + n].reshape(buf.shape[:lead] + tuple(s)))
        o += n
    return out


def kernel(x, meta_tokens, attn_norm_w, ffn_norm_w, mix_w_in, conv_a_w, dn_conv_w, dn_a_log, dn_dt_bias, dn_norm_w, mix_w_out, swa_wq, swa_wk, swa_wv, swa_q_norm_w, swa_k_norm_w, swa_sinks, swa_wo, ffn_w_up, ffn_conv_w, ffn_w_down, loss_target, m_meta_tokens, m_attn_norm_w, m_ffn_norm_w, m_mix_w_in, m_conv_a_w, m_dn_conv_w, m_dn_a_log, m_dn_dt_bias, m_dn_norm_w, m_mix_w_out, m_swa_wq, m_swa_wk, m_swa_wv, m_swa_q_norm_w, m_swa_k_norm_w, m_swa_sinks, m_swa_wo, m_ffn_w_up, m_ffn_conv_w, m_ffn_w_down, v_meta_tokens, v_attn_norm_w, v_ffn_norm_w, v_mix_w_in, v_conv_a_w, v_dn_conv_w, v_dn_a_log, v_dn_dt_bias, v_dn_norm_w, v_mix_w_out, v_swa_wq, v_swa_wk, v_swa_wv, v_swa_q_norm_w, v_swa_k_norm_w, v_swa_sinks, v_swa_wo, v_ffn_w_up, v_ffn_conv_w, v_ffn_w_down):
    w = dict(meta_tokens=meta_tokens, attn_norm_w=attn_norm_w, ffn_norm_w=ffn_norm_w, mix_w_in=mix_w_in, conv_a_w=conv_a_w, dn_conv_w=dn_conv_w, dn_a_log=dn_a_log, dn_dt_bias=dn_dt_bias, dn_norm_w=dn_norm_w, mix_w_out=mix_w_out, swa_wq=swa_wq, swa_wk=swa_wk, swa_wv=swa_wv, swa_q_norm_w=swa_q_norm_w, swa_k_norm_w=swa_k_norm_w, swa_sinks=swa_sinks, swa_wo=swa_wo, ffn_w_up=ffn_w_up, ffn_conv_w=ffn_conv_w, ffn_w_down=ffn_w_down)
    mom = dict(meta_tokens=m_meta_tokens, attn_norm_w=m_attn_norm_w, ffn_norm_w=m_ffn_norm_w, mix_w_in=m_mix_w_in, conv_a_w=m_conv_a_w, dn_conv_w=m_dn_conv_w, dn_a_log=m_dn_a_log, dn_dt_bias=m_dn_dt_bias, dn_norm_w=m_dn_norm_w, mix_w_out=m_mix_w_out, swa_wq=m_swa_wq, swa_wk=m_swa_wk, swa_wv=m_swa_wv, swa_q_norm_w=m_swa_q_norm_w, swa_k_norm_w=m_swa_k_norm_w, swa_sinks=m_swa_sinks, swa_wo=m_swa_wo, ffn_w_up=m_ffn_w_up, ffn_conv_w=m_ffn_conv_w, ffn_w_down=m_ffn_w_down)
    var = dict(meta_tokens=v_meta_tokens, attn_norm_w=v_attn_norm_w, ffn_norm_w=v_ffn_norm_w, mix_w_in=v_mix_w_in, conv_a_w=v_conv_a_w, dn_conv_w=v_dn_conv_w, dn_a_log=v_dn_a_log, dn_dt_bias=v_dn_dt_bias, dn_norm_w=v_dn_norm_w, mix_w_out=v_mix_w_out, swa_wq=v_swa_wq, swa_wk=v_swa_wk, swa_wv=v_swa_wv, swa_q_norm_w=v_swa_q_norm_w, swa_k_norm_w=v_swa_k_norm_w, swa_sinks=v_swa_sinks, swa_wo=v_swa_wo, ffn_w_up=v_ffn_w_up, ffn_conv_w=v_ffn_conv_w, ffn_w_down=v_ffn_w_down)
    me = _my_index()

    transposed = ("mix_w_in", "ffn_w_up")
    view = lambda k, a: jnp.swapaxes(a, 1, 2) if k in transposed else a
    axis2d = {k: 0 if k in transposed else SHARD_AXIS[k] - 1 for k in BIG}
    shard16 = {k: view(k, w[k]).astype(MXU_DTYPE) for k in BIG}
    small_shard_shapes = [w[k].shape for k in SMALL_SHARDED]
    rows_in = shard16["mix_w_in"].shape[1]
    sent_in = jnp.pad(shard16["mix_w_in"][0], ((0, -rows_in % (2 * SUBLANES)), (0, 0)))
    got = _all_gather([sent_in, _pack([w[k] for k in SMALL_SHARDED])], name="gather_weights")
    whole = {"mix_w_in_t": _whole(got[0][:, :rows_in], 0)}
    for k, a in zip(SMALL_SHARDED, _unpack(got[1], small_shard_shapes, lead=1)):
        whole[k] = _whole(a, SHARD_AXIS[k])
    for k in SMALL:
        whole.setdefault(k, w[k])
    stages = {"in": [("mix_w_in", 0)], "l0": [("mix_w_out", 0), ("ffn_w_up", 0), ("ffn_w_down", 0)],
              "l1": [("swa_wq", 0), ("swa_wk", 0), ("swa_wv", 0), ("swa_wo", 0), ("ffn_w_up", 1), ("ffn_w_down", 1)]}
    pushed = {}
    follows = got[1]
    for stage in ("l0", "l1"):
        pushed[stage] = _push_start("gather", [shard16[k][l] for k, l in stages[stage]], follows, name=f"push_weights_{stage}")
        follows = pushed[stage]["zero"]
    early = _prepare_early(whole)
    early["anw"][0] = early["anw"][0] + follows[0, 0]

    def fetch(stage, after):
        got = _push_wait(pushed[stage], after, name=f"wait_weights_{stage}")
        full = {kl: _whole(a, axis2d[kl[0]]) for kl, a in zip(stages[stage], got)}
        if stage == "l0":
            return dict(w_out=full["mix_w_out", 0], w_up0_t=full["ffn_w_up", 0], w_down0=full["ffn_w_down", 0])
        wqkv = jnp.concatenate([full["swa_wq", 0], full["swa_wk", 0], full["swa_wv", 0]], 1)
        return dict(wqkv=wqkv, wo=full["swa_wo", 0], w_up1_t=full["ffn_w_up", 1], w_down1=full["ffn_w_down", 1])

    nq, nkv = SWA_HEADS * SWA_HEAD_DIM, SWA_KV_HEADS * SWA_HEAD_DIM
    grad_pushes = {}

    def push(stage, gd):
        if stage == "in":
            named = {("mix_w_in", 0): gd["w_in_t"]}
        elif stage == "l1":
            named = {("swa_wq", 0): gd["wqkv"][:, :nq], ("swa_wk", 0): gd["wqkv"][:, nq:nq + nkv],
                     ("swa_wv", 0): gd["wqkv"][:, nq + nkv:], ("swa_wo", 0): gd["wo"],
                     ("ffn_w_up", 1): gd["w_up_t"], ("ffn_w_down", 1): gd["w_down"]}
        else:
            named = {("mix_w_out", 0): gd["w_out"], ("ffn_w_up", 0): gd["w_up_t"], ("ffn_w_down", 0): gd["w_down"]}
        sent = [_by_owner(named[kl], axis2d[kl[0]]) for kl in stages[stage]]
        grad_pushes[stage] = _push_start("scatter", sent, jnp.zeros((SUBLANES, LANES), F32), name=f"push_grads_{stage}")
        return grad_pushes[stage]["zero"][0, 0]

    loss, g = _local_step(x[0], loss_target[0], early, fetch, push)
    grads = _small_named(g)

    results = {}

    def update(stage, follows):
        got = _push_wait(grad_pushes[stage], follows, name=f"wait_grads_{stage}")
        for (k, l), parts in zip(stages[stage], got):
            w3, m3, v3 = view(k, w[k]), view(k, mom[k]), view(k, var[k])
            results[k] = _adamw(parts.reshape((N_DEV,) + w3.shape[1:]), w3, m3, v3, l, results.get(k), name=f"adamw_{k}_{l}")
        return results[stages[stage][0][0]][0]

    follows = update("l0", update("l1", g["meta"]))

    small_shapes = [grads[k].shape for k in SMALL]
    (all_small,) = _all_gather([_pack([loss] + [grads[k].astype(F32) for k in SMALL])], name="gather_small_grads")
    loss_parts, *small_parts = _unpack(all_small, [loss.shape] + small_shapes, lead=1)
    mine = []
    for k, p in zip(SMALL, small_parts):
        ax = SHARD_AXIS[k]
        mine.append(p if ax is None else lax.dynamic_slice_in_dim(p, me * w[k].shape[ax], w[k].shape[ax], 1 + ax))
    zero = jnp.zeros(loss.shape, F32)
    packed = [_pack([z] + [d[k] for k in SMALL]) for z, d in ((zero, w), (zero, mom), (zero, var))]
    res = _adamw(_pack([loss_parts] + mine, lead=1), *[t[None] for t in packed], 0, name="adamw_small")
    shapes = [loss.shape] + [w[k].shape for k in SMALL]
    for t, which in zip(res, range(4)):
        for k, a in zip(["loss"] + SMALL, _unpack(t[0], shapes)):
            results.setdefault(k, [None] * 4)[which] = a
    update("in", jnp.maximum(follows[0, :1, :1], res[0][0, :1, :1]))

    outs = [[view(k, results[k][which]) for k in WEIGHTS] for which in range(4)]
    return (results["loss"][0][0, 0], g["x"][None], *outs[0], *outs[1], *outs[2], *outs[3])
```

```python
import functools

import jax
import jax.numpy as jnp
from jax import lax
from jax.experimental import pallas as pl
from jax.experimental.pallas import tpu as pltpu

F32 = jnp.float32
BF16 = jnp.bfloat16
MXU_DTYPE = BF16
GRAD_WIRE_DTYPE = BF16

D_MODEL = 1024
N_META = 16
PAD_ROWS = 112
D_CONV = 512
DN_HEADS = 4
DN_HEAD_DIM = 128
DN_DIM = DN_HEADS * DN_HEAD_DIM
DN_CHUNK = 64
SEG = 512
N_SEG = 7
SWA_HEADS = 16
SWA_KV_HEADS = 4
SWA_GROUP = SWA_HEADS // SWA_KV_HEADS
SWA_HEAD_DIM = 64
SWA_BLOCK = 128
D_FF = 2816
EPS = 1e-6
NEG = -1e30
N_DEV = 8

ADAM_LR = 0.001
ADAM_B1 = 0.9
ADAM_B2 = 0.999
ADAM_EPS = 1e-08
ADAM_WD = 0.01
ADAM_STEP = 10

VMEM_LIMIT_BYTES = 52 * 1024 * 1024
SUBLANES = 8
LANES = 128


def _pick(n, prefs):
    for p in prefs:
        if n % p == 0:
            return p
    return n


def _params(sem, vmem=VMEM_LIMIT_BYTES):
    return pltpu.CompilerParams(dimension_semantics=sem, vmem_limit_bytes=vmem)


def _rms(x, w):
    return x * lax.rsqrt(jnp.mean(x * x, -1, keepdims=True) + EPS) * w


def _norm_matmul(h, nw, w, *, o_seg=None, trans_w=False, n=None, name):
    m, k = h.shape
    n = n or (w.shape[0] if trans_w else w.shape[1])
    tm = _pick(m, (1408, 384, 128))
    tn = _pick(o_seg or n, (1408, 1024, 512, 256, 128))
    dims = _DOT_DIMS["nt" if trans_w else "nn"]

    def body(h_ref, nw_ref, w_ref, o_ref, hn_ref, hn_s):
        @pl.when(pl.program_id(1) == 0)
        def _():
            hn = _rms(h_ref[...], nw_ref[...]).astype(MXU_DTYPE)
            hn_s[...] = hn
            hn_ref[...] = hn

        o_ref[...] = lax.dot_general(hn_s[...], w_ref[...], dims, preferred_element_type=F32)

    if o_seg:
        per = o_seg // tn
        o_shape = jax.ShapeDtypeStruct((n // o_seg, m, o_seg), F32)
        o_spec = pl.BlockSpec((None, tm, tn), lambda i, j: (j // per, i, j % per))
    else:
        o_shape = jax.ShapeDtypeStruct((m, n), F32)
        o_spec = pl.BlockSpec((tm, tn), lambda i, j: (i, j))
    return pl.pallas_call(
        body, name=name, grid=(m // tm, n // tn),
        in_specs=[pl.BlockSpec((tm, k), lambda i, j: (i, 0)), pl.BlockSpec((1, k), lambda i, j: (0, 0)),
                  pl.BlockSpec((tn, k), lambda i, j: (j, 0)) if trans_w else pl.BlockSpec((k, tn), lambda i, j: (0, j))],
        out_specs=[o_spec, pl.BlockSpec((tm, k), lambda i, j: (i, 0))],
        out_shape=[o_shape, jax.ShapeDtypeStruct((m, k), MXU_DTYPE)],
        scratch_shapes=[pltpu.VMEM((tm, k), MXU_DTYPE)],
        compiler_params=_params(("parallel", "arbitrary")),
    )(h, nw, w)


TILE_BUDGET_BYTES = 38 * 1024 * 1024
TILE_SIZES = (4224, 2816, 1792, 1536, 1408, 1024, 512, 256, 128)


def _divisor_tiles(n):
    return [t for t in TILE_SIZES if n % t == 0] or [n]


def _mm_nn(a, w, *, res=None, a_seg=False, trans_w=False, w_seg=False, out_dtype=F32, name):
    if a_seg:
        s, m, seg = a.shape
    else:
        m, seg = a.shape
        s = 1
    k = s * seg
    n = w.shape[0] * w.shape[2] if w_seg else (w.shape[0] if trans_w else w.shape[1])
    n_seg = w.shape[2] if w_seg else n
    tm = _pick(m, (1408, 1024, 512, 384, 256, 128))
    ab = a.dtype.itemsize
    k_steps = [(sb, seg) for sb in range(s, 0, -1) if s % sb == 0] if a_seg else [(1, t) for t in _divisor_tiles(seg)]
    best = None
    for tn in _divisor_tiles(n_seg):
        for sb, tk1 in k_steps:
            tk = sb * tk1
            nk = k // tk
            need = 2 * tm * tk * ab + 2 * tk * tn * 2 + 2 * tm * tn * 4 + (tm * tn * 4 if nk > 1 else 0) + (2 * tm * tn * 4 if res is not None else 0)
            if need <= TILE_BUDGET_BYTES and (best is None or tk * tn > best[0]):
                best = (tk * tn, tn, sb, tk1)
    _, tn, sb, tk1 = best
    tk = sb * tk1
    nk = k // tk
    w_dims = _DOT_DIMS["nt" if trans_w else "nn"]

    def body(*refs):
        a_ref, w_ref = refs[:2]
        r_ref = refs[2] if res is not None else None
        o_ref = refs[3 if res is not None else 2]

        def partial_product():
            if not a_seg:
                return lax.dot_general(a_ref[...].astype(MXU_DTYPE), w_ref[...], w_dims, preferred_element_type=F32)
            out = None
            for t in range(sb):
                wt = w_ref[:, t * seg:(t + 1) * seg] if trans_w else w_ref[t * seg:(t + 1) * seg, :]
                d = lax.dot_general(a_ref[t].astype(MXU_DTYPE), wt, w_dims, preferred_element_type=F32)
                out = d if out is None else out + d
            return out

        if nk == 1:
            o_ref[...] = (partial_product() if res is None else partial_product() + r_ref[...]).astype(o_ref.dtype)
            return
        acc = refs[-1]
        kk = pl.program_id(2)

        @pl.when(kk == 0)
        def _():
            acc[...] = jnp.zeros_like(acc)

        acc[...] += partial_product()

        @pl.when(kk == nk - 1)
        def _():
            o_ref[...] = (acc[...] if res is None else acc[...] + r_ref[...]).astype(o_ref.dtype)

    a_spec = pl.BlockSpec((sb, tm, seg), lambda i, j, kk: (kk, i, 0)) if a_seg else pl.BlockSpec((tm, tk), lambda i, j, kk: (i, kk))
    if w_seg:
        per = n_seg // tn
        w_spec = pl.BlockSpec((None, tk, tn), lambda i, j, kk: (j // per, kk, j % per))
    elif trans_w:
        w_spec = pl.BlockSpec((tn, tk), lambda i, j, kk: (j, kk))
    else:
        w_spec = pl.BlockSpec((tk, tn), lambda i, j, kk: (kk, j))
    in_specs = [a_spec, w_spec]
    args = [a, w]
    if res is not None:
        in_specs.append(pl.BlockSpec((tm, tn), lambda i, j, kk: (i, j)))
        args.append(res)
    return pl.pallas_call(
        body, name=name, grid=(m // tm, n // tn, nk), in_specs=in_specs,
        out_specs=pl.BlockSpec((tm, tn), lambda i, j, kk: (i, j)),
        out_shape=jax.ShapeDtypeStruct((m, n), out_dtype),
        scratch_shapes=[pltpu.VMEM((tm, tn), F32)] if nk > 1 else [],
        compiler_params=_params(("parallel", "parallel", "arbitrary")),
    )(*args)


def _mm_tn(a, b, *, a_seg=False, b_seg=False, out_dtype=None, name):
    out_dtype = out_dtype or GRAD_WIRE_DTYPE
    if a_seg:
        sa, m, a_unit = a.shape
        ka = sa * a_unit
    else:
        m, ka = a.shape
        a_unit = ka
    if b_seg:
        s, _, seg = b.shape
        n = s * seg
    else:
        n = b.shape[1]
        seg = n
    tmc = _pick(m, (1408, 384, 128))
    best = None
    for tka in _divisor_tiles(a_unit):
        for tn in _divisor_tiles(seg):
            need = 2 * tmc * tka * a.dtype.itemsize + 2 * tmc * tn * b.dtype.itemsize + tka * tn * 4 + 2 * tka * tn * 4
            if need <= TILE_BUDGET_BYTES and (best is None or (tka * tn, tn) > best[:2]):
                best = (tka * tn, tn, tka)
    _, tn, tka = best
    nm = m // tmc

    def body(a_ref, b_ref, o_ref, acc):
        mm = pl.program_id(2)

        @pl.when(mm == 0)
        def _():
            acc[...] = jnp.zeros_like(acc)

        acc[...] += lax.dot_general(a_ref[...].astype(MXU_DTYPE), b_ref[...].astype(MXU_DTYPE),
                                    (((0,), (0,)), ((), ())), preferred_element_type=F32)

        @pl.when(mm == nm - 1)
        def _():
            o_ref[...] = acc[...].astype(o_ref.dtype)

    if b_seg:
        per = seg // tn
        b_spec = pl.BlockSpec((None, tmc, tn), lambda i, j, mm: (j // per, mm, j % per))
    else:
        b_spec = pl.BlockSpec((tmc, tn), lambda i, j, mm: (mm, j))
    if a_seg:
        a_per = a_unit // tka
        a_spec = pl.BlockSpec((None, tmc, tka), lambda i, j, mm: (i // a_per, mm, i % a_per))
    else:
        a_spec = pl.BlockSpec((tmc, tka), lambda i, j, mm: (mm, i))
    return pl.pallas_call(
        body, name=name, grid=(ka // tka, n // tn, nm),
        in_specs=[a_spec, b_spec],
        out_specs=pl.BlockSpec((tka, tn), lambda i, j, mm: (i, j)),
        out_shape=jax.ShapeDtypeStruct((ka, n), out_dtype),
        scratch_shapes=[pltpu.VMEM((tka, tn), F32)],
        compiler_params=_params(("parallel", "parallel", "arbitrary")),
    )(a, b)


def _rmsnorm_bwd(dhn, h, nw, dres, *, name):
    m, d = h.shape
    tm = _pick(m, (384, 128))

    def body(dhn_ref, h_ref, nw_ref, dres_ref, dh_ref, dnw_ref):
        i = pl.program_id(0)
        x = h_ref[...]
        r = lax.rsqrt(jnp.mean(x * x, -1, keepdims=True) + EPS)
        xh = x * r
        dy = dhn_ref[...]
        dxh = dy * nw_ref[...]
        dx = r * (dxh - xh * jnp.mean(dxh * xh, -1, keepdims=True))
        row = i * tm + lax.broadcasted_iota(jnp.int32, (tm, 1), 0)
        dh_ref[...] = jnp.where(row >= PAD_ROWS, dres_ref[...] + dx, 0.0)

        @pl.when(i == 0)
        def _():
            dnw_ref[...] = jnp.zeros_like(dnw_ref)

        dnw_ref[...] += jnp.sum(dy * xh, 0, keepdims=True)

    return pl.pallas_call(
        body, name=name, grid=(m // tm,),
        in_specs=[pl.BlockSpec((tm, d), lambda i: (i, 0)), pl.BlockSpec((tm, d), lambda i: (i, 0)),
                  pl.BlockSpec((1, d), lambda i: (0, 0)), pl.BlockSpec((tm, d), lambda i: (i, 0))],
        out_specs=[pl.BlockSpec((tm, d), lambda i: (i, 0)), pl.BlockSpec((1, d), lambda i: (0, 0))],
        out_shape=[jax.ShapeDtypeStruct((m, d), F32), jax.ShapeDtypeStruct((1, d), F32)],
        compiler_params=_params(("arbitrary",)),
    )(dhn, h, nw, dres)


ROW_CHUNK = 248


def _row_chunks(m):
    out, s = [], SUBLANES
    while s < m:
        n = min(ROW_CHUNK, m - s)
        out.append((s, n))
        s += n
    return out


def _conv_at(load, w, width, s, n):
    acc = w[width - 1:width, :] * load(s, n)
    for j in range(width - 1):
        acc = acc + w[j:j + 1, :] * load(s - (width - 1 - j), n)
    return acc


def _conv_t_at(load, w, width, s, n):
    acc = w[width - 1:width, :] * load(s, n)
    for j in range(width - 1):
        acc = acc + w[j:j + 1, :] * load(s + (width - 1 - j), n)
    return acc


def _dconv_w(load_x, d, width, s, n):
    rows = [jnp.sum(d * load_x(s - (width - 1 - j), n), 0, keepdims=True) for j in range(width)]
    rows.append(jnp.zeros((SUBLANES - width, d.shape[1]), F32))
    return jnp.concatenate(rows, 0)


def _pad_w(w):
    return jnp.concatenate([w, jnp.zeros((SUBLANES - w.shape[0], w.shape[1]), w.dtype)], 0)


def _sigmoid(x):
    return 0.5 * jnp.tanh(0.5 * x) + 0.5


def _ffn_act_fwd(u, cw, *, name):
    _, m, f = u.shape
    cb = _pick(f, (256, 128))
    chunks = _row_chunks(m)

    def body(g_ref, v_ref, w_ref, o_ref):
        w = w_ref[...]
        o_ref[pl.ds(0, SUBLANES), :] = jnp.zeros((SUBLANES, cb), o_ref.dtype)
        for s, n in chunks:
            c = _conv_at(lambda a, b: g_ref[pl.ds(a, b), :], w, 3, s, n)
            o_ref[pl.ds(s, n), :] = (c * _sigmoid(c) * v_ref[pl.ds(s, n), :]).astype(o_ref.dtype)

    return pl.pallas_call(
        body, name=name, grid=(f // cb,),
        in_specs=[pl.BlockSpec((None, m, cb), lambda j: (0, 0, j)), pl.BlockSpec((None, m, cb), lambda j: (1, 0, j)),
                  pl.BlockSpec((SUBLANES, cb), lambda j: (0, j))],
        out_specs=pl.BlockSpec((m, cb), lambda j: (0, j)),
        out_shape=jax.ShapeDtypeStruct((m, f), MXU_DTYPE),
        compiler_params=_params(("parallel",)),
    )(u, u, cw)


def _ffn_act_bwd(da, u, cw, *, name):
    _, m, f = u.shape
    cb = LANES
    chunks = _row_chunks(m)

    def body(da_ref, g_ref, v_ref, w_ref, du_ref, dw_ref, dg_s):
        w = w_ref[...]
        zeros8 = jnp.zeros((SUBLANES, cb), F32)
        dg_s[pl.ds(0, SUBLANES), :] = zeros8
        dg_s[pl.ds(m, SUBLANES), :] = zeros8
        du_ref[0, pl.ds(0, SUBLANES), :] = zeros8.astype(du_ref.dtype)
        du_ref[1, pl.ds(0, SUBLANES), :] = zeros8.astype(du_ref.dtype)
        load_g = lambda a, b: g_ref[pl.ds(a, b), :]
        dw = jnp.zeros((SUBLANES, cb), F32)
        for s, n in chunks:
            c = _conv_at(load_g, w, 3, s, n)
            sg = _sigmoid(c)
            d = da_ref[pl.ds(s, n), :]
            du_ref[1, pl.ds(s, n), :] = (d * (c * sg)).astype(du_ref.dtype)
            dc = d * v_ref[pl.ds(s, n), :] * (sg * (1.0 + c * (1.0 - sg)))
            dg_s[pl.ds(s, n), :] = dc
            dw = dw + _dconv_w(load_g, dc, 3, s, n)
        dw_ref[...] = dw
        for s, n in chunks:
            du_ref[0, pl.ds(s, n), :] = _conv_t_at(lambda a, b: dg_s[pl.ds(a, b), :], w, 3, s, n).astype(du_ref.dtype)

    return pl.pallas_call(
        body, name=name, grid=(f // cb,),
        in_specs=[pl.BlockSpec((m, cb), lambda j: (0, j)), pl.BlockSpec((None, m, cb), lambda j: (0, 0, j)),
                  pl.BlockSpec((None, m, cb), lambda j: (1, 0, j)), pl.BlockSpec((SUBLANES, cb), lambda j: (0, j))],
        out_specs=[pl.BlockSpec((2, m, cb), lambda j: (0, 0, j)), pl.BlockSpec((SUBLANES, cb), lambda j: (0, j))],
        out_shape=[jax.ShapeDtypeStruct((2, m, f), MXU_DTYPE), jax.ShapeDtypeStruct((SUBLANES, f), F32)],
        scratch_shapes=[pltpu.VMEM((m + SUBLANES, cb), F32)],
        compiler_params=_params(("parallel",)),
    )(da, u, u, cw)


def _shortconv_fwd(pm, cw, y, *, name):
    _, m, seg = pm.shape
    cb = _pick(seg, (256, 128))
    chunks = _row_chunks(m)

    def body(gi_ref, go_ref, ah_ref, w_ref, y_in, o_ref):
        del y_in
        w = w_ref[...]
        o_ref[pl.ds(0, SUBLANES), :] = jnp.zeros((SUBLANES, cb), o_ref.dtype)
        load_m = lambda a, b: gi_ref[pl.ds(a, b), :] * ah_ref[pl.ds(a, b), :]
        for s, n in chunks:
            o_ref[pl.ds(s, n), :] = (go_ref[pl.ds(s, n), :] * _conv_at(load_m, w, 3, s, n)).astype(o_ref.dtype)

    return pl.pallas_call(
        body, name=name, grid=(seg // cb,),
        in_specs=[pl.BlockSpec((None, m, cb), lambda j: (0, 0, j)), pl.BlockSpec((None, m, cb), lambda j: (1, 0, j)),
                  pl.BlockSpec((None, m, cb), lambda j: (2, 0, j)), pl.BlockSpec((SUBLANES, cb), lambda j: (0, j)),
                  pl.BlockSpec(memory_space=pl.ANY)],
        out_specs=pl.BlockSpec((m, cb), lambda j: (0, j)),
        out_shape=jax.ShapeDtypeStruct(y.shape, y.dtype),
        input_output_aliases={4: 0},
        compiler_params=_params(("parallel",)),
    )(pm, pm, pm, cw, y)


def _shortconv_bwd(dy, pm, cw, dpm, *, name):
    _, m, seg = pm.shape
    cb = LANES
    chunks = _row_chunks(m)

    def body(dy_ref, gi_ref, go_ref, ah_ref, w_ref, dpm_in, dp_ref, dw_ref, dc_s):
        del dpm_in
        w = w_ref[...]
        zeros8 = jnp.zeros((SUBLANES, cb), F32)
        dc_s[pl.ds(0, SUBLANES), :] = zeros8
        dc_s[pl.ds(m, SUBLANES), :] = zeros8
        for t in range(3):
            dp_ref[t, pl.ds(0, SUBLANES), :] = zeros8.astype(dp_ref.dtype)
        load_m = lambda a, b: gi_ref[pl.ds(a, b), :] * ah_ref[pl.ds(a, b), :]
        dw = jnp.zeros((SUBLANES, cb), F32)
        for s, n in chunks:
            d = dy_ref[pl.ds(s, n), :]
            dp_ref[1, pl.ds(s, n), :] = (d * _conv_at(load_m, w, 3, s, n)).astype(dp_ref.dtype)
            dc = d * go_ref[pl.ds(s, n), :]
            dc_s[pl.ds(s, n), :] = dc
            dw = dw + _dconv_w(load_m, dc, 3, s, n)
        dw_ref[...] = dw
        for s, n in chunks:
            dm = _conv_t_at(lambda a, b: dc_s[pl.ds(a, b), :], w, 3, s, n)
            dp_ref[0, pl.ds(s, n), :] = (dm * ah_ref[pl.ds(s, n), :]).astype(dp_ref.dtype)
            dp_ref[2, pl.ds(s, n), :] = (dm * gi_ref[pl.ds(s, n), :]).astype(dp_ref.dtype)

    return pl.pallas_call(
        body, name=name, grid=(seg // cb,),
        in_specs=[pl.BlockSpec((m, cb), lambda j: (0, j)), pl.BlockSpec((None, m, cb), lambda j: (0, 0, j)),
                  pl.BlockSpec((None, m, cb), lambda j: (1, 0, j)), pl.BlockSpec((None, m, cb), lambda j: (2, 0, j)),
                  pl.BlockSpec((SUBLANES, cb), lambda j: (0, j)), pl.BlockSpec(memory_space=pl.ANY)],
        out_specs=[pl.BlockSpec((3, m, cb), lambda j: (0, 0, j)), pl.BlockSpec((SUBLANES, cb), lambda j: (0, j))],
        out_shape=[jax.ShapeDtypeStruct(dpm.shape, dpm.dtype), jax.ShapeDtypeStruct((SUBLANES, seg), F32)],
        scratch_shapes=[pltpu.VMEM((m + SUBLANES, cb), F32)],
        input_output_aliases={5: 0},
        compiler_params=_params(("parallel",)),
    )(dy, pm, pm, pm, cw, dpm)


def _dnpre_fwd(pm, cw, *, name):
    _, m, seg = pm.shape
    cb = _pick(seg, (256, 128))
    per = seg // cb
    chunks = _row_chunks(m)

    def body(x_ref, w_ref, o_ref):
        w = w_ref[...]
        o_ref[pl.ds(0, SUBLANES), :] = jnp.zeros((SUBLANES, cb), F32)
        for s, n in chunks:
            c = _conv_at(lambda a, b: x_ref[pl.ds(a, b), :], w, 4, s, n)
            o_ref[pl.ds(s, n), :] = c * _sigmoid(c)

    return pl.pallas_call(
        body, name=name, grid=(3 * per,),
        in_specs=[pl.BlockSpec((None, m, cb), lambda j: (3 + j // per, 0, j % per)), pl.BlockSpec((SUBLANES, cb), lambda j: (0, j))],
        out_specs=pl.BlockSpec((None, m, cb), lambda j: (j // per, 0, j % per)),
        out_shape=jax.ShapeDtypeStruct((3, m, seg), F32),
        compiler_params=_params(("parallel",)),
    )(pm, cw)


def _dnpre_bwd(dqkv, pm, cw, dpm, *, name):
    _, m, seg = pm.shape
    cb = _pick(seg, (256, 128))
    per = seg // cb
    chunks = _row_chunks(m)

    def body(d_ref, x_ref, w_ref, dpm_in, dp_ref, dw_ref, dc_s):
        del dpm_in
        w = w_ref[...]
        zeros8 = jnp.zeros((SUBLANES, cb), F32)
        dc_s[pl.ds(0, SUBLANES), :] = zeros8
        dc_s[pl.ds(m, SUBLANES), :] = zeros8
        dp_ref[pl.ds(0, SUBLANES), :] = zeros8.astype(dp_ref.dtype)
        load_x = lambda a, b: x_ref[pl.ds(a, b), :]
        dw = jnp.zeros((SUBLANES, cb), F32)
        for s, n in chunks:
            c = _conv_at(load_x, w, 4, s, n)
            sg = _sigmoid(c)
            dc = d_ref[pl.ds(s, n), :] * (sg * (1.0 + c * (1.0 - sg)))
            dc_s[pl.ds(s, n), :] = dc
            dw = dw + _dconv_w(load_x, dc, 4, s, n)
        dw_ref[...] = dw
        for s, n in chunks:
            dp_ref[pl.ds(s, n), :] = _conv_t_at(lambda a, b: dc_s[pl.ds(a, b), :], w, 4, s, n).astype(dp_ref.dtype)

    return pl.pallas_call(
        body, name=name, grid=(3 * per,),
        in_specs=[pl.BlockSpec((None, m, cb), lambda j: (j // per, 0, j % per)),
                  pl.BlockSpec((None, m, cb), lambda j: (3 + j // per, 0, j % per)),
                  pl.BlockSpec((SUBLANES, cb), lambda j: (0, j)), pl.BlockSpec(memory_space=pl.ANY)],
        out_specs=[pl.BlockSpec((None, m, cb), lambda j: (3 + j // per, 0, j % per)), pl.BlockSpec((SUBLANES, cb), lambda j: (0, j))],
        out_shape=[jax.ShapeDtypeStruct(dpm.shape, dpm.dtype), jax.ShapeDtypeStruct((SUBLANES, 3 * seg), F32)],
        scratch_shapes=[pltpu.VMEM((m + SUBLANES, cb), F32)],
        input_output_aliases={3: 0},
        compiler_params=_params(("parallel",)),
    )(dqkv, pm, cw, dpm)


def _mxu_dot_impl(a, b, form):
    a = a.astype(MXU_DTYPE)
    b = b.astype(MXU_DTYPE)
    dims = {"nn": (((1,), (0,)), ((), ())), "nt": (((1,), (1,)), ((), ())), "tn": (((0,), (0,)), ((), ()))}[form]
    return lax.dot_general(a, b, dims, preferred_element_type=F32)


@functools.partial(jax.custom_vjp, nondiff_argnums=(2,))
def _mxu_dot(a, b, form):
    return _mxu_dot_impl(a, b, form)


def _mxu_dot_fwd(a, b, form):
    return _mxu_dot_impl(a, b, form), (a, b)


def _mxu_dot_bwd(form, saved, g):
    a, b = saved
    if form == "nn":
        return _mxu_dot_impl(g, b, "nt"), _mxu_dot_impl(a, g, "tn")
    if form == "nt":
        return _mxu_dot_impl(g, b, "nn"), _mxu_dot_impl(g, a, "tn")
    return _mxu_dot_impl(b, g, "nt"), _mxu_dot_impl(a, g, "nn")


_mxu_dot.defvjp(_mxu_dot_fwd, _mxu_dot_bwd)


_DOT_DIMS = {"nn": (((1,), (0,)), ((), ())), "nt": (((1,), (1,)), ((), ())), "tn": (((0,), (0,)), ((), ()))}


def _split(x):
    hi = x.astype(BF16)
    return hi, (x - hi.astype(F32)).astype(BF16)


def _dot3_impl(a, b, form):
    dg = lambda p, q: lax.dot_general(p, q, _DOT_DIMS[form], preferred_element_type=F32)
    ah, al = _split(a)
    bh, bl = _split(b)
    return dg(ah, bh) + (dg(ah, bl) + dg(al, bh))


@functools.partial(jax.custom_vjp, nondiff_argnums=(2,))
def _dot3(a, b, form):
    return _dot3_impl(a, b, form)


def _dot3_fwd(a, b, form):
    return _dot3_impl(a, b, form), (a, b)


def _dot3_bwd(form, saved, g):
    a, b = saved
    if form == "nn":
        return _dot3_impl(g, b, "nt"), _dot3_impl(a, g, "tn")
    if form == "nt":
        return _dot3_impl(g, b, "nn"), _dot3_impl(g, a, "tn")
    return _dot3_impl(b, g, "nt"), _dot3_impl(a, g, "nn")


_dot3.defvjp(_dot3_fwd, _dot3_bwd)


def _hdot(a, b):
    return _dot3(a, b, "nn")


def _mask_dot(mask, x, form):
    dg = lambda q: lax.dot_general(mask.astype(BF16), q, _DOT_DIMS[form], preferred_element_type=F32)
    x1 = x.astype(BF16)
    r1 = x - x1.astype(F32)
    x2 = r1.astype(BF16)
    x3 = (r1 - x2.astype(F32)).astype(BF16)
    return dg(x1) + (dg(x2) + dg(x3))


def _decay_masks(c):
    row = lax.broadcasted_iota(jnp.int32, (c, c), 0)
    col = lax.broadcasted_iota(jnp.int32, (c, c), 1)
    return (row >= col).astype(F32), row <= col


def _decay_impl(gb):
    lower, upper = _decay_masks(gb.shape[0])
    return _mask_dot(lower, gb, "nn"), _mask_dot(jnp.ones_like(gb), jnp.where(upper, gb, 0.0), "nn")


@jax.custom_vjp
def _decay_matrices(gb):
    return _decay_impl(gb)


def _decay_fwd(gb):
    return _decay_impl(gb), None


def _decay_bwd(_, cts):
    gc, gr = cts
    lower, upper = _decay_masks(gc.shape[0])
    return (_mask_dot(lower, gc, "tn") + jnp.where(upper, _mask_dot(jnp.ones_like(gr), gr, "tn"), 0.0),)


_decay_matrices.defvjp(_decay_fwd, _decay_bwd)


def _heads(f, *lists):
    return [f(*t) for t in zip(*lists)]


def _inverses_impl(a):
    c = a[0].shape[0]
    row = lax.broadcasted_iota(jnp.int32, (c, c), 0)
    col = lax.broadcasted_iota(jnp.int32, (c, c), 1)
    eye = jnp.where(row == col, 1.0, 0.0)
    x = _heads(lambda t: eye - t, a)
    p = _heads(lambda t: _dot3_impl(t, t, "nn"), a)
    power = 2
    while power < c:
        x = _heads(lambda s, t: s + _dot3_impl(s, t, "nn"), x, p)
        power *= 2
        if power < c:
            p = _heads(lambda t: _dot3_impl(t, t, "nn"), p)
    return x


@jax.custom_vjp
def _unit_lower_inverses(a):
    return _inverses_impl(a)


def _unit_lower_inverses_fwd(a):
    x = _inverses_impl(a)
    return x, x


def _unit_lower_inverses_bwd(x, g):
    t = _heads(lambda s, u: _dot3_impl(s, u, "tn"), x, g)
    return (_heads(lambda u, s: -_dot3_impl(u, s, "nt"), t, x),)


_unit_lower_inverses.defvjp(_unit_lower_inverses_fwd, _unit_lower_inverses_bwd)


def _softplus(x):
    return jnp.maximum(x, 0.0) + jnp.log(1.0 + jnp.exp(-jnp.abs(x)))


DN_STEP_CHUNKS = 6


def _dn_chunk(qr, kr, v, z, braw, araw, alog, dtb, nw, state, valid):
    c = DN_CHUNK
    nh = len(state)
    chunks = len(qr) // nh
    alog, dtb, valid_i = alog * chunks, dtb * chunks, [vv for vv in valid for _ in range(nh)]
    row = lax.broadcasted_iota(jnp.int32, (c, c), 0)
    col = lax.broadcasted_iota(jnp.int32, (c, c), 1)
    incl = row >= col
    strict = row > col
    q = _heads(lambda t: t * lax.rsqrt(jnp.sum(t * t, -1, keepdims=True) + EPS) * (DN_HEAD_DIM ** -0.5), qr)
    k = _heads(lambda t: t * lax.rsqrt(jnp.sum(t * t, -1, keepdims=True) + EPS), kr)
    beta = _heads(lambda t, vv: _sigmoid(t) * vv, braw, valid_i)
    g = _heads(lambda al, ar, dt, vv: -jnp.exp(al) * _softplus(ar + dt) * vv, alog, araw, dtb, valid_i)
    decay = _heads(lambda t: _decay_matrices(jnp.broadcast_to(t, (c, c))), g)
    dmask = _heads(lambda d: jnp.where(incl, jnp.exp(jnp.where(incl, d[0] - d[1], 0.0)), 0.0), decay)
    dec = _heads(lambda d: d[0][:, :1], decay)
    dlast = _heads(lambda d: d[0][c - 1:c, :1], decay)
    kk = _heads(lambda t: _mxu_dot(t, t, "nt"), k)
    a = _heads(lambda b, t, d: jnp.where(strict, b * t * d, 0.0), beta, kk, dmask)
    x = _unit_lower_inverses(a)
    uw = _heads(lambda s, vv, kk_, b, d: _hdot(s, jnp.concatenate([vv * b, kk_ * (b * jnp.exp(d))], 1)), x, v, k, beta, dec)
    u = _heads(lambda t: t[:, :DN_HEAD_DIM], uw)
    w = _heads(lambda t: t[:, DN_HEAD_DIM:], uw)
    qk = _heads(lambda s, t, d: _mxu_dot(s, t, "nt") * d, q, k, dmask)
    q_dec = _heads(lambda t, d: t * jnp.exp(d), q, dec)
    k_dec = _heads(lambda t, dl, d: t * jnp.exp(dl - d), k, dlast, dec)
    o = []
    for ci in range(chunks):
        of = lambda lst: lst[ci * nh:(ci + 1) * nh]
        v_new = _heads(lambda s, t, st: s - _mxu_dot(t, st, "nn"), of(u), of(w), state)
        o += _heads(lambda qd, st, s, vn: _mxu_dot(qd, st, "nn") + _mxu_dot(s, vn, "nn"), of(q_dec), state, of(qk), v_new)
        state = _heads(lambda st, dl, kd, vn: st * jnp.exp(dl) + _mxu_dot(kd, vn, "tn"), state, of(dlast), of(k_dec), v_new)
    y = _heads(lambda t, zz: _rms(t, nw) * (zz * _sigmoid(zz)), o, z)
    return y, state


DN_STEP_ROWS = DN_STEP_CHUNKS * DN_CHUNK
DN_ITEMS = [(ci, h) for ci in range(DN_STEP_CHUNKS) for h in range(DN_HEADS)]


def _dn_valid(n):
    rows = [n * DN_STEP_ROWS + ci * DN_CHUNK + lax.broadcasted_iota(jnp.int32, (DN_CHUNK, 1), 0) for ci in range(DN_STEP_CHUNKS)]
    return [(r >= PAD_ROWS).astype(F32) for r in rows]


def _dn_in_specs(rev, nc):
    cn = (lambda n: nc - 1 - n) if rev else (lambda n: n)
    c, hd = DN_STEP_ROWS, DN_HEAD_DIM
    return [
        pl.BlockSpec((None, c, DN_DIM), lambda n: (0, cn(n), 0)),
        pl.BlockSpec((None, c, DN_DIM), lambda n: (1, cn(n), 0)),
        pl.BlockSpec((None, c, DN_DIM), lambda n: (2, cn(n), 0)),
        pl.BlockSpec((None, c, DN_DIM), lambda n: (6, cn(n), 0)),
        pl.BlockSpec((DN_HEADS, 2, c, 1), lambda n: (0, 0, cn(n), 0)),
        pl.BlockSpec((DN_HEADS, SUBLANES, LANES), lambda n: (0, 0, 0)),
        pl.BlockSpec((1, hd), lambda n: (0, 0)),
    ]


def _rows(ci):
    return slice(ci * DN_CHUNK, (ci + 1) * DN_CHUNK)


def _cols(h):
    return slice(h * DN_HEAD_DIM, (h + 1) * DN_HEAD_DIM)


def _dn_load(q_ref, k_ref, v_ref, z_ref, ba_ref, hp_ref):
    heads = range(DN_HEADS)
    item = lambda ref: [ref[_rows(ci), _cols(h)] for ci, h in DN_ITEMS]
    return (item(q_ref), item(k_ref), item(v_ref), item(z_ref),
            [ba_ref[h, 0, _rows(ci), :] for ci, h in DN_ITEMS], [ba_ref[h, 1, _rows(ci), :] for ci, h in DN_ITEMS],
            [hp_ref[h, 0:1, 0:1] for h in heads], [hp_ref[h, 1:2, 0:1] for h in heads])


def _delta_fwd(qkvc, pm, ba, hp, nw, *, name):
    _, m, _ = qkvc.shape
    nc = m // DN_STEP_ROWS
    hd = DN_HEAD_DIM

    def body(q_ref, k_ref, v_ref, z_ref, ba_ref, hp_ref, nw_ref, y_ref, s_ref, state):
        n = pl.program_id(0)

        @pl.when(n == 0)
        def _():
            state[...] = jnp.zeros_like(state)

        heads = range(DN_HEADS)
        old = [state[h] for h in heads]
        y, new = _dn_chunk(*_dn_load(q_ref, k_ref, v_ref, z_ref, ba_ref, hp_ref), nw_ref[...], old, _dn_valid(n))
        for h in heads:
            s_ref[h] = old[h]
            state[h] = new[h]
        for (ci, h), yy in zip(DN_ITEMS, y):
            y_ref[_rows(ci), _cols(h)] = yy.astype(y_ref.dtype)

    return pl.pallas_call(
        body, name=name, grid=(nc,), in_specs=_dn_in_specs(False, nc),
        out_specs=[pl.BlockSpec((DN_STEP_ROWS, DN_DIM), lambda n: (n, 1)), pl.BlockSpec((DN_HEADS, None, hd, hd), lambda n: (0, n, 0, 0))],
        out_shape=[jax.ShapeDtypeStruct((m, D_CONV + DN_DIM), MXU_DTYPE), jax.ShapeDtypeStruct((DN_HEADS, nc, hd, hd), F32)],
        scratch_shapes=[pltpu.VMEM((DN_HEADS, hd, hd), F32)],
        compiler_params=_params(("arbitrary",)),
    )(qkvc, qkvc, qkvc, pm, ba, hp, nw)


def _delta_bwd(dy, qkvc, pm, ba, hp, nw, states, *, name):
    _, m, _ = qkvc.shape
    nc = m // DN_STEP_ROWS
    hd, c = DN_HEAD_DIM, DN_STEP_ROWS

    def body(q_ref, k_ref, v_ref, z_ref, ba_ref, hp_ref, nw_ref, s_ref, dy_ref,
             dz_ref, dqkv_ref, dba_ref, dhp_ref, dnw_ref, dstate):
        step = pl.program_id(0)
        n = nc - 1 - step

        @pl.when(step == 0)
        def _():
            dstate[...] = jnp.zeros_like(dstate)
            dhp_ref[...] = jnp.zeros_like(dhp_ref)
            dnw_ref[...] = jnp.zeros_like(dnw_ref)

        valid = _dn_valid(n)
        heads = range(DN_HEADS)
        fn = lambda *a: _dn_chunk(*a, valid)
        _, vjp = jax.vjp(fn, *_dn_load(q_ref, k_ref, v_ref, z_ref, ba_ref, hp_ref), nw_ref[...], [s_ref[h] for h in heads])
        dy = [dy_ref[_rows(ci), _cols(h)] for ci, h in DN_ITEMS]
        dq, dk, dv, dz, dbr, dar, dalog, ddtb, dnw, dst = vjp((dy, [dstate[h] for h in heads]))
        for i, (ci, h) in enumerate(DN_ITEMS):
            dqkv_ref[0, _rows(ci), _cols(h)] = dq[i]
            dqkv_ref[1, _rows(ci), _cols(h)] = dk[i]
            dqkv_ref[2, _rows(ci), _cols(h)] = dv[i]
            dz_ref[_rows(ci), _cols(h)] = dz[i].astype(dz_ref.dtype)
            dba_ref[h, 0, _rows(ci), :] = dbr[i]
            dba_ref[h, 1, _rows(ci), :] = dar[i]
        for h in heads:
            dstate[h] = dst[h]
            dhp_ref[h] += jnp.concatenate([jnp.broadcast_to(dalog[h], (1, LANES)), jnp.broadcast_to(ddtb[h], (1, LANES)),
                                           jnp.zeros((SUBLANES - 2, LANES), F32)], 0)
        dnw_ref[...] += dnw

    rn = lambda n: nc - 1 - n
    in_specs = _dn_in_specs(True, nc) + [
        pl.BlockSpec((DN_HEADS, None, hd, hd), lambda n: (0, rn(n), 0, 0)),
        pl.BlockSpec((c, DN_DIM), lambda n: (rn(n), 1)),
    ]
    out_specs = [
        pl.BlockSpec((None, c, DN_DIM), lambda n: (6, rn(n), 0)),
        pl.BlockSpec((3, c, DN_DIM), lambda n: (0, rn(n), 0)),
        pl.BlockSpec((DN_HEADS, 2, c, 1), lambda n: (0, 0, rn(n), 0)),
        pl.BlockSpec((DN_HEADS, SUBLANES, LANES), lambda n: (0, 0, 0)),
        pl.BlockSpec((1, hd), lambda n: (0, 0)),
    ]
    return pl.pallas_call(
        body, name=name, grid=(nc,), in_specs=in_specs, out_specs=out_specs,
        out_shape=[jax.ShapeDtypeStruct(pm.shape, MXU_DTYPE), jax.ShapeDtypeStruct(qkvc.shape, F32),
                   jax.ShapeDtypeStruct(ba.shape, F32), jax.ShapeDtypeStruct(hp.shape, F32),
                   jax.ShapeDtypeStruct((1, hd), F32)],
        scratch_shapes=[pltpu.VMEM((DN_HEADS, hd, hd), F32)],
        compiler_params=_params(("arbitrary",)),
    )(qkvc, qkvc, qkvc, pm, ba, hp, nw, states, dy)


SWA_PAIR = 4


def _attn_block(q, k0, kp, kc, v0, vp, vc, qw, kw, sink, n):
    g, b, hd = SWA_GROUP, SWA_BLOCK, SWA_HEAD_DIM
    pair = list(range(SWA_PAIR))
    lanes = lambda t, e: t[:, e * hd:(e + 1) * hd]
    q4 = [jnp.concatenate([lanes(q, e * g + i)[None] for i in range(g)], 0) for e in pair]
    qn = _heads(lambda t: _rms(t, qw) * (hd ** -0.5), q4)
    kn = [_rms(jnp.concatenate([lanes(k0, e), lanes(kp, e), lanes(kc, e)], 0), kw) for e in pair]
    vcat = [jnp.concatenate([lanes(v0, e), lanes(vp, e), lanes(vc, e)], 0) for e in pair]
    s = _heads(lambda a, k: _mxu_dot(a.reshape(g * b, hd), k, "nt").reshape(g, b, 3 * b), qn, kn)
    i = lax.broadcasted_iota(jnp.int32, (b, 3 * b), 0)
    c = lax.broadcasted_iota(jnp.int32, (b, 3 * b), 1)
    in_meta, in_prev, in_cur = c < b, (c >= b) & (c < 2 * b), c >= 2 * b
    j = c - jnp.where(in_meta, 0, jnp.where(in_prev, b, 2 * b))
    meta_lo = jnp.where(n == 0, b, PAD_ROWS)
    cur_lo = jnp.where(n == 0, PAD_ROWS, 0)
    prev_off = jnp.where(n >= 2, 0, 2 * b)
    valid = (in_meta & (j >= meta_lo)) | (in_prev & (j > i + prev_off)) | (in_cur & (j <= i) & (j >= cur_lo))
    s = _heads(lambda t: jnp.where(valid[None], t, NEG), s)
    m = [lax.stop_gradient(jnp.maximum(jnp.max(t, -1, keepdims=True), sink[e])) for e, t in zip(pair, s)]
    ex = _heads(lambda t, mm: jnp.exp(t - mm), s, m)
    p = [t / (jnp.sum(t, -1, keepdims=True) + jnp.exp(sink[e] - mm)) for e, t, mm in zip(pair, ex, m)]
    o = _heads(lambda t, v: _mxu_dot(t.reshape(g * b, 3 * b), v, "nn").reshape(g, b, hd), p, vcat)
    return jnp.concatenate([o[e][i] for e in pair for i in range(g)], 1)


Q_LANES = SWA_PAIR * SWA_GROUP * SWA_HEAD_DIM
KV_LANES = SWA_PAIR * SWA_HEAD_DIM
K_BLOCK0 = SWA_HEADS * SWA_HEAD_DIM // KV_LANES
V_BLOCK0 = K_BLOCK0 + SWA_KV_HEADS * SWA_HEAD_DIM // KV_LANES


def _attn_in_specs():
    g, b, hd = SWA_GROUP, SWA_BLOCK, SWA_HEAD_DIM
    kv = lambda f, first: pl.BlockSpec((b, KV_LANES), lambda p, n: (f(n), first + p))
    blocks = [lambda n: 0, lambda n: jnp.maximum(n - 1, 0), lambda n: n]
    return ([pl.BlockSpec((b, Q_LANES), lambda p, n: (n, p))] + [kv(f, K_BLOCK0) for f in blocks] + [kv(f, V_BLOCK0) for f in blocks]
            + [pl.BlockSpec((1, hd), lambda p, n: (0, 0)), pl.BlockSpec((1, hd), lambda p, n: (0, 0)),
               pl.BlockSpec((SWA_PAIR, g, 1, 1), lambda p, n: (p, 0, 0, 0))])


def _attn_fwd(qkv, qw, kw, sink, *, name):
    m = qkv.shape[0]
    b = SWA_BLOCK

    def body(q_ref, k0, kp, kc, v0, vp, vc, qw_ref, kw_ref, s_ref, o_ref):
        o_ref[...] = _attn_block(q_ref[...], k0[...], kp[...], kc[...], v0[...], vp[...], vc[...], qw_ref[...], kw_ref[...],
                                 s_ref[...], pl.program_id(1)).astype(o_ref.dtype)

    return pl.pallas_call(
        body, name=name, grid=(SWA_KV_HEADS // SWA_PAIR, m // b), in_specs=_attn_in_specs(),
        out_specs=pl.BlockSpec((b, Q_LANES), lambda p, n: (n, p)),
        out_shape=jax.ShapeDtypeStruct((m, SWA_HEADS * SWA_HEAD_DIM), MXU_DTYPE),
        compiler_params=_params(("parallel", "parallel")),
    )(*([qkv] * 7), qw, kw, sink)


def _attn_bwd(do, qkv, qw, kw, sink, *, name):
    m = qkv.shape[0]
    g, b, hd = SWA_GROUP, SWA_BLOCK, SWA_HEAD_DIM

    def body(q_ref, k0, kp, kc, v0, vp, vc, qw_ref, kw_ref, s_ref, do_ref, dq_ref, dk_ref, dv_ref, dqw_ref, dkw_ref, ds_ref):
        n = pl.program_id(1)

        @pl.when(n == 0)
        def _():
            for r in (dk_ref, dv_ref, dqw_ref, dkw_ref, ds_ref):
                r[...] = jnp.zeros_like(r)

        fn = lambda *a: _attn_block(*a, n)
        _, vjp = jax.vjp(fn, q_ref[...], k0[...], kp[...], kc[...], v0[...], vp[...], vc[...], qw_ref[...], kw_ref[...], s_ref[...])
        dq, dk0, dkp, dkc, dv0, dvp, dvc, dqw, dkw, dsk = vjp(do_ref[...])
        dq_ref[...] = dq
        prev = pl.multiple_of(jnp.maximum(n - 1, 0) * b, b)
        cur = pl.multiple_of(n * b, b)
        for ref, parts in ((dk_ref, (dk0, dkp, dkc)), (dv_ref, (dv0, dvp, dvc))):
            ref[pl.ds(0, b), :] += parts[0]
            ref[pl.ds(prev, b), :] += parts[1]
            ref[pl.ds(cur, b), :] += parts[2]
        dqw_ref[...] += dqw
        dkw_ref[...] += dkw
        ds_ref[...] += dsk

    pairs = SWA_KV_HEADS // SWA_PAIR
    kv_acc = pl.BlockSpec((m, KV_LANES), lambda p, n: (0, p))
    w_acc = pl.BlockSpec((None, 1, hd), lambda p, n: (p, 0, 0))
    kv_shape = jax.ShapeDtypeStruct((m, SWA_KV_HEADS * hd), F32)
    return pl.pallas_call(
        body, name=name, grid=(pairs, m // b),
        in_specs=_attn_in_specs() + [pl.BlockSpec((b, Q_LANES), lambda p, n: (n, p))],
        out_specs=[pl.BlockSpec((b, Q_LANES), lambda p, n: (n, p)), kv_acc, kv_acc, w_acc, w_acc,
                   pl.BlockSpec((SWA_PAIR, g, 1, 1), lambda p, n: (p, 0, 0, 0))],
        out_shape=[jax.ShapeDtypeStruct((m, SWA_HEADS * hd), F32), kv_shape, kv_shape,
                   jax.ShapeDtypeStruct((pairs, 1, hd), F32), jax.ShapeDtypeStruct((pairs, 1, hd), F32),
                   jax.ShapeDtypeStruct(sink.shape, F32)],
        compiler_params=_params(("parallel", "arbitrary")),
    )(*([qkv] * 7), qw, kw, sink, do)


def _loss_bwd(h, target, *, name):
    m, d = h.shape
    b = SWA_BLOCK

    def body(h_ref, t_ref, l_ref, dh_ref):
        i = pl.program_id(0)

        @pl.when(i == 0)
        def _():
            l_ref[...] = jnp.zeros_like(l_ref)
            dh_ref[...] = jnp.zeros_like(dh_ref)

        @pl.when(i > 0)
        def _():
            e = h_ref[...] - t_ref[...]
            dh_ref[...] = e * (1.0 / d)
            l_ref[...] += jnp.sum(jnp.sum(e * e, 0, keepdims=True), 1, keepdims=True) * (0.5 / d)

    return pl.pallas_call(
        body, name=name, grid=(m // b,),
        in_specs=[pl.BlockSpec((b, d), lambda i: (i, 0)), pl.BlockSpec((b, d), lambda i: (jnp.maximum(i - 1, 0), 0))],
        out_specs=[pl.BlockSpec((1, LANES), lambda i: (0, 0)), pl.BlockSpec((b, d), lambda i: (i, 0))],
        out_shape=[jax.ShapeDtypeStruct((1, LANES), F32), jax.ShapeDtypeStruct((m, d), F32)],
        compiler_params=_params(("arbitrary",)),
    )(h, target)


def _ffn_fwd(h, nw, w_up_t, cw, w_down, tag):
    u, hn = _norm_matmul(h, nw, w_up_t, o_seg=D_FF, trans_w=True, name=f"ffn_up_{tag}")
    a = _ffn_act_fwd(u, cw, name=f"ffn_act_{tag}")
    return _mm_nn(a, w_down, res=h, name=f"ffn_down_{tag}"), (h, hn, u, a)


def _ffn_bwd(dh, saved, nw, w_up_t, cw, w_down, tag):
    h, hn, u, a = saved
    da = _mm_nn(dh, w_down, trans_w=True, name=f"ffn_da_{tag}")
    dw_down = _mm_tn(a, dh, name=f"ffn_dwdown_{tag}")
    du, dcw = _ffn_act_bwd(da, u, cw, name=f"ffn_act_bwd_{tag}")
    dhn = _mm_nn(du, w_up_t, a_seg=True, name=f"ffn_dhn_{tag}")
    dw_up_t = _mm_tn(du, hn, a_seg=True, name=f"ffn_dwup_{tag}")
    dh_in, dnw = _rmsnorm_bwd(dhn, h, nw, dh, name=f"ffn_norm_bwd_{tag}")
    return dh_in, dnw, dw_up_t, dcw, dw_down


def _local_step(x, target, w, fetch=None, push=None):
    fetch = fetch or (lambda stage, after: {})
    push = push or (lambda stage, grads: None)
    plus = lambda a, zero: a if zero is None else a + zero
    seq, d = x.shape
    m = PAD_ROWS + N_META + seq
    h0 = jnp.concatenate([jnp.zeros((PAD_ROWS, d), F32), w["meta"], x], 0)

    pm, hn0 = _norm_matmul(h0, w["anw"][0], w["w_in_t"], o_seg=SEG, trans_w=True, n=N_SEG * SEG, name="mix_in")
    pba = _mm_nn(hn0, w["w_in_tail_t"], trans_w=True, name="mix_in_tail")
    qkvc = _dnpre_fwd(pm, w["dcw"], name="dn_conv")
    ba = pba[:, :2 * DN_HEADS].T.reshape(2, DN_HEADS, m, 1).transpose(1, 0, 2, 3)
    y, states = _delta_fwd(qkvc, pm, ba, w["hp"], w["dnw"], name="delta")
    y = _shortconv_fwd(pm, w["caw"], y, name="shortconv")
    w = {**w, **fetch("l0", y)}
    h1 = _mm_nn(y, w["w_out"], res=h0, name="mix_out")
    h2, ffn0 = _ffn_fwd(h1, w["fnw"][0], w["w_up0_t"], w["fcw"][0], w["w_down0"], "l0")

    w = {**w, **fetch("l1", h2)}
    qkv, hn2 = _norm_matmul(h2, w["anw"][1], w["wqkv"], name="attn_qkv")
    o = _attn_fwd(qkv, w["qnw"], w["knw"], w["sink"], name="attn")
    h3 = _mm_nn(o, w["wo"], res=h2, name="attn_out")
    h4, ffn1 = _ffn_fwd(h3, w["fnw"][1], w["w_up1_t"], w["fcw"][1], w["w_down1"], "l1")

    loss, dh4 = _loss_bwd(h4, target, name="loss")

    g = {}
    dh3, dfnw1, dwup1, dfcw1, dwdown1 = _ffn_bwd(dh4, ffn1, w["fnw"][1], w["w_up1_t"], w["fcw"][1], w["w_down1"], "l1")

    do = _mm_nn(dh3, w["wo"], trans_w=True, name="attn_do")
    g["wo"] = _mm_tn(o, dh3, name="attn_dwo")
    dq, dk, dv, dqw, dkw, dsink = _attn_bwd(do, qkv, w["qnw"], w["knw"], w["sink"], name="attn_bwd")
    dqkv = jnp.concatenate([dq, dk, dv], 1).astype(MXU_DTYPE)
    dhn2 = _mm_nn(dqkv, w["wqkv"], trans_w=True, name="attn_dhn")
    g["wqkv"] = _mm_tn(hn2, dqkv, name="attn_dwqkv")
    zero = push("l1", dict(w_up_t=dwup1, w_down=dwdown1, wo=g["wo"], wqkv=g["wqkv"]))
    dh2, danw1 = _rmsnorm_bwd(dhn2, h2, plus(w["anw"][1], zero), dh3, name="attn_norm_bwd")

    dh1, dfnw0, dwup0, dfcw0, dwdown0 = _ffn_bwd(dh2, ffn0, w["fnw"][0], w["w_up0_t"], w["fcw"][0], w["w_down0"], "l0")

    dy = _mm_nn(dh1, w["w_out"], trans_w=True, name="mix_dy")
    g["w_out"] = _mm_tn(y, dh1, name="mix_dwout")
    zero = push("l0", dict(w_up_t=dwup0, w_down=dwdown0, w_out=g["w_out"]))
    dpm, dqkvc, dba, dhp, ddnw = _delta_bwd(dy, qkvc, pm, ba, w["hp"], plus(w["dnw"], zero), states, name="delta_bwd")
    dpm, ddcw = _dnpre_bwd(dqkvc, pm, w["dcw"], dpm, name="dn_conv_bwd")
    dpm, dcaw = _shortconv_bwd(dy, pm, w["caw"], dpm, name="shortconv_bwd")
    dpba = jnp.pad(dba.transpose(1, 0, 2, 3).reshape(2 * DN_HEADS, m).T, ((0, 0), (0, LANES - 2 * DN_HEADS))).astype(MXU_DTYPE)
    g["w_in_t"] = jnp.concatenate([_mm_tn(dpm, hn0, a_seg=True, out_dtype=F32, name="mix_dwin"),
                                   _mm_tn(dpba, hn0, out_dtype=F32, name="mix_dwin_tail")[:N_TAIL]], 0).astype(GRAD_WIRE_DTYPE)
    zero = push("in", dict(w_in_t=g["w_in_t"]))
    dhn0 = _mm_nn(dpba, plus(w["w_in_tail_t"], None if zero is None else zero.astype(MXU_DTYPE)), name="mix_dhn_tail")
    dhn0 = _mm_nn(dpm, w["w_in_t"], res=dhn0, a_seg=True, name="mix_dhn")
    dh0, danw0 = _rmsnorm_bwd(dhn0, h0, w["anw"][0], dh1, name="mix_norm_bwd")

    g.update(
        x=dh0[PAD_ROWS + N_META:], meta=dh0[PAD_ROWS:PAD_ROWS + N_META], anw=[danw0, danw1], fnw=[dfnw0, dfnw1],
        caw=dcaw, dcw=ddcw, hp=dhp, dnw=ddnw, qnw=jnp.sum(dqw, 0), knw=jnp.sum(dkw, 0), sink=dsink,
        w_up_t=[dwup0, dwup1], fcw=[dfcw0, dfcw1], w_down=[dwdown0, dwdown1])
    return loss, g


N_TAIL = 2 * DN_HEADS


def _prepare_early(p):
    n_main = N_SEG * SEG
    w_in_t = p["mix_w_in_t"]
    tail_t = jnp.pad(w_in_t[n_main:], ((0, LANES - N_TAIL), (0, 0)))
    hp = jnp.zeros((DN_HEADS, SUBLANES, LANES), F32)
    hp = hp.at[:, 0, :].set(p["dn_a_log"][0][:, None]).at[:, 1, :].set(p["dn_dt_bias"][0][:, None])
    depth = p["ffn_conv_w"].shape[0]
    return dict(
        meta=p["meta_tokens"], anw=[p["attn_norm_w"][i:i + 1] for i in range(depth)],
        fnw=[p["ffn_norm_w"][i:i + 1] for i in range(depth)],
        w_in_t=w_in_t, w_in_tail_t=tail_t,
        caw=_pad_w(p["conv_a_w"][0]), dcw=_pad_w(p["dn_conv_w"][0]), hp=hp, dnw=p["dn_norm_w"],
        qnw=p["swa_q_norm_w"], knw=p["swa_k_norm_w"], sink=p["swa_sinks"].reshape(SWA_KV_HEADS, SWA_GROUP, 1, 1),
        fcw=[_pad_w(p["ffn_conv_w"][i]) for i in range(depth)])


def _prepare_weights(p):
    return dict(
        _prepare_early(dict(p, mix_w_in_t=p["mix_w_in"][0].T)), w_out=p["mix_w_out"][0], wo=p["swa_wo"][0],
        wqkv=jnp.concatenate([p["swa_wq"][0], p["swa_wk"][0], p["swa_wv"][0]], 1),
        w_up0_t=p["ffn_w_up"][0].T, w_up1_t=p["ffn_w_up"][1].T, w_down0=p["ffn_w_down"][0], w_down1=p["ffn_w_down"][1])


def _small_named(g):
    return dict(
        meta_tokens=g["meta"], attn_norm_w=jnp.concatenate(g["anw"], 0), ffn_norm_w=jnp.concatenate(g["fnw"], 0),
        conv_a_w=g["caw"][None, :3], dn_conv_w=g["dcw"][None, :4],
        dn_a_log=g["hp"][None, :, 0, 0], dn_dt_bias=g["hp"][None, :, 1, 0], dn_norm_w=g["dnw"],
        swa_q_norm_w=g["qnw"], swa_k_norm_w=g["knw"], swa_sinks=g["sink"].reshape(1, SWA_HEADS),
        ffn_conv_w=jnp.stack([c[:3] for c in g["fcw"]]))


def _reference_named(g):
    nq, nkv = SWA_HEADS * SWA_HEAD_DIM, SWA_KV_HEADS * SWA_HEAD_DIM
    return dict(
        _small_named(g), mix_w_in=g["w_in_t"].T[None],
        mix_w_out=g["w_out"][None], swa_wq=g["wqkv"][None, :, :nq], swa_wk=g["wqkv"][None, :, nq:nq + nkv],
        swa_wv=g["wqkv"][None, :, nq + nkv:], swa_wo=g["wo"][None],
        ffn_w_up=jnp.stack([t.T for t in g["w_up_t"]]), ffn_w_down=jnp.stack(g["w_down"]))


def _my_index():
    return 4 * lax.axis_index("x") + 2 * lax.axis_index("y") + lax.axis_index("c")


def _all_gather(arrays, *, name):
    n = len(arrays)

    def body(*refs):
        ins, outs = refs[:n], refs[n:2 * n]
        send_sems, recv_sems, local_sems = refs[2 * n:]
        x, y, c = lax.axis_index("x"), lax.axis_index("y"), lax.axis_index("c")
        me, sibling = (x, y, c), (x, y, 1 - c)
        chips = [(1 - x, y), (x, 1 - y), (1 - x, 1 - y)]

        def copy(i, k, block, to, src=None):
            rows = outs[i].at[4 * block[0] + 2 * block[1] + block[2]]
            return pltpu.make_async_remote_copy(
                src_ref=rows if src is None else src, dst_ref=rows, send_sem=send_sems.at[i, k], recv_sem=recv_sems.at[i, k],
                device_id=to, device_id_type=pl.DeviceIdType.MESH)

        mine = [pltpu.make_async_copy(ins[i], outs[i].at[4 * x + 2 * y + c], local_sems.at[i]) for i in range(n)]
        first = []
        for j, chip in enumerate(chips):
            first += [copy(i, 1 + j, me, (*chip, c), src=ins[i]) for i in range(n)]
        first += [copy(i, 0, me, sibling, src=ins[i]) for i in range(n)]
        for cp in first + mine:
            cp.start()
        passed = []
        for j, chip in enumerate(chips):
            for i in range(n):
                copy(i, 1 + j, (*chip, c), me).wait_recv()
                fwd = copy(i, 4 + j, (*chip, c), sibling)
                fwd.start()
                passed.append(fwd)
        for i in range(n):
            copy(i, 0, sibling, me).wait_recv()
            for j, chip in enumerate(chips):
                copy(i, 4 + j, (*chip, 1 - c), me).wait_recv()
        for cp in first + passed:
            cp.wait_send()
        for cp in mine:
            cp.wait()

    hbm = pl.BlockSpec(memory_space=pl.ANY)
    return pl.pallas_call(
        body, name=name, in_specs=[hbm] * n, out_specs=[hbm] * n,
        out_shape=[jax.ShapeDtypeStruct((N_DEV,) + tuple(a.shape), a.dtype) for a in arrays],
        scratch_shapes=[pltpu.SemaphoreType.DMA((n, 7)), pltpu.SemaphoreType.DMA((n, 7)), pltpu.SemaphoreType.DMA((n,))],
    )(*arrays)


def _peer(d):
    px, py, pc = lax.axis_index("x") ^ (d >> 2), lax.axis_index("y") ^ ((d >> 1) & 1), lax.axis_index("c") ^ (d & 1)
    return (px, py, pc), 4 * px + 2 * py + pc


def _push_copies(mode, srcs, lands, send_sems, recv_sems):
    me = _my_index()
    out = []
    for d in range(1, N_DEV):
        pos, idx = _peer(d)
        for i in range(len(srcs)):
            out.append(pltpu.make_async_remote_copy(
                src_ref=srcs[i] if mode == "gather" else srcs[i].at[idx], dst_ref=lands[i].at[me],
                send_sem=send_sems.at[i * N_DEV + d], recv_sem=recv_sems.at[i * N_DEV + d], device_id=pos,
                device_id_type=pl.DeviceIdType.MESH))
    return out


_HBM = pl.BlockSpec(memory_space=pltpu.HBM)
_SEM = pl.BlockSpec(memory_space=pltpu.SEMAPHORE)


def _push_start(mode, arrays, follows, *, name):
    n = len(arrays)
    blocks = [a.shape if mode == "gather" else a.shape[1:] for a in arrays]
    lands = [lax.empty((N_DEV,) + tuple(b), a.dtype) for a, b in zip(arrays, blocks)]

    def body(*refs):
        srcs, land_refs = refs[:n], refs[n:2 * n]
        send_sems, recv_sems = refs[2 * n + 1], refs[2 * n + 2]
        zero = refs[-1]
        for cp in _push_copies(mode, srcs, land_refs, send_sems, recv_sems):
            cp.start()
        zero[...] = jnp.zeros_like(zero)

    hbm_in = [pltpu.with_memory_space_constraint(a, pltpu.HBM) for a in list(arrays) + lands]
    outs = pl.pallas_call(
        body, name=name,
        out_shape=[pltpu.SemaphoreType.DMA((n * N_DEV,)), pltpu.SemaphoreType.DMA((n * N_DEV,))]
        + [pltpu.HBM(a.shape, a.dtype) for a in hbm_in] + [jax.ShapeDtypeStruct((SUBLANES, LANES), F32)],
        in_specs=[_HBM] * (2 * n) + [pl.BlockSpec(memory_space=pl.ANY)],
        out_specs=[_SEM, _SEM] + [_HBM] * (2 * n) + [pl.BlockSpec(memory_space=pltpu.VMEM)],
        input_output_aliases={i: 2 + i for i in range(2 * n)},
        compiler_params=pltpu.CompilerParams(has_side_effects=pltpu.SideEffectType.DATAFLOW_SIDE_EFFECTING),
    )(*hbm_in, follows)
    return dict(mode=mode, sems=outs[:2], srcs=outs[2:2 + n], lands=outs[2 + n:2 + 2 * n], zero=outs[-1])


def _push_wait(push, follows, *, name):
    n = len(push["srcs"])
    mode = push["mode"]

    def body(*refs):
        srcs, land_refs = refs[:n], refs[n:2 * n]
        send_sems, recv_sems = refs[2 * n], refs[2 * n + 1]
        for cp in _push_copies(mode, srcs, land_refs, send_sems, recv_sems):
            cp.wait_send()
            cp.wait_recv()

    args = list(push["srcs"]) + list(push["lands"])
    outs = pl.pallas_call(
        body, name=name, out_shape=[pltpu.HBM(a.shape, a.dtype) for a in args],
        in_specs=[_HBM] * (2 * n) + [_SEM, _SEM, pl.BlockSpec(memory_space=pl.ANY)], out_specs=[_HBM] * (2 * n),
        input_output_aliases={i: i for i in range(2 * n)},
        compiler_params=pltpu.CompilerParams(has_side_effects=pltpu.SideEffectType.DATAFLOW_SIDE_EFFECTING),
    )(*args, *push["sems"], follows)
    me = _my_index()
    got = []
    for src, land in zip(outs[:n], outs[n:]):
        own = src if mode == "gather" else lax.dynamic_index_in_dim(src, me, 0, keepdims=False)
        got.append(lax.dynamic_update_index_in_dim(land, own, me, 0))
    return got


ADAMW_BLOCK_BYTES = 6 * 1024 * 1024


def _adamw_tile(r, c):
    fits = lambda tr, tc: N_DEV * tr * tc * 4 <= ADAMW_BLOCK_BYTES
    rows = [t for t in range(2 * SUBLANES, r + 1, 2 * SUBLANES) if r % t == 0 and fits(t, c)]
    if rows or fits(r, c):
        return (max(rows) if rows else r), c
    cols = [t for t in range(LANES, c + 1, LANES) if c % t == 0 and fits(r, t)]
    return r, max(cols)


def _adamw(parts, w, m, v, layer, outs=None, *, name):
    nl, r, c = w.shape
    tr, tc = _adamw_tile(r, c)

    def body(p_ref, w_ref, m_ref, v_ref, *rest):
        g_ref, d_ref, nm_ref, nv_ref = rest[-4:]
        g = p_ref[0].astype(F32)
        for j in range(1, N_DEV):
            g = g + p_ref[j].astype(F32)
        m2 = ADAM_B1 * m_ref[...] + (1.0 - ADAM_B1) * g
        v2 = ADAM_B2 * v_ref[...] + (1.0 - ADAM_B2) * jnp.square(g)
        m_hat = m2 / (1.0 - ADAM_B1 ** ADAM_STEP)
        v_hat = v2 / (1.0 - ADAM_B2 ** ADAM_STEP)
        g_ref[...] = g
        d_ref[...] = -ADAM_LR * (m_hat / (jnp.sqrt(v_hat) + ADAM_EPS) + ADAM_WD * w_ref[...])
        nm_ref[...] = m2
        nv_ref[...] = v2

    blk = pl.BlockSpec((None, tr, tc), lambda i, j: (layer, i, j))
    out = jax.ShapeDtypeStruct((nl, r, c), F32)
    given = list(outs) if outs is not None else []
    return pl.pallas_call(
        body, name=name, grid=(r // tr, c // tc),
        in_specs=[pl.BlockSpec((N_DEV, tr, tc), lambda i, j: (0, i, j)), blk, blk, blk] + [pl.BlockSpec(memory_space=pl.ANY)] * len(given),
        out_specs=[blk, blk, blk, blk], out_shape=[out, out, out, out],
        input_output_aliases={4 + t: t for t in range(len(given))},
        compiler_params=_params(("parallel", "parallel")),
    )(parts, w, m, v, *given)


SHARD_AXIS = dict(
    meta_tokens=1, attn_norm_w=None, ffn_norm_w=None, mix_w_in=2, conv_a_w=2, dn_conv_w=2, dn_a_log=None, dn_dt_bias=None,
    dn_norm_w=None, mix_w_out=1, swa_wq=1, swa_wk=1, swa_wv=1, swa_q_norm_w=None, swa_k_norm_w=None, swa_sinks=None,
    swa_wo=1, ffn_w_up=2, ffn_conv_w=2, ffn_w_down=1)
WEIGHTS = list(SHARD_AXIS)
BIG = ["mix_w_in", "mix_w_out", "swa_wq", "swa_wk", "swa_wv", "swa_wo", "ffn_w_up", "ffn_w_down"]
SMALL = [k for k in WEIGHTS if k not in BIG]
SMALL_SHARDED = [k for k in SMALL if SHARD_AXIS[k] is not None]


def _whole(g8, axis):
    t = jnp.moveaxis(g8, 0, axis)
    return t.reshape(t.shape[:axis] + (t.shape[axis] * t.shape[axis + 1],) + t.shape[axis + 2:])


def _by_owner(a, axis):
    s = a.shape[axis] // N_DEV
    return jnp.moveaxis(a.reshape(a.shape[:axis] + (N_DEV, s) + a.shape[axis + 1:]), axis, 0)


def _pack(arrays, lead=0):
    flat = jnp.concatenate([a.reshape(a.shape[:lead] + (-1,)) for a in arrays], -1)
    n = flat.shape[-1]
    rows = -(-n // (SUBLANES * LANES)) * SUBLANES
    flat = jnp.pad(flat, [(0, 0)] * lead + [(0, rows * LANES - n)])
    return flat.reshape(flat.shape[:lead] + (rows, LANES))


def _unpack(buf, shapes, lead=0):
    flat = buf.reshape(buf.shape[:lead] + (-1,))
    out, o = [], 0
    for s in shapes:
        n = 1
        for e in s:
            n *= e
        out.append(flat[..., o:o + n].reshape(buf.shape[:lead] + tuple(s)))
        o += n
    return out


def kernel(x, meta_tokens, attn_norm_w, ffn_norm_w, mix_w_in, conv_a_w, dn_conv_w, dn_a_log, dn_dt_bias, dn_norm_w, mix_w_out, swa_wq, swa_wk, swa_wv, swa_q_norm_w, swa_k_norm_w, swa_sinks, swa_wo, ffn_w_up, ffn_conv_w, ffn_w_down, loss_target, m_meta_tokens, m_attn_norm_w, m_ffn_norm_w, m_mix_w_in, m_conv_a_w, m_dn_conv_w, m_dn_a_log, m_dn_dt_bias, m_dn_norm_w, m_mix_w_out, m_swa_wq, m_swa_wk, m_swa_wv, m_swa_q_norm_w, m_swa_k_norm_w, m_swa_sinks, m_swa_wo, m_ffn_w_up, m_ffn_conv_w, m_ffn_w_down, v_meta_tokens, v_attn_norm_w, v_ffn_norm_w, v_mix_w_in, v_conv_a_w, v_dn_conv_w, v_dn_a_log, v_dn_dt_bias, v_dn_norm_w, v_mix_w_out, v_swa_wq, v_swa_wk, v_swa_wv, v_swa_q_norm_w, v_swa_k_norm_w, v_swa_sinks, v_swa_wo, v_ffn_w_up, v_ffn_conv_w, v_ffn_w_down):
    w = dict(meta_tokens=meta_tokens, attn_norm_w=attn_norm_w, ffn_norm_w=ffn_norm_w, mix_w_in=mix_w_in, conv_a_w=conv_a_w, dn_conv_w=dn_conv_w, dn_a_log=dn_a_log, dn_dt_bias=dn_dt_bias, dn_norm_w=dn_norm_w, mix_w_out=mix_w_out, swa_wq=swa_wq, swa_wk=swa_wk, swa_wv=swa_wv, swa_q_norm_w=swa_q_norm_w, swa_k_norm_w=swa_k_norm_w, swa_sinks=swa_sinks, swa_wo=swa_wo, ffn_w_up=ffn_w_up, ffn_conv_w=ffn_conv_w, ffn_w_down=ffn_w_down)
    mom = dict(meta_tokens=m_meta_tokens, attn_norm_w=m_attn_norm_w, ffn_norm_w=m_ffn_norm_w, mix_w_in=m_mix_w_in, conv_a_w=m_conv_a_w, dn_conv_w=m_dn_conv_w, dn_a_log=m_dn_a_log, dn_dt_bias=m_dn_dt_bias, dn_norm_w=m_dn_norm_w, mix_w_out=m_mix_w_out, swa_wq=m_swa_wq, swa_wk=m_swa_wk, swa_wv=m_swa_wv, swa_q_norm_w=m_swa_q_norm_w, swa_k_norm_w=m_swa_k_norm_w, swa_sinks=m_swa_sinks, swa_wo=m_swa_wo, ffn_w_up=m_ffn_w_up, ffn_conv_w=m_ffn_conv_w, ffn_w_down=m_ffn_w_down)
    var = dict(meta_tokens=v_meta_tokens, attn_norm_w=v_attn_norm_w, ffn_norm_w=v_ffn_norm_w, mix_w_in=v_mix_w_in, conv_a_w=v_conv_a_w, dn_conv_w=v_dn_conv_w, dn_a_log=v_dn_a_log, dn_dt_bias=v_dn_dt_bias, dn_norm_w=v_dn_norm_w, mix_w_out=v_mix_w_out, swa_wq=v_swa_wq, swa_wk=v_swa_wk, swa_wv=v_swa_wv, swa_q_norm_w=v_swa_q_norm_w, swa_k_norm_w=v_swa_k_norm_w, swa_sinks=v_swa_sinks, swa_wo=v_swa_wo, ffn_w_up=v_ffn_w_up, ffn_conv_w=v_ffn_conv_w, ffn_w_down=v_ffn_w_down)
    me = _my_index()

    transposed = ("mix_w_in", "ffn_w_up")
    view = lambda k, a: jnp.swapaxes(a, 1, 2) if k in transposed else a
    axis2d = {k: 0 if k in transposed else SHARD_AXIS[k] - 1 for k in BIG}
    shard16 = {k: view(k, w[k]).astype(MXU_DTYPE) for k in BIG}
    small_shard_shapes = [w[k].shape for k in SMALL_SHARDED]
    rows_in = shard16["mix_w_in"].shape[1]
    sent_in = jnp.pad(shard16["mix_w_in"][0], ((0, -rows_in % (2 * SUBLANES)), (0, 0)))
    got = _all_gather([sent_in, _pack([w[k] for k in SMALL_SHARDED])], name="gather_weights")
    whole = {"mix_w_in_t": _whole(got[0][:, :rows_in], 0)}
    for k, a in zip(SMALL_SHARDED, _unpack(got[1], small_shard_shapes, lead=1)):
        whole[k] = _whole(a, SHARD_AXIS[k])
    for k in SMALL:
        whole.setdefault(k, w[k])
    stages = {"in": [("mix_w_in", 0)], "l0": [("mix_w_out", 0), ("ffn_w_up", 0), ("ffn_w_down", 0)],
              "l1": [("swa_wq", 0), ("swa_wk", 0), ("swa_wv", 0), ("swa_wo", 0), ("ffn_w_up", 1), ("ffn_w_down", 1)]}
    pushed = {}
    follows = got[1]
    for stage in ("l0", "l1"):
        pushed[stage] = _push_start("gather", [shard16[k][l] for k, l in stages[stage]], follows, name=f"push_weights_{stage}")
        follows = pushed[stage]["zero"]
    early = _prepare_early(whole)
    early["anw"][0] = early["anw"][0] + follows[0, 0]

    def fetch(stage, after):
        got = _push_wait(pushed[stage], after, name=f"wait_weights_{stage}")
        full = {kl: _whole(a, axis2d[kl[0]]) for kl, a in zip(stages[stage], got)}
        if stage == "l0":
            return dict(w_out=full["mix_w_out", 0], w_up0_t=full["ffn_w_up", 0], w_down0=full["ffn_w_down", 0])
        wqkv = jnp.concatenate([full["swa_wq", 0], full["swa_wk", 0], full["swa_wv", 0]], 1)
        return dict(wqkv=wqkv, wo=full["swa_wo", 0], w_up1_t=full["ffn_w_up", 1], w_down1=full["ffn_w_down", 1])

    nq, nkv = SWA_HEADS * SWA_HEAD_DIM, SWA_KV_HEADS * SWA_HEAD_DIM
    grad_pushes = {}

    def push(stage, gd):
        if stage == "in":
            named = {("mix_w_in", 0): gd["w_in_t"]}
        elif stage == "l1":
            named = {("swa_wq", 0): gd["wqkv"][:, :nq], ("swa_wk", 0): gd["wqkv"][:, nq:nq + nkv],
                     ("swa_wv", 0): gd["wqkv"][:, nq + nkv:], ("swa_wo", 0): gd["wo"],
                     ("ffn_w_up", 1): gd["w_up_t"], ("ffn_w_down", 1): gd["w_down"]}
        else:
            named = {("mix_w_out", 0): gd["w_out"], ("ffn_w_up", 0): gd["w_up_t"], ("ffn_w_down", 0): gd["w_down"]}
        sent = [_by_owner(named[kl], axis2d[kl[0]]) for kl in stages[stage]]
        grad_pushes[stage] = _push_start("scatter", sent, jnp.zeros((SUBLANES, LANES), F32), name=f"push_grads_{stage}")
        return grad_pushes[stage]["zero"][0, 0]

    loss, g = _local_step(x[0], loss_target[0], early, fetch, push)
    grads = _small_named(g)

    results = {}

    def update(stage, follows):
        got = _push_wait(grad_pushes[stage], follows, name=f"wait_grads_{stage}")
        for (k, l), parts in zip(stages[stage], got):
            w3, m3, v3 = view(k, w[k]), view(k, mom[k]), view(k, var[k])
            results[k] = _adamw(parts.reshape((N_DEV,) + w3.shape[1:]), w3, m3, v3, l, results.get(k), name=f"adamw_{k}_{l}")
        return results[stages[stage][0][0]][0]

    follows = update("l0", update("l1", g["meta"]))

    small_shapes = [grads[k].shape for k in SMALL]
    (all_small,) = _all_gather([_pack([loss] + [grads[k].astype(F32) for k in SMALL])], name="gather_small_grads")
    loss_parts, *small_parts = _unpack(all_small, [loss.shape] + small_shapes, lead=1)
    mine = []
    for k, p in zip(SMALL, small_parts):
        ax = SHARD_AXIS[k]
        mine.append(p if ax is None else lax.dynamic_slice_in_dim(p, me * w[k].shape[ax], w[k].shape[ax], 1 + ax))
    zero = jnp.zeros(loss.shape, F32)
    packed = [_pack([z] + [d[k] for k in SMALL]) for z, d in ((zero, w), (zero, mom), (zero, var))]
    res = _adamw(_pack([loss_parts] + mine, lead=1), *[t[None] for t in packed], 0, name="adamw_small")
    shapes = [loss.shape] + [w[k].shape for k in SMALL]
    for t, which in zip(res, range(4)):
        for k, a in zip(["loss"] + SMALL, _unpack(t[0], shapes)):
            results.setdefault(k, [None] * 4)[which] = a
    update("in", jnp.maximum(follows[0, :1, :1], res[0][0, :1, :1]))

    outs = [[view(k, results[k][which]) for k in WEIGHTS] for which in range(4)]
    return (results["loss"][0][0, 0], g["x"][None], *outs[0], *outs[1], *outs[2], *outs[3])
```
